```python
import jax, jax.numpy as jnp
from jax import lax
import numpy as np

D_MODEL = 1024
BATCH = 8
SEQ = 8192
DEPTH = 2

CHUNK = 64
N_MEM = 256
CONV_WIDTH = 4
D_LRU = D_MODEL
LRU_BLOCKS = 8
LRU_BLOCK = D_LRU // LRU_BLOCKS
LRU_C = 8.0
D_SSD = 2 * D_MODEL
SSD_HEAD_DIM = 64
SSD_HEADS = D_SSD // SSD_HEAD_DIM
SSD_GROUPS = 4
SSD_HEADS_PER_GROUP = SSD_HEADS // SSD_GROUPS
SSD_STATE = 128
D_BC = SSD_GROUPS * SSD_STATE
D_XBC = D_SSD + 2 * D_BC
XA_HEADS = 4
XA_HEAD_DIM = 256
D_XA = XA_HEADS * XA_HEAD_DIM
N_BRANCH = 3
D_FF = ((8 * D_MODEL // 3 + 255) // 256) * 256
ALPHA = (2 * DEPTH) ** 0.25
BETA = (8 * DEPTH) ** -0.25
EPS = 1e-5

_SPLITS = (D_LRU, D_LRU, D_SSD, D_XBC, SSD_HEADS, D_XA, N_BRANCH * D_MODEL)
N_IN = sum(_SPLITS)
_OFFSETS = tuple(sum(_SPLITS[:i + 1]) for i in range(len(_SPLITS) - 1))

kernel_name = 'hybrid_rglru_ssd_memxattn_deepnorm'


def layer_norm(x, g, b):
    xf = x.astype(jnp.float32)
    mu = jnp.mean(xf, axis=-1, keepdims=True)
    var = jnp.mean(jnp.square(xf - mu), axis=-1, keepdims=True)
    return ((xf - mu) * lax.rsqrt(var + EPS) * g + b).astype(x.dtype)


def causal_depthwise_conv(x, w, b):
    c = x.shape[-1]
    y = lax.conv_general_dilated(
        x, w[:, None, :].astype(x.dtype), window_strides=(1,),
        padding=[(CONV_WIDTH - 1, 0)], dimension_numbers=('NWC', 'WIO', 'NWC'),
        feature_group_count=c)
    return y + b


def rg_lru(x, w_a, b_a, w_i, b_i, lam):
    xf = x.astype(jnp.float32)
    xb = xf.reshape(*xf.shape[:-1], LRU_BLOCKS, LRU_BLOCK)
    r = jax.nn.sigmoid(jnp.einsum('bsnk,nkj->bsnj', xb, w_a.astype(jnp.float32)).reshape(xf.shape) + b_a)
    i = jax.nn.sigmoid(jnp.einsum('bsnk,nkj->bsnj', xb, w_i.astype(jnp.float32)).reshape(xf.shape) + b_i)
    log_a = -LRU_C * r * jax.nn.softplus(-lam.astype(jnp.float32))
    a = jnp.exp(log_a)
    u = jnp.sqrt(-jnp.expm1(2.0 * log_a)) * (i * xf)

    def combine(lhs, rhs):
        a1, b1 = lhs
        a2, b2 = rhs
        return a1 * a2, a2 * b1 + b2

    _, h = lax.associative_scan(combine, (a, u), axis=1)
    return h


def ssd_chunked(xs, dt, a, bm, cm):
    bsz, s = xs.shape[:2]
    nc = s // CHUNK
    g, k, p, n = SSD_GROUPS, SSD_HEADS_PER_GROUP, SSD_HEAD_DIM, SSD_STATE
    x_c = (xs * dt[..., None]).reshape(bsz, nc, CHUNK, g, k, p)
    da = (dt * a).reshape(bsz, nc, CHUNK, g, k)
    b_c = bm.reshape(bsz, nc, CHUNK, g, n)
    c_c = cm.reshape(bsz, nc, CHUNK, g, n)
    cs = jnp.cumsum(da, axis=2)
    idx = jnp.arange(CHUNK)
    causal = (idx[:, None] >= idx[None, :])[:, :, None, None]
    seg = cs[:, :, :, None] - cs[:, :, None, :]
    decay = jnp.exp(jnp.where(causal, seg, -jnp.inf))
    cb = jnp.einsum('bclgn,bcsgn->bclsg', c_c, b_c)
    y_diag = jnp.einsum('bclsgk,bcsgkp->bclgkp', cb[..., None] * decay, x_c)
    decay_end = jnp.exp(cs[:, :, -1:] - cs)
    states = jnp.einsum('bclgn,bclgkp->bcgkpn', b_c, x_c * decay_end[..., None])
    chunk_decay = jnp.exp(cs[:, :, -1])

    def step(h, inp):
        st, dec = inp
        return h * dec[..., None, None] + st, h

    h0 = jnp.zeros((bsz, g, k, p, n), jnp.float32)
    _, prev = lax.scan(step, h0, (jnp.moveaxis(states, 1, 0), jnp.moveaxis(chunk_decay, 1, 0)))
    prev = jnp.moveaxis(prev, 0, 1)
    y_off = jnp.einsum('bclgn,bcgkpn->bclgkp', c_c, prev) * jnp.exp(cs)[..., None]
    return (y_diag + y_off).reshape(bsz, s, SSD_HEADS, p)


def ssd_branch(z, xbc, dt_raw, conv_w, conv_b, dt_bias, a_log, d_skip, norm_w):
    bsz, s = z.shape[:2]
    xbc = jax.nn.silu(causal_depthwise_conv(xbc, conv_w, conv_b)).astype(jnp.float32)
    xs, bm, cm = jnp.split(xbc, [D_SSD, D_SSD + D_BC], axis=-1)
    xs = xs.reshape(bsz, s, SSD_HEADS, SSD_HEAD_DIM)
    bm = bm.reshape(bsz, s, SSD_GROUPS, SSD_STATE)
    cm = cm.reshape(bsz, s, SSD_GROUPS, SSD_STATE)
    dt = jax.nn.softplus(dt_raw.astype(jnp.float32) + dt_bias)
    a = -jnp.exp(a_log.astype(jnp.float32))
    y = ssd_chunked(xs, dt, a, bm, cm) + d_skip[:, None] * xs
    y = y.reshape(bsz, s, D_SSD) * jax.nn.silu(z.astype(jnp.float32))
    yg = y.reshape(bsz, s, SSD_GROUPS, D_SSD // SSD_GROUPS)
    yg = yg * lax.rsqrt(jnp.mean(jnp.square(yg), axis=-1, keepdims=True) + EPS)
    return (yg.reshape(bsz, s, D_SSD) * norm_w).astype(z.dtype)


def memory_cross_attention(q, mem, w_kv):
    bsz, s = q.shape[:2]
    m = mem.shape[1]
    k, v = jnp.split(mem @ w_kv, 2, axis=-1)
    q = q.reshape(bsz, s, XA_HEADS, XA_HEAD_DIM)
    k = k.reshape(bsz, m, XA_HEADS, XA_HEAD_DIM)
    v = v.reshape(bsz, m, XA_HEADS, XA_HEAD_DIM)
    scores = jnp.einsum('bshd,bmhd->bhsm', q, k).astype(jnp.float32) * (XA_HEAD_DIM ** -0.5)
    probs = jax.nn.softmax(scores, axis=-1).astype(v.dtype)
    return jnp.einsum('bhsm,bmhd->bshd', probs, v).reshape(bsz, s, D_XA)


def hybrid_mixer(x, mem, w_in, b_gate, lru_conv_w, lru_conv_b, lru_w_a, lru_b_a, lru_w_i, lru_b_i,
                 lru_lambda, ssd_conv_w, ssd_conv_b, ssd_dt_bias, ssd_a_log, ssd_d, ssd_norm_w,
                 mem_w_kv, w_br_lru, w_br_ssd, w_br_xa, w_out):
    bsz, s = x.shape[:2]
    proj = x @ w_in
    lru_x, lru_gate, ssd_z, ssd_xbc, ssd_dt, xa_q, gate_logits = jnp.split(proj, _OFFSETS, axis=-1)
    h = rg_lru(causal_depthwise_conv(lru_x, lru_conv_w, lru_conv_b), lru_w_a, lru_b_a, lru_w_i, lru_b_i, lru_lambda)
    y_lru = (jax.nn.gelu(lru_gate.astype(jnp.float32)) * h).astype(x.dtype)
    y_ssd = ssd_branch(ssd_z, ssd_xbc, ssd_dt, ssd_conv_w, ssd_conv_b, ssd_dt_bias, ssd_a_log, ssd_d, ssd_norm_w)
    y_xa = memory_cross_attention(xa_q, mem, mem_w_kv)
    gates = jax.nn.sigmoid(gate_logits.reshape(bsz, s, N_BRANCH, D_MODEL) + b_gate)
    merged = (gates[:, :, 0] * (y_lru @ w_br_lru)
              + gates[:, :, 1] * (y_ssd @ w_br_ssd)
              + gates[:, :, 2] * (y_xa @ w_br_xa))
    return merged @ w_out


def swiglu(x, w_in, w_down):
    gate, up = jnp.split(x @ w_in, 2, axis=-1)
    return (jax.nn.silu(gate) * up) @ w_down


def _fwd_setup_inputs(seed: int = 0) -> dict:
    key = jax.random.key(seed)
    ks = jax.random.split(key, 32)

    def nrm(k, shape, scale):
        return jax.random.normal(k, shape, jnp.float32) * scale

    a0 = jax.random.uniform(ks[10], (DEPTH, D_LRU), jnp.float32, 0.9, 0.999)
    root = a0 ** (1.0 / LRU_C)
    lru_lambda = jnp.log(root) - jnp.log1p(-root)
    dt0 = jnp.exp(jax.random.uniform(ks[13], (DEPTH, SSD_HEADS), jnp.float32, np.log(0.001), np.log(0.1)))
    ssd_dt_bias = dt0 + jnp.log(-jnp.expm1(-dt0))
    ssd_a_log = jnp.log(jax.random.uniform(ks[14], (DEPTH, SSD_HEADS), jnp.float32, 1.0, 16.0))
    return {
        'x': nrm(ks[0], (BATCH, SEQ, D_MODEL), 1.0),
        'mem': nrm(ks[1], (BATCH, N_MEM, D_MODEL), 1.0),
        'w_in': nrm(ks[2], (DEPTH, D_MODEL, N_IN), D_MODEL ** -0.5),
        'b_gate': nrm(ks[3], (DEPTH, N_BRANCH, D_MODEL), 0.1),
        'lru_conv_w': nrm(ks[4], (DEPTH, CONV_WIDTH, D_LRU), CONV_WIDTH ** -0.5),
        'lru_conv_b': nrm(ks[5], (DEPTH, D_LRU), 0.02),
        'lru_w_a': nrm(ks[6], (DEPTH, LRU_BLOCKS, LRU_BLOCK, LRU_BLOCK), LRU_BLOCK ** -0.5),
        'lru_b_a': nrm(ks[7], (DEPTH, D_LRU), 0.02),
        'lru_w_i': nrm(ks[8], (DEPTH, LRU_BLOCKS, LRU_BLOCK, LRU_BLOCK), LRU_BLOCK ** -0.5),
        'lru_b_i': nrm(ks[9], (DEPTH, D_LRU), 0.02),
        'lru_lambda': lru_lambda,
        'ssd_conv_w': nrm(ks[11], (DEPTH, CONV_WIDTH, D_XBC), CONV_WIDTH ** -0.5),
        'ssd_conv_b': nrm(ks[12], (DEPTH, D_XBC), 0.02),
        'ssd_dt_bias': ssd_dt_bias,
        'ssd_a_log': ssd_a_log,
        'ssd_d': 1.0 + nrm(ks[15], (DEPTH, SSD_HEADS), 0.02),
        'ssd_norm_w': 1.0 + nrm(ks[16], (DEPTH, D_SSD), 0.02),
        'mem_w_kv': nrm(ks[17], (DEPTH, D_MODEL, 2 * D_XA), D_MODEL ** -0.5),
        'w_br_lru': nrm(ks[18], (DEPTH, D_LRU, D_MODEL), D_LRU ** -0.5),
        'w_br_ssd': nrm(ks[19], (DEPTH, D_SSD, D_MODEL), D_SSD ** -0.5),
        'w_br_xa': nrm(ks[20], (DEPTH, D_XA, D_MODEL), D_XA ** -0.5),
        'w_out': nrm(ks[21], (DEPTH, D_MODEL, D_MODEL), BETA * D_MODEL ** -0.5),
        'ln1_g': 1.0 + nrm(ks[22], (DEPTH, D_MODEL), 0.02),
        'ln1_b': nrm(ks[23], (DEPTH, D_MODEL), 0.02),
        'ffn_w_in': nrm(ks[24], (DEPTH, D_MODEL, 2 * D_FF), D_MODEL ** -0.5),
        'ffn_w_down': nrm(ks[25], (DEPTH, D_FF, D_MODEL), BETA * D_FF ** -0.5),
        'ln2_g': 1.0 + nrm(ks[26], (DEPTH, D_MODEL), 0.02),
        'ln2_b': nrm(ks[27], (DEPTH, D_MODEL), 0.02),
    }


def _fwd_reference(x, mem, w_in, b_gate, lru_conv_w, lru_conv_b, lru_w_a, lru_b_a, lru_w_i, lru_b_i,
              lru_lambda, ssd_conv_w, ssd_conv_b, ssd_dt_bias, ssd_a_log, ssd_d, ssd_norm_w,
              mem_w_kv, w_br_lru, w_br_ssd, w_br_xa, w_out, ln1_g, ln1_b, ffn_w_in, ffn_w_down,
              ln2_g, ln2_b):
    for l in range(DEPTH):
        mix = hybrid_mixer(x, mem, w_in[l], b_gate[l], lru_conv_w[l], lru_conv_b[l], lru_w_a[l], lru_b_a[l],
                           lru_w_i[l], lru_b_i[l], lru_lambda[l], ssd_conv_w[l], ssd_conv_b[l], ssd_dt_bias[l],
                           ssd_a_log[l], ssd_d[l], ssd_norm_w[l], mem_w_kv[l], w_br_lru[l], w_br_ssd[l],
                           w_br_xa[l], w_out[l])
        x = layer_norm(ALPHA * x + mix, ln1_g[l], ln1_b[l])
        x = layer_norm(ALPHA * x + swiglu(x, ffn_w_in[l], ffn_w_down[l]), ln2_g[l], ln2_b[l])
    return x


import jax as _jax
import jax.numpy as _jnp

TWIN_FORMAT = 'train_step'
FWD_PARAMS = ['x', 'mem', 'w_in', 'b_gate', 'lru_conv_w', 'lru_conv_b', 'lru_w_a', 'lru_b_a', 'lru_w_i', 'lru_b_i', 'lru_lambda', 'ssd_conv_w', 'ssd_conv_b', 'ssd_dt_bias', 'ssd_a_log', 'ssd_d', 'ssd_norm_w', 'mem_w_kv', 'w_br_lru', 'w_br_ssd', 'w_br_xa', 'w_out', 'ln1_g', 'ln1_b', 'ffn_w_in', 'ffn_w_down', 'ln2_g', 'ln2_b']
TWIN_WEIGHTS = ['w_in', 'b_gate', 'lru_conv_w', 'lru_conv_b', 'lru_w_a', 'lru_b_a', 'lru_w_i', 'lru_b_i', 'lru_lambda', 'ssd_conv_w', 'ssd_conv_b', 'ssd_dt_bias', 'ssd_a_log', 'ssd_d', 'ssd_norm_w', 'mem_w_kv', 'w_br_lru', 'w_br_ssd', 'w_br_xa', 'w_out', 'ln1_g', 'ln1_b', 'ffn_w_in', 'ffn_w_down', 'ln2_g', 'ln2_b']
TWIN_DIFF_INPUT = 'x'
TWIN_INPUTS = ['x', 'mem', 'w_in', 'b_gate', 'lru_conv_w', 'lru_conv_b', 'lru_w_a', 'lru_b_a', 'lru_w_i', 'lru_b_i', 'lru_lambda', 'ssd_conv_w', 'ssd_conv_b', 'ssd_dt_bias', 'ssd_a_log', 'ssd_d', 'ssd_norm_w', 'mem_w_kv', 'w_br_lru', 'w_br_ssd', 'w_br_xa', 'w_out', 'ln1_g', 'ln1_b', 'ffn_w_in', 'ffn_w_down', 'ln2_g', 'ln2_b', 'loss_target', 'm_w_in', 'm_b_gate', 'm_lru_conv_w', 'm_lru_conv_b', 'm_lru_w_a', 'm_lru_b_a', 'm_lru_w_i', 'm_lru_b_i', 'm_lru_lambda', 'm_ssd_conv_w', 'm_ssd_conv_b', 'm_ssd_dt_bias', 'm_ssd_a_log', 'm_ssd_d', 'm_ssd_norm_w', 'm_mem_w_kv', 'm_w_br_lru', 'm_w_br_ssd', 'm_w_br_xa', 'm_w_out', 'm_ln1_g', 'm_ln1_b', 'm_ffn_w_in', 'm_ffn_w_down', 'm_ln2_g', 'm_ln2_b', 'v_w_in', 'v_b_gate', 'v_lru_conv_w', 'v_lru_conv_b', 'v_lru_w_a', 'v_lru_b_a', 'v_lru_w_i', 'v_lru_b_i', 'v_lru_lambda', 'v_ssd_conv_w', 'v_ssd_conv_b', 'v_ssd_dt_bias', 'v_ssd_a_log', 'v_ssd_d', 'v_ssd_norm_w', 'v_mem_w_kv', 'v_w_br_lru', 'v_w_br_ssd', 'v_w_br_xa', 'v_w_out', 'v_ln1_g', 'v_ln1_b', 'v_ffn_w_in', 'v_ffn_w_down', 'v_ln2_g', 'v_ln2_b']
TWIN_OUTPUTS = ['loss', 'grad_x', 'grad_w_in', 'grad_b_gate', 'grad_lru_conv_w', 'grad_lru_conv_b', 'grad_lru_w_a', 'grad_lru_b_a', 'grad_lru_w_i', 'grad_lru_b_i', 'grad_lru_lambda', 'grad_ssd_conv_w', 'grad_ssd_conv_b', 'grad_ssd_dt_bias', 'grad_ssd_a_log', 'grad_ssd_d', 'grad_ssd_norm_w', 'grad_mem_w_kv', 'grad_w_br_lru', 'grad_w_br_ssd', 'grad_w_br_xa', 'grad_w_out', 'grad_ln1_g', 'grad_ln1_b', 'grad_ffn_w_in', 'grad_ffn_w_down', 'grad_ln2_g', 'grad_ln2_b', 'delta_w_in', 'delta_b_gate', 'delta_lru_conv_w', 'delta_lru_conv_b', 'delta_lru_w_a', 'delta_lru_b_a', 'delta_lru_w_i', 'delta_lru_b_i', 'delta_lru_lambda', 'delta_ssd_conv_w', 'delta_ssd_conv_b', 'delta_ssd_dt_bias', 'delta_ssd_a_log', 'delta_ssd_d', 'delta_ssd_norm_w', 'delta_mem_w_kv', 'delta_w_br_lru', 'delta_w_br_ssd', 'delta_w_br_xa', 'delta_w_out', 'delta_ln1_g', 'delta_ln1_b', 'delta_ffn_w_in', 'delta_ffn_w_down', 'delta_ln2_g', 'delta_ln2_b', 'new_m_w_in', 'new_m_b_gate', 'new_m_lru_conv_w', 'new_m_lru_conv_b', 'new_m_lru_w_a', 'new_m_lru_b_a', 'new_m_lru_w_i', 'new_m_lru_b_i', 'new_m_lru_lambda', 'new_m_ssd_conv_w', 'new_m_ssd_conv_b', 'new_m_ssd_dt_bias', 'new_m_ssd_a_log', 'new_m_ssd_d', 'new_m_ssd_norm_w', 'new_m_mem_w_kv', 'new_m_w_br_lru', 'new_m_w_br_ssd', 'new_m_w_br_xa', 'new_m_w_out', 'new_m_ln1_g', 'new_m_ln1_b', 'new_m_ffn_w_in', 'new_m_ffn_w_down', 'new_m_ln2_g', 'new_m_ln2_b', 'new_v_w_in', 'new_v_b_gate', 'new_v_lru_conv_w', 'new_v_lru_conv_b', 'new_v_lru_w_a', 'new_v_lru_b_a', 'new_v_lru_w_i', 'new_v_lru_b_i', 'new_v_lru_lambda', 'new_v_ssd_conv_w', 'new_v_ssd_conv_b', 'new_v_ssd_dt_bias', 'new_v_ssd_a_log', 'new_v_ssd_d', 'new_v_ssd_norm_w', 'new_v_mem_w_kv', 'new_v_w_br_lru', 'new_v_w_br_ssd', 'new_v_w_br_xa', 'new_v_w_out', 'new_v_ln1_g', 'new_v_ln1_b', 'new_v_ffn_w_in', 'new_v_ffn_w_down', 'new_v_ln2_g', 'new_v_ln2_b']
TWIN_LEAF_KINDS = {'loss': 'loss', 'grad_x': 'grad_x', 'grad_w_in': 'grad_w', 'grad_b_gate': 'grad_w', 'grad_lru_conv_w': 'grad_w', 'grad_lru_conv_b': 'grad_w', 'grad_lru_w_a': 'grad_w', 'grad_lru_b_a': 'grad_w', 'grad_lru_w_i': 'grad_w', 'grad_lru_b_i': 'grad_w', 'grad_lru_lambda': 'grad_w', 'grad_ssd_conv_w': 'grad_w', 'grad_ssd_conv_b': 'grad_w', 'grad_ssd_dt_bias': 'grad_w', 'grad_ssd_a_log': 'grad_w', 'grad_ssd_d': 'grad_w', 'grad_ssd_norm_w': 'grad_w', 'grad_mem_w_kv': 'grad_w', 'grad_w_br_lru': 'grad_w', 'grad_w_br_ssd': 'grad_w', 'grad_w_br_xa': 'grad_w', 'grad_w_out': 'grad_w', 'grad_ln1_g': 'grad_w', 'grad_ln1_b': 'grad_w', 'grad_ffn_w_in': 'grad_w', 'grad_ffn_w_down': 'grad_w', 'grad_ln2_g': 'grad_w', 'grad_ln2_b': 'grad_w', 'delta_w_in': 'delta_w', 'delta_b_gate': 'delta_w', 'delta_lru_conv_w': 'delta_w', 'delta_lru_conv_b': 'delta_w', 'delta_lru_w_a': 'delta_w', 'delta_lru_b_a': 'delta_w', 'delta_lru_w_i': 'delta_w', 'delta_lru_b_i': 'delta_w', 'delta_lru_lambda': 'delta_w', 'delta_ssd_conv_w': 'delta_w', 'delta_ssd_conv_b': 'delta_w', 'delta_ssd_dt_bias': 'delta_w', 'delta_ssd_a_log': 'delta_w', 'delta_ssd_d': 'delta_w', 'delta_ssd_norm_w': 'delta_w', 'delta_mem_w_kv': 'delta_w', 'delta_w_br_lru': 'delta_w', 'delta_w_br_ssd': 'delta_w', 'delta_w_br_xa': 'delta_w', 'delta_w_out': 'delta_w', 'delta_ln1_g': 'delta_w', 'delta_ln1_b': 'delta_w', 'delta_ffn_w_in': 'delta_w', 'delta_ffn_w_down': 'delta_w', 'delta_ln2_g': 'delta_w', 'delta_ln2_b': 'delta_w', 'new_m_w_in': 'new_m', 'new_m_b_gate': 'new_m', 'new_m_lru_conv_w': 'new_m', 'new_m_lru_conv_b': 'new_m', 'new_m_lru_w_a': 'new_m', 'new_m_lru_b_a': 'new_m', 'new_m_lru_w_i': 'new_m', 'new_m_lru_b_i': 'new_m', 'new_m_lru_lambda': 'new_m', 'new_m_ssd_conv_w': 'new_m', 'new_m_ssd_conv_b': 'new_m', 'new_m_ssd_dt_bias': 'new_m', 'new_m_ssd_a_log': 'new_m', 'new_m_ssd_d': 'new_m', 'new_m_ssd_norm_w': 'new_m', 'new_m_mem_w_kv': 'new_m', 'new_m_w_br_lru': 'new_m', 'new_m_w_br_ssd': 'new_m', 'new_m_w_br_xa': 'new_m', 'new_m_w_out': 'new_m', 'new_m_ln1_g': 'new_m', 'new_m_ln1_b': 'new_m', 'new_m_ffn_w_in': 'new_m', 'new_m_ffn_w_down': 'new_m', 'new_m_ln2_g': 'new_m', 'new_m_ln2_b': 'new_m', 'new_v_w_in': 'new_v', 'new_v_b_gate': 'new_v', 'new_v_lru_conv_w': 'new_v', 'new_v_lru_conv_b': 'new_v', 'new_v_lru_w_a': 'new_v', 'new_v_lru_b_a': 'new_v', 'new_v_lru_w_i': 'new_v', 'new_v_lru_b_i': 'new_v', 'new_v_lru_lambda': 'new_v', 'new_v_ssd_conv_w': 'new_v', 'new_v_ssd_conv_b': 'new_v', 'new_v_ssd_dt_bias': 'new_v', 'new_v_ssd_a_log': 'new_v', 'new_v_ssd_d': 'new_v', 'new_v_ssd_norm_w': 'new_v', 'new_v_mem_w_kv': 'new_v', 'new_v_w_br_lru': 'new_v', 'new_v_w_br_ssd': 'new_v', 'new_v_w_br_xa': 'new_v', 'new_v_w_out': 'new_v', 'new_v_ln1_g': 'new_v', 'new_v_ln1_b': 'new_v', 'new_v_ffn_w_in': 'new_v', 'new_v_ffn_w_down': 'new_v', 'new_v_ln2_g': 'new_v', 'new_v_ln2_b': 'new_v'}


def _forward(args):
    return _fwd_reference(*[args[k] for k in FWD_PARAMS])


def _output_shape():
    def fwd():
        inp = _fwd_setup_inputs(0)
        return _fwd_reference(*[inp[k] for k in FWD_PARAMS])
    out = _jax.eval_shape(fwd)
    return out.shape, out.dtype

N_MICROBATCH = 1
ADAM_LR = 0.001
ADAM_B1 = 0.9
ADAM_B2 = 0.999
ADAM_EPS = 1e-08
ADAM_WD = 0.01
ADAM_STEP = 10
PER_EXAMPLE_BATCH_AXIS = {'x': 0, 'mem': 0, 'loss_target': 0}
SHARED_INPUTS = []
_WEIGHT_DTYPES = {'w_in': _jnp.float32, 'b_gate': _jnp.float32, 'lru_conv_w': _jnp.float32, 'lru_conv_b': _jnp.float32, 'lru_w_a': _jnp.float32, 'lru_b_a': _jnp.float32, 'lru_w_i': _jnp.float32, 'lru_b_i': _jnp.float32, 'lru_lambda': _jnp.float32, 'ssd_conv_w': _jnp.float32, 'ssd_conv_b': _jnp.float32, 'ssd_dt_bias': _jnp.float32, 'ssd_a_log': _jnp.float32, 'ssd_d': _jnp.float32, 'ssd_norm_w': _jnp.float32, 'mem_w_kv': _jnp.float32, 'w_br_lru': _jnp.float32, 'w_br_ssd': _jnp.float32, 'w_br_xa': _jnp.float32, 'w_out': _jnp.float32, 'ln1_g': _jnp.float32, 'ln1_b': _jnp.float32, 'ffn_w_in': _jnp.float32, 'ffn_w_down': _jnp.float32, 'ln2_g': _jnp.float32, 'ln2_b': _jnp.float32}
MOMENT_SCALE = {'w_in': 2.545278e-02, 'b_gate': 1.372102e-02, 'lru_conv_w': 2.525628e-02, 'lru_conv_b': 3.654265e-01, 'lru_w_a': 9.857326e-03, 'lru_b_a': 7.447642e-03, 'lru_w_i': 1.809920e-02, 'lru_b_i': 8.872850e-03, 'lru_lambda': 1.309867e-02, 'ssd_conv_w': 3.107587e-02, 'ssd_conv_b': 4.892152e-02, 'ssd_dt_bias': 9.456714e-02, 'ssd_a_log': 2.405317e-01, 'ssd_d': 2.017140e-01, 'ssd_norm_w': 4.037457e-02, 'mem_w_kv': 5.604815e-03, 'w_br_lru': 2.770538e-02, 'w_br_ssd': 5.399146e-02, 'w_br_xa': 6.023742e-03, 'w_out': 1.204904e-01, 'ln1_g': 1.900791e+00, 'ln1_b': 9.890052e-01, 'ffn_w_in': 3.260586e-02, 'ffn_w_down': 1.065131e-01, 'ln2_g': 4.535706e+01, 'ln2_b': 2.626337e+00}


def _to_microbatches(a, axis):
    t = _jnp.moveaxis(a, axis, 0)
    t = t.reshape((N_MICROBATCH, t.shape[0] // N_MICROBATCH) + t.shape[1:])
    return _jnp.moveaxis(t, 1, axis + 1)


def setup_inputs(seed: int = 0) -> dict:
    inp = _fwd_setup_inputs(seed)
    key = _jax.random.fold_in(_jax.random.key(seed), 7919)
    shape, _ = _output_shape()
    out = dict(inp)
    out["loss_target"] = _jax.random.normal(_jax.random.fold_in(key, 0), shape, _jnp.float32)
    for i, name in enumerate(TWIN_WEIGHTS):
        w = inp[name].astype(_jnp.float32)
        if MOMENT_SCALE is None:
            s = _jnp.sqrt(_jnp.mean(_jnp.square(w)) + 1e-30)
        else:
            s = MOMENT_SCALE[name]
        km, kv = _jax.random.split(_jax.random.fold_in(key, i + 1))
        out[name] = w
        out["m_" + name] = s * _jax.random.normal(km, w.shape, _jnp.float32)
        out["v_" + name] = (s * s) * _jax.random.uniform(kv, w.shape, _jnp.float32, 0.5, 1.5)
    if N_MICROBATCH > 1:
        for name, axis in PER_EXAMPLE_BATCH_AXIS.items():
            out[name] = _to_microbatches(out[name], axis)
    return {'x': out['x'], 'mem': out['mem'], 'w_in': out['w_in'], 'b_gate': out['b_gate'], 'lru_conv_w': out['lru_conv_w'], 'lru_conv_b': out['lru_conv_b'], 'lru_w_a': out['lru_w_a'], 'lru_b_a': out['lru_b_a'], 'lru_w_i': out['lru_w_i'], 'lru_b_i': out['lru_b_i'], 'lru_lambda': out['lru_lambda'], 'ssd_conv_w': out['ssd_conv_w'], 'ssd_conv_b': out['ssd_conv_b'], 'ssd_dt_bias': out['ssd_dt_bias'], 'ssd_a_log': out['ssd_a_log'], 'ssd_d': out['ssd_d'], 'ssd_norm_w': out['ssd_norm_w'], 'mem_w_kv': out['mem_w_kv'], 'w_br_lru': out['w_br_lru'], 'w_br_ssd': out['w_br_ssd'], 'w_br_xa': out['w_br_xa'], 'w_out': out['w_out'], 'ln1_g': out['ln1_g'], 'ln1_b': out['ln1_b'], 'ffn_w_in': out['ffn_w_in'], 'ffn_w_down': out['ffn_w_down'], 'ln2_g': out['ln2_g'], 'ln2_b': out['ln2_b'], 'loss_target': out['loss_target'], 'm_w_in': out['m_w_in'], 'm_b_gate': out['m_b_gate'], 'm_lru_conv_w': out['m_lru_conv_w'], 'm_lru_conv_b': out['m_lru_conv_b'], 'm_lru_w_a': out['m_lru_w_a'], 'm_lru_b_a': out['m_lru_b_a'], 'm_lru_w_i': out['m_lru_w_i'], 'm_lru_b_i': out['m_lru_b_i'], 'm_lru_lambda': out['m_lru_lambda'], 'm_ssd_conv_w': out['m_ssd_conv_w'], 'm_ssd_conv_b': out['m_ssd_conv_b'], 'm_ssd_dt_bias': out['m_ssd_dt_bias'], 'm_ssd_a_log': out['m_ssd_a_log'], 'm_ssd_d': out['m_ssd_d'], 'm_ssd_norm_w': out['m_ssd_norm_w'], 'm_mem_w_kv': out['m_mem_w_kv'], 'm_w_br_lru': out['m_w_br_lru'], 'm_w_br_ssd': out['m_w_br_ssd'], 'm_w_br_xa': out['m_w_br_xa'], 'm_w_out': out['m_w_out'], 'm_ln1_g': out['m_ln1_g'], 'm_ln1_b': out['m_ln1_b'], 'm_ffn_w_in': out['m_ffn_w_in'], 'm_ffn_w_down': out['m_ffn_w_down'], 'm_ln2_g': out['m_ln2_g'], 'm_ln2_b': out['m_ln2_b'], 'v_w_in': out['v_w_in'], 'v_b_gate': out['v_b_gate'], 'v_lru_conv_w': out['v_lru_conv_w'], 'v_lru_conv_b': out['v_lru_conv_b'], 'v_lru_w_a': out['v_lru_w_a'], 'v_lru_b_a': out['v_lru_b_a'], 'v_lru_w_i': out['v_lru_w_i'], 'v_lru_b_i': out['v_lru_b_i'], 'v_lru_lambda': out['v_lru_lambda'], 'v_ssd_conv_w': out['v_ssd_conv_w'], 'v_ssd_conv_b': out['v_ssd_conv_b'], 'v_ssd_dt_bias': out['v_ssd_dt_bias'], 'v_ssd_a_log': out['v_ssd_a_log'], 'v_ssd_d': out['v_ssd_d'], 'v_ssd_norm_w': out['v_ssd_norm_w'], 'v_mem_w_kv': out['v_mem_w_kv'], 'v_w_br_lru': out['v_w_br_lru'], 'v_w_br_ssd': out['v_w_br_ssd'], 'v_w_br_xa': out['v_w_br_xa'], 'v_w_out': out['v_w_out'], 'v_ln1_g': out['v_ln1_g'], 'v_ln1_b': out['v_ln1_b'], 'v_ffn_w_in': out['v_ffn_w_in'], 'v_ffn_w_down': out['v_ffn_w_down'], 'v_ln2_g': out['v_ln2_g'], 'v_ln2_b': out['v_ln2_b']}


def _loss(weights, diff, rest, loss_target):
    with _jax.named_scope("forward"):
        args = {**rest, TWIN_DIFF_INPUT: diff, **{k: w.astype(_WEIGHT_DTYPES[k]) for k, w in weights.items()}}
        y = _forward(args)
    with _jax.named_scope("loss_head"):
        err = _jnp.square(y.astype(_jnp.float32) - loss_target)
        return 0.5 * _jnp.sum(_jnp.mean(err, axis=-1)) if err.ndim else 0.5 * err


def _adamw(w, g, m, v):
    m = ADAM_B1 * m + (1.0 - ADAM_B1) * g
    v = ADAM_B2 * v + (1.0 - ADAM_B2) * _jnp.square(g)
    m_hat = m / (1.0 - ADAM_B1 ** ADAM_STEP)
    v_hat = v / (1.0 - ADAM_B2 ** ADAM_STEP)
    delta = -ADAM_LR * (m_hat / (_jnp.sqrt(v_hat) + ADAM_EPS) + ADAM_WD * w)
    return delta, m, v


def reference(x, mem, w_in, b_gate, lru_conv_w, lru_conv_b, lru_w_a, lru_b_a, lru_w_i, lru_b_i, lru_lambda, ssd_conv_w, ssd_conv_b, ssd_dt_bias, ssd_a_log, ssd_d, ssd_norm_w, mem_w_kv, w_br_lru, w_br_ssd, w_br_xa, w_out, ln1_g, ln1_b, ffn_w_in, ffn_w_down, ln2_g, ln2_b, loss_target, m_w_in, m_b_gate, m_lru_conv_w, m_lru_conv_b, m_lru_w_a, m_lru_b_a, m_lru_w_i, m_lru_b_i, m_lru_lambda, m_ssd_conv_w, m_ssd_conv_b, m_ssd_dt_bias, m_ssd_a_log, m_ssd_d, m_ssd_norm_w, m_mem_w_kv, m_w_br_lru, m_w_br_ssd, m_w_br_xa, m_w_out, m_ln1_g, m_ln1_b, m_ffn_w_in, m_ffn_w_down, m_ln2_g, m_ln2_b, v_w_in, v_b_gate, v_lru_conv_w, v_lru_conv_b, v_lru_w_a, v_lru_b_a, v_lru_w_i, v_lru_b_i, v_lru_lambda, v_ssd_conv_w, v_ssd_conv_b, v_ssd_dt_bias, v_ssd_a_log, v_ssd_d, v_ssd_norm_w, v_mem_w_kv, v_w_br_lru, v_w_br_ssd, v_w_br_xa, v_w_out, v_ln1_g, v_ln1_b, v_ffn_w_in, v_ffn_w_down, v_ln2_g, v_ln2_b):
    given = dict(x=x, mem=mem, w_in=w_in, b_gate=b_gate, lru_conv_w=lru_conv_w, lru_conv_b=lru_conv_b, lru_w_a=lru_w_a, lru_b_a=lru_b_a, lru_w_i=lru_w_i, lru_b_i=lru_b_i, lru_lambda=lru_lambda, ssd_conv_w=ssd_conv_w, ssd_conv_b=ssd_conv_b, ssd_dt_bias=ssd_dt_bias, ssd_a_log=ssd_a_log, ssd_d=ssd_d, ssd_norm_w=ssd_norm_w, mem_w_kv=mem_w_kv, w_br_lru=w_br_lru, w_br_ssd=w_br_ssd, w_br_xa=w_br_xa, w_out=w_out, ln1_g=ln1_g, ln1_b=ln1_b, ffn_w_in=ffn_w_in, ffn_w_down=ffn_w_down, ln2_g=ln2_g, ln2_b=ln2_b, loss_target=loss_target, m_w_in=m_w_in, m_b_gate=m_b_gate, m_lru_conv_w=m_lru_conv_w, m_lru_conv_b=m_lru_conv_b, m_lru_w_a=m_lru_w_a, m_lru_b_a=m_lru_b_a, m_lru_w_i=m_lru_w_i, m_lru_b_i=m_lru_b_i, m_lru_lambda=m_lru_lambda, m_ssd_conv_w=m_ssd_conv_w, m_ssd_conv_b=m_ssd_conv_b, m_ssd_dt_bias=m_ssd_dt_bias, m_ssd_a_log=m_ssd_a_log, m_ssd_d=m_ssd_d, m_ssd_norm_w=m_ssd_norm_w, m_mem_w_kv=m_mem_w_kv, m_w_br_lru=m_w_br_lru, m_w_br_ssd=m_w_br_ssd, m_w_br_xa=m_w_br_xa, m_w_out=m_w_out, m_ln1_g=m_ln1_g, m_ln1_b=m_ln1_b, m_ffn_w_in=m_ffn_w_in, m_ffn_w_down=m_ffn_w_down, m_ln2_g=m_ln2_g, m_ln2_b=m_ln2_b, v_w_in=v_w_in, v_b_gate=v_b_gate, v_lru_conv_w=v_lru_conv_w, v_lru_conv_b=v_lru_conv_b, v_lru_w_a=v_lru_w_a, v_lru_b_a=v_lru_b_a, v_lru_w_i=v_lru_w_i, v_lru_b_i=v_lru_b_i, v_lru_lambda=v_lru_lambda, v_ssd_conv_w=v_ssd_conv_w, v_ssd_conv_b=v_ssd_conv_b, v_ssd_dt_bias=v_ssd_dt_bias, v_ssd_a_log=v_ssd_a_log, v_ssd_d=v_ssd_d, v_ssd_norm_w=v_ssd_norm_w, v_mem_w_kv=v_mem_w_kv, v_w_br_lru=v_w_br_lru, v_w_br_ssd=v_w_br_ssd, v_w_br_xa=v_w_br_xa, v_w_out=v_w_out, v_ln1_g=v_ln1_g, v_ln1_b=v_ln1_b, v_ffn_w_in=v_ffn_w_in, v_ffn_w_down=v_ffn_w_down, v_ln2_g=v_ln2_g, v_ln2_b=v_ln2_b)
    weights = {n: given[n] for n in TWIN_WEIGHTS}
    shared = {n: given[n] for n in SHARED_INPUTS}
    per_example = {n: given[n] for n in ['x', 'mem']}
    grad_fn = _jax.value_and_grad(_loss, argnums=(0, 1))

    def one_microbatch(ex, loss_target):
        ex = dict(ex)
        diff = ex.pop(TWIN_DIFF_INPUT)
        return grad_fn(weights, diff, {**shared, **ex}, loss_target)

    if N_MICROBATCH == 1:
        loss, (grad_w, grad_x) = one_microbatch(per_example, given["loss_target"])
    else:
        def body(carry, xs):
            loss_sum, grad_sum = carry
            l_k, (gw_k, gx_k) = one_microbatch(xs[0], xs[1])
            with _jax.named_scope("update"):
                return (loss_sum + l_k, _jax.tree.map(_jnp.add, grad_sum, gw_k)), gx_k

        init = (_jnp.zeros((), _jnp.float32), _jax.tree.map(_jnp.zeros_like, weights))
        (loss, grad_w), grad_x = _jax.lax.scan(body, init, (per_example, given["loss_target"]))
    with _jax.named_scope("update"):
        delta_w, new_m, new_v = {}, {}, {}
        for n in TWIN_WEIGHTS:
            delta_w[n], new_m[n], new_v[n] = _adamw(weights[n], grad_w[n], given["m_" + n], given["v_" + n])
    return (loss, grad_x, *[grad_w[n] for n in TWIN_WEIGHTS], *[delta_w[n] for n in TWIN_WEIGHTS],
            *[new_m[n] for n in TWIN_WEIGHTS], *[new_v[n] for n in TWIN_WEIGHTS])
```

```python
import math

import jax
import jax.numpy as jnp
from jax import lax
from jax.experimental import pallas as pl
from jax.experimental.pallas import tpu as pltpu

F32 = jnp.float32
BF16 = jnp.bfloat16

D = 1024
DEPTH = 2
N_DEV = 8
CHUNK = 64
LRU_BLOCKS = 8
LRU_BLOCK = 128
LRU_C = 8.0
D_SSD = 2 * D
SSD_HEADS = 32
SSD_GROUPS = 4
GROUP_W = D_SSD // SSD_GROUPS
SSD_STATE = 128
D_XBC = D_SSD + 2 * SSD_GROUPS * SSD_STATE
XA_HEADS = 4
XA_HEAD_DIM = 256
D_FF = 2816
ALPHA = (2 * DEPTH) ** 0.25
EPS = 1e-5
N_IN = 11296

S_Z, S_XBC, S_DT, W_SSD = 0, 2048, 5120, 5632
DT_REAL = 32

ADAM_LR, ADAM_B1, ADAM_B2, ADAM_EPS, ADAM_WD, ADAM_STEP = 0.001, 0.9, 0.999, 1e-08, 0.01, 10

VMEM_LIMIT = 56 * 1024 * 1024
MESH = pl.DeviceIdType.MESH
ANY = pl.BlockSpec(memory_space=pl.ANY)


def _cp(*sem):
    return pltpu.CompilerParams(dimension_semantics=sem, vmem_limit_bytes=VMEM_LIMIT)


def _blk(n, target):
    if n % 128:
        return n
    best = 128
    for b in range(128, min(n, target) + 1, 128):
        if n % b == 0:
            best = b
    return best


def _iota(shape, dim):
    return lax.broadcasted_iota(jnp.int32, shape, dim)


def _sigmoid(x):
    return 1.0 / (1.0 + jnp.exp(-x))


def _log1p(e):
    u = 1.0 + e
    return jnp.where(u == 1.0, e, jnp.log(u) * (e / (u - 1.0)))


def _softplus(x):
    return jnp.maximum(x, 0.0) + _log1p(jnp.exp(-jnp.abs(x)))


def _expm1(x):
    u = jnp.exp(x)
    um = u - 1.0
    return jnp.where(um == 0.0, x, jnp.where(um == -1.0, -1.0, um * (x / jnp.log(u))))


_G0 = math.sqrt(2.0 / math.pi)
_G1 = 0.044715


def _gelu_and_grad(x):
    t = jnp.tanh(_G0 * (x + _G1 * x * x * x))
    g = 0.5 * x * (1.0 + t)
    dg = 0.5 * (1.0 + t) + 0.5 * x * (1.0 - t * t) * (_G0 * (1.0 + 3.0 * _G1 * x * x))
    return g, dg


_NN = (((1,), (0,)), ((), ()))
_NT = (((1,), (1,)), ((), ()))
_TN = (((0,), (0,)), ((), ()))


def _dot(a, b, dims=_NN):
    return lax.dot_general(a.astype(BF16), b.astype(BF16), dims, preferred_element_type=F32)


def _dot_hi(a, b, dims=_NN):
    return lax.dot_general(a, b, dims, precision=lax.Precision.HIGHEST, preferred_element_type=F32)


def _split(v):
    hi = v.astype(BF16)
    return hi, (v - hi.astype(F32)).astype(BF16)


def _dot01(v, e, dims=_NN):
    hi, lo = _split(v)
    return (lax.dot_general(hi, e, dims, preferred_element_type=F32)
            + lax.dot_general(lo, e, dims, preferred_element_type=F32))


def _conv_taps(xe, n):
    return [xe[8:8 + n] if j == 3 else pltpu.roll(xe, 3 - j, 0)[8:8 + n] for j in range(4)]


def _conv_fwd(taps, cw, cb):
    return cb + cw[0:1] * taps[0] + cw[1:2] * taps[1] + cw[2:3] * taps[2] + cw[3:4] * taps[3]


def _conv_bwd(dc, dnext, taps, cw, n):
    ext = jnp.concatenate([dc, dnext], axis=0)
    dx = cw[3:4] * dc
    for j in range(3):
        dx = dx + cw[j:j + 1] * pltpu.roll(ext, n + 8 - (3 - j), 0)[0:n]
    dcw = jnp.concatenate([jnp.sum(dc * taps[j], axis=0, keepdims=True) for j in range(4)], axis=0)
    return dx, dcw, jnp.sum(dc, axis=0, keepdims=True)


def _mm(a, b, *, ta=False, tb=False, out_dtype=F32, add=None, add_scale=1.0, name, tm=1024, tn=512, tk=1024):
    if ta:
        k_dim, m_dim = a.shape
    else:
        m_dim, k_dim = a.shape
    if tb:
        n_dim, k2 = b.shape
    else:
        k2, n_dim = b.shape
    assert k_dim == k2, (a.shape, b.shape, ta, tb)
    tm, tn, tk = _blk(m_dim, tm), _blk(n_dim, tn), _blk(k_dim, tk)
    nk = k_dim // tk
    a_spec = pl.BlockSpec((tk, tm), lambda i, j, k: (k, i)) if ta else pl.BlockSpec((tm, tk), lambda i, j, k: (i, k))
    b_spec = pl.BlockSpec((tn, tk), lambda i, j, k: (j, k)) if tb else pl.BlockSpec((tk, tn), lambda i, j, k: (k, j))
    o_spec = pl.BlockSpec((tm, tn), lambda i, j, k: (i, j))
    dims = (((0 if ta else 1,), (1 if tb else 0,)), ((), ()))
    has_add = add is not None

    def body(*refs):
        if has_add:
            a_ref, b_ref, add_ref, o_ref, acc_ref = refs
        else:
            a_ref, b_ref, o_ref, acc_ref = refs
        k = pl.program_id(2)

        @pl.when(k == 0)
        def _():
            acc_ref[...] = jnp.zeros_like(acc_ref)

        acc_ref[...] += lax.dot_general(a_ref[...].astype(BF16), b_ref[...].astype(BF16), dims,
                                        preferred_element_type=F32)

        @pl.when(k == nk - 1)
        def _():
            r = acc_ref[...]
            if has_add:
                r = r + add_scale * add_ref[...]
            o_ref[...] = r.astype(out_dtype)

    in_specs = [a_spec, b_spec] + ([o_spec] if has_add else [])
    args = (a, b) + ((add,) if has_add else ())
    return pl.pallas_call(
        body, name=name, grid=(m_dim // tm, n_dim // tn, nk),
        in_specs=in_specs, out_specs=o_spec,
        out_shape=jax.ShapeDtypeStruct((m_dim, n_dim), out_dtype),
        scratch_shapes=[pltpu.VMEM((tm, tn), F32)],
        compiler_params=_cp("parallel", "parallel", "arbitrary"),
    )(*args)


def _ln_fwd(x, f, g, b, name, tb=512):
    t = x.shape[0]
    tb = min(tb, t)

    def body(x_ref, f_ref, g_ref, b_ref, o_ref):
        u = ALPHA * x_ref[...] + f_ref[...]
        mu = jnp.mean(u, axis=-1, keepdims=True)
        d = u - mu
        var = jnp.mean(d * d, axis=-1, keepdims=True)
        o_ref[...] = d * lax.rsqrt(var + EPS) * g_ref[...] + b_ref[...]

    row = pl.BlockSpec((tb, D), lambda i: (i, 0))
    par = pl.BlockSpec((1, D), lambda i: (0, 0))
    return pl.pallas_call(
        body, name=name, grid=(t // tb,), in_specs=[row, row, par, par], out_specs=row,
        out_shape=jax.ShapeDtypeStruct((t, D), F32), compiler_params=_cp("parallel"),
    )(x, f, g, b)


def _ln_bwd(x, f, dy, g, name, tb=512):
    t = x.shape[0]
    tb = min(tb, t)

    def body(x_ref, f_ref, dy_ref, g_ref, du_ref, dg_ref, db_ref):
        @pl.when(pl.program_id(0) == 0)
        def _():
            dg_ref[...] = jnp.zeros_like(dg_ref)
            db_ref[...] = jnp.zeros_like(db_ref)

        u = ALPHA * x_ref[...] + f_ref[...]
        mu = jnp.mean(u, axis=-1, keepdims=True)
        d = u - mu
        var = jnp.mean(d * d, axis=-1, keepdims=True)
        rstd = lax.rsqrt(var + EPS)
        xhat = d * rstd
        dy = dy_ref[...]
        dxh = dy * g_ref[...]
        m1 = jnp.mean(dxh, axis=-1, keepdims=True)
        m2 = jnp.mean(dxh * xhat, axis=-1, keepdims=True)
        du_ref[...] = rstd * (dxh - m1 - xhat * m2)
        dg_ref[...] += jnp.sum(dy * xhat, axis=0, keepdims=True)
        db_ref[...] += jnp.sum(dy, axis=0, keepdims=True)

    row = pl.BlockSpec((tb, D), lambda i: (i, 0))
    par = pl.BlockSpec((1, D), lambda i: (0, 0))
    return pl.pallas_call(
        body, name=name, grid=(t // tb,), in_specs=[row, row, row, par], out_specs=[row, par, par],
        out_shape=[jax.ShapeDtypeStruct((t, D), F32), jax.ShapeDtypeStruct((1, D), F32),
                   jax.ShapeDtypeStruct((1, D), F32)],
        compiler_params=_cp("arbitrary"),
    )(x, f, dy, g)


def _swiglu_fwd(gu, name, tb=512):
    t = gu.shape[0]
    tb = min(tb, t)

    def body(g_ref, u_ref, o_ref):
        g = g_ref[...]
        o_ref[...] = (g * _sigmoid(g) * u_ref[...]).astype(BF16)

    return pl.pallas_call(
        body, name=name, grid=(t // tb,),
        in_specs=[pl.BlockSpec((tb, D_FF), lambda i: (i, 0)), pl.BlockSpec((tb, D_FF), lambda i: (i, 1))],
        out_specs=pl.BlockSpec((tb, D_FF), lambda i: (i, 0)),
        out_shape=jax.ShapeDtypeStruct((t, D_FF), BF16), compiler_params=_cp("parallel"),
    )(gu, gu)


def _swiglu_bwd(gu, dact, name, tb=512):
    t = gu.shape[0]
    tb = min(tb, t)

    def body(g_ref, u_ref, da_ref, o_ref):
        g = g_ref[...]
        s = _sigmoid(g)
        da = da_ref[...]
        o_ref[:, :D_FF] = (da * u_ref[...] * (s * (1.0 + g * (1.0 - s)))).astype(BF16)
        o_ref[:, D_FF:] = (da * g * s).astype(BF16)

    return pl.pallas_call(
        body, name=name, grid=(t // tb,),
        in_specs=[pl.BlockSpec((tb, D_FF), lambda i: (i, 0)), pl.BlockSpec((tb, D_FF), lambda i: (i, 1)),
                  pl.BlockSpec((tb, D_FF), lambda i: (i, 0))],
        out_specs=pl.BlockSpec((tb, 2 * D_FF), lambda i: (i, 0)),
        out_shape=jax.ShapeDtypeStruct((t, 2 * D_FF), BF16),
        compiler_params=_cp("parallel"),
    )(gu, gu, dact)


def _merge_fwd(pgl, bg, b1, b2, b3, name, tb=512):
    t = pgl.shape[0]
    tb = min(tb, t)

    def body(gl_ref, bg_ref, b1_ref, b2_ref, b3_ref, o_ref):
        acc = None
        for j, b_ref in enumerate((b1_ref, b2_ref, b3_ref)):
            sl = slice(j * D, (j + 1) * D)
            term = _sigmoid(gl_ref[:, sl] + bg_ref[:, sl]) * b_ref[...]
            acc = term if acc is None else acc + term
        o_ref[...] = acc.astype(BF16)

    row = pl.BlockSpec((tb, D), lambda i: (i, 0))
    return pl.pallas_call(
        body, name=name, grid=(t // tb,),
        in_specs=[pl.BlockSpec((tb, 3 * D), lambda i: (i, 0)), pl.BlockSpec((1, 3 * D), lambda i: (0, 0)), row, row, row],
        out_specs=row, out_shape=jax.ShapeDtypeStruct((t, D), BF16), compiler_params=_cp("parallel"),
    )(pgl, bg, b1, b2, b3)


def _merge_bwd(pgl, bg, b1, b2, b3, dm, name, tb=512):
    t = pgl.shape[0]
    tb = min(tb, t)

    def body(gl_ref, bg_ref, b1_ref, b2_ref, b3_ref, dm_ref, dgl_ref, d1_ref, d2_ref, d3_ref, dbg_ref):
        @pl.when(pl.program_id(0) == 0)
        def _():
            dbg_ref[...] = jnp.zeros_like(dbg_ref)

        dm_v = dm_ref[...]
        for j, (b_ref, d_ref) in enumerate(((b1_ref, d1_ref), (b2_ref, d2_ref), (b3_ref, d3_ref))):
            sl = slice(j * D, (j + 1) * D)
            gate = _sigmoid(gl_ref[:, sl] + bg_ref[:, sl])
            d_ref[...] = (dm_v * gate).astype(BF16)
            dgl = dm_v * b_ref[...] * (gate * (1.0 - gate))
            dgl_ref[:, sl] = dgl.astype(BF16)
            dbg_ref[:, sl] += jnp.sum(dgl, axis=0, keepdims=True)

    row = pl.BlockSpec((tb, D), lambda i: (i, 0))
    wide = pl.BlockSpec((tb, 3 * D), lambda i: (i, 0))
    par = pl.BlockSpec((1, 3 * D), lambda i: (0, 0))
    return pl.pallas_call(
        body, name=name, grid=(t // tb,),
        in_specs=[wide, par, row, row, row, row], out_specs=[wide, row, row, row, par],
        out_shape=[jax.ShapeDtypeStruct((t, 3 * D), BF16)] + [jax.ShapeDtypeStruct((t, D), BF16)] * 3
                  + [jax.ShapeDtypeStruct((1, 3 * D), F32)],
        compiler_params=_cp("arbitrary"),
    )(pgl, bg, b1, b2, b3, dm)


def _xa_probs(q, kv_ref, hd):
    sl = slice(hd * XA_HEAD_DIM, (hd + 1) * XA_HEAD_DIM)
    k = kv_ref[:, sl]
    v = kv_ref[:, D + hd * XA_HEAD_DIM:D + (hd + 1) * XA_HEAD_DIM]
    s = _dot(q[:, sl], k, _NT) * (XA_HEAD_DIM ** -0.5)
    e = jnp.exp(s - jnp.max(s, axis=1, keepdims=True))
    return sl, k, v, e / jnp.sum(e, axis=1, keepdims=True)


def _xa_fwd(pq, kv, name, tb=512):
    t = pq.shape[0]
    tb = min(tb, t)

    def body(q_ref, kv_ref, o_ref):
        q = q_ref[...]
        for hd in range(XA_HEADS):
            sl, _, v, p = _xa_probs(q, kv_ref, hd)
            o_ref[:, sl] = _dot(p, v).astype(BF16)

    row = pl.BlockSpec((tb, D), lambda i: (i, 0))
    return pl.pallas_call(
        body, name=name, grid=(t // tb,),
        in_specs=[row, pl.BlockSpec(kv.shape, lambda i: (0, 0))], out_specs=row,
        out_shape=jax.ShapeDtypeStruct((t, D), BF16), compiler_params=_cp("parallel"),
    )(pq, kv)


def _xa_bwd(pq, kv, dy, name, tb=512):
    t = pq.shape[0]
    tb = min(tb, t)

    def body(q_ref, kv_ref, dy_ref, dq_ref, dkv_ref):
        @pl.when(pl.program_id(0) == 0)
        def _():
            dkv_ref[...] = jnp.zeros_like(dkv_ref)

        q = q_ref[...]
        for hd in range(XA_HEADS):
            sl, k, v, p = _xa_probs(q, kv_ref, hd)
            dyh = dy_ref[:, sl]
            vsl = slice(D + hd * XA_HEAD_DIM, D + (hd + 1) * XA_HEAD_DIM)
            dkv_ref[:, vsl] += _dot(p, dyh, _TN)
            dp = _dot(dyh, v, _NT)
            ds = p * (dp - jnp.sum(dp * p, axis=1, keepdims=True)) * (XA_HEAD_DIM ** -0.5)
            dq_ref[:, sl] = _dot(ds, k).astype(BF16)
            dkv_ref[:, sl] += _dot(ds, q[:, sl], _TN)

    row = pl.BlockSpec((tb, D), lambda i: (i, 0))
    kvs = pl.BlockSpec(kv.shape, lambda i: (0, 0))
    return pl.pallas_call(
        body, name=name, grid=(t // tb,), in_specs=[row, kvs, row], out_specs=[row, kvs],
        out_shape=[jax.ShapeDtypeStruct((t, D), BF16), jax.ShapeDtypeStruct(kv.shape, F32)],
        compiler_params=_cp("arbitrary"),
    )(pq, kv, dy)


def _scan_fwd(a, u):
    n = a.shape[0]
    row = _iota((n, 1), 0)
    d = 1
    while d < n:
        us = jnp.where(row >= d, pltpu.roll(u, d, 0), 0.0)
        u = a * us + u
        a = a * pltpu.roll(a, d, 0)
        d *= 2
    return u


def _scan_rev(b, u):
    n = b.shape[0]
    row = _iota((n, 1), 0)
    d = 1
    while d < n:
        us = jnp.where(row < n - d, pltpu.roll(u, n - d, 0), 0.0)
        u = b * us + u
        b = b * pltpu.roll(b, n - d, 0)
        d *= 2
    return u


def _lru_gates(xc, wa_ref, ba, wi_ref, bi, lam):
    za = jnp.concatenate([_dot(xc[:, n * 128:(n + 1) * 128], wa_ref[n]) for n in range(LRU_BLOCKS)], axis=1) + ba
    zi = jnp.concatenate([_dot(xc[:, n * 128:(n + 1) * 128], wi_ref[n]) for n in range(LRU_BLOCKS)], axis=1) + bi
    r = _sigmoid(za)
    ig = _sigmoid(zi)
    sp = _softplus(-lam)
    log_a = (-LRU_C) * r * sp
    a = jnp.exp(log_a)
    m = jnp.sqrt(-_expm1(2.0 * log_a))
    u = m * (ig * xc)
    return a, u, r, ig, m, sp


def _lru_conv(i_blk, x_ref, xp_ref, cw_ref, cb_ref, tb):
    halo = jnp.where(i_blk == 0, 0.0, xp_ref[...])
    taps = _conv_taps(jnp.concatenate([halo, x_ref[...]], axis=0), tb)
    cw = cw_ref[...]
    return _conv_fwd(taps, cw, cb_ref[...]), taps, cw


def _lru_fwd(p, cw, cb, wa, ba, wi, bi, lam, name, tb=256):
    t = p.shape[0]
    tb = min(tb, t)
    nb = t // tb
    r8 = tb // 8

    def body(x_ref, xp_ref, g_ref, cw_ref, cb_ref, wa_ref, ba_ref, wi_ref, bi_ref, lam_ref, y_ref, h_ref, hc_ref):
        i = pl.program_id(0)

        @pl.when(i == 0)
        def _():
            hc_ref[...] = jnp.zeros_like(hc_ref)

        xc, _, _ = _lru_conv(i, x_ref, xp_ref, cw_ref, cb_ref, tb)
        a, u, _, _, _, _ = _lru_gates(xc, wa_ref, ba_ref[...], wi_ref, bi_ref[...], lam_ref[...])
        row = _iota((tb, 1), 0)
        u = u + jnp.where(row == 0, a * hc_ref[...], 0.0)
        h = _scan_fwd(a, u)
        h_ref[...] = h
        hc_ref[...] = h[tb - 1:tb, :]
        gl, _ = _gelu_and_grad(g_ref[...])
        y_ref[...] = (gl * h).astype(BF16)

    par = pl.BlockSpec((1, D), lambda i: (0, 0))
    wsp = pl.BlockSpec((LRU_BLOCKS, LRU_BLOCK, LRU_BLOCK), lambda i: (0, 0, 0))
    row = pl.BlockSpec((tb, D), lambda i: (i, 0))
    return pl.pallas_call(
        body, name=name, grid=(nb,),
        in_specs=[row, pl.BlockSpec((8, D), lambda i: (jnp.maximum(i * r8 - 1, 0), 0)),
                  pl.BlockSpec((tb, D), lambda i: (i, 1)),
                  pl.BlockSpec((4, D), lambda i: (0, 0)), par, wsp, par, wsp, par, par],
        out_specs=[row, row],
        out_shape=[jax.ShapeDtypeStruct((t, D), BF16), jax.ShapeDtypeStruct((t, D), F32)],
        scratch_shapes=[pltpu.VMEM((1, D), F32)],
        compiler_params=_cp("arbitrary"),
    )(p, p, p, cw, cb, wa, ba, wi, bi, lam)


def _lru_bwd(p, h, dy, cw, cb, wa, ba, wi, bi, lam, name, tb=256):
    t = p.shape[0]
    tb = min(tb, t)
    nb = t // tb
    r8 = tb // 8

    def body(x_ref, xp_ref, g_ref, h_ref, hp_ref, dy_ref, cw_ref, cb_ref, wa_ref, ba_ref, wi_ref, bi_ref, lam_ref,
             dp_ref, dcw_ref, dcb_ref, dwa_ref, dba_ref, dwi_ref, dbi_ref, dlam_ref, carry_ref, dnext_ref):
        i = pl.program_id(0)
        blk = nb - 1 - i

        @pl.when(i == 0)
        def _():
            for r in (dcw_ref, dcb_ref, dwa_ref, dba_ref, dwi_ref, dbi_ref, dlam_ref, carry_ref, dnext_ref):
                r[...] = jnp.zeros_like(r)

        xc, taps, cw = _lru_conv(blk, x_ref, xp_ref, cw_ref, cb_ref, tb)
        lam = lam_ref[...]
        a, _, r, ig, m, sp = _lru_gates(xc, wa_ref, ba_ref[...], wi_ref, bi_ref[...], lam)
        gl, dgl = _gelu_and_grad(g_ref[...])
        h = h_ref[...]
        dy = dy_ref[...]
        dp_ref[:, D:] = (dy * h * dgl).astype(BF16)
        row = _iota((tb, 1), 0)
        dh = dy * gl + jnp.where(row == tb - 1, carry_ref[...], 0.0)
        b = jnp.where(row < tb - 1, pltpu.roll(a, tb - 1, 0), 0.0)
        gs = _scan_rev(b, dh)
        carry_ref[...] = a[0:1] * gs[0:1]
        h_last = jnp.where(blk == 0, 0.0, hp_ref[7:8, :])
        hprev = jnp.where(row == 0, h_last, pltpu.roll(h, 1, 0))
        da = gs * hprev
        dm = gs * ig * xc
        di = gs * m * xc
        dxc = gs * m * ig
        dlog = (0.5 * dm / m) * (-2.0 * a * a) + da * a
        dr = dlog * ((-LRU_C) * sp)
        dsp = jnp.sum(dlog * ((-LRU_C) * r), axis=0, keepdims=True)
        dlam_ref[...] += dsp * (-_sigmoid(-lam))
        dza = dr * r * (1.0 - r)
        dzi = di * ig * (1.0 - ig)
        dba_ref[...] += jnp.sum(dza, axis=0, keepdims=True)
        dbi_ref[...] += jnp.sum(dzi, axis=0, keepdims=True)
        parts = []
        for n in range(LRU_BLOCKS):
            sl = slice(n * 128, (n + 1) * 128)
            dwa_ref[n] += _dot(xc[:, sl], dza[:, sl], _TN)
            dwi_ref[n] += _dot(xc[:, sl], dzi[:, sl], _TN)
            parts.append(_dot(dza[:, sl], wa_ref[n], _NT) + _dot(dzi[:, sl], wi_ref[n], _NT))
        dxc = dxc + jnp.concatenate(parts, axis=1)
        dx, dcw, dcb = _conv_bwd(dxc, dnext_ref[...], taps, cw, tb)
        dp_ref[:, :D] = dx.astype(BF16)
        dcw_ref[...] += dcw
        dcb_ref[...] += dcb
        dnext_ref[...] = dxc[0:8]

    par = pl.BlockSpec((1, D), lambda i: (0, 0))
    wsp = pl.BlockSpec((LRU_BLOCKS, LRU_BLOCK, LRU_BLOCK), lambda i: (0, 0, 0))
    cws = pl.BlockSpec((4, D), lambda i: (0, 0))
    rev = lambda i: nb - 1 - i
    prev8 = lambda i: (jnp.maximum(rev(i) * r8 - 1, 0), 0)
    w_shape = jax.ShapeDtypeStruct((LRU_BLOCKS, LRU_BLOCK, LRU_BLOCK), F32)
    v_shape = jax.ShapeDtypeStruct((1, D), F32)
    return pl.pallas_call(
        body, name=name, grid=(nb,),
        in_specs=[pl.BlockSpec((tb, D), lambda i: (rev(i), 0)), pl.BlockSpec((8, D), prev8),
                  pl.BlockSpec((tb, D), lambda i: (rev(i), 1)),
                  pl.BlockSpec((tb, D), lambda i: (rev(i), 0)), pl.BlockSpec((8, D), prev8),
                  pl.BlockSpec((tb, D), lambda i: (rev(i), 0)),
                  cws, par, wsp, par, wsp, par, par],
        out_specs=[pl.BlockSpec((tb, 2 * D), lambda i: (rev(i), 0)), cws, par, wsp, par, wsp, par, par],
        out_shape=[jax.ShapeDtypeStruct((t, 2 * D), BF16), jax.ShapeDtypeStruct((4, D), F32), v_shape,
                   w_shape, v_shape, w_shape, v_shape, v_shape],
        scratch_shapes=[pltpu.VMEM((1, D), F32), pltpu.VMEM((8, D), F32)],
        compiler_params=_cp("arbitrary"),
    )(p, p, p, h, h, dy, cw, cb, wa, ba, wi, bi, lam)


def _ssd_consts():
    m0 = _iota((1, 128), 1) < 64
    e = (jnp.right_shift(_iota((SSD_HEADS, D_SSD), 1), 6) == _iota((SSD_HEADS, D_SSD), 0)).astype(BF16)
    tril = (_iota((CHUNK, CHUNK), 0) >= _iota((CHUNK, CHUNK), 1)).astype(F32)
    eye = (_iota((SSD_HEADS, SSD_HEADS), 0) == _iota((SSD_HEADS, SSD_HEADS), 1)).astype(F32)
    r2 = _iota((CHUNK, 128), 0)
    c2 = jnp.bitwise_and(_iota((CHUNK, 128), 1), 63)
    return dict(m0=m0, e=e, tril=tril, eye=eye, causal2=r2 >= c2, fold=(c2 == r2).astype(BF16))


def _ssd_pre(blk, p_ref, pp_ref, cw_ref, cb_ref, dtb_ref, alog_ref, dvec_ref, k):
    halo = jnp.where(blk == 0, 0.0, pp_ref[:, S_XBC:S_DT])
    taps = _conv_taps(jnp.concatenate([halo, p_ref[:, S_XBC:S_DT]], axis=0), CHUNK)
    cw = cw_ref[...]
    c = _conv_fwd(taps, cw, cb_ref[...])
    sg = _sigmoid(c)
    xbc = c * sg
    dtp = p_ref[:, S_DT:S_DT + DT_REAL] + dtb_ref[...]
    dt = _softplus(dtp)
    a = -jnp.exp(alog_ref[...])
    cs = _dot_hi(k["tril"], dt * a)
    cs_last = cs[CHUNK - 1:CHUNK]
    dend = jnp.exp(cs_last - cs)
    cdec = jnp.exp(cs_last)
    big = _dot01(jnp.concatenate([dt, jnp.exp(cs), dend], axis=0), k["e"])
    small = _dot01(jnp.concatenate([jnp.broadcast_to(cdec, (8, SSD_HEADS)),
                                    jnp.broadcast_to(dvec_ref[...], (8, SSD_HEADS))], axis=0), k["e"])
    cst2 = _dot_hi(k["eye"], jnp.concatenate([cs, cs], axis=0), _NT)
    return dict(taps=taps, cw=cw, c=c, sg=sg, xs=xbc[:, :D_SSD], bm=xbc[:, D_SSD:D_SSD + 512],
                cm=xbc[:, D_SSD + 512:], dtp=dtp, dt=dt, a=a, cs=cs, dend=dend, cdec=cdec,
                dtx=big[0:CHUNK], ecx=big[CHUNK:2 * CHUNK], dex=big[2 * CHUNK:3 * CHUNK],
                cdx=small[0:1], ddx=small[8:9], cst2=cst2)


def _pair_decay(p, cs, cst2, k):
    h0, h1 = 2 * p, 2 * p + 1
    colp = jnp.where(k["m0"], cs[:, h0:h0 + 1], cs[:, h1:h1 + 1])
    rowp = jnp.where(k["m0"], cst2[h0:h0 + 1, :], cst2[h1:h1 + 1, :])
    return jnp.where(k["causal2"], jnp.exp(colp - rowp), 0.0)


def _pair_stack(xp, k):
    return jnp.concatenate([jnp.where(k["m0"], xp, 0.0), jnp.where(k["m0"], 0.0, xp)], axis=0)


def _group_norm(yz, nw, with_stats=False):
    outs, stats = [], []
    for g in range(SSD_GROUPS):
        yzg = yz[:, g * GROUP_W:(g + 1) * GROUP_W]
        r = lax.rsqrt(jnp.mean(yzg * yzg, axis=1, keepdims=True) + EPS)
        outs.append(yzg * r)
        stats.append(r)
    y = jnp.concatenate(outs, axis=1) * nw
    return (y, stats) if with_stats else y


def _ssd_fwd(p, cw, cb, dtb, alog, dvec, nw, name):
    t = p.shape[0]
    nc = t // CHUNK

    def body(p_ref, pp_ref, cw_ref, cb_ref, dtb_ref, alog_ref, dvec_ref, nw_ref, y_ref, yraw_ref, hs_ref, h_scr):
        i = pl.program_id(0)

        @pl.when(i == 0)
        def _():
            h_scr[...] = jnp.zeros_like(h_scr)

        k = _ssd_consts()
        s = _ssd_pre(i, p_ref, pp_ref, cw_ref, cb_ref, dtb_ref, alog_ref, dvec_ref, k)
        xs, bm, cm = s["xs"], s["bm"], s["cm"]
        xdt = xs * s["dtx"]
        hprev = h_scr[...]
        hs_ref[0] = hprev
        ys, hn = [], []
        for g in range(SSD_GROUPS):
            gs = slice(g * GROUP_W, (g + 1) * GROUP_W)
            bg = bm[:, g * 128:(g + 1) * 128]
            cg = cm[:, g * 128:(g + 1) * 128]
            cbdup = _dot(cg, jnp.concatenate([bg, bg], axis=0), _NT)
            hp_g = hprev[:, gs]
            yd = []
            for q in range(4):
                pr = g * 4 + q
                mp = cbdup * _pair_decay(pr, s["cs"], s["cst2"], k)
                yd.append(_dot(mp, _pair_stack(xdt[:, pr * 128:(pr + 1) * 128], k)))
            ys.append(jnp.concatenate(yd, axis=1) + _dot(cg, hp_g) * s["ecx"][:, gs])
            hn.append(hp_g * s["cdx"][:, gs] + _dot(bg, xdt[:, gs] * s["dex"][:, gs], _TN))
        h_scr[...] = jnp.concatenate(hn, axis=1)
        yraw = jnp.concatenate(ys, axis=1) + s["ddx"] * xs
        yraw_ref[...] = yraw
        z = p_ref[:, S_Z:S_Z + D_SSD]
        y_ref[...] = _group_norm(yraw * (z * _sigmoid(z)), nw_ref[...]).astype(BF16)

    hv = pl.BlockSpec((1, DT_REAL), lambda i: (0, 0))
    return pl.pallas_call(
        body, name=name, grid=(nc,),
        in_specs=[pl.BlockSpec((CHUNK, W_SSD), lambda i: (i, 0)),
                  pl.BlockSpec((8, W_SSD), lambda i: (jnp.maximum(i * (CHUNK // 8) - 1, 0), 0)),
                  pl.BlockSpec((4, D_XBC), lambda i: (0, 0)), pl.BlockSpec((1, D_XBC), lambda i: (0, 0)),
                  hv, hv, hv, pl.BlockSpec((1, D_SSD), lambda i: (0, 0))],
        out_specs=[pl.BlockSpec((CHUNK, D_SSD), lambda i: (i, 0)), pl.BlockSpec((CHUNK, D_SSD), lambda i: (i, 0)),
                   pl.BlockSpec((1, SSD_STATE, D_SSD), lambda i: (i, 0, 0))],
        out_shape=[jax.ShapeDtypeStruct((t, D_SSD), BF16), jax.ShapeDtypeStruct((t, D_SSD), F32),
                   jax.ShapeDtypeStruct((nc, SSD_STATE, D_SSD), F32)],
        scratch_shapes=[pltpu.VMEM((SSD_STATE, D_SSD), F32)],
        compiler_params=_cp("arbitrary"),
    )(p, p, cw, cb, dtb, alog, dvec, nw)


def _ssd_bwd(p, yraw, hs, dy, cw, cb, dtb, alog, dvec, nw, name):
    t = p.shape[0]
    nc = t // CHUNK

    def body(p_ref, pp_ref, yraw_ref, hs_ref, dy_ref, cw_ref, cb_ref, dtb_ref, alog_ref, dvec_ref, nw_ref,
             dp_ref, dcw_ref, dcb_ref, ddtb_ref, dalog_ref, dd_ref, dnw_ref, dh_scr, dnext_scr):
        i = pl.program_id(0)
        blk = nc - 1 - i

        @pl.when(i == 0)
        def _():
            for r in (dcw_ref, dcb_ref, ddtb_ref, dalog_ref, dd_ref, dnw_ref, dh_scr, dnext_scr):
                r[...] = jnp.zeros_like(r)

        k = _ssd_consts()
        s = _ssd_pre(blk, p_ref, pp_ref, cw_ref, cb_ref, dtb_ref, alog_ref, dvec_ref, k)
        xs, bm, cm, cs, dt, a = s["xs"], s["bm"], s["cm"], s["cs"], s["dt"], s["a"]
        m0 = k["m0"]
        xdt = xs * s["dtx"]
        hprev = hs_ref[0]
        dh = dh_scr[...]

        nw_v = nw_ref[...]
        yraw = yraw_ref[...]
        z = p_ref[:, S_Z:S_Z + D_SSD]
        sz = _sigmoid(z)
        siluz = z * sz
        yz = yraw * siluz
        dyo = dy_ref[...]
        dyn = dyo * nw_v
        dyz_parts, dnw_parts = [], []
        for g in range(SSD_GROUPS):
            gs = slice(g * GROUP_W, (g + 1) * GROUP_W)
            yzg = yz[:, gs]
            r = lax.rsqrt(jnp.mean(yzg * yzg, axis=1, keepdims=True) + EPS)
            dnw_parts.append(jnp.sum(dyo[:, gs] * yzg * r, axis=0, keepdims=True))
            dyz_parts.append(r * dyn[:, gs] - yzg * (r * r * r) * jnp.mean(dyn[:, gs] * yzg, axis=1, keepdims=True))
        dnw_ref[...] += jnp.concatenate(dnw_parts, axis=1)
        dyz = jnp.concatenate(dyz_parts, axis=1)
        d_y = dyz * siluz
        dp_ref[:, S_Z:S_Z + D_SSD] = (dyz * yraw * (sz * (1.0 + z * (1.0 - sz)))).astype(BF16)
        dd_row = jnp.sum(d_y * xs, axis=0, keepdims=True)
        dxs = d_y * s["ddx"]

        lane_h = _iota((1, SSD_HEADS), 1)
        sub_h = _iota((SSD_HEADS, 1), 0)
        dcs = jnp.zeros((CHUNK, SSD_HEADS), F32)
        dcst2 = jnp.zeros((SSD_HEADS, 128), F32)
        dxdt_parts, db_parts, dc_parts, dhp_parts, yoff_parts, dend_parts, dcd_parts = [], [], [], [], [], [], []
        for g in range(SSD_GROUPS):
            gs = slice(g * GROUP_W, (g + 1) * GROUP_W)
            bg = bm[:, g * 128:(g + 1) * 128]
            cg = cm[:, g * 128:(g + 1) * 128]
            bdup = jnp.concatenate([bg, bg], axis=0)
            cbdup = _dot(cg, bdup, _NT)
            dcb2 = jnp.zeros((CHUNK, 128), F32)
            dxp_parts = []
            for q in range(4):
                pr = g * 4 + q
                h0, h1 = 2 * pr, 2 * pr + 1
                lp = _pair_decay(pr, cs, s["cst2"], k)
                mp = cbdup * lp
                xst = _pair_stack(xdt[:, pr * 128:(pr + 1) * 128], k)
                dyp = d_y[:, pr * 128:(pr + 1) * 128]
                dmp = _dot(dyp, xst, _NT)
                dxst = _dot(mp, dyp, _TN)
                dxp_parts.append(jnp.where(m0, dxst[:CHUNK], dxst[CHUNK:]))
                dcb2 = dcb2 + dmp * lp
                dlm = dmp * mp
                rs0 = jnp.sum(jnp.where(m0, dlm, 0.0), axis=1, keepdims=True)
                rs1 = jnp.sum(jnp.where(m0, 0.0, dlm), axis=1, keepdims=True)
                dcs = dcs + jnp.where(lane_h == h0, rs0, 0.0) + jnp.where(lane_h == h1, rs1, 0.0)
                colsum = jnp.sum(dlm, axis=0, keepdims=True)
                sel = ((sub_h == h0) & m0) | ((sub_h == h1) & jnp.logical_not(m0))
                dcst2 = dcst2 - jnp.where(sel, colsum, 0.0)
            dcg = _dot(dcb2, bdup)
            dbdup = _dot(dcb2, cg, _TN)
            dbg = dbdup[:CHUNK] + dbdup[CHUNK:]
            hp_g = hprev[:, gs]
            zoff = _dot(cg, hp_g)
            dzo = d_y[:, gs] * s["ecx"][:, gs]
            yoff_parts.append(dzo * zoff)
            dcg = dcg + _dot(dzo, hp_g, _NT)
            dh_g = dh[:, gs]
            dhp_parts.append(_dot(cg, dzo, _TN) + dh_g * s["cdx"][:, gs])
            dcd_parts.append(jnp.sum(dh_g * hp_g, axis=0, keepdims=True))
            wg = xdt[:, gs] * s["dex"][:, gs]
            dbg = dbg + _dot(wg, dh_g, _NT)
            dwg = _dot(bg, dh_g)
            dxdt_parts.append(jnp.concatenate(dxp_parts, axis=1) + dwg * s["dex"][:, gs])
            dend_parts.append(dwg * xdt[:, gs])
            db_parts.append(dbg)
            dc_parts.append(dcg)
        dh_scr[...] = jnp.concatenate(dhp_parts, axis=1)
        dxdt = jnp.concatenate(dxdt_parts, axis=1)
        sums = _dot01(jnp.concatenate([jnp.concatenate(yoff_parts, axis=1), jnp.concatenate(dend_parts, axis=1),
                                       dxdt * xs], axis=0), k["e"], _NT)
        s_dend = sums[CHUNK:2 * CHUNK] * s["dend"]
        rows8 = jnp.concatenate([jnp.broadcast_to(jnp.concatenate(dcd_parts, axis=1), (8, D_SSD)),
                                 jnp.broadcast_to(dd_row, (8, D_SSD))], axis=0)
        small = _dot01(rows8, k["e"], _NT)
        dd_ref[...] += small[8:9]
        dcs_last = small[0:1] * s["cdec"] + jnp.sum(s_dend, axis=0, keepdims=True)
        hi, lo = _split(dcst2)
        dcs = (dcs + sums[0:CHUNK] - s_dend
               + lax.dot_general(k["fold"], hi, _NT, preferred_element_type=F32)
               + lax.dot_general(k["fold"], lo, _NT, preferred_element_type=F32)
               + jnp.where(_iota((CHUNK, 1), 0) == CHUNK - 1, dcs_last, 0.0))
        dda = _dot_hi(k["tril"], dcs, _TN)
        ddt = dda * a + sums[2 * CHUNK:3 * CHUNK]
        dalog_ref[...] += jnp.sum(dda * dt, axis=0, keepdims=True) * a
        dxs = dxs + dxdt * s["dtx"]
        draw = ddt * _sigmoid(s["dtp"])
        ddtb_ref[...] += jnp.sum(draw, axis=0, keepdims=True)
        dp_ref[:, S_DT:] = jnp.zeros((CHUNK, W_SSD - S_DT), BF16)
        dp_ref[:, S_DT:S_DT + DT_REAL] = draw.astype(BF16)
        dxbc = jnp.concatenate([dxs] + db_parts + dc_parts, axis=1)
        sg, c = s["sg"], s["c"]
        dc = dxbc * (sg * (1.0 + c * (1.0 - sg)))
        dx, dcw, dcb = _conv_bwd(dc, dnext_scr[...], s["taps"], s["cw"], CHUNK)
        dp_ref[:, S_XBC:S_DT] = dx.astype(BF16)
        dcw_ref[...] += dcw
        dcb_ref[...] += dcb
        dnext_scr[...] = dc[0:8]

    rev = lambda i: nc - 1 - i
    hv = pl.BlockSpec((1, DT_REAL), lambda i: (0, 0))
    cws = pl.BlockSpec((4, D_XBC), lambda i: (0, 0))
    cbs = pl.BlockSpec((1, D_XBC), lambda i: (0, 0))
    nws = pl.BlockSpec((1, D_SSD), lambda i: (0, 0))
    wide = pl.BlockSpec((CHUNK, D_SSD), lambda i: (rev(i), 0))
    hshape = jax.ShapeDtypeStruct((1, DT_REAL), F32)
    return pl.pallas_call(
        body, name=name, grid=(nc,),
        in_specs=[pl.BlockSpec((CHUNK, W_SSD), lambda i: (rev(i), 0)),
                  pl.BlockSpec((8, W_SSD), lambda i: (jnp.maximum(rev(i) * (CHUNK // 8) - 1, 0), 0)),
                  wide, pl.BlockSpec((1, SSD_STATE, D_SSD), lambda i: (rev(i), 0, 0)), wide,
                  cws, cbs, hv, hv, hv, nws],
        out_specs=[pl.BlockSpec((CHUNK, W_SSD), lambda i: (rev(i), 0)), cws, cbs, hv, hv, hv, nws],
        out_shape=[jax.ShapeDtypeStruct((t, W_SSD), BF16), jax.ShapeDtypeStruct((4, D_XBC), F32),
                   jax.ShapeDtypeStruct((1, D_XBC), F32), hshape, hshape, hshape,
                   jax.ShapeDtypeStruct((1, D_SSD), F32)],
        scratch_shapes=[pltpu.VMEM((SSD_STATE, D_SSD), F32), pltpu.VMEM((8, D_XBC), F32)],
        compiler_params=_cp("arbitrary"),
    )(p, p, yraw, hs, dy, cw, cb, dtb, alog, dvec, nw)


def _loss_head(y, target, name, tb=512):
    t = y.shape[0]
    tb = min(tb, t)

    def body(y_ref, t_ref, dy_ref, l_ref):
        @pl.when(pl.program_id(0) == 0)
        def _():
            l_ref[...] = jnp.zeros_like(l_ref)

        e = y_ref[...] - t_ref[...]
        dy_ref[...] = e * (1.0 / D)
        l_ref[...] += jnp.sum(jnp.sum(e * e, axis=1, keepdims=True), axis=0, keepdims=True) * (0.5 / D)

    row = pl.BlockSpec((tb, D), lambda i: (i, 0))
    return pl.pallas_call(
        body, name=name, grid=(t // tb,), in_specs=[row, row],
        out_specs=[row, pl.BlockSpec((8, 128), lambda i: (0, 0))],
        out_shape=[jax.ShapeDtypeStruct((t, D), F32), jax.ShapeDtypeStruct((8, 128), F32)],
        compiler_params=_cp("arbitrary"),
    )(y, target)


def _adamw(slots, w, m, v, name, tb):
    ns, r, c = slots.shape
    assert r % tb == 0, (r, tb)

    def body(s_ref, w_ref, m_ref, v_ref, g_ref, d_ref, m2_ref, v2_ref):
        g = s_ref[0].astype(F32)
        for j in range(1, ns):
            g = g + s_ref[j].astype(F32)
        m2 = ADAM_B1 * m_ref[...] + (1.0 - ADAM_B1) * g
        v2 = ADAM_B2 * v_ref[...] + (1.0 - ADAM_B2) * (g * g)
        m_hat = m2 / (1.0 - ADAM_B1 ** ADAM_STEP)
        v_hat = v2 / (1.0 - ADAM_B2 ** ADAM_STEP)
        g_ref[...] = g
        d_ref[...] = -ADAM_LR * (m_hat / (jnp.sqrt(v_hat) + ADAM_EPS) + ADAM_WD * w_ref[...])
        m2_ref[...] = m2
        v2_ref[...] = v2

    row = pl.BlockSpec((tb, c), lambda i: (i, 0))
    shp = jax.ShapeDtypeStruct((r, c), F32)
    return pl.pallas_call(
        body, name=name, grid=(r // tb,),
        in_specs=[pl.BlockSpec((ns, tb, c), lambda i: (0, i, 0)), row, row, row],
        out_specs=[row, row, row, row], out_shape=[shp, shp, shp, shp], compiler_params=_cp("parallel"),
    )(slots, w, m, v)


def _pair_sum(own, got, name, out_dtype, tb):
    _, nj, r, c = own.shape
    mc = lax.axis_index("c")

    def body(mc_ref, a_ref, b_ref, o_ref):
        del mc_ref
        o_ref[...] = (a_ref[...] + b_ref[...]).astype(out_dtype)

    return pl.pallas_call(
        body, name=name,
        grid_spec=pltpu.PrefetchScalarGridSpec(
            num_scalar_prefetch=1, grid=(nj, r // tb),
            in_specs=[pl.BlockSpec((None, None, tb, c), lambda j, i, mc_ref: (mc_ref[0], j, i, 0)),
                      pl.BlockSpec((None, tb, c), lambda j, i, mc_ref: (j, i, 0))],
            out_specs=pl.BlockSpec((None, tb, c), lambda j, i, mc_ref: (j, i, 0))),
        out_shape=jax.ShapeDtypeStruct((nj, r, c), out_dtype), compiler_params=_cp("parallel", "parallel"),
    )(jnp.reshape(mc, (1,)).astype(jnp.int32), own, got)


def _slot_sum(slots, name):
    ns, r, c = slots.shape

    def body(s_ref, o_ref):
        g = s_ref[0]
        for j in range(1, ns):
            g = g + s_ref[j]
        o_ref[...] = g

    return pl.pallas_call(body, name=name, out_shape=jax.ShapeDtypeStruct((r, c), F32))(slots)


def _position():
    return lax.axis_index("x"), lax.axis_index("y"), lax.axis_index("c")


def _all_gather(x, name):
    r, c = x.shape

    def body(x_ref, out_ref, send_sems, recv_sems, local_sem):
        mx, my, mc = _position()
        me, sibling = (mx, my, mc), (mx, my, 1 - mc)
        chips = [(1 - mx, my), (mx, 1 - my), (1 - mx, 1 - my)]

        def rows(px, py, pc):
            return out_ref.at[4 * px + 2 * py + pc]

        def copy(k, block, to, src=None):
            return pltpu.make_async_remote_copy(
                src_ref=rows(*block) if src is None else src, dst_ref=rows(*block),
                send_sem=send_sems.at[k], recv_sem=recv_sems.at[k], device_id=to, device_id_type=MESH)

        mine = pltpu.make_async_copy(x_ref, rows(*me), local_sem)
        mine.start()
        first = [copy(0, me, sibling, src=x_ref)]
        first += [copy(1 + j, me, (*chip, mc), src=x_ref) for j, chip in enumerate(chips)]
        for cp in first:
            cp.start()
        passed = [copy(4 + j, (*chip, mc), sibling) for j, chip in enumerate(chips)]
        for j, chip in enumerate(chips):
            copy(1 + j, (*chip, mc), me).wait_recv()
            passed[j].start()
        copy(0, sibling, me).wait_recv()
        for j, chip in enumerate(chips):
            copy(4 + j, (*chip, 1 - mc), me).wait_recv()
        for cp in first + passed:
            cp.wait_send()
        mine.wait()

    return pl.pallas_call(
        body, name=name, in_specs=[ANY], out_specs=ANY,
        out_shape=jax.ShapeDtypeStruct((N_DEV, r, c), x.dtype),
        scratch_shapes=[pltpu.SemaphoreType.DMA((7,)), pltpu.SemaphoreType.DMA((7,)), pltpu.SemaphoreType.DMA],
    )(x)


def _exchange_sibling(gb, gs, name):
    def body(gb_ref, gs_ref, rb_ref, rs_ref, send_sems, recv_sems):
        mx, my, mc = _position()
        sibling = (mx, my, 1 - mc)
        cps = [pltpu.make_async_remote_copy(src_ref=src.at[1 - mc], dst_ref=dst, send_sem=send_sems.at[k],
                                            recv_sem=recv_sems.at[k], device_id=sibling, device_id_type=MESH)
               for k, (src, dst) in enumerate(((gb_ref, rb_ref), (gs_ref, rs_ref)))]
        for cp in cps:
            cp.start()
        for cp in cps:
            cp.wait()

    return pl.pallas_call(
        body, name=name, in_specs=[ANY, ANY], out_specs=[ANY, ANY],
        out_shape=[jax.ShapeDtypeStruct(gb.shape[1:], gb.dtype), jax.ShapeDtypeStruct(gs.shape[1:], gs.dtype)],
        scratch_shapes=[pltpu.SemaphoreType.DMA((2,)), pltpu.SemaphoreType.DMA((2,))],
    )(gb, gs)


def _exchange_chips(sb, ss, name):
    def body(sb_ref, ss_ref, rb_ref, rs_ref, send_sems, recv_sems, local_sems):
        mx, my, mc = _position()
        my_chip = 2 * mx + my
        pairs = ((sb_ref, rb_ref), (ss_ref, rs_ref))
        local = [pltpu.make_async_copy(src.at[my_chip], dst.at[my_chip], local_sems.at[a])
                 for a, (src, dst) in enumerate(pairs)]
        for cp in local:
            cp.start()
        chips = [(1 - mx, my), (mx, 1 - my), (1 - mx, 1 - my)]

        def copy(k, a, to_slot):
            px, py = chips[k]
            src, dst = pairs[a]
            return pltpu.make_async_remote_copy(
                src_ref=src.at[2 * px + py], dst_ref=dst.at[to_slot], send_sem=send_sems.at[2 * k + a],
                recv_sem=recv_sems.at[2 * k + a], device_id=(px, py, mc), device_id_type=MESH)

        sends = [copy(k, a, my_chip) for k in range(3) for a in range(2)]
        for cp in sends:
            cp.start()
        for k in range(3):
            px, py = chips[k]
            for a in range(2):
                copy(k, a, 2 * px + py).wait_recv()
        for cp in sends:
            cp.wait_send()
        for cp in local:
            cp.wait()

    return pl.pallas_call(
        body, name=name, in_specs=[ANY, ANY], out_specs=[ANY, ANY],
        out_shape=[jax.ShapeDtypeStruct(sb.shape, sb.dtype), jax.ShapeDtypeStruct(ss.shape, ss.dtype)],
        scratch_shapes=[pltpu.SemaphoreType.DMA((6,)), pltpu.SemaphoreType.DMA((6,)), pltpu.SemaphoreType.DMA((2,))],
    )(sb, ss)


_BIG = (("w_in", "col", (1024, 1412)), ("mem_w_kv", "col", (1024, 256)), ("w_br_lru", "row", (128, 1024)),
        ("w_br_ssd", "row", (256, 1024)), ("w_br_xa", "row", (128, 1024)), ("w_out", "row", (128, 1024)),
        ("ffn_w_in", "col", (1024, 704)), ("ffn_w_down", "row", (352, 1024)))
_SMALL = (("b_gate", (3, 128)), ("lru_conv_w", (4, 128)), ("ssd_conv_w", (4, 384)))
_REP = (("lru_conv_b", (1024,)), ("lru_w_a", (8, 128, 128)), ("lru_b_a", (1024,)), ("lru_w_i", (8, 128, 128)),
        ("lru_b_i", (1024,)), ("lru_lambda", (1024,)), ("ssd_conv_b", (3072,)), ("ssd_dt_bias", (32,)),
        ("ssd_a_log", (32,)), ("ssd_d", (32,)), ("ssd_norm_w", (2048,)), ("ln1_g", (1024,)), ("ln1_b", (1024,)),
        ("ln2_g", (1024,)), ("ln2_b", (1024,)))
_ORDER = ("w_in", "b_gate", "lru_conv_w", "lru_conv_b", "lru_w_a", "lru_b_a", "lru_w_i", "lru_b_i", "lru_lambda",
          "ssd_conv_w", "ssd_conv_b", "ssd_dt_bias", "ssd_a_log", "ssd_d", "ssd_norm_w", "mem_w_kv", "w_br_lru",
          "w_br_ssd", "w_br_xa", "w_out", "ln1_g", "ln1_b", "ffn_w_in", "ffn_w_down", "ln2_g", "ln2_b")

LANES = 1024
R_BIG_REAL = sum(DEPTH * s[0] * s[1] for _, _, s in _BIG) // LANES
R_BIG = 6784
TB_BIG = 128
N_SMALL = sum(DEPTH * s[0] * s[1] for _, s in _SMALL)
R_SMALL = 8
N_REP = sum(DEPTH * math.prod(s) for _, s in _REP)
R_REP = 68
R_SM = R_SMALL + R_REP + 4
R_TAIL = R_SMALL + N_DEV * R_REP
TB_TAIL = 184
assert R_BIG_REAL <= R_BIG and N_SMALL <= R_SMALL * LANES and N_REP <= N_DEV * R_REP * LANES


def _rows(flat, rows):
    return jnp.pad(flat, (0, rows * LANES - flat.shape[0])).reshape(rows, LANES)


def _pack_big(d, dtype):
    flat = jnp.concatenate([d[n].astype(dtype).reshape(-1) for n, _, _ in _BIG])
    return _rows(flat, R_BIG)


def _unpack_big(a):
    out, r0 = {}, 0
    for n, _, s in _BIG:
        rows = DEPTH * s[0] * s[1] // LANES
        out[n] = a[r0:r0 + rows].reshape((DEPTH,) + s)
        r0 += rows
    return out


def _pack_tail(d):
    small = jnp.concatenate([d[n].reshape(-1) for n, _ in _SMALL])
    rep = jnp.concatenate([d[n].reshape(-1) for n, _ in _REP])
    return jnp.concatenate([_rows(small, R_SMALL), _rows(rep, N_DEV * R_REP)], axis=0)


def _unpack_tail(a):
    out, o = {}, 0
    flat = a[:R_SMALL].reshape(-1)
    for n, s in _SMALL:
        k = DEPTH * math.prod(s)
        out[n] = flat[o:o + k].reshape((DEPTH,) + s)
        o += k
    flat, o = a[R_SMALL:].reshape(-1), 0
    for n, s in _REP:
        k = DEPTH * math.prod(s)
        out[n] = flat[o:o + k].reshape((DEPTH,) + s)
        o += k
    return out


def _by_dest(g, kind):
    if kind == "col":
        g = g.reshape(g.shape[:-1] + (N_DEV, g.shape[-1] // N_DEV))
        g = jnp.moveaxis(g, -2, 0)
    else:
        g = g.reshape((DEPTH, N_DEV, g.shape[1] // N_DEV) + g.shape[2:])
        g = jnp.moveaxis(g, 1, 0)
    return g.reshape(N_DEV, -1)


def _from_stack(st, kind):
    if kind == "col":
        st = jnp.moveaxis(st, 0, -2)
        return st.reshape(st.shape[:-2] + (st.shape[-2] * st.shape[-1],))
    st = jnp.moveaxis(st, 0, 1)
    return st.reshape((DEPTH, st.shape[1] * st.shape[2]) + st.shape[3:])


def _split_w_in(w):
    pad = jnp.zeros((D, W_SSD - S_DT - DT_REAL), w.dtype)
    return dict(ssd=jnp.concatenate([w[:, 2048:7168], w[:, 7168:7200], pad], axis=1), lru=w[:, 0:2048],
                q=w[:, 7200:8224], gl=w[:, 8224:11296])


def _join_w_in(g):
    return jnp.concatenate([g["lru"], g["ssd"][:, :S_DT + DT_REAL], g["q"], g["gl"]], axis=1)


def _layer_fwd(x, mem, w, l):
    nm = lambda s: f"{s}_l{l}"
    wi = _split_w_in(w["w_in"])
    row = lambda v: v.reshape(1, -1)
    s = dict(x=x, wi=wi)
    s["p_ssd"] = _mm(x, wi["ssd"], name=nm("proj_ssd"))
    s["p_lru"] = _mm(x, wi["lru"], name=nm("proj_lru"))
    s["p_q"] = _mm(x, wi["q"], name=nm("proj_q"))
    s["p_gl"] = _mm(x, wi["gl"], name=nm("proj_gl"))
    s["lru_par"] = (w["lru_conv_w"], row(w["lru_conv_b"]), w["lru_w_a"], row(w["lru_b_a"]), w["lru_w_i"],
                    row(w["lru_b_i"]), row(w["lru_lambda"]))
    s["y_lru"], s["h"] = _lru_fwd(s["p_lru"], *s["lru_par"], name=nm("lru_fwd"))
    s["ssd_par"] = (w["ssd_conv_w"], row(w["ssd_conv_b"]), row(w["ssd_dt_bias"]), row(w["ssd_a_log"]),
                    row(w["ssd_d"]), row(w["ssd_norm_w"]))
    s["y_ssd"], s["yraw"], s["hs"] = _ssd_fwd(s["p_ssd"], *s["ssd_par"], name=nm("ssd_fwd"))
    s["kv"] = _mm(mem, w["mem_w_kv"], name=nm("kv"))
    s["y_xa"] = _xa_fwd(s["p_q"], s["kv"], name=nm("xa_fwd"))
    s["b1"] = _mm(s["y_lru"], w["w_br_lru"], name=nm("br_lru"))
    s["b2"] = _mm(s["y_ssd"], w["w_br_ssd"], name=nm("br_ssd"))
    s["b3"] = _mm(s["y_xa"], w["w_br_xa"], name=nm("br_xa"))
    s["bg"] = row(w["b_gate"])
    s["merged"] = _merge_fwd(s["p_gl"], s["bg"], s["b1"], s["b2"], s["b3"], name=nm("merge_fwd"))
    s["mix"] = _mm(s["merged"], w["w_out"], name=nm("out_proj"))
    s["x1"] = _ln_fwd(x, s["mix"], row(w["ln1_g"]), row(w["ln1_b"]), name=nm("ln1_fwd"))
    s["gu"] = _mm(s["x1"], w["ffn_w_in"], name=nm("ffn_in"))
    s["act"] = _swiglu_fwd(s["gu"], name=nm("swiglu_fwd"))
    s["f"] = _mm(s["act"], w["ffn_w_down"], name=nm("ffn_down"))
    s["x2"] = _ln_fwd(s["x1"], s["f"], row(w["ln2_g"]), row(w["ln2_b"]), name=nm("ln2_fwd"))
    return s


def _layer_bwd(s, mem, w, dxo, l):
    nm = lambda t: f"{t}_l{l}"
    row = lambda v: v.reshape(1, -1)
    g = {}
    du2, dg, db = _ln_bwd(s["x1"], s["f"], dxo, row(w["ln2_g"]), name=nm("ln2_bwd"))
    g["ln2_g"], g["ln2_b"] = dg[0], db[0]
    dact = _mm(du2, w["ffn_w_down"], tb=True, name=nm("d_act"))
    g["ffn_w_down"] = _mm(s["act"], du2, ta=True, name=nm("dw_ffn_down"))
    dgu = _swiglu_bwd(s["gu"], dact, name=nm("swiglu_bwd"))
    dx1 = _mm(dgu, w["ffn_w_in"], tb=True, add=du2, add_scale=ALPHA, name=nm("d_x1"))
    g["ffn_w_in"] = _mm(s["x1"], dgu, ta=True, name=nm("dw_ffn_in"))
    du1, dg, db = _ln_bwd(s["x"], s["mix"], dx1, row(w["ln1_g"]), name=nm("ln1_bwd"))
    g["ln1_g"], g["ln1_b"] = dg[0], db[0]
    dmerged = _mm(du1, w["w_out"], tb=True, name=nm("d_merged"))
    g["w_out"] = _mm(s["merged"], du1, ta=True, name=nm("dw_out"))
    dp_gl, d1, d2, d3, dbg = _merge_bwd(s["p_gl"], s["bg"], s["b1"], s["b2"], s["b3"], dmerged, name=nm("merge_bwd"))
    g["b_gate"] = dbg.reshape(3, D)
    dy_lru = _mm(d1, w["w_br_lru"], tb=True, name=nm("d_y_lru"))
    g["w_br_lru"] = _mm(s["y_lru"], d1, ta=True, name=nm("dw_br_lru"))
    dy_ssd = _mm(d2, w["w_br_ssd"], tb=True, name=nm("d_y_ssd"))
    g["w_br_ssd"] = _mm(s["y_ssd"], d2, ta=True, name=nm("dw_br_ssd"))
    dy_xa = _mm(d3, w["w_br_xa"], tb=True, name=nm("d_y_xa"))
    g["w_br_xa"] = _mm(s["y_xa"], d3, ta=True, name=nm("dw_br_xa"))
    dp_q, dkv = _xa_bwd(s["p_q"], s["kv"], dy_xa, name=nm("xa_bwd"))
    g["mem_w_kv"] = _mm(mem, dkv, ta=True, name=nm("dw_kv"))
    dp_ssd, dcw, dcb, ddtb, dalog, dd, dnw = _ssd_bwd(s["p_ssd"], s["yraw"], s["hs"], dy_ssd, *s["ssd_par"],
                                                      name=nm("ssd_bwd"))
    g["ssd_conv_w"], g["ssd_conv_b"], g["ssd_dt_bias"] = dcw, dcb[0], ddtb[0]
    g["ssd_a_log"], g["ssd_d"], g["ssd_norm_w"] = dalog[0], dd[0], dnw[0]
    dp_lru, dcw, dcb, dwa, dba, dwi, dbi, dlam = _lru_bwd(s["p_lru"], s["h"], dy_lru, *s["lru_par"], name=nm("lru_bwd"))
    g["lru_conv_w"], g["lru_conv_b"], g["lru_w_a"], g["lru_b_a"] = dcw, dcb[0], dwa, dba[0]
    g["lru_w_i"], g["lru_b_i"], g["lru_lambda"] = dwi, dbi[0], dlam[0]
    wi = s["wi"]
    dx = _mm(dp_ssd, wi["ssd"], tb=True, add=du1, add_scale=ALPHA, name=nm("dx_ssd"))
    dx = _mm(dp_lru, wi["lru"], tb=True, add=dx, name=nm("dx_lru"))
    dx = _mm(dp_q, wi["q"], tb=True, add=dx, name=nm("dx_q"))
    dx = _mm(dp_gl, wi["gl"], tb=True, add=dx, name=nm("dx_gl"))
    x = s["x"]
    g["w_in"] = _join_w_in(dict(ssd=_mm(x, dp_ssd, ta=True, name=nm("dw_in_ssd")),
                                lru=_mm(x, dp_lru, ta=True, name=nm("dw_in_lru")),
                                q=_mm(x, dp_q, ta=True, name=nm("dw_in_q")),
                                gl=_mm(x, dp_gl, ta=True, name=nm("dw_in_gl"))))
    return dx, g


def _local_step(x, mem, target, w):
    layers = [{n: v[l] for n, v in w.items()} for l in range(DEPTH)]
    saved = []
    for l in range(DEPTH):
        saved.append(_layer_fwd(x, mem, layers[l], l))
        x = saved[-1]["x2"]
    dx, loss = _loss_head(x, target, name="loss_head")
    grads = [None] * DEPTH
    for l in reversed(range(DEPTH)):
        dx, grads[l] = _layer_bwd(saved[l], mem, layers[l], dx, l)
    return loss, dx, {n: jnp.stack([grads[l][n] for l in range(DEPTH)]) for n in _ORDER}


def kernel(x, mem, w_in, b_gate, lru_conv_w, lru_conv_b, lru_w_a, lru_b_a, lru_w_i, lru_b_i, lru_lambda, ssd_conv_w, ssd_conv_b, ssd_dt_bias, ssd_a_log, ssd_d, ssd_norm_w, mem_w_kv, w_br_lru, w_br_ssd, w_br_xa, w_out, ln1_g, ln1_b, ffn_w_in, ffn_w_down, ln2_g, ln2_b, loss_target, m_w_in, m_b_gate, m_lru_conv_w, m_lru_conv_b, m_lru_w_a, m_lru_b_a, m_lru_w_i, m_lru_b_i, m_lru_lambda, m_ssd_conv_w, m_ssd_conv_b, m_ssd_dt_bias, m_ssd_a_log, m_ssd_d, m_ssd_norm_w, m_mem_w_kv, m_w_br_lru, m_w_br_ssd, m_w_br_xa, m_w_out, m_ln1_g, m_ln1_b, m_ffn_w_in, m_ffn_w_down, m_ln2_g, m_ln2_b, v_w_in, v_b_gate, v_lru_conv_w, v_lru_conv_b, v_lru_w_a, v_lru_b_a, v_lru_w_i, v_lru_b_i, v_lru_lambda, v_ssd_conv_w, v_ssd_conv_b, v_ssd_dt_bias, v_ssd_a_log, v_ssd_d, v_ssd_norm_w, v_mem_w_kv, v_w_br_lru, v_w_br_ssd, v_w_br_xa, v_w_out, v_ln1_g, v_ln1_b, v_ffn_w_in, v_ffn_w_down, v_ln2_g, v_ln2_b):
    local = dict(locals())
    w = {n: local[n] for n in _ORDER}
    m = {n: local["m_" + n] for n in _ORDER}
    v = {n: local["v_" + n] for n in _ORDER}

    small = _rows(jnp.concatenate([w[n].reshape(-1) for n, _ in _SMALL]), R_SMALL).reshape(-1)
    small_rows = lax.bitcast_convert_type(small, BF16).reshape(2 * R_SMALL, LANES)
    gathered = _all_gather(jnp.concatenate([_pack_big(w, BF16), small_rows], axis=0), name="gather_weights")
    full = dict(w)
    r0 = 0
    for n, kind, s in _BIG:
        rows = DEPTH * s[0] * s[1] // LANES
        full[n] = _from_stack(gathered[:, r0:r0 + rows].reshape((N_DEV, DEPTH) + s), kind)
        r0 += rows
    small_all = lax.bitcast_convert_type(gathered[:, R_BIG:].reshape(N_DEV, R_SMALL * LANES, 2), F32)
    o = 0
    for n, s in _SMALL:
        k = DEPTH * s[0] * s[1]
        full[n] = _from_stack(small_all[:, o:o + k].reshape((N_DEV, DEPTH) + s), "col")
        o += k

    loss_tile, dx, grads = _local_step(x[0], mem[0], loss_target[0], full)
    loss = lax.psum(loss_tile[0, 0], ("x", "y", "c"))

    big = jnp.concatenate([_by_dest(grads[n], kind) for n, kind, _ in _BIG], axis=1)
    big = jnp.pad(big, ((0, 0), (0, R_BIG * LANES - big.shape[1]))).reshape(4, 2, R_BIG, LANES)
    sm = jnp.concatenate([_by_dest(grads[n], "col") for n, _ in _SMALL], axis=1)
    sm = jnp.pad(sm, ((0, 0), (0, R_SMALL * LANES - sm.shape[1])))
    rep = jnp.concatenate([grads[n].reshape(-1) for n, _ in _REP])
    rep = jnp.pad(rep, (0, N_DEV * R_REP * LANES - rep.shape[0])).reshape(N_DEV, R_REP * LANES)
    tail = jnp.concatenate([sm, rep, jnp.zeros((N_DEV, (R_SM - R_SMALL - R_REP) * LANES), F32)], axis=1)
    tail = tail.reshape(4, 2, R_SM, LANES)
    big, tail = jnp.swapaxes(big, 0, 1), jnp.swapaxes(tail, 0, 1)
    got_big, got_tail = _exchange_sibling(big, tail, name="reduce_cores")
    sum_big = _pair_sum(big, got_big, name="pair_sum_big", out_dtype=BF16, tb=TB_BIG)
    sum_tail = _pair_sum(tail, got_tail, name="pair_sum_tail", out_dtype=F32, tb=R_SM)
    slots_big, slots_tail = _exchange_chips(sum_big, sum_tail, name="reduce_chips")

    gb, db_, mb, vb = _adamw(slots_big, _pack_big(w, F32), _pack_big(m, F32), _pack_big(v, F32),
                             name="adamw_big", tb=TB_BIG)
    tail_sum = _slot_sum(slots_tail, name="sum_tail")
    rep_all = _all_gather(tail_sum[R_SMALL:R_SMALL + R_REP], name="gather_replicated")
    g_tail = jnp.concatenate([tail_sum[:R_SMALL], rep_all.reshape(N_DEV * R_REP, LANES)], axis=0)
    gt, dt_, mt, vt = _adamw(g_tail[None], _pack_tail(w), _pack_tail(m), _pack_tail(v), name="adamw_tail", tb=TB_TAIL)

    outs = []
    for big_a, tail_a in ((gb, gt), (db_, dt_), (mb, mt), (vb, vt)):
        d = {**_unpack_big(big_a), **_unpack_tail(tail_a)}
        outs += [d[n] for n in _ORDER]
    return (loss, dx[None], *outs)
```

```python
import math

import jax
import jax.numpy as jnp
from jax import lax
from jax.experimental import pallas as pl
from jax.experimental.pallas import tpu as pltpu

F32 = jnp.float32
BF16 = jnp.bfloat16

D = 1024
DEPTH = 2
N_DEV = 8
CHUNK = 64
LRU_BLOCKS = 8
LRU_BLOCK = 128
LRU_C = 8.0
D_SSD = 2 * D
SSD_HEADS = 32
SSD_GROUPS = 4
GROUP_W = D_SSD // SSD_GROUPS
SSD_STATE = 128
D_XBC = D_SSD + 2 * SSD_GROUPS * SSD_STATE
XA_HEADS = 4
XA_HEAD_DIM = 256
D_FF = 2816
ALPHA = (2 * DEPTH) ** 0.25
EPS = 1e-5
N_IN = 11296

S_Z, S_XBC, S_DT, W_SSD = 0, 2048, 5120, 5632
DT_REAL = 32

ADAM_LR, ADAM_B1, ADAM_B2, ADAM_EPS, ADAM_WD, ADAM_STEP = 0.001, 0.9, 0.999, 1e-08, 0.01, 10

VMEM_LIMIT = 56 * 1024 * 1024
MESH = pl.DeviceIdType.MESH
ANY = pl.BlockSpec(memory_space=pl.ANY)


def _cp(*sem):
    return pltpu.CompilerParams(dimension_semantics=sem, vmem_limit_bytes=VMEM_LIMIT)


def _blk(n, target):
    if n % 128:
        return n
    best = 128
    for b in range(128, min(n, target) + 1, 128):
        if n % b == 0:
            best = b
    return best


def _iota(shape, dim):
    return lax.broadcasted_iota(jnp.int32, shape, dim)


def _sigmoid(x):
    return 1.0 / (1.0 + jnp.exp(-x))


def _log1p(e):
    u = 1.0 + e
    return jnp.where(u == 1.0, e, jnp.log(u) * (e / (u - 1.0)))


def _softplus(x):
    return jnp.maximum(x, 0.0) + _log1p(jnp.exp(-jnp.abs(x)))


def _expm1(x):
    u = jnp.exp(x)
    um = u - 1.0
    return jnp.where(um == 0.0, x, jnp.where(um == -1.0, -1.0, um * (x / jnp.log(u))))


_G0 = math.sqrt(2.0 / math.pi)
_G1 = 0.044715


def _gelu_and_grad(x):
    t = jnp.tanh(_G0 * (x + _G1 * x * x * x))
    g = 0.5 * x * (1.0 + t)
    dg = 0.5 * (1.0 + t) + 0.5 * x * (1.0 - t * t) * (_G0 * (1.0 + 3.0 * _G1 * x * x))
    return g, dg


_NN = (((1,), (0,)), ((), ()))
_NT = (((1,), (1,)), ((), ()))
_TN = (((0,), (0,)), ((), ()))


def _dot(a, b, dims=_NN):
    return lax.dot_general(a.astype(BF16), b.astype(BF16), dims, preferred_element_type=F32)


def _dot_hi(a, b, dims=_NN):
    return lax.dot_general(a, b, dims, precision=lax.Precision.HIGHEST, preferred_element_type=F32)


def _split(v):
    hi = v.astype(BF16)
    return hi, (v - hi.astype(F32)).astype(BF16)


def _dot01(v, e, dims=_NN):
    hi, lo = _split(v)
    return (lax.dot_general(hi, e, dims, preferred_element_type=F32)
            + lax.dot_general(lo, e, dims, preferred_element_type=F32))


def _conv_taps(xe, n):
    return [xe[8:8 + n] if j == 3 else pltpu.roll(xe, 3 - j, 0)[8:8 + n] for j in range(4)]


def _conv_fwd(taps, cw, cb):
    return cb + cw[0:1] * taps[0] + cw[1:2] * taps[1] + cw[2:3] * taps[2] + cw[3:4] * taps[3]


def _conv_bwd(dc, dnext, taps, cw, n):
    ext = jnp.concatenate([dc, dnext], axis=0)
    dx = cw[3:4] * dc
    for j in range(3):
        dx = dx + cw[j:j + 1] * pltpu.roll(ext, n + 8 - (3 - j), 0)[0:n]
    dcw = jnp.concatenate([jnp.sum(dc * taps[j], axis=0, keepdims=True) for j in range(4)], axis=0)
    return dx, dcw, jnp.sum(dc, axis=0, keepdims=True)


def _mm(a, b, *, ta=False, tb=False, out_dtype=F32, add=None, add_scale=1.0, name, tm=1024, tn=512, tk=1024,
        split_n=None):
    if ta:
        k_dim, m_dim = a.shape
    else:
        m_dim, k_dim = a.shape
    if tb:
        n_dim, k2 = b.shape
    else:
        k2, n_dim = b.shape
    assert k_dim == k2, (a.shape, b.shape, ta, tb)
    tm, tn, tk = _blk(m_dim, tm), _blk(n_dim, tn), _blk(k_dim, tk)
    nk = k_dim // tk
    a_spec = pl.BlockSpec((tk, tm), lambda i, j, k: (k, i)) if ta else pl.BlockSpec((tm, tk), lambda i, j, k: (i, k))
    b_spec = pl.BlockSpec((tn, tk), lambda i, j, k: (j, k)) if tb else pl.BlockSpec((tk, tn), lambda i, j, k: (k, j))
    o_spec = pl.BlockSpec((tm, tn), lambda i, j, k: (i, j))
    out_shape = (m_dim, n_dim)
    if split_n is not None:
        assert add is None and tn == split_n, (tn, split_n)
        o_spec = pl.BlockSpec((None, tm, tn), lambda i, j, k: (j, i, 0))
        out_shape = (n_dim // tn, m_dim, tn)
    dims = (((0 if ta else 1,), (1 if tb else 0,)), ((), ()))
    has_add = add is not None

    def body(*refs):
        if has_add:
            a_ref, b_ref, add_ref, o_ref, acc_ref = refs
        else:
            a_ref, b_ref, o_ref, acc_ref = refs
        k = pl.program_id(2)

        @pl.when(k == 0)
        def _():
            acc_ref[...] = jnp.zeros_like(acc_ref)

        acc_ref[...] += lax.dot_general(a_ref[...].astype(BF16), b_ref[...].astype(BF16), dims,
                                        preferred_element_type=F32)

        @pl.when(k == nk - 1)
        def _():
            r = acc_ref[...]
            if has_add:
                r = r + add_scale * add_ref[...]
            o_ref[...] = r.astype(out_dtype)

    in_specs = [a_spec, b_spec] + ([o_spec] if has_add else [])
    args = (a, b) + ((add,) if has_add else ())
    return pl.pallas_call(
        body, name=name, grid=(m_dim // tm, n_dim // tn, nk),
        in_specs=in_specs, out_specs=o_spec,
        out_shape=jax.ShapeDtypeStruct(out_shape, out_dtype),
        scratch_shapes=[pltpu.VMEM((tm, tn), F32)],
        compiler_params=_cp("parallel", "parallel", "arbitrary"),
    )(*args)


def _ln_fwd(x, f, g, b, name, tb=512):
    t = x.shape[0]
    tb = min(tb, t)

    def body(x_ref, f_ref, g_ref, b_ref, o_ref):
        u = ALPHA * x_ref[...] + f_ref[...]
        mu = jnp.mean(u, axis=-1, keepdims=True)
        d = u - mu
        var = jnp.mean(d * d, axis=-1, keepdims=True)
        o_ref[...] = d * lax.rsqrt(var + EPS) * g_ref[...] + b_ref[...]

    row = pl.BlockSpec((tb, D), lambda i: (i, 0))
    par = pl.BlockSpec((1, D), lambda i: (0, 0))
    return pl.pallas_call(
        body, name=name, grid=(t // tb,), in_specs=[row, row, par, par], out_specs=row,
        out_shape=jax.ShapeDtypeStruct((t, D), F32), compiler_params=_cp("parallel"),
    )(x, f, g, b)


def _ln_bwd(x, f, dy, g, name, tb=512):
    t = x.shape[0]
    tb = min(tb, t)

    def body(x_ref, f_ref, dy_ref, g_ref, du_ref, dg_ref, db_ref):
        @pl.when(pl.program_id(0) == 0)
        def _():
            dg_ref[...] = jnp.zeros_like(dg_ref)
            db_ref[...] = jnp.zeros_like(db_ref)

        u = ALPHA * x_ref[...] + f_ref[...]
        mu = jnp.mean(u, axis=-1, keepdims=True)
        d = u - mu
        var = jnp.mean(d * d, axis=-1, keepdims=True)
        rstd = lax.rsqrt(var + EPS)
        xhat = d * rstd
        dy = dy_ref[...]
        dxh = dy * g_ref[...]
        m1 = jnp.mean(dxh, axis=-1, keepdims=True)
        m2 = jnp.mean(dxh * xhat, axis=-1, keepdims=True)
        du_ref[...] = rstd * (dxh - m1 - xhat * m2)
        dg_ref[...] += jnp.sum(dy * xhat, axis=0, keepdims=True)
        db_ref[...] += jnp.sum(dy, axis=0, keepdims=True)

    row = pl.BlockSpec((tb, D), lambda i: (i, 0))
    par = pl.BlockSpec((1, D), lambda i: (0, 0))
    return pl.pallas_call(
        body, name=name, grid=(t // tb,), in_specs=[row, row, row, par], out_specs=[row, par, par],
        out_shape=[jax.ShapeDtypeStruct((t, D), F32), jax.ShapeDtypeStruct((1, D), F32),
                   jax.ShapeDtypeStruct((1, D), F32)],
        compiler_params=_cp("arbitrary"),
    )(x, f, dy, g)


def _swiglu_fwd(gu, name, tb=512):
    t = gu.shape[0]
    tb = min(tb, t)

    def body(g_ref, u_ref, o_ref):
        g = g_ref[...]
        o_ref[...] = (g * _sigmoid(g) * u_ref[...]).astype(BF16)

    return pl.pallas_call(
        body, name=name, grid=(t // tb,),
        in_specs=[pl.BlockSpec((tb, D_FF), lambda i: (i, 0)), pl.BlockSpec((tb, D_FF), lambda i: (i, 1))],
        out_specs=pl.BlockSpec((tb, D_FF), lambda i: (i, 0)),
        out_shape=jax.ShapeDtypeStruct((t, D_FF), BF16), compiler_params=_cp("parallel"),
    )(gu, gu)


def _swiglu_bwd(gu, dact, name, tb=512):
    t = gu.shape[0]
    tb = min(tb, t)

    def body(g_ref, u_ref, da_ref, o_ref):
        g = g_ref[...]
        s = _sigmoid(g)
        da = da_ref[...]
        o_ref[:, :D_FF] = (da * u_ref[...] * (s * (1.0 + g * (1.0 - s)))).astype(BF16)
        o_ref[:, D_FF:] = (da * g * s).astype(BF16)

    return pl.pallas_call(
        body, name=name, grid=(t // tb,),
        in_specs=[pl.BlockSpec((tb, D_FF), lambda i: (i, 0)), pl.BlockSpec((tb, D_FF), lambda i: (i, 1)),
                  pl.BlockSpec((tb, D_FF), lambda i: (i, 0))],
        out_specs=pl.BlockSpec((tb, 2 * D_FF), lambda i: (i, 0)),
        out_shape=jax.ShapeDtypeStruct((t, 2 * D_FF), BF16),
        compiler_params=_cp("parallel"),
    )(gu, gu, dact)


def _merge_fwd(pgl, bg, b1, b2, b3, name, tb=512):
    t = pgl.shape[0]
    tb = min(tb, t)

    def body(gl_ref, bg_ref, b1_ref, b2_ref, b3_ref, o_ref):
        acc = None
        for j, b_ref in enumerate((b1_ref, b2_ref, b3_ref)):
            sl = slice(j * D, (j + 1) * D)
            term = _sigmoid(gl_ref[:, sl] + bg_ref[:, sl]) * b_ref[...]
            acc = term if acc is None else acc + term
        o_ref[...] = acc.astype(BF16)

    row = pl.BlockSpec((tb, D), lambda i: (i, 0))
    return pl.pallas_call(
        body, name=name, grid=(t // tb,),
        in_specs=[pl.BlockSpec((tb, 3 * D), lambda i: (i, 0)), pl.BlockSpec((1, 3 * D), lambda i: (0, 0)), row, row, row],
        out_specs=row, out_shape=jax.ShapeDtypeStruct((t, D), BF16), compiler_params=_cp("parallel"),
    )(pgl, bg, b1, b2, b3)


def _merge_bwd(pgl, bg, b1, b2, b3, dm, name, tb=512):
    t = pgl.shape[0]
    tb = min(tb, t)

    def body(gl_ref, bg_ref, b1_ref, b2_ref, b3_ref, dm_ref, dgl_ref, d1_ref, d2_ref, d3_ref, dbg_ref):
        @pl.when(pl.program_id(0) == 0)
        def _():
            dbg_ref[...] = jnp.zeros_like(dbg_ref)

        dm_v = dm_ref[...]
        for j, (b_ref, d_ref) in enumerate(((b1_ref, d1_ref), (b2_ref, d2_ref), (b3_ref, d3_ref))):
            sl = slice(j * D, (j + 1) * D)
            gate = _sigmoid(gl_ref[:, sl] + bg_ref[:, sl])
            d_ref[...] = (dm_v * gate).astype(BF16)
            dgl = dm_v * b_ref[...] * (gate * (1.0 - gate))
            dgl_ref[:, sl] = dgl.astype(BF16)
            dbg_ref[:, sl] += jnp.sum(dgl, axis=0, keepdims=True)

    row = pl.BlockSpec((tb, D), lambda i: (i, 0))
    wide = pl.BlockSpec((tb, 3 * D), lambda i: (i, 0))
    par = pl.BlockSpec((1, 3 * D), lambda i: (0, 0))
    return pl.pallas_call(
        body, name=name, grid=(t // tb,),
        in_specs=[wide, par, row, row, row, row], out_specs=[wide, row, row, row, par],
        out_shape=[jax.ShapeDtypeStruct((t, 3 * D), BF16)] + [jax.ShapeDtypeStruct((t, D), BF16)] * 3
                  + [jax.ShapeDtypeStruct((1, 3 * D), F32)],
        compiler_params=_cp("arbitrary"),
    )(pgl, bg, b1, b2, b3, dm)


def _xa_probs(q, kv_ref, hd):
    sl = slice(hd * XA_HEAD_DIM, (hd + 1) * XA_HEAD_DIM)
    k = kv_ref[:, sl]
    v = kv_ref[:, D + hd * XA_HEAD_DIM:D + (hd + 1) * XA_HEAD_DIM]
    s = _dot(q[:, sl], k, _NT) * (XA_HEAD_DIM ** -0.5)
    e = jnp.exp(s - jnp.max(s, axis=1, keepdims=True))
    return sl, k, v, e / jnp.sum(e, axis=1, keepdims=True)


def _xa_fwd(pq, kv, name, tb=512):
    t = pq.shape[0]
    tb = min(tb, t)

    def body(q_ref, kv_ref, o_ref):
        q = q_ref[...]
        for hd in range(XA_HEADS):
            sl, _, v, p = _xa_probs(q, kv_ref, hd)
            o_ref[:, sl] = _dot(p, v).astype(BF16)

    row = pl.BlockSpec((tb, D), lambda i: (i, 0))
    return pl.pallas_call(
        body, name=name, grid=(t // tb,),
        in_specs=[row, pl.BlockSpec(kv.shape, lambda i: (0, 0))], out_specs=row,
        out_shape=jax.ShapeDtypeStruct((t, D), BF16), compiler_params=_cp("parallel"),
    )(pq, kv)


def _xa_bwd(pq, kv, dy, name, tb=512):
    t = pq.shape[0]
    tb = min(tb, t)

    def body(q_ref, kv_ref, dy_ref, dq_ref, dkv_ref):
        @pl.when(pl.program_id(0) == 0)
        def _():
            dkv_ref[...] = jnp.zeros_like(dkv_ref)

        q = q_ref[...]
        for hd in range(XA_HEADS):
            sl, k, v, p = _xa_probs(q, kv_ref, hd)
            dyh = dy_ref[:, sl]
            vsl = slice(D + hd * XA_HEAD_DIM, D + (hd + 1) * XA_HEAD_DIM)
            dkv_ref[:, vsl] += _dot(p, dyh, _TN)
            dp = _dot(dyh, v, _NT)
            ds = p * (dp - jnp.sum(dp * p, axis=1, keepdims=True)) * (XA_HEAD_DIM ** -0.5)
            dq_ref[:, sl] = _dot(ds, k).astype(BF16)
            dkv_ref[:, sl] += _dot(ds, q[:, sl], _TN)

    row = pl.BlockSpec((tb, D), lambda i: (i, 0))
    kvs = pl.BlockSpec(kv.shape, lambda i: (0, 0))
    return pl.pallas_call(
        body, name=name, grid=(t // tb,), in_specs=[row, kvs, row], out_specs=[row, kvs],
        out_shape=[jax.ShapeDtypeStruct((t, D), BF16), jax.ShapeDtypeStruct(kv.shape, F32)],
        compiler_params=_cp("arbitrary"),
    )(pq, kv, dy)


def _scan_fwd(a, u):
    n = a.shape[0]
    row = _iota((n, 1), 0)
    d = 1
    while d < n:
        us = jnp.where(row >= d, pltpu.roll(u, d, 0), 0.0)
        u = a * us + u
        a = a * pltpu.roll(a, d, 0)
        d *= 2
    return u


def _scan_rev(b, u):
    n = b.shape[0]
    row = _iota((n, 1), 0)
    d = 1
    while d < n:
        us = jnp.where(row < n - d, pltpu.roll(u, n - d, 0), 0.0)
        u = b * us + u
        b = b * pltpu.roll(b, n - d, 0)
        d *= 2
    return u


def _lru_gates(xc, wa_ref, ba, wi_ref, bi, lam):
    za = jnp.concatenate([_dot(xc[:, n * 128:(n + 1) * 128], wa_ref[n]) for n in range(LRU_BLOCKS)], axis=1) + ba
    zi = jnp.concatenate([_dot(xc[:, n * 128:(n + 1) * 128], wi_ref[n]) for n in range(LRU_BLOCKS)], axis=1) + bi
    r = _sigmoid(za)
    ig = _sigmoid(zi)
    sp = _softplus(-lam)
    log_a = (-LRU_C) * r * sp
    a = jnp.exp(log_a)
    m = jnp.sqrt(-_expm1(2.0 * log_a))
    u = m * (ig * xc)
    return a, u, r, ig, m, sp


def _lru_conv(i_blk, x_ref, xp_ref, cw_ref, cb_ref, tb):
    halo = jnp.where(i_blk == 0, 0.0, xp_ref[...])
    taps = _conv_taps(jnp.concatenate([halo, x_ref[...]], axis=0), tb)
    cw = cw_ref[...]
    return _conv_fwd(taps, cw, cb_ref[...]), taps, cw


def _lru_fwd(p, cw, cb, wa, ba, wi, bi, lam, name, tb=256):
    t = p.shape[0]
    tb = min(tb, t)
    nb = t // tb
    r8 = tb // 8

    def body(x_ref, xp_ref, g_ref, cw_ref, cb_ref, wa_ref, ba_ref, wi_ref, bi_ref, lam_ref, y_ref, h_ref, hc_ref):
        i = pl.program_id(0)

        @pl.when(i == 0)
        def _():
            hc_ref[...] = jnp.zeros_like(hc_ref)

        xc, _, _ = _lru_conv(i, x_ref, xp_ref, cw_ref, cb_ref, tb)
        a, u, _, _, _, _ = _lru_gates(xc, wa_ref, ba_ref[...], wi_ref, bi_ref[...], lam_ref[...])
        row = _iota((tb, 1), 0)
        u = u + jnp.where(row == 0, a * hc_ref[...], 0.0)
        h = _scan_fwd(a, u)
        h_ref[...] = h
        hc_ref[...] = h[tb - 1:tb, :]
        gl, _ = _gelu_and_grad(g_ref[...])
        y_ref[...] = (gl * h).astype(BF16)

    par = pl.BlockSpec((1, D), lambda i: (0, 0))
    wsp = pl.BlockSpec((LRU_BLOCKS, LRU_BLOCK, LRU_BLOCK), lambda i: (0, 0, 0))
    row = pl.BlockSpec((tb, D), lambda i: (i, 0))
    return pl.pallas_call(
        body, name=name, grid=(nb,),
        in_specs=[row, pl.BlockSpec((8, D), lambda i: (jnp.maximum(i * r8 - 1, 0), 0)),
                  pl.BlockSpec((tb, D), lambda i: (i, 1)),
                  pl.BlockSpec((4, D), lambda i: (0, 0)), par, wsp, par, wsp, par, par],
        out_specs=[row, row],
        out_shape=[jax.ShapeDtypeStruct((t, D), BF16), jax.ShapeDtypeStruct((t, D), F32)],
        scratch_shapes=[pltpu.VMEM((1, D), F32)],
        compiler_params=_cp("arbitrary"),
    )(p, p, p, cw, cb, wa, ba, wi, bi, lam)


def _lru_bwd(p, h, dy, cw, cb, wa, ba, wi, bi, lam, name, tb=256):
    t = p.shape[0]
    tb = min(tb, t)
    nb = t // tb
    r8 = tb // 8

    def body(x_ref, xp_ref, g_ref, h_ref, hp_ref, dy_ref, cw_ref, cb_ref, wa_ref, ba_ref, wi_ref, bi_ref, lam_ref,
             dp_ref, dcw_ref, dcb_ref, dwa_ref, dba_ref, dwi_ref, dbi_ref, dlam_ref, carry_ref, dnext_ref):
        i = pl.program_id(0)
        blk = nb - 1 - i

        @pl.when(i == 0)
        def _():
            for r in (dcw_ref, dcb_ref, dwa_ref, dba_ref, dwi_ref, dbi_ref, dlam_ref, carry_ref, dnext_ref):
                r[...] = jnp.zeros_like(r)

        xc, taps, cw = _lru_conv(blk, x_ref, xp_ref, cw_ref, cb_ref, tb)
        lam = lam_ref[...]
        a, _, r, ig, m, sp = _lru_gates(xc, wa_ref, ba_ref[...], wi_ref, bi_ref[...], lam)
        gl, dgl = _gelu_and_grad(g_ref[...])
        h = h_ref[...]
        dy = dy_ref[...]
        dp_ref[:, D:] = (dy * h * dgl).astype(BF16)
        row = _iota((tb, 1), 0)
        dh = dy * gl + jnp.where(row == tb - 1, carry_ref[...], 0.0)
        b = jnp.where(row < tb - 1, pltpu.roll(a, tb - 1, 0), 0.0)
        gs = _scan_rev(b, dh)
        carry_ref[...] = a[0:1] * gs[0:1]
        h_last = jnp.where(blk == 0, 0.0, hp_ref[7:8, :])
        hprev = jnp.where(row == 0, h_last, pltpu.roll(h, 1, 0))
        da = gs * hprev
        dm = gs * ig * xc
        di = gs * m * xc
        dxc = gs * m * ig
        dlog = (0.5 * dm / m) * (-2.0 * a * a) + da * a
        dr = dlog * ((-LRU_C) * sp)
        dsp = jnp.sum(dlog * ((-LRU_C) * r), axis=0, keepdims=True)
        dlam_ref[...] += dsp * (-_sigmoid(-lam))
        dza = dr * r * (1.0 - r)
        dzi = di * ig * (1.0 - ig)
        dba_ref[...] += jnp.sum(dza, axis=0, keepdims=True)
        dbi_ref[...] += jnp.sum(dzi, axis=0, keepdims=True)
        parts = []
        for n in range(LRU_BLOCKS):
            sl = slice(n * 128, (n + 1) * 128)
            dwa_ref[n] += _dot(xc[:, sl], dza[:, sl], _TN)
            dwi_ref[n] += _dot(xc[:, sl], dzi[:, sl], _TN)
            parts.append(_dot(dza[:, sl], wa_ref[n], _NT) + _dot(dzi[:, sl], wi_ref[n], _NT))
        dxc = dxc + jnp.concatenate(parts, axis=1)
        dx, dcw, dcb = _conv_bwd(dxc, dnext_ref[...], taps, cw, tb)
        dp_ref[:, :D] = dx.astype(BF16)
        dcw_ref[...] += dcw
        dcb_ref[...] += dcb
        dnext_ref[...] = dxc[0:8]

    par = pl.BlockSpec((1, D), lambda i: (0, 0))
    wsp = pl.BlockSpec((LRU_BLOCKS, LRU_BLOCK, LRU_BLOCK), lambda i: (0, 0, 0))
    cws = pl.BlockSpec((4, D), lambda i: (0, 0))
    rev = lambda i: nb - 1 - i
    prev8 = lambda i: (jnp.maximum(rev(i) * r8 - 1, 0), 0)
    w_shape = jax.ShapeDtypeStruct((LRU_BLOCKS, LRU_BLOCK, LRU_BLOCK), F32)
    v_shape = jax.ShapeDtypeStruct((1, D), F32)
    return pl.pallas_call(
        body, name=name, grid=(nb,),
        in_specs=[pl.BlockSpec((tb, D), lambda i: (rev(i), 0)), pl.BlockSpec((8, D), prev8),
                  pl.BlockSpec((tb, D), lambda i: (rev(i), 1)),
                  pl.BlockSpec((tb, D), lambda i: (rev(i), 0)), pl.BlockSpec((8, D), prev8),
                  pl.BlockSpec((tb, D), lambda i: (rev(i), 0)),
                  cws, par, wsp, par, wsp, par, par],
        out_specs=[pl.BlockSpec((tb, 2 * D), lambda i: (rev(i), 0)), cws, par, wsp, par, wsp, par, par],
        out_shape=[jax.ShapeDtypeStruct((t, 2 * D), BF16), jax.ShapeDtypeStruct((4, D), F32), v_shape,
                   w_shape, v_shape, w_shape, v_shape, v_shape],
        scratch_shapes=[pltpu.VMEM((1, D), F32), pltpu.VMEM((8, D), F32)],
        compiler_params=_cp("arbitrary"),
    )(p, p, p, h, h, dy, cw, cb, wa, ba, wi, bi, lam)


def _ssd_consts():
    m0 = _iota((1, 128), 1) < 64
    e = (jnp.right_shift(_iota((SSD_HEADS, D_SSD), 1), 6) == _iota((SSD_HEADS, D_SSD), 0)).astype(BF16)
    tril = (_iota((CHUNK, CHUNK), 0) >= _iota((CHUNK, CHUNK), 1)).astype(F32)
    eye = (_iota((SSD_HEADS, SSD_HEADS), 0) == _iota((SSD_HEADS, SSD_HEADS), 1)).astype(F32)
    r2 = _iota((CHUNK, 128), 0)
    c2 = jnp.bitwise_and(_iota((CHUNK, 128), 1), 63)
    return dict(m0=m0, e=e, tril=tril, eye=eye, causal2=r2 >= c2, fold=(c2 == r2).astype(BF16))


def _ssd_pre(blk, p_ref, pp_ref, cw_ref, cb_ref, dtb_ref, alog_ref, dvec_ref, k):
    halo = jnp.where(blk == 0, 0.0, pp_ref[:, S_XBC:S_DT])
    taps = _conv_taps(jnp.concatenate([halo, p_ref[:, S_XBC:S_DT]], axis=0), CHUNK)
    cw = cw_ref[...]
    c = _conv_fwd(taps, cw, cb_ref[...])
    sg = _sigmoid(c)
    xbc = c * sg
    dtp = p_ref[:, S_DT:S_DT + DT_REAL] + dtb_ref[...]
    dt = _softplus(dtp)
    a = -jnp.exp(alog_ref[...])
    cs = _dot_hi(k["tril"], dt * a)
    cs_last = cs[CHUNK - 1:CHUNK]
    dend = jnp.exp(cs_last - cs)
    cdec = jnp.exp(cs_last)
    big = _dot01(jnp.concatenate([dt, jnp.exp(cs), dend], axis=0), k["e"])
    small = _dot01(jnp.concatenate([jnp.broadcast_to(cdec, (8, SSD_HEADS)),
                                    jnp.broadcast_to(dvec_ref[...], (8, SSD_HEADS))], axis=0), k["e"])
    cst2 = _dot_hi(k["eye"], jnp.concatenate([cs, cs], axis=0), _NT)
    return dict(taps=taps, cw=cw, c=c, sg=sg, xs=xbc[:, :D_SSD], bm=xbc[:, D_SSD:D_SSD + 512],
                cm=xbc[:, D_SSD + 512:], dtp=dtp, dt=dt, a=a, cs=cs, dend=dend, cdec=cdec,
                dtx=big[0:CHUNK], ecx=big[CHUNK:2 * CHUNK], dex=big[2 * CHUNK:3 * CHUNK],
                cdx=small[0:1], ddx=small[8:9], cst2=cst2)


def _pair_decay(p, cs, cst2, k):
    h0, h1 = 2 * p, 2 * p + 1
    colp = jnp.where(k["m0"], cs[:, h0:h0 + 1], cs[:, h1:h1 + 1])
    rowp = jnp.where(k["m0"], cst2[h0:h0 + 1, :], cst2[h1:h1 + 1, :])
    return jnp.where(k["causal2"], jnp.exp(colp - rowp), 0.0)


def _pair_stack(xp, k):
    return jnp.concatenate([jnp.where(k["m0"], xp, 0.0), jnp.where(k["m0"], 0.0, xp)], axis=0)


def _group_norm(yz, nw, with_stats=False):
    outs, stats = [], []
    for g in range(SSD_GROUPS):
        yzg = yz[:, g * GROUP_W:(g + 1) * GROUP_W]
        r = lax.rsqrt(jnp.mean(yzg * yzg, axis=1, keepdims=True) + EPS)
        outs.append(yzg * r)
        stats.append(r)
    y = jnp.concatenate(outs, axis=1) * nw
    return (y, stats) if with_stats else y


def _ssd_fwd(p, cw, cb, dtb, alog, dvec, nw, name):
    t = p.shape[0]
    nc = t // CHUNK

    def body(p_ref, pp_ref, cw_ref, cb_ref, dtb_ref, alog_ref, dvec_ref, nw_ref, y_ref, yraw_ref, hs_ref, h_scr):
        i = pl.program_id(0)

        @pl.when(i == 0)
        def _():
            h_scr[...] = jnp.zeros_like(h_scr)

        k = _ssd_consts()
        s = _ssd_pre(i, p_ref, pp_ref, cw_ref, cb_ref, dtb_ref, alog_ref, dvec_ref, k)
        xs, bm, cm = s["xs"], s["bm"], s["cm"]
        xdt = xs * s["dtx"]
        hprev = h_scr[...]
        hs_ref[0] = hprev
        ys, hn = [], []
        for g in range(SSD_GROUPS):
            gs = slice(g * GROUP_W, (g + 1) * GROUP_W)
            bg = bm[:, g * 128:(g + 1) * 128]
            cg = cm[:, g * 128:(g + 1) * 128]
            cbdup = _dot(cg, jnp.concatenate([bg, bg], axis=0), _NT)
            hp_g = hprev[:, gs]
            yd = []
            for q in range(4):
                pr = g * 4 + q
                mp = cbdup * _pair_decay(pr, s["cs"], s["cst2"], k)
                yd.append(_dot(mp, _pair_stack(xdt[:, pr * 128:(pr + 1) * 128], k)))
            ys.append(jnp.concatenate(yd, axis=1) + _dot(cg, hp_g) * s["ecx"][:, gs])
            hn.append(hp_g * s["cdx"][:, gs] + _dot(bg, xdt[:, gs] * s["dex"][:, gs], _TN))
        h_scr[...] = jnp.concatenate(hn, axis=1)
        yraw = jnp.concatenate(ys, axis=1) + s["ddx"] * xs
        yraw_ref[...] = yraw
        z = p_ref[:, S_Z:S_Z + D_SSD]
        y_ref[...] = _group_norm(yraw * (z * _sigmoid(z)), nw_ref[...]).astype(BF16)

    hv = pl.BlockSpec((1, DT_REAL), lambda i: (0, 0))
    return pl.pallas_call(
        body, name=name, grid=(nc,),
        in_specs=[pl.BlockSpec((CHUNK, W_SSD), lambda i: (i, 0)),
                  pl.BlockSpec((8, W_SSD), lambda i: (jnp.maximum(i * (CHUNK // 8) - 1, 0), 0)),
                  pl.BlockSpec((4, D_XBC), lambda i: (0, 0)), pl.BlockSpec((1, D_XBC), lambda i: (0, 0)),
                  hv, hv, hv, pl.BlockSpec((1, D_SSD), lambda i: (0, 0))],
        out_specs=[pl.BlockSpec((CHUNK, D_SSD), lambda i: (i, 0)), pl.BlockSpec((CHUNK, D_SSD), lambda i: (i, 0)),
                   pl.BlockSpec((1, SSD_STATE, D_SSD), lambda i: (i, 0, 0))],
        out_shape=[jax.ShapeDtypeStruct((t, D_SSD), BF16), jax.ShapeDtypeStruct((t, D_SSD), F32),
                   jax.ShapeDtypeStruct((nc, SSD_STATE, D_SSD), F32)],
        scratch_shapes=[pltpu.VMEM((SSD_STATE, D_SSD), F32)],
        compiler_params=_cp("arbitrary"),
    )(p, p, cw, cb, dtb, alog, dvec, nw)


def _ssd_bwd(p, yraw, hs, dy, cw, cb, dtb, alog, dvec, nw, name):
    t = p.shape[0]
    nc = t // CHUNK

    def body(p_ref, pp_ref, yraw_ref, hs_ref, dy_ref, cw_ref, cb_ref, dtb_ref, alog_ref, dvec_ref, nw_ref,
             dp_ref, dcw_ref, dcb_ref, ddtb_ref, dalog_ref, dd_ref, dnw_ref, dh_scr, dnext_scr):
        i = pl.program_id(0)
        blk = nc - 1 - i

        @pl.when(i == 0)
        def _():
            for r in (dcw_ref, dcb_ref, ddtb_ref, dalog_ref, dd_ref, dnw_ref, dh_scr, dnext_scr):
                r[...] = jnp.zeros_like(r)

        k = _ssd_consts()
        s = _ssd_pre(blk, p_ref, pp_ref, cw_ref, cb_ref, dtb_ref, alog_ref, dvec_ref, k)
        xs, bm, cm, cs, dt, a = s["xs"], s["bm"], s["cm"], s["cs"], s["dt"], s["a"]
        m0 = k["m0"]
        xdt = xs * s["dtx"]
        hprev = hs_ref[0]
        dh = dh_scr[...]

        nw_v = nw_ref[...]
        yraw = yraw_ref[...]
        z = p_ref[:, S_Z:S_Z + D_SSD]
        sz = _sigmoid(z)
        siluz = z * sz
        yz = yraw * siluz
        dyo = dy_ref[...]
        dyn = dyo * nw_v
        dyz_parts, dnw_parts = [], []
        for g in range(SSD_GROUPS):
            gs = slice(g * GROUP_W, (g + 1) * GROUP_W)
            yzg = yz[:, gs]
            r = lax.rsqrt(jnp.mean(yzg * yzg, axis=1, keepdims=True) + EPS)
            dnw_parts.append(jnp.sum(dyo[:, gs] * yzg * r, axis=0, keepdims=True))
            dyz_parts.append(r * dyn[:, gs] - yzg * (r * r * r) * jnp.mean(dyn[:, gs] * yzg, axis=1, keepdims=True))
        dnw_ref[...] += jnp.concatenate(dnw_parts, axis=1)
        dyz = jnp.concatenate(dyz_parts, axis=1)
        d_y = dyz * siluz
        dp_ref[:, S_Z:S_Z + D_SSD] = (dyz * yraw * (sz * (1.0 + z * (1.0 - sz)))).astype(BF16)
        dd_row = jnp.sum(d_y * xs, axis=0, keepdims=True)
        dxs = d_y * s["ddx"]

        lane_h = _iota((1, SSD_HEADS), 1)
        sub_h = _iota((SSD_HEADS, 1), 0)
        dcs = jnp.zeros((CHUNK, SSD_HEADS), F32)
        dcst2 = jnp.zeros((SSD_HEADS, 128), F32)
        dxdt_parts, db_parts, dc_parts, dhp_parts, yoff_parts, dend_parts, dcd_parts = [], [], [], [], [], [], []
        for g in range(SSD_GROUPS):
            gs = slice(g * GROUP_W, (g + 1) * GROUP_W)
            bg = bm[:, g * 128:(g + 1) * 128]
            cg = cm[:, g * 128:(g + 1) * 128]
            bdup = jnp.concatenate([bg, bg], axis=0)
            cbdup = _dot(cg, bdup, _NT)
            dcb2 = jnp.zeros((CHUNK, 128), F32)
            dxp_parts = []
            for q in range(4):
                pr = g * 4 + q
                h0, h1 = 2 * pr, 2 * pr + 1
                lp = _pair_decay(pr, cs, s["cst2"], k)
                mp = cbdup * lp
                xst = _pair_stack(xdt[:, pr * 128:(pr + 1) * 128], k)
                dyp = d_y[:, pr * 128:(pr + 1) * 128]
                dmp = _dot(dyp, xst, _NT)
                dxst = _dot(mp, dyp, _TN)
                dxp_parts.append(jnp.where(m0, dxst[:CHUNK], dxst[CHUNK:]))
                dcb2 = dcb2 + dmp * lp
                dlm = dmp * mp
                rs0 = jnp.sum(jnp.where(m0, dlm, 0.0), axis=1, keepdims=True)
                rs1 = jnp.sum(jnp.where(m0, 0.0, dlm), axis=1, keepdims=True)
                dcs = dcs + jnp.where(lane_h == h0, rs0, 0.0) + jnp.where(lane_h == h1, rs1, 0.0)
                colsum = jnp.sum(dlm, axis=0, keepdims=True)
                sel = ((sub_h == h0) & m0) | ((sub_h == h1) & jnp.logical_not(m0))
                dcst2 = dcst2 - jnp.where(sel, colsum, 0.0)
            dcg = _dot(dcb2, bdup)
            dbdup = _dot(dcb2, cg, _TN)
            dbg = dbdup[:CHUNK] + dbdup[CHUNK:]
            hp_g = hprev[:, gs]
            zoff = _dot(cg, hp_g)
            dzo = d_y[:, gs] * s["ecx"][:, gs]
            yoff_parts.append(dzo * zoff)
            dcg = dcg + _dot(dzo, hp_g, _NT)
            dh_g = dh[:, gs]
            dhp_parts.append(_dot(cg, dzo, _TN) + dh_g * s["cdx"][:, gs])
            dcd_parts.append(jnp.sum(dh_g * hp_g, axis=0, keepdims=True))
            wg = xdt[:, gs] * s["dex"][:, gs]
            dbg = dbg + _dot(wg, dh_g, _NT)
            dwg = _dot(bg, dh_g)
            dxdt_parts.append(jnp.concatenate(dxp_parts, axis=1) + dwg * s["dex"][:, gs])
            dend_parts.append(dwg * xdt[:, gs])
            db_parts.append(dbg)
            dc_parts.append(dcg)
        dh_scr[...] = jnp.concatenate(dhp_parts, axis=1)
        dxdt = jnp.concatenate(dxdt_parts, axis=1)
        sums = _dot01(jnp.concatenate([jnp.concatenate(yoff_parts, axis=1), jnp.concatenate(dend_parts, axis=1),
                                       dxdt * xs], axis=0), k["e"], _NT)
        s_dend = sums[CHUNK:2 * CHUNK] * s["dend"]
        rows8 = jnp.concatenate([jnp.broadcast_to(jnp.concatenate(dcd_parts, axis=1), (8, D_SSD)),
                                 jnp.broadcast_to(dd_row, (8, D_SSD))], axis=0)
        small = _dot01(rows8, k["e"], _NT)
        dd_ref[...] += small[8:9]
        dcs_last = small[0:1] * s["cdec"] + jnp.sum(s_dend, axis=0, keepdims=True)
        hi, lo = _split(dcst2)
        dcs = (dcs + sums[0:CHUNK] - s_dend
               + lax.dot_general(k["fold"], hi, _NT, preferred_element_type=F32)
               + lax.dot_general(k["fold"], lo, _NT, preferred_element_type=F32)
               + jnp.where(_iota((CHUNK, 1), 0) == CHUNK - 1, dcs_last, 0.0))
        dda = _dot_hi(k["tril"], dcs, _TN)
        ddt = dda * a + sums[2 * CHUNK:3 * CHUNK]
        dalog_ref[...] += jnp.sum(dda * dt, axis=0, keepdims=True) * a
        dxs = dxs + dxdt * s["dtx"]
        draw = ddt * _sigmoid(s["dtp"])
        ddtb_ref[...] += jnp.sum(draw, axis=0, keepdims=True)
        dp_ref[:, S_DT:] = jnp.zeros((CHUNK, W_SSD - S_DT), BF16)
        dp_ref[:, S_DT:S_DT + DT_REAL] = draw.astype(BF16)
        dxbc = jnp.concatenate([dxs] + db_parts + dc_parts, axis=1)
        sg, c = s["sg"], s["c"]
        dc = dxbc * (sg * (1.0 + c * (1.0 - sg)))
        dx, dcw, dcb = _conv_bwd(dc, dnext_scr[...], s["taps"], s["cw"], CHUNK)
        dp_ref[:, S_XBC:S_DT] = dx.astype(BF16)
        dcw_ref[...] += dcw
        dcb_ref[...] += dcb
        dnext_scr[...] = dc[0:8]

    rev = lambda i: nc - 1 - i
    hv = pl.BlockSpec((1, DT_REAL), lambda i: (0, 0))
    cws = pl.BlockSpec((4, D_XBC), lambda i: (0, 0))
    cbs = pl.BlockSpec((1, D_XBC), lambda i: (0, 0))
    nws = pl.BlockSpec((1, D_SSD), lambda i: (0, 0))
    wide = pl.BlockSpec((CHUNK, D_SSD), lambda i: (rev(i), 0))
    hshape = jax.ShapeDtypeStruct((1, DT_REAL), F32)
    return pl.pallas_call(
        body, name=name, grid=(nc,),
        in_specs=[pl.BlockSpec((CHUNK, W_SSD), lambda i: (rev(i), 0)),
                  pl.BlockSpec((8, W_SSD), lambda i: (jnp.maximum(rev(i) * (CHUNK // 8) - 1, 0), 0)),
                  wide, pl.BlockSpec((1, SSD_STATE, D_SSD), lambda i: (rev(i), 0, 0)), wide,
                  cws, cbs, hv, hv, hv, nws],
        out_specs=[pl.BlockSpec((CHUNK, W_SSD), lambda i: (rev(i), 0)), cws, cbs, hv, hv, hv, nws],
        out_shape=[jax.ShapeDtypeStruct((t, W_SSD), BF16), jax.ShapeDtypeStruct((4, D_XBC), F32),
                   jax.ShapeDtypeStruct((1, D_XBC), F32), hshape, hshape, hshape,
                   jax.ShapeDtypeStruct((1, D_SSD), F32)],
        scratch_shapes=[pltpu.VMEM((SSD_STATE, D_SSD), F32), pltpu.VMEM((8, D_XBC), F32)],
        compiler_params=_cp("arbitrary"),
    )(p, p, yraw, hs, dy, cw, cb, dtb, alog, dvec, nw)


def _loss_head(y, target, name, tb=512):
    t = y.shape[0]
    tb = min(tb, t)

    def body(y_ref, t_ref, dy_ref, l_ref):
        @pl.when(pl.program_id(0) == 0)
        def _():
            l_ref[...] = jnp.zeros_like(l_ref)

        e = y_ref[...] - t_ref[...]
        dy_ref[...] = e * (1.0 / D)
        l_ref[...] += jnp.sum(jnp.sum(e * e, axis=1, keepdims=True), axis=0, keepdims=True) * (0.5 / D)

    row = pl.BlockSpec((tb, D), lambda i: (i, 0))
    return pl.pallas_call(
        body, name=name, grid=(t // tb,), in_specs=[row, row],
        out_specs=[row, pl.BlockSpec((8, 128), lambda i: (0, 0))],
        out_shape=[jax.ShapeDtypeStruct((t, D), F32), jax.ShapeDtypeStruct((8, 128), F32)],
        compiler_params=_cp("arbitrary"),
    )(y, target)


def _adamw(slots, w, m, v, name, tb):
    nl = len(slots)
    ns, r, c = slots[0].shape
    assert r % tb == 0 and w.shape == (nl, r, c), (r, tb, w.shape)

    def body(*refs):
        s_refs = refs[:nl]
        w_ref, m_ref, v_ref, g_ref, d_ref, m2_ref, v2_ref = refs[nl:]

        def total(ref):
            acc = ref[0].astype(F32)
            for j in range(1, ns):
                acc = acc + ref[j].astype(F32)
            return acc

        g = total(s_refs[0])
        for layer in range(1, nl):
            g = jnp.where(pl.program_id(0) == layer, total(s_refs[layer]), g)
        m2 = ADAM_B1 * m_ref[...] + (1.0 - ADAM_B1) * g
        v2 = ADAM_B2 * v_ref[...] + (1.0 - ADAM_B2) * (g * g)
        m_hat = m2 / (1.0 - ADAM_B1 ** ADAM_STEP)
        v_hat = v2 / (1.0 - ADAM_B2 ** ADAM_STEP)
        g_ref[...] = g
        d_ref[...] = -ADAM_LR * (m_hat / (jnp.sqrt(v_hat) + ADAM_EPS) + ADAM_WD * w_ref[...])
        m2_ref[...] = m2
        v2_ref[...] = v2

    def slot_spec(layer):
        return pl.BlockSpec((ns, tb, c), lambda l, i: (0, jnp.where(l == layer, i, 0), 0))

    row = pl.BlockSpec((None, tb, c), lambda l, i: (l, i, 0))
    shp = jax.ShapeDtypeStruct((nl, r, c), F32)
    return pl.pallas_call(
        body, name=name, grid=(nl, r // tb),
        in_specs=[slot_spec(layer) for layer in range(nl)] + [row, row, row],
        out_specs=[row, row, row, row], out_shape=[shp, shp, shp, shp], compiler_params=_cp("arbitrary", "arbitrary"),
    )(*slots, w, m, v)


def _pair_sum(own, got, name, out_dtype, tb):
    nj, _, r, c = own.shape
    mc = lax.axis_index("c")

    def body(mc_ref, a_ref, b_ref, o_ref):
        del mc_ref
        o_ref[...] = (a_ref[...] + b_ref[...]).astype(out_dtype)

    return pl.pallas_call(
        body, name=name,
        grid_spec=pltpu.PrefetchScalarGridSpec(
            num_scalar_prefetch=1, grid=(nj, r // tb),
            in_specs=[pl.BlockSpec((None, None, tb, c), lambda j, i, mc_ref: (j, mc_ref[0], i, 0)),
                      pl.BlockSpec((None, tb, c), lambda j, i, mc_ref: (j, i, 0))],
            out_specs=pl.BlockSpec((None, tb, c), lambda j, i, mc_ref: (j, i, 0))),
        out_shape=jax.ShapeDtypeStruct((nj, r, c), out_dtype), compiler_params=_cp("parallel", "parallel"),
    )(jnp.reshape(mc, (1,)).astype(jnp.int32), own, got)


def _slot_sum(slots, name):
    ns, r, c = slots.shape

    def body(s_ref, o_ref):
        g = s_ref[0]
        for j in range(1, ns):
            g = g + s_ref[j]
        o_ref[...] = g

    return pl.pallas_call(body, name=name, out_shape=jax.ShapeDtypeStruct((r, c), F32))(slots)


def _position():
    return lax.axis_index("x"), lax.axis_index("y"), lax.axis_index("c")


def _all_gather(xs, name):
    n = len(xs)

    def body(*refs):
        x_refs, out_refs = refs[:n], refs[n:2 * n]
        send_sems, recv_sems, local_sems = refs[2 * n:]
        mx, my, mc = _position()
        me, sibling = (mx, my, mc), (mx, my, 1 - mc)
        chips = [(1 - mx, my), (mx, 1 - my), (1 - mx, 1 - my)]

        def copy(a, k, block, to, own=False):
            dst = out_refs[a].at[4 * block[0] + 2 * block[1] + block[2]]
            return pltpu.make_async_remote_copy(
                src_ref=x_refs[a] if own else dst, dst_ref=dst,
                send_sem=send_sems.at[a, k], recv_sem=recv_sems.at[a, k], device_id=to, device_id_type=MESH)

        mine = [pltpu.make_async_copy(x_refs[a], out_refs[a].at[4 * mx + 2 * my + mc], local_sems.at[a])
                for a in range(n)]
        first = [copy(a, 1 + j, me, (*chip, mc), own=True) for j, chip in enumerate(chips) for a in range(n)]
        first += [copy(a, 0, me, sibling, own=True) for a in range(n)]
        for cp in first + mine:
            cp.start()
        passed = []
        for j, chip in enumerate(chips):
            for a in range(n):
                copy(a, 1 + j, (*chip, mc), me).wait_recv()
                passed.append(copy(a, 4 + j, (*chip, mc), sibling))
                passed[-1].start()
        for a in range(n):
            copy(a, 0, sibling, me).wait_recv()
        for j, chip in enumerate(chips):
            for a in range(n):
                copy(a, 4 + j, (*chip, 1 - mc), me).wait_recv()
        for cp in first + passed:
            cp.wait_send()
        for cp in mine:
            cp.wait()

    return pl.pallas_call(
        body, name=name, in_specs=[ANY] * n, out_specs=[ANY] * n,
        out_shape=[jax.ShapeDtypeStruct((N_DEV,) + x.shape, x.dtype) for x in xs],
        scratch_shapes=[pltpu.SemaphoreType.DMA((n, 7)), pltpu.SemaphoreType.DMA((n, 7)), pltpu.SemaphoreType.DMA((n,))],
    )(*xs)


def _exchange_sibling(gs, name):
    n = len(gs)

    def body(*refs):
        g_refs, r_refs = refs[:n], refs[n:2 * n]
        send_sems, recv_sems = refs[2 * n:]
        mx, my, mc = _position()
        cps = [pltpu.make_async_remote_copy(src_ref=g_refs[a].at[:, 1 - mc], dst_ref=r_refs[a],
                                            send_sem=send_sems.at[a], recv_sem=recv_sems.at[a],
                                            device_id=(mx, my, 1 - mc), device_id_type=MESH) for a in range(n)]
        for cp in cps:
            cp.start()
        for cp in cps:
            cp.wait()

    return pl.pallas_call(
        body, name=name, in_specs=[ANY] * n, out_specs=[ANY] * n,
        out_shape=[jax.ShapeDtypeStruct(g.shape[:1] + g.shape[2:], g.dtype) for g in gs],
        scratch_shapes=[pltpu.SemaphoreType.DMA((n,)), pltpu.SemaphoreType.DMA((n,))],
    )(*gs)


def _exchange_chips(ss, name):
    n = len(ss)

    def body(*refs):
        s_refs, r_refs = refs[:n], refs[n:2 * n]
        send_sems, recv_sems, local_sems = refs[2 * n:]
        mx, my, mc = _position()
        my_chip = 2 * mx + my
        chips = [(1 - mx, my), (mx, 1 - my), (1 - mx, 1 - my)]

        def copy(a, k, to_slot):
            px, py = chips[k]
            return pltpu.make_async_remote_copy(
                src_ref=s_refs[a].at[2 * px + py], dst_ref=r_refs[a].at[to_slot], send_sem=send_sems.at[a, k],
                recv_sem=recv_sems.at[a, k], device_id=(px, py, mc), device_id_type=MESH)

        sends = [copy(a, k, my_chip) for k in range(3) for a in range(n)]
        local = [pltpu.make_async_copy(s_refs[a].at[my_chip], r_refs[a].at[my_chip], local_sems.at[a])
                 for a in range(n)]
        for cp in sends + local:
            cp.start()
        for k in range(3):
            px, py = chips[k]
            for a in range(n):
                copy(a, k, 2 * px + py).wait_recv()
        for cp in sends:
            cp.wait_send()
        for cp in local:
            cp.wait()

    return pl.pallas_call(
        body, name=name, in_specs=[ANY] * n, out_specs=[ANY] * n,
        out_shape=[jax.ShapeDtypeStruct(s.shape, s.dtype) for s in ss],
        scratch_shapes=[pltpu.SemaphoreType.DMA((n, 3)), pltpu.SemaphoreType.DMA((n, 3)), pltpu.SemaphoreType.DMA((n,))],
    )(*ss)


def _cols_concat(g, l, name, tb=128):
    _, _, k_dim, n = g.shape

    def body(g_ref, o_ref):
        o_ref[...] = jnp.concatenate([g_ref[d] for d in range(N_DEV)], axis=1)

    return pl.pallas_call(
        body, name=name, grid=(k_dim // tb,),
        in_specs=[pl.BlockSpec((N_DEV, None, tb, n), lambda i: (0, l, i, 0))],
        out_specs=pl.BlockSpec((tb, N_DEV * n), lambda i: (i, 0)),
        out_shape=jax.ShapeDtypeStruct((k_dim, N_DEV * n), g.dtype), compiler_params=_cp("parallel"),
    )(g)


def _cols_split(dw, name, tb=128):
    k_dim, n8 = dw.shape
    n = n8 // N_DEV

    def body(g_ref, o_ref):
        full = g_ref[...]
        for d in range(N_DEV):
            o_ref[d] = full[:, d * n:(d + 1) * n]

    return pl.pallas_call(
        body, name=name, grid=(k_dim // tb,),
        in_specs=[pl.BlockSpec((tb, n8), lambda i: (i, 0))],
        out_specs=pl.BlockSpec((N_DEV, tb, n), lambda i: (0, i, 0)),
        out_shape=jax.ShapeDtypeStruct((N_DEV, k_dim, n), dw.dtype), compiler_params=_cp("parallel"),
    )(dw)


_Q0, _GL0 = 7200, 8224
N_SHARD_IN = N_IN // N_DEV


def _w_in_regions(g, l, name, tb=128):
    def body(g_ref, ssd_ref, lru_ref, q_ref, gl_ref):
        full = jnp.concatenate([g_ref[d] for d in range(N_DEV)], axis=1)
        lru_ref[...] = full[:, 0:2 * D]
        ssd_ref[:, :S_DT] = full[:, 2 * D:2 * D + S_DT]
        ssd_ref[:, S_DT:] = jnp.zeros((tb, W_SSD - S_DT), g.dtype)
        ssd_ref[:, S_DT:S_DT + DT_REAL] = full[:, 2 * D + S_DT:_Q0]
        q_ref[...] = full[:, _Q0:_GL0]
        gl_ref[...] = full[:, _GL0:N_IN]

    widths = (W_SSD, 2 * D, D, 3 * D)
    return pl.pallas_call(
        body, name=name, grid=(D // tb,),
        in_specs=[pl.BlockSpec((N_DEV, None, tb, N_SHARD_IN), lambda i: (0, l, i, 0))],
        out_specs=[pl.BlockSpec((tb, wd), lambda i: (i, 0)) for wd in widths],
        out_shape=[jax.ShapeDtypeStruct((D, wd), g.dtype) for wd in widths], compiler_params=_cp("parallel"),
    )(g)


def _w_in_shards(dssd, dlru, dq, dgl, name, tb=128):
    def body(ssd_ref, lru_ref, q_ref, gl_ref, o_ref):
        full = jnp.concatenate([lru_ref[...], ssd_ref[:, :S_DT + DT_REAL], q_ref[...], gl_ref[...]], axis=1)
        for d in range(N_DEV):
            o_ref[d] = full[:, d * N_SHARD_IN:(d + 1) * N_SHARD_IN]

    return pl.pallas_call(
        body, name=name, grid=(D // tb,),
        in_specs=[pl.BlockSpec((tb, a.shape[1]), lambda i: (i, 0)) for a in (dssd, dlru, dq, dgl)],
        out_specs=pl.BlockSpec((N_DEV, tb, N_SHARD_IN), lambda i: (0, i, 0)),
        out_shape=jax.ShapeDtypeStruct((N_DEV, D, N_SHARD_IN), F32), compiler_params=_cp("parallel"),
    )(dssd, dlru, dq, dgl)


_BIG = (("w_in", "col", (1024, 1412)), ("mem_w_kv", "col", (1024, 256)), ("w_br_lru", "row", (128, 1024)),
        ("w_br_ssd", "row", (256, 1024)), ("w_br_xa", "row", (128, 1024)), ("w_out", "row", (128, 1024)),
        ("ffn_w_in", "col", (1024, 704)), ("ffn_w_down", "row", (352, 1024)))
_SMALL = (("b_gate", (3, 128)), ("lru_conv_w", (4, 128)), ("ssd_conv_w", (4, 384)))
_REP = (("lru_conv_b", (1024,)), ("lru_w_a", (8, 128, 128)), ("lru_b_a", (1024,)), ("lru_w_i", (8, 128, 128)),
        ("lru_b_i", (1024,)), ("lru_lambda", (1024,)), ("ssd_conv_b", (3072,)), ("ssd_dt_bias", (32,)),
        ("ssd_a_log", (32,)), ("ssd_d", (32,)), ("ssd_norm_w", (2048,)), ("ln1_g", (1024,)), ("ln1_b", (1024,)),
        ("ln2_g", (1024,)), ("ln2_b", (1024,)))
_ORDER = ("w_in", "b_gate", "lru_conv_w", "lru_conv_b", "lru_w_a", "lru_b_a", "lru_w_i", "lru_b_i", "lru_lambda",
          "ssd_conv_w", "ssd_conv_b", "ssd_dt_bias", "ssd_a_log", "ssd_d", "ssd_norm_w", "mem_w_kv", "w_br_lru",
          "w_br_ssd", "w_br_xa", "w_out", "ln1_g", "ln1_b", "ffn_w_in", "ffn_w_down", "ln2_g", "ln2_b")

LANES = 1024
N_SMALL = sum(DEPTH * s[0] * s[1] for _, s in _SMALL)
R_SMALL = 8
N_REP = sum(DEPTH * math.prod(s) for _, s in _REP)
R_REP = 68
R_SM = R_SMALL + R_REP + 4
R_TAIL = R_SMALL + N_DEV * R_REP
TB_TAIL = 184
assert N_SMALL <= R_SMALL * LANES and N_REP <= N_DEV * R_REP * LANES


def _rows(flat, rows):
    return jnp.pad(flat, (0, rows * LANES - flat.shape[0])).reshape(rows, LANES)


def _rowblk(a, cap):
    return max(b for b in range(16, cap + 1, 16) if a % b == 0)


def _pack_tail(d):
    small = jnp.concatenate([d[n].reshape(-1) for n, _ in _SMALL])
    rep = jnp.concatenate([d[n].reshape(-1) for n, _ in _REP])
    return jnp.concatenate([_rows(small, R_SMALL), _rows(rep, N_DEV * R_REP)], axis=0)


def _unpack_tail(a):
    out, o = {}, 0
    flat = a[:R_SMALL].reshape(-1)
    for n, s in _SMALL:
        k = DEPTH * math.prod(s)
        out[n] = flat[o:o + k].reshape((DEPTH,) + s)
        o += k
    flat, o = a[R_SMALL:].reshape(-1), 0
    for n, s in _REP:
        k = DEPTH * math.prod(s)
        out[n] = flat[o:o + k].reshape((DEPTH,) + s)
        o += k
    return out


def _by_dest(g):
    g = g.reshape(g.shape[:-1] + (N_DEV, g.shape[-1] // N_DEV))
    return jnp.moveaxis(g, -2, 0).reshape(N_DEV, -1)


def _from_stack(st):
    st = jnp.moveaxis(st, 0, -2)
    return st.reshape(st.shape[:-2] + (st.shape[-2] * st.shape[-1],))


def _layer_fwd(x, mem, w, l):
    nm = lambda s: f"{s}_l{l}"
    wi = w["wi"]
    row = lambda v: v.reshape(1, -1)
    s = dict(x=x, wi=wi)
    s["p_ssd"] = _mm(x, wi["ssd"], name=nm("proj_ssd"))
    s["p_lru"] = _mm(x, wi["lru"], name=nm("proj_lru"))
    s["p_q"] = _mm(x, wi["q"], name=nm("proj_q"))
    s["p_gl"] = _mm(x, wi["gl"], name=nm("proj_gl"))
    s["lru_par"] = (w["lru_conv_w"], row(w["lru_conv_b"]), w["lru_w_a"], row(w["lru_b_a"]), w["lru_w_i"],
                    row(w["lru_b_i"]), row(w["lru_lambda"]))
    s["y_lru"], s["h"] = _lru_fwd(s["p_lru"], *s["lru_par"], name=nm("lru_fwd"))
    s["ssd_par"] = (w["ssd_conv_w"], row(w["ssd_conv_b"]), row(w["ssd_dt_bias"]), row(w["ssd_a_log"]),
                    row(w["ssd_d"]), row(w["ssd_norm_w"]))
    s["y_ssd"], s["yraw"], s["hs"] = _ssd_fwd(s["p_ssd"], *s["ssd_par"], name=nm("ssd_fwd"))
    s["kv"] = _mm(mem, w["mem_w_kv"], name=nm("kv"))
    s["y_xa"] = _xa_fwd(s["p_q"], s["kv"], name=nm("xa_fwd"))
    s["b1"] = _mm(s["y_lru"], w["w_br_lru"], name=nm("br_lru"))
    s["b2"] = _mm(s["y_ssd"], w["w_br_ssd"], name=nm("br_ssd"))
    s["b3"] = _mm(s["y_xa"], w["w_br_xa"], name=nm("br_xa"))
    s["bg"] = row(w["b_gate"])
    s["merged"] = _merge_fwd(s["p_gl"], s["bg"], s["b1"], s["b2"], s["b3"], name=nm("merge_fwd"))
    s["mix"] = _mm(s["merged"], w["w_out"], name=nm("out_proj"))
    s["x1"] = _ln_fwd(x, s["mix"], row(w["ln1_g"]), row(w["ln1_b"]), name=nm("ln1_fwd"))
    s["gu"] = _mm(s["x1"], w["ffn_w_in"], name=nm("ffn_in"))
    s["act"] = _swiglu_fwd(s["gu"], name=nm("swiglu_fwd"))
    s["f"] = _mm(s["act"], w["ffn_w_down"], name=nm("ffn_down"))
    s["x2"] = _ln_fwd(s["x1"], s["f"], row(w["ln2_g"]), row(w["ln2_b"]), name=nm("ln2_fwd"))
    return s


def _layer_bwd(s, mem, w, dxo, l):
    nm = lambda t: f"{t}_l{l}"
    row = lambda v: v.reshape(1, -1)
    slabs = lambda a: a.reshape(N_DEV, a.shape[0] // N_DEV, a.shape[1])
    g = {}
    du2, dg, db = _ln_bwd(s["x1"], s["f"], dxo, row(w["ln2_g"]), name=nm("ln2_bwd"))
    g["ln2_g"], g["ln2_b"] = dg[0], db[0]
    dact = _mm(du2, w["ffn_w_down"], tb=True, name=nm("d_act"))
    g["ffn_w_down"] = slabs(_mm(s["act"], du2, ta=True, name=nm("dw_ffn_down")))
    dgu = _swiglu_bwd(s["gu"], dact, name=nm("swiglu_bwd"))
    dx1 = _mm(dgu, w["ffn_w_in"], tb=True, add=du2, add_scale=ALPHA, name=nm("d_x1"))
    g["ffn_w_in"] = _cols_split(_mm(s["x1"], dgu, ta=True, name=nm("dw_ffn_in")), name=nm("dw_ffn_in_shards"))
    du1, dg, db = _ln_bwd(s["x"], s["mix"], dx1, row(w["ln1_g"]), name=nm("ln1_bwd"))
    g["ln1_g"], g["ln1_b"] = dg[0], db[0]
    dmerged = _mm(du1, w["w_out"], tb=True, name=nm("d_merged"))
    g["w_out"] = slabs(_mm(s["merged"], du1, ta=True, name=nm("dw_out")))
    dp_gl, d1, d2, d3, dbg = _merge_bwd(s["p_gl"], s["bg"], s["b1"], s["b2"], s["b3"], dmerged, name=nm("merge_bwd"))
    g["b_gate"] = dbg.reshape(3, D)
    dy_lru = _mm(d1, w["w_br_lru"], tb=True, name=nm("d_y_lru"))
    g["w_br_lru"] = slabs(_mm(s["y_lru"], d1, ta=True, name=nm("dw_br_lru")))
    dy_ssd = _mm(d2, w["w_br_ssd"], tb=True, name=nm("d_y_ssd"))
    g["w_br_ssd"] = slabs(_mm(s["y_ssd"], d2, ta=True, name=nm("dw_br_ssd")))
    dy_xa = _mm(d3, w["w_br_xa"], tb=True, name=nm("d_y_xa"))
    g["w_br_xa"] = slabs(_mm(s["y_xa"], d3, ta=True, name=nm("dw_br_xa")))
    dp_q, dkv = _xa_bwd(s["p_q"], s["kv"], dy_xa, name=nm("xa_bwd"))
    g["mem_w_kv"] = _mm(mem, dkv, ta=True, tn=2 * D // N_DEV, split_n=2 * D // N_DEV, name=nm("dw_kv"))
    dp_ssd, dcw, dcb, ddtb, dalog, dd, dnw = _ssd_bwd(s["p_ssd"], s["yraw"], s["hs"], dy_ssd, *s["ssd_par"],
                                                      name=nm("ssd_bwd"))
    g["ssd_conv_w"], g["ssd_conv_b"], g["ssd_dt_bias"] = dcw, dcb[0], ddtb[0]
    g["ssd_a_log"], g["ssd_d"], g["ssd_norm_w"] = dalog[0], dd[0], dnw[0]
    dp_lru, dcw, dcb, dwa, dba, dwi, dbi, dlam = _lru_bwd(s["p_lru"], s["h"], dy_lru, *s["lru_par"], name=nm("lru_bwd"))
    g["lru_conv_w"], g["lru_conv_b"], g["lru_w_a"], g["lru_b_a"] = dcw, dcb[0], dwa, dba[0]
    g["lru_w_i"], g["lru_b_i"], g["lru_lambda"] = dwi, dbi[0], dlam[0]
    wi = s["wi"]
    dx = _mm(dp_ssd, wi["ssd"], tb=True, add=du1, add_scale=ALPHA, name=nm("dx_ssd"))
    dx = _mm(dp_lru, wi["lru"], tb=True, add=dx, name=nm("dx_lru"))
    dx = _mm(dp_q, wi["q"], tb=True, add=dx, name=nm("dx_q"))
    dx = _mm(dp_gl, wi["gl"], tb=True, add=dx, name=nm("dx_gl"))
    x = s["x"]
    g["w_in"] = _w_in_shards(_mm(x, dp_ssd, ta=True, name=nm("dw_in_ssd")), _mm(x, dp_lru, ta=True, name=nm("dw_in_lru")),
                             _mm(x, dp_q, ta=True, name=nm("dw_in_q")), _mm(x, dp_gl, ta=True, name=nm("dw_in_gl")),
                             name=nm("dw_in_shards"))
    return dx, g


def _local_step(x, mem, target, layers):
    saved = []
    for l in range(DEPTH):
        saved.append(_layer_fwd(x, mem, layers[l], l))
        x = saved[-1]["x2"]
    dx, loss = _loss_head(x, target, name="loss_head")
    grads = [None] * DEPTH
    for l in reversed(range(DEPTH)):
        dx, grads[l] = _layer_bwd(saved[l], mem, layers[l], dx, l)
    return loss, dx, grads


def kernel(x, mem, w_in, b_gate, lru_conv_w, lru_conv_b, lru_w_a, lru_b_a, lru_w_i, lru_b_i, lru_lambda, ssd_conv_w, ssd_conv_b, ssd_dt_bias, ssd_a_log, ssd_d, ssd_norm_w, mem_w_kv, w_br_lru, w_br_ssd, w_br_xa, w_out, ln1_g, ln1_b, ffn_w_in, ffn_w_down, ln2_g, ln2_b, loss_target, m_w_in, m_b_gate, m_lru_conv_w, m_lru_conv_b, m_lru_w_a, m_lru_b_a, m_lru_w_i, m_lru_b_i, m_lru_lambda, m_ssd_conv_w, m_ssd_conv_b, m_ssd_dt_bias, m_ssd_a_log, m_ssd_d, m_ssd_norm_w, m_mem_w_kv, m_w_br_lru, m_w_br_ssd, m_w_br_xa, m_w_out, m_ln1_g, m_ln1_b, m_ffn_w_in, m_ffn_w_down, m_ln2_g, m_ln2_b, v_w_in, v_b_gate, v_lru_conv_w, v_lru_conv_b, v_lru_w_a, v_lru_b_a, v_lru_w_i, v_lru_b_i, v_lru_lambda, v_ssd_conv_w, v_ssd_conv_b, v_ssd_dt_bias, v_ssd_a_log, v_ssd_d, v_ssd_norm_w, v_mem_w_kv, v_w_br_lru, v_w_br_ssd, v_w_br_xa, v_w_out, v_ln1_g, v_ln1_b, v_ffn_w_in, v_ffn_w_down, v_ln2_g, v_ln2_b):
    local = dict(locals())
    w = {n: local[n] for n in _ORDER}
    m = {n: local["m_" + n] for n in _ORDER}
    v = {n: local["v_" + n] for n in _ORDER}

    big = [n for n, _, _ in _BIG]
    kinds = {n: kind for n, kind, _ in _BIG}

    small = _rows(jnp.concatenate([w[n].reshape(-1) for n, _ in _SMALL]), R_SMALL)
    gathered = _all_gather([w[n].astype(BF16) for n in big] + [small], name="gather_weights")
    stacks = dict(zip(big, gathered[:-1]))
    small_all, o, small_full = gathered[-1].reshape(N_DEV, R_SMALL * LANES), 0, {}
    for n, s in _SMALL:
        k = DEPTH * s[0] * s[1]
        small_full[n] = _from_stack(small_all[:, o:o + k].reshape((N_DEV, DEPTH) + s))
        o += k
    layers = []
    for l in range(DEPTH):
        lw = {n: w[n][l] for n, _ in _REP}
        lw.update({n: small_full[n][l] for n, _ in _SMALL})
        lw["wi"] = dict(zip(("ssd", "lru", "q", "gl"), _w_in_regions(stacks["w_in"], l, name=f"w_in_regions_l{l}")))
        for n in big[1:]:
            if kinds[n] == "col":
                lw[n] = _cols_concat(stacks[n], l, name=f"full_{n}_l{l}")
            else:
                lw[n] = stacks[n][:, l].reshape(-1, stacks[n].shape[-1])
        layers.append(lw)

    loss_tile, dx, grads = _local_step(x[0], mem[0], loss_target[0], layers)
    loss = lax.psum(loss_tile[0, 0], ("x", "y", "c"))

    stacked = {n: jnp.stack([grads[l][n] for l in range(DEPTH)]) for n in [s[0] for s in _SMALL + _REP]}
    sm = jnp.concatenate([_by_dest(stacked[n]) for n, _ in _SMALL], axis=1)
    sm = jnp.pad(sm, ((0, 0), (0, R_SMALL * LANES - sm.shape[1])))
    rep = jnp.concatenate([stacked[n].reshape(-1) for n, _ in _REP])
    rep = jnp.pad(rep, (0, N_DEV * R_REP * LANES - rep.shape[0])).reshape(N_DEV, R_REP * LANES)
    tail = jnp.concatenate([sm, rep, jnp.zeros((N_DEV, (R_SM - R_SMALL - R_REP) * LANES), F32)], axis=1)
    owns = [grads[l][n].reshape((4, 2) + grads[l][n].shape[1:]) for l in range(DEPTH) for n in big]
    owns.append(tail.reshape(4, 2, R_SM, LANES))
    gots = _exchange_sibling(owns, name="reduce_cores")
    sums = [_pair_sum(own, got, name=f"pair_sum_{i}", out_dtype=BF16, tb=_rowblk(own.shape[2], 256))
            for i, (own, got) in enumerate(zip(owns[:-1], gots[:-1]))]
    sums.append(_pair_sum(owns[-1], gots[-1], name="pair_sum_tail", out_dtype=F32, tb=R_SM))
    slots = _exchange_chips(sums, name="reduce_chips")

    res = {}
    for i, n in enumerate(big):
        tb = _rowblk(w[n].shape[1], 128 if w[n].shape[2] > LANES else 256)
        res[n] = _adamw([slots[i], slots[len(big) + i]], w[n], m[n], v[n], name=f"adamw_{n}", tb=tb)
    tail_sum = _slot_sum(slots[-1], name="sum_tail")
    rep_all = _all_gather([tail_sum[R_SMALL:R_SMALL + R_REP]], name="gather_replicated")[0]
    g_tail = jnp.concatenate([tail_sum[:R_SMALL], rep_all.reshape(N_DEV * R_REP, LANES)], axis=0)
    tails = _adamw([g_tail[None]], _pack_tail(w)[None], _pack_tail(m)[None], _pack_tail(v)[None],
                   name="adamw_tail", tb=TB_TAIL)

    outs = []
    for kind in range(4):
        d = {**{n: res[n][kind] for n in big}, **_unpack_tail(tails[kind][0])}
        outs += [d[n] for n in _ORDER]
    return (loss, dx[None], *outs)
```

```python
import math

import jax
import jax.numpy as jnp
from jax import lax
from jax.experimental import pallas as pl
from jax.experimental.pallas import tpu as pltpu

F32 = jnp.float32
BF16 = jnp.bfloat16

D = 1024
DEPTH = 2
N_DEV = 8
CHUNK = 64
LRU_BLOCKS = 8
LRU_BLOCK = 128
LRU_C = 8.0
D_SSD = 2 * D
SSD_HEADS = 32
SSD_GROUPS = 4
GROUP_W = D_SSD // SSD_GROUPS
SSD_STATE = 128
D_XBC = D_SSD + 2 * SSD_GROUPS * SSD_STATE
XA_HEADS = 4
XA_HEAD_DIM = 256
D_FF = 2816
ALPHA = (2 * DEPTH) ** 0.25
EPS = 1e-5
N_IN = 11296

S_Z, S_XBC, S_DT, W_SSD = 0, 2048, 5120, 5632
DT_REAL = 32

ADAM_LR, ADAM_B1, ADAM_B2, ADAM_EPS, ADAM_WD, ADAM_STEP = 0.001, 0.9, 0.999, 1e-08, 0.01, 10

VMEM_LIMIT = 56 * 1024 * 1024
MESH = pl.DeviceIdType.MESH
ANY = pl.BlockSpec(memory_space=pl.ANY)


def _cp(*sem):
    return pltpu.CompilerParams(dimension_semantics=sem, vmem_limit_bytes=VMEM_LIMIT)


def _blk(n, target):
    if n % 128:
        return n
    best = 128
    for b in range(128, min(n, target) + 1, 128):
        if n % b == 0:
            best = b
    return best


def _iota(shape, dim):
    return lax.broadcasted_iota(jnp.int32, shape, dim)


def _sigmoid(x):
    return 1.0 / (1.0 + jnp.exp(-x))


def _log1p(e):
    u = 1.0 + e
    return jnp.where(u == 1.0, e, jnp.log(u) * (e / (u - 1.0)))


def _softplus(x):
    return jnp.maximum(x, 0.0) + _log1p(jnp.exp(-jnp.abs(x)))


def _expm1(x):
    p = 1.0 + x * (1.0 / 7.0)
    for k in (6.0, 5.0, 4.0, 3.0, 2.0):
        p = 1.0 + (x * (1.0 / k)) * p
    return jnp.where(jnp.abs(x) < 0.25, x * p, jnp.exp(x) - 1.0)


_G0 = math.sqrt(2.0 / math.pi)
_G1 = 0.044715


def _gelu_and_grad(x):
    t = jnp.tanh(_G0 * (x + _G1 * x * x * x))
    g = 0.5 * x * (1.0 + t)
    dg = 0.5 * (1.0 + t) + 0.5 * x * (1.0 - t * t) * (_G0 * (1.0 + 3.0 * _G1 * x * x))
    return g, dg


_NN = (((1,), (0,)), ((), ()))
_NT = (((1,), (1,)), ((), ()))
_TN = (((0,), (0,)), ((), ()))


def _dot(a, b, dims=_NN):
    return lax.dot_general(a.astype(BF16), b.astype(BF16), dims, preferred_element_type=F32)


def _dot_hi(a, b, dims=_NN):
    return lax.dot_general(a, b, dims, precision=lax.Precision.HIGHEST, preferred_element_type=F32)


def _split(v):
    hi = v.astype(BF16)
    return hi, (v - hi.astype(F32)).astype(BF16)


def _dot01(v, e, dims=_NN):
    hi, lo = _split(v)
    return (lax.dot_general(hi, e, dims, preferred_element_type=F32)
            + lax.dot_general(lo, e, dims, preferred_element_type=F32))


def _conv_taps(xe, n):
    return [xe[8:8 + n] if j == 3 else pltpu.roll(xe, 3 - j, 0)[8:8 + n] for j in range(4)]


def _conv_fwd(taps, cw, cb):
    return cb + cw[0:1] * taps[0] + cw[1:2] * taps[1] + cw[2:3] * taps[2] + cw[3:4] * taps[3]


def _conv_bwd(dc, dnext, taps, cw, n):
    ext = jnp.concatenate([dc, dnext], axis=0)
    dx = cw[3:4] * dc
    for j in range(3):
        dx = dx + cw[j:j + 1] * pltpu.roll(ext, n + 8 - (3 - j), 0)[0:n]
    dcw = jnp.concatenate([jnp.sum(dc * taps[j], axis=0, keepdims=True) for j in range(4)], axis=0)
    return dx, dcw, jnp.sum(dc, axis=0, keepdims=True)


MM_VMEM_BUDGET = 44 * 1024 * 1024
MM_MAX_TILE = 1408
MM_MAX_K = 5632


def _divisors(n, cap):
    return [n] if n % 128 else [b for b in range(128, min(n, cap) + 1, 128) if n % b == 0]


def _mm_tiles(m_dim, n_dim, k_dim, a_bytes, b_bytes, o_bytes, has_add, tn_fixed):
    best = None
    for tm in _divisors(m_dim, MM_MAX_TILE):
        for tn in ([tn_fixed] if tn_fixed else _divisors(n_dim, MM_MAX_TILE)):
            for tk in _divisors(k_dim, MM_MAX_K):
                vmem = 2 * (tm * tk * a_bytes + tk * tn * b_bytes + tm * tn * (o_bytes + (4 if has_add else 0)))
                vmem += tm * tn * 4 if tk < k_dim else 0
                if vmem <= MM_VMEM_BUDGET:
                    key = (tm * tn * tk, tk, tn)
                    if best is None or key > best[0]:
                        best = (key, (tm, tn, tk))
    assert best is not None, (m_dim, n_dim, k_dim)
    return best[1]


def _mm(a, b, *, ta=False, tb=False, out_dtype=F32, add=None, add_scale=1.0, name, split_n=None):
    if ta:
        k_dim, m_dim = a.shape
    else:
        m_dim, k_dim = a.shape
    if tb:
        n_dim, k2 = b.shape
    else:
        k2, n_dim = b.shape
    assert k_dim == k2, (a.shape, b.shape, ta, tb)
    tm, tn, tk = _mm_tiles(m_dim, n_dim, k_dim, a.dtype.itemsize, b.dtype.itemsize, jnp.dtype(out_dtype).itemsize,
                           add is not None, split_n)
    nk = k_dim // tk
    a_spec = pl.BlockSpec((tk, tm), lambda i, j, k: (k, i)) if ta else pl.BlockSpec((tm, tk), lambda i, j, k: (i, k))
    b_spec = pl.BlockSpec((tn, tk), lambda i, j, k: (j, k)) if tb else pl.BlockSpec((tk, tn), lambda i, j, k: (k, j))
    o_spec = pl.BlockSpec((tm, tn), lambda i, j, k: (i, j))
    out_shape = (m_dim, n_dim)
    if split_n is not None:
        assert add is None and tn == split_n, (tn, split_n)
        o_spec = pl.BlockSpec((None, tm, tn), lambda i, j, k: (j, i, 0))
        out_shape = (n_dim // tn, m_dim, tn)
    dims = (((0 if ta else 1,), (1 if tb else 0,)), ((), ()))
    has_add = add is not None

    def body(*refs):
        a_ref, b_ref = refs[:2]
        add_ref = refs[2] if has_add else None
        o_ref = refs[3] if has_add else refs[2]
        acc_ref = refs[-1] if nk > 1 else None
        k = pl.program_id(2)

        def product():
            return lax.dot_general(a_ref[...].astype(BF16), b_ref[...].astype(BF16), dims, preferred_element_type=F32)

        def finish(r):
            if has_add:
                r = r + add_scale * add_ref[...]
            o_ref[...] = r.astype(out_dtype)

        if nk == 1:
            finish(product())
            return

        @pl.when(k == 0)
        def _():
            acc_ref[...] = product()

        @pl.when((k > 0) & (k < nk - 1))
        def _():
            acc_ref[...] += product()

        @pl.when(k == nk - 1)
        def _():
            finish(acc_ref[...] + product())

    in_specs = [a_spec, b_spec] + ([o_spec] if has_add else [])
    args = (a, b) + ((add,) if has_add else ())
    return pl.pallas_call(
        body, name=name, grid=(m_dim // tm, n_dim // tn, nk),
        in_specs=in_specs, out_specs=o_spec,
        out_shape=jax.ShapeDtypeStruct(out_shape, out_dtype),
        scratch_shapes=[pltpu.VMEM((tm, tn), F32)] if nk > 1 else [],
        compiler_params=_cp("parallel", "parallel", "arbitrary"),
    )(*args)


def _ln_fwd(x, f, g, b, name, tb=512):
    t = x.shape[0]
    tb = min(tb, t)

    def body(x_ref, f_ref, g_ref, b_ref, o_ref):
        u = ALPHA * x_ref[...] + f_ref[...]
        mu = jnp.mean(u, axis=-1, keepdims=True)
        d = u - mu
        var = jnp.mean(d * d, axis=-1, keepdims=True)
        o_ref[...] = d * lax.rsqrt(var + EPS) * g_ref[...] + b_ref[...]

    row = pl.BlockSpec((tb, D), lambda i: (i, 0))
    par = pl.BlockSpec((1, D), lambda i: (0, 0))
    return pl.pallas_call(
        body, name=name, grid=(t // tb,), in_specs=[row, row, par, par], out_specs=row,
        out_shape=jax.ShapeDtypeStruct((t, D), F32), compiler_params=_cp("parallel"),
    )(x, f, g, b)


def _ln_bwd(x, f, dy, g, name, tb=512):
    t = x.shape[0]
    tb = min(tb, t)

    def body(x_ref, f_ref, dy_ref, g_ref, du_ref, dg_ref, db_ref):
        @pl.when(pl.program_id(0) == 0)
        def _():
            dg_ref[...] = jnp.zeros_like(dg_ref)
            db_ref[...] = jnp.zeros_like(db_ref)

        u = ALPHA * x_ref[...] + f_ref[...]
        mu = jnp.mean(u, axis=-1, keepdims=True)
        d = u - mu
        var = jnp.mean(d * d, axis=-1, keepdims=True)
        rstd = lax.rsqrt(var + EPS)
        xhat = d * rstd
        dy = dy_ref[...]
        dxh = dy * g_ref[...]
        m1 = jnp.mean(dxh, axis=-1, keepdims=True)
        m2 = jnp.mean(dxh * xhat, axis=-1, keepdims=True)
        du_ref[...] = rstd * (dxh - m1 - xhat * m2)
        dg_ref[...] += jnp.sum(dy * xhat, axis=0, keepdims=True)
        db_ref[...] += jnp.sum(dy, axis=0, keepdims=True)

    row = pl.BlockSpec((tb, D), lambda i: (i, 0))
    par = pl.BlockSpec((1, D), lambda i: (0, 0))
    return pl.pallas_call(
        body, name=name, grid=(t // tb,), in_specs=[row, row, row, par], out_specs=[row, par, par],
        out_shape=[jax.ShapeDtypeStruct((t, D), F32), jax.ShapeDtypeStruct((1, D), F32),
                   jax.ShapeDtypeStruct((1, D), F32)],
        compiler_params=_cp("arbitrary"),
    )(x, f, dy, g)


def _swiglu_fwd(gu, name, tb=512):
    t = gu.shape[0]
    tb = min(tb, t)

    def body(g_ref, u_ref, o_ref):
        g = g_ref[...]
        o_ref[...] = (g * _sigmoid(g) * u_ref[...]).astype(BF16)

    return pl.pallas_call(
        body, name=name, grid=(t // tb,),
        in_specs=[pl.BlockSpec((tb, D_FF), lambda i: (i, 0)), pl.BlockSpec((tb, D_FF), lambda i: (i, 1))],
        out_specs=pl.BlockSpec((tb, D_FF), lambda i: (i, 0)),
        out_shape=jax.ShapeDtypeStruct((t, D_FF), BF16), compiler_params=_cp("parallel"),
    )(gu, gu)


def _swiglu_bwd(gu, dact, name, tb=512):
    t = gu.shape[0]
    tb = min(tb, t)

    def body(g_ref, u_ref, da_ref, o_ref):
        g = g_ref[...]
        s = _sigmoid(g)
        da = da_ref[...]
        o_ref[:, :D_FF] = (da * u_ref[...] * (s * (1.0 + g * (1.0 - s)))).astype(BF16)
        o_ref[:, D_FF:] = (da * g * s).astype(BF16)

    return pl.pallas_call(
        body, name=name, grid=(t // tb,),
        in_specs=[pl.BlockSpec((tb, D_FF), lambda i: (i, 0)), pl.BlockSpec((tb, D_FF), lambda i: (i, 1)),
                  pl.BlockSpec((tb, D_FF), lambda i: (i, 0))],
        out_specs=pl.BlockSpec((tb, 2 * D_FF), lambda i: (i, 0)),
        out_shape=jax.ShapeDtypeStruct((t, 2 * D_FF), BF16),
        compiler_params=_cp("parallel"),
    )(gu, gu, dact)


def _merge_fwd(pgl, bg, b1, b2, b3, name, tb=512):
    t = pgl.shape[0]
    tb = min(tb, t)

    def body(gl_ref, bg_ref, b1_ref, b2_ref, b3_ref, o_ref):
        acc = None
        for j, b_ref in enumerate((b1_ref, b2_ref, b3_ref)):
            sl = slice(j * D, (j + 1) * D)
            term = _sigmoid(gl_ref[:, sl] + bg_ref[:, sl]) * b_ref[...]
            acc = term if acc is None else acc + term
        o_ref[...] = acc.astype(BF16)

    row = pl.BlockSpec((tb, D), lambda i: (i, 0))
    return pl.pallas_call(
        body, name=name, grid=(t // tb,),
        in_specs=[pl.BlockSpec((tb, 3 * D), lambda i: (i, 0)), pl.BlockSpec((1, 3 * D), lambda i: (0, 0)), row, row, row],
        out_specs=row, out_shape=jax.ShapeDtypeStruct((t, D), BF16), compiler_params=_cp("parallel"),
    )(pgl, bg, b1, b2, b3)


def _merge_bwd(pgl, bg, b1, b2, b3, dm, name, tb=512):
    t = pgl.shape[0]
    tb = min(tb, t)

    def body(gl_ref, bg_ref, b1_ref, b2_ref, b3_ref, dm_ref, dgl_ref, d1_ref, d2_ref, d3_ref, dbg_ref):
        @pl.when(pl.program_id(0) == 0)
        def _():
            dbg_ref[...] = jnp.zeros_like(dbg_ref)

        dm_v = dm_ref[...]
        for j, (b_ref, d_ref) in enumerate(((b1_ref, d1_ref), (b2_ref, d2_ref), (b3_ref, d3_ref))):
            sl = slice(j * D, (j + 1) * D)
            gate = _sigmoid(gl_ref[:, sl] + bg_ref[:, sl])
            d_ref[...] = (dm_v * gate).astype(BF16)
            dgl = dm_v * b_ref[...] * (gate * (1.0 - gate))
            dgl_ref[:, sl] = dgl.astype(BF16)
            dbg_ref[:, sl] += jnp.sum(dgl, axis=0, keepdims=True)

    row = pl.BlockSpec((tb, D), lambda i: (i, 0))
    wide = pl.BlockSpec((tb, 3 * D), lambda i: (i, 0))
    par = pl.BlockSpec((1, 3 * D), lambda i: (0, 0))
    return pl.pallas_call(
        body, name=name, grid=(t // tb,),
        in_specs=[wide, par, row, row, row, row], out_specs=[wide, row, row, row, par],
        out_shape=[jax.ShapeDtypeStruct((t, 3 * D), BF16)] + [jax.ShapeDtypeStruct((t, D), BF16)] * 3
                  + [jax.ShapeDtypeStruct((1, 3 * D), F32)],
        compiler_params=_cp("arbitrary"),
    )(pgl, bg, b1, b2, b3, dm)


def _xa_probs(q, kv_ref, hd):
    sl = slice(hd * XA_HEAD_DIM, (hd + 1) * XA_HEAD_DIM)
    k = kv_ref[:, sl]
    v = kv_ref[:, D + hd * XA_HEAD_DIM:D + (hd + 1) * XA_HEAD_DIM]
    s = _dot(q[:, sl], k, _NT) * (XA_HEAD_DIM ** -0.5)
    e = jnp.exp(s - jnp.max(s, axis=1, keepdims=True))
    return sl, k, v, e / jnp.sum(e, axis=1, keepdims=True)


def _xa_fwd(pq, kv, name, tb=512):
    t = pq.shape[0]
    tb = min(tb, t)

    def body(q_ref, kv_ref, o_ref):
        q = q_ref[...]
        for hd in range(XA_HEADS):
            sl, _, v, p = _xa_probs(q, kv_ref, hd)
            o_ref[:, sl] = _dot(p, v).astype(BF16)

    row = pl.BlockSpec((tb, D), lambda i: (i, 0))
    return pl.pallas_call(
        body, name=name, grid=(t // tb,),
        in_specs=[row, pl.BlockSpec(kv.shape, lambda i: (0, 0))], out_specs=row,
        out_shape=jax.ShapeDtypeStruct((t, D), BF16), compiler_params=_cp("parallel"),
    )(pq, kv)


def _xa_bwd(pq, kv, dy, name, tb=512):
    t = pq.shape[0]
    tb = min(tb, t)

    def body(q_ref, kv_ref, dy_ref, dq_ref, dkv_ref):
        @pl.when(pl.program_id(0) == 0)
        def _():
            dkv_ref[...] = jnp.zeros_like(dkv_ref)

        q = q_ref[...]
        for hd in range(XA_HEADS):
            sl, k, v, p = _xa_probs(q, kv_ref, hd)
            dyh = dy_ref[:, sl]
            vsl = slice(D + hd * XA_HEAD_DIM, D + (hd + 1) * XA_HEAD_DIM)
            dkv_ref[:, vsl] += _dot(p, dyh, _TN)
            dp = _dot(dyh, v, _NT)
            ds = p * (dp - jnp.sum(dp * p, axis=1, keepdims=True)) * (XA_HEAD_DIM ** -0.5)
            dq_ref[:, sl] = _dot(ds, k).astype(BF16)
            dkv_ref[:, sl] += _dot(ds, q[:, sl], _TN)

    row = pl.BlockSpec((tb, D), lambda i: (i, 0))
    kvs = pl.BlockSpec(kv.shape, lambda i: (0, 0))
    return pl.pallas_call(
        body, name=name, grid=(t // tb,), in_specs=[row, kvs, row], out_specs=[row, kvs],
        out_shape=[jax.ShapeDtypeStruct((t, D), BF16), jax.ShapeDtypeStruct(kv.shape, F32)],
        compiler_params=_cp("arbitrary"),
    )(pq, kv, dy)


def _scan_fwd(a, u):
    n = a.shape[0]
    row = _iota((n, 1), 0)
    d = 1
    while d < n:
        us = jnp.where(row >= d, pltpu.roll(u, d, 0), 0.0)
        u = a * us + u
        a = a * pltpu.roll(a, d, 0)
        d *= 2
    return u


def _scan_rev(b, u):
    n = b.shape[0]
    row = _iota((n, 1), 0)
    d = 1
    while d < n:
        us = jnp.where(row < n - d, pltpu.roll(u, n - d, 0), 0.0)
        u = b * us + u
        b = b * pltpu.roll(b, n - d, 0)
        d *= 2
    return u


def _lru_gates(xc, wa_ref, ba, wi_ref, bi, lam):
    za = jnp.concatenate([_dot(xc[:, n * 128:(n + 1) * 128], wa_ref[n]) for n in range(LRU_BLOCKS)], axis=1) + ba
    zi = jnp.concatenate([_dot(xc[:, n * 128:(n + 1) * 128], wi_ref[n]) for n in range(LRU_BLOCKS)], axis=1) + bi
    r = _sigmoid(za)
    ig = _sigmoid(zi)
    sp = _softplus(-lam)
    log_a = (-LRU_C) * r * sp
    a = jnp.exp(log_a)
    m = jnp.sqrt(-_expm1(2.0 * log_a))
    u = m * (ig * xc)
    return a, u, r, ig, m, sp


def _lru_conv(i_blk, x_ref, xp_ref, cw_ref, cb_ref, tb):
    halo = jnp.where(i_blk == 0, 0.0, xp_ref[...])
    taps = _conv_taps(jnp.concatenate([halo, x_ref[...]], axis=0), tb)
    cw = cw_ref[...]
    return _conv_fwd(taps, cw, cb_ref[...]), taps, cw


def _lru_fwd(p, cw, cb, wa, ba, wi, bi, lam, name, tb=256):
    t = p.shape[0]
    tb = min(tb, t)
    nb = t // tb
    r8 = tb // 8

    def body(x_ref, xp_ref, g_ref, cw_ref, cb_ref, wa_ref, ba_ref, wi_ref, bi_ref, lam_ref, y_ref, h_ref, hc_ref):
        i = pl.program_id(0)

        @pl.when(i == 0)
        def _():
            hc_ref[...] = jnp.zeros_like(hc_ref)

        xc, _, _ = _lru_conv(i, x_ref, xp_ref, cw_ref, cb_ref, tb)
        a, u, _, _, _, _ = _lru_gates(xc, wa_ref, ba_ref[...], wi_ref, bi_ref[...], lam_ref[...])
        row = _iota((tb, 1), 0)
        u = u + jnp.where(row == 0, a * hc_ref[...], 0.0)
        h = _scan_fwd(a, u)
        h_ref[...] = h
        hc_ref[...] = h[tb - 1:tb, :]
        gl, _ = _gelu_and_grad(g_ref[...])
        y_ref[...] = (gl * h).astype(BF16)

    par = pl.BlockSpec((1, D), lambda i: (0, 0))
    wsp = pl.BlockSpec((LRU_BLOCKS, LRU_BLOCK, LRU_BLOCK), lambda i: (0, 0, 0))
    row = pl.BlockSpec((tb, D), lambda i: (i, 0))
    return pl.pallas_call(
        body, name=name, grid=(nb,),
        in_specs=[row, pl.BlockSpec((8, D), lambda i: (jnp.maximum(i * r8 - 1, 0), 0)),
                  pl.BlockSpec((tb, D), lambda i: (i, 1)),
                  pl.BlockSpec((4, D), lambda i: (0, 0)), par, wsp, par, wsp, par, par],
        out_specs=[row, row],
        out_shape=[jax.ShapeDtypeStruct((t, D), BF16), jax.ShapeDtypeStruct((t, D), F32)],
        scratch_shapes=[pltpu.VMEM((1, D), F32)],
        compiler_params=_cp("arbitrary"),
    )(p, p, p, cw, cb, wa, ba, wi, bi, lam)


def _lru_bwd(p, h, dy, cw, cb, wa, ba, wi, bi, lam, name, tb=256):
    t = p.shape[0]
    tb = min(tb, t)
    nb = t // tb
    r8 = tb // 8

    def body(x_ref, xp_ref, g_ref, h_ref, hp_ref, dy_ref, cw_ref, cb_ref, wa_ref, ba_ref, wi_ref, bi_ref, lam_ref,
             dp_ref, dcw_ref, dcb_ref, dwa_ref, dba_ref, dwi_ref, dbi_ref, dlam_ref, carry_ref, dnext_ref):
        i = pl.program_id(0)
        blk = nb - 1 - i

        @pl.when(i == 0)
        def _():
            for r in (dcw_ref, dcb_ref, dwa_ref, dba_ref, dwi_ref, dbi_ref, dlam_ref, carry_ref, dnext_ref):
                r[...] = jnp.zeros_like(r)

        xc, taps, cw = _lru_conv(blk, x_ref, xp_ref, cw_ref, cb_ref, tb)
        lam = lam_ref[...]
        a, _, r, ig, m, sp = _lru_gates(xc, wa_ref, ba_ref[...], wi_ref, bi_ref[...], lam)
        gl, dgl = _gelu_and_grad(g_ref[...])
        h = h_ref[...]
        dy = dy_ref[...]
        dp_ref[:, D:] = (dy * h * dgl).astype(BF16)
        row = _iota((tb, 1), 0)
        dh = dy * gl + jnp.where(row == tb - 1, carry_ref[...], 0.0)
        b = jnp.where(row < tb - 1, pltpu.roll(a, tb - 1, 0), 0.0)
        gs = _scan_rev(b, dh)
        carry_ref[...] = a[0:1] * gs[0:1]
        h_last = jnp.where(blk == 0, 0.0, hp_ref[7:8, :])
        hprev = jnp.where(row == 0, h_last, pltpu.roll(h, 1, 0))
        da = gs * hprev
        dm = gs * ig * xc
        di = gs * m * xc
        dxc = gs * m * ig
        dlog = (0.5 * dm / m) * (-2.0 * a * a) + da * a
        dr = dlog * ((-LRU_C) * sp)
        dsp = jnp.sum(dlog * ((-LRU_C) * r), axis=0, keepdims=True)
        dlam_ref[...] += dsp * (-_sigmoid(-lam))
        dza = dr * r * (1.0 - r)
        dzi = di * ig * (1.0 - ig)
        dba_ref[...] += jnp.sum(dza, axis=0, keepdims=True)
        dbi_ref[...] += jnp.sum(dzi, axis=0, keepdims=True)
        parts = []
        for n in range(LRU_BLOCKS):
            sl = slice(n * 128, (n + 1) * 128)
            dwa_ref[n] += _dot(xc[:, sl], dza[:, sl], _TN)
            dwi_ref[n] += _dot(xc[:, sl], dzi[:, sl], _TN)
            parts.append(_dot(dza[:, sl], wa_ref[n], _NT) + _dot(dzi[:, sl], wi_ref[n], _NT))
        dxc = dxc + jnp.concatenate(parts, axis=1)
        dx, dcw, dcb = _conv_bwd(dxc, dnext_ref[...], taps, cw, tb)
        dp_ref[:, :D] = dx.astype(BF16)
        dcw_ref[...] += dcw
        dcb_ref[...] += dcb
        dnext_ref[...] = dxc[0:8]

    par = pl.BlockSpec((1, D), lambda i: (0, 0))
    wsp = pl.BlockSpec((LRU_BLOCKS, LRU_BLOCK, LRU_BLOCK), lambda i: (0, 0, 0))
    cws = pl.BlockSpec((4, D), lambda i: (0, 0))
    rev = lambda i: nb - 1 - i
    prev8 = lambda i: (jnp.maximum(rev(i) * r8 - 1, 0), 0)
    w_shape = jax.ShapeDtypeStruct((LRU_BLOCKS, LRU_BLOCK, LRU_BLOCK), F32)
    v_shape = jax.ShapeDtypeStruct((1, D), F32)
    return pl.pallas_call(
        body, name=name, grid=(nb,),
        in_specs=[pl.BlockSpec((tb, D), lambda i: (rev(i), 0)), pl.BlockSpec((8, D), prev8),
                  pl.BlockSpec((tb, D), lambda i: (rev(i), 1)),
                  pl.BlockSpec((tb, D), lambda i: (rev(i), 0)), pl.BlockSpec((8, D), prev8),
                  pl.BlockSpec((tb, D), lambda i: (rev(i), 0)),
                  cws, par, wsp, par, wsp, par, par],
        out_specs=[pl.BlockSpec((tb, 2 * D), lambda i: (rev(i), 0)), cws, par, wsp, par, wsp, par, par],
        out_shape=[jax.ShapeDtypeStruct((t, 2 * D), BF16), jax.ShapeDtypeStruct((4, D), F32), v_shape,
                   w_shape, v_shape, w_shape, v_shape, v_shape],
        scratch_shapes=[pltpu.VMEM((1, D), F32), pltpu.VMEM((8, D), F32)],
        compiler_params=_cp("arbitrary"),
    )(p, p, p, h, h, dy, cw, cb, wa, ba, wi, bi, lam)


def _ssd_consts():
    m0 = _iota((1, 128), 1) < 64
    e = (jnp.right_shift(_iota((SSD_HEADS, D_SSD), 1), 6) == _iota((SSD_HEADS, D_SSD), 0)).astype(BF16)
    tril = (_iota((CHUNK, CHUNK), 0) >= _iota((CHUNK, CHUNK), 1)).astype(F32)
    eye = (_iota((SSD_HEADS, SSD_HEADS), 0) == _iota((SSD_HEADS, SSD_HEADS), 1)).astype(F32)
    r2 = _iota((CHUNK, 128), 0)
    c2 = jnp.bitwise_and(_iota((CHUNK, 128), 1), 63)
    return dict(m0=m0, e=e, tril=tril, eye=eye, causal2=r2 >= c2, fold=(c2 == r2).astype(BF16))


def _ssd_pre(blk, p_ref, pp_ref, cw_ref, cb_ref, dtb_ref, alog_ref, dvec_ref, k):
    halo = jnp.where(blk == 0, 0.0, pp_ref[:, S_XBC:S_DT])
    taps = _conv_taps(jnp.concatenate([halo, p_ref[:, S_XBC:S_DT]], axis=0), CHUNK)
    cw = cw_ref[...]
    c = _conv_fwd(taps, cw, cb_ref[...])
    sg = _sigmoid(c)
    xbc = c * sg
    dtp = p_ref[:, S_DT:S_DT + DT_REAL] + dtb_ref[...]
    dt = _softplus(dtp)
    a = -jnp.exp(alog_ref[...])
    cs = _dot_hi(k["tril"], dt * a)
    cs_last = cs[CHUNK - 1:CHUNK]
    dend = jnp.exp(cs_last - cs)
    cdec = jnp.exp(cs_last)
    big = _dot01(jnp.concatenate([dt, jnp.exp(cs), dend], axis=0), k["e"])
    small = _dot01(jnp.concatenate([jnp.broadcast_to(cdec, (8, SSD_HEADS)),
                                    jnp.broadcast_to(dvec_ref[...], (8, SSD_HEADS))], axis=0), k["e"])
    cst2 = _dot_hi(k["eye"], jnp.concatenate([cs, cs], axis=0), _NT)
    return dict(taps=taps, cw=cw, c=c, sg=sg, xs=xbc[:, :D_SSD], bm=xbc[:, D_SSD:D_SSD + 512],
                cm=xbc[:, D_SSD + 512:], dtp=dtp, dt=dt, a=a, cs=cs, dend=dend, cdec=cdec,
                dtx=big[0:CHUNK], ecx=big[CHUNK:2 * CHUNK], dex=big[2 * CHUNK:3 * CHUNK],
                cdx=small[0:1], ddx=small[8:9], cst2=cst2)


def _pair_decay(p, cs, cst2, k):
    h0, h1 = 2 * p, 2 * p + 1
    colp = jnp.where(k["m0"], cs[:, h0:h0 + 1], cs[:, h1:h1 + 1])
    rowp = jnp.where(k["m0"], cst2[h0:h0 + 1, :], cst2[h1:h1 + 1, :])
    return jnp.where(k["causal2"], jnp.exp(colp - rowp), 0.0)


def _pair_stack(xp, k):
    return jnp.concatenate([jnp.where(k["m0"], xp, 0.0), jnp.where(k["m0"], 0.0, xp)], axis=0)


def _group_norm(yz, nw, with_stats=False):
    outs, stats = [], []
    for g in range(SSD_GROUPS):
        yzg = yz[:, g * GROUP_W:(g + 1) * GROUP_W]
        r = lax.rsqrt(jnp.mean(yzg * yzg, axis=1, keepdims=True) + EPS)
        outs.append(yzg * r)
        stats.append(r)
    y = jnp.concatenate(outs, axis=1) * nw
    return (y, stats) if with_stats else y


def _ssd_fwd(p, cw, cb, dtb, alog, dvec, nw, name):
    t = p.shape[0]
    nc = t // CHUNK

    def body(p_ref, pp_ref, cw_ref, cb_ref, dtb_ref, alog_ref, dvec_ref, nw_ref, y_ref, yraw_ref, hs_ref, h_scr):
        i = pl.program_id(0)

        @pl.when(i == 0)
        def _():
            h_scr[...] = jnp.zeros_like(h_scr)

        k = _ssd_consts()
        s = _ssd_pre(i, p_ref, pp_ref, cw_ref, cb_ref, dtb_ref, alog_ref, dvec_ref, k)
        xs, bm, cm = s["xs"], s["bm"], s["cm"]
        xdt = xs * s["dtx"]
        hprev = h_scr[...]
        hs_ref[0] = hprev
        ys, hn = [], []
        for g in range(SSD_GROUPS):
            gs = slice(g * GROUP_W, (g + 1) * GROUP_W)
            bg = bm[:, g * 128:(g + 1) * 128]
            cg = cm[:, g * 128:(g + 1) * 128]
            cbdup = _dot(cg, jnp.concatenate([bg, bg], axis=0), _NT)
            hp_g = hprev[:, gs]
            yd = []
            for q in range(4):
                pr = g * 4 + q
                mp = cbdup * _pair_decay(pr, s["cs"], s["cst2"], k)
                yd.append(_dot(mp, _pair_stack(xdt[:, pr * 128:(pr + 1) * 128], k)))
            ys.append(jnp.concatenate(yd, axis=1) + _dot(cg, hp_g) * s["ecx"][:, gs])
            hn.append(hp_g * s["cdx"][:, gs] + _dot(bg, xdt[:, gs] * s["dex"][:, gs], _TN))
        h_scr[...] = jnp.concatenate(hn, axis=1)
        yraw = jnp.concatenate(ys, axis=1) + s["ddx"] * xs
        yraw_ref[...] = yraw
        z = p_ref[:, S_Z:S_Z + D_SSD]
        y_ref[...] = _group_norm(yraw * (z * _sigmoid(z)), nw_ref[...]).astype(BF16)

    hv = pl.BlockSpec((1, DT_REAL), lambda i: (0, 0))
    return pl.pallas_call(
        body, name=name, grid=(nc,),
        in_specs=[pl.BlockSpec((CHUNK, W_SSD), lambda i: (i, 0)),
                  pl.BlockSpec((8, W_SSD), lambda i: (jnp.maximum(i * (CHUNK // 8) - 1, 0), 0)),
                  pl.BlockSpec((4, D_XBC), lambda i: (0, 0)), pl.BlockSpec((1, D_XBC), lambda i: (0, 0)),
                  hv, hv, hv, pl.BlockSpec((1, D_SSD), lambda i: (0, 0))],
        out_specs=[pl.BlockSpec((CHUNK, D_SSD), lambda i: (i, 0)), pl.BlockSpec((CHUNK, D_SSD), lambda i: (i, 0)),
                   pl.BlockSpec((1, SSD_STATE, D_SSD), lambda i: (i, 0, 0))],
        out_shape=[jax.ShapeDtypeStruct((t, D_SSD), BF16), jax.ShapeDtypeStruct((t, D_SSD), F32),
                   jax.ShapeDtypeStruct((nc, SSD_STATE, D_SSD), F32)],
        scratch_shapes=[pltpu.VMEM((SSD_STATE, D_SSD), F32)],
        compiler_params=_cp("arbitrary"),
    )(p, p, cw, cb, dtb, alog, dvec, nw)


def _ssd_bwd(p, yraw, hs, dy, cw, cb, dtb, alog, dvec, nw, name):
    t = p.shape[0]
    nc = t // CHUNK

    def body(p_ref, pp_ref, yraw_ref, hs_ref, dy_ref, cw_ref, cb_ref, dtb_ref, alog_ref, dvec_ref, nw_ref,
             dp_ref, dcw_ref, dcb_ref, ddtb_ref, dalog_ref, dd_ref, dnw_ref, dh_scr, dnext_scr):
        i = pl.program_id(0)
        blk = nc - 1 - i

        @pl.when(i == 0)
        def _():
            for r in (dcw_ref, dcb_ref, ddtb_ref, dalog_ref, dd_ref, dnw_ref, dh_scr, dnext_scr):
                r[...] = jnp.zeros_like(r)

        k = _ssd_consts()
        s = _ssd_pre(blk, p_ref, pp_ref, cw_ref, cb_ref, dtb_ref, alog_ref, dvec_ref, k)
        xs, bm, cm, cs, dt, a = s["xs"], s["bm"], s["cm"], s["cs"], s["dt"], s["a"]
        m0 = k["m0"]
        xdt = xs * s["dtx"]
        hprev = hs_ref[0]
        dh = dh_scr[...]

        nw_v = nw_ref[...]
        yraw = yraw_ref[...]
        z = p_ref[:, S_Z:S_Z + D_SSD]
        sz = _sigmoid(z)
        siluz = z * sz
        yz = yraw * siluz
        dyo = dy_ref[...]
        dyn = dyo * nw_v
        dyz_parts, dnw_parts = [], []
        for g in range(SSD_GROUPS):
            gs = slice(g * GROUP_W, (g + 1) * GROUP_W)
            yzg = yz[:, gs]
            r = lax.rsqrt(jnp.mean(yzg * yzg, axis=1, keepdims=True) + EPS)
            dnw_parts.append(jnp.sum(dyo[:, gs] * yzg * r, axis=0, keepdims=True))
            dyz_parts.append(r * dyn[:, gs] - yzg * (r * r * r) * jnp.mean(dyn[:, gs] * yzg, axis=1, keepdims=True))
        dnw_ref[...] += jnp.concatenate(dnw_parts, axis=1)
        dyz = jnp.concatenate(dyz_parts, axis=1)
        d_y = dyz * siluz
        dp_ref[:, S_Z:S_Z + D_SSD] = (dyz * yraw * (sz * (1.0 + z * (1.0 - sz)))).astype(BF16)
        dd_row = jnp.sum(d_y * xs, axis=0, keepdims=True)
        dxs = d_y * s["ddx"]

        lane_h = _iota((1, SSD_HEADS), 1)
        sub_h = _iota((SSD_HEADS, 1), 0)
        dcs = jnp.zeros((CHUNK, SSD_HEADS), F32)
        dcst2 = jnp.zeros((SSD_HEADS, 128), F32)
        dxdt_parts, db_parts, dc_parts, dhp_parts, yoff_parts, dend_parts, dcd_parts = [], [], [], [], [], [], []
        for g in range(SSD_GROUPS):
            gs = slice(g * GROUP_W, (g + 1) * GROUP_W)
            bg = bm[:, g * 128:(g + 1) * 128]
            cg = cm[:, g * 128:(g + 1) * 128]
            bdup = jnp.concatenate([bg, bg], axis=0)
            cbdup = _dot(cg, bdup, _NT)
            dcb2 = jnp.zeros((CHUNK, 128), F32)
            dxp_parts = []
            for q in range(4):
                pr = g * 4 + q
                h0, h1 = 2 * pr, 2 * pr + 1
                lp = _pair_decay(pr, cs, s["cst2"], k)
                mp = cbdup * lp
                xst = _pair_stack(xdt[:, pr * 128:(pr + 1) * 128], k)
                dyp = d_y[:, pr * 128:(pr + 1) * 128]
                dmp = _dot(dyp, xst, _NT)
                dxst = _dot(mp, dyp, _TN)
                dxp_parts.append(jnp.where(m0, dxst[:CHUNK], dxst[CHUNK:]))
                dcb2 = dcb2 + dmp * lp
                dlm = dmp * mp
                rs0 = jnp.sum(jnp.where(m0, dlm, 0.0), axis=1, keepdims=True)
                rs1 = jnp.sum(jnp.where(m0, 0.0, dlm), axis=1, keepdims=True)
                dcs = dcs + jnp.where(lane_h == h0, rs0, 0.0) + jnp.where(lane_h == h1, rs1, 0.0)
                colsum = jnp.sum(dlm, axis=0, keepdims=True)
                sel = ((sub_h == h0) & m0) | ((sub_h == h1) & jnp.logical_not(m0))
                dcst2 = dcst2 - jnp.where(sel, colsum, 0.0)
            dcg = _dot(dcb2, bdup)
            dbdup = _dot(dcb2, cg, _TN)
            dbg = dbdup[:CHUNK] + dbdup[CHUNK:]
            hp_g = hprev[:, gs]
            zoff = _dot(cg, hp_g)
            dzo = d_y[:, gs] * s["ecx"][:, gs]
            dcg = dcg + _dot(dzo, hp_g, _NT)
            dh_g = dh[:, gs]
            dhp_parts.append(_dot(cg, dzo, _TN) + dh_g * s["cdx"][:, gs])
            dcd_parts.append(jnp.sum(dh_g * hp_g, axis=0, keepdims=True))
            wg = xdt[:, gs] * s["dex"][:, gs]
            dbg = dbg + _dot(wg, dh_g, _NT)
            dwg = _dot(bg, dh_g)
            dxdt_parts.append(jnp.concatenate(dxp_parts, axis=1) + dwg * s["dex"][:, gs])
            dend_g = dwg * wg
            dend_parts.append(jnp.sum(dend_g, axis=0, keepdims=True))
            yoff_parts.append(dzo * zoff - dend_g)
            db_parts.append(dbg)
            dc_parts.append(dcg)
        dh_scr[...] = jnp.concatenate(dhp_parts, axis=1)
        dxdt = jnp.concatenate(dxdt_parts, axis=1)
        sums = _dot01(jnp.concatenate([jnp.concatenate(yoff_parts, axis=1), dxdt * xs], axis=0), k["e"], _NT)
        rows8 = jnp.concatenate([jnp.broadcast_to(jnp.concatenate(r, axis=1), (8, D_SSD))
                                 for r in (dcd_parts, [dd_row], dend_parts)], axis=0)
        small = _dot01(rows8, k["e"], _NT)
        dd_ref[...] += small[8:9]
        dcs_last = small[0:1] * s["cdec"] + small[16:17]
        hi, lo = _split(dcst2)
        dcs = (dcs + sums[0:CHUNK]
               + lax.dot_general(k["fold"], hi, _NT, preferred_element_type=F32)
               + lax.dot_general(k["fold"], lo, _NT, preferred_element_type=F32)
               + jnp.where(_iota((CHUNK, 1), 0) == CHUNK - 1, dcs_last, 0.0))
        dda = _dot_hi(k["tril"], dcs, _TN)
        ddt = dda * a + sums[CHUNK:2 * CHUNK]
        dalog_ref[...] += jnp.sum(dda * dt, axis=0, keepdims=True) * a
        dxs = dxs + dxdt * s["dtx"]
        draw = ddt * _sigmoid(s["dtp"])
        ddtb_ref[...] += jnp.sum(draw, axis=0, keepdims=True)
        dp_ref[:, S_DT:] = jnp.zeros((CHUNK, W_SSD - S_DT), BF16)
        dp_ref[:, S_DT:S_DT + DT_REAL] = draw.astype(BF16)
        dxbc = jnp.concatenate([dxs] + db_parts + dc_parts, axis=1)
        sg, c = s["sg"], s["c"]
        dc = dxbc * (sg * (1.0 + c * (1.0 - sg)))
        dx, dcw, dcb = _conv_bwd(dc, dnext_scr[...], s["taps"], s["cw"], CHUNK)
        dp_ref[:, S_XBC:S_DT] = dx.astype(BF16)
        dcw_ref[...] += dcw
        dcb_ref[...] += dcb
        dnext_scr[...] = dc[0:8]

    rev = lambda i: nc - 1 - i
    hv = pl.BlockSpec((1, DT_REAL), lambda i: (0, 0))
    cws = pl.BlockSpec((4, D_XBC), lambda i: (0, 0))
    cbs = pl.BlockSpec((1, D_XBC), lambda i: (0, 0))
    nws = pl.BlockSpec((1, D_SSD), lambda i: (0, 0))
    wide = pl.BlockSpec((CHUNK, D_SSD), lambda i: (rev(i), 0))
    hshape = jax.ShapeDtypeStruct((1, DT_REAL), F32)
    return pl.pallas_call(
        body, name=name, grid=(nc,),
        in_specs=[pl.BlockSpec((CHUNK, W_SSD), lambda i: (rev(i), 0)),
                  pl.BlockSpec((8, W_SSD), lambda i: (jnp.maximum(rev(i) * (CHUNK // 8) - 1, 0), 0)),
                  wide, pl.BlockSpec((1, SSD_STATE, D_SSD), lambda i: (rev(i), 0, 0)), wide,
                  cws, cbs, hv, hv, hv, nws],
        out_specs=[pl.BlockSpec((CHUNK, W_SSD), lambda i: (rev(i), 0)), cws, cbs, hv, hv, hv, nws],
        out_shape=[jax.ShapeDtypeStruct((t, W_SSD), BF16), jax.ShapeDtypeStruct((4, D_XBC), F32),
                   jax.ShapeDtypeStruct((1, D_XBC), F32), hshape, hshape, hshape,
                   jax.ShapeDtypeStruct((1, D_SSD), F32)],
        scratch_shapes=[pltpu.VMEM((SSD_STATE, D_SSD), F32), pltpu.VMEM((8, D_XBC), F32)],
        compiler_params=_cp("arbitrary"),
    )(p, p, yraw, hs, dy, cw, cb, dtb, alog, dvec, nw)


def _loss_head(y, target, name, tb=512):
    t = y.shape[0]
    tb = min(tb, t)

    def body(y_ref, t_ref, dy_ref, l_ref):
        @pl.when(pl.program_id(0) == 0)
        def _():
            l_ref[...] = jnp.zeros_like(l_ref)

        e = y_ref[...] - t_ref[...]
        dy_ref[...] = e * (1.0 / D)
        l_ref[...] += jnp.sum(jnp.sum(e * e, axis=1, keepdims=True), axis=0, keepdims=True) * (0.5 / D)

    row = pl.BlockSpec((tb, D), lambda i: (i, 0))
    return pl.pallas_call(
        body, name=name, grid=(t // tb,), in_specs=[row, row],
        out_specs=[row, pl.BlockSpec((8, 128), lambda i: (0, 0))],
        out_shape=[jax.ShapeDtypeStruct((t, D), F32), jax.ShapeDtypeStruct((8, 128), F32)],
        compiler_params=_cp("arbitrary"),
    )(y, target)


def _adamw(slots, w, m, v, name, tb):
    nl = len(slots)
    ns, r, c = slots[0].shape
    assert r % tb == 0 and w.shape == (nl, r, c), (r, tb, w.shape)

    def body(*refs):
        s_refs = refs[:nl]
        w_ref, m_ref, v_ref, g_ref, d_ref, m2_ref, v2_ref = refs[nl:]

        def total(ref):
            acc = ref[0].astype(F32)
            for j in range(1, ns):
                acc = acc + ref[j].astype(F32)
            return acc

        g = total(s_refs[0])
        for layer in range(1, nl):
            g = jnp.where(pl.program_id(0) == layer, total(s_refs[layer]), g)
        m2 = ADAM_B1 * m_ref[...] + (1.0 - ADAM_B1) * g
        v2 = ADAM_B2 * v_ref[...] + (1.0 - ADAM_B2) * (g * g)
        m_hat = m2 / (1.0 - ADAM_B1 ** ADAM_STEP)
        v_hat = v2 / (1.0 - ADAM_B2 ** ADAM_STEP)
        g_ref[...] = g
        d_ref[...] = -ADAM_LR * (m_hat / (jnp.sqrt(v_hat) + ADAM_EPS) + ADAM_WD * w_ref[...])
        m2_ref[...] = m2
        v2_ref[...] = v2

    def slot_spec(layer):
        return pl.BlockSpec((ns, tb, c), lambda l, i: (0, jnp.where(l == layer, i, 0), 0))

    row = pl.BlockSpec((None, tb, c), lambda l, i: (l, i, 0))
    shp = jax.ShapeDtypeStruct((nl, r, c), F32)
    return pl.pallas_call(
        body, name=name, grid=(nl, r // tb),
        in_specs=[slot_spec(layer) for layer in range(nl)] + [row, row, row],
        out_specs=[row, row, row, row], out_shape=[shp, shp, shp, shp], compiler_params=_cp("arbitrary", "arbitrary"),
    )(*slots, w, m, v)


def _pair_sum(own, got, name, out_dtype, tb):
    nj, _, r, c = own.shape
    mc = lax.axis_index("c")

    def body(mc_ref, a_ref, b_ref, o_ref):
        del mc_ref
        o_ref[...] = (a_ref[...] + b_ref[...]).astype(out_dtype)

    return pl.pallas_call(
        body, name=name,
        grid_spec=pltpu.PrefetchScalarGridSpec(
            num_scalar_prefetch=1, grid=(nj, r // tb),
            in_specs=[pl.BlockSpec((None, None, tb, c), lambda j, i, mc_ref: (j, mc_ref[0], i, 0)),
                      pl.BlockSpec((None, tb, c), lambda j, i, mc_ref: (j, i, 0))],
            out_specs=pl.BlockSpec((None, tb, c), lambda j, i, mc_ref: (j, i, 0))),
        out_shape=jax.ShapeDtypeStruct((nj, r, c), out_dtype), compiler_params=_cp("parallel", "parallel"),
    )(jnp.reshape(mc, (1,)).astype(jnp.int32), own, got)


def _slot_sum(slots, name):
    ns, r, c = slots.shape

    def body(s_ref, o_ref):
        g = s_ref[0]
        for j in range(1, ns):
            g = g + s_ref[j]
        o_ref[...] = g

    return pl.pallas_call(body, name=name, out_shape=jax.ShapeDtypeStruct((r, c), F32))(slots)


def _position():
    return lax.axis_index("x"), lax.axis_index("y"), lax.axis_index("c")


def _all_gather(xs, name):
    n = len(xs)

    def body(*refs):
        x_refs, out_refs = refs[:n], refs[n:2 * n]
        send_sems, recv_sems, local_sems = refs[2 * n:]
        mx, my, mc = _position()
        me, sibling = (mx, my, mc), (mx, my, 1 - mc)
        chips = [(1 - mx, my), (mx, 1 - my), (1 - mx, 1 - my)]

        def copy(a, k, block, to, own=False):
            dst = out_refs[a].at[4 * block[0] + 2 * block[1] + block[2]]
            return pltpu.make_async_remote_copy(
                src_ref=x_refs[a] if own else dst, dst_ref=dst,
                send_sem=send_sems.at[a, k], recv_sem=recv_sems.at[a, k], device_id=to, device_id_type=MESH)

        mine = [pltpu.make_async_copy(x_refs[a], out_refs[a].at[4 * mx + 2 * my + mc], local_sems.at[a])
                for a in range(n)]
        first = [copy(a, 1 + j, me, (*chip, mc), own=True) for j, chip in enumerate(chips) for a in range(n)]
        first += [copy(a, 0, me, sibling, own=True) for a in range(n)]
        for cp in first + mine:
            cp.start()
        passed = []
        for j, chip in enumerate(chips):
            for a in range(n):
                copy(a, 1 + j, (*chip, mc), me).wait_recv()
                passed.append(copy(a, 4 + j, (*chip, mc), sibling))
                passed[-1].start()
        for a in range(n):
            copy(a, 0, sibling, me).wait_recv()
        for j, chip in enumerate(chips):
            for a in range(n):
                copy(a, 4 + j, (*chip, 1 - mc), me).wait_recv()
        for cp in first + passed:
            cp.wait_send()
        for cp in mine:
            cp.wait()

    return pl.pallas_call(
        body, name=name, in_specs=[ANY] * n, out_specs=[ANY] * n,
        out_shape=[jax.ShapeDtypeStruct((N_DEV,) + x.shape, x.dtype) for x in xs],
        scratch_shapes=[pltpu.SemaphoreType.DMA((n, 7)), pltpu.SemaphoreType.DMA((n, 7)), pltpu.SemaphoreType.DMA((n,))],
    )(*xs)


def _exchange_sibling(gs, name):
    n = len(gs)

    def body(*refs):
        g_refs, r_refs = refs[:n], refs[n:2 * n]
        send_sems, recv_sems = refs[2 * n:]
        mx, my, mc = _position()
        cps = [pltpu.make_async_remote_copy(src_ref=g_refs[a].at[:, 1 - mc], dst_ref=r_refs[a],
                                            send_sem=send_sems.at[a], recv_sem=recv_sems.at[a],
                                            device_id=(mx, my, 1 - mc), device_id_type=MESH) for a in range(n)]
        for cp in cps:
            cp.start()
        for cp in cps:
            cp.wait()

    return pl.pallas_call(
        body, name=name, in_specs=[ANY] * n, out_specs=[ANY] * n,
        out_shape=[jax.ShapeDtypeStruct(g.shape[:1] + g.shape[2:], g.dtype) for g in gs],
        scratch_shapes=[pltpu.SemaphoreType.DMA((n,)), pltpu.SemaphoreType.DMA((n,))],
    )(*gs)


def _exchange_chips(ss, name):
    n = len(ss)

    def body(*refs):
        s_refs, r_refs = refs[:n], refs[n:2 * n]
        send_sems, recv_sems, local_sems = refs[2 * n:]
        mx, my, mc = _position()
        my_chip = 2 * mx + my
        chips = [(1 - mx, my), (mx, 1 - my), (1 - mx, 1 - my)]

        def copy(a, k, to_slot):
            px, py = chips[k]
            return pltpu.make_async_remote_copy(
                src_ref=s_refs[a].at[2 * px + py], dst_ref=r_refs[a].at[to_slot], send_sem=send_sems.at[a, k],
                recv_sem=recv_sems.at[a, k], device_id=(px, py, mc), device_id_type=MESH)

        sends = [copy(a, k, my_chip) for k in range(3) for a in range(n)]
        local = [pltpu.make_async_copy(s_refs[a].at[my_chip], r_refs[a].at[my_chip], local_sems.at[a])
                 for a in range(n)]
        for cp in sends + local:
            cp.start()
        for k in range(3):
            px, py = chips[k]
            for a in range(n):
                copy(a, k, 2 * px + py).wait_recv()
        for cp in sends:
            cp.wait_send()
        for cp in local:
            cp.wait()

    return pl.pallas_call(
        body, name=name, in_specs=[ANY] * n, out_specs=[ANY] * n,
        out_shape=[jax.ShapeDtypeStruct(s.shape, s.dtype) for s in ss],
        scratch_shapes=[pltpu.SemaphoreType.DMA((n, 3)), pltpu.SemaphoreType.DMA((n, 3)), pltpu.SemaphoreType.DMA((n,))],
    )(*ss)


def _cols_concat(g, l, name, tb=128):
    _, _, k_dim, n = g.shape

    def body(g_ref, o_ref):
        o_ref[...] = jnp.concatenate([g_ref[d] for d in range(N_DEV)], axis=1)

    return pl.pallas_call(
        body, name=name, grid=(k_dim // tb,),
        in_specs=[pl.BlockSpec((N_DEV, None, tb, n), lambda i: (0, l, i, 0))],
        out_specs=pl.BlockSpec((tb, N_DEV * n), lambda i: (i, 0)),
        out_shape=jax.ShapeDtypeStruct((k_dim, N_DEV * n), g.dtype), compiler_params=_cp("parallel"),
    )(g)


def _cols_split(dw, name, tb=128):
    k_dim, n8 = dw.shape
    n = n8 // N_DEV

    def body(g_ref, o_ref):
        full = g_ref[...]
        for d in range(N_DEV):
            o_ref[d] = full[:, d * n:(d + 1) * n]

    return pl.pallas_call(
        body, name=name, grid=(k_dim // tb,),
        in_specs=[pl.BlockSpec((tb, n8), lambda i: (i, 0))],
        out_specs=pl.BlockSpec((N_DEV, tb, n), lambda i: (0, i, 0)),
        out_shape=jax.ShapeDtypeStruct((N_DEV, k_dim, n), dw.dtype), compiler_params=_cp("parallel"),
    )(dw)


_Q0, _GL0 = 7200, 8224
N_SHARD_IN = N_IN // N_DEV


def _w_in_regions(g, l, name, tb=128):
    def body(g_ref, ssd_ref, lru_ref, q_ref, gl_ref):
        full = jnp.concatenate([g_ref[d] for d in range(N_DEV)], axis=1)
        lru_ref[...] = full[:, 0:2 * D]
        ssd_ref[:, :S_DT] = full[:, 2 * D:2 * D + S_DT]
        ssd_ref[:, S_DT:] = jnp.zeros((tb, W_SSD - S_DT), g.dtype)
        ssd_ref[:, S_DT:S_DT + DT_REAL] = full[:, 2 * D + S_DT:_Q0]
        q_ref[...] = full[:, _Q0:_GL0]
        gl_ref[...] = full[:, _GL0:N_IN]

    widths = (W_SSD, 2 * D, D, 3 * D)
    return pl.pallas_call(
        body, name=name, grid=(D // tb,),
        in_specs=[pl.BlockSpec((N_DEV, None, tb, N_SHARD_IN), lambda i: (0, l, i, 0))],
        out_specs=[pl.BlockSpec((tb, wd), lambda i: (i, 0)) for wd in widths],
        out_shape=[jax.ShapeDtypeStruct((D, wd), g.dtype) for wd in widths], compiler_params=_cp("parallel"),
    )(g)


def _w_in_shards(dssd, dlru, dq, dgl, name, tb=128):
    def body(ssd_ref, lru_ref, q_ref, gl_ref, o_ref):
        full = jnp.concatenate([lru_ref[...], ssd_ref[:, :S_DT + DT_REAL], q_ref[...], gl_ref[...]], axis=1)
        for d in range(N_DEV):
            o_ref[d] = full[:, d * N_SHARD_IN:(d + 1) * N_SHARD_IN]

    return pl.pallas_call(
        body, name=name, grid=(D // tb,),
        in_specs=[pl.BlockSpec((tb, a.shape[1]), lambda i: (i, 0)) for a in (dssd, dlru, dq, dgl)],
        out_specs=pl.BlockSpec((N_DEV, tb, N_SHARD_IN), lambda i: (0, i, 0)),
        out_shape=jax.ShapeDtypeStruct((N_DEV, D, N_SHARD_IN), F32), compiler_params=_cp("parallel"),
    )(dssd, dlru, dq, dgl)


_BIG = (("w_in", "col", (1024, 1412)), ("mem_w_kv", "col", (1024, 256)), ("w_br_lru", "row", (128, 1024)),
        ("w_br_ssd", "row", (256, 1024)), ("w_br_xa", "row", (128, 1024)), ("w_out", "row", (128, 1024)),
        ("ffn_w_in", "col", (1024, 704)), ("ffn_w_down", "row", (352, 1024)))
_SMALL = (("b_gate", (3, 128)), ("lru_conv_w", (4, 128)), ("ssd_conv_w", (4, 384)))
_REP = (("lru_conv_b", (1024,)), ("lru_w_a", (8, 128, 128)), ("lru_b_a", (1024,)), ("lru_w_i", (8, 128, 128)),
        ("lru_b_i", (1024,)), ("lru_lambda", (1024,)), ("ssd_conv_b", (3072,)), ("ssd_dt_bias", (32,)),
        ("ssd_a_log", (32,)), ("ssd_d", (32,)), ("ssd_norm_w", (2048,)), ("ln1_g", (1024,)), ("ln1_b", (1024,)),
        ("ln2_g", (1024,)), ("ln2_b", (1024,)))
_ORDER = ("w_in", "b_gate", "lru_conv_w", "lru_conv_b", "lru_w_a", "lru_b_a", "lru_w_i", "lru_b_i", "lru_lambda",
          "ssd_conv_w", "ssd_conv_b", "ssd_dt_bias", "ssd_a_log", "ssd_d", "ssd_norm_w", "mem_w_kv", "w_br_lru",
          "w_br_ssd", "w_br_xa", "w_out", "ln1_g", "ln1_b", "ffn_w_in", "ffn_w_down", "ln2_g", "ln2_b")

LANES = 1024
N_SMALL = sum(DEPTH * s[0] * s[1] for _, s in _SMALL)
R_SMALL = 8
N_REP = sum(DEPTH * math.prod(s) for _, s in _REP)
R_REP = 68
R_SM = R_SMALL + R_REP + 4
R_TAIL = R_SMALL + N_DEV * R_REP
TB_TAIL = 184
assert N_SMALL <= R_SMALL * LANES and N_REP <= N_DEV * R_REP * LANES


def _rows(flat, rows):
    return jnp.pad(flat, (0, rows * LANES - flat.shape[0])).reshape(rows, LANES)


def _rowblk(a, cap):
    return max(b for b in range(16, cap + 1, 16) if a % b == 0)


def _pack_tail(d):
    small = jnp.concatenate([d[n].reshape(-1) for n, _ in _SMALL])
    rep = jnp.concatenate([d[n].reshape(-1) for n, _ in _REP])
    return jnp.concatenate([_rows(small, R_SMALL), _rows(rep, N_DEV * R_REP)], axis=0)


def _unpack_tail(a):
    out, o = {}, 0
    flat = a[:R_SMALL].reshape(-1)
    for n, s in _SMALL:
        k = DEPTH * math.prod(s)
        out[n] = flat[o:o + k].reshape((DEPTH,) + s)
        o += k
    flat, o = a[R_SMALL:].reshape(-1), 0
    for n, s in _REP:
        k = DEPTH * math.prod(s)
        out[n] = flat[o:o + k].reshape((DEPTH,) + s)
        o += k
    return out


def _by_dest(g):
    g = g.reshape(g.shape[:-1] + (N_DEV, g.shape[-1] // N_DEV))
    return jnp.moveaxis(g, -2, 0).reshape(N_DEV, -1)


def _from_stack(st):
    st = jnp.moveaxis(st, 0, -2)
    return st.reshape(st.shape[:-2] + (st.shape[-2] * st.shape[-1],))


def _layer_fwd(x, mem, w, l):
    nm = lambda s: f"{s}_l{l}"
    wi = w["wi"]
    row = lambda v: v.reshape(1, -1)
    s = dict(x=x, wi=wi)
    s["p_ssd"] = _mm(x, wi["ssd"], name=nm("proj_ssd"))
    s["p_lru"] = _mm(x, wi["lru"], name=nm("proj_lru"))
    s["p_q"] = _mm(x, wi["q"], name=nm("proj_q"))
    s["p_gl"] = _mm(x, wi["gl"], name=nm("proj_gl"))
    s["lru_par"] = (w["lru_conv_w"], row(w["lru_conv_b"]), w["lru_w_a"], row(w["lru_b_a"]), w["lru_w_i"],
                    row(w["lru_b_i"]), row(w["lru_lambda"]))
    s["y_lru"], s["h"] = _lru_fwd(s["p_lru"], *s["lru_par"], name=nm("lru_fwd"))
    s["ssd_par"] = (w["ssd_conv_w"], row(w["ssd_conv_b"]), row(w["ssd_dt_bias"]), row(w["ssd_a_log"]),
                    row(w["ssd_d"]), row(w["ssd_norm_w"]))
    s["y_ssd"], s["yraw"], s["hs"] = _ssd_fwd(s["p_ssd"], *s["ssd_par"], name=nm("ssd_fwd"))
    s["kv"] = _mm(mem, w["mem_w_kv"], name=nm("kv"))
    s["y_xa"] = _xa_fwd(s["p_q"], s["kv"], name=nm("xa_fwd"))
    s["b1"] = _mm(s["y_lru"], w["w_br_lru"], name=nm("br_lru"))
    s["b2"] = _mm(s["y_ssd"], w["w_br_ssd"], name=nm("br_ssd"))
    s["b3"] = _mm(s["y_xa"], w["w_br_xa"], name=nm("br_xa"))
    s["bg"] = row(w["b_gate"])
    s["merged"] = _merge_fwd(s["p_gl"], s["bg"], s["b1"], s["b2"], s["b3"], name=nm("merge_fwd"))
    s["mix"] = _mm(s["merged"], w["w_out"], name=nm("out_proj"))
    s["x1"] = _ln_fwd(x, s["mix"], row(w["ln1_g"]), row(w["ln1_b"]), name=nm("ln1_fwd"))
    s["gu"] = _mm(s["x1"], w["ffn_w_in"], name=nm("ffn_in"))
    s["act"] = _swiglu_fwd(s["gu"], name=nm("swiglu_fwd"))
    s["f"] = _mm(s["act"], w["ffn_w_down"], name=nm("ffn_down"))
    s["x2"] = _ln_fwd(s["x1"], s["f"], row(w["ln2_g"]), row(w["ln2_b"]), name=nm("ln2_fwd"))
    return s


def _layer_bwd(s, mem, w, dxo, l):
    nm = lambda t: f"{t}_l{l}"
    row = lambda v: v.reshape(1, -1)
    slabs = lambda a: a.reshape(N_DEV, a.shape[0] // N_DEV, a.shape[1])
    g = {}
    du2, dg, db = _ln_bwd(s["x1"], s["f"], dxo, row(w["ln2_g"]), name=nm("ln2_bwd"))
    g["ln2_g"], g["ln2_b"] = dg[0], db[0]
    dact = _mm(du2, w["ffn_w_down"], tb=True, name=nm("d_act"))
    g["ffn_w_down"] = slabs(_mm(s["act"], du2, ta=True, name=nm("dw_ffn_down")))
    dgu = _swiglu_bwd(s["gu"], dact, name=nm("swiglu_bwd"))
    dx1 = _mm(dgu, w["ffn_w_in"], tb=True, add=du2, add_scale=ALPHA, name=nm("d_x1"))
    g["ffn_w_in"] = _cols_split(_mm(s["x1"], dgu, ta=True, name=nm("dw_ffn_in")), name=nm("dw_ffn_in_shards"))
    du1, dg, db = _ln_bwd(s["x"], s["mix"], dx1, row(w["ln1_g"]), name=nm("ln1_bwd"))
    g["ln1_g"], g["ln1_b"] = dg[0], db[0]
    dmerged = _mm(du1, w["w_out"], tb=True, name=nm("d_merged"))
    g["w_out"] = slabs(_mm(s["merged"], du1, ta=True, name=nm("dw_out")))
    dp_gl, d1, d2, d3, dbg = _merge_bwd(s["p_gl"], s["bg"], s["b1"], s["b2"], s["b3"], dmerged, name=nm("merge_bwd"))
    g["b_gate"] = dbg.reshape(3, D)
    dy_lru = _mm(d1, w["w_br_lru"], tb=True, name=nm("d_y_lru"))
    g["w_br_lru"] = slabs(_mm(s["y_lru"], d1, ta=True, name=nm("dw_br_lru")))
    dy_ssd = _mm(d2, w["w_br_ssd"], tb=True, name=nm("d_y_ssd"))
    g["w_br_ssd"] = slabs(_mm(s["y_ssd"], d2, ta=True, name=nm("dw_br_ssd")))
    dy_xa = _mm(d3, w["w_br_xa"], tb=True, name=nm("d_y_xa"))
    g["w_br_xa"] = slabs(_mm(s["y_xa"], d3, ta=True, name=nm("dw_br_xa")))
    dp_q, dkv = _xa_bwd(s["p_q"], s["kv"], dy_xa, name=nm("xa_bwd"))
    g["mem_w_kv"] = _mm(mem, dkv, ta=True, split_n=2 * D // N_DEV, name=nm("dw_kv"))
    dp_ssd, dcw, dcb, ddtb, dalog, dd, dnw = _ssd_bwd(s["p_ssd"], s["yraw"], s["hs"], dy_ssd, *s["ssd_par"],
                                                      name=nm("ssd_bwd"))
    g["ssd_conv_w"], g["ssd_conv_b"], g["ssd_dt_bias"] = dcw, dcb[0], ddtb[0]
    g["ssd_a_log"], g["ssd_d"], g["ssd_norm_w"] = dalog[0], dd[0], dnw[0]
    dp_lru, dcw, dcb, dwa, dba, dwi, dbi, dlam = _lru_bwd(s["p_lru"], s["h"], dy_lru, *s["lru_par"], name=nm("lru_bwd"))
    g["lru_conv_w"], g["lru_conv_b"], g["lru_w_a"], g["lru_b_a"] = dcw, dcb[0], dwa, dba[0]
    g["lru_w_i"], g["lru_b_i"], g["lru_lambda"] = dwi, dbi[0], dlam[0]
    wi = s["wi"]
    dx = _mm(dp_ssd, wi["ssd"], tb=True, add=du1, add_scale=ALPHA, name=nm("dx_ssd"))
    dx = _mm(dp_lru, wi["lru"], tb=True, add=dx, name=nm("dx_lru"))
    dx = _mm(dp_q, wi["q"], tb=True, add=dx, name=nm("dx_q"))
    dx = _mm(dp_gl, wi["gl"], tb=True, add=dx, name=nm("dx_gl"))
    x = s["x"]
    g["w_in"] = _w_in_shards(_mm(x, dp_ssd, ta=True, name=nm("dw_in_ssd")), _mm(x, dp_lru, ta=True, name=nm("dw_in_lru")),
                             _mm(x, dp_q, ta=True, name=nm("dw_in_q")), _mm(x, dp_gl, ta=True, name=nm("dw_in_gl")),
                             name=nm("dw_in_shards"))
    return dx, g


def _local_step(x, mem, target, layers):
    saved = []
    for l in range(DEPTH):
        saved.append(_layer_fwd(x, mem, layers[l], l))
        x = saved[-1]["x2"]
    dx, loss = _loss_head(x, target, name="loss_head")
    grads = [None] * DEPTH
    for l in reversed(range(DEPTH)):
        dx, grads[l] = _layer_bwd(saved[l], mem, layers[l], dx, l)
    return loss, dx, grads


def kernel(x, mem, w_in, b_gate, lru_conv_w, lru_conv_b, lru_w_a, lru_b_a, lru_w_i, lru_b_i, lru_lambda, ssd_conv_w, ssd_conv_b, ssd_dt_bias, ssd_a_log, ssd_d, ssd_norm_w, mem_w_kv, w_br_lru, w_br_ssd, w_br_xa, w_out, ln1_g, ln1_b, ffn_w_in, ffn_w_down, ln2_g, ln2_b, loss_target, m_w_in, m_b_gate, m_lru_conv_w, m_lru_conv_b, m_lru_w_a, m_lru_b_a, m_lru_w_i, m_lru_b_i, m_lru_lambda, m_ssd_conv_w, m_ssd_conv_b, m_ssd_dt_bias, m_ssd_a_log, m_ssd_d, m_ssd_norm_w, m_mem_w_kv, m_w_br_lru, m_w_br_ssd, m_w_br_xa, m_w_out, m_ln1_g, m_ln1_b, m_ffn_w_in, m_ffn_w_down, m_ln2_g, m_ln2_b, v_w_in, v_b_gate, v_lru_conv_w, v_lru_conv_b, v_lru_w_a, v_lru_b_a, v_lru_w_i, v_lru_b_i, v_lru_lambda, v_ssd_conv_w, v_ssd_conv_b, v_ssd_dt_bias, v_ssd_a_log, v_ssd_d, v_ssd_norm_w, v_mem_w_kv, v_w_br_lru, v_w_br_ssd, v_w_br_xa, v_w_out, v_ln1_g, v_ln1_b, v_ffn_w_in, v_ffn_w_down, v_ln2_g, v_ln2_b):
    local = dict(locals())
    w = {n: local[n] for n in _ORDER}
    m = {n: local["m_" + n] for n in _ORDER}
    v = {n: local["v_" + n] for n in _ORDER}

    big = [n for n, _, _ in _BIG]
    kinds = {n: kind for n, kind, _ in _BIG}

    small = _rows(jnp.concatenate([w[n].reshape(-1) for n, _ in _SMALL]), R_SMALL)
    gathered = _all_gather([w[n].astype(BF16) for n in big] + [small], name="gather_weights")
    stacks = dict(zip(big, gathered[:-1]))
    small_all, o, small_full = gathered[-1].reshape(N_DEV, R_SMALL * LANES), 0, {}
    for n, s in _SMALL:
        k = DEPTH * s[0] * s[1]
        small_full[n] = _from_stack(small_all[:, o:o + k].reshape((N_DEV, DEPTH) + s))
        o += k
    layers = []
    for l in range(DEPTH):
        lw = {n: w[n][l] for n, _ in _REP}
        lw.update({n: small_full[n][l] for n, _ in _SMALL})
        lw["wi"] = dict(zip(("ssd", "lru", "q", "gl"), _w_in_regions(stacks["w_in"], l, name=f"w_in_regions_l{l}")))
        for n in big[1:]:
            if kinds[n] == "col":
                lw[n] = _cols_concat(stacks[n], l, name=f"full_{n}_l{l}")
            else:
                lw[n] = stacks[n][:, l].reshape(-1, stacks[n].shape[-1])
        layers.append(lw)

    loss_tile, dx, grads = _local_step(x[0], mem[0], loss_target[0], layers)
    loss = lax.psum(loss_tile[0, 0], ("x", "y", "c"))

    stacked = {n: jnp.stack([grads[l][n] for l in range(DEPTH)]) for n in [s[0] for s in _SMALL + _REP]}
    sm = jnp.concatenate([_by_dest(stacked[n]) for n, _ in _SMALL], axis=1)
    sm = jnp.pad(sm, ((0, 0), (0, R_SMALL * LANES - sm.shape[1])))
    rep = jnp.concatenate([stacked[n].reshape(-1) for n, _ in _REP])
    rep = jnp.pad(rep, (0, N_DEV * R_REP * LANES - rep.shape[0])).reshape(N_DEV, R_REP * LANES)
    tail = jnp.concatenate([sm, rep, jnp.zeros((N_DEV, (R_SM - R_SMALL - R_REP) * LANES), F32)], axis=1)
    owns = [grads[l][n].reshape((4, 2) + grads[l][n].shape[1:]) for l in range(DEPTH) for n in big]
    owns.append(tail.reshape(4, 2, R_SM, LANES))
    gots = _exchange_sibling(owns, name="reduce_cores")
    sums = [_pair_sum(own, got, name=f"pair_sum_{i}", out_dtype=BF16, tb=_rowblk(own.shape[2], 256))
            for i, (own, got) in enumerate(zip(owns[:-1], gots[:-1]))]
    sums.append(_pair_sum(owns[-1], gots[-1], name="pair_sum_tail", out_dtype=F32, tb=R_SM))
    slots = _exchange_chips(sums, name="reduce_chips")

    res = {}
    for i, n in enumerate(big):
        tb = _rowblk(w[n].shape[1], 128 if w[n].shape[2] > LANES else 256)
        res[n] = _adamw([slots[i], slots[len(big) + i]], w[n], m[n], v[n], name=f"adamw_{n}", tb=tb)
    tail_sum = _slot_sum(slots[-1], name="sum_tail")
    rep_all = _all_gather([tail_sum[R_SMALL:R_SMALL + R_REP]], name="gather_replicated")[0]
    g_tail = jnp.concatenate([tail_sum[:R_SMALL], rep_all.reshape(N_DEV * R_REP, LANES)], axis=0)
    tails = _adamw([g_tail[None]], _pack_tail(w)[None], _pack_tail(m)[None], _pack_tail(v)[None],
                   name="adamw_tail", tb=TB_TAIL)

    outs = []
    for kind in range(4):
        d = {**{n: res[n][kind] for n in big}, **_unpack_tail(tails[kind][0])}
        outs += [d[n] for n in _ORDER]
    return (loss, dx[None], *outs)
```

```python
import math

import jax
import jax.numpy as jnp
from jax import lax
from jax.experimental import pallas as pl
from jax.experimental.pallas import tpu as pltpu
from jax.experimental.pallas import tpu_sc as plsc

F32 = jnp.float32
BF16 = jnp.bfloat16

D = 1024
DEPTH = 2
N_DEV = 8
CHUNK = 64
LRU_BLOCKS = 8
LRU_BLOCK = 128
LRU_C = 8.0
D_SSD = 2 * D
SSD_HEADS = 32
SSD_GROUPS = 4
GROUP_W = D_SSD // SSD_GROUPS
SSD_STATE = 128
D_XBC = D_SSD + 2 * SSD_GROUPS * SSD_STATE
XA_HEADS = 4
XA_HEAD_DIM = 256
D_FF = 2816
ALPHA = (2 * DEPTH) ** 0.25
EPS = 1e-5
N_IN = 11296

S_Z, S_XBC, S_DT, W_SSD = 0, 2048, 5120, 5632
DT_REAL = 32

ADAM_LR, ADAM_B1, ADAM_B2, ADAM_EPS, ADAM_WD, ADAM_STEP = 0.001, 0.9, 0.999, 1e-08, 0.01, 10

VMEM_LIMIT = 56 * 1024 * 1024
MESH = pl.DeviceIdType.MESH
ANY = pl.BlockSpec(memory_space=pl.ANY)


def _cp(*sem):
    return pltpu.CompilerParams(dimension_semantics=sem, vmem_limit_bytes=VMEM_LIMIT)


def _blk(n, target):
    if n % 128:
        return n
    best = 128
    for b in range(128, min(n, target) + 1, 128):
        if n % b == 0:
            best = b
    return best


def _iota(shape, dim):
    return lax.broadcasted_iota(jnp.int32, shape, dim)


def _sigmoid(x):
    return 1.0 / (1.0 + jnp.exp(-x))


def _log1p(e):
    u = 1.0 + e
    return jnp.where(u == 1.0, e, jnp.log(u) * (e / (u - 1.0)))


def _softplus(x):
    return jnp.maximum(x, 0.0) + _log1p(jnp.exp(-jnp.abs(x)))


def _expm1(x):
    u = jnp.exp(x)
    um = u - 1.0
    return jnp.where(um == 0.0, x, jnp.where(um == -1.0, -1.0, um * (x / jnp.log(u))))


_G0 = math.sqrt(2.0 / math.pi)
_G1 = 0.044715


def _gelu_and_grad(x):
    t = jnp.tanh(_G0 * (x + _G1 * x * x * x))
    g = 0.5 * x * (1.0 + t)
    dg = 0.5 * (1.0 + t) + 0.5 * x * (1.0 - t * t) * (_G0 * (1.0 + 3.0 * _G1 * x * x))
    return g, dg


_NN = (((1,), (0,)), ((), ()))
_NT = (((1,), (1,)), ((), ()))
_TN = (((0,), (0,)), ((), ()))


def _dot(a, b, dims=_NN):
    return lax.dot_general(a.astype(BF16), b.astype(BF16), dims, preferred_element_type=F32)


def _dot_hi(a, b, dims=_NN):
    return lax.dot_general(a, b, dims, precision=lax.Precision.HIGHEST, preferred_element_type=F32)


def _split(v):
    hi = v.astype(BF16)
    return hi, (v - hi.astype(F32)).astype(BF16)


def _dot01(v, e, dims=_NN):
    hi, lo = _split(v)
    return (lax.dot_general(hi, e, dims, preferred_element_type=F32)
            + lax.dot_general(lo, e, dims, preferred_element_type=F32))


def _conv_taps(xe, n):
    return [xe[8:8 + n] if j == 3 else pltpu.roll(xe, 3 - j, 0)[8:8 + n] for j in range(4)]


def _conv_fwd(taps, cw, cb):
    return cb + cw[0:1] * taps[0] + cw[1:2] * taps[1] + cw[2:3] * taps[2] + cw[3:4] * taps[3]


def _conv_bwd(dc, dnext, x, cw, n):
    ext = jnp.concatenate([dc, dnext], axis=0)
    shifted = [pltpu.roll(ext, n + 8 - (3 - j), 0)[0:n] for j in range(3)] + [dc]
    dx = cw[0:1] * shifted[0] + cw[1:2] * shifted[1] + cw[2:3] * shifted[2] + cw[3:4] * dc
    dcw = jnp.concatenate([jnp.sum(x * shifted[j], axis=0, keepdims=True) for j in range(4)], axis=0)
    return dx, dcw, jnp.sum(dc, axis=0, keepdims=True)


MM_VMEM_BUDGET = 44 * 1024 * 1024
MM_MAX_TILE = 1408
MM_MAX_K = 5632


def _divisors(n, cap):
    return [n] if n % 128 else [b for b in range(128, min(n, cap) + 1, 128) if n % b == 0]


def _mm_tiles(m_dim, n_dim, k_dim, a_bytes, b_bytes, o_bytes, has_add, tn_fixed):
    best = None
    for tm in _divisors(m_dim, MM_MAX_TILE):
        for tn in ([tn_fixed] if tn_fixed else _divisors(n_dim, MM_MAX_TILE)):
            for tk in _divisors(k_dim, MM_MAX_K):
                vmem = 2 * (tm * tk * a_bytes + tk * tn * b_bytes + tm * tn * (o_bytes + (4 if has_add else 0)))
                vmem += tm * tn * 4 if tk < k_dim else 0
                if vmem <= MM_VMEM_BUDGET:
                    key = (tm * tn * tk, tk, tn)
                    if best is None or key > best[0]:
                        best = (key, (tm, tn, tk))
    assert best is not None, (m_dim, n_dim, k_dim)
    return best[1]


def _mm(a, b, *, ta=False, tb=False, out_dtype=F32, add=None, add_scale=1.0, name, split_n=None):
    if ta:
        k_dim, m_dim = a.shape
    else:
        m_dim, k_dim = a.shape
    if tb:
        n_dim, k2 = b.shape
    else:
        k2, n_dim = b.shape
    assert k_dim == k2, (a.shape, b.shape, ta, tb)
    tm, tn, tk = _mm_tiles(m_dim, n_dim, k_dim, a.dtype.itemsize, b.dtype.itemsize, jnp.dtype(out_dtype).itemsize,
                           add is not None, split_n)
    nk = k_dim // tk
    a_spec = pl.BlockSpec((tk, tm), lambda i, j, k: (k, i)) if ta else pl.BlockSpec((tm, tk), lambda i, j, k: (i, k))
    b_spec = pl.BlockSpec((tn, tk), lambda i, j, k: (j, k)) if tb else pl.BlockSpec((tk, tn), lambda i, j, k: (k, j))
    o_spec = pl.BlockSpec((tm, tn), lambda i, j, k: (i, j))
    out_shape = (m_dim, n_dim)
    if split_n is not None:
        assert add is None and tn == split_n, (tn, split_n)
        o_spec = pl.BlockSpec((None, tm, tn), lambda i, j, k: (j, i, 0))
        out_shape = (n_dim // tn, m_dim, tn)
    dims = (((0 if ta else 1,), (1 if tb else 0,)), ((), ()))
    has_add = add is not None

    def body(*refs):
        a_ref, b_ref = refs[:2]
        add_ref = refs[2] if has_add else None
        o_ref = refs[3] if has_add else refs[2]
        acc_ref = refs[-1] if nk > 1 else None
        k = pl.program_id(2)

        def product():
            return lax.dot_general(a_ref[...].astype(BF16), b_ref[...].astype(BF16), dims, preferred_element_type=F32)

        def finish(r):
            if has_add:
                r = r + add_scale * add_ref[...]
            o_ref[...] = r.astype(out_dtype)

        if nk == 1:
            finish(product())
            return

        @pl.when(k == 0)
        def _():
            acc_ref[...] = product()

        @pl.when((k > 0) & (k < nk - 1))
        def _():
            acc_ref[...] += product()

        @pl.when(k == nk - 1)
        def _():
            finish(acc_ref[...] + product())

    in_specs = [a_spec, b_spec] + ([o_spec] if has_add else [])
    args = (a, b) + ((add,) if has_add else ())
    return pl.pallas_call(
        body, name=name, grid=(m_dim // tm, n_dim // tn, nk),
        in_specs=in_specs, out_specs=o_spec,
        out_shape=jax.ShapeDtypeStruct(out_shape, out_dtype),
        scratch_shapes=[pltpu.VMEM((tm, tn), F32)] if nk > 1 else [],
        compiler_params=_cp("parallel", "parallel", "arbitrary"),
    )(*args)


def _ln_fwd(x, f, g, b, name, tb=512):
    t = x.shape[0]
    tb = min(tb, t)

    def body(x_ref, f_ref, g_ref, b_ref, o_ref):
        u = ALPHA * x_ref[...] + f_ref[...]
        mu = jnp.mean(u, axis=-1, keepdims=True)
        d = u - mu
        var = jnp.mean(d * d, axis=-1, keepdims=True)
        o_ref[...] = d * lax.rsqrt(var + EPS) * g_ref[...] + b_ref[...]

    row = pl.BlockSpec((tb, D), lambda i: (i, 0))
    par = pl.BlockSpec((1, D), lambda i: (0, 0))
    return pl.pallas_call(
        body, name=name, grid=(t // tb,), in_specs=[row, row, par, par], out_specs=row,
        out_shape=jax.ShapeDtypeStruct((t, D), F32), compiler_params=_cp("parallel"),
    )(x, f, g, b)


def _ln_bwd(x, f, dy, g, name, tb=512):
    t = x.shape[0]
    tb = min(tb, t)

    def body(x_ref, f_ref, dy_ref, g_ref, du_ref, dg_ref, db_ref):
        @pl.when(pl.program_id(0) == 0)
        def _():
            dg_ref[...] = jnp.zeros_like(dg_ref)
            db_ref[...] = jnp.zeros_like(db_ref)

        u = ALPHA * x_ref[...] + f_ref[...]
        mu = jnp.mean(u, axis=-1, keepdims=True)
        d = u - mu
        var = jnp.mean(d * d, axis=-1, keepdims=True)
        rstd = lax.rsqrt(var + EPS)
        xhat = d * rstd
        dy = dy_ref[...]
        dxh = dy * g_ref[...]
        m1 = jnp.mean(dxh, axis=-1, keepdims=True)
        m2 = jnp.mean(dxh * xhat, axis=-1, keepdims=True)
        du_ref[...] = rstd * (dxh - m1 - xhat * m2)
        dg_ref[...] += jnp.sum(dy * xhat, axis=0, keepdims=True)
        db_ref[...] += jnp.sum(dy, axis=0, keepdims=True)

    row = pl.BlockSpec((tb, D), lambda i: (i, 0))
    par = pl.BlockSpec((1, D), lambda i: (0, 0))
    return pl.pallas_call(
        body, name=name, grid=(t // tb,), in_specs=[row, row, row, par], out_specs=[row, par, par],
        out_shape=[jax.ShapeDtypeStruct((t, D), F32), jax.ShapeDtypeStruct((1, D), F32),
                   jax.ShapeDtypeStruct((1, D), F32)],
        compiler_params=_cp("arbitrary"),
    )(x, f, dy, g)


FFN_TM, FFN_TN = 512, D_FF // 2


def _ffn_in_swiglu(x, w, name):
    t = x.shape[0]
    tm = min(FFN_TM, t)
    nj = D_FF // FFN_TN

    def body(x_ref, wg_ref, wu_ref, g_ref, u_ref, a_ref):
        xb = x_ref[...].astype(BF16)
        g = lax.dot_general(xb, wg_ref[...], _NN, preferred_element_type=F32)
        u = lax.dot_general(xb, wu_ref[...], _NN, preferred_element_type=F32)
        g_ref[...] = g
        u_ref[...] = u
        a_ref[...] = (g * _sigmoid(g) * u).astype(BF16)

    tile = pl.BlockSpec((tm, FFN_TN), lambda i, j: (i, j))
    return pl.pallas_call(
        body, name=name, grid=(t // tm, nj),
        in_specs=[pl.BlockSpec((tm, D), lambda i, j: (i, 0)), pl.BlockSpec((D, FFN_TN), lambda i, j: (0, j)),
                  pl.BlockSpec((D, FFN_TN), lambda i, j: (0, nj + j))],
        out_specs=[tile, tile, tile],
        out_shape=[jax.ShapeDtypeStruct((t, D_FF), F32), jax.ShapeDtypeStruct((t, D_FF), F32),
                   jax.ShapeDtypeStruct((t, D_FF), BF16)],
        compiler_params=_cp("parallel", "parallel"),
    )(x, w, w)


def _d_swiglu(du, w_down, g, u, name):
    t = du.shape[0]
    tm = min(FFN_TM, t)

    def body(du_ref, w_ref, g_ref, u_ref, dg_ref, dup_ref):
        da = lax.dot_general(du_ref[...].astype(BF16), w_ref[...], _NT, preferred_element_type=F32)
        g_v = g_ref[...]
        s = _sigmoid(g_v)
        dg_ref[...] = (da * u_ref[...] * (s * (1.0 + g_v * (1.0 - s)))).astype(BF16)
        dup_ref[...] = (da * g_v * s).astype(BF16)

    tile = pl.BlockSpec((tm, FFN_TN), lambda i, j: (i, j))
    return pl.pallas_call(
        body, name=name, grid=(t // tm, D_FF // FFN_TN),
        in_specs=[pl.BlockSpec((tm, D), lambda i, j: (i, 0)), pl.BlockSpec((FFN_TN, D), lambda i, j: (j, 0)), tile, tile],
        out_specs=[tile, tile],
        out_shape=[jax.ShapeDtypeStruct((t, D_FF), BF16), jax.ShapeDtypeStruct((t, D_FF), BF16)],
        compiler_params=_cp("parallel", "parallel"),
    )(du, w_down, g, u)


def _merge_fwd(pgl, bg, b1, b2, b3, name, tb=512):
    t = pgl.shape[0]
    tb = min(tb, t)

    def body(gl_ref, bg_ref, b1_ref, b2_ref, b3_ref, o_ref):
        acc = None
        for j, b_ref in enumerate((b1_ref, b2_ref, b3_ref)):
            sl = slice(j * D, (j + 1) * D)
            term = _sigmoid(gl_ref[:, sl] + bg_ref[:, sl]) * b_ref[...]
            acc = term if acc is None else acc + term
        o_ref[...] = acc.astype(BF16)

    row = pl.BlockSpec((tb, D), lambda i: (i, 0))
    return pl.pallas_call(
        body, name=name, grid=(t // tb,),
        in_specs=[pl.BlockSpec((tb, 3 * D), lambda i: (i, 0)), pl.BlockSpec((1, 3 * D), lambda i: (0, 0)), row, row, row],
        out_specs=row, out_shape=jax.ShapeDtypeStruct((t, D), BF16), compiler_params=_cp("parallel"),
    )(pgl, bg, b1, b2, b3)


def _merge_bwd(pgl, bg, b1, b2, b3, dm, name, tb=512):
    t = pgl.shape[0]
    tb = min(tb, t)

    def body(gl_ref, bg_ref, b1_ref, b2_ref, b3_ref, dm_ref, dgl_ref, d1_ref, d2_ref, d3_ref, dbg_ref):
        @pl.when(pl.program_id(0) == 0)
        def _():
            dbg_ref[...] = jnp.zeros_like(dbg_ref)

        dm_v = dm_ref[...]
        for j, (b_ref, d_ref) in enumerate(((b1_ref, d1_ref), (b2_ref, d2_ref), (b3_ref, d3_ref))):
            sl = slice(j * D, (j + 1) * D)
            gate = _sigmoid(gl_ref[:, sl] + bg_ref[:, sl])
            d_ref[...] = (dm_v * gate).astype(BF16)
            dgl = dm_v * b_ref[...] * (gate * (1.0 - gate))
            dgl_ref[:, sl] = dgl.astype(BF16)
            dbg_ref[:, sl] += jnp.sum(dgl, axis=0, keepdims=True)

    row = pl.BlockSpec((tb, D), lambda i: (i, 0))
    wide = pl.BlockSpec((tb, 3 * D), lambda i: (i, 0))
    par = pl.BlockSpec((1, 3 * D), lambda i: (0, 0))
    return pl.pallas_call(
        body, name=name, grid=(t // tb,),
        in_specs=[wide, par, row, row, row, row], out_specs=[wide, row, row, row, par],
        out_shape=[jax.ShapeDtypeStruct((t, 3 * D), BF16)] + [jax.ShapeDtypeStruct((t, D), BF16)] * 3
                  + [jax.ShapeDtypeStruct((1, 3 * D), F32)],
        compiler_params=_cp("arbitrary"),
    )(pgl, bg, b1, b2, b3, dm)


def _xa_probs(q, kv_ref, hd):
    sl = slice(hd * XA_HEAD_DIM, (hd + 1) * XA_HEAD_DIM)
    k = kv_ref[:, sl]
    v = kv_ref[:, D + hd * XA_HEAD_DIM:D + (hd + 1) * XA_HEAD_DIM]
    s = _dot(q[:, sl], k, _NT) * (XA_HEAD_DIM ** -0.5)
    e = jnp.exp(s - jnp.max(s, axis=1, keepdims=True))
    return sl, k, v, e / jnp.sum(e, axis=1, keepdims=True)


def _xa_fwd(pq, kv, name, tb=512):
    t = pq.shape[0]
    tb = min(tb, t)

    def body(q_ref, kv_ref, o_ref):
        q = q_ref[...]
        for hd in range(XA_HEADS):
            sl, _, v, p = _xa_probs(q, kv_ref, hd)
            o_ref[:, sl] = _dot(p, v).astype(BF16)

    row = pl.BlockSpec((tb, D), lambda i: (i, 0))
    return pl.pallas_call(
        body, name=name, grid=(t // tb,),
        in_specs=[row, pl.BlockSpec(kv.shape, lambda i: (0, 0))], out_specs=row,
        out_shape=jax.ShapeDtypeStruct((t, D), BF16), compiler_params=_cp("parallel"),
    )(pq, kv)


def _xa_bwd(pq, kv, dy, name, tb=512):
    t = pq.shape[0]
    tb = min(tb, t)

    def body(q_ref, kv_ref, dy_ref, dq_ref, dkv_ref):
        @pl.when(pl.program_id(0) == 0)
        def _():
            dkv_ref[...] = jnp.zeros_like(dkv_ref)

        q = q_ref[...]
        for hd in range(XA_HEADS):
            sl, k, v, p = _xa_probs(q, kv_ref, hd)
            dyh = dy_ref[:, sl]
            vsl = slice(D + hd * XA_HEAD_DIM, D + (hd + 1) * XA_HEAD_DIM)
            dkv_ref[:, vsl] += _dot(p, dyh, _TN)
            dp = _dot(dyh, v, _NT)
            ds = p * (dp - jnp.sum(dp * p, axis=1, keepdims=True)) * (XA_HEAD_DIM ** -0.5)
            dq_ref[:, sl] = _dot(ds, k).astype(BF16)
            dkv_ref[:, sl] += _dot(ds, q[:, sl], _TN)

    row = pl.BlockSpec((tb, D), lambda i: (i, 0))
    kvs = pl.BlockSpec(kv.shape, lambda i: (0, 0))
    return pl.pallas_call(
        body, name=name, grid=(t // tb,), in_specs=[row, kvs, row], out_specs=[row, kvs],
        out_shape=[jax.ShapeDtypeStruct((t, D), BF16), jax.ShapeDtypeStruct(kv.shape, F32)],
        compiler_params=_cp("arbitrary"),
    )(pq, kv, dy)


def _scan_fwd(a, u):
    n = a.shape[0]
    row = _iota((n, 1), 0)
    d = 1
    while d < n:
        us = jnp.where(row >= d, pltpu.roll(u, d, 0), 0.0)
        u = a * us + u
        a = a * pltpu.roll(a, d, 0)
        d *= 2
    return u


def _scan_rev(b, u):
    n = b.shape[0]
    row = _iota((n, 1), 0)
    d = 1
    while d < n:
        us = jnp.where(row < n - d, pltpu.roll(u, n - d, 0), 0.0)
        u = b * us + u
        b = b * pltpu.roll(b, n - d, 0)
        d *= 2
    return u


def _lru_gates(xc, wa_ref, ba, wi_ref, bi, lam):
    za = jnp.concatenate([_dot(xc[:, n * 128:(n + 1) * 128], wa_ref[n]) for n in range(LRU_BLOCKS)], axis=1) + ba
    zi = jnp.concatenate([_dot(xc[:, n * 128:(n + 1) * 128], wi_ref[n]) for n in range(LRU_BLOCKS)], axis=1) + bi
    r = _sigmoid(za)
    ig = _sigmoid(zi)
    sp = _softplus(-lam)
    log_a = (-LRU_C) * r * sp
    a = jnp.exp(log_a)
    m = jnp.sqrt(-_expm1(2.0 * log_a))
    u = m * (ig * xc)
    return a, u, r, ig, m, sp


def _lru_fwd(p, cw, cb, wa, ba, wi, bi, lam, name, tb=256):
    t = p.shape[0]
    tb = min(tb, t)
    nb = t // tb
    r8 = tb // 8

    def body(x_ref, xp_ref, g_ref, cw_ref, cb_ref, wa_ref, ba_ref, wi_ref, bi_ref, lam_ref, y_ref, h_ref, xc_ref,
             hc_ref):
        i = pl.program_id(0)

        @pl.when(i == 0)
        def _():
            hc_ref[...] = jnp.zeros_like(hc_ref)

        halo = jnp.where(i == 0, 0.0, xp_ref[...])
        taps = _conv_taps(jnp.concatenate([halo, x_ref[...]], axis=0), tb)
        xc = _conv_fwd(taps, cw_ref[...], cb_ref[...])
        xc_ref[...] = xc
        a, u, _, _, _, _ = _lru_gates(xc, wa_ref, ba_ref[...], wi_ref, bi_ref[...], lam_ref[...])
        row = _iota((tb, 1), 0)
        u = u + jnp.where(row == 0, a * hc_ref[...], 0.0)
        h = _scan_fwd(a, u)
        h_ref[...] = h
        hc_ref[...] = h[tb - 1:tb, :]
        gl, _ = _gelu_and_grad(g_ref[...])
        y_ref[...] = (gl * h).astype(BF16)

    par = pl.BlockSpec((1, D), lambda i: (0, 0))
    wsp = pl.BlockSpec((LRU_BLOCKS, LRU_BLOCK, LRU_BLOCK), lambda i: (0, 0, 0))
    row = pl.BlockSpec((tb, D), lambda i: (i, 0))
    return pl.pallas_call(
        body, name=name, grid=(nb,),
        in_specs=[row, pl.BlockSpec((8, D), lambda i: (jnp.maximum(i * r8 - 1, 0), 0)),
                  pl.BlockSpec((tb, D), lambda i: (i, 1)),
                  pl.BlockSpec((4, D), lambda i: (0, 0)), par, wsp, par, wsp, par, par],
        out_specs=[row, row, row],
        out_shape=[jax.ShapeDtypeStruct((t, D), BF16), jax.ShapeDtypeStruct((t, D), F32),
                   jax.ShapeDtypeStruct((t, D), F32)],
        scratch_shapes=[pltpu.VMEM((1, D), F32)],
        compiler_params=_cp("arbitrary"),
    )(p, p, p, cw, cb, wa, ba, wi, bi, lam)


def _lru_bwd(p, xc, h, dy, cw, wa, ba, wi, bi, lam, name, tb=256):
    t = p.shape[0]
    tb = min(tb, t)
    nb = t // tb
    r8 = tb // 8

    def body(x_ref, g_ref, xc_ref, h_ref, hp_ref, dy_ref, cw_ref, wa_ref, ba_ref, wi_ref, bi_ref, lam_ref,
             dp_ref, dcw_ref, dcb_ref, dwa_ref, dba_ref, dwi_ref, dbi_ref, dlam_ref, carry_ref, dnext_ref):
        i = pl.program_id(0)
        blk = nb - 1 - i

        @pl.when(i == 0)
        def _():
            for r in (dcw_ref, dcb_ref, dwa_ref, dba_ref, dwi_ref, dbi_ref, dlam_ref, carry_ref, dnext_ref):
                r[...] = jnp.zeros_like(r)

        xc = xc_ref[...]
        lam = lam_ref[...]
        a, _, r, ig, m, sp = _lru_gates(xc, wa_ref, ba_ref[...], wi_ref, bi_ref[...], lam)
        gl, dgl = _gelu_and_grad(g_ref[...])
        h = h_ref[...]
        dy = dy_ref[...]
        dp_ref[:, D:] = (dy * h * dgl).astype(BF16)
        row = _iota((tb, 1), 0)
        dh = dy * gl + jnp.where(row == tb - 1, carry_ref[...], 0.0)
        b = jnp.where(row < tb - 1, pltpu.roll(a, tb - 1, 0), 0.0)
        gs = _scan_rev(b, dh)
        carry_ref[...] = a[0:1] * gs[0:1]
        h_last = jnp.where(blk == 0, 0.0, hp_ref[7:8, :])
        hprev = jnp.where(row == 0, h_last, pltpu.roll(h, 1, 0))
        da = gs * hprev
        dm = gs * ig * xc
        di = gs * m * xc
        dxc = gs * m * ig
        dlog = (0.5 * dm / m) * (-2.0 * a * a) + da * a
        dr = dlog * ((-LRU_C) * sp)
        dsp = jnp.sum(dlog * ((-LRU_C) * r), axis=0, keepdims=True)
        dlam_ref[...] += dsp * (-_sigmoid(-lam))
        dza = dr * r * (1.0 - r)
        dzi = di * ig * (1.0 - ig)
        dba_ref[...] += jnp.sum(dza, axis=0, keepdims=True)
        dbi_ref[...] += jnp.sum(dzi, axis=0, keepdims=True)
        parts = []
        for n in range(LRU_BLOCKS):
            sl = slice(n * 128, (n + 1) * 128)
            dwa_ref[n] += _dot(xc[:, sl], dza[:, sl], _TN)
            dwi_ref[n] += _dot(xc[:, sl], dzi[:, sl], _TN)
            parts.append(_dot(dza[:, sl], wa_ref[n], _NT) + _dot(dzi[:, sl], wi_ref[n], _NT))
        dxc = dxc + jnp.concatenate(parts, axis=1)
        dx, dcw, dcb = _conv_bwd(dxc, dnext_ref[...], x_ref[...], cw_ref[...], tb)
        dp_ref[:, :D] = dx.astype(BF16)
        dcw_ref[...] += dcw
        dcb_ref[...] += dcb
        dnext_ref[...] = dxc[0:8]

    par = pl.BlockSpec((1, D), lambda i: (0, 0))
    wsp = pl.BlockSpec((LRU_BLOCKS, LRU_BLOCK, LRU_BLOCK), lambda i: (0, 0, 0))
    cws = pl.BlockSpec((4, D), lambda i: (0, 0))
    rev = lambda i: nb - 1 - i
    blk0 = pl.BlockSpec((tb, D), lambda i: (rev(i), 0))
    w_shape = jax.ShapeDtypeStruct((LRU_BLOCKS, LRU_BLOCK, LRU_BLOCK), F32)
    v_shape = jax.ShapeDtypeStruct((1, D), F32)
    return pl.pallas_call(
        body, name=name, grid=(nb,),
        in_specs=[blk0, pl.BlockSpec((tb, D), lambda i: (rev(i), 1)), blk0, blk0,
                  pl.BlockSpec((8, D), lambda i: (jnp.maximum(rev(i) * r8 - 1, 0), 0)), blk0,
                  cws, wsp, par, wsp, par, par],
        out_specs=[pl.BlockSpec((tb, 2 * D), lambda i: (rev(i), 0)), cws, par, wsp, par, wsp, par, par],
        out_shape=[jax.ShapeDtypeStruct((t, 2 * D), BF16), jax.ShapeDtypeStruct((4, D), F32), v_shape,
                   w_shape, v_shape, w_shape, v_shape, v_shape],
        scratch_shapes=[pltpu.VMEM((1, D), F32), pltpu.VMEM((8, D), F32)],
        compiler_params=_cp("arbitrary"),
    )(p, p, xc, h, h, dy, cw, wa, ba, wi, bi, lam)


def _ssd_consts():
    m0 = _iota((1, 128), 1) < 64
    e = (jnp.right_shift(_iota((SSD_HEADS, D_SSD), 1), 6) == _iota((SSD_HEADS, D_SSD), 0)).astype(BF16)
    tril = (_iota((CHUNK, CHUNK), 0) >= _iota((CHUNK, CHUNK), 1)).astype(F32)
    eye = (_iota((SSD_HEADS, SSD_HEADS), 0) == _iota((SSD_HEADS, SSD_HEADS), 1)).astype(F32)
    r2 = _iota((CHUNK, 128), 0)
    c2 = jnp.bitwise_and(_iota((CHUNK, 128), 1), 63)
    return dict(m0=m0, e=e, tril=tril, eye=eye, causal2=r2 >= c2, fold=(c2 == r2).astype(BF16))


def _ssd_pre(c, p_ref, dtb_ref, alog_ref, dvec_ref, k):
    sg = _sigmoid(c)
    xbc = c * sg
    dtp = p_ref[:, S_DT:S_DT + DT_REAL] + dtb_ref[...]
    dt = _softplus(dtp)
    a = -jnp.exp(alog_ref[...])
    cs = _dot_hi(k["tril"], dt * a)
    cs_last = cs[CHUNK - 1:CHUNK]
    dend = jnp.exp(cs_last - cs)
    cdec = jnp.exp(cs_last)
    big = _dot01(jnp.concatenate([dt, jnp.exp(cs), dend], axis=0), k["e"])
    small = _dot01(jnp.concatenate([jnp.broadcast_to(cdec, (8, SSD_HEADS)),
                                    jnp.broadcast_to(dvec_ref[...], (8, SSD_HEADS))], axis=0), k["e"])
    cst2 = _dot_hi(k["eye"], jnp.concatenate([cs, cs], axis=0), _NT)
    return dict(c=c, sg=sg, xs=xbc[:, :D_SSD], bm=xbc[:, D_SSD:D_SSD + 512],
                cm=xbc[:, D_SSD + 512:], dtp=dtp, dt=dt, a=a, cs=cs, dend=dend, cdec=cdec,
                dtx=big[0:CHUNK], ecx=big[CHUNK:2 * CHUNK], dex=big[2 * CHUNK:3 * CHUNK],
                cdx=small[0:1], ddx=small[8:9], cst2=cst2)


def _pair_decay(p, cs, cst2, k):
    h0, h1 = 2 * p, 2 * p + 1
    colp = jnp.where(k["m0"], cs[:, h0:h0 + 1], cs[:, h1:h1 + 1])
    rowp = jnp.where(k["m0"], cst2[h0:h0 + 1, :], cst2[h1:h1 + 1, :])
    return jnp.where(k["causal2"], jnp.exp(colp - rowp), 0.0)


def _pair_stack(xp, k):
    return jnp.concatenate([jnp.where(k["m0"], xp, 0.0), jnp.where(k["m0"], 0.0, xp)], axis=0)


def _group_norm(yz, nw, with_stats=False):
    outs, stats = [], []
    for g in range(SSD_GROUPS):
        yzg = yz[:, g * GROUP_W:(g + 1) * GROUP_W]
        r = lax.rsqrt(jnp.mean(yzg * yzg, axis=1, keepdims=True) + EPS)
        outs.append(yzg * r)
        stats.append(r)
    y = jnp.concatenate(outs, axis=1) * nw
    return (y, stats) if with_stats else y


def _ssd_fwd(p, cw, cb, dtb, alog, dvec, nw, name):
    t = p.shape[0]
    nc = t // CHUNK

    def body(p_ref, pp_ref, cw_ref, cb_ref, dtb_ref, alog_ref, dvec_ref, nw_ref, y_ref, yraw_ref, hs_ref, c_ref,
             h_scr):
        i = pl.program_id(0)

        @pl.when(i == 0)
        def _():
            h_scr[...] = jnp.zeros_like(h_scr)

        k = _ssd_consts()
        halo = jnp.where(i == 0, 0.0, pp_ref[:, S_XBC:S_DT])
        taps = _conv_taps(jnp.concatenate([halo, p_ref[:, S_XBC:S_DT]], axis=0), CHUNK)
        c = _conv_fwd(taps, cw_ref[...], cb_ref[...])
        c_ref[...] = c
        s = _ssd_pre(c, p_ref, dtb_ref, alog_ref, dvec_ref, k)
        xs, bm, cm = s["xs"], s["bm"], s["cm"]
        xdt = xs * s["dtx"]
        hprev = h_scr[...]
        hs_ref[0] = hprev
        ys, hn = [], []
        for g in range(SSD_GROUPS):
            gs = slice(g * GROUP_W, (g + 1) * GROUP_W)
            bg = bm[:, g * 128:(g + 1) * 128]
            cg = cm[:, g * 128:(g + 1) * 128]
            cbdup = _dot(cg, jnp.concatenate([bg, bg], axis=0), _NT)
            hp_g = hprev[:, gs]
            yd = []
            for q in range(4):
                pr = g * 4 + q
                mp = cbdup * _pair_decay(pr, s["cs"], s["cst2"], k)
                yd.append(_dot(mp, _pair_stack(xdt[:, pr * 128:(pr + 1) * 128], k)))
            ys.append(jnp.concatenate(yd, axis=1) + _dot(cg, hp_g) * s["ecx"][:, gs])
            hn.append(hp_g * s["cdx"][:, gs] + _dot(bg, xdt[:, gs] * s["dex"][:, gs], _TN))
        h_scr[...] = jnp.concatenate(hn, axis=1)
        yraw = jnp.concatenate(ys, axis=1) + s["ddx"] * xs
        yraw_ref[...] = yraw
        z = p_ref[:, S_Z:S_Z + D_SSD]
        y_ref[...] = _group_norm(yraw * (z * _sigmoid(z)), nw_ref[...]).astype(BF16)

    hv = pl.BlockSpec((1, DT_REAL), lambda i: (0, 0))
    return pl.pallas_call(
        body, name=name, grid=(nc,),
        in_specs=[pl.BlockSpec((CHUNK, W_SSD), lambda i: (i, 0)),
                  pl.BlockSpec((8, W_SSD), lambda i: (jnp.maximum(i * (CHUNK // 8) - 1, 0), 0)),
                  pl.BlockSpec((4, D_XBC), lambda i: (0, 0)), pl.BlockSpec((1, D_XBC), lambda i: (0, 0)),
                  hv, hv, hv, pl.BlockSpec((1, D_SSD), lambda i: (0, 0))],
        out_specs=[pl.BlockSpec((CHUNK, D_SSD), lambda i: (i, 0)), pl.BlockSpec((CHUNK, D_SSD), lambda i: (i, 0)),
                   pl.BlockSpec((1, SSD_STATE, D_SSD), lambda i: (i, 0, 0)),
                   pl.BlockSpec((CHUNK, D_XBC), lambda i: (i, 0))],
        out_shape=[jax.ShapeDtypeStruct((t, D_SSD), BF16), jax.ShapeDtypeStruct((t, D_SSD), F32),
                   jax.ShapeDtypeStruct((nc, SSD_STATE, D_SSD), F32), jax.ShapeDtypeStruct((t, D_XBC), F32)],
        scratch_shapes=[pltpu.VMEM((SSD_STATE, D_SSD), F32)],
        compiler_params=_cp("arbitrary"),
    )(p, p, cw, cb, dtb, alog, dvec, nw)


def _ssd_bwd(p, c, yraw, hs, dy, cw, dtb, alog, dvec, nw, name):
    t = p.shape[0]
    nc = t // CHUNK

    def body(p_ref, c_ref, yraw_ref, hs_ref, dy_ref, cw_ref, dtb_ref, alog_ref, dvec_ref, nw_ref,
             dp_ref, dcw_ref, dcb_ref, ddtb_ref, dalog_ref, dd_ref, dnw_ref, dh_scr, dnext_scr):
        i = pl.program_id(0)

        @pl.when(i == 0)
        def _():
            for r in (dcw_ref, dcb_ref, ddtb_ref, dalog_ref, dd_ref, dnw_ref, dh_scr, dnext_scr):
                r[...] = jnp.zeros_like(r)

        k = _ssd_consts()
        s = _ssd_pre(c_ref[...], p_ref, dtb_ref, alog_ref, dvec_ref, k)
        xs, bm, cm, cs, dt, a = s["xs"], s["bm"], s["cm"], s["cs"], s["dt"], s["a"]
        m0 = k["m0"]
        xdt = xs * s["dtx"]
        hprev = hs_ref[0]
        dh = dh_scr[...]

        nw_v = nw_ref[...]
        yraw = yraw_ref[...]
        z = p_ref[:, S_Z:S_Z + D_SSD]
        sz = _sigmoid(z)
        siluz = z * sz
        yz = yraw * siluz
        dyo = dy_ref[...]
        dyn = dyo * nw_v
        dyz_parts, dnw_parts = [], []
        for g in range(SSD_GROUPS):
            gs = slice(g * GROUP_W, (g + 1) * GROUP_W)
            yzg = yz[:, gs]
            r = lax.rsqrt(jnp.mean(yzg * yzg, axis=1, keepdims=True) + EPS)
            dnw_parts.append(jnp.sum(dyo[:, gs] * yzg * r, axis=0, keepdims=True))
            dyz_parts.append(r * dyn[:, gs] - yzg * (r * r * r) * jnp.mean(dyn[:, gs] * yzg, axis=1, keepdims=True))
        dnw_ref[...] += jnp.concatenate(dnw_parts, axis=1)
        dyz = jnp.concatenate(dyz_parts, axis=1)
        d_y = dyz * siluz
        dp_ref[:, S_Z:S_Z + D_SSD] = (dyz * yraw * (sz * (1.0 + z * (1.0 - sz)))).astype(BF16)
        dd_row = jnp.sum(d_y * xs, axis=0, keepdims=True)
        dxs = d_y * s["ddx"]

        lane_h = _iota((1, SSD_HEADS), 1)
        sub_h = _iota((SSD_HEADS, 1), 0)
        dcs = jnp.zeros((CHUNK, SSD_HEADS), F32)
        dcst2 = jnp.zeros((SSD_HEADS, 128), F32)
        dxdt_parts, db_parts, dc_parts, dhp_parts, yoff_parts, dend_parts, dcd_parts = [], [], [], [], [], [], []
        for g in range(SSD_GROUPS):
            gs = slice(g * GROUP_W, (g + 1) * GROUP_W)
            bg = bm[:, g * 128:(g + 1) * 128]
            cg = cm[:, g * 128:(g + 1) * 128]
            bdup = jnp.concatenate([bg, bg], axis=0)
            cbdup = _dot(cg, bdup, _NT)
            dcb2 = jnp.zeros((CHUNK, 128), F32)
            dxp_parts = []
            for q in range(4):
                pr = g * 4 + q
                h0, h1 = 2 * pr, 2 * pr + 1
                lp = _pair_decay(pr, cs, s["cst2"], k)
                mp = cbdup * lp
                xst = _pair_stack(xdt[:, pr * 128:(pr + 1) * 128], k)
                dyp = d_y[:, pr * 128:(pr + 1) * 128]
                dmp = _dot(dyp, xst, _NT)
                dxst = _dot(mp, dyp, _TN)
                dxp_parts.append(jnp.where(m0, dxst[:CHUNK], dxst[CHUNK:]))
                dcb2 = dcb2 + dmp * lp
                dlm = dmp * mp
                rs0 = jnp.sum(jnp.where(m0, dlm, 0.0), axis=1, keepdims=True)
                rs1 = jnp.sum(jnp.where(m0, 0.0, dlm), axis=1, keepdims=True)
                dcs = dcs + jnp.where(lane_h == h0, rs0, 0.0) + jnp.where(lane_h == h1, rs1, 0.0)
                colsum = jnp.sum(dlm, axis=0, keepdims=True)
                sel = ((sub_h == h0) & m0) | ((sub_h == h1) & jnp.logical_not(m0))
                dcst2 = dcst2 - jnp.where(sel, colsum, 0.0)
            dcg = _dot(dcb2, bdup)
            dbdup = _dot(dcb2, cg, _TN)
            dbg = dbdup[:CHUNK] + dbdup[CHUNK:]
            hp_g = hprev[:, gs]
            zoff = _dot(cg, hp_g)
            dzo = d_y[:, gs] * s["ecx"][:, gs]
            dcg = dcg + _dot(dzo, hp_g, _NT)
            dh_g = dh[:, gs]
            dhp_parts.append(_dot(cg, dzo, _TN) + dh_g * s["cdx"][:, gs])
            dcd_parts.append(jnp.sum(dh_g * hp_g, axis=0, keepdims=True))
            wg = xdt[:, gs] * s["dex"][:, gs]
            dbg = dbg + _dot(wg, dh_g, _NT)
            dwg = _dot(bg, dh_g)
            dxdt_parts.append(jnp.concatenate(dxp_parts, axis=1) + dwg * s["dex"][:, gs])
            dend_g = dwg * wg
            dend_parts.append(jnp.sum(dend_g, axis=0, keepdims=True))
            yoff_parts.append(dzo * zoff - dend_g)
            db_parts.append(dbg)
            dc_parts.append(dcg)
        dh_scr[...] = jnp.concatenate(dhp_parts, axis=1)
        dxdt = jnp.concatenate(dxdt_parts, axis=1)
        sums = _dot01(jnp.concatenate([jnp.concatenate(yoff_parts, axis=1), dxdt * xs], axis=0), k["e"], _NT)
        rows8 = jnp.concatenate([jnp.broadcast_to(jnp.concatenate(r, axis=1), (8, D_SSD))
                                 for r in (dcd_parts, [dd_row], dend_parts)], axis=0)
        small = _dot01(rows8, k["e"], _NT)
        dd_ref[...] += small[8:9]
        dcs_last = small[0:1] * s["cdec"] + small[16:17]
        hi, lo = _split(dcst2)
        dcs = (dcs + sums[0:CHUNK]
               + lax.dot_general(k["fold"], hi, _NT, preferred_element_type=F32)
               + lax.dot_general(k["fold"], lo, _NT, preferred_element_type=F32)
               + jnp.where(_iota((CHUNK, 1), 0) == CHUNK - 1, dcs_last, 0.0))
        dda = _dot_hi(k["tril"], dcs, _TN)
        ddt = dda * a + sums[CHUNK:2 * CHUNK]
        dalog_ref[...] += jnp.sum(dda * dt, axis=0, keepdims=True) * a
        dxs = dxs + dxdt * s["dtx"]
        draw = ddt * _sigmoid(s["dtp"])
        ddtb_ref[...] += jnp.sum(draw, axis=0, keepdims=True)
        dp_ref[:, S_DT:] = jnp.zeros((CHUNK, W_SSD - S_DT), BF16)
        dp_ref[:, S_DT:S_DT + DT_REAL] = draw.astype(BF16)
        dxbc = jnp.concatenate([dxs] + db_parts + dc_parts, axis=1)
        sg, c = s["sg"], s["c"]
        dc = dxbc * (sg * (1.0 + c * (1.0 - sg)))
        dx, dcw, dcb = _conv_bwd(dc, dnext_scr[...], p_ref[:, S_XBC:S_DT], cw_ref[...], CHUNK)
        dp_ref[:, S_XBC:S_DT] = dx.astype(BF16)
        dcw_ref[...] += dcw
        dcb_ref[...] += dcb
        dnext_scr[...] = dc[0:8]

    rev = lambda i: nc - 1 - i
    hv = pl.BlockSpec((1, DT_REAL), lambda i: (0, 0))
    cws = pl.BlockSpec((4, D_XBC), lambda i: (0, 0))
    cbs = pl.BlockSpec((1, D_XBC), lambda i: (0, 0))
    nws = pl.BlockSpec((1, D_SSD), lambda i: (0, 0))
    wide = pl.BlockSpec((CHUNK, D_SSD), lambda i: (rev(i), 0))
    hshape = jax.ShapeDtypeStruct((1, DT_REAL), F32)
    return pl.pallas_call(
        body, name=name, grid=(nc,),
        in_specs=[pl.BlockSpec((CHUNK, W_SSD), lambda i: (rev(i), 0)),
                  pl.BlockSpec((CHUNK, D_XBC), lambda i: (rev(i), 0)),
                  wide, pl.BlockSpec((1, SSD_STATE, D_SSD), lambda i: (rev(i), 0, 0)), wide,
                  cws, hv, hv, hv, nws],
        out_specs=[pl.BlockSpec((CHUNK, W_SSD), lambda i: (rev(i), 0)), cws, cbs, hv, hv, hv, nws],
        out_shape=[jax.ShapeDtypeStruct((t, W_SSD), BF16), jax.ShapeDtypeStruct((4, D_XBC), F32),
                   jax.ShapeDtypeStruct((1, D_XBC), F32), hshape, hshape, hshape,
                   jax.ShapeDtypeStruct((1, D_SSD), F32)],
        scratch_shapes=[pltpu.VMEM((SSD_STATE, D_SSD), F32), pltpu.VMEM((8, D_XBC), F32)],
        compiler_params=_cp("arbitrary"),
    )(p, c, yraw, hs, dy, cw, dtb, alog, dvec, nw)


def _loss_head(y, target, name, tb=512):
    t = y.shape[0]
    tb = min(tb, t)

    def body(y_ref, t_ref, dy_ref, l_ref):
        @pl.when(pl.program_id(0) == 0)
        def _():
            l_ref[...] = jnp.zeros_like(l_ref)

        e = y_ref[...] - t_ref[...]
        dy_ref[...] = e * (1.0 / D)
        l_ref[...] += jnp.sum(jnp.sum(e * e, axis=1, keepdims=True), axis=0, keepdims=True) * (0.5 / D)

    row = pl.BlockSpec((tb, D), lambda i: (i, 0))
    return pl.pallas_call(
        body, name=name, grid=(t // tb,), in_specs=[row, row],
        out_specs=[row, pl.BlockSpec((8, 128), lambda i: (0, 0))],
        out_shape=[jax.ShapeDtypeStruct((t, D), F32), jax.ShapeDtypeStruct((8, 128), F32)],
        compiler_params=_cp("arbitrary"),
    )(y, target)


def _adamw(slots, w, m, v, name, tb):
    nl = len(slots)
    ns, r, c = slots[0].shape
    assert r % tb == 0 and w.shape == (nl, r, c), (r, tb, w.shape)

    def body(*refs):
        s_refs = refs[:nl]
        w_ref, m_ref, v_ref, g_ref, d_ref, m2_ref, v2_ref = refs[nl:]

        def total(ref):
            acc = ref[0].astype(F32)
            for j in range(1, ns):
                acc = acc + ref[j].astype(F32)
            return acc

        g = total(s_refs[0])
        for layer in range(1, nl):
            g = jnp.where(pl.program_id(0) == layer, total(s_refs[layer]), g)
        m2 = ADAM_B1 * m_ref[...] + (1.0 - ADAM_B1) * g
        v2 = ADAM_B2 * v_ref[...] + (1.0 - ADAM_B2) * (g * g)
        m_hat = m2 / (1.0 - ADAM_B1 ** ADAM_STEP)
        v_hat = v2 / (1.0 - ADAM_B2 ** ADAM_STEP)
        g_ref[...] = g
        d_ref[...] = -ADAM_LR * (m_hat / (jnp.sqrt(v_hat) + ADAM_EPS) + ADAM_WD * w_ref[...])
        m2_ref[...] = m2
        v2_ref[...] = v2

    def slot_spec(layer):
        return pl.BlockSpec((ns, tb, c), lambda l, i: (0, jnp.where(l == layer, i, 0), 0))

    row = pl.BlockSpec((None, tb, c), lambda l, i: (l, i, 0))
    shp = jax.ShapeDtypeStruct((nl, r, c), F32)
    return pl.pallas_call(
        body, name=name, grid=(nl, r // tb),
        in_specs=[slot_spec(layer) for layer in range(nl)] + [row, row, row],
        out_specs=[row, row, row, row], out_shape=[shp, shp, shp, shp], compiler_params=_cp("arbitrary", "arbitrary"),
    )(*slots, w, m, v)


def _pair_sum(own, got, name, out_dtype, tb):
    nj, _, r, c = own.shape
    mc = lax.axis_index("c")

    def body(mc_ref, a_ref, b_ref, o_ref):
        del mc_ref
        o_ref[...] = (a_ref[...] + b_ref[...]).astype(out_dtype)

    return pl.pallas_call(
        body, name=name,
        grid_spec=pltpu.PrefetchScalarGridSpec(
            num_scalar_prefetch=1, grid=(nj, r // tb),
            in_specs=[pl.BlockSpec((None, None, tb, c), lambda j, i, mc_ref: (j, mc_ref[0], i, 0)),
                      pl.BlockSpec((None, tb, c), lambda j, i, mc_ref: (j, i, 0))],
            out_specs=pl.BlockSpec((None, tb, c), lambda j, i, mc_ref: (j, i, 0))),
        out_shape=jax.ShapeDtypeStruct((nj, r, c), out_dtype), compiler_params=_cp("parallel", "parallel"),
    )(jnp.reshape(mc, (1,)).astype(jnp.int32), own, got)


def _slot_sum(slots, name):
    ns, r, c = slots.shape

    def body(s_ref, o_ref):
        g = s_ref[0]
        for j in range(1, ns):
            g = g + s_ref[j]
        o_ref[...] = g

    return pl.pallas_call(body, name=name, out_shape=jax.ShapeDtypeStruct((r, c), F32))(slots)


def _position():
    return lax.axis_index("x"), lax.axis_index("y"), lax.axis_index("c")


def _all_gather(xs, name):
    n = len(xs)

    def body(*refs):
        _gather_body(refs[:n], refs[n:2 * n], *refs[2 * n:])

    return pl.pallas_call(
        body, name=name, in_specs=[ANY] * n, out_specs=[ANY] * n,
        out_shape=[jax.ShapeDtypeStruct((N_DEV,) + x.shape, x.dtype) for x in xs], scratch_shapes=_gather_sems(n),
    )(*xs)


def _gather_sems(n):
    return [pltpu.SemaphoreType.DMA((n, 7)), pltpu.SemaphoreType.DMA((n, 7)), pltpu.SemaphoreType.DMA((n,))]


def _gather_body(x_refs, out_refs, send_sems, recv_sems, local_sems):
    n = len(x_refs)
    mx, my, mc = _position()
    me, sibling = (mx, my, mc), (mx, my, 1 - mc)
    chips = [(1 - mx, my), (mx, 1 - my), (1 - mx, 1 - my)]

    def copy(a, k, block, to, own=False):
        dst = out_refs[a].at[4 * block[0] + 2 * block[1] + block[2]]
        return pltpu.make_async_remote_copy(
            src_ref=x_refs[a] if own else dst, dst_ref=dst,
            send_sem=send_sems.at[a, k], recv_sem=recv_sems.at[a, k], device_id=to, device_id_type=MESH)

    mine = [pltpu.make_async_copy(x_refs[a], out_refs[a].at[4 * mx + 2 * my + mc], local_sems.at[a]) for a in range(n)]
    first = [copy(a, 1 + j, me, (*chip, mc), own=True) for j, chip in enumerate(chips) for a in range(n)]
    first += [copy(a, 0, me, sibling, own=True) for a in range(n)]
    for cp in first + mine:
        cp.start()
    passed = []
    for j, chip in enumerate(chips):
        for a in range(n):
            copy(a, 1 + j, (*chip, mc), me).wait_recv()
            passed.append(copy(a, 4 + j, (*chip, mc), sibling))
            passed[-1].start()
    for a in range(n):
        copy(a, 0, sibling, me).wait_recv()
    for j, chip in enumerate(chips):
        for a in range(n):
            copy(a, 4 + j, (*chip, 1 - mc), me).wait_recv()
    for cp in first + passed:
        cp.wait_send()
    for cp in mine:
        cp.wait()


def _all_gather_async(xs, name, collective_id):
    n = len(xs)
    x_refs = [jax.new_ref(x, memory_space=pltpu.MemorySpace.HBM) for x in xs]
    out_refs = [jax.empty_ref(jax.ShapeDtypeStruct((N_DEV,) + x.shape, x.dtype), memory_space=pltpu.MemorySpace.HBM)
                for x in xs]

    @pl.kernel(mesh=plsc.ScalarSubcoreMesh(axis_name="seq", num_cores=1), name=name, scratch_types=_gather_sems(n),
               compiler_params=pltpu.CompilerParams(collective_id=collective_id))
    def launch(send_sems, recv_sems, local_sems):
        mx, my, mc = _position()
        peers = [(mx, my, 1 - mc), (1 - mx, my, mc), (mx, 1 - my, mc), (1 - mx, 1 - my, mc)]
        barrier = pltpu.get_barrier_semaphore()
        for peer in peers:
            pl.semaphore_signal(barrier, inc=1, device_id=peer, device_id_type=MESH)
        pl.semaphore_wait(barrier, len(peers))
        _gather_body(x_refs, out_refs, send_sems, recv_sems, local_sems)

    launch()
    return [r[...] for r in out_refs]


def _exchange_sibling(gs, name):
    n = len(gs)

    def body(*refs):
        g_refs, r_refs = refs[:n], refs[n:2 * n]
        send_sems, recv_sems = refs[2 * n:]
        mx, my, mc = _position()
        cps = [pltpu.make_async_remote_copy(src_ref=g_refs[a].at[:, 1 - mc], dst_ref=r_refs[a],
                                            send_sem=send_sems.at[a], recv_sem=recv_sems.at[a],
                                            device_id=(mx, my, 1 - mc), device_id_type=MESH) for a in range(n)]
        for cp in cps:
            cp.start()
        for cp in cps:
            cp.wait()

    return pl.pallas_call(
        body, name=name, in_specs=[ANY] * n, out_specs=[ANY] * n,
        out_shape=[jax.ShapeDtypeStruct(g.shape[:1] + g.shape[2:], g.dtype) for g in gs],
        scratch_shapes=[pltpu.SemaphoreType.DMA((n,)), pltpu.SemaphoreType.DMA((n,))],
    )(*gs)


def _exchange_chips(ss, name):
    n = len(ss)

    def body(*refs):
        s_refs, r_refs = refs[:n], refs[n:2 * n]
        send_sems, recv_sems, local_sems = refs[2 * n:]
        mx, my, mc = _position()
        my_chip = 2 * mx + my
        chips = [(1 - mx, my), (mx, 1 - my), (1 - mx, 1 - my)]

        def copy(a, k, to_slot):
            px, py = chips[k]
            return pltpu.make_async_remote_copy(
                src_ref=s_refs[a].at[2 * px + py], dst_ref=r_refs[a].at[to_slot], send_sem=send_sems.at[a, k],
                recv_sem=recv_sems.at[a, k], device_id=(px, py, mc), device_id_type=MESH)

        sends = [copy(a, k, my_chip) for k in range(3) for a in range(n)]
        local = [pltpu.make_async_copy(s_refs[a].at[my_chip], r_refs[a].at[my_chip], local_sems.at[a])
                 for a in range(n)]
        for cp in sends + local:
            cp.start()
        for k in range(3):
            px, py = chips[k]
            for a in range(n):
                copy(a, k, 2 * px + py).wait_recv()
        for cp in sends:
            cp.wait_send()
        for cp in local:
            cp.wait()

    return pl.pallas_call(
        body, name=name, in_specs=[ANY] * n, out_specs=[ANY] * n,
        out_shape=[jax.ShapeDtypeStruct(s.shape, s.dtype) for s in ss],
        scratch_shapes=[pltpu.SemaphoreType.DMA((n, 3)), pltpu.SemaphoreType.DMA((n, 3)), pltpu.SemaphoreType.DMA((n,))],
    )(*ss)


def _cols_concat(g, name, tb=128):
    _, k_dim, n = g.shape

    def body(g_ref, o_ref):
        o_ref[...] = jnp.concatenate([g_ref[d] for d in range(N_DEV)], axis=1)

    return pl.pallas_call(
        body, name=name, grid=(k_dim // tb,),
        in_specs=[pl.BlockSpec((N_DEV, tb, n), lambda i: (0, i, 0))],
        out_specs=pl.BlockSpec((tb, N_DEV * n), lambda i: (i, 0)),
        out_shape=jax.ShapeDtypeStruct((k_dim, N_DEV * n), g.dtype), compiler_params=_cp("parallel"),
    )(g)


def _cols_split(parts, name, tb=128):
    k_dim = parts[0].shape[0]
    n = sum(p.shape[1] for p in parts) // N_DEV

    def body(*refs):
        full = jnp.concatenate([r[...] for r in refs[:-1]], axis=1)
        for d in range(N_DEV):
            refs[-1][d] = full[:, d * n:(d + 1) * n]

    return pl.pallas_call(
        body, name=name, grid=(k_dim // tb,),
        in_specs=[pl.BlockSpec((tb, p.shape[1]), lambda i: (i, 0)) for p in parts],
        out_specs=pl.BlockSpec((N_DEV, tb, n), lambda i: (0, i, 0)),
        out_shape=jax.ShapeDtypeStruct((N_DEV, k_dim, n), parts[0].dtype), compiler_params=_cp("parallel"),
    )(*parts)


_Q0, _GL0 = 7200, 8224
N_SHARD_IN = N_IN // N_DEV


def _w_in_regions(g, name, tb=128):
    def body(g_ref, ssd_ref, lru_ref, q_ref, gl_ref):
        full = jnp.concatenate([g_ref[d] for d in range(N_DEV)], axis=1)
        lru_ref[...] = full[:, 0:2 * D]
        ssd_ref[:, :S_DT] = full[:, 2 * D:2 * D + S_DT]
        ssd_ref[:, S_DT:] = jnp.zeros((tb, W_SSD - S_DT), g.dtype)
        ssd_ref[:, S_DT:S_DT + DT_REAL] = full[:, 2 * D + S_DT:_Q0]
        q_ref[...] = full[:, _Q0:_GL0]
        gl_ref[...] = full[:, _GL0:N_IN]

    widths = (W_SSD, 2 * D, D, 3 * D)
    return pl.pallas_call(
        body, name=name, grid=(D // tb,),
        in_specs=[pl.BlockSpec((N_DEV, tb, N_SHARD_IN), lambda i: (0, i, 0))],
        out_specs=[pl.BlockSpec((tb, wd), lambda i: (i, 0)) for wd in widths],
        out_shape=[jax.ShapeDtypeStruct((D, wd), g.dtype) for wd in widths], compiler_params=_cp("parallel"),
    )(g)


def _w_in_shards(dssd, dlru, dq, dgl, name, tb=128):
    def body(ssd_ref, lru_ref, q_ref, gl_ref, o_ref):
        full = jnp.concatenate([lru_ref[...], ssd_ref[:, :S_DT + DT_REAL], q_ref[...], gl_ref[...]], axis=1)
        for d in range(N_DEV):
            o_ref[d] = full[:, d * N_SHARD_IN:(d + 1) * N_SHARD_IN]

    return pl.pallas_call(
        body, name=name, grid=(D // tb,),
        in_specs=[pl.BlockSpec((tb, a.shape[1]), lambda i: (i, 0)) for a in (dssd, dlru, dq, dgl)],
        out_specs=pl.BlockSpec((N_DEV, tb, N_SHARD_IN), lambda i: (0, i, 0)),
        out_shape=jax.ShapeDtypeStruct((N_DEV, D, N_SHARD_IN), F32), compiler_params=_cp("parallel"),
    )(dssd, dlru, dq, dgl)


_BIG = (("w_in", "col", (1024, 1412)), ("mem_w_kv", "col", (1024, 256)), ("w_br_lru", "row", (128, 1024)),
        ("w_br_ssd", "row", (256, 1024)), ("w_br_xa", "row", (128, 1024)), ("w_out", "row", (128, 1024)),
        ("ffn_w_in", "col", (1024, 704)), ("ffn_w_down", "row", (352, 1024)))
_SMALL = (("b_gate", (3, 128)), ("lru_conv_w", (4, 128)), ("ssd_conv_w", (4, 384)))
_REP = (("lru_conv_b", (1024,)), ("lru_w_a", (8, 128, 128)), ("lru_b_a", (1024,)), ("lru_w_i", (8, 128, 128)),
        ("lru_b_i", (1024,)), ("lru_lambda", (1024,)), ("ssd_conv_b", (3072,)), ("ssd_dt_bias", (32,)),
        ("ssd_a_log", (32,)), ("ssd_d", (32,)), ("ssd_norm_w", (2048,)), ("ln1_g", (1024,)), ("ln1_b", (1024,)),
        ("ln2_g", (1024,)), ("ln2_b", (1024,)))
_ORDER = ("w_in", "b_gate", "lru_conv_w", "lru_conv_b", "lru_w_a", "lru_b_a", "lru_w_i", "lru_b_i", "lru_lambda",
          "ssd_conv_w", "ssd_conv_b", "ssd_dt_bias", "ssd_a_log", "ssd_d", "ssd_norm_w", "mem_w_kv", "w_br_lru",
          "w_br_ssd", "w_br_xa", "w_out", "ln1_g", "ln1_b", "ffn_w_in", "ffn_w_down", "ln2_g", "ln2_b")

LANES = 1024
N_SMALL = sum(DEPTH * s[0] * s[1] for _, s in _SMALL)
R_SMALL = 8
N_REP = sum(DEPTH * math.prod(s) for _, s in _REP)
R_REP = 68
R_SM = R_SMALL + R_REP + 4
R_TAIL = R_SMALL + N_DEV * R_REP
TB_TAIL = 184
assert N_SMALL <= R_SMALL * LANES and N_REP <= N_DEV * R_REP * LANES


def _rows(flat, rows):
    return jnp.pad(flat, (0, rows * LANES - flat.shape[0])).reshape(rows, LANES)


def _rowblk(a, cap):
    return max(b for b in range(16, cap + 1, 16) if a % b == 0)


def _pack_tail(d):
    small = jnp.concatenate([d[n].reshape(-1) for n, _ in _SMALL])
    rep = jnp.concatenate([d[n].reshape(-1) for n, _ in _REP])
    return jnp.concatenate([_rows(small, R_SMALL), _rows(rep, N_DEV * R_REP)], axis=0)


def _unpack_tail(a):
    out, o = {}, 0
    flat = a[:R_SMALL].reshape(-1)
    for n, s in _SMALL:
        k = DEPTH * math.prod(s)
        out[n] = flat[o:o + k].reshape((DEPTH,) + s)
        o += k
    flat, o = a[R_SMALL:].reshape(-1), 0
    for n, s in _REP:
        k = DEPTH * math.prod(s)
        out[n] = flat[o:o + k].reshape((DEPTH,) + s)
        o += k
    return out


def _by_dest(g):
    g = g.reshape(g.shape[:-1] + (N_DEV, g.shape[-1] // N_DEV))
    return jnp.moveaxis(g, -2, 0).reshape(N_DEV, -1)


def _from_stack(st):
    st = jnp.moveaxis(st, 0, -2)
    return st.reshape(st.shape[:-2] + (st.shape[-2] * st.shape[-1],))


def _layer_fwd(x, mem, w, l):
    nm = lambda s: f"{s}_l{l}"
    wi = w["wi"]
    row = lambda v: v.reshape(1, -1)
    s = dict(x=x, wi=wi)
    s["p_ssd"] = _mm(x, wi["ssd"], name=nm("proj_ssd"))
    s["p_lru"] = _mm(x, wi["lru"], name=nm("proj_lru"))
    s["p_q"] = _mm(x, wi["q"], name=nm("proj_q"))
    s["p_gl"] = _mm(x, wi["gl"], name=nm("proj_gl"))
    s["lru_par"] = (w["lru_conv_w"], row(w["lru_conv_b"]), w["lru_w_a"], row(w["lru_b_a"]), w["lru_w_i"],
                    row(w["lru_b_i"]), row(w["lru_lambda"]))
    s["y_lru"], s["h"], s["xc"] = _lru_fwd(s["p_lru"], *s["lru_par"], name=nm("lru_fwd"))
    s["ssd_par"] = (w["ssd_conv_w"], row(w["ssd_conv_b"]), row(w["ssd_dt_bias"]), row(w["ssd_a_log"]),
                    row(w["ssd_d"]), row(w["ssd_norm_w"]))
    s["y_ssd"], s["yraw"], s["hs"], s["c_ssd"] = _ssd_fwd(s["p_ssd"], *s["ssd_par"], name=nm("ssd_fwd"))
    s["kv"] = _mm(mem, w["mem_w_kv"], name=nm("kv"))
    s["y_xa"] = _xa_fwd(s["p_q"], s["kv"], name=nm("xa_fwd"))
    s["b1"] = _mm(s["y_lru"], w["w_br_lru"], name=nm("br_lru"))
    s["b2"] = _mm(s["y_ssd"], w["w_br_ssd"], name=nm("br_ssd"))
    s["b3"] = _mm(s["y_xa"], w["w_br_xa"], name=nm("br_xa"))
    s["bg"] = row(w["b_gate"])
    s["merged"] = _merge_fwd(s["p_gl"], s["bg"], s["b1"], s["b2"], s["b3"], name=nm("merge_fwd"))
    s["mix"] = _mm(s["merged"], w["w_out"], name=nm("out_proj"))
    s["x1"] = _ln_fwd(x, s["mix"], row(w["ln1_g"]), row(w["ln1_b"]), name=nm("ln1_fwd"))
    s["gate"], s["up"], s["act"] = _ffn_in_swiglu(s["x1"], w["ffn_w_in"], name=nm("ffn_in"))
    s["f"] = _mm(s["act"], w["ffn_w_down"], name=nm("ffn_down"))
    s["x2"] = _ln_fwd(s["x1"], s["f"], row(w["ln2_g"]), row(w["ln2_b"]), name=nm("ln2_fwd"))
    return s


def _layer_bwd(s, mem, w, dxo, l):
    nm = lambda t: f"{t}_l{l}"
    row = lambda v: v.reshape(1, -1)
    slabs = lambda a: a.reshape(N_DEV, a.shape[0] // N_DEV, a.shape[1])
    g = {}
    du2, dg, db = _ln_bwd(s["x1"], s["f"], dxo, row(w["ln2_g"]), name=nm("ln2_bwd"))
    g["ln2_g"], g["ln2_b"] = dg[0], db[0]
    dgate, dup = _d_swiglu(du2, w["ffn_w_down"], s["gate"], s["up"], name=nm("d_swiglu"))
    g["ffn_w_down"] = slabs(_mm(s["act"], du2, ta=True, name=nm("dw_ffn_down")))
    dx1 = _mm(dgate, w["ffn_w_in"][:, :D_FF], tb=True, add=du2, add_scale=ALPHA, name=nm("d_x1_gate"))
    dx1 = _mm(dup, w["ffn_w_in"][:, D_FF:], tb=True, add=dx1, name=nm("d_x1_up"))
    g["ffn_w_in"] = _cols_split([_mm(s["x1"], dgate, ta=True, name=nm("dw_ffn_gate")),
                                 _mm(s["x1"], dup, ta=True, name=nm("dw_ffn_up"))], name=nm("dw_ffn_in_shards"))
    du1, dg, db = _ln_bwd(s["x"], s["mix"], dx1, row(w["ln1_g"]), name=nm("ln1_bwd"))
    g["ln1_g"], g["ln1_b"] = dg[0], db[0]
    dmerged = _mm(du1, w["w_out"], tb=True, name=nm("d_merged"))
    g["w_out"] = slabs(_mm(s["merged"], du1, ta=True, name=nm("dw_out")))
    dp_gl, d1, d2, d3, dbg = _merge_bwd(s["p_gl"], s["bg"], s["b1"], s["b2"], s["b3"], dmerged, name=nm("merge_bwd"))
    g["b_gate"] = dbg.reshape(3, D)
    dy_lru = _mm(d1, w["w_br_lru"], tb=True, name=nm("d_y_lru"))
    g["w_br_lru"] = slabs(_mm(s["y_lru"], d1, ta=True, name=nm("dw_br_lru")))
    dy_ssd = _mm(d2, w["w_br_ssd"], tb=True, name=nm("d_y_ssd"))
    g["w_br_ssd"] = slabs(_mm(s["y_ssd"], d2, ta=True, name=nm("dw_br_ssd")))
    dy_xa = _mm(d3, w["w_br_xa"], tb=True, name=nm("d_y_xa"))
    g["w_br_xa"] = slabs(_mm(s["y_xa"], d3, ta=True, name=nm("dw_br_xa")))
    dp_q, dkv = _xa_bwd(s["p_q"], s["kv"], dy_xa, name=nm("xa_bwd"))
    g["mem_w_kv"] = _mm(mem, dkv, ta=True, split_n=2 * D // N_DEV, name=nm("dw_kv"))
    ssd_cw, _, *ssd_rest = s["ssd_par"]
    dp_ssd, dcw, dcb, ddtb, dalog, dd, dnw = _ssd_bwd(s["p_ssd"], s["c_ssd"], s["yraw"], s["hs"], dy_ssd, ssd_cw,
                                                      *ssd_rest, name=nm("ssd_bwd"))
    g["ssd_conv_w"], g["ssd_conv_b"], g["ssd_dt_bias"] = dcw, dcb[0], ddtb[0]
    g["ssd_a_log"], g["ssd_d"], g["ssd_norm_w"] = dalog[0], dd[0], dnw[0]
    lru_cw, _, *lru_rest = s["lru_par"]
    dp_lru, dcw, dcb, dwa, dba, dwi, dbi, dlam = _lru_bwd(s["p_lru"], s["xc"], s["h"], dy_lru, lru_cw, *lru_rest,
                                                          name=nm("lru_bwd"))
    g["lru_conv_w"], g["lru_conv_b"], g["lru_w_a"], g["lru_b_a"] = dcw, dcb[0], dwa, dba[0]
    g["lru_w_i"], g["lru_b_i"], g["lru_lambda"] = dwi, dbi[0], dlam[0]
    wi = s["wi"]
    dx = _mm(dp_ssd, wi["ssd"], tb=True, add=du1, add_scale=ALPHA, name=nm("dx_ssd"))
    dx = _mm(dp_lru, wi["lru"], tb=True, add=dx, name=nm("dx_lru"))
    dx = _mm(dp_q, wi["q"], tb=True, add=dx, name=nm("dx_q"))
    dx = _mm(dp_gl, wi["gl"], tb=True, add=dx, name=nm("dx_gl"))
    x = s["x"]
    g["w_in"] = _w_in_shards(_mm(x, dp_ssd, ta=True, name=nm("dw_in_ssd")), _mm(x, dp_lru, ta=True, name=nm("dw_in_lru")),
                             _mm(x, dp_q, ta=True, name=nm("dw_in_q")), _mm(x, dp_gl, ta=True, name=nm("dw_in_gl")),
                             name=nm("dw_in_shards"))
    return dx, g


def _local_step(x, mem, target, layers):
    saved = []
    for l in range(DEPTH):
        saved.append(_layer_fwd(x, mem, layers[l], l))
        x = saved[-1]["x2"]
    dx, loss = _loss_head(x, target, name="loss_head")
    grads = [None] * DEPTH
    for l in reversed(range(DEPTH)):
        dx, grads[l] = _layer_bwd(saved[l], mem, layers[l], dx, l)
    return loss, dx, grads


def kernel(x, mem, w_in, b_gate, lru_conv_w, lru_conv_b, lru_w_a, lru_b_a, lru_w_i, lru_b_i, lru_lambda, ssd_conv_w, ssd_conv_b, ssd_dt_bias, ssd_a_log, ssd_d, ssd_norm_w, mem_w_kv, w_br_lru, w_br_ssd, w_br_xa, w_out, ln1_g, ln1_b, ffn_w_in, ffn_w_down, ln2_g, ln2_b, loss_target, m_w_in, m_b_gate, m_lru_conv_w, m_lru_conv_b, m_lru_w_a, m_lru_b_a, m_lru_w_i, m_lru_b_i, m_lru_lambda, m_ssd_conv_w, m_ssd_conv_b, m_ssd_dt_bias, m_ssd_a_log, m_ssd_d, m_ssd_norm_w, m_mem_w_kv, m_w_br_lru, m_w_br_ssd, m_w_br_xa, m_w_out, m_ln1_g, m_ln1_b, m_ffn_w_in, m_ffn_w_down, m_ln2_g, m_ln2_b, v_w_in, v_b_gate, v_lru_conv_w, v_lru_conv_b, v_lru_w_a, v_lru_b_a, v_lru_w_i, v_lru_b_i, v_lru_lambda, v_ssd_conv_w, v_ssd_conv_b, v_ssd_dt_bias, v_ssd_a_log, v_ssd_d, v_ssd_norm_w, v_mem_w_kv, v_w_br_lru, v_w_br_ssd, v_w_br_xa, v_w_out, v_ln1_g, v_ln1_b, v_ffn_w_in, v_ffn_w_down, v_ln2_g, v_ln2_b):
    local = dict(locals())
    w = {n: local[n] for n in _ORDER}
    m = {n: local["m_" + n] for n in _ORDER}
    v = {n: local["v_" + n] for n in _ORDER}

    big = [n for n, _, _ in _BIG]
    kinds = {n: kind for n, kind, _ in _BIG}

    small = _rows(jnp.concatenate([w[n].reshape(-1) for n, _ in _SMALL]), R_SMALL)
    first = _all_gather([w[n][0].astype(BF16) for n in big] + [small], name="gather_weights_l0")
    later, _ = lax.optimization_barrier(([w[n][1].astype(BF16) for n in big], first[-1]))
    stacks = [dict(zip(big, first[:-1])), dict(zip(big, _all_gather_async(later, "gather_weights_l1", collective_id=1)))]
    small_all, o, small_full = first[-1].reshape(N_DEV, R_SMALL * LANES), 0, {}
    for n, s in _SMALL:
        k = DEPTH * s[0] * s[1]
        small_full[n] = _from_stack(small_all[:, o:o + k].reshape((N_DEV, DEPTH) + s))
        o += k
    layers = []
    for l in range(DEPTH):
        lw = {n: w[n][l] for n, _ in _REP}
        lw.update({n: small_full[n][l] for n, _ in _SMALL})
        lw["wi"] = dict(zip(("ssd", "lru", "q", "gl"), _w_in_regions(stacks[l]["w_in"], name=f"w_in_regions_l{l}")))
        for n in big[1:]:
            if kinds[n] == "col":
                lw[n] = _cols_concat(stacks[l][n], name=f"full_{n}_l{l}")
            else:
                lw[n] = stacks[l][n].reshape(-1, stacks[l][n].shape[-1])
        layers.append(lw)

    loss_tile, dx, grads = _local_step(x[0], mem[0], loss_target[0], layers)
    loss = lax.psum(loss_tile[0, 0], ("x", "y", "c"))

    stacked = {n: jnp.stack([grads[l][n] for l in range(DEPTH)]) for n in [s[0] for s in _SMALL + _REP]}
    sm = jnp.concatenate([_by_dest(stacked[n]) for n, _ in _SMALL], axis=1)
    sm = jnp.pad(sm, ((0, 0), (0, R_SMALL * LANES - sm.shape[1])))
    rep = jnp.concatenate([stacked[n].reshape(-1) for n, _ in _REP])
    rep = jnp.pad(rep, (0, N_DEV * R_REP * LANES - rep.shape[0])).reshape(N_DEV, R_REP * LANES)
    tail = jnp.concatenate([sm, rep, jnp.zeros((N_DEV, (R_SM - R_SMALL - R_REP) * LANES), F32)], axis=1)
    owns = [grads[l][n].reshape((4, 2) + grads[l][n].shape[1:]) for l in range(DEPTH) for n in big]
    owns.append(tail.reshape(4, 2, R_SM, LANES))
    gots = _exchange_sibling(owns, name="reduce_cores")
    sums = [_pair_sum(own, got, name=f"pair_sum_{i}", out_dtype=BF16, tb=_rowblk(own.shape[2], 256))
            for i, (own, got) in enumerate(zip(owns[:-1], gots[:-1]))]
    sums.append(_pair_sum(owns[-1], gots[-1], name="pair_sum_tail", out_dtype=F32, tb=R_SM))
    slots = _exchange_chips(sums, name="reduce_chips")

    res = {}
    for i, n in enumerate(big):
        tb = _rowblk(w[n].shape[1], 128 if w[n].shape[2] > LANES else 256)
        res[n] = _adamw([slots[i], slots[len(big) + i]], w[n], m[n], v[n], name=f"adamw_{n}", tb=tb)
    tail_sum = _slot_sum(slots[-1], name="sum_tail")
    rep_all = _all_gather([tail_sum[R_SMALL:R_SMALL + R_REP]], name="gather_replicated")[0]
    g_tail = jnp.concatenate([tail_sum[:R_SMALL], rep_all.reshape(N_DEV * R_REP, LANES)], axis=0)
    tails = _adamw([g_tail[None]], _pack_tail(w)[None], _pack_tail(m)[None], _pack_tail(v)[None],
                   name="adamw_tail", tb=TB_TAIL)

    outs = []
    for kind in range(4):
        d = {**{n: res[n][kind] for n in big}, **_unpack_tail(tails[kind][0])}
        outs += [d[n] for n in _ORDER]
    return (loss, dx[None], *outs)
```

```python
import math

import jax
import jax.numpy as jnp
from jax import lax
from jax.experimental import pallas as pl
from jax.experimental.pallas import tpu as pltpu
from jax.experimental.pallas import tpu_sc as plsc

F32 = jnp.float32
BF16 = jnp.bfloat16

D = 1024
DEPTH = 2
N_DEV = 8
CHUNK = 64
LRU_BLOCKS = 8
LRU_BLOCK = 128
LRU_C = 8.0
D_SSD = 2 * D
SSD_HEADS = 32
SSD_GROUPS = 4
GROUP_W = D_SSD // SSD_GROUPS
SSD_STATE = 128
D_XBC = D_SSD + 2 * SSD_GROUPS * SSD_STATE
XA_HEADS = 4
XA_HEAD_DIM = 256
D_FF = 2816
ALPHA = (2 * DEPTH) ** 0.25
EPS = 1e-5
N_IN = 11296

S_Z, S_XBC, S_DT, W_SSD = 0, 2048, 5120, 5632
DT_REAL = 32

ADAM_LR, ADAM_B1, ADAM_B2, ADAM_EPS, ADAM_WD, ADAM_STEP = 0.001, 0.9, 0.999, 1e-08, 0.01, 10

VMEM_LIMIT = 56 * 1024 * 1024
MESH = pl.DeviceIdType.MESH
ANY = pl.BlockSpec(memory_space=pl.ANY)


def _cp(*sem):
    return pltpu.CompilerParams(dimension_semantics=sem, vmem_limit_bytes=VMEM_LIMIT)


def _blk(n, target):
    if n % 128:
        return n
    best = 128
    for b in range(128, min(n, target) + 1, 128):
        if n % b == 0:
            best = b
    return best


def _iota(shape, dim):
    return lax.broadcasted_iota(jnp.int32, shape, dim)


def _sigmoid(x):
    return 1.0 / (1.0 + jnp.exp(-x))


def _log1p(e):
    u = 1.0 + e
    return jnp.where(u == 1.0, e, jnp.log(u) * (e / (u - 1.0)))


def _softplus(x):
    return jnp.maximum(x, 0.0) + _log1p(jnp.exp(-jnp.abs(x)))


def _expm1(x):
    u = jnp.exp(x)
    um = u - 1.0
    return jnp.where(um == 0.0, x, jnp.where(um == -1.0, -1.0, um * (x / jnp.log(u))))


_G0 = math.sqrt(2.0 / math.pi)
_G1 = 0.044715


def _gelu_and_grad(x):
    t = jnp.tanh(_G0 * (x + _G1 * x * x * x))
    g = 0.5 * x * (1.0 + t)
    dg = 0.5 * (1.0 + t) + 0.5 * x * (1.0 - t * t) * (_G0 * (1.0 + 3.0 * _G1 * x * x))
    return g, dg


_NN = (((1,), (0,)), ((), ()))
_NT = (((1,), (1,)), ((), ()))
_TN = (((0,), (0,)), ((), ()))


def _dot(a, b, dims=_NN):
    return lax.dot_general(a.astype(BF16), b.astype(BF16), dims, preferred_element_type=F32)


def _dot_hi(a, b, dims=_NN):
    return lax.dot_general(a, b, dims, precision=lax.Precision.HIGHEST, preferred_element_type=F32)


def _split(v):
    hi = v.astype(BF16)
    return hi, (v - hi.astype(F32)).astype(BF16)


def _dot01(v, e, dims=_NN):
    hi, lo = _split(v)
    return (lax.dot_general(hi, e, dims, preferred_element_type=F32)
            + lax.dot_general(lo, e, dims, preferred_element_type=F32))


def _conv_taps(xe, n):
    return [xe[8:8 + n] if j == 3 else pltpu.roll(xe, 3 - j, 0)[8:8 + n] for j in range(4)]


def _conv_fwd(taps, cw, cb):
    return cb + cw[0:1] * taps[0] + cw[1:2] * taps[1] + cw[2:3] * taps[2] + cw[3:4] * taps[3]


def _conv_bwd(dc, dnext, x, cw, n):
    ext = jnp.concatenate([dc, dnext], axis=0)
    shifted = [pltpu.roll(ext, n + 8 - (3 - j), 0)[0:n] for j in range(3)] + [dc]
    dx = cw[0:1] * shifted[0] + cw[1:2] * shifted[1] + cw[2:3] * shifted[2] + cw[3:4] * dc
    dcw = jnp.concatenate([jnp.sum(x * shifted[j], axis=0, keepdims=True) for j in range(4)], axis=0)
    return dx, dcw, jnp.sum(dc, axis=0, keepdims=True)


MM_VMEM_BUDGET = 44 * 1024 * 1024
MM_MAX_TILE = 1408
MM_MAX_K = 5632


def _divisors(n, cap):
    return [n] if n % 128 else [b for b in range(128, min(n, cap) + 1, 128) if n % b == 0]


def _mm_tiles(m_dim, n_dim, k_dim, a_bytes, b_bytes, o_bytes, has_add, tn_fixed):
    best = None
    for tm in _divisors(m_dim, MM_MAX_TILE):
        for tn in ([tn_fixed] if tn_fixed else _divisors(n_dim, MM_MAX_TILE)):
            for tk in _divisors(k_dim, MM_MAX_K):
                vmem = 2 * (tm * tk * a_bytes + tk * tn * b_bytes + tm * tn * (o_bytes + (4 if has_add else 0)))
                vmem += tm * tn * 4 if tk < k_dim else 0
                if vmem <= MM_VMEM_BUDGET:
                    key = (tm * tn * tk, tk, tn)
                    if best is None or key > best[0]:
                        best = (key, (tm, tn, tk))
    assert best is not None, (m_dim, n_dim, k_dim)
    return best[1]


def _mm(a, b, *, ta=False, tb=False, out_dtype=F32, add=None, add_scale=1.0, name, split_n=None):
    if ta:
        k_dim, m_dim = a.shape
    else:
        m_dim, k_dim = a.shape
    if tb:
        n_dim, k2 = b.shape
    else:
        k2, n_dim = b.shape
    assert k_dim == k2, (a.shape, b.shape, ta, tb)
    tm, tn, tk = _mm_tiles(m_dim, n_dim, k_dim, a.dtype.itemsize, b.dtype.itemsize, jnp.dtype(out_dtype).itemsize,
                           add is not None, split_n)
    nk = k_dim // tk
    a_spec = pl.BlockSpec((tk, tm), lambda i, j, k: (k, i)) if ta else pl.BlockSpec((tm, tk), lambda i, j, k: (i, k))
    b_spec = pl.BlockSpec((tn, tk), lambda i, j, k: (j, k)) if tb else pl.BlockSpec((tk, tn), lambda i, j, k: (k, j))
    o_spec = pl.BlockSpec((tm, tn), lambda i, j, k: (i, j))
    out_shape = (m_dim, n_dim)
    if split_n is not None:
        assert add is None and tn == split_n, (tn, split_n)
        o_spec = pl.BlockSpec((None, tm, tn), lambda i, j, k: (j, i, 0))
        out_shape = (n_dim // tn, m_dim, tn)
    dims = (((0 if ta else 1,), (1 if tb else 0,)), ((), ()))
    has_add = add is not None

    def body(*refs):
        a_ref, b_ref = refs[:2]
        add_ref = refs[2] if has_add else None
        o_ref = refs[3] if has_add else refs[2]
        acc_ref = refs[-1] if nk > 1 else None
        k = pl.program_id(2)

        def product():
            return lax.dot_general(a_ref[...].astype(BF16), b_ref[...].astype(BF16), dims, preferred_element_type=F32)

        def finish(r):
            if has_add:
                r = r + add_scale * add_ref[...]
            o_ref[...] = r.astype(out_dtype)

        if nk == 1:
            finish(product())
            return

        @pl.when(k == 0)
        def _():
            acc_ref[...] = product()

        @pl.when((k > 0) & (k < nk - 1))
        def _():
            acc_ref[...] += product()

        @pl.when(k == nk - 1)
        def _():
            finish(acc_ref[...] + product())

    in_specs = [a_spec, b_spec] + ([o_spec] if has_add else [])
    args = (a, b) + ((add,) if has_add else ())
    return pl.pallas_call(
        body, name=name, grid=(m_dim // tm, n_dim // tn, nk),
        in_specs=in_specs, out_specs=o_spec,
        out_shape=jax.ShapeDtypeStruct(out_shape, out_dtype),
        scratch_shapes=[pltpu.VMEM((tm, tn), F32)] if nk > 1 else [],
        compiler_params=_cp("parallel", "parallel", "arbitrary"),
    )(*args)


def _ln_fwd(x, f, g, b, name, tb=512):
    t = x.shape[0]
    tb = min(tb, t)

    def body(x_ref, f_ref, g_ref, b_ref, o_ref):
        u = ALPHA * x_ref[...] + f_ref[...]
        mu = jnp.mean(u, axis=-1, keepdims=True)
        d = u - mu
        var = jnp.mean(d * d, axis=-1, keepdims=True)
        o_ref[...] = d * lax.rsqrt(var + EPS) * g_ref[...] + b_ref[...]

    row = pl.BlockSpec((tb, D), lambda i: (i, 0))
    par = pl.BlockSpec((1, D), lambda i: (0, 0))
    return pl.pallas_call(
        body, name=name, grid=(t // tb,), in_specs=[row, row, par, par], out_specs=row,
        out_shape=jax.ShapeDtypeStruct((t, D), F32), compiler_params=_cp("parallel"),
    )(x, f, g, b)


def _ln_bwd(x, f, dy, g, name, tb=512):
    t = x.shape[0]
    tb = min(tb, t)

    def body(x_ref, f_ref, dy_ref, g_ref, du_ref, dg_ref, db_ref):
        @pl.when(pl.program_id(0) == 0)
        def _():
            dg_ref[...] = jnp.zeros_like(dg_ref)
            db_ref[...] = jnp.zeros_like(db_ref)

        u = ALPHA * x_ref[...] + f_ref[...]
        mu = jnp.mean(u, axis=-1, keepdims=True)
        d = u - mu
        var = jnp.mean(d * d, axis=-1, keepdims=True)
        rstd = lax.rsqrt(var + EPS)
        xhat = d * rstd
        dy = dy_ref[...]
        dxh = dy * g_ref[...]
        m1 = jnp.mean(dxh, axis=-1, keepdims=True)
        m2 = jnp.mean(dxh * xhat, axis=-1, keepdims=True)
        du_ref[...] = rstd * (dxh - m1 - xhat * m2)
        dg_ref[...] += jnp.sum(dy * xhat, axis=0, keepdims=True)
        db_ref[...] += jnp.sum(dy, axis=0, keepdims=True)

    row = pl.BlockSpec((tb, D), lambda i: (i, 0))
    par = pl.BlockSpec((1, D), lambda i: (0, 0))
    return pl.pallas_call(
        body, name=name, grid=(t // tb,), in_specs=[row, row, row, par], out_specs=[row, par, par],
        out_shape=[jax.ShapeDtypeStruct((t, D), F32), jax.ShapeDtypeStruct((1, D), F32),
                   jax.ShapeDtypeStruct((1, D), F32)],
        compiler_params=_cp("arbitrary"),
    )(x, f, dy, g)


FFN_TM, FFN_TN = 512, D_FF // 2


def _ffn_in_swiglu(x, w, name):
    t = x.shape[0]
    tm = min(FFN_TM, t)
    nj = D_FF // FFN_TN

    def body(x_ref, wg_ref, wu_ref, g_ref, u_ref, a_ref):
        xb = x_ref[...].astype(BF16)
        g = lax.dot_general(xb, wg_ref[...], _NN, preferred_element_type=F32)
        u = lax.dot_general(xb, wu_ref[...], _NN, preferred_element_type=F32)
        g_ref[...] = g
        u_ref[...] = u
        a_ref[...] = (g * _sigmoid(g) * u).astype(BF16)

    tile = pl.BlockSpec((tm, FFN_TN), lambda i, j: (i, j))
    return pl.pallas_call(
        body, name=name, grid=(t // tm, nj),
        in_specs=[pl.BlockSpec((tm, D), lambda i, j: (i, 0)), pl.BlockSpec((D, FFN_TN), lambda i, j: (0, j)),
                  pl.BlockSpec((D, FFN_TN), lambda i, j: (0, nj + j))],
        out_specs=[tile, tile, tile],
        out_shape=[jax.ShapeDtypeStruct((t, D_FF), F32), jax.ShapeDtypeStruct((t, D_FF), F32),
                   jax.ShapeDtypeStruct((t, D_FF), BF16)],
        compiler_params=_cp("parallel", "parallel"),
    )(x, w, w)


def _d_swiglu(du, w_down, g, u, name):
    t = du.shape[0]
    tm = min(FFN_TM, t)

    def body(du_ref, w_ref, g_ref, u_ref, dg_ref, dup_ref):
        da = lax.dot_general(du_ref[...].astype(BF16), w_ref[...], _NT, preferred_element_type=F32)
        g_v = g_ref[...]
        s = _sigmoid(g_v)
        dg_ref[...] = (da * u_ref[...] * (s * (1.0 + g_v * (1.0 - s)))).astype(BF16)
        dup_ref[...] = (da * g_v * s).astype(BF16)

    tile = pl.BlockSpec((tm, FFN_TN), lambda i, j: (i, j))
    return pl.pallas_call(
        body, name=name, grid=(t // tm, D_FF // FFN_TN),
        in_specs=[pl.BlockSpec((tm, D), lambda i, j: (i, 0)), pl.BlockSpec((FFN_TN, D), lambda i, j: (j, 0)), tile, tile],
        out_specs=[tile, tile],
        out_shape=[jax.ShapeDtypeStruct((t, D_FF), BF16), jax.ShapeDtypeStruct((t, D_FF), BF16)],
        compiler_params=_cp("parallel", "parallel"),
    )(du, w_down, g, u)


def _merge_fwd(pgl, bg, b1, b2, b3, name, tb=512):
    t = pgl.shape[0]
    tb = min(tb, t)

    def body(gl_ref, bg_ref, b1_ref, b2_ref, b3_ref, o_ref):
        acc = None
        for j, b_ref in enumerate((b1_ref, b2_ref, b3_ref)):
            sl = slice(j * D, (j + 1) * D)
            term = _sigmoid(gl_ref[:, sl] + bg_ref[:, sl]) * b_ref[...]
            acc = term if acc is None else acc + term
        o_ref[...] = acc.astype(BF16)

    row = pl.BlockSpec((tb, D), lambda i: (i, 0))
    return pl.pallas_call(
        body, name=name, grid=(t // tb,),
        in_specs=[pl.BlockSpec((tb, 3 * D), lambda i: (i, 0)), pl.BlockSpec((1, 3 * D), lambda i: (0, 0)), row, row, row],
        out_specs=row, out_shape=jax.ShapeDtypeStruct((t, D), BF16), compiler_params=_cp("parallel"),
    )(pgl, bg, b1, b2, b3)


def _merge_bwd(pgl, bg, b1, b2, b3, dm, name, tb=512):
    t = pgl.shape[0]
    tb = min(tb, t)

    def body(gl_ref, bg_ref, b1_ref, b2_ref, b3_ref, dm_ref, dgl_ref, d1_ref, d2_ref, d3_ref, dbg_ref):
        @pl.when(pl.program_id(0) == 0)
        def _():
            dbg_ref[...] = jnp.zeros_like(dbg_ref)

        dm_v = dm_ref[...]
        for j, (b_ref, d_ref) in enumerate(((b1_ref, d1_ref), (b2_ref, d2_ref), (b3_ref, d3_ref))):
            sl = slice(j * D, (j + 1) * D)
            gate = _sigmoid(gl_ref[:, sl] + bg_ref[:, sl])
            d_ref[...] = (dm_v * gate).astype(BF16)
            dgl = dm_v * b_ref[...] * (gate * (1.0 - gate))
            dgl_ref[:, sl] = dgl.astype(BF16)
            dbg_ref[:, sl] += jnp.sum(dgl, axis=0, keepdims=True)

    row = pl.BlockSpec((tb, D), lambda i: (i, 0))
    wide = pl.BlockSpec((tb, 3 * D), lambda i: (i, 0))
    par = pl.BlockSpec((1, 3 * D), lambda i: (0, 0))
    return pl.pallas_call(
        body, name=name, grid=(t // tb,),
        in_specs=[wide, par, row, row, row, row], out_specs=[wide, row, row, row, par],
        out_shape=[jax.ShapeDtypeStruct((t, 3 * D), BF16)] + [jax.ShapeDtypeStruct((t, D), BF16)] * 3
                  + [jax.ShapeDtypeStruct((1, 3 * D), F32)],
        compiler_params=_cp("arbitrary"),
    )(pgl, bg, b1, b2, b3, dm)


def _xa_probs(q, kv_ref, hd):
    sl = slice(hd * XA_HEAD_DIM, (hd + 1) * XA_HEAD_DIM)
    k = kv_ref[:, sl]
    v = kv_ref[:, D + hd * XA_HEAD_DIM:D + (hd + 1) * XA_HEAD_DIM]
    s = _dot(q[:, sl], k, _NT) * (XA_HEAD_DIM ** -0.5)
    e = jnp.exp(s - jnp.max(s, axis=1, keepdims=True))
    return sl, k, v, e / jnp.sum(e, axis=1, keepdims=True)


def _xa_fwd(pq, kv, name, tb=512):
    t = pq.shape[0]
    tb = min(tb, t)

    def body(q_ref, kv_ref, o_ref):
        q = q_ref[...]
        for hd in range(XA_HEADS):
            sl, _, v, p = _xa_probs(q, kv_ref, hd)
            o_ref[:, sl] = _dot(p, v).astype(BF16)

    row = pl.BlockSpec((tb, D), lambda i: (i, 0))
    return pl.pallas_call(
        body, name=name, grid=(t // tb,),
        in_specs=[row, pl.BlockSpec(kv.shape, lambda i: (0, 0))], out_specs=row,
        out_shape=jax.ShapeDtypeStruct((t, D), BF16), compiler_params=_cp("parallel"),
    )(pq, kv)


def _xa_bwd(pq, kv, dy, name, tb=512):
    t = pq.shape[0]
    tb = min(tb, t)

    def body(q_ref, kv_ref, dy_ref, dq_ref, dkv_ref):
        @pl.when(pl.program_id(0) == 0)
        def _():
            dkv_ref[...] = jnp.zeros_like(dkv_ref)

        q = q_ref[...]
        for hd in range(XA_HEADS):
            sl, k, v, p = _xa_probs(q, kv_ref, hd)
            dyh = dy_ref[:, sl]
            vsl = slice(D + hd * XA_HEAD_DIM, D + (hd + 1) * XA_HEAD_DIM)
            dkv_ref[:, vsl] += _dot(p, dyh, _TN)
            dp = _dot(dyh, v, _NT)
            ds = p * (dp - jnp.sum(dp * p, axis=1, keepdims=True)) * (XA_HEAD_DIM ** -0.5)
            dq_ref[:, sl] = _dot(ds, k).astype(BF16)
            dkv_ref[:, sl] += _dot(ds, q[:, sl], _TN)

    row = pl.BlockSpec((tb, D), lambda i: (i, 0))
    kvs = pl.BlockSpec(kv.shape, lambda i: (0, 0))
    return pl.pallas_call(
        body, name=name, grid=(t // tb,), in_specs=[row, kvs, row], out_specs=[row, kvs],
        out_shape=[jax.ShapeDtypeStruct((t, D), BF16), jax.ShapeDtypeStruct(kv.shape, F32)],
        compiler_params=_cp("arbitrary"),
    )(pq, kv, dy)


def _scan_fwd(a, u):
    n = a.shape[0]
    row = _iota((n, 1), 0)
    d = 1
    while d < n:
        us = jnp.where(row >= d, pltpu.roll(u, d, 0), 0.0)
        u = a * us + u
        a = a * pltpu.roll(a, d, 0)
        d *= 2
    return u


def _scan_rev(b, u):
    n = b.shape[0]
    row = _iota((n, 1), 0)
    d = 1
    while d < n:
        us = jnp.where(row < n - d, pltpu.roll(u, n - d, 0), 0.0)
        u = b * us + u
        b = b * pltpu.roll(b, n - d, 0)
        d *= 2
    return u


def _lru_gates(xc, wa_ref, ba, wi_ref, bi, lam):
    za = jnp.concatenate([_dot(xc[:, n * 128:(n + 1) * 128], wa_ref[n]) for n in range(LRU_BLOCKS)], axis=1) + ba
    zi = jnp.concatenate([_dot(xc[:, n * 128:(n + 1) * 128], wi_ref[n]) for n in range(LRU_BLOCKS)], axis=1) + bi
    r = _sigmoid(za)
    ig = _sigmoid(zi)
    sp = _softplus(-lam)
    log_a = (-LRU_C) * r * sp
    a = jnp.exp(log_a)
    m = jnp.sqrt(-_expm1(2.0 * log_a))
    u = m * (ig * xc)
    return a, u, r, ig, m, sp


def _lru_fwd(p, cw, cb, wa, ba, wi, bi, lam, name, tb=256):
    t = p.shape[0]
    tb = min(tb, t)
    nb = t // tb
    r8 = tb // 8

    def body(x_ref, xp_ref, g_ref, cw_ref, cb_ref, wa_ref, ba_ref, wi_ref, bi_ref, lam_ref, y_ref, h_ref, xc_ref,
             hc_ref):
        i = pl.program_id(0)

        @pl.when(i == 0)
        def _():
            hc_ref[...] = jnp.zeros_like(hc_ref)

        halo = jnp.where(i == 0, 0.0, xp_ref[...])
        taps = _conv_taps(jnp.concatenate([halo, x_ref[...]], axis=0), tb)
        xc = _conv_fwd(taps, cw_ref[...], cb_ref[...])
        xc_ref[...] = xc
        a, u, _, _, _, _ = _lru_gates(xc, wa_ref, ba_ref[...], wi_ref, bi_ref[...], lam_ref[...])
        row = _iota((tb, 1), 0)
        u = u + jnp.where(row == 0, a * hc_ref[...], 0.0)
        h = _scan_fwd(a, u)
        h_ref[...] = h
        hc_ref[...] = h[tb - 1:tb, :]
        gl, _ = _gelu_and_grad(g_ref[...])
        y_ref[...] = (gl * h).astype(BF16)

    par = pl.BlockSpec((1, D), lambda i: (0, 0))
    wsp = pl.BlockSpec((LRU_BLOCKS, LRU_BLOCK, LRU_BLOCK), lambda i: (0, 0, 0))
    row = pl.BlockSpec((tb, D), lambda i: (i, 0))
    return pl.pallas_call(
        body, name=name, grid=(nb,),
        in_specs=[row, pl.BlockSpec((8, D), lambda i: (jnp.maximum(i * r8 - 1, 0), 0)),
                  pl.BlockSpec((tb, D), lambda i: (i, 1)),
                  pl.BlockSpec((4, D), lambda i: (0, 0)), par, wsp, par, wsp, par, par],
        out_specs=[row, row, row],
        out_shape=[jax.ShapeDtypeStruct((t, D), BF16), jax.ShapeDtypeStruct((t, D), F32),
                   jax.ShapeDtypeStruct((t, D), F32)],
        scratch_shapes=[pltpu.VMEM((1, D), F32)],
        compiler_params=_cp("arbitrary"),
    )(p, p, p, cw, cb, wa, ba, wi, bi, lam)


def _lru_bwd(p, xc, h, dy, cw, wa, ba, wi, bi, lam, name, tb=256):
    t = p.shape[0]
    tb = min(tb, t)
    nb = t // tb
    r8 = tb // 8

    def body(x_ref, g_ref, xc_ref, h_ref, hp_ref, dy_ref, cw_ref, wa_ref, ba_ref, wi_ref, bi_ref, lam_ref,
             dp_ref, dcw_ref, dcb_ref, dwa_ref, dba_ref, dwi_ref, dbi_ref, dlam_ref, carry_ref, dnext_ref):
        i = pl.program_id(0)
        blk = nb - 1 - i

        @pl.when(i == 0)
        def _():
            for r in (dcw_ref, dcb_ref, dwa_ref, dba_ref, dwi_ref, dbi_ref, dlam_ref, carry_ref, dnext_ref):
                r[...] = jnp.zeros_like(r)

        xc = xc_ref[...]
        lam = lam_ref[...]
        a, _, r, ig, m, sp = _lru_gates(xc, wa_ref, ba_ref[...], wi_ref, bi_ref[...], lam)
        gl, dgl = _gelu_and_grad(g_ref[...])
        h = h_ref[...]
        dy = dy_ref[...]
        dp_ref[:, D:] = (dy * h * dgl).astype(BF16)
        row = _iota((tb, 1), 0)
        dh = dy * gl + jnp.where(row == tb - 1, carry_ref[...], 0.0)
        b = jnp.where(row < tb - 1, pltpu.roll(a, tb - 1, 0), 0.0)
        gs = _scan_rev(b, dh)
        carry_ref[...] = a[0:1] * gs[0:1]
        h_last = jnp.where(blk == 0, 0.0, hp_ref[7:8, :])
        hprev = jnp.where(row == 0, h_last, pltpu.roll(h, 1, 0))
        da = gs * hprev
        dm = gs * ig * xc
        di = gs * m * xc
        dxc = gs * m * ig
        dlog = (0.5 * dm / m) * (-2.0 * a * a) + da * a
        dr = dlog * ((-LRU_C) * sp)
        dsp = jnp.sum(dlog * ((-LRU_C) * r), axis=0, keepdims=True)
        dlam_ref[...] += dsp * (-_sigmoid(-lam))
        dza = dr * r * (1.0 - r)
        dzi = di * ig * (1.0 - ig)
        dba_ref[...] += jnp.sum(dza, axis=0, keepdims=True)
        dbi_ref[...] += jnp.sum(dzi, axis=0, keepdims=True)
        parts = []
        for n in range(LRU_BLOCKS):
            sl = slice(n * 128, (n + 1) * 128)
            dwa_ref[n] += _dot(xc[:, sl], dza[:, sl], _TN)
            dwi_ref[n] += _dot(xc[:, sl], dzi[:, sl], _TN)
            parts.append(_dot(dza[:, sl], wa_ref[n], _NT) + _dot(dzi[:, sl], wi_ref[n], _NT))
        dxc = dxc + jnp.concatenate(parts, axis=1)
        dx, dcw, dcb = _conv_bwd(dxc, dnext_ref[...], x_ref[...], cw_ref[...], tb)
        dp_ref[:, :D] = dx.astype(BF16)
        dcw_ref[...] += dcw
        dcb_ref[...] += dcb
        dnext_ref[...] = dxc[0:8]

    par = pl.BlockSpec((1, D), lambda i: (0, 0))
    wsp = pl.BlockSpec((LRU_BLOCKS, LRU_BLOCK, LRU_BLOCK), lambda i: (0, 0, 0))
    cws = pl.BlockSpec((4, D), lambda i: (0, 0))
    rev = lambda i: nb - 1 - i
    blk0 = pl.BlockSpec((tb, D), lambda i: (rev(i), 0))
    w_shape = jax.ShapeDtypeStruct((LRU_BLOCKS, LRU_BLOCK, LRU_BLOCK), F32)
    v_shape = jax.ShapeDtypeStruct((1, D), F32)
    return pl.pallas_call(
        body, name=name, grid=(nb,),
        in_specs=[blk0, pl.BlockSpec((tb, D), lambda i: (rev(i), 1)), blk0, blk0,
                  pl.BlockSpec((8, D), lambda i: (jnp.maximum(rev(i) * r8 - 1, 0), 0)), blk0,
                  cws, wsp, par, wsp, par, par],
        out_specs=[pl.BlockSpec((tb, 2 * D), lambda i: (rev(i), 0)), cws, par, wsp, par, wsp, par, par],
        out_shape=[jax.ShapeDtypeStruct((t, 2 * D), BF16), jax.ShapeDtypeStruct((4, D), F32), v_shape,
                   w_shape, v_shape, w_shape, v_shape, v_shape],
        scratch_shapes=[pltpu.VMEM((1, D), F32), pltpu.VMEM((8, D), F32)],
        compiler_params=_cp("arbitrary"),
    )(p, p, xc, h, h, dy, cw, wa, ba, wi, bi, lam)


def _ssd_consts():
    m0 = _iota((1, 128), 1) < 64
    e = (jnp.right_shift(_iota((SSD_HEADS, D_SSD), 1), 6) == _iota((SSD_HEADS, D_SSD), 0)).astype(BF16)
    tril = (_iota((CHUNK, CHUNK), 0) >= _iota((CHUNK, CHUNK), 1)).astype(F32)
    eye = (_iota((SSD_HEADS, SSD_HEADS), 0) == _iota((SSD_HEADS, SSD_HEADS), 1)).astype(F32)
    r2 = _iota((CHUNK, 128), 0)
    c2 = jnp.bitwise_and(_iota((CHUNK, 128), 1), 63)
    return dict(m0=m0, e=e, tril=tril, eye=eye, causal2=r2 >= c2, fold=(c2 == r2).astype(BF16))


def _ssd_pre(c, p_ref, dtb_ref, alog_ref, dvec_ref, k):
    sg = _sigmoid(c)
    xbc = c * sg
    dtp = p_ref[:, S_DT:S_DT + DT_REAL] + dtb_ref[...]
    dt = _softplus(dtp)
    a = -jnp.exp(alog_ref[...])
    cs = _dot_hi(k["tril"], dt * a)
    cs_last = cs[CHUNK - 1:CHUNK]
    dend = jnp.exp(cs_last - cs)
    cdec = jnp.exp(cs_last)
    big = _dot01(jnp.concatenate([dt, jnp.exp(cs), dend], axis=0), k["e"])
    small = _dot01(jnp.concatenate([jnp.broadcast_to(cdec, (8, SSD_HEADS)),
                                    jnp.broadcast_to(dvec_ref[...], (8, SSD_HEADS))], axis=0), k["e"])
    cst2 = _dot_hi(k["eye"], jnp.concatenate([cs, cs], axis=0), _NT)
    return dict(c=c, sg=sg, xs=xbc[:, :D_SSD], bm=xbc[:, D_SSD:D_SSD + 512],
                cm=xbc[:, D_SSD + 512:], dtp=dtp, dt=dt, a=a, cs=cs, dend=dend, cdec=cdec,
                dtx=big[0:CHUNK], ecx=big[CHUNK:2 * CHUNK], dex=big[2 * CHUNK:3 * CHUNK],
                cdx=small[0:1], ddx=small[8:9], cst2=cst2)


def _pair_decay(p, cs, cst2, k):
    h0, h1 = 2 * p, 2 * p + 1
    colp = jnp.where(k["m0"], cs[:, h0:h0 + 1], cs[:, h1:h1 + 1])
    rowp = jnp.where(k["m0"], cst2[h0:h0 + 1, :], cst2[h1:h1 + 1, :])
    return jnp.where(k["causal2"], jnp.exp(colp - rowp), 0.0)


def _pair_stack(xp, k):
    return jnp.concatenate([jnp.where(k["m0"], xp, 0.0), jnp.where(k["m0"], 0.0, xp)], axis=0)


def _group_norm(yz, nw, with_stats=False):
    outs, stats = [], []
    for g in range(SSD_GROUPS):
        yzg = yz[:, g * GROUP_W:(g + 1) * GROUP_W]
        r = lax.rsqrt(jnp.mean(yzg * yzg, axis=1, keepdims=True) + EPS)
        outs.append(yzg * r)
        stats.append(r)
    y = jnp.concatenate(outs, axis=1) * nw
    return (y, stats) if with_stats else y


def _ssd_fwd(p, cw, cb, dtb, alog, dvec, nw, name):
    t = p.shape[0]
    nc = t // CHUNK

    def body(p_ref, pp_ref, cw_ref, cb_ref, dtb_ref, alog_ref, dvec_ref, nw_ref, y_ref, yraw_ref, hs_ref, c_ref,
             h_scr):
        i = pl.program_id(0)

        @pl.when(i == 0)
        def _():
            h_scr[...] = jnp.zeros_like(h_scr)

        k = _ssd_consts()
        halo = jnp.where(i == 0, 0.0, pp_ref[:, S_XBC:S_DT])
        taps = _conv_taps(jnp.concatenate([halo, p_ref[:, S_XBC:S_DT]], axis=0), CHUNK)
        c = _conv_fwd(taps, cw_ref[...], cb_ref[...])
        c_ref[...] = c
        s = _ssd_pre(c, p_ref, dtb_ref, alog_ref, dvec_ref, k)
        xs, bm, cm = s["xs"], s["bm"], s["cm"]
        xdt = xs * s["dtx"]
        hprev = h_scr[...]
        hs_ref[0] = hprev
        ys, hn = [], []
        for g in range(SSD_GROUPS):
            gs = slice(g * GROUP_W, (g + 1) * GROUP_W)
            bg = bm[:, g * 128:(g + 1) * 128]
            cg = cm[:, g * 128:(g + 1) * 128]
            cbdup = _dot(cg, jnp.concatenate([bg, bg], axis=0), _NT)
            hp_g = hprev[:, gs]
            yd = []
            for q in range(4):
                pr = g * 4 + q
                mp = cbdup * _pair_decay(pr, s["cs"], s["cst2"], k)
                yd.append(_dot(mp, _pair_stack(xdt[:, pr * 128:(pr + 1) * 128], k)))
            ys.append(jnp.concatenate(yd, axis=1) + _dot(cg, hp_g) * s["ecx"][:, gs])
            hn.append(hp_g * s["cdx"][:, gs] + _dot(bg, xdt[:, gs] * s["dex"][:, gs], _TN))
        h_scr[...] = jnp.concatenate(hn, axis=1)
        yraw = jnp.concatenate(ys, axis=1) + s["ddx"] * xs
        yraw_ref[...] = yraw
        z = p_ref[:, S_Z:S_Z + D_SSD]
        y_ref[...] = _group_norm(yraw * (z * _sigmoid(z)), nw_ref[...]).astype(BF16)

    hv = pl.BlockSpec((1, DT_REAL), lambda i: (0, 0))
    return pl.pallas_call(
        body, name=name, grid=(nc,),
        in_specs=[pl.BlockSpec((CHUNK, W_SSD), lambda i: (i, 0)),
                  pl.BlockSpec((8, W_SSD), lambda i: (jnp.maximum(i * (CHUNK // 8) - 1, 0), 0)),
                  pl.BlockSpec((4, D_XBC), lambda i: (0, 0)), pl.BlockSpec((1, D_XBC), lambda i: (0, 0)),
                  hv, hv, hv, pl.BlockSpec((1, D_SSD), lambda i: (0, 0))],
        out_specs=[pl.BlockSpec((CHUNK, D_SSD), lambda i: (i, 0)), pl.BlockSpec((CHUNK, D_SSD), lambda i: (i, 0)),
                   pl.BlockSpec((1, SSD_STATE, D_SSD), lambda i: (i, 0, 0)),
                   pl.BlockSpec((CHUNK, D_XBC), lambda i: (i, 0))],
        out_shape=[jax.ShapeDtypeStruct((t, D_SSD), BF16), jax.ShapeDtypeStruct((t, D_SSD), F32),
                   jax.ShapeDtypeStruct((nc, SSD_STATE, D_SSD), F32), jax.ShapeDtypeStruct((t, D_XBC), F32)],
        scratch_shapes=[pltpu.VMEM((SSD_STATE, D_SSD), F32)],
        compiler_params=_cp("arbitrary"),
    )(p, p, cw, cb, dtb, alog, dvec, nw)


def _ssd_bwd(p, c, yraw, hs, dy, cw, dtb, alog, dvec, nw, name):
    t = p.shape[0]
    nc = t // CHUNK

    def body(p_ref, c_ref, yraw_ref, hs_ref, dy_ref, cw_ref, dtb_ref, alog_ref, dvec_ref, nw_ref,
             dp_ref, dcw_ref, dcb_ref, ddtb_ref, dalog_ref, dd_ref, dnw_ref, dh_scr, dnext_scr):
        i = pl.program_id(0)

        @pl.when(i == 0)
        def _():
            for r in (dcw_ref, dcb_ref, ddtb_ref, dalog_ref, dd_ref, dnw_ref, dh_scr, dnext_scr):
                r[...] = jnp.zeros_like(r)

        k = _ssd_consts()
        s = _ssd_pre(c_ref[...], p_ref, dtb_ref, alog_ref, dvec_ref, k)
        xs, bm, cm, cs, dt, a = s["xs"], s["bm"], s["cm"], s["cs"], s["dt"], s["a"]
        m0 = k["m0"]
        xdt = xs * s["dtx"]
        hprev = hs_ref[0]
        dh = dh_scr[...]

        nw_v = nw_ref[...]
        yraw = yraw_ref[...]
        z = p_ref[:, S_Z:S_Z + D_SSD]
        sz = _sigmoid(z)
        siluz = z * sz
        yz = yraw * siluz
        dyo = dy_ref[...]
        dyn = dyo * nw_v
        dyz_parts, dnw_parts = [], []
        for g in range(SSD_GROUPS):
            gs = slice(g * GROUP_W, (g + 1) * GROUP_W)
            yzg = yz[:, gs]
            r = lax.rsqrt(jnp.mean(yzg * yzg, axis=1, keepdims=True) + EPS)
            dnw_parts.append(jnp.sum(dyo[:, gs] * yzg * r, axis=0, keepdims=True))
            dyz_parts.append(r * dyn[:, gs] - yzg * (r * r * r) * jnp.mean(dyn[:, gs] * yzg, axis=1, keepdims=True))
        dnw_ref[...] += jnp.concatenate(dnw_parts, axis=1)
        dyz = jnp.concatenate(dyz_parts, axis=1)
        d_y = dyz * siluz
        dp_ref[:, S_Z:S_Z + D_SSD] = (dyz * yraw * (sz * (1.0 + z * (1.0 - sz)))).astype(BF16)
        dd_row = jnp.sum(d_y * xs, axis=0, keepdims=True)
        dxs = d_y * s["ddx"]

        lane_h = _iota((1, SSD_HEADS), 1)
        sub_h = _iota((SSD_HEADS, 1), 0)
        dcs = jnp.zeros((CHUNK, SSD_HEADS), F32)
        dcst2 = jnp.zeros((SSD_HEADS, 128), F32)
        dxdt_parts, db_parts, dc_parts, dhp_parts, yoff_parts, dend_parts, dcd_parts = [], [], [], [], [], [], []
        for g in range(SSD_GROUPS):
            gs = slice(g * GROUP_W, (g + 1) * GROUP_W)
            bg = bm[:, g * 128:(g + 1) * 128]
            cg = cm[:, g * 128:(g + 1) * 128]
            bdup = jnp.concatenate([bg, bg], axis=0)
            cbdup = _dot(cg, bdup, _NT)
            dcb2 = jnp.zeros((CHUNK, 128), F32)
            dxp_parts = []
            for q in range(4):
                pr = g * 4 + q
                h0, h1 = 2 * pr, 2 * pr + 1
                lp = _pair_decay(pr, cs, s["cst2"], k)
                mp = cbdup * lp
                xst = _pair_stack(xdt[:, pr * 128:(pr + 1) * 128], k)
                dyp = d_y[:, pr * 128:(pr + 1) * 128]
                dmp = _dot(dyp, xst, _NT)
                dxst = _dot(mp, dyp, _TN)
                dxp_parts.append(jnp.where(m0, dxst[:CHUNK], dxst[CHUNK:]))
                dcb2 = dcb2 + dmp * lp
                dlm = dmp * mp
                rs0 = jnp.sum(jnp.where(m0, dlm, 0.0), axis=1, keepdims=True)
                rs1 = jnp.sum(jnp.where(m0, 0.0, dlm), axis=1, keepdims=True)
                dcs = dcs + jnp.where(lane_h == h0, rs0, 0.0) + jnp.where(lane_h == h1, rs1, 0.0)
                colsum = jnp.sum(dlm, axis=0, keepdims=True)
                sel = ((sub_h == h0) & m0) | ((sub_h == h1) & jnp.logical_not(m0))
                dcst2 = dcst2 - jnp.where(sel, colsum, 0.0)
            dcg = _dot(dcb2, bdup)
            dbdup = _dot(dcb2, cg, _TN)
            dbg = dbdup[:CHUNK] + dbdup[CHUNK:]
            hp_g = hprev[:, gs]
            zoff = _dot(cg, hp_g)
            dzo = d_y[:, gs] * s["ecx"][:, gs]
            dcg = dcg + _dot(dzo, hp_g, _NT)
            dh_g = dh[:, gs]
            dhp_parts.append(_dot(cg, dzo, _TN) + dh_g * s["cdx"][:, gs])
            dcd_parts.append(jnp.sum(dh_g * hp_g, axis=0, keepdims=True))
            wg = xdt[:, gs] * s["dex"][:, gs]
            dbg = dbg + _dot(wg, dh_g, _NT)
            dwg = _dot(bg, dh_g)
            dxdt_parts.append(jnp.concatenate(dxp_parts, axis=1) + dwg * s["dex"][:, gs])
            dend_g = dwg * wg
            dend_parts.append(jnp.sum(dend_g, axis=0, keepdims=True))
            yoff_parts.append(dzo * zoff - dend_g)
            db_parts.append(dbg)
            dc_parts.append(dcg)
        dh_scr[...] = jnp.concatenate(dhp_parts, axis=1)
        dxdt = jnp.concatenate(dxdt_parts, axis=1)
        sums = _dot01(jnp.concatenate([jnp.concatenate(yoff_parts, axis=1), dxdt * xs], axis=0), k["e"], _NT)
        rows8 = jnp.concatenate([jnp.broadcast_to(jnp.concatenate(r, axis=1), (8, D_SSD))
                                 for r in (dcd_parts, [dd_row], dend_parts)], axis=0)
        small = _dot01(rows8, k["e"], _NT)
        dd_ref[...] += small[8:9]
        dcs_last = small[0:1] * s["cdec"] + small[16:17]
        hi, lo = _split(dcst2)
        dcs = (dcs + sums[0:CHUNK]
               + lax.dot_general(k["fold"], hi, _NT, preferred_element_type=F32)
               + lax.dot_general(k["fold"], lo, _NT, preferred_element_type=F32)
               + jnp.where(_iota((CHUNK, 1), 0) == CHUNK - 1, dcs_last, 0.0))
        dda = _dot_hi(k["tril"], dcs, _TN)
        ddt = dda * a + sums[CHUNK:2 * CHUNK]
        dalog_ref[...] += jnp.sum(dda * dt, axis=0, keepdims=True) * a
        dxs = dxs + dxdt * s["dtx"]
        draw = ddt * _sigmoid(s["dtp"])
        ddtb_ref[...] += jnp.sum(draw, axis=0, keepdims=True)
        dp_ref[:, S_DT:] = jnp.zeros((CHUNK, W_SSD - S_DT), BF16)
        dp_ref[:, S_DT:S_DT + DT_REAL] = draw.astype(BF16)
        dxbc = jnp.concatenate([dxs] + db_parts + dc_parts, axis=1)
        sg, c = s["sg"], s["c"]
        dc = dxbc * (sg * (1.0 + c * (1.0 - sg)))
        dx, dcw, dcb = _conv_bwd(dc, dnext_scr[...], p_ref[:, S_XBC:S_DT], cw_ref[...], CHUNK)
        dp_ref[:, S_XBC:S_DT] = dx.astype(BF16)
        dcw_ref[...] += dcw
        dcb_ref[...] += dcb
        dnext_scr[...] = dc[0:8]

    rev = lambda i: nc - 1 - i
    hv = pl.BlockSpec((1, DT_REAL), lambda i: (0, 0))
    cws = pl.BlockSpec((4, D_XBC), lambda i: (0, 0))
    cbs = pl.BlockSpec((1, D_XBC), lambda i: (0, 0))
    nws = pl.BlockSpec((1, D_SSD), lambda i: (0, 0))
    wide = pl.BlockSpec((CHUNK, D_SSD), lambda i: (rev(i), 0))
    hshape = jax.ShapeDtypeStruct((1, DT_REAL), F32)
    return pl.pallas_call(
        body, name=name, grid=(nc,),
        in_specs=[pl.BlockSpec((CHUNK, W_SSD), lambda i: (rev(i), 0)),
                  pl.BlockSpec((CHUNK, D_XBC), lambda i: (rev(i), 0)),
                  wide, pl.BlockSpec((1, SSD_STATE, D_SSD), lambda i: (rev(i), 0, 0)), wide,
                  cws, hv, hv, hv, nws],
        out_specs=[pl.BlockSpec((CHUNK, W_SSD), lambda i: (rev(i), 0)), cws, cbs, hv, hv, hv, nws],
        out_shape=[jax.ShapeDtypeStruct((t, W_SSD), BF16), jax.ShapeDtypeStruct((4, D_XBC), F32),
                   jax.ShapeDtypeStruct((1, D_XBC), F32), hshape, hshape, hshape,
                   jax.ShapeDtypeStruct((1, D_SSD), F32)],
        scratch_shapes=[pltpu.VMEM((SSD_STATE, D_SSD), F32), pltpu.VMEM((8, D_XBC), F32)],
        compiler_params=_cp("arbitrary"),
    )(p, c, yraw, hs, dy, cw, dtb, alog, dvec, nw)


def _loss_head(y, target, name, tb=512):
    t = y.shape[0]
    tb = min(tb, t)

    def body(y_ref, t_ref, dy_ref, l_ref):
        @pl.when(pl.program_id(0) == 0)
        def _():
            l_ref[...] = jnp.zeros_like(l_ref)

        e = y_ref[...] - t_ref[...]
        dy_ref[...] = e * (1.0 / D)
        l_ref[...] += jnp.sum(jnp.sum(e * e, axis=1, keepdims=True), axis=0, keepdims=True) * (0.5 / D)

    row = pl.BlockSpec((tb, D), lambda i: (i, 0))
    return pl.pallas_call(
        body, name=name, grid=(t // tb,), in_specs=[row, row],
        out_specs=[row, pl.BlockSpec((8, 128), lambda i: (0, 0))],
        out_shape=[jax.ShapeDtypeStruct((t, D), F32), jax.ShapeDtypeStruct((8, 128), F32)],
        compiler_params=_cp("arbitrary"),
    )(y, target)


def _adamw(slots, w, m, v, name, tb):
    nl = len(slots)
    ns, r, c = slots[0].shape
    assert r % tb == 0 and w.shape == (nl, r, c), (r, tb, w.shape)

    def body(*refs):
        s_refs = refs[:nl]
        w_ref, m_ref, v_ref, g_ref, d_ref, m2_ref, v2_ref = refs[nl:]

        def total(ref):
            acc = ref[0].astype(F32)
            for j in range(1, ns):
                acc = acc + ref[j].astype(F32)
            return acc

        g = total(s_refs[0])
        for layer in range(1, nl):
            g = jnp.where(pl.program_id(0) == layer, total(s_refs[layer]), g)
        m2 = ADAM_B1 * m_ref[...] + (1.0 - ADAM_B1) * g
        v2 = ADAM_B2 * v_ref[...] + (1.0 - ADAM_B2) * (g * g)
        m_hat = m2 / (1.0 - ADAM_B1 ** ADAM_STEP)
        v_hat = v2 / (1.0 - ADAM_B2 ** ADAM_STEP)
        g_ref[...] = g
        d_ref[...] = -ADAM_LR * (m_hat / (jnp.sqrt(v_hat) + ADAM_EPS) + ADAM_WD * w_ref[...])
        m2_ref[...] = m2
        v2_ref[...] = v2

    def slot_spec(layer):
        return pl.BlockSpec((ns, tb, c), lambda l, i: (0, jnp.where(l == layer, i, 0), 0))

    row = pl.BlockSpec((None, tb, c), lambda l, i: (l, i, 0))
    shp = jax.ShapeDtypeStruct((nl, r, c), F32)
    return pl.pallas_call(
        body, name=name, grid=(nl, r // tb),
        in_specs=[slot_spec(layer) for layer in range(nl)] + [row, row, row],
        out_specs=[row, row, row, row], out_shape=[shp, shp, shp, shp], compiler_params=_cp("arbitrary", "arbitrary"),
    )(*slots, w, m, v)


def _pair_sum(own, got, name, out_dtype, tb):
    nj, _, r, c = own.shape
    mc = lax.axis_index("c")

    def body(mc_ref, a_ref, b_ref, o_ref):
        del mc_ref
        o_ref[...] = (a_ref[...] + b_ref[...]).astype(out_dtype)

    return pl.pallas_call(
        body, name=name,
        grid_spec=pltpu.PrefetchScalarGridSpec(
            num_scalar_prefetch=1, grid=(nj, r // tb),
            in_specs=[pl.BlockSpec((None, None, tb, c), lambda j, i, mc_ref: (j, mc_ref[0], i, 0)),
                      pl.BlockSpec((None, tb, c), lambda j, i, mc_ref: (j, i, 0))],
            out_specs=pl.BlockSpec((None, tb, c), lambda j, i, mc_ref: (j, i, 0))),
        out_shape=jax.ShapeDtypeStruct((nj, r, c), out_dtype), compiler_params=_cp("parallel", "parallel"),
    )(jnp.reshape(mc, (1,)).astype(jnp.int32), own, got)


def _slot_sum(slots, name):
    ns, r, c = slots.shape

    def body(s_ref, o_ref):
        g = s_ref[0]
        for j in range(1, ns):
            g = g + s_ref[j]
        o_ref[...] = g

    return pl.pallas_call(body, name=name, out_shape=jax.ShapeDtypeStruct((r, c), F32))(slots)


def _position():
    return lax.axis_index("x"), lax.axis_index("y"), lax.axis_index("c")


def _comm(exchange, peers, xs, out_shapes, sems, name, collective_id):
    n = len(xs)
    if collective_id is None:
        def body(*refs):
            exchange(refs[:n], refs[n:n + len(out_shapes)], *refs[n + len(out_shapes):])

        return pl.pallas_call(body, name=name, in_specs=[ANY] * n, out_specs=[ANY] * len(out_shapes),
                              out_shape=out_shapes, scratch_shapes=sems)(*xs)
    x_refs = [jax.new_ref(x, memory_space=pltpu.MemorySpace.HBM) for x in xs]
    out_refs = [jax.empty_ref(s, memory_space=pltpu.MemorySpace.HBM) for s in out_shapes]

    @pl.kernel(mesh=plsc.ScalarSubcoreMesh(axis_name="seq", num_cores=1), name=name, scratch_types=sems,
               compiler_params=pltpu.CompilerParams(collective_id=collective_id))
    def launch(*sem_refs):
        barrier = pltpu.get_barrier_semaphore()
        to = peers(*_position())
        for peer in to:
            pl.semaphore_signal(barrier, inc=1, device_id=peer, device_id_type=MESH)
        pl.semaphore_wait(barrier, len(to))
        exchange(x_refs, out_refs, *sem_refs)

    launch()
    return [r[...] for r in out_refs]


def _all_gather(xs, name, collective_id=None):
    n = len(xs)
    return _comm(_gather_body, lambda x, y, c: [(x, y, 1 - c), (1 - x, y, c), (x, 1 - y, c), (1 - x, 1 - y, c)], xs,
                 [jax.ShapeDtypeStruct((N_DEV,) + x.shape, x.dtype) for x in xs],
                 [pltpu.SemaphoreType.DMA((n, 7)), pltpu.SemaphoreType.DMA((n, 7)), pltpu.SemaphoreType.DMA((n,))],
                 name, collective_id)


def _gather_body(x_refs, out_refs, send_sems, recv_sems, local_sems):
    n = len(x_refs)
    mx, my, mc = _position()
    me, sibling = (mx, my, mc), (mx, my, 1 - mc)
    chips = [(1 - mx, my), (mx, 1 - my), (1 - mx, 1 - my)]

    def copy(a, k, block, to, own=False):
        dst = out_refs[a].at[4 * block[0] + 2 * block[1] + block[2]]
        return pltpu.make_async_remote_copy(
            src_ref=x_refs[a] if own else dst, dst_ref=dst,
            send_sem=send_sems.at[a, k], recv_sem=recv_sems.at[a, k], device_id=to, device_id_type=MESH)

    mine = [pltpu.make_async_copy(x_refs[a], out_refs[a].at[4 * mx + 2 * my + mc], local_sems.at[a]) for a in range(n)]
    first = [copy(a, 1 + j, me, (*chip, mc), own=True) for j, chip in enumerate(chips) for a in range(n)]
    first += [copy(a, 0, me, sibling, own=True) for a in range(n)]
    for cp in first + mine:
        cp.start()
    passed = []
    for j, chip in enumerate(chips):
        for a in range(n):
            copy(a, 1 + j, (*chip, mc), me).wait_recv()
            passed.append(copy(a, 4 + j, (*chip, mc), sibling))
            passed[-1].start()
    for a in range(n):
        copy(a, 0, sibling, me).wait_recv()
    for j, chip in enumerate(chips):
        for a in range(n):
            copy(a, 4 + j, (*chip, 1 - mc), me).wait_recv()
    for cp in first + passed:
        cp.wait_send()
    for cp in mine:
        cp.wait()


def _exchange_sibling(gs, name, collective_id=None):
    n = len(gs)

    def exchange(g_refs, r_refs, send_sems, recv_sems):
        mx, my, mc = _position()
        cps = [pltpu.make_async_remote_copy(src_ref=g_refs[a].at[:, 1 - mc], dst_ref=r_refs[a],
                                            send_sem=send_sems.at[a], recv_sem=recv_sems.at[a],
                                            device_id=(mx, my, 1 - mc), device_id_type=MESH) for a in range(n)]
        for cp in cps:
            cp.start()
        for cp in cps:
            cp.wait()

    return _comm(exchange, lambda x, y, c: [(x, y, 1 - c)], gs,
                 [jax.ShapeDtypeStruct(g.shape[:1] + g.shape[2:], g.dtype) for g in gs],
                 [pltpu.SemaphoreType.DMA((n,)), pltpu.SemaphoreType.DMA((n,))], name, collective_id)


def _exchange_chips(ss, name, collective_id=None):
    n = len(ss)

    def exchange(s_refs, r_refs, send_sems, recv_sems, local_sems):
        mx, my, mc = _position()
        my_chip = 2 * mx + my
        chips = [(1 - mx, my), (mx, 1 - my), (1 - mx, 1 - my)]

        def copy(a, k, to_slot):
            px, py = chips[k]
            return pltpu.make_async_remote_copy(
                src_ref=s_refs[a].at[2 * px + py], dst_ref=r_refs[a].at[to_slot], send_sem=send_sems.at[a, k],
                recv_sem=recv_sems.at[a, k], device_id=(px, py, mc), device_id_type=MESH)

        sends = [copy(a, k, my_chip) for k in range(3) for a in range(n)]
        local = [pltpu.make_async_copy(s_refs[a].at[my_chip], r_refs[a].at[my_chip], local_sems.at[a])
                 for a in range(n)]
        for cp in sends + local:
            cp.start()
        for k in range(3):
            px, py = chips[k]
            for a in range(n):
                copy(a, k, 2 * px + py).wait_recv()
        for cp in sends:
            cp.wait_send()
        for cp in local:
            cp.wait()

    return _comm(exchange, lambda x, y, c: [(1 - x, y, c), (x, 1 - y, c), (1 - x, 1 - y, c)], ss,
                 [jax.ShapeDtypeStruct(s.shape, s.dtype) for s in ss],
                 [pltpu.SemaphoreType.DMA((n, 3)), pltpu.SemaphoreType.DMA((n, 3)), pltpu.SemaphoreType.DMA((n,))],
                 name, collective_id)


def _cols_concat(g, name, tb=128):
    _, k_dim, n = g.shape

    def body(g_ref, o_ref):
        o_ref[...] = jnp.concatenate([g_ref[d] for d in range(N_DEV)], axis=1)

    return pl.pallas_call(
        body, name=name, grid=(k_dim // tb,),
        in_specs=[pl.BlockSpec((N_DEV, tb, n), lambda i: (0, i, 0))],
        out_specs=pl.BlockSpec((tb, N_DEV * n), lambda i: (i, 0)),
        out_shape=jax.ShapeDtypeStruct((k_dim, N_DEV * n), g.dtype), compiler_params=_cp("parallel"),
    )(g)


def _cols_split(parts, name, tb=128):
    k_dim = parts[0].shape[0]
    n = sum(p.shape[1] for p in parts) // N_DEV

    def body(*refs):
        full = jnp.concatenate([r[...] for r in refs[:-1]], axis=1)
        for d in range(N_DEV):
            refs[-1][d] = full[:, d * n:(d + 1) * n]

    return pl.pallas_call(
        body, name=name, grid=(k_dim // tb,),
        in_specs=[pl.BlockSpec((tb, p.shape[1]), lambda i: (i, 0)) for p in parts],
        out_specs=pl.BlockSpec((N_DEV, tb, n), lambda i: (0, i, 0)),
        out_shape=jax.ShapeDtypeStruct((N_DEV, k_dim, n), parts[0].dtype), compiler_params=_cp("parallel"),
    )(*parts)


_Q0, _GL0 = 7200, 8224
N_SHARD_IN = N_IN // N_DEV


def _w_in_regions(g, name, tb=128):
    def body(g_ref, ssd_ref, lru_ref, q_ref, gl_ref):
        full = jnp.concatenate([g_ref[d] for d in range(N_DEV)], axis=1)
        lru_ref[...] = full[:, 0:2 * D]
        ssd_ref[:, :S_DT] = full[:, 2 * D:2 * D + S_DT]
        ssd_ref[:, S_DT:] = jnp.zeros((tb, W_SSD - S_DT), g.dtype)
        ssd_ref[:, S_DT:S_DT + DT_REAL] = full[:, 2 * D + S_DT:_Q0]
        q_ref[...] = full[:, _Q0:_GL0]
        gl_ref[...] = full[:, _GL0:N_IN]

    widths = (W_SSD, 2 * D, D, 3 * D)
    return pl.pallas_call(
        body, name=name, grid=(D // tb,),
        in_specs=[pl.BlockSpec((N_DEV, tb, N_SHARD_IN), lambda i: (0, i, 0))],
        out_specs=[pl.BlockSpec((tb, wd), lambda i: (i, 0)) for wd in widths],
        out_shape=[jax.ShapeDtypeStruct((D, wd), g.dtype) for wd in widths], compiler_params=_cp("parallel"),
    )(g)


def _w_in_shards(dssd, dlru, dq, dgl, name, tb=128):
    def body(ssd_ref, lru_ref, q_ref, gl_ref, o_ref):
        full = jnp.concatenate([lru_ref[...], ssd_ref[:, :S_DT + DT_REAL], q_ref[...], gl_ref[...]], axis=1)
        for d in range(N_DEV):
            o_ref[d] = full[:, d * N_SHARD_IN:(d + 1) * N_SHARD_IN]

    return pl.pallas_call(
        body, name=name, grid=(D // tb,),
        in_specs=[pl.BlockSpec((tb, a.shape[1]), lambda i: (i, 0)) for a in (dssd, dlru, dq, dgl)],
        out_specs=pl.BlockSpec((N_DEV, tb, N_SHARD_IN), lambda i: (0, i, 0)),
        out_shape=jax.ShapeDtypeStruct((N_DEV, D, N_SHARD_IN), F32), compiler_params=_cp("parallel"),
    )(dssd, dlru, dq, dgl)


_BIG = (("w_in", "col", (1024, 1412)), ("mem_w_kv", "col", (1024, 256)), ("w_br_lru", "row", (128, 1024)),
        ("w_br_ssd", "row", (256, 1024)), ("w_br_xa", "row", (128, 1024)), ("w_out", "row", (128, 1024)),
        ("ffn_w_in", "col", (1024, 704)), ("ffn_w_down", "row", (352, 1024)))
_SMALL = (("b_gate", (3, 128)), ("lru_conv_w", (4, 128)), ("ssd_conv_w", (4, 384)))
_REP = (("lru_conv_b", (1024,)), ("lru_w_a", (8, 128, 128)), ("lru_b_a", (1024,)), ("lru_w_i", (8, 128, 128)),
        ("lru_b_i", (1024,)), ("lru_lambda", (1024,)), ("ssd_conv_b", (3072,)), ("ssd_dt_bias", (32,)),
        ("ssd_a_log", (32,)), ("ssd_d", (32,)), ("ssd_norm_w", (2048,)), ("ln1_g", (1024,)), ("ln1_b", (1024,)),
        ("ln2_g", (1024,)), ("ln2_b", (1024,)))
_ORDER = ("w_in", "b_gate", "lru_conv_w", "lru_conv_b", "lru_w_a", "lru_b_a", "lru_w_i", "lru_b_i", "lru_lambda",
          "ssd_conv_w", "ssd_conv_b", "ssd_dt_bias", "ssd_a_log", "ssd_d", "ssd_norm_w", "mem_w_kv", "w_br_lru",
          "w_br_ssd", "w_br_xa", "w_out", "ln1_g", "ln1_b", "ffn_w_in", "ffn_w_down", "ln2_g", "ln2_b")

LANES = 1024
N_SMALL = sum(DEPTH * s[0] * s[1] for _, s in _SMALL)
R_SMALL = 8
N_REP = sum(DEPTH * math.prod(s) for _, s in _REP)
R_REP = 68
R_SM = R_SMALL + R_REP + 4
R_TAIL = R_SMALL + N_DEV * R_REP
TB_TAIL = 184
assert N_SMALL <= R_SMALL * LANES and N_REP <= N_DEV * R_REP * LANES


def _rows(flat, rows):
    return jnp.pad(flat, (0, rows * LANES - flat.shape[0])).reshape(rows, LANES)


def _rowblk(a, cap):
    return max(b for b in range(16, cap + 1, 16) if a % b == 0)


def _pack_tail(d):
    small = jnp.concatenate([d[n].reshape(-1) for n, _ in _SMALL])
    rep = jnp.concatenate([d[n].reshape(-1) for n, _ in _REP])
    return jnp.concatenate([_rows(small, R_SMALL), _rows(rep, N_DEV * R_REP)], axis=0)


def _unpack_tail(a):
    out, o = {}, 0
    flat = a[:R_SMALL].reshape(-1)
    for n, s in _SMALL:
        k = DEPTH * math.prod(s)
        out[n] = flat[o:o + k].reshape((DEPTH,) + s)
        o += k
    flat, o = a[R_SMALL:].reshape(-1), 0
    for n, s in _REP:
        k = DEPTH * math.prod(s)
        out[n] = flat[o:o + k].reshape((DEPTH,) + s)
        o += k
    return out


def _by_dest(g):
    g = g.reshape(g.shape[:-1] + (N_DEV, g.shape[-1] // N_DEV))
    return jnp.moveaxis(g, -2, 0).reshape(N_DEV, -1)


def _from_stack(st):
    st = jnp.moveaxis(st, 0, -2)
    return st.reshape(st.shape[:-2] + (st.shape[-2] * st.shape[-1],))


def _layer_fwd(x, mem, w, l):
    nm = lambda s: f"{s}_l{l}"
    wi = w["wi"]
    row = lambda v: v.reshape(1, -1)
    s = dict(x=x, wi=wi)
    s["p_ssd"] = _mm(x, wi["ssd"], name=nm("proj_ssd"))
    s["p_lru"] = _mm(x, wi["lru"], name=nm("proj_lru"))
    s["p_q"] = _mm(x, wi["q"], name=nm("proj_q"))
    s["p_gl"] = _mm(x, wi["gl"], name=nm("proj_gl"))
    s["lru_par"] = (w["lru_conv_w"], row(w["lru_conv_b"]), w["lru_w_a"], row(w["lru_b_a"]), w["lru_w_i"],
                    row(w["lru_b_i"]), row(w["lru_lambda"]))
    s["y_lru"], s["h"], s["xc"] = _lru_fwd(s["p_lru"], *s["lru_par"], name=nm("lru_fwd"))
    s["ssd_par"] = (w["ssd_conv_w"], row(w["ssd_conv_b"]), row(w["ssd_dt_bias"]), row(w["ssd_a_log"]),
                    row(w["ssd_d"]), row(w["ssd_norm_w"]))
    s["y_ssd"], s["yraw"], s["hs"], s["c_ssd"] = _ssd_fwd(s["p_ssd"], *s["ssd_par"], name=nm("ssd_fwd"))
    s["kv"] = _mm(mem, w["mem_w_kv"], name=nm("kv"))
    s["y_xa"] = _xa_fwd(s["p_q"], s["kv"], name=nm("xa_fwd"))
    s["b1"] = _mm(s["y_lru"], w["w_br_lru"], name=nm("br_lru"))
    s["b2"] = _mm(s["y_ssd"], w["w_br_ssd"], name=nm("br_ssd"))
    s["b3"] = _mm(s["y_xa"], w["w_br_xa"], name=nm("br_xa"))
    s["bg"] = row(w["b_gate"])
    s["merged"] = _merge_fwd(s["p_gl"], s["bg"], s["b1"], s["b2"], s["b3"], name=nm("merge_fwd"))
    s["mix"] = _mm(s["merged"], w["w_out"], name=nm("out_proj"))
    s["x1"] = _ln_fwd(x, s["mix"], row(w["ln1_g"]), row(w["ln1_b"]), name=nm("ln1_fwd"))
    s["gate"], s["up"], s["act"] = _ffn_in_swiglu(s["x1"], w["ffn_w_in"], name=nm("ffn_in"))
    s["f"] = _mm(s["act"], w["ffn_w_down"], name=nm("ffn_down"))
    s["x2"] = _ln_fwd(s["x1"], s["f"], row(w["ln2_g"]), row(w["ln2_b"]), name=nm("ln2_fwd"))
    return s


def _layer_bwd(s, mem, w, dxo, l, midway=None):
    nm = lambda t: f"{t}_l{l}"
    row = lambda v: v.reshape(1, -1)
    slabs = lambda a: a.reshape(N_DEV, a.shape[0] // N_DEV, a.shape[1])
    g = {}
    du2, dg, db = _ln_bwd(s["x1"], s["f"], dxo, row(w["ln2_g"]), name=nm("ln2_bwd"))
    g["ln2_g"], g["ln2_b"] = dg[0], db[0]
    dgate, dup = _d_swiglu(du2, w["ffn_w_down"], s["gate"], s["up"], name=nm("d_swiglu"))
    g["ffn_w_down"] = slabs(_mm(s["act"], du2, ta=True, name=nm("dw_ffn_down")))
    dx1 = _mm(dgate, w["ffn_w_in"][:, :D_FF], tb=True, add=du2, add_scale=ALPHA, name=nm("d_x1_gate"))
    dx1 = _mm(dup, w["ffn_w_in"][:, D_FF:], tb=True, add=dx1, name=nm("d_x1_up"))
    g["ffn_w_in"] = _cols_split([_mm(s["x1"], dgate, ta=True, name=nm("dw_ffn_gate")),
                                 _mm(s["x1"], dup, ta=True, name=nm("dw_ffn_up"))], name=nm("dw_ffn_in_shards"))
    du1, dg, db = _ln_bwd(s["x"], s["mix"], dx1, row(w["ln1_g"]), name=nm("ln1_bwd"))
    g["ln1_g"], g["ln1_b"] = dg[0], db[0]
    dmerged = _mm(du1, w["w_out"], tb=True, name=nm("d_merged"))
    if midway is not None:
        dmerged = midway(dmerged)
    g["w_out"] = slabs(_mm(s["merged"], du1, ta=True, name=nm("dw_out")))
    dp_gl, d1, d2, d3, dbg = _merge_bwd(s["p_gl"], s["bg"], s["b1"], s["b2"], s["b3"], dmerged, name=nm("merge_bwd"))
    g["b_gate"] = dbg.reshape(3, D)
    dy_lru = _mm(d1, w["w_br_lru"], tb=True, name=nm("d_y_lru"))
    g["w_br_lru"] = slabs(_mm(s["y_lru"], d1, ta=True, name=nm("dw_br_lru")))
    dy_ssd = _mm(d2, w["w_br_ssd"], tb=True, name=nm("d_y_ssd"))
    g["w_br_ssd"] = slabs(_mm(s["y_ssd"], d2, ta=True, name=nm("dw_br_ssd")))
    dy_xa = _mm(d3, w["w_br_xa"], tb=True, name=nm("d_y_xa"))
    g["w_br_xa"] = slabs(_mm(s["y_xa"], d3, ta=True, name=nm("dw_br_xa")))
    dp_q, dkv = _xa_bwd(s["p_q"], s["kv"], dy_xa, name=nm("xa_bwd"))
    g["mem_w_kv"] = _mm(mem, dkv, ta=True, split_n=2 * D // N_DEV, name=nm("dw_kv"))
    ssd_cw, _, *ssd_rest = s["ssd_par"]
    dp_ssd, dcw, dcb, ddtb, dalog, dd, dnw = _ssd_bwd(s["p_ssd"], s["c_ssd"], s["yraw"], s["hs"], dy_ssd, ssd_cw,
                                                      *ssd_rest, name=nm("ssd_bwd"))
    g["ssd_conv_w"], g["ssd_conv_b"], g["ssd_dt_bias"] = dcw, dcb[0], ddtb[0]
    g["ssd_a_log"], g["ssd_d"], g["ssd_norm_w"] = dalog[0], dd[0], dnw[0]
    lru_cw, _, *lru_rest = s["lru_par"]
    dp_lru, dcw, dcb, dwa, dba, dwi, dbi, dlam = _lru_bwd(s["p_lru"], s["xc"], s["h"], dy_lru, lru_cw, *lru_rest,
                                                          name=nm("lru_bwd"))
    g["lru_conv_w"], g["lru_conv_b"], g["lru_w_a"], g["lru_b_a"] = dcw, dcb[0], dwa, dba[0]
    g["lru_w_i"], g["lru_b_i"], g["lru_lambda"] = dwi, dbi[0], dlam[0]
    wi = s["wi"]
    dx = _mm(dp_ssd, wi["ssd"], tb=True, add=du1, add_scale=ALPHA, name=nm("dx_ssd"))
    dx = _mm(dp_lru, wi["lru"], tb=True, add=dx, name=nm("dx_lru"))
    dx = _mm(dp_q, wi["q"], tb=True, add=dx, name=nm("dx_q"))
    dx = _mm(dp_gl, wi["gl"], tb=True, add=dx, name=nm("dx_gl"))
    x = s["x"]
    g["w_in"] = _w_in_shards(_mm(x, dp_ssd, ta=True, name=nm("dw_in_ssd")), _mm(x, dp_lru, ta=True, name=nm("dw_in_lru")),
                             _mm(x, dp_q, ta=True, name=nm("dw_in_q")), _mm(x, dp_gl, ta=True, name=nm("dw_in_gl")),
                             name=nm("dw_in_shards"))
    return dx, g


def _local_step(x, mem, target, layers, between=None, midway=None):
    saved = []
    for l in range(DEPTH):
        saved.append(_layer_fwd(x, mem, layers[l], l))
        x = saved[-1]["x2"]
    dx, loss = _loss_head(x, target, name="loss_head")
    grads = [None] * DEPTH
    for l in reversed(range(DEPTH)):
        dx, grads[l] = _layer_bwd(saved[l], mem, layers[l], dx, l, midway if l == 0 else None)
        if l == DEPTH - 1 and between is not None:
            between(grads[l])
    return loss, dx, grads


def kernel(x, mem, w_in, b_gate, lru_conv_w, lru_conv_b, lru_w_a, lru_b_a, lru_w_i, lru_b_i, lru_lambda, ssd_conv_w, ssd_conv_b, ssd_dt_bias, ssd_a_log, ssd_d, ssd_norm_w, mem_w_kv, w_br_lru, w_br_ssd, w_br_xa, w_out, ln1_g, ln1_b, ffn_w_in, ffn_w_down, ln2_g, ln2_b, loss_target, m_w_in, m_b_gate, m_lru_conv_w, m_lru_conv_b, m_lru_w_a, m_lru_b_a, m_lru_w_i, m_lru_b_i, m_lru_lambda, m_ssd_conv_w, m_ssd_conv_b, m_ssd_dt_bias, m_ssd_a_log, m_ssd_d, m_ssd_norm_w, m_mem_w_kv, m_w_br_lru, m_w_br_ssd, m_w_br_xa, m_w_out, m_ln1_g, m_ln1_b, m_ffn_w_in, m_ffn_w_down, m_ln2_g, m_ln2_b, v_w_in, v_b_gate, v_lru_conv_w, v_lru_conv_b, v_lru_w_a, v_lru_b_a, v_lru_w_i, v_lru_b_i, v_lru_lambda, v_ssd_conv_w, v_ssd_conv_b, v_ssd_dt_bias, v_ssd_a_log, v_ssd_d, v_ssd_norm_w, v_mem_w_kv, v_w_br_lru, v_w_br_ssd, v_w_br_xa, v_w_out, v_ln1_g, v_ln1_b, v_ffn_w_in, v_ffn_w_down, v_ln2_g, v_ln2_b):
    local = dict(locals())
    w = {n: local[n] for n in _ORDER}
    m = {n: local["m_" + n] for n in _ORDER}
    v = {n: local["v_" + n] for n in _ORDER}

    big = [n for n, _, _ in _BIG]
    kinds = {n: kind for n, kind, _ in _BIG}

    small = _rows(jnp.concatenate([w[n].reshape(-1) for n, _ in _SMALL]), R_SMALL)
    first = _all_gather([w["w_in"][0].astype(BF16), small], name="gather_w_in_l0")
    rest, later, _ = lax.optimization_barrier(([w[n][0].astype(BF16) for n in big[1:]],
                                               [w[n][1].astype(BF16) for n in big], first[-1]))
    rest = _all_gather(rest, "gather_weights_l0", collective_id=1)
    later = _all_gather(later, "gather_weights_l1", collective_id=4)
    stacks = [dict(zip(big, [first[0]] + rest)), dict(zip(big, later))]
    small_all, o, small_full = first[-1].reshape(N_DEV, R_SMALL * LANES), 0, {}
    for n, s in _SMALL:
        k = DEPTH * s[0] * s[1]
        small_full[n] = _from_stack(small_all[:, o:o + k].reshape((N_DEV, DEPTH) + s))
        o += k
    layers = []
    for l in range(DEPTH):
        lw = {n: w[n][l] for n, _ in _REP}
        lw.update({n: small_full[n][l] for n, _ in _SMALL})
        lw["wi"] = dict(zip(("ssd", "lru", "q", "gl"), _w_in_regions(stacks[l]["w_in"], name=f"w_in_regions_l{l}")))
        for n in big[1:]:
            if kinds[n] == "col":
                lw[n] = _cols_concat(stacks[l][n], name=f"full_{n}_l{l}")
            else:
                lw[n] = stacks[l][n].reshape(-1, stacks[l][n].shape[-1])
        layers.append(lw)

    def pair_sums(owns, gots, tag):
        return [_pair_sum(own, got, name=f"pair_sum_{tag}_{i}", out_dtype=BF16, tb=_rowblk(own.shape[2], 256))
                for i, (own, got) in enumerate(zip(owns, gots))]

    last = {}

    def between(g_last):
        last["own"] = [g_last[n].reshape((4, 2) + g_last[n].shape[1:]) for n in big]
        last["got"] = _exchange_sibling(last["own"], name="reduce_cores_l1", collective_id=2)

    def midway(t):
        t, got = lax.optimization_barrier((t, last["got"]))
        t, sums = lax.optimization_barrier((t, pair_sums(last["own"], got, "l1")))
        last["slots"] = _exchange_chips(sums, name="reduce_chips_l1", collective_id=3)
        return t

    loss_tile, dx, grads = _local_step(x[0], mem[0], loss_target[0], layers, between, midway)
    loss = lax.psum(loss_tile[0, 0], ("x", "y", "c"))
    stacked = {n: jnp.stack([grads[l][n] for l in range(DEPTH)]) for n in [s[0] for s in _SMALL + _REP]}
    sm = jnp.concatenate([_by_dest(stacked[n]) for n, _ in _SMALL], axis=1)
    sm = jnp.pad(sm, ((0, 0), (0, R_SMALL * LANES - sm.shape[1])))
    rep = jnp.concatenate([stacked[n].reshape(-1) for n, _ in _REP])
    rep = jnp.pad(rep, (0, N_DEV * R_REP * LANES - rep.shape[0])).reshape(N_DEV, R_REP * LANES)
    tail = jnp.concatenate([sm, rep, jnp.zeros((N_DEV, (R_SM - R_SMALL - R_REP) * LANES), F32)], axis=1)
    owns = [grads[0][n].reshape((4, 2) + grads[0][n].shape[1:]) for n in big] + [tail.reshape(4, 2, R_SM, LANES)]
    gots = _exchange_sibling(owns, name="reduce_cores_l0")
    sums = pair_sums(owns[:-1], gots[:-1], "l0")
    sums.append(_pair_sum(owns[-1], gots[-1], name="pair_sum_tail", out_dtype=F32, tb=R_SM))
    slots = _exchange_chips(sums, name="reduce_chips_l0")

    res = {}
    for i, n in enumerate(big):
        tb = _rowblk(w[n].shape[1], 128 if w[n].shape[2] > LANES else 256)
        res[n] = _adamw([slots[i], last["slots"][i]], w[n], m[n], v[n], name=f"adamw_{n}", tb=tb)
    tail_sum = _slot_sum(slots[-1], name="sum_tail")
    rep_all = _all_gather([tail_sum[R_SMALL:R_SMALL + R_REP]], name="gather_replicated")[0]
    g_tail = jnp.concatenate([tail_sum[:R_SMALL], rep_all.reshape(N_DEV * R_REP, LANES)], axis=0)
    tails = _adamw([g_tail[None]], _pack_tail(w)[None], _pack_tail(m)[None], _pack_tail(v)[None],
                   name="adamw_tail", tb=TB_TAIL)

    outs = []
    for kind in range(4):
        d = {**{n: res[n][kind] for n in big}, **_unpack_tail(tails[kind][0])}
        outs += [d[n] for n in _ORDER]
    return (loss, dx[None], *outs)
```

```python
import math

import jax
import jax.numpy as jnp
from jax import lax
from jax.experimental import pallas as pl
from jax.experimental.pallas import tpu as pltpu
from jax.experimental.pallas import tpu_sc as plsc

F32 = jnp.float32
BF16 = jnp.bfloat16

D = 1024
DEPTH = 2
N_DEV = 8
CHUNK = 64
LRU_BLOCKS = 8
LRU_BLOCK = 128
LRU_C = 8.0
D_SSD = 2 * D
SSD_HEADS = 32
SSD_GROUPS = 4
GROUP_W = D_SSD // SSD_GROUPS
SSD_STATE = 128
D_XBC = D_SSD + 2 * SSD_GROUPS * SSD_STATE
XA_HEADS = 4
XA_HEAD_DIM = 256
D_FF = 2816
ALPHA = (2 * DEPTH) ** 0.25
EPS = 1e-5
N_IN = 11296

S_Z, S_XBC, S_DT, W_SSD = 0, 2048, 5120, 5632
DT_REAL = 32

ADAM_LR, ADAM_B1, ADAM_B2, ADAM_EPS, ADAM_WD, ADAM_STEP = 0.001, 0.9, 0.999, 1e-08, 0.01, 10

VMEM_LIMIT = 56 * 1024 * 1024
MESH = pl.DeviceIdType.MESH
ANY = pl.BlockSpec(memory_space=pl.ANY)


def _cp(*sem):
    return pltpu.CompilerParams(dimension_semantics=sem, vmem_limit_bytes=VMEM_LIMIT)


def _blk(n, target):
    if n % 128:
        return n
    best = 128
    for b in range(128, min(n, target) + 1, 128):
        if n % b == 0:
            best = b
    return best


def _iota(shape, dim):
    return lax.broadcasted_iota(jnp.int32, shape, dim)


def _sigmoid(x):
    return 1.0 / (1.0 + jnp.exp(-x))


def _log1p(e):
    u = 1.0 + e
    return jnp.where(u == 1.0, e, jnp.log(u) * (e / (u - 1.0)))


def _softplus(x):
    return jnp.maximum(x, 0.0) + _log1p(jnp.exp(-jnp.abs(x)))


def _expm1(x):
    u = jnp.exp(x)
    um = u - 1.0
    return jnp.where(um == 0.0, x, jnp.where(um == -1.0, -1.0, um * (x / jnp.log(u))))


_G0 = math.sqrt(2.0 / math.pi)
_G1 = 0.044715


def _gelu_and_grad(x):
    t = jnp.tanh(_G0 * (x + _G1 * x * x * x))
    g = 0.5 * x * (1.0 + t)
    dg = 0.5 * (1.0 + t) + 0.5 * x * (1.0 - t * t) * (_G0 * (1.0 + 3.0 * _G1 * x * x))
    return g, dg


_NN = (((1,), (0,)), ((), ()))
_NT = (((1,), (1,)), ((), ()))
_TN = (((0,), (0,)), ((), ()))


def _dot(a, b, dims=_NN):
    return lax.dot_general(a.astype(BF16), b.astype(BF16), dims, preferred_element_type=F32)


def _dot_hi(a, b, dims=_NN):
    return lax.dot_general(a, b, dims, precision=lax.Precision.HIGHEST, preferred_element_type=F32)


def _split(v):
    hi = v.astype(BF16)
    return hi, (v - hi.astype(F32)).astype(BF16)


def _dot01(v, e, dims=_NN):
    hi, lo = _split(v)
    return (lax.dot_general(hi, e, dims, preferred_element_type=F32)
            + lax.dot_general(lo, e, dims, preferred_element_type=F32))


def _conv_taps(xe, n):
    return [xe[8:8 + n] if j == 3 else pltpu.roll(xe, 3 - j, 0)[8:8 + n] for j in range(4)]


def _conv_fwd(taps, cw, cb):
    return cb + cw[0:1] * taps[0] + cw[1:2] * taps[1] + cw[2:3] * taps[2] + cw[3:4] * taps[3]


def _conv_bwd(dc, dnext, x, cw, n):
    ext = jnp.concatenate([dc, dnext], axis=0)
    shifted = [pltpu.roll(ext, n + 8 - (3 - j), 0)[0:n] for j in range(3)] + [dc]
    dx = cw[0:1] * shifted[0] + cw[1:2] * shifted[1] + cw[2:3] * shifted[2] + cw[3:4] * dc
    dcw = jnp.concatenate([jnp.sum(x * shifted[j], axis=0, keepdims=True) for j in range(4)], axis=0)
    return dx, dcw, jnp.sum(dc, axis=0, keepdims=True)


MM_VMEM_BUDGET = 44 * 1024 * 1024
MM_MAX_TILE = 1408
MM_MAX_K = 5632


def _divisors(n, cap):
    return [n] if n % 128 else [b for b in range(128, min(n, cap) + 1, 128) if n % b == 0]


def _mm_tiles(m_dim, n_dim, k_dim, a_bytes, b_bytes, o_bytes, has_add, tn_fixed):
    best = None
    for tm in _divisors(m_dim, MM_MAX_TILE):
        for tn in ([tn_fixed] if tn_fixed else _divisors(n_dim, MM_MAX_TILE)):
            for tk in _divisors(k_dim, MM_MAX_K):
                vmem = 2 * (tm * tk * a_bytes + tk * tn * b_bytes + tm * tn * (o_bytes + (4 if has_add else 0)))
                vmem += tm * tn * 4 if tk < k_dim else 0
                if vmem <= MM_VMEM_BUDGET:
                    key = (tm * tn * tk, tk, tn)
                    if best is None or key > best[0]:
                        best = (key, (tm, tn, tk))
    assert best is not None, (m_dim, n_dim, k_dim)
    return best[1]


def _mm(a, b, *, ta=False, tb=False, out_dtype=F32, add=None, add_scale=1.0, name, split_n=None):
    if ta:
        k_dim, m_dim = a.shape
    else:
        m_dim, k_dim = a.shape
    if tb:
        n_dim, k2 = b.shape
    else:
        k2, n_dim = b.shape
    assert k_dim == k2, (a.shape, b.shape, ta, tb)
    tm, tn, tk = _mm_tiles(m_dim, n_dim, k_dim, a.dtype.itemsize, b.dtype.itemsize, jnp.dtype(out_dtype).itemsize,
                           add is not None, split_n)
    nk = k_dim // tk
    a_spec = pl.BlockSpec((tk, tm), lambda i, j, k: (k, i)) if ta else pl.BlockSpec((tm, tk), lambda i, j, k: (i, k))
    b_spec = pl.BlockSpec((tn, tk), lambda i, j, k: (j, k)) if tb else pl.BlockSpec((tk, tn), lambda i, j, k: (k, j))
    o_spec = pl.BlockSpec((tm, tn), lambda i, j, k: (i, j))
    out_shape = (m_dim, n_dim)
    if split_n is not None:
        assert add is None and tn == split_n, (tn, split_n)
        o_spec = pl.BlockSpec((None, tm, tn), lambda i, j, k: (j, i, 0))
        out_shape = (n_dim // tn, m_dim, tn)
    dims = (((0 if ta else 1,), (1 if tb else 0,)), ((), ()))
    has_add = add is not None

    def body(*refs):
        a_ref, b_ref = refs[:2]
        add_ref = refs[2] if has_add else None
        o_ref = refs[3] if has_add else refs[2]
        acc_ref = refs[-1] if nk > 1 else None
        k = pl.program_id(2)

        def product():
            return lax.dot_general(a_ref[...].astype(BF16), b_ref[...].astype(BF16), dims, preferred_element_type=F32)

        def finish(r):
            if has_add:
                r = r + add_scale * add_ref[...]
            o_ref[...] = r.astype(out_dtype)

        if nk == 1:
            finish(product())
            return

        @pl.when(k == 0)
        def _():
            acc_ref[...] = product()

        @pl.when((k > 0) & (k < nk - 1))
        def _():
            acc_ref[...] += product()

        @pl.when(k == nk - 1)
        def _():
            finish(acc_ref[...] + product())

    in_specs = [a_spec, b_spec] + ([o_spec] if has_add else [])
    args = (a, b) + ((add,) if has_add else ())
    return pl.pallas_call(
        body, name=name, grid=(m_dim // tm, n_dim // tn, nk),
        in_specs=in_specs, out_specs=o_spec,
        out_shape=jax.ShapeDtypeStruct(out_shape, out_dtype),
        scratch_shapes=[pltpu.VMEM((tm, tn), F32)] if nk > 1 else [],
        compiler_params=_cp("parallel", "parallel", "arbitrary"),
    )(*args)


def _ln_fwd(x, f, g, b, name, tb=512):
    t = x.shape[0]
    tb = min(tb, t)

    def body(x_ref, f_ref, g_ref, b_ref, o_ref):
        u = ALPHA * x_ref[...] + f_ref[...]
        mu = jnp.mean(u, axis=-1, keepdims=True)
        d = u - mu
        var = jnp.mean(d * d, axis=-1, keepdims=True)
        o_ref[...] = d * lax.rsqrt(var + EPS) * g_ref[...] + b_ref[...]

    row = pl.BlockSpec((tb, D), lambda i: (i, 0))
    par = pl.BlockSpec((1, D), lambda i: (0, 0))
    return pl.pallas_call(
        body, name=name, grid=(t // tb,), in_specs=[row, row, par, par], out_specs=row,
        out_shape=jax.ShapeDtypeStruct((t, D), F32), compiler_params=_cp("parallel"),
    )(x, f, g, b)


def _ln_bwd(x, f, dy, g, name, tb=512):
    t = x.shape[0]
    tb = min(tb, t)

    def body(x_ref, f_ref, dy_ref, g_ref, du_ref, dg_ref, db_ref):
        @pl.when(pl.program_id(0) == 0)
        def _():
            dg_ref[...] = jnp.zeros_like(dg_ref)
            db_ref[...] = jnp.zeros_like(db_ref)

        u = ALPHA * x_ref[...] + f_ref[...]
        mu = jnp.mean(u, axis=-1, keepdims=True)
        d = u - mu
        var = jnp.mean(d * d, axis=-1, keepdims=True)
        rstd = lax.rsqrt(var + EPS)
        xhat = d * rstd
        dy = dy_ref[...]
        dxh = dy * g_ref[...]
        m1 = jnp.mean(dxh, axis=-1, keepdims=True)
        m2 = jnp.mean(dxh * xhat, axis=-1, keepdims=True)
        du_ref[...] = rstd * (dxh - m1 - xhat * m2)
        dg_ref[...] += jnp.sum(dy * xhat, axis=0, keepdims=True)
        db_ref[...] += jnp.sum(dy, axis=0, keepdims=True)

    row = pl.BlockSpec((tb, D), lambda i: (i, 0))
    par = pl.BlockSpec((1, D), lambda i: (0, 0))
    return pl.pallas_call(
        body, name=name, grid=(t // tb,), in_specs=[row, row, row, par], out_specs=[row, par, par],
        out_shape=[jax.ShapeDtypeStruct((t, D), F32), jax.ShapeDtypeStruct((1, D), F32),
                   jax.ShapeDtypeStruct((1, D), F32)],
        compiler_params=_cp("arbitrary"),
    )(x, f, dy, g)


FFN_TM, FFN_TN = 512, D_FF // 2


def _ffn_in_swiglu(x, w, name):
    t = x.shape[0]
    tm = min(FFN_TM, t)
    nj = D_FF // FFN_TN

    def body(x_ref, wg_ref, wu_ref, g_ref, u_ref, a_ref):
        xb = x_ref[...].astype(BF16)
        g = lax.dot_general(xb, wg_ref[...], _NN, preferred_element_type=F32)
        u = lax.dot_general(xb, wu_ref[...], _NN, preferred_element_type=F32)
        g_ref[...] = g
        u_ref[...] = u
        a_ref[...] = (g * _sigmoid(g) * u).astype(BF16)

    tile = pl.BlockSpec((tm, FFN_TN), lambda i, j: (i, j))
    return pl.pallas_call(
        body, name=name, grid=(t // tm, nj),
        in_specs=[pl.BlockSpec((tm, D), lambda i, j: (i, 0)), pl.BlockSpec((D, FFN_TN), lambda i, j: (0, j)),
                  pl.BlockSpec((D, FFN_TN), lambda i, j: (0, nj + j))],
        out_specs=[tile, tile, tile],
        out_shape=[jax.ShapeDtypeStruct((t, D_FF), F32), jax.ShapeDtypeStruct((t, D_FF), F32),
                   jax.ShapeDtypeStruct((t, D_FF), BF16)],
        compiler_params=_cp("parallel", "parallel"),
    )(x, w, w)


def _d_swiglu(du, w_down, g, u, name):
    t = du.shape[0]
    tm = min(FFN_TM, t)

    def body(du_ref, w_ref, g_ref, u_ref, dg_ref, dup_ref):
        da = lax.dot_general(du_ref[...].astype(BF16), w_ref[...], _NT, preferred_element_type=F32)
        g_v = g_ref[...]
        s = _sigmoid(g_v)
        dg_ref[...] = (da * u_ref[...] * (s * (1.0 + g_v * (1.0 - s)))).astype(BF16)
        dup_ref[...] = (da * g_v * s).astype(BF16)

    tile = pl.BlockSpec((tm, FFN_TN), lambda i, j: (i, j))
    return pl.pallas_call(
        body, name=name, grid=(t // tm, D_FF // FFN_TN),
        in_specs=[pl.BlockSpec((tm, D), lambda i, j: (i, 0)), pl.BlockSpec((FFN_TN, D), lambda i, j: (j, 0)), tile, tile],
        out_specs=[tile, tile],
        out_shape=[jax.ShapeDtypeStruct((t, D_FF), BF16), jax.ShapeDtypeStruct((t, D_FF), BF16)],
        compiler_params=_cp("parallel", "parallel"),
    )(du, w_down, g, u)


def _merge_fwd(pgl, bg, b1, b2, b3, name, tb=512):
    t = pgl.shape[0]
    tb = min(tb, t)

    def body(gl_ref, bg_ref, b1_ref, b2_ref, b3_ref, o_ref):
        acc = None
        for j, b_ref in enumerate((b1_ref, b2_ref, b3_ref)):
            sl = slice(j * D, (j + 1) * D)
            term = _sigmoid(gl_ref[:, sl] + bg_ref[:, sl]) * b_ref[...]
            acc = term if acc is None else acc + term
        o_ref[...] = acc.astype(BF16)

    row = pl.BlockSpec((tb, D), lambda i: (i, 0))
    return pl.pallas_call(
        body, name=name, grid=(t // tb,),
        in_specs=[pl.BlockSpec((tb, 3 * D), lambda i: (i, 0)), pl.BlockSpec((1, 3 * D), lambda i: (0, 0)), row, row, row],
        out_specs=row, out_shape=jax.ShapeDtypeStruct((t, D), BF16), compiler_params=_cp("parallel"),
    )(pgl, bg, b1, b2, b3)


def _merge_bwd(pgl, bg, b1, b2, b3, dm, name, tb=512):
    t = pgl.shape[0]
    tb = min(tb, t)

    def body(gl_ref, bg_ref, b1_ref, b2_ref, b3_ref, dm_ref, dgl_ref, d1_ref, d2_ref, d3_ref, dbg_ref):
        @pl.when(pl.program_id(0) == 0)
        def _():
            dbg_ref[...] = jnp.zeros_like(dbg_ref)

        dm_v = dm_ref[...]
        for j, (b_ref, d_ref) in enumerate(((b1_ref, d1_ref), (b2_ref, d2_ref), (b3_ref, d3_ref))):
            sl = slice(j * D, (j + 1) * D)
            gate = _sigmoid(gl_ref[:, sl] + bg_ref[:, sl])
            d_ref[...] = (dm_v * gate).astype(BF16)
            dgl = dm_v * b_ref[...] * (gate * (1.0 - gate))
            dgl_ref[:, sl] = dgl.astype(BF16)
            dbg_ref[:, sl] += jnp.sum(dgl, axis=0, keepdims=True)

    row = pl.BlockSpec((tb, D), lambda i: (i, 0))
    wide = pl.BlockSpec((tb, 3 * D), lambda i: (i, 0))
    par = pl.BlockSpec((1, 3 * D), lambda i: (0, 0))
    return pl.pallas_call(
        body, name=name, grid=(t // tb,),
        in_specs=[wide, par, row, row, row, row], out_specs=[wide, row, row, row, par],
        out_shape=[jax.ShapeDtypeStruct((t, 3 * D), BF16)] + [jax.ShapeDtypeStruct((t, D), BF16)] * 3
                  + [jax.ShapeDtypeStruct((1, 3 * D), F32)],
        compiler_params=_cp("arbitrary"),
    )(pgl, bg, b1, b2, b3, dm)


def _xa_probs(q, kv_ref, hd):
    sl = slice(hd * XA_HEAD_DIM, (hd + 1) * XA_HEAD_DIM)
    k = kv_ref[:, sl]
    v = kv_ref[:, D + hd * XA_HEAD_DIM:D + (hd + 1) * XA_HEAD_DIM]
    s = _dot(q[:, sl], k, _NT) * (XA_HEAD_DIM ** -0.5)
    e = jnp.exp(s - jnp.max(s, axis=1, keepdims=True))
    return sl, k, v, e / jnp.sum(e, axis=1, keepdims=True)


def _xa_fwd(pq, kv, name, tb=512):
    t = pq.shape[0]
    tb = min(tb, t)

    def body(q_ref, kv_ref, o_ref):
        q = q_ref[...]
        for hd in range(XA_HEADS):
            sl, _, v, p = _xa_probs(q, kv_ref, hd)
            o_ref[:, sl] = _dot(p, v).astype(BF16)

    row = pl.BlockSpec((tb, D), lambda i: (i, 0))
    return pl.pallas_call(
        body, name=name, grid=(t // tb,),
        in_specs=[row, pl.BlockSpec(kv.shape, lambda i: (0, 0))], out_specs=row,
        out_shape=jax.ShapeDtypeStruct((t, D), BF16), compiler_params=_cp("parallel"),
    )(pq, kv)


def _xa_bwd(pq, kv, dy, name, tb=512):
    t = pq.shape[0]
    tb = min(tb, t)

    def body(q_ref, kv_ref, dy_ref, dq_ref, dkv_ref):
        @pl.when(pl.program_id(0) == 0)
        def _():
            dkv_ref[...] = jnp.zeros_like(dkv_ref)

        q = q_ref[...]
        for hd in range(XA_HEADS):
            sl, k, v, p = _xa_probs(q, kv_ref, hd)
            dyh = dy_ref[:, sl]
            vsl = slice(D + hd * XA_HEAD_DIM, D + (hd + 1) * XA_HEAD_DIM)
            dkv_ref[:, vsl] += _dot(p, dyh, _TN)
            dp = _dot(dyh, v, _NT)
            ds = p * (dp - jnp.sum(dp * p, axis=1, keepdims=True)) * (XA_HEAD_DIM ** -0.5)
            dq_ref[:, sl] = _dot(ds, k).astype(BF16)
            dkv_ref[:, sl] += _dot(ds, q[:, sl], _TN)

    row = pl.BlockSpec((tb, D), lambda i: (i, 0))
    kvs = pl.BlockSpec(kv.shape, lambda i: (0, 0))
    return pl.pallas_call(
        body, name=name, grid=(t // tb,), in_specs=[row, kvs, row], out_specs=[row, kvs],
        out_shape=[jax.ShapeDtypeStruct((t, D), BF16), jax.ShapeDtypeStruct(kv.shape, F32)],
        compiler_params=_cp("arbitrary"),
    )(pq, kv, dy)


def _scan_fwd(a, u):
    n = a.shape[0]
    row = _iota((n, 1), 0)
    d = 1
    while d < n:
        us = jnp.where(row >= d, pltpu.roll(u, d, 0), 0.0)
        u = a * us + u
        a = a * pltpu.roll(a, d, 0)
        d *= 2
    return u


def _scan_rev(b, u):
    n = b.shape[0]
    row = _iota((n, 1), 0)
    d = 1
    while d < n:
        us = jnp.where(row < n - d, pltpu.roll(u, n - d, 0), 0.0)
        u = b * us + u
        b = b * pltpu.roll(b, n - d, 0)
        d *= 2
    return u


def _lru_gates(xc, wa_ref, ba, wi_ref, bi, lam):
    za = jnp.concatenate([_dot(xc[:, n * 128:(n + 1) * 128], wa_ref[n]) for n in range(LRU_BLOCKS)], axis=1) + ba
    zi = jnp.concatenate([_dot(xc[:, n * 128:(n + 1) * 128], wi_ref[n]) for n in range(LRU_BLOCKS)], axis=1) + bi
    r = _sigmoid(za)
    ig = _sigmoid(zi)
    sp = _softplus(-lam)
    log_a = (-LRU_C) * r * sp
    a = jnp.exp(log_a)
    m = jnp.sqrt(-_expm1(2.0 * log_a))
    u = m * (ig * xc)
    return a, u, r, ig, m, sp


def _lru_fwd(p, cw, cb, wa, ba, wi, bi, lam, name, tb=256):
    t = p.shape[0]
    tb = min(tb, t)
    nb = t // tb
    r8 = tb // 8

    def body(x_ref, xp_ref, g_ref, cw_ref, cb_ref, wa_ref, ba_ref, wi_ref, bi_ref, lam_ref, y_ref, h_ref, xc_ref,
             hc_ref):
        i = pl.program_id(0)

        @pl.when(i == 0)
        def _():
            hc_ref[...] = jnp.zeros_like(hc_ref)

        halo = jnp.where(i == 0, 0.0, xp_ref[...])
        taps = _conv_taps(jnp.concatenate([halo, x_ref[...]], axis=0), tb)
        xc = _conv_fwd(taps, cw_ref[...], cb_ref[...])
        xc_ref[...] = xc
        a, u, _, _, _, _ = _lru_gates(xc, wa_ref, ba_ref[...], wi_ref, bi_ref[...], lam_ref[...])
        row = _iota((tb, 1), 0)
        u = u + jnp.where(row == 0, a * hc_ref[...], 0.0)
        h = _scan_fwd(a, u)
        h_ref[...] = h
        hc_ref[...] = h[tb - 1:tb, :]
        gl, _ = _gelu_and_grad(g_ref[...])
        y_ref[...] = (gl * h).astype(BF16)

    par = pl.BlockSpec((1, D), lambda i: (0, 0))
    wsp = pl.BlockSpec((LRU_BLOCKS, LRU_BLOCK, LRU_BLOCK), lambda i: (0, 0, 0))
    row = pl.BlockSpec((tb, D), lambda i: (i, 0))
    return pl.pallas_call(
        body, name=name, grid=(nb,),
        in_specs=[row, pl.BlockSpec((8, D), lambda i: (jnp.maximum(i * r8 - 1, 0), 0)),
                  pl.BlockSpec((tb, D), lambda i: (i, 1)),
                  pl.BlockSpec((4, D), lambda i: (0, 0)), par, wsp, par, wsp, par, par],
        out_specs=[row, row, row],
        out_shape=[jax.ShapeDtypeStruct((t, D), BF16), jax.ShapeDtypeStruct((t, D), F32),
                   jax.ShapeDtypeStruct((t, D), F32)],
        scratch_shapes=[pltpu.VMEM((1, D), F32)],
        compiler_params=_cp("arbitrary"),
    )(p, p, p, cw, cb, wa, ba, wi, bi, lam)


def _lru_bwd(p, xc, h, dy, cw, wa, ba, wi, bi, lam, name, tb=256):
    t = p.shape[0]
    tb = min(tb, t)
    nb = t // tb
    r8 = tb // 8

    def body(x_ref, g_ref, xc_ref, h_ref, hp_ref, dy_ref, cw_ref, wa_ref, ba_ref, wi_ref, bi_ref, lam_ref,
             dp_ref, dcw_ref, dcb_ref, dwa_ref, dba_ref, dwi_ref, dbi_ref, dlam_ref, carry_ref, dnext_ref):
        i = pl.program_id(0)
        blk = nb - 1 - i

        @pl.when(i == 0)
        def _():
            for r in (dcw_ref, dcb_ref, dwa_ref, dba_ref, dwi_ref, dbi_ref, dlam_ref, carry_ref, dnext_ref):
                r[...] = jnp.zeros_like(r)

        xc = xc_ref[...]
        lam = lam_ref[...]
        a, _, r, ig, m, sp = _lru_gates(xc, wa_ref, ba_ref[...], wi_ref, bi_ref[...], lam)
        gl, dgl = _gelu_and_grad(g_ref[...])
        h = h_ref[...]
        dy = dy_ref[...]
        dp_ref[:, D:] = (dy * h * dgl).astype(BF16)
        row = _iota((tb, 1), 0)
        dh = dy * gl + jnp.where(row == tb - 1, carry_ref[...], 0.0)
        b = jnp.where(row < tb - 1, pltpu.roll(a, tb - 1, 0), 0.0)
        gs = _scan_rev(b, dh)
        carry_ref[...] = a[0:1] * gs[0:1]
        h_last = jnp.where(blk == 0, 0.0, hp_ref[7:8, :])
        hprev = jnp.where(row == 0, h_last, pltpu.roll(h, 1, 0))
        da = gs * hprev
        dm = gs * ig * xc
        di = gs * m * xc
        dxc = gs * m * ig
        dlog = (0.5 * dm / m) * (-2.0 * a * a) + da * a
        dr = dlog * ((-LRU_C) * sp)
        dsp = jnp.sum(dlog * ((-LRU_C) * r), axis=0, keepdims=True)
        dlam_ref[...] += dsp * (-_sigmoid(-lam))
        dza = dr * r * (1.0 - r)
        dzi = di * ig * (1.0 - ig)
        dba_ref[...] += jnp.sum(dza, axis=0, keepdims=True)
        dbi_ref[...] += jnp.sum(dzi, axis=0, keepdims=True)
        parts = []
        for n in range(LRU_BLOCKS):
            sl = slice(n * 128, (n + 1) * 128)
            dwa_ref[n] += _dot(xc[:, sl], dza[:, sl], _TN)
            dwi_ref[n] += _dot(xc[:, sl], dzi[:, sl], _TN)
            parts.append(_dot(dza[:, sl], wa_ref[n], _NT) + _dot(dzi[:, sl], wi_ref[n], _NT))
        dxc = dxc + jnp.concatenate(parts, axis=1)
        dx, dcw, dcb = _conv_bwd(dxc, dnext_ref[...], x_ref[...], cw_ref[...], tb)
        dp_ref[:, :D] = dx.astype(BF16)
        dcw_ref[...] += dcw
        dcb_ref[...] += dcb
        dnext_ref[...] = dxc[0:8]

    par = pl.BlockSpec((1, D), lambda i: (0, 0))
    wsp = pl.BlockSpec((LRU_BLOCKS, LRU_BLOCK, LRU_BLOCK), lambda i: (0, 0, 0))
    cws = pl.BlockSpec((4, D), lambda i: (0, 0))
    rev = lambda i: nb - 1 - i
    blk0 = pl.BlockSpec((tb, D), lambda i: (rev(i), 0))
    w_shape = jax.ShapeDtypeStruct((LRU_BLOCKS, LRU_BLOCK, LRU_BLOCK), F32)
    v_shape = jax.ShapeDtypeStruct((1, D), F32)
    return pl.pallas_call(
        body, name=name, grid=(nb,),
        in_specs=[blk0, pl.BlockSpec((tb, D), lambda i: (rev(i), 1)), blk0, blk0,
                  pl.BlockSpec((8, D), lambda i: (jnp.maximum(rev(i) * r8 - 1, 0), 0)), blk0,
                  cws, wsp, par, wsp, par, par],
        out_specs=[pl.BlockSpec((tb, 2 * D), lambda i: (rev(i), 0)), cws, par, wsp, par, wsp, par, par],
        out_shape=[jax.ShapeDtypeStruct((t, 2 * D), BF16), jax.ShapeDtypeStruct((4, D), F32), v_shape,
                   w_shape, v_shape, w_shape, v_shape, v_shape],
        scratch_shapes=[pltpu.VMEM((1, D), F32), pltpu.VMEM((8, D), F32)],
        compiler_params=_cp("arbitrary"),
    )(p, p, xc, h, h, dy, cw, wa, ba, wi, bi, lam)


def _ssd_consts():
    m0 = _iota((1, 128), 1) < 64
    e = (jnp.right_shift(_iota((SSD_HEADS, D_SSD), 1), 6) == _iota((SSD_HEADS, D_SSD), 0)).astype(BF16)
    tril = (_iota((CHUNK, CHUNK), 0) >= _iota((CHUNK, CHUNK), 1)).astype(F32)
    eye = (_iota((SSD_HEADS, SSD_HEADS), 0) == _iota((SSD_HEADS, SSD_HEADS), 1)).astype(F32)
    r2 = _iota((CHUNK, 128), 0)
    c2 = jnp.bitwise_and(_iota((CHUNK, 128), 1), 63)
    return dict(m0=m0, e=e, tril=tril, eye=eye, causal2=r2 >= c2, fold=(c2 == r2).astype(BF16))


def _ssd_pre(c, p_ref, dtb_ref, alog_ref, dvec_ref, k):
    sg = _sigmoid(c)
    xbc = c * sg
    dtp = p_ref[:, S_DT:S_DT + DT_REAL] + dtb_ref[...]
    dt = _softplus(dtp)
    a = -jnp.exp(alog_ref[...])
    cs = _dot_hi(k["tril"], dt * a)
    cs_last = cs[CHUNK - 1:CHUNK]
    dend = jnp.exp(cs_last - cs)
    cdec = jnp.exp(cs_last)
    big = _dot01(jnp.concatenate([dt, jnp.exp(cs), dend], axis=0), k["e"])
    small = _dot01(jnp.concatenate([jnp.broadcast_to(cdec, (8, SSD_HEADS)),
                                    jnp.broadcast_to(dvec_ref[...], (8, SSD_HEADS))], axis=0), k["e"])
    cst2 = _dot_hi(k["eye"], jnp.concatenate([cs, cs], axis=0), _NT)
    return dict(c=c, sg=sg, xs=xbc[:, :D_SSD], bm=xbc[:, D_SSD:D_SSD + 512],
                cm=xbc[:, D_SSD + 512:], dtp=dtp, dt=dt, a=a, cs=cs, dend=dend, cdec=cdec,
                dtx=big[0:CHUNK], ecx=big[CHUNK:2 * CHUNK], dex=big[2 * CHUNK:3 * CHUNK],
                cdx=small[0:1], ddx=small[8:9], cst2=cst2)


def _pair_decay(p, cs, cst2, k):
    h0, h1 = 2 * p, 2 * p + 1
    colp = jnp.where(k["m0"], cs[:, h0:h0 + 1], cs[:, h1:h1 + 1])
    rowp = jnp.where(k["m0"], cst2[h0:h0 + 1, :], cst2[h1:h1 + 1, :])
    return jnp.where(k["causal2"], jnp.exp(colp - rowp), 0.0)


def _pair_stack(xp, k):
    return jnp.concatenate([jnp.where(k["m0"], xp, 0.0), jnp.where(k["m0"], 0.0, xp)], axis=0)


def _group_norm(yz, nw, with_stats=False):
    outs, stats = [], []
    for g in range(SSD_GROUPS):
        yzg = yz[:, g * GROUP_W:(g + 1) * GROUP_W]
        r = lax.rsqrt(jnp.mean(yzg * yzg, axis=1, keepdims=True) + EPS)
        outs.append(yzg * r)
        stats.append(r)
    y = jnp.concatenate(outs, axis=1) * nw
    return (y, stats) if with_stats else y


def _ssd_fwd(p, cw, cb, dtb, alog, dvec, nw, name):
    t = p.shape[0]
    nc = t // CHUNK

    def body(p_ref, pp_ref, cw_ref, cb_ref, dtb_ref, alog_ref, dvec_ref, nw_ref, y_ref, yraw_ref, hs_ref, c_ref,
             h_scr):
        i = pl.program_id(0)

        @pl.when(i == 0)
        def _():
            h_scr[...] = jnp.zeros_like(h_scr)

        k = _ssd_consts()
        halo = jnp.where(i == 0, 0.0, pp_ref[:, S_XBC:S_DT])
        taps = _conv_taps(jnp.concatenate([halo, p_ref[:, S_XBC:S_DT]], axis=0), CHUNK)
        c = _conv_fwd(taps, cw_ref[...], cb_ref[...])
        c_ref[...] = c
        s = _ssd_pre(c, p_ref, dtb_ref, alog_ref, dvec_ref, k)
        xs, bm, cm = s["xs"], s["bm"], s["cm"]
        xdt = xs * s["dtx"]
        hprev = h_scr[...]
        hs_ref[0] = hprev
        ys, hn = [], []
        for g in range(SSD_GROUPS):
            gs = slice(g * GROUP_W, (g + 1) * GROUP_W)
            bg = bm[:, g * 128:(g + 1) * 128]
            cg = cm[:, g * 128:(g + 1) * 128]
            cbdup = _dot(cg, jnp.concatenate([bg, bg], axis=0), _NT)
            hp_g = hprev[:, gs]
            yd = []
            for q in range(4):
                pr = g * 4 + q
                mp = cbdup * _pair_decay(pr, s["cs"], s["cst2"], k)
                yd.append(_dot(mp, _pair_stack(xdt[:, pr * 128:(pr + 1) * 128], k)))
            ys.append(jnp.concatenate(yd, axis=1) + _dot(cg, hp_g) * s["ecx"][:, gs])
            hn.append(hp_g * s["cdx"][:, gs] + _dot(bg, xdt[:, gs] * s["dex"][:, gs], _TN))
        h_scr[...] = jnp.concatenate(hn, axis=1)
        yraw = jnp.concatenate(ys, axis=1) + s["ddx"] * xs
        yraw_ref[...] = yraw
        z = p_ref[:, S_Z:S_Z + D_SSD]
        y_ref[...] = _group_norm(yraw * (z * _sigmoid(z)), nw_ref[...]).astype(BF16)

    hv = pl.BlockSpec((1, DT_REAL), lambda i: (0, 0))
    return pl.pallas_call(
        body, name=name, grid=(nc,),
        in_specs=[pl.BlockSpec((CHUNK, W_SSD), lambda i: (i, 0)),
                  pl.BlockSpec((8, W_SSD), lambda i: (jnp.maximum(i * (CHUNK // 8) - 1, 0), 0)),
                  pl.BlockSpec((4, D_XBC), lambda i: (0, 0)), pl.BlockSpec((1, D_XBC), lambda i: (0, 0)),
                  hv, hv, hv, pl.BlockSpec((1, D_SSD), lambda i: (0, 0))],
        out_specs=[pl.BlockSpec((CHUNK, D_SSD), lambda i: (i, 0)), pl.BlockSpec((CHUNK, D_SSD), lambda i: (i, 0)),
                   pl.BlockSpec((1, SSD_STATE, D_SSD), lambda i: (i, 0, 0)),
                   pl.BlockSpec((CHUNK, D_XBC), lambda i: (i, 0))],
        out_shape=[jax.ShapeDtypeStruct((t, D_SSD), BF16), jax.ShapeDtypeStruct((t, D_SSD), F32),
                   jax.ShapeDtypeStruct((nc, SSD_STATE, D_SSD), F32), jax.ShapeDtypeStruct((t, D_XBC), F32)],
        scratch_shapes=[pltpu.VMEM((SSD_STATE, D_SSD), F32)],
        compiler_params=_cp("arbitrary"),
    )(p, p, cw, cb, dtb, alog, dvec, nw)


def _ssd_bwd(p, c, yraw, hs, dy, cw, dtb, alog, dvec, nw, name):
    t = p.shape[0]
    nc = t // CHUNK

    def body(p_ref, c_ref, yraw_ref, hs_ref, dy_ref, cw_ref, dtb_ref, alog_ref, dvec_ref, nw_ref,
             dp_ref, dcw_ref, dcb_ref, ddtb_ref, dalog_ref, dd_ref, dnw_ref, dh_scr, dnext_scr):
        i = pl.program_id(0)

        @pl.when(i == 0)
        def _():
            for r in (dcw_ref, dcb_ref, ddtb_ref, dalog_ref, dd_ref, dnw_ref, dh_scr, dnext_scr):
                r[...] = jnp.zeros_like(r)

        k = _ssd_consts()
        s = _ssd_pre(c_ref[...], p_ref, dtb_ref, alog_ref, dvec_ref, k)
        xs, bm, cm, cs, dt, a = s["xs"], s["bm"], s["cm"], s["cs"], s["dt"], s["a"]
        m0 = k["m0"]
        xdt = xs * s["dtx"]
        hprev = hs_ref[0]
        dh = dh_scr[...]

        nw_v = nw_ref[...]
        yraw = yraw_ref[...]
        z = p_ref[:, S_Z:S_Z + D_SSD]
        sz = _sigmoid(z)
        siluz = z * sz
        yz = yraw * siluz
        dyo = dy_ref[...]
        dyn = dyo * nw_v
        dyz_parts, dnw_parts = [], []
        for g in range(SSD_GROUPS):
            gs = slice(g * GROUP_W, (g + 1) * GROUP_W)
            yzg = yz[:, gs]
            r = lax.rsqrt(jnp.mean(yzg * yzg, axis=1, keepdims=True) + EPS)
            dnw_parts.append(jnp.sum(dyo[:, gs] * yzg * r, axis=0, keepdims=True))
            dyz_parts.append(r * dyn[:, gs] - yzg * (r * r * r) * jnp.mean(dyn[:, gs] * yzg, axis=1, keepdims=True))
        dnw_ref[...] += jnp.concatenate(dnw_parts, axis=1)
        dyz = jnp.concatenate(dyz_parts, axis=1)
        d_y = dyz * siluz
        dp_ref[:, S_Z:S_Z + D_SSD] = (dyz * yraw * (sz * (1.0 + z * (1.0 - sz)))).astype(BF16)
        dd_row = jnp.sum(d_y * xs, axis=0, keepdims=True)
        dxs = d_y * s["ddx"]

        lane_h = _iota((1, SSD_HEADS), 1)
        sub_h = _iota((SSD_HEADS, 1), 0)
        dcs = jnp.zeros((CHUNK, SSD_HEADS), F32)
        dcst2 = jnp.zeros((SSD_HEADS, 128), F32)
        dxdt_parts, db_parts, dc_parts, dhp_parts, yoff_parts, dend_parts, dcd_parts = [], [], [], [], [], [], []
        for g in range(SSD_GROUPS):
            gs = slice(g * GROUP_W, (g + 1) * GROUP_W)
            bg = bm[:, g * 128:(g + 1) * 128]
            cg = cm[:, g * 128:(g + 1) * 128]
            bdup = jnp.concatenate([bg, bg], axis=0)
            cbdup = _dot(cg, bdup, _NT)
            dcb2 = jnp.zeros((CHUNK, 128), F32)
            dxp_parts = []
            for q in range(4):
                pr = g * 4 + q
                h0, h1 = 2 * pr, 2 * pr + 1
                lp = _pair_decay(pr, cs, s["cst2"], k)
                mp = cbdup * lp
                xst = _pair_stack(xdt[:, pr * 128:(pr + 1) * 128], k)
                dyp = d_y[:, pr * 128:(pr + 1) * 128]
                dmp = _dot(dyp, xst, _NT)
                dxst = _dot(mp, dyp, _TN)
                dxp_parts.append(jnp.where(m0, dxst[:CHUNK], dxst[CHUNK:]))
                dcb2 = dcb2 + dmp * lp
                dlm = dmp * mp
                rs0 = jnp.sum(jnp.where(m0, dlm, 0.0), axis=1, keepdims=True)
                rs1 = jnp.sum(jnp.where(m0, 0.0, dlm), axis=1, keepdims=True)
                dcs = dcs + jnp.where(lane_h == h0, rs0, 0.0) + jnp.where(lane_h == h1, rs1, 0.0)
                colsum = jnp.sum(dlm, axis=0, keepdims=True)
                sel = ((sub_h == h0) & m0) | ((sub_h == h1) & jnp.logical_not(m0))
                dcst2 = dcst2 - jnp.where(sel, colsum, 0.0)
            dcg = _dot(dcb2, bdup)
            dbdup = _dot(dcb2, cg, _TN)
            dbg = dbdup[:CHUNK] + dbdup[CHUNK:]
            hp_g = hprev[:, gs]
            zoff = _dot(cg, hp_g)
            dzo = d_y[:, gs] * s["ecx"][:, gs]
            dcg = dcg + _dot(dzo, hp_g, _NT)
            dh_g = dh[:, gs]
            dhp_parts.append(_dot(cg, dzo, _TN) + dh_g * s["cdx"][:, gs])
            dcd_parts.append(jnp.sum(dh_g * hp_g, axis=0, keepdims=True))
            wg = xdt[:, gs] * s["dex"][:, gs]
            dbg = dbg + _dot(wg, dh_g, _NT)
            dwg = _dot(bg, dh_g)
            dxdt_parts.append(jnp.concatenate(dxp_parts, axis=1) + dwg * s["dex"][:, gs])
            dend_g = dwg * wg
            dend_parts.append(jnp.sum(dend_g, axis=0, keepdims=True))
            yoff_parts.append(dzo * zoff - dend_g)
            db_parts.append(dbg)
            dc_parts.append(dcg)
        dh_scr[...] = jnp.concatenate(dhp_parts, axis=1)
        dxdt = jnp.concatenate(dxdt_parts, axis=1)
        sums = _dot01(jnp.concatenate([jnp.concatenate(yoff_parts, axis=1), dxdt * xs], axis=0), k["e"], _NT)
        rows8 = jnp.concatenate([jnp.broadcast_to(jnp.concatenate(r, axis=1), (8, D_SSD))
                                 for r in (dcd_parts, [dd_row], dend_parts)], axis=0)
        small = _dot01(rows8, k["e"], _NT)
        dd_ref[...] += small[8:9]
        dcs_last = small[0:1] * s["cdec"] + small[16:17]
        hi, lo = _split(dcst2)
        dcs = (dcs + sums[0:CHUNK]
               + lax.dot_general(k["fold"], hi, _NT, preferred_element_type=F32)
               + lax.dot_general(k["fold"], lo, _NT, preferred_element_type=F32)
               + jnp.where(_iota((CHUNK, 1), 0) == CHUNK - 1, dcs_last, 0.0))
        dda = _dot_hi(k["tril"], dcs, _TN)
        ddt = dda * a + sums[CHUNK:2 * CHUNK]
        dalog_ref[...] += jnp.sum(dda * dt, axis=0, keepdims=True) * a
        dxs = dxs + dxdt * s["dtx"]
        draw = ddt * _sigmoid(s["dtp"])
        ddtb_ref[...] += jnp.sum(draw, axis=0, keepdims=True)
        dp_ref[:, S_DT:] = jnp.zeros((CHUNK, W_SSD - S_DT), BF16)
        dp_ref[:, S_DT:S_DT + DT_REAL] = draw.astype(BF16)
        dxbc = jnp.concatenate([dxs] + db_parts + dc_parts, axis=1)
        sg, c = s["sg"], s["c"]
        dc = dxbc * (sg * (1.0 + c * (1.0 - sg)))
        dx, dcw, dcb = _conv_bwd(dc, dnext_scr[...], p_ref[:, S_XBC:S_DT], cw_ref[...], CHUNK)
        dp_ref[:, S_XBC:S_DT] = dx.astype(BF16)
        dcw_ref[...] += dcw
        dcb_ref[...] += dcb
        dnext_scr[...] = dc[0:8]

    rev = lambda i: nc - 1 - i
    hv = pl.BlockSpec((1, DT_REAL), lambda i: (0, 0))
    cws = pl.BlockSpec((4, D_XBC), lambda i: (0, 0))
    cbs = pl.BlockSpec((1, D_XBC), lambda i: (0, 0))
    nws = pl.BlockSpec((1, D_SSD), lambda i: (0, 0))
    wide = pl.BlockSpec((CHUNK, D_SSD), lambda i: (rev(i), 0))
    hshape = jax.ShapeDtypeStruct((1, DT_REAL), F32)
    return pl.pallas_call(
        body, name=name, grid=(nc,),
        in_specs=[pl.BlockSpec((CHUNK, W_SSD), lambda i: (rev(i), 0)),
                  pl.BlockSpec((CHUNK, D_XBC), lambda i: (rev(i), 0)),
                  wide, pl.BlockSpec((1, SSD_STATE, D_SSD), lambda i: (rev(i), 0, 0)), wide,
                  cws, hv, hv, hv, nws],
        out_specs=[pl.BlockSpec((CHUNK, W_SSD), lambda i: (rev(i), 0)), cws, cbs, hv, hv, hv, nws],
        out_shape=[jax.ShapeDtypeStruct((t, W_SSD), BF16), jax.ShapeDtypeStruct((4, D_XBC), F32),
                   jax.ShapeDtypeStruct((1, D_XBC), F32), hshape, hshape, hshape,
                   jax.ShapeDtypeStruct((1, D_SSD), F32)],
        scratch_shapes=[pltpu.VMEM((SSD_STATE, D_SSD), F32), pltpu.VMEM((8, D_XBC), F32)],
        compiler_params=_cp("arbitrary"),
    )(p, c, yraw, hs, dy, cw, dtb, alog, dvec, nw)


def _loss_head(y, target, name, tb=512):
    t = y.shape[0]
    tb = min(tb, t)

    def body(y_ref, t_ref, dy_ref, l_ref):
        @pl.when(pl.program_id(0) == 0)
        def _():
            l_ref[...] = jnp.zeros_like(l_ref)

        e = y_ref[...] - t_ref[...]
        dy_ref[...] = e * (1.0 / D)
        l_ref[...] += jnp.sum(jnp.sum(e * e, axis=1, keepdims=True), axis=0, keepdims=True) * (0.5 / D)

    row = pl.BlockSpec((tb, D), lambda i: (i, 0))
    return pl.pallas_call(
        body, name=name, grid=(t // tb,), in_specs=[row, row],
        out_specs=[row, pl.BlockSpec((8, 128), lambda i: (0, 0))],
        out_shape=[jax.ShapeDtypeStruct((t, D), F32), jax.ShapeDtypeStruct((8, 128), F32)],
        compiler_params=_cp("arbitrary"),
    )(y, target)


def _adamw(slots, w, m, v, name, tb):
    nl = len(slots)
    ns, r, c = slots[0].shape
    assert r % tb == 0 and w.shape == (nl, r, c), (r, tb, w.shape)

    def body(*refs):
        s_refs = refs[:nl]
        w_ref, m_ref, v_ref, g_ref, d_ref, m2_ref, v2_ref = refs[nl:]

        def total(ref):
            acc = ref[0].astype(F32)
            for j in range(1, ns):
                acc = acc + ref[j].astype(F32)
            return acc

        g = total(s_refs[0])
        for layer in range(1, nl):
            g = jnp.where(pl.program_id(0) == layer, total(s_refs[layer]), g)
        m2 = ADAM_B1 * m_ref[...] + (1.0 - ADAM_B1) * g
        v2 = ADAM_B2 * v_ref[...] + (1.0 - ADAM_B2) * (g * g)
        m_hat = m2 / (1.0 - ADAM_B1 ** ADAM_STEP)
        v_hat = v2 / (1.0 - ADAM_B2 ** ADAM_STEP)
        g_ref[...] = g
        d_ref[...] = -ADAM_LR * (m_hat / (jnp.sqrt(v_hat) + ADAM_EPS) + ADAM_WD * w_ref[...])
        m2_ref[...] = m2
        v2_ref[...] = v2

    def slot_spec(layer):
        return pl.BlockSpec((ns, tb, c), lambda l, i: (0, jnp.where(l == layer, i, 0), 0))

    row = pl.BlockSpec((None, tb, c), lambda l, i: (l, i, 0))
    shp = jax.ShapeDtypeStruct((nl, r, c), F32)
    return pl.pallas_call(
        body, name=name, grid=(nl, r // tb),
        in_specs=[slot_spec(layer) for layer in range(nl)] + [row, row, row],
        out_specs=[row, row, row, row], out_shape=[shp, shp, shp, shp], compiler_params=_cp("arbitrary", "arbitrary"),
    )(*slots, w, m, v)


def _pair_sum(own, got, name, out_dtype, tb):
    nj, _, r, c = own.shape
    mc = lax.axis_index("c")

    def body(mc_ref, a_ref, b_ref, o_ref):
        del mc_ref
        o_ref[...] = (a_ref[...] + b_ref[...]).astype(out_dtype)

    return pl.pallas_call(
        body, name=name,
        grid_spec=pltpu.PrefetchScalarGridSpec(
            num_scalar_prefetch=1, grid=(nj, r // tb),
            in_specs=[pl.BlockSpec((None, None, tb, c), lambda j, i, mc_ref: (j, mc_ref[0], i, 0)),
                      pl.BlockSpec((None, tb, c), lambda j, i, mc_ref: (j, i, 0))],
            out_specs=pl.BlockSpec((None, tb, c), lambda j, i, mc_ref: (j, i, 0))),
        out_shape=jax.ShapeDtypeStruct((nj, r, c), out_dtype), compiler_params=_cp("parallel", "parallel"),
    )(jnp.reshape(mc, (1,)).astype(jnp.int32), own, got)


def _slot_sum(slots, name):
    ns, r, c = slots.shape

    def body(s_ref, o_ref):
        g = s_ref[0]
        for j in range(1, ns):
            g = g + s_ref[j]
        o_ref[...] = g

    return pl.pallas_call(body, name=name, out_shape=jax.ShapeDtypeStruct((r, c), F32))(slots)


def _position():
    return lax.axis_index("x"), lax.axis_index("y"), lax.axis_index("c")


def _comm(exchange, peers, xs, out_shapes, sems, name, collective_id):
    n = len(xs)
    if collective_id is None:
        def body(*refs):
            exchange(refs[:n], refs[n:n + len(out_shapes)], *refs[n + len(out_shapes):])

        return pl.pallas_call(body, name=name, in_specs=[ANY] * n, out_specs=[ANY] * len(out_shapes),
                              out_shape=out_shapes, scratch_shapes=sems)(*xs)
    x_refs = [jax.new_ref(x, memory_space=pltpu.MemorySpace.HBM) for x in xs]
    out_refs = [jax.empty_ref(s, memory_space=pltpu.MemorySpace.HBM) for s in out_shapes]

    @pl.kernel(mesh=plsc.ScalarSubcoreMesh(axis_name="seq", num_cores=1), name=name, scratch_types=sems,
               compiler_params=pltpu.CompilerParams(collective_id=collective_id))
    def launch(*sem_refs):
        barrier = pltpu.get_barrier_semaphore()
        to = peers(*_position())
        for peer in to:
            pl.semaphore_signal(barrier, inc=1, device_id=peer, device_id_type=MESH)
        pl.semaphore_wait(barrier, len(to))
        exchange(x_refs, out_refs, *sem_refs)

    launch()
    return [r[...] for r in out_refs]


def _all_gather(xs, name, collective_id=None):
    n = len(xs)
    return _comm(_gather_body, lambda x, y, c: [(x, y, 1 - c), (1 - x, y, c), (x, 1 - y, c), (1 - x, 1 - y, c)], xs,
                 [jax.ShapeDtypeStruct((N_DEV,) + x.shape, x.dtype) for x in xs],
                 [pltpu.SemaphoreType.DMA((n, 7)), pltpu.SemaphoreType.DMA((n, 7)), pltpu.SemaphoreType.DMA((n,))],
                 name, collective_id)


def _gather_body(x_refs, out_refs, send_sems, recv_sems, local_sems):
    n = len(x_refs)
    mx, my, mc = _position()
    me, sibling = (mx, my, mc), (mx, my, 1 - mc)
    chips = [(1 - mx, my), (mx, 1 - my), (1 - mx, 1 - my)]

    def copy(a, k, block, to, own=False):
        dst = out_refs[a].at[4 * block[0] + 2 * block[1] + block[2]]
        return pltpu.make_async_remote_copy(
            src_ref=x_refs[a] if own else dst, dst_ref=dst,
            send_sem=send_sems.at[a, k], recv_sem=recv_sems.at[a, k], device_id=to, device_id_type=MESH)

    mine = [pltpu.make_async_copy(x_refs[a], out_refs[a].at[4 * mx + 2 * my + mc], local_sems.at[a]) for a in range(n)]
    first = [copy(a, 1 + j, me, (*chip, mc), own=True) for j, chip in enumerate(chips) for a in range(n)]
    first += [copy(a, 0, me, sibling, own=True) for a in range(n)]
    for cp in first + mine:
        cp.start()
    passed = []
    for j, chip in enumerate(chips):
        for a in range(n):
            copy(a, 1 + j, (*chip, mc), me).wait_recv()
            passed.append(copy(a, 4 + j, (*chip, mc), sibling))
            passed[-1].start()
    for a in range(n):
        copy(a, 0, sibling, me).wait_recv()
    for j, chip in enumerate(chips):
        for a in range(n):
            copy(a, 4 + j, (*chip, 1 - mc), me).wait_recv()
    for cp in first + passed:
        cp.wait_send()
    for cp in mine:
        cp.wait()


def _exchange_sibling(gs, name, collective_id=None):
    n = len(gs)

    def exchange(g_refs, r_refs, send_sems, recv_sems):
        mx, my, mc = _position()
        cps = [pltpu.make_async_remote_copy(src_ref=g_refs[a].at[:, 1 - mc], dst_ref=r_refs[a],
                                            send_sem=send_sems.at[a], recv_sem=recv_sems.at[a],
                                            device_id=(mx, my, 1 - mc), device_id_type=MESH) for a in range(n)]
        for cp in cps:
            cp.start()
        for cp in cps:
            cp.wait()

    return _comm(exchange, lambda x, y, c: [(x, y, 1 - c)], gs,
                 [jax.ShapeDtypeStruct(g.shape[:1] + g.shape[2:], g.dtype) for g in gs],
                 [pltpu.SemaphoreType.DMA((n,)), pltpu.SemaphoreType.DMA((n,))], name, collective_id)


def _exchange_chips(ss, name, collective_id=None):
    n = len(ss)

    def exchange(s_refs, r_refs, send_sems, recv_sems, local_sems):
        mx, my, mc = _position()
        my_chip = 2 * mx + my
        chips = [(1 - mx, my), (mx, 1 - my), (1 - mx, 1 - my)]

        def copy(a, k, to_slot):
            px, py = chips[k]
            return pltpu.make_async_remote_copy(
                src_ref=s_refs[a].at[2 * px + py], dst_ref=r_refs[a].at[to_slot], send_sem=send_sems.at[a, k],
                recv_sem=recv_sems.at[a, k], device_id=(px, py, mc), device_id_type=MESH)

        sends = [copy(a, k, my_chip) for k in range(3) for a in range(n)]
        local = [pltpu.make_async_copy(s_refs[a].at[my_chip], r_refs[a].at[my_chip], local_sems.at[a])
                 for a in range(n)]
        for cp in sends + local:
            cp.start()
        for k in range(3):
            px, py = chips[k]
            for a in range(n):
                copy(a, k, 2 * px + py).wait_recv()
        for cp in sends:
            cp.wait_send()
        for cp in local:
            cp.wait()

    return _comm(exchange, lambda x, y, c: [(1 - x, y, c), (x, 1 - y, c), (1 - x, 1 - y, c)], ss,
                 [jax.ShapeDtypeStruct(s.shape, s.dtype) for s in ss],
                 [pltpu.SemaphoreType.DMA((n, 3)), pltpu.SemaphoreType.DMA((n, 3)), pltpu.SemaphoreType.DMA((n,))],
                 name, collective_id)


def _cols_concat(g, name, tb=128):
    _, k_dim, n = g.shape

    def body(g_ref, o_ref):
        o_ref[...] = jnp.concatenate([g_ref[d] for d in range(N_DEV)], axis=1)

    return pl.pallas_call(
        body, name=name, grid=(k_dim // tb,),
        in_specs=[pl.BlockSpec((N_DEV, tb, n), lambda i: (0, i, 0))],
        out_specs=pl.BlockSpec((tb, N_DEV * n), lambda i: (i, 0)),
        out_shape=jax.ShapeDtypeStruct((k_dim, N_DEV * n), g.dtype), compiler_params=_cp("parallel"),
    )(g)


def _cols_split(parts, name, tb=128):
    k_dim = parts[0].shape[0]
    n = sum(p.shape[1] for p in parts) // N_DEV

    def body(*refs):
        full = jnp.concatenate([r[...] for r in refs[:-1]], axis=1)
        for d in range(N_DEV):
            refs[-1][d] = full[:, d * n:(d + 1) * n]

    return pl.pallas_call(
        body, name=name, grid=(k_dim // tb,),
        in_specs=[pl.BlockSpec((tb, p.shape[1]), lambda i: (i, 0)) for p in parts],
        out_specs=pl.BlockSpec((N_DEV, tb, n), lambda i: (0, i, 0)),
        out_shape=jax.ShapeDtypeStruct((N_DEV, k_dim, n), parts[0].dtype), compiler_params=_cp("parallel"),
    )(*parts)


_Q0, _GL0 = 7200, 8224
N_SHARD_IN = N_IN // N_DEV


def _w_in_regions(g, name, tb=128):
    def body(g_ref, ssd_ref, lru_ref, q_ref, gl_ref):
        full = jnp.concatenate([g_ref[d] for d in range(N_DEV)], axis=1)
        lru_ref[...] = full[:, 0:2 * D]
        ssd_ref[:, :S_DT] = full[:, 2 * D:2 * D + S_DT]
        ssd_ref[:, S_DT:] = jnp.zeros((tb, W_SSD - S_DT), g.dtype)
        ssd_ref[:, S_DT:S_DT + DT_REAL] = full[:, 2 * D + S_DT:_Q0]
        q_ref[...] = full[:, _Q0:_GL0]
        gl_ref[...] = full[:, _GL0:N_IN]

    widths = (W_SSD, 2 * D, D, 3 * D)
    return pl.pallas_call(
        body, name=name, grid=(D // tb,),
        in_specs=[pl.BlockSpec((N_DEV, tb, N_SHARD_IN), lambda i: (0, i, 0))],
        out_specs=[pl.BlockSpec((tb, wd), lambda i: (i, 0)) for wd in widths],
        out_shape=[jax.ShapeDtypeStruct((D, wd), g.dtype) for wd in widths], compiler_params=_cp("parallel"),
    )(g)


def _w_in_shards(dssd, dlru, dq, dgl, name, tb=128):
    def body(ssd_ref, lru_ref, q_ref, gl_ref, o_ref):
        full = jnp.concatenate([lru_ref[...], ssd_ref[:, :S_DT + DT_REAL], q_ref[...], gl_ref[...]], axis=1)
        for d in range(N_DEV):
            o_ref[d] = full[:, d * N_SHARD_IN:(d + 1) * N_SHARD_IN]

    return pl.pallas_call(
        body, name=name, grid=(D // tb,),
        in_specs=[pl.BlockSpec((tb, a.shape[1]), lambda i: (i, 0)) for a in (dssd, dlru, dq, dgl)],
        out_specs=pl.BlockSpec((N_DEV, tb, N_SHARD_IN), lambda i: (0, i, 0)),
        out_shape=jax.ShapeDtypeStruct((N_DEV, D, N_SHARD_IN), F32), compiler_params=_cp("parallel"),
    )(dssd, dlru, dq, dgl)


_BIG = (("w_in", "col", (1024, 1412)), ("mem_w_kv", "col", (1024, 256)), ("w_br_lru", "row", (128, 1024)),
        ("w_br_ssd", "row", (256, 1024)), ("w_br_xa", "row", (128, 1024)), ("w_out", "row", (128, 1024)),
        ("ffn_w_in", "col", (1024, 704)), ("ffn_w_down", "row", (352, 1024)))
_SMALL = (("b_gate", (3, 128)), ("lru_conv_w", (4, 128)), ("ssd_conv_w", (4, 384)))
_REP = (("lru_conv_b", (1024,)), ("lru_w_a", (8, 128, 128)), ("lru_b_a", (1024,)), ("lru_w_i", (8, 128, 128)),
        ("lru_b_i", (1024,)), ("lru_lambda", (1024,)), ("ssd_conv_b", (3072,)), ("ssd_dt_bias", (32,)),
        ("ssd_a_log", (32,)), ("ssd_d", (32,)), ("ssd_norm_w", (2048,)), ("ln1_g", (1024,)), ("ln1_b", (1024,)),
        ("ln2_g", (1024,)), ("ln2_b", (1024,)))
_ORDER = ("w_in", "b_gate", "lru_conv_w", "lru_conv_b", "lru_w_a", "lru_b_a", "lru_w_i", "lru_b_i", "lru_lambda",
          "ssd_conv_w", "ssd_conv_b", "ssd_dt_bias", "ssd_a_log", "ssd_d", "ssd_norm_w", "mem_w_kv", "w_br_lru",
          "w_br_ssd", "w_br_xa", "w_out", "ln1_g", "ln1_b", "ffn_w_in", "ffn_w_down", "ln2_g", "ln2_b")

LANES = 1024
N_SMALL = sum(DEPTH * s[0] * s[1] for _, s in _SMALL)
R_SMALL = 8
N_REP = sum(DEPTH * math.prod(s) for _, s in _REP)
R_REP = 68
R_SM = R_SMALL + R_REP + 4
R_TAIL = R_SMALL + N_DEV * R_REP
TB_TAIL = 184
assert N_SMALL <= R_SMALL * LANES and N_REP <= N_DEV * R_REP * LANES


def _rows(flat, rows):
    return jnp.pad(flat, (0, rows * LANES - flat.shape[0])).reshape(rows, LANES)


def _rowblk(a, cap):
    return max(b for b in range(16, cap + 1, 16) if a % b == 0)


def _pack_tail(d):
    small = jnp.concatenate([d[n].reshape(-1) for n, _ in _SMALL])
    rep = jnp.concatenate([d[n].reshape(-1) for n, _ in _REP])
    return jnp.concatenate([_rows(small, R_SMALL), _rows(rep, N_DEV * R_REP)], axis=0)


def _unpack_tail(a):
    out, o = {}, 0
    flat = a[:R_SMALL].reshape(-1)
    for n, s in _SMALL:
        k = DEPTH * math.prod(s)
        out[n] = flat[o:o + k].reshape((DEPTH,) + s)
        o += k
    flat, o = a[R_SMALL:].reshape(-1), 0
    for n, s in _REP:
        k = DEPTH * math.prod(s)
        out[n] = flat[o:o + k].reshape((DEPTH,) + s)
        o += k
    return out


def _by_dest(g):
    g = g.reshape(g.shape[:-1] + (N_DEV, g.shape[-1] // N_DEV))
    return jnp.moveaxis(g, -2, 0).reshape(N_DEV, -1)


def _from_stack(st):
    st = jnp.moveaxis(st, 0, -2)
    return st.reshape(st.shape[:-2] + (st.shape[-2] * st.shape[-1],))


def _layer_fwd(x, mem, w, l):
    nm = lambda s: f"{s}_l{l}"
    wi = w["wi"]
    row = lambda v: v.reshape(1, -1)
    s = dict(x=x, wi=wi)
    s["p_ssd"] = _mm(x, wi["ssd"], name=nm("proj_ssd"))
    s["p_lru"] = _mm(x, wi["lru"], name=nm("proj_lru"))
    s["p_q"] = _mm(x, wi["q"], name=nm("proj_q"))
    s["p_gl"] = _mm(x, wi["gl"], name=nm("proj_gl"))
    s["lru_par"] = (w["lru_conv_w"], row(w["lru_conv_b"]), w["lru_w_a"], row(w["lru_b_a"]), w["lru_w_i"],
                    row(w["lru_b_i"]), row(w["lru_lambda"]))
    s["y_lru"], s["h"], s["xc"] = _lru_fwd(s["p_lru"], *s["lru_par"], name=nm("lru_fwd"))
    s["ssd_par"] = (w["ssd_conv_w"], row(w["ssd_conv_b"]), row(w["ssd_dt_bias"]), row(w["ssd_a_log"]),
                    row(w["ssd_d"]), row(w["ssd_norm_w"]))
    s["y_ssd"], s["yraw"], s["hs"], s["c_ssd"] = _ssd_fwd(s["p_ssd"], *s["ssd_par"], name=nm("ssd_fwd"))
    s["kv"] = _mm(mem, w["mem_w_kv"], name=nm("kv"))
    s["y_xa"] = _xa_fwd(s["p_q"], s["kv"], name=nm("xa_fwd"))
    s["b1"] = _mm(s["y_lru"], w["w_br_lru"], name=nm("br_lru"))
    s["b2"] = _mm(s["y_ssd"], w["w_br_ssd"], name=nm("br_ssd"))
    s["b3"] = _mm(s["y_xa"], w["w_br_xa"], name=nm("br_xa"))
    s["bg"] = row(w["b_gate"])
    s["merged"] = _merge_fwd(s["p_gl"], s["bg"], s["b1"], s["b2"], s["b3"], name=nm("merge_fwd"))
    s["mix"] = _mm(s["merged"], w["w_out"], name=nm("out_proj"))
    s["x1"] = _ln_fwd(x, s["mix"], row(w["ln1_g"]), row(w["ln1_b"]), name=nm("ln1_fwd"))
    s["gate"], s["up"], s["act"] = _ffn_in_swiglu(s["x1"], w["ffn_w_in"], name=nm("ffn_in"))
    s["f"] = _mm(s["act"], w["ffn_w_down"], name=nm("ffn_down"))
    s["x2"] = _ln_fwd(s["x1"], s["f"], row(w["ln2_g"]), row(w["ln2_b"]), name=nm("ln2_fwd"))
    return s


def _layer_bwd(s, mem, w, dxo, l, hooks=None):
    nm = lambda t: f"{t}_l{l}"
    g = {}
    hook = lambda stage, t: hooks[stage](t, g) if hooks and stage in hooks else t
    row = lambda v: v.reshape(1, -1)
    slabs = lambda a: a.reshape(N_DEV, a.shape[0] // N_DEV, a.shape[1])
    du2, dg, db = _ln_bwd(s["x1"], s["f"], dxo, row(w["ln2_g"]), name=nm("ln2_bwd"))
    g["ln2_g"], g["ln2_b"] = dg[0], db[0]
    dgate, dup = _d_swiglu(du2, w["ffn_w_down"], s["gate"], s["up"], name=nm("d_swiglu"))
    g["ffn_w_down"] = slabs(_mm(s["act"], du2, ta=True, name=nm("dw_ffn_down")))
    dx1 = _mm(dgate, w["ffn_w_in"][:, :D_FF], tb=True, add=du2, add_scale=ALPHA, name=nm("d_x1_gate"))
    dx1 = _mm(dup, w["ffn_w_in"][:, D_FF:], tb=True, add=dx1, name=nm("d_x1_up"))
    g["ffn_w_in"] = _cols_split([_mm(s["x1"], dgate, ta=True, name=nm("dw_ffn_gate")),
                                 _mm(s["x1"], dup, ta=True, name=nm("dw_ffn_up"))], name=nm("dw_ffn_in_shards"))
    du1, dg, db = _ln_bwd(s["x"], s["mix"], dx1, row(w["ln1_g"]), name=nm("ln1_bwd"))
    g["ln1_g"], g["ln1_b"] = dg[0], db[0]
    dmerged = hook("mid", _mm(du1, w["w_out"], tb=True, name=nm("d_merged")))
    g["w_out"] = slabs(_mm(s["merged"], du1, ta=True, name=nm("dw_out")))
    dp_gl, d1, d2, d3, dbg = _merge_bwd(s["p_gl"], s["bg"], s["b1"], s["b2"], s["b3"], dmerged, name=nm("merge_bwd"))
    g["b_gate"] = dbg.reshape(3, D)
    dy_lru = _mm(d1, w["w_br_lru"], tb=True, name=nm("d_y_lru"))
    g["w_br_lru"] = slabs(_mm(s["y_lru"], d1, ta=True, name=nm("dw_br_lru")))
    dy_ssd = _mm(d2, w["w_br_ssd"], tb=True, name=nm("d_y_ssd"))
    g["w_br_ssd"] = slabs(_mm(s["y_ssd"], d2, ta=True, name=nm("dw_br_ssd")))
    dy_xa = _mm(d3, w["w_br_xa"], tb=True, name=nm("d_y_xa"))
    g["w_br_xa"] = slabs(_mm(s["y_xa"], d3, ta=True, name=nm("dw_br_xa")))
    dp_q, dkv = _xa_bwd(s["p_q"], s["kv"], dy_xa, name=nm("xa_bwd"))
    g["mem_w_kv"] = _mm(mem, dkv, ta=True, split_n=2 * D // N_DEV, name=nm("dw_kv"))
    dy_ssd = hook("branches", dy_ssd)
    ssd_cw, _, *ssd_rest = s["ssd_par"]
    dp_ssd, dcw, dcb, ddtb, dalog, dd, dnw = _ssd_bwd(s["p_ssd"], s["c_ssd"], s["yraw"], s["hs"], dy_ssd, ssd_cw,
                                                      *ssd_rest, name=nm("ssd_bwd"))
    g["ssd_conv_w"], g["ssd_conv_b"], g["ssd_dt_bias"] = dcw, dcb[0], ddtb[0]
    g["ssd_a_log"], g["ssd_d"], g["ssd_norm_w"] = dalog[0], dd[0], dnw[0]
    dp_ssd = hook("ssd", dp_ssd)
    lru_cw, _, *lru_rest = s["lru_par"]
    dp_lru, dcw, dcb, dwa, dba, dwi, dbi, dlam = _lru_bwd(s["p_lru"], s["xc"], s["h"], dy_lru, lru_cw, *lru_rest,
                                                          name=nm("lru_bwd"))
    g["lru_conv_w"], g["lru_conv_b"], g["lru_w_a"], g["lru_b_a"] = dcw, dcb[0], dwa, dba[0]
    g["lru_w_i"], g["lru_b_i"], g["lru_lambda"] = dwi, dbi[0], dlam[0]
    wi, x = s["wi"], s["x"]
    g["w_in"] = _w_in_shards(_mm(x, dp_ssd, ta=True, name=nm("dw_in_ssd")), _mm(x, dp_lru, ta=True, name=nm("dw_in_lru")),
                             _mm(x, dp_q, ta=True, name=nm("dw_in_q")), _mm(x, dp_gl, ta=True, name=nm("dw_in_gl")),
                             name=nm("dw_in_shards"))
    dp_ssd = hook("weights", dp_ssd)
    dx = _mm(dp_ssd, wi["ssd"], tb=True, add=du1, add_scale=ALPHA, name=nm("dx_ssd"))
    dx = hook("dx", _mm(dp_lru, wi["lru"], tb=True, add=dx, name=nm("dx_lru")))
    dx = _mm(dp_q, wi["q"], tb=True, add=dx, name=nm("dx_q"))
    dx = _mm(dp_gl, wi["gl"], tb=True, add=dx, name=nm("dx_gl"))
    return dx, g


def _local_step(x, mem, target, layers, hooks=None):
    saved = []
    for l in range(DEPTH):
        saved.append(_layer_fwd(x, mem, layers[l], l))
        x = saved[-1]["x2"]
    dx, loss = _loss_head(x, target, name="loss_head")
    grads = [None] * DEPTH
    for l in reversed(range(DEPTH)):
        dx, grads[l] = _layer_bwd(saved[l], mem, layers[l], dx, l, hooks[l] if hooks else None)
    return loss, dx, grads


def kernel(x, mem, w_in, b_gate, lru_conv_w, lru_conv_b, lru_w_a, lru_b_a, lru_w_i, lru_b_i, lru_lambda, ssd_conv_w, ssd_conv_b, ssd_dt_bias, ssd_a_log, ssd_d, ssd_norm_w, mem_w_kv, w_br_lru, w_br_ssd, w_br_xa, w_out, ln1_g, ln1_b, ffn_w_in, ffn_w_down, ln2_g, ln2_b, loss_target, m_w_in, m_b_gate, m_lru_conv_w, m_lru_conv_b, m_lru_w_a, m_lru_b_a, m_lru_w_i, m_lru_b_i, m_lru_lambda, m_ssd_conv_w, m_ssd_conv_b, m_ssd_dt_bias, m_ssd_a_log, m_ssd_d, m_ssd_norm_w, m_mem_w_kv, m_w_br_lru, m_w_br_ssd, m_w_br_xa, m_w_out, m_ln1_g, m_ln1_b, m_ffn_w_in, m_ffn_w_down, m_ln2_g, m_ln2_b, v_w_in, v_b_gate, v_lru_conv_w, v_lru_conv_b, v_lru_w_a, v_lru_b_a, v_lru_w_i, v_lru_b_i, v_lru_lambda, v_ssd_conv_w, v_ssd_conv_b, v_ssd_dt_bias, v_ssd_a_log, v_ssd_d, v_ssd_norm_w, v_mem_w_kv, v_w_br_lru, v_w_br_ssd, v_w_br_xa, v_w_out, v_ln1_g, v_ln1_b, v_ffn_w_in, v_ffn_w_down, v_ln2_g, v_ln2_b):
    local = dict(locals())
    w = {n: local[n] for n in _ORDER}
    m = {n: local["m_" + n] for n in _ORDER}
    v = {n: local["v_" + n] for n in _ORDER}

    big = [n for n, _, _ in _BIG]
    kinds = {n: kind for n, kind, _ in _BIG}

    small = _rows(jnp.concatenate([w[n].reshape(-1) for n, _ in _SMALL]), R_SMALL)
    first = _all_gather([w["w_in"][0].astype(BF16), small], name="gather_w_in_l0")
    rest, later, _ = lax.optimization_barrier(([w[n][0].astype(BF16) for n in big[1:]],
                                               [w[n][1].astype(BF16) for n in big], first[-1]))
    rest = _all_gather(rest, "gather_weights_l0", collective_id=1)
    later = _all_gather(later, "gather_weights_l1", collective_id=4)
    stacks = [dict(zip(big, [first[0], *rest])), dict(zip(big, later))]
    small_all, o, small_full = first[-1].reshape(N_DEV, R_SMALL * LANES), 0, {}
    for n, s in _SMALL:
        k = DEPTH * s[0] * s[1]
        small_full[n] = _from_stack(small_all[:, o:o + k].reshape((N_DEV, DEPTH) + s))
        o += k
    layers = []
    for l in range(DEPTH):
        lw = {n: w[n][l] for n, _ in _REP}
        lw.update({n: small_full[n][l] for n, _ in _SMALL})
        lw["wi"] = dict(zip(("ssd", "lru", "q", "gl"), _w_in_regions(stacks[l]["w_in"], name=f"w_in_regions_l{l}")))
        for n in big[1:]:
            if kinds[n] == "col":
                lw[n] = _cols_concat(stacks[l][n], name=f"full_{n}_l{l}")
            else:
                lw[n] = stacks[l][n].reshape(-1, stacks[l][n].shape[-1])
        layers.append(lw)

    by_dest = lambda a: a.reshape((4, 2) + a.shape[1:])
    slots, pending, last_layer = {}, {}, {}

    def start(tag, collective_id, names_and_grads):
        names, owns = zip(*names_and_grads)
        pending[tag] = (names, owns, _exchange_sibling(list(owns), name=f"reduce_cores_{tag}", collective_id=collective_id))

    def finish(tag, collective_id, t):
        names, owns, gots = pending.pop(tag)
        t, gots = lax.optimization_barrier((t, gots))
        sums = [_pair_sum(own, got, name=f"pair_sum_{tag}_{n}", out_dtype=F32 if n == "tail" else BF16,
                          tb=R_SM if n == "tail" else _rowblk(own.shape[2], 256))
                for n, own, got in zip(names, owns, gots)]
        t, sums = lax.optimization_barrier((t, sums))
        got = _exchange_chips(sums, name=f"reduce_chips_{tag}", collective_id=collective_id)
        slots.update({(tag, n): s for n, s in zip(names, got)})
        return t

    def tail_of(g0):
        stacked = {n: jnp.stack([g0[n], last_layer[n]]) for n in [s[0] for s in _SMALL + _REP]}
        sm = jnp.concatenate([_by_dest(stacked[n]) for n, _ in _SMALL], axis=1)
        sm = jnp.pad(sm, ((0, 0), (0, R_SMALL * LANES - sm.shape[1])))
        rep = jnp.concatenate([stacked[n].reshape(-1) for n, _ in _REP])
        rep = jnp.pad(rep, (0, N_DEV * R_REP * LANES - rep.shape[0])).reshape(N_DEV, R_REP * LANES)
        tail = jnp.concatenate([sm, rep, jnp.zeros((N_DEV, (R_SM - R_SMALL - R_REP) * LANES), F32)], axis=1)
        return tail.reshape(4, 2, R_SM, LANES)

    def weights_l1(t, g):
        last_layer.update(g)
        start("l1", 2, [(n, by_dest(g[n])) for n in big])
        return t

    def branches_l0(t, g):
        start("l0a", 5, [(n, by_dest(g[n])) for n in big[1:]])
        return t

    def weights_l0(t, g):
        start("l0b", 7, [("w_in", by_dest(g["w_in"])), ("tail", tail_of(g))])
        return t

    hooks = [{"branches": branches_l0, "ssd": lambda t, g: finish("l0a", 6, t), "weights": weights_l0,
              "dx": lambda t, g: finish("l0b", 8, t)},
             {"weights": weights_l1, "dx": lambda t, g: finish("l1", 3, t)}]
    loss_tile, dx, grads = _local_step(x[0], mem[0], loss_target[0], layers, hooks)
    loss = lax.psum(loss_tile[0, 0], ("x", "y", "c"))

    res = {}
    for n in big:
        tb = _rowblk(w[n].shape[1], 128 if w[n].shape[2] > LANES else 256)
        res[n] = _adamw([slots["l0b" if n == "w_in" else "l0a", n], slots["l1", n]], w[n], m[n], v[n],
                        name=f"adamw_{n}", tb=tb)
    tail_sum = _slot_sum(slots["l0b", "tail"], name="sum_tail")
    rep_all = _all_gather([tail_sum[R_SMALL:R_SMALL + R_REP]], name="gather_replicated")[0]
    g_tail = jnp.concatenate([tail_sum[:R_SMALL], rep_all.reshape(N_DEV * R_REP, LANES)], axis=0)
    tails = _adamw([g_tail[None]], _pack_tail(w)[None], _pack_tail(m)[None], _pack_tail(v)[None],
                   name="adamw_tail", tb=TB_TAIL)

    outs = []
    for kind in range(4):
        d = {**{n: res[n][kind] for n in big}, **_unpack_tail(tails[kind][0])}
        outs += [d[n] for n in _ORDER]
    return (loss, dx[None], *outs)
```

```python
import math

import jax
import jax.numpy as jnp
from jax import lax
from jax.experimental import pallas as pl
from jax.experimental.pallas import tpu as pltpu
from jax.experimental.pallas import tpu_sc as plsc

F32 = jnp.float32
BF16 = jnp.bfloat16

D = 1024
DEPTH = 2
N_DEV = 8
CHUNK = 64
LRU_BLOCKS = 8
LRU_BLOCK = 128
LRU_C = 8.0
D_SSD = 2 * D
SSD_HEADS = 32
SSD_GROUPS = 4
GROUP_W = D_SSD // SSD_GROUPS
SSD_STATE = 128
D_XBC = D_SSD + 2 * SSD_GROUPS * SSD_STATE
XA_HEADS = 4
XA_HEAD_DIM = 256
D_FF = 2816
ALPHA = (2 * DEPTH) ** 0.25
EPS = 1e-5
N_IN = 11296

S_Z, S_XBC, S_DT, W_SSD = 0, 2048, 5120, 5632
DT_REAL = 32

ADAM_LR, ADAM_B1, ADAM_B2, ADAM_EPS, ADAM_WD, ADAM_STEP = 0.001, 0.9, 0.999, 1e-08, 0.01, 10

VMEM_LIMIT = 56 * 1024 * 1024
MESH = pl.DeviceIdType.MESH
ANY = pl.BlockSpec(memory_space=pl.ANY)


def _cp(*sem):
    return pltpu.CompilerParams(dimension_semantics=sem, vmem_limit_bytes=VMEM_LIMIT)


def _blk(n, target):
    if n % 128:
        return n
    best = 128
    for b in range(128, min(n, target) + 1, 128):
        if n % b == 0:
            best = b
    return best


def _iota(shape, dim):
    return lax.broadcasted_iota(jnp.int32, shape, dim)


def _sigmoid(x):
    return 1.0 / (1.0 + jnp.exp(-x))


def _log1p(e):
    u = 1.0 + e
    return jnp.where(u == 1.0, e, jnp.log(u) * (e / (u - 1.0)))


def _softplus(x):
    return jnp.maximum(x, 0.0) + _log1p(jnp.exp(-jnp.abs(x)))


def _expm1(x):
    u = jnp.exp(x)
    um = u - 1.0
    return jnp.where(um == 0.0, x, jnp.where(um == -1.0, -1.0, um * (x / jnp.log(u))))


_G0 = math.sqrt(2.0 / math.pi)
_G1 = 0.044715


def _gelu_and_grad(x):
    t = jnp.tanh(_G0 * (x + _G1 * x * x * x))
    g = 0.5 * x * (1.0 + t)
    dg = 0.5 * (1.0 + t) + 0.5 * x * (1.0 - t * t) * (_G0 * (1.0 + 3.0 * _G1 * x * x))
    return g, dg


_NN = (((1,), (0,)), ((), ()))
_NT = (((1,), (1,)), ((), ()))
_TN = (((0,), (0,)), ((), ()))


def _dot(a, b, dims=_NN):
    return lax.dot_general(a.astype(BF16), b.astype(BF16), dims, preferred_element_type=F32)


def _dot_hi(a, b, dims=_NN):
    return lax.dot_general(a, b, dims, precision=lax.Precision.HIGHEST, preferred_element_type=F32)


def _split(v):
    hi = v.astype(BF16)
    return hi, (v - hi.astype(F32)).astype(BF16)


def _dot01(v, e, dims=_NN):
    hi, lo = _split(v)
    return (lax.dot_general(hi, e, dims, preferred_element_type=F32)
            + lax.dot_general(lo, e, dims, preferred_element_type=F32))


def _conv_taps(xe, n):
    return [xe[8:8 + n] if j == 3 else pltpu.roll(xe, 3 - j, 0)[8:8 + n] for j in range(4)]


def _conv_fwd(taps, cw, cb):
    return cb + cw[0:1] * taps[0] + cw[1:2] * taps[1] + cw[2:3] * taps[2] + cw[3:4] * taps[3]


def _conv_bwd(dc, dnext, x, cw, n):
    ext = jnp.concatenate([dc, dnext], axis=0)
    shifted = [pltpu.roll(ext, n + 8 - (3 - j), 0)[0:n] for j in range(3)] + [dc]
    dx = cw[0:1] * shifted[0] + cw[1:2] * shifted[1] + cw[2:3] * shifted[2] + cw[3:4] * dc
    dcw = jnp.concatenate([jnp.sum(x * shifted[j], axis=0, keepdims=True) for j in range(4)], axis=0)
    return dx, dcw, jnp.sum(dc, axis=0, keepdims=True)


MM_VMEM_BUDGET = 44 * 1024 * 1024
MM_MAX_TILE = 1408
MM_MAX_K = 5632


def _divisors(n, cap):
    return [n] if n % 128 else [b for b in range(128, min(n, cap) + 1, 128) if n % b == 0]


def _mm_tiles(m_dim, n_dim, k_dim, a_bytes, b_bytes, o_bytes, has_add, tn_fixed):
    best = None
    for tm in _divisors(m_dim, MM_MAX_TILE):
        for tn in ([tn_fixed] if tn_fixed else _divisors(n_dim, MM_MAX_TILE)):
            for tk in _divisors(k_dim, MM_MAX_K):
                vmem = 2 * (tm * tk * a_bytes + tk * tn * b_bytes + tm * tn * (o_bytes + (4 if has_add else 0)))
                vmem += tm * tn * 4 if tk < k_dim else 0
                if vmem <= MM_VMEM_BUDGET:
                    key = (tm * tn * tk, tk, tn)
                    if best is None or key > best[0]:
                        best = (key, (tm, tn, tk))
    assert best is not None, (m_dim, n_dim, k_dim)
    return best[1]


def _mm(a, b, *, ta=False, tb=False, out_dtype=F32, add=None, add_scale=1.0, name, split_n=None):
    if ta:
        k_dim, m_dim = a.shape
    else:
        m_dim, k_dim = a.shape
    if tb:
        n_dim, k2 = b.shape
    else:
        k2, n_dim = b.shape
    assert k_dim == k2, (a.shape, b.shape, ta, tb)
    tm, tn, tk = _mm_tiles(m_dim, n_dim, k_dim, a.dtype.itemsize, b.dtype.itemsize, jnp.dtype(out_dtype).itemsize,
                           add is not None, split_n)
    nk = k_dim // tk
    a_spec = pl.BlockSpec((tk, tm), lambda i, j, k: (k, i)) if ta else pl.BlockSpec((tm, tk), lambda i, j, k: (i, k))
    b_spec = pl.BlockSpec((tn, tk), lambda i, j, k: (j, k)) if tb else pl.BlockSpec((tk, tn), lambda i, j, k: (k, j))
    o_spec = pl.BlockSpec((tm, tn), lambda i, j, k: (i, j))
    out_shape = (m_dim, n_dim)
    if split_n is not None:
        assert add is None and tn == split_n, (tn, split_n)
        o_spec = pl.BlockSpec((None, tm, tn), lambda i, j, k: (j, i, 0))
        out_shape = (n_dim // tn, m_dim, tn)
    dims = (((0 if ta else 1,), (1 if tb else 0,)), ((), ()))
    has_add = add is not None

    def body(*refs):
        a_ref, b_ref = refs[:2]
        add_ref = refs[2] if has_add else None
        o_ref = refs[3] if has_add else refs[2]
        acc_ref = refs[-1] if nk > 1 else None
        k = pl.program_id(2)

        def product():
            return lax.dot_general(a_ref[...].astype(BF16), b_ref[...].astype(BF16), dims, preferred_element_type=F32)

        def finish(r):
            if has_add:
                r = r + add_scale * add_ref[...]
            o_ref[...] = r.astype(out_dtype)

        if nk == 1:
            finish(product())
            return

        @pl.when(k == 0)
        def _():
            acc_ref[...] = product()

        @pl.when((k > 0) & (k < nk - 1))
        def _():
            acc_ref[...] += product()

        @pl.when(k == nk - 1)
        def _():
            finish(acc_ref[...] + product())

    in_specs = [a_spec, b_spec] + ([o_spec] if has_add else [])
    args = (a, b) + ((add,) if has_add else ())
    return pl.pallas_call(
        body, name=name, grid=(m_dim // tm, n_dim // tn, nk),
        in_specs=in_specs, out_specs=o_spec,
        out_shape=jax.ShapeDtypeStruct(out_shape, out_dtype),
        scratch_shapes=[pltpu.VMEM((tm, tn), F32)] if nk > 1 else [],
        cost_estimate=pl.CostEstimate(
            flops=2 * m_dim * n_dim * k_dim, transcendentals=0,
            bytes_accessed=a.size * a.dtype.itemsize + b.size * b.dtype.itemsize
            + m_dim * n_dim * (jnp.dtype(out_dtype).itemsize + (4 if has_add else 0))),
        compiler_params=_cp("parallel", "parallel", "arbitrary"),
    )(*args)


def _ln_fwd(x, f, g, b, name, tb=512):
    t = x.shape[0]
    tb = min(tb, t)

    def body(x_ref, f_ref, g_ref, b_ref, o_ref):
        u = ALPHA * x_ref[...] + f_ref[...]
        mu = jnp.mean(u, axis=-1, keepdims=True)
        d = u - mu
        var = jnp.mean(d * d, axis=-1, keepdims=True)
        o_ref[...] = d * lax.rsqrt(var + EPS) * g_ref[...] + b_ref[...]

    row = pl.BlockSpec((tb, D), lambda i: (i, 0))
    par = pl.BlockSpec((1, D), lambda i: (0, 0))
    return pl.pallas_call(
        body, name=name, grid=(t // tb,), in_specs=[row, row, par, par], out_specs=row,
        out_shape=jax.ShapeDtypeStruct((t, D), F32), compiler_params=_cp("parallel"),
    )(x, f, g, b)


def _ln_bwd(x, f, dy, g, name, tb=512):
    t = x.shape[0]
    tb = min(tb, t)

    def body(x_ref, f_ref, dy_ref, g_ref, du_ref, dg_ref, db_ref):
        @pl.when(pl.program_id(0) == 0)
        def _():
            dg_ref[...] = jnp.zeros_like(dg_ref)
            db_ref[...] = jnp.zeros_like(db_ref)

        u = ALPHA * x_ref[...] + f_ref[...]
        mu = jnp.mean(u, axis=-1, keepdims=True)
        d = u - mu
        var = jnp.mean(d * d, axis=-1, keepdims=True)
        rstd = lax.rsqrt(var + EPS)
        xhat = d * rstd
        dy = dy_ref[...]
        dxh = dy * g_ref[...]
        m1 = jnp.mean(dxh, axis=-1, keepdims=True)
        m2 = jnp.mean(dxh * xhat, axis=-1, keepdims=True)
        du_ref[...] = rstd * (dxh - m1 - xhat * m2)
        dg_ref[...] += jnp.sum(dy * xhat, axis=0, keepdims=True)
        db_ref[...] += jnp.sum(dy, axis=0, keepdims=True)

    row = pl.BlockSpec((tb, D), lambda i: (i, 0))
    par = pl.BlockSpec((1, D), lambda i: (0, 0))
    return pl.pallas_call(
        body, name=name, grid=(t // tb,), in_specs=[row, row, row, par], out_specs=[row, par, par],
        out_shape=[jax.ShapeDtypeStruct((t, D), F32), jax.ShapeDtypeStruct((1, D), F32),
                   jax.ShapeDtypeStruct((1, D), F32)],
        compiler_params=_cp("arbitrary"),
    )(x, f, dy, g)


FFN_TM, FFN_TN = 512, D_FF // 2


def _ffn_in_swiglu(x, w, name):
    t = x.shape[0]
    tm = min(FFN_TM, t)
    nj = D_FF // FFN_TN

    def body(x_ref, wg_ref, wu_ref, g_ref, u_ref, a_ref):
        xb = x_ref[...].astype(BF16)
        g = lax.dot_general(xb, wg_ref[...], _NN, preferred_element_type=F32)
        u = lax.dot_general(xb, wu_ref[...], _NN, preferred_element_type=F32)
        g_ref[...] = g
        u_ref[...] = u
        a_ref[...] = (g * _sigmoid(g) * u).astype(BF16)

    tile = pl.BlockSpec((tm, FFN_TN), lambda i, j: (i, j))
    return pl.pallas_call(
        body, name=name, grid=(t // tm, nj),
        in_specs=[pl.BlockSpec((tm, D), lambda i, j: (i, 0)), pl.BlockSpec((D, FFN_TN), lambda i, j: (0, j)),
                  pl.BlockSpec((D, FFN_TN), lambda i, j: (0, nj + j))],
        out_specs=[tile, tile, tile],
        out_shape=[jax.ShapeDtypeStruct((t, D_FF), F32), jax.ShapeDtypeStruct((t, D_FF), F32),
                   jax.ShapeDtypeStruct((t, D_FF), BF16)],
        compiler_params=_cp("parallel", "parallel"),
    )(x, w, w)


def _d_swiglu(du, w_down, g, u, name):
    t = du.shape[0]
    tm = min(FFN_TM, t)

    def body(du_ref, w_ref, g_ref, u_ref, dg_ref, dup_ref):
        da = lax.dot_general(du_ref[...].astype(BF16), w_ref[...], _NT, preferred_element_type=F32)
        g_v = g_ref[...]
        s = _sigmoid(g_v)
        dg_ref[...] = (da * u_ref[...] * (s * (1.0 + g_v * (1.0 - s)))).astype(BF16)
        dup_ref[...] = (da * g_v * s).astype(BF16)

    tile = pl.BlockSpec((tm, FFN_TN), lambda i, j: (i, j))
    return pl.pallas_call(
        body, name=name, grid=(t // tm, D_FF // FFN_TN),
        in_specs=[pl.BlockSpec((tm, D), lambda i, j: (i, 0)), pl.BlockSpec((FFN_TN, D), lambda i, j: (j, 0)), tile, tile],
        out_specs=[tile, tile],
        out_shape=[jax.ShapeDtypeStruct((t, D_FF), BF16), jax.ShapeDtypeStruct((t, D_FF), BF16)],
        compiler_params=_cp("parallel", "parallel"),
    )(du, w_down, g, u)


def _merge_fwd(pgl, bg, b1, b2, b3, name, tb=512):
    t = pgl.shape[0]
    tb = min(tb, t)

    def body(gl_ref, bg_ref, b1_ref, b2_ref, b3_ref, o_ref):
        acc = None
        for j, b_ref in enumerate((b1_ref, b2_ref, b3_ref)):
            sl = slice(j * D, (j + 1) * D)
            term = _sigmoid(gl_ref[:, sl] + bg_ref[:, sl]) * b_ref[...]
            acc = term if acc is None else acc + term
        o_ref[...] = acc.astype(BF16)

    row = pl.BlockSpec((tb, D), lambda i: (i, 0))
    return pl.pallas_call(
        body, name=name, grid=(t // tb,),
        in_specs=[pl.BlockSpec((tb, 3 * D), lambda i: (i, 0)), pl.BlockSpec((1, 3 * D), lambda i: (0, 0)), row, row, row],
        out_specs=row, out_shape=jax.ShapeDtypeStruct((t, D), BF16), compiler_params=_cp("parallel"),
    )(pgl, bg, b1, b2, b3)


def _merge_bwd(pgl, bg, b1, b2, b3, dm, name, tb=512):
    t = pgl.shape[0]
    tb = min(tb, t)

    def body(gl_ref, bg_ref, b1_ref, b2_ref, b3_ref, dm_ref, dgl_ref, d1_ref, d2_ref, d3_ref, dbg_ref):
        @pl.when(pl.program_id(0) == 0)
        def _():
            dbg_ref[...] = jnp.zeros_like(dbg_ref)

        dm_v = dm_ref[...]
        for j, (b_ref, d_ref) in enumerate(((b1_ref, d1_ref), (b2_ref, d2_ref), (b3_ref, d3_ref))):
            sl = slice(j * D, (j + 1) * D)
            gate = _sigmoid(gl_ref[:, sl] + bg_ref[:, sl])
            d_ref[...] = (dm_v * gate).astype(BF16)
            dgl = dm_v * b_ref[...] * (gate * (1.0 - gate))
            dgl_ref[:, sl] = dgl.astype(BF16)
            dbg_ref[:, sl] += jnp.sum(dgl, axis=0, keepdims=True)

    row = pl.BlockSpec((tb, D), lambda i: (i, 0))
    wide = pl.BlockSpec((tb, 3 * D), lambda i: (i, 0))
    par = pl.BlockSpec((1, 3 * D), lambda i: (0, 0))
    return pl.pallas_call(
        body, name=name, grid=(t // tb,),
        in_specs=[wide, par, row, row, row, row], out_specs=[wide, row, row, row, par],
        out_shape=[jax.ShapeDtypeStruct((t, 3 * D), BF16)] + [jax.ShapeDtypeStruct((t, D), BF16)] * 3
                  + [jax.ShapeDtypeStruct((1, 3 * D), F32)],
        compiler_params=_cp("arbitrary"),
    )(pgl, bg, b1, b2, b3, dm)


def _xa_probs(q, kv_ref, hd):
    sl = slice(hd * XA_HEAD_DIM, (hd + 1) * XA_HEAD_DIM)
    k = kv_ref[:, sl]
    v = kv_ref[:, D + hd * XA_HEAD_DIM:D + (hd + 1) * XA_HEAD_DIM]
    s = _dot(q[:, sl], k, _NT) * (XA_HEAD_DIM ** -0.5)
    e = jnp.exp(s - jnp.max(s, axis=1, keepdims=True))
    return sl, k, v, e / jnp.sum(e, axis=1, keepdims=True)


def _xa_fwd(pq, kv, name, tb=512):
    t = pq.shape[0]
    tb = min(tb, t)

    def body(q_ref, kv_ref, o_ref):
        q = q_ref[...]
        for hd in range(XA_HEADS):
            sl, _, v, p = _xa_probs(q, kv_ref, hd)
            o_ref[:, sl] = _dot(p, v).astype(BF16)

    row = pl.BlockSpec((tb, D), lambda i: (i, 0))
    return pl.pallas_call(
        body, name=name, grid=(t // tb,),
        in_specs=[row, pl.BlockSpec(kv.shape, lambda i: (0, 0))], out_specs=row,
        out_shape=jax.ShapeDtypeStruct((t, D), BF16), compiler_params=_cp("parallel"),
    )(pq, kv)


def _xa_bwd(pq, kv, dy, name, tb=512):
    t = pq.shape[0]
    tb = min(tb, t)

    def body(q_ref, kv_ref, dy_ref, dq_ref, dkv_ref):
        @pl.when(pl.program_id(0) == 0)
        def _():
            dkv_ref[...] = jnp.zeros_like(dkv_ref)

        q = q_ref[...]
        for hd in range(XA_HEADS):
            sl, k, v, p = _xa_probs(q, kv_ref, hd)
            dyh = dy_ref[:, sl]
            vsl = slice(D + hd * XA_HEAD_DIM, D + (hd + 1) * XA_HEAD_DIM)
            dkv_ref[:, vsl] += _dot(p, dyh, _TN)
            dp = _dot(dyh, v, _NT)
            ds = p * (dp - jnp.sum(dp * p, axis=1, keepdims=True)) * (XA_HEAD_DIM ** -0.5)
            dq_ref[:, sl] = _dot(ds, k).astype(BF16)
            dkv_ref[:, sl] += _dot(ds, q[:, sl], _TN)

    row = pl.BlockSpec((tb, D), lambda i: (i, 0))
    kvs = pl.BlockSpec(kv.shape, lambda i: (0, 0))
    return pl.pallas_call(
        body, name=name, grid=(t // tb,), in_specs=[row, kvs, row], out_specs=[row, kvs],
        out_shape=[jax.ShapeDtypeStruct((t, D), BF16), jax.ShapeDtypeStruct(kv.shape, F32)],
        compiler_params=_cp("arbitrary"),
    )(pq, kv, dy)


def _scan_fwd(a, u):
    n = a.shape[0]
    row = _iota((n, 1), 0)
    d = 1
    while d < n:
        us = jnp.where(row >= d, pltpu.roll(u, d, 0), 0.0)
        u = a * us + u
        a = a * pltpu.roll(a, d, 0)
        d *= 2
    return u


def _scan_rev(b, u):
    n = b.shape[0]
    row = _iota((n, 1), 0)
    d = 1
    while d < n:
        us = jnp.where(row < n - d, pltpu.roll(u, n - d, 0), 0.0)
        u = b * us + u
        b = b * pltpu.roll(b, n - d, 0)
        d *= 2
    return u


def _lru_gates(xc, wa_ref, ba, wi_ref, bi, lam):
    za = jnp.concatenate([_dot(xc[:, n * 128:(n + 1) * 128], wa_ref[n]) for n in range(LRU_BLOCKS)], axis=1) + ba
    zi = jnp.concatenate([_dot(xc[:, n * 128:(n + 1) * 128], wi_ref[n]) for n in range(LRU_BLOCKS)], axis=1) + bi
    r = _sigmoid(za)
    ig = _sigmoid(zi)
    sp = _softplus(-lam)
    log_a = (-LRU_C) * r * sp
    a = jnp.exp(log_a)
    m = jnp.sqrt(-_expm1(2.0 * log_a))
    u = m * (ig * xc)
    return a, u, r, ig, m, sp


def _lru_fwd(p, cw, cb, wa, ba, wi, bi, lam, name, tb=256):
    t = p.shape[0]
    tb = min(tb, t)
    nb = t // tb
    r8 = tb // 8

    def body(x_ref, xp_ref, g_ref, cw_ref, cb_ref, wa_ref, ba_ref, wi_ref, bi_ref, lam_ref, y_ref, h_ref, xc_ref,
             hc_ref):
        i = pl.program_id(0)

        @pl.when(i == 0)
        def _():
            hc_ref[...] = jnp.zeros_like(hc_ref)

        halo = jnp.where(i == 0, 0.0, xp_ref[...])
        taps = _conv_taps(jnp.concatenate([halo, x_ref[...]], axis=0), tb)
        xc = _conv_fwd(taps, cw_ref[...], cb_ref[...])
        xc_ref[...] = xc
        a, u, _, _, _, _ = _lru_gates(xc, wa_ref, ba_ref[...], wi_ref, bi_ref[...], lam_ref[...])
        row = _iota((tb, 1), 0)
        u = u + jnp.where(row == 0, a * hc_ref[...], 0.0)
        h = _scan_fwd(a, u)
        h_ref[...] = h
        hc_ref[...] = h[tb - 1:tb, :]
        gl, _ = _gelu_and_grad(g_ref[...])
        y_ref[...] = (gl * h).astype(BF16)

    par = pl.BlockSpec((1, D), lambda i: (0, 0))
    wsp = pl.BlockSpec((LRU_BLOCKS, LRU_BLOCK, LRU_BLOCK), lambda i: (0, 0, 0))
    row = pl.BlockSpec((tb, D), lambda i: (i, 0))
    return pl.pallas_call(
        body, name=name, grid=(nb,),
        in_specs=[row, pl.BlockSpec((8, D), lambda i: (jnp.maximum(i * r8 - 1, 0), 0)),
                  pl.BlockSpec((tb, D), lambda i: (i, 1)),
                  pl.BlockSpec((4, D), lambda i: (0, 0)), par, wsp, par, wsp, par, par],
        out_specs=[row, row, row],
        out_shape=[jax.ShapeDtypeStruct((t, D), BF16), jax.ShapeDtypeStruct((t, D), F32),
                   jax.ShapeDtypeStruct((t, D), F32)],
        scratch_shapes=[pltpu.VMEM((1, D), F32)],
        compiler_params=_cp("arbitrary"),
    )(p, p, p, cw, cb, wa, ba, wi, bi, lam)


def _lru_bwd(p, xc, h, dy, cw, wa, ba, wi, bi, lam, name, tb=256):
    t = p.shape[0]
    tb = min(tb, t)
    nb = t // tb
    r8 = tb // 8

    def body(x_ref, g_ref, xc_ref, h_ref, hp_ref, dy_ref, cw_ref, wa_ref, ba_ref, wi_ref, bi_ref, lam_ref,
             dp_ref, dcw_ref, dcb_ref, dwa_ref, dba_ref, dwi_ref, dbi_ref, dlam_ref, carry_ref, dnext_ref):
        i = pl.program_id(0)
        blk = nb - 1 - i

        @pl.when(i == 0)
        def _():
            for r in (dcw_ref, dcb_ref, dwa_ref, dba_ref, dwi_ref, dbi_ref, dlam_ref, carry_ref, dnext_ref):
                r[...] = jnp.zeros_like(r)

        xc = xc_ref[...]
        lam = lam_ref[...]
        a, _, r, ig, m, sp = _lru_gates(xc, wa_ref, ba_ref[...], wi_ref, bi_ref[...], lam)
        gl, dgl = _gelu_and_grad(g_ref[...])
        h = h_ref[...]
        dy = dy_ref[...]
        dp_ref[:, D:] = (dy * h * dgl).astype(BF16)
        row = _iota((tb, 1), 0)
        dh = dy * gl + jnp.where(row == tb - 1, carry_ref[...], 0.0)
        b = jnp.where(row < tb - 1, pltpu.roll(a, tb - 1, 0), 0.0)
        gs = _scan_rev(b, dh)
        carry_ref[...] = a[0:1] * gs[0:1]
        h_last = jnp.where(blk == 0, 0.0, hp_ref[7:8, :])
        hprev = jnp.where(row == 0, h_last, pltpu.roll(h, 1, 0))
        da = gs * hprev
        dm = gs * ig * xc
        di = gs * m * xc
        dxc = gs * m * ig
        dlog = (0.5 * dm / m) * (-2.0 * a * a) + da * a
        dr = dlog * ((-LRU_C) * sp)
        dsp = jnp.sum(dlog * ((-LRU_C) * r), axis=0, keepdims=True)
        dlam_ref[...] += dsp * (-_sigmoid(-lam))
        dza = dr * r * (1.0 - r)
        dzi = di * ig * (1.0 - ig)
        dba_ref[...] += jnp.sum(dza, axis=0, keepdims=True)
        dbi_ref[...] += jnp.sum(dzi, axis=0, keepdims=True)
        parts = []
        for n in range(LRU_BLOCKS):
            sl = slice(n * 128, (n + 1) * 128)
            dwa_ref[n] += _dot(xc[:, sl], dza[:, sl], _TN)
            dwi_ref[n] += _dot(xc[:, sl], dzi[:, sl], _TN)
            parts.append(_dot(dza[:, sl], wa_ref[n], _NT) + _dot(dzi[:, sl], wi_ref[n], _NT))
        dxc = dxc + jnp.concatenate(parts, axis=1)
        dx, dcw, dcb = _conv_bwd(dxc, dnext_ref[...], x_ref[...], cw_ref[...], tb)
        dp_ref[:, :D] = dx.astype(BF16)
        dcw_ref[...] += dcw
        dcb_ref[...] += dcb
        dnext_ref[...] = dxc[0:8]

    par = pl.BlockSpec((1, D), lambda i: (0, 0))
    wsp = pl.BlockSpec((LRU_BLOCKS, LRU_BLOCK, LRU_BLOCK), lambda i: (0, 0, 0))
    cws = pl.BlockSpec((4, D), lambda i: (0, 0))
    rev = lambda i: nb - 1 - i
    blk0 = pl.BlockSpec((tb, D), lambda i: (rev(i), 0))
    w_shape = jax.ShapeDtypeStruct((LRU_BLOCKS, LRU_BLOCK, LRU_BLOCK), F32)
    v_shape = jax.ShapeDtypeStruct((1, D), F32)
    return pl.pallas_call(
        body, name=name, grid=(nb,),
        in_specs=[blk0, pl.BlockSpec((tb, D), lambda i: (rev(i), 1)), blk0, blk0,
                  pl.BlockSpec((8, D), lambda i: (jnp.maximum(rev(i) * r8 - 1, 0), 0)), blk0,
                  cws, wsp, par, wsp, par, par],
        out_specs=[pl.BlockSpec((tb, 2 * D), lambda i: (rev(i), 0)), cws, par, wsp, par, wsp, par, par],
        out_shape=[jax.ShapeDtypeStruct((t, 2 * D), BF16), jax.ShapeDtypeStruct((4, D), F32), v_shape,
                   w_shape, v_shape, w_shape, v_shape, v_shape],
        scratch_shapes=[pltpu.VMEM((1, D), F32), pltpu.VMEM((8, D), F32)],
        compiler_params=_cp("arbitrary"),
    )(p, p, xc, h, h, dy, cw, wa, ba, wi, bi, lam)


def _ssd_consts():
    m0 = _iota((1, 128), 1) < 64
    e = (jnp.right_shift(_iota((SSD_HEADS, D_SSD), 1), 6) == _iota((SSD_HEADS, D_SSD), 0)).astype(BF16)
    tril = (_iota((CHUNK, CHUNK), 0) >= _iota((CHUNK, CHUNK), 1)).astype(F32)
    eye = (_iota((SSD_HEADS, SSD_HEADS), 0) == _iota((SSD_HEADS, SSD_HEADS), 1)).astype(F32)
    r2 = _iota((CHUNK, 128), 0)
    c2 = jnp.bitwise_and(_iota((CHUNK, 128), 1), 63)
    return dict(m0=m0, e=e, tril=tril, eye=eye, causal2=r2 >= c2, fold=(c2 == r2).astype(BF16))


def _ssd_pre(c, p_ref, dtb_ref, alog_ref, dvec_ref, k):
    sg = _sigmoid(c)
    xbc = c * sg
    dtp = p_ref[:, S_DT:S_DT + DT_REAL] + dtb_ref[...]
    dt = _softplus(dtp)
    a = -jnp.exp(alog_ref[...])
    cs = _dot_hi(k["tril"], dt * a)
    cs_last = cs[CHUNK - 1:CHUNK]
    dend = jnp.exp(cs_last - cs)
    cdec = jnp.exp(cs_last)
    big = _dot01(jnp.concatenate([dt, jnp.exp(cs), dend], axis=0), k["e"])
    small = _dot01(jnp.concatenate([jnp.broadcast_to(cdec, (8, SSD_HEADS)),
                                    jnp.broadcast_to(dvec_ref[...], (8, SSD_HEADS))], axis=0), k["e"])
    cst2 = _dot_hi(k["eye"], jnp.concatenate([cs, cs], axis=0), _NT)
    return dict(c=c, sg=sg, xs=xbc[:, :D_SSD], bm=xbc[:, D_SSD:D_SSD + 512],
                cm=xbc[:, D_SSD + 512:], dtp=dtp, dt=dt, a=a, cs=cs, dend=dend, cdec=cdec,
                dtx=big[0:CHUNK], ecx=big[CHUNK:2 * CHUNK], dex=big[2 * CHUNK:3 * CHUNK],
                cdx=small[0:1], ddx=small[8:9], cst2=cst2)


def _pair_decay(p, cs, cst2, k):
    h0, h1 = 2 * p, 2 * p + 1
    colp = jnp.where(k["m0"], cs[:, h0:h0 + 1], cs[:, h1:h1 + 1])
    rowp = jnp.where(k["m0"], cst2[h0:h0 + 1, :], cst2[h1:h1 + 1, :])
    return jnp.where(k["causal2"], jnp.exp(colp - rowp), 0.0)


def _pair_stack(xp, k):
    return jnp.concatenate([jnp.where(k["m0"], xp, 0.0), jnp.where(k["m0"], 0.0, xp)], axis=0)


def _group_norm(yz, nw, with_stats=False):
    outs, stats = [], []
    for g in range(SSD_GROUPS):
        yzg = yz[:, g * GROUP_W:(g + 1) * GROUP_W]
        r = lax.rsqrt(jnp.mean(yzg * yzg, axis=1, keepdims=True) + EPS)
        outs.append(yzg * r)
        stats.append(r)
    y = jnp.concatenate(outs, axis=1) * nw
    return (y, stats) if with_stats else y


def _ssd_fwd(p, cw, cb, dtb, alog, dvec, nw, name):
    t = p.shape[0]
    nc = t // CHUNK

    def body(p_ref, pp_ref, cw_ref, cb_ref, dtb_ref, alog_ref, dvec_ref, nw_ref, y_ref, yraw_ref, hs_ref, c_ref,
             h_scr):
        i = pl.program_id(0)

        @pl.when(i == 0)
        def _():
            h_scr[...] = jnp.zeros_like(h_scr)

        k = _ssd_consts()
        halo = jnp.where(i == 0, 0.0, pp_ref[:, S_XBC:S_DT])
        taps = _conv_taps(jnp.concatenate([halo, p_ref[:, S_XBC:S_DT]], axis=0), CHUNK)
        c = _conv_fwd(taps, cw_ref[...], cb_ref[...])
        c_ref[...] = c
        s = _ssd_pre(c, p_ref, dtb_ref, alog_ref, dvec_ref, k)
        xs, bm, cm = s["xs"], s["bm"], s["cm"]
        xdt = xs * s["dtx"]
        hprev = h_scr[...]
        hs_ref[0] = hprev
        ys, hn = [], []
        for g in range(SSD_GROUPS):
            gs = slice(g * GROUP_W, (g + 1) * GROUP_W)
            bg = bm[:, g * 128:(g + 1) * 128]
            cg = cm[:, g * 128:(g + 1) * 128]
            cbdup = _dot(cg, jnp.concatenate([bg, bg], axis=0), _NT)
            hp_g = hprev[:, gs]
            yd = []
            for q in range(4):
                pr = g * 4 + q
                mp = cbdup * _pair_decay(pr, s["cs"], s["cst2"], k)
                yd.append(_dot(mp, _pair_stack(xdt[:, pr * 128:(pr + 1) * 128], k)))
            ys.append(jnp.concatenate(yd, axis=1) + _dot(cg, hp_g) * s["ecx"][:, gs])
            hn.append(hp_g * s["cdx"][:, gs] + _dot(bg, xdt[:, gs] * s["dex"][:, gs], _TN))
        h_scr[...] = jnp.concatenate(hn, axis=1)
        yraw = jnp.concatenate(ys, axis=1) + s["ddx"] * xs
        yraw_ref[...] = yraw
        z = p_ref[:, S_Z:S_Z + D_SSD]
        y_ref[...] = _group_norm(yraw * (z * _sigmoid(z)), nw_ref[...]).astype(BF16)

    hv = pl.BlockSpec((1, DT_REAL), lambda i: (0, 0))
    return pl.pallas_call(
        body, name=name, grid=(nc,),
        in_specs=[pl.BlockSpec((CHUNK, W_SSD), lambda i: (i, 0)),
                  pl.BlockSpec((8, W_SSD), lambda i: (jnp.maximum(i * (CHUNK // 8) - 1, 0), 0)),
                  pl.BlockSpec((4, D_XBC), lambda i: (0, 0)), pl.BlockSpec((1, D_XBC), lambda i: (0, 0)),
                  hv, hv, hv, pl.BlockSpec((1, D_SSD), lambda i: (0, 0))],
        out_specs=[pl.BlockSpec((CHUNK, D_SSD), lambda i: (i, 0)), pl.BlockSpec((CHUNK, D_SSD), lambda i: (i, 0)),
                   pl.BlockSpec((1, SSD_STATE, D_SSD), lambda i: (i, 0, 0)),
                   pl.BlockSpec((CHUNK, D_XBC), lambda i: (i, 0))],
        out_shape=[jax.ShapeDtypeStruct((t, D_SSD), BF16), jax.ShapeDtypeStruct((t, D_SSD), F32),
                   jax.ShapeDtypeStruct((nc, SSD_STATE, D_SSD), F32), jax.ShapeDtypeStruct((t, D_XBC), F32)],
        scratch_shapes=[pltpu.VMEM((SSD_STATE, D_SSD), F32)],
        compiler_params=_cp("arbitrary"),
    )(p, p, cw, cb, dtb, alog, dvec, nw)


def _ssd_bwd(p, c, yraw, hs, dy, cw, dtb, alog, dvec, nw, name):
    t = p.shape[0]
    nc = t // CHUNK

    def body(p_ref, c_ref, yraw_ref, hs_ref, dy_ref, cw_ref, dtb_ref, alog_ref, dvec_ref, nw_ref,
             dp_ref, dcw_ref, dcb_ref, ddtb_ref, dalog_ref, dd_ref, dnw_ref, dh_scr, dnext_scr):
        i = pl.program_id(0)

        @pl.when(i == 0)
        def _():
            for r in (dcw_ref, dcb_ref, ddtb_ref, dalog_ref, dd_ref, dnw_ref, dh_scr, dnext_scr):
                r[...] = jnp.zeros_like(r)

        k = _ssd_consts()
        s = _ssd_pre(c_ref[...], p_ref, dtb_ref, alog_ref, dvec_ref, k)
        xs, bm, cm, cs, dt, a = s["xs"], s["bm"], s["cm"], s["cs"], s["dt"], s["a"]
        m0 = k["m0"]
        xdt = xs * s["dtx"]
        hprev = hs_ref[0]
        dh = dh_scr[...]

        nw_v = nw_ref[...]
        yraw = yraw_ref[...]
        z = p_ref[:, S_Z:S_Z + D_SSD]
        sz = _sigmoid(z)
        siluz = z * sz
        yz = yraw * siluz
        dyo = dy_ref[...]
        dyn = dyo * nw_v
        dyz_parts, dnw_parts = [], []
        for g in range(SSD_GROUPS):
            gs = slice(g * GROUP_W, (g + 1) * GROUP_W)
            yzg = yz[:, gs]
            r = lax.rsqrt(jnp.mean(yzg * yzg, axis=1, keepdims=True) + EPS)
            dnw_parts.append(jnp.sum(dyo[:, gs] * yzg * r, axis=0, keepdims=True))
            dyz_parts.append(r * dyn[:, gs] - yzg * (r * r * r) * jnp.mean(dyn[:, gs] * yzg, axis=1, keepdims=True))
        dnw_ref[...] += jnp.concatenate(dnw_parts, axis=1)
        dyz = jnp.concatenate(dyz_parts, axis=1)
        d_y = dyz * siluz
        dp_ref[:, S_Z:S_Z + D_SSD] = (dyz * yraw * (sz * (1.0 + z * (1.0 - sz)))).astype(BF16)
        dd_row = jnp.sum(d_y * xs, axis=0, keepdims=True)
        dxs = d_y * s["ddx"]

        lane_h = _iota((1, SSD_HEADS), 1)
        sub_h = _iota((SSD_HEADS, 1), 0)
        dcs = jnp.zeros((CHUNK, SSD_HEADS), F32)
        dcst2 = jnp.zeros((SSD_HEADS, 128), F32)
        dxdt_parts, db_parts, dc_parts, dhp_parts, yoff_parts, dend_parts, dcd_parts = [], [], [], [], [], [], []
        for g in range(SSD_GROUPS):
            gs = slice(g * GROUP_W, (g + 1) * GROUP_W)
            bg = bm[:, g * 128:(g + 1) * 128]
            cg = cm[:, g * 128:(g + 1) * 128]
            bdup = jnp.concatenate([bg, bg], axis=0)
            cbdup = _dot(cg, bdup, _NT)
            dcb2 = jnp.zeros((CHUNK, 128), F32)
            dxp_parts = []
            for q in range(4):
                pr = g * 4 + q
                h0, h1 = 2 * pr, 2 * pr + 1
                lp = _pair_decay(pr, cs, s["cst2"], k)
                mp = cbdup * lp
                xst = _pair_stack(xdt[:, pr * 128:(pr + 1) * 128], k)
                dyp = d_y[:, pr * 128:(pr + 1) * 128]
                dmp = _dot(dyp, xst, _NT)
                dxst = _dot(mp, dyp, _TN)
                dxp_parts.append(jnp.where(m0, dxst[:CHUNK], dxst[CHUNK:]))
                dcb2 = dcb2 + dmp * lp
                dlm = dmp * mp
                rs0 = jnp.sum(jnp.where(m0, dlm, 0.0), axis=1, keepdims=True)
                rs1 = jnp.sum(jnp.where(m0, 0.0, dlm), axis=1, keepdims=True)
                dcs = dcs + jnp.where(lane_h == h0, rs0, 0.0) + jnp.where(lane_h == h1, rs1, 0.0)
                colsum = jnp.sum(dlm, axis=0, keepdims=True)
                sel = ((sub_h == h0) & m0) | ((sub_h == h1) & jnp.logical_not(m0))
                dcst2 = dcst2 - jnp.where(sel, colsum, 0.0)
            dcg = _dot(dcb2, bdup)
            dbdup = _dot(dcb2, cg, _TN)
            dbg = dbdup[:CHUNK] + dbdup[CHUNK:]
            hp_g = hprev[:, gs]
            zoff = _dot(cg, hp_g)
            dzo = d_y[:, gs] * s["ecx"][:, gs]
            dcg = dcg + _dot(dzo, hp_g, _NT)
            dh_g = dh[:, gs]
            dhp_parts.append(_dot(cg, dzo, _TN) + dh_g * s["cdx"][:, gs])
            dcd_parts.append(jnp.sum(dh_g * hp_g, axis=0, keepdims=True))
            wg = xdt[:, gs] * s["dex"][:, gs]
            dbg = dbg + _dot(wg, dh_g, _NT)
            dwg = _dot(bg, dh_g)
            dxdt_parts.append(jnp.concatenate(dxp_parts, axis=1) + dwg * s["dex"][:, gs])
            dend_g = dwg * wg
            dend_parts.append(jnp.sum(dend_g, axis=0, keepdims=True))
            yoff_parts.append(dzo * zoff - dend_g)
            db_parts.append(dbg)
            dc_parts.append(dcg)
        dh_scr[...] = jnp.concatenate(dhp_parts, axis=1)
        dxdt = jnp.concatenate(dxdt_parts, axis=1)
        sums = _dot01(jnp.concatenate([jnp.concatenate(yoff_parts, axis=1), dxdt * xs], axis=0), k["e"], _NT)
        rows8 = jnp.concatenate([jnp.broadcast_to(jnp.concatenate(r, axis=1), (8, D_SSD))
                                 for r in (dcd_parts, [dd_row], dend_parts)], axis=0)
        small = _dot01(rows8, k["e"], _NT)
        dd_ref[...] += small[8:9]
        dcs_last = small[0:1] * s["cdec"] + small[16:17]
        hi, lo = _split(dcst2)
        dcs = (dcs + sums[0:CHUNK]
               + lax.dot_general(k["fold"], hi, _NT, preferred_element_type=F32)
               + lax.dot_general(k["fold"], lo, _NT, preferred_element_type=F32)
               + jnp.where(_iota((CHUNK, 1), 0) == CHUNK - 1, dcs_last, 0.0))
        dda = _dot_hi(k["tril"], dcs, _TN)
        ddt = dda * a + sums[CHUNK:2 * CHUNK]
        dalog_ref[...] += jnp.sum(dda * dt, axis=0, keepdims=True) * a
        dxs = dxs + dxdt * s["dtx"]
        draw = ddt * _sigmoid(s["dtp"])
        ddtb_ref[...] += jnp.sum(draw, axis=0, keepdims=True)
        dp_ref[:, S_DT:] = jnp.zeros((CHUNK, W_SSD - S_DT), BF16)
        dp_ref[:, S_DT:S_DT + DT_REAL] = draw.astype(BF16)
        dxbc = jnp.concatenate([dxs] + db_parts + dc_parts, axis=1)
        sg, c = s["sg"], s["c"]
        dc = dxbc * (sg * (1.0 + c * (1.0 - sg)))
        dx, dcw, dcb = _conv_bwd(dc, dnext_scr[...], p_ref[:, S_XBC:S_DT], cw_ref[...], CHUNK)
        dp_ref[:, S_XBC:S_DT] = dx.astype(BF16)
        dcw_ref[...] += dcw
        dcb_ref[...] += dcb
        dnext_scr[...] = dc[0:8]

    rev = lambda i: nc - 1 - i
    hv = pl.BlockSpec((1, DT_REAL), lambda i: (0, 0))
    cws = pl.BlockSpec((4, D_XBC), lambda i: (0, 0))
    cbs = pl.BlockSpec((1, D_XBC), lambda i: (0, 0))
    nws = pl.BlockSpec((1, D_SSD), lambda i: (0, 0))
    wide = pl.BlockSpec((CHUNK, D_SSD), lambda i: (rev(i), 0))
    hshape = jax.ShapeDtypeStruct((1, DT_REAL), F32)
    return pl.pallas_call(
        body, name=name, grid=(nc,),
        in_specs=[pl.BlockSpec((CHUNK, W_SSD), lambda i: (rev(i), 0)),
                  pl.BlockSpec((CHUNK, D_XBC), lambda i: (rev(i), 0)),
                  wide, pl.BlockSpec((1, SSD_STATE, D_SSD), lambda i: (rev(i), 0, 0)), wide,
                  cws, hv, hv, hv, nws],
        out_specs=[pl.BlockSpec((CHUNK, W_SSD), lambda i: (rev(i), 0)), cws, cbs, hv, hv, hv, nws],
        out_shape=[jax.ShapeDtypeStruct((t, W_SSD), BF16), jax.ShapeDtypeStruct((4, D_XBC), F32),
                   jax.ShapeDtypeStruct((1, D_XBC), F32), hshape, hshape, hshape,
                   jax.ShapeDtypeStruct((1, D_SSD), F32)],
        scratch_shapes=[pltpu.VMEM((SSD_STATE, D_SSD), F32), pltpu.VMEM((8, D_XBC), F32)],
        compiler_params=_cp("arbitrary"),
    )(p, c, yraw, hs, dy, cw, dtb, alog, dvec, nw)


def _loss_head(y, target, name, tb=512):
    t = y.shape[0]
    tb = min(tb, t)

    def body(y_ref, t_ref, dy_ref, l_ref):
        @pl.when(pl.program_id(0) == 0)
        def _():
            l_ref[...] = jnp.zeros_like(l_ref)

        e = y_ref[...] - t_ref[...]
        dy_ref[...] = e * (1.0 / D)
        l_ref[...] += jnp.sum(jnp.sum(e * e, axis=1, keepdims=True), axis=0, keepdims=True) * (0.5 / D)

    row = pl.BlockSpec((tb, D), lambda i: (i, 0))
    return pl.pallas_call(
        body, name=name, grid=(t // tb,), in_specs=[row, row],
        out_specs=[row, pl.BlockSpec((8, 128), lambda i: (0, 0))],
        out_shape=[jax.ShapeDtypeStruct((t, D), F32), jax.ShapeDtypeStruct((8, 128), F32)],
        compiler_params=_cp("arbitrary"),
    )(y, target)


def _adamw(slots, w, m, v, name, tb):
    nl = len(slots)
    ns, r, c = slots[0].shape
    assert r % tb == 0 and w.shape == (nl, r, c), (r, tb, w.shape)

    def body(*refs):
        s_refs = refs[:nl]
        w_ref, m_ref, v_ref, g_ref, d_ref, m2_ref, v2_ref = refs[nl:]

        def total(ref):
            acc = ref[0].astype(F32)
            for j in range(1, ns):
                acc = acc + ref[j].astype(F32)
            return acc

        g = total(s_refs[0])
        for layer in range(1, nl):
            g = jnp.where(pl.program_id(0) == layer, total(s_refs[layer]), g)
        m2 = ADAM_B1 * m_ref[...] + (1.0 - ADAM_B1) * g
        v2 = ADAM_B2 * v_ref[...] + (1.0 - ADAM_B2) * (g * g)
        m_hat = m2 / (1.0 - ADAM_B1 ** ADAM_STEP)
        v_hat = v2 / (1.0 - ADAM_B2 ** ADAM_STEP)
        g_ref[...] = g
        d_ref[...] = -ADAM_LR * (m_hat / (jnp.sqrt(v_hat) + ADAM_EPS) + ADAM_WD * w_ref[...])
        m2_ref[...] = m2
        v2_ref[...] = v2

    def slot_spec(layer):
        return pl.BlockSpec((ns, tb, c), lambda l, i: (0, jnp.where(l == layer, i, 0), 0))

    row = pl.BlockSpec((None, tb, c), lambda l, i: (l, i, 0))
    shp = jax.ShapeDtypeStruct((nl, r, c), F32)
    return pl.pallas_call(
        body, name=name, grid=(nl, r // tb),
        in_specs=[slot_spec(layer) for layer in range(nl)] + [row, row, row],
        out_specs=[row, row, row, row], out_shape=[shp, shp, shp, shp], compiler_params=_cp("arbitrary", "arbitrary"),
    )(*slots, w, m, v)


def _pair_sum(own, got, name, out_dtype, tb):
    nj, _, r, c = own.shape
    mc = lax.axis_index("c")

    def body(mc_ref, a_ref, b_ref, o_ref):
        del mc_ref
        o_ref[...] = (a_ref[...] + b_ref[...]).astype(out_dtype)

    return pl.pallas_call(
        body, name=name,
        grid_spec=pltpu.PrefetchScalarGridSpec(
            num_scalar_prefetch=1, grid=(nj, r // tb),
            in_specs=[pl.BlockSpec((None, None, tb, c), lambda j, i, mc_ref: (j, mc_ref[0], i, 0)),
                      pl.BlockSpec((None, tb, c), lambda j, i, mc_ref: (j, i, 0))],
            out_specs=pl.BlockSpec((None, tb, c), lambda j, i, mc_ref: (j, i, 0))),
        out_shape=jax.ShapeDtypeStruct((nj, r, c), out_dtype), compiler_params=_cp("parallel", "parallel"),
    )(jnp.reshape(mc, (1,)).astype(jnp.int32), own, got)


def _slot_sum(slots, name):
    ns, r, c = slots.shape

    def body(s_ref, o_ref):
        g = s_ref[0]
        for j in range(1, ns):
            g = g + s_ref[j]
        o_ref[...] = g

    return pl.pallas_call(body, name=name, out_shape=jax.ShapeDtypeStruct((r, c), F32))(slots)


def _position():
    return lax.axis_index("x"), lax.axis_index("y"), lax.axis_index("c")


def _comm(exchange, peers, xs, out_shapes, sems, name, collective_id):
    n = len(xs)
    if collective_id is None:
        def body(*refs):
            exchange(refs[:n], refs[n:n + len(out_shapes)], *refs[n + len(out_shapes):])

        return pl.pallas_call(body, name=name, in_specs=[ANY] * n, out_specs=[ANY] * len(out_shapes),
                              out_shape=out_shapes, scratch_shapes=sems)(*xs)
    def launch(*refs):
        barrier = pltpu.get_barrier_semaphore()
        to = peers(*_position())
        for peer in to:
            pl.semaphore_signal(barrier, inc=1, device_id=peer, device_id_type=MESH)
        pl.semaphore_wait(barrier, len(to))
        exchange(refs[:n], refs[n:n + len(out_shapes)], *refs[n + len(out_shapes):])

    return pl.kernel(launch, out_type=out_shapes, mesh=plsc.ScalarSubcoreMesh(axis_name="seq", num_cores=1), name=name,
                     scratch_types=sems, compiler_params=pltpu.CompilerParams(collective_id=collective_id))(*xs)


def _all_gather(xs, name, collective_id=None):
    n = len(xs)
    return _comm(_gather_body, lambda x, y, c: [(x, y, 1 - c), (1 - x, y, c), (x, 1 - y, c), (1 - x, 1 - y, c)], xs,
                 [jax.ShapeDtypeStruct((N_DEV,) + x.shape, x.dtype) for x in xs],
                 [pltpu.SemaphoreType.DMA((n, 7)), pltpu.SemaphoreType.DMA((n, 7)), pltpu.SemaphoreType.DMA((n,))],
                 name, collective_id)


def _gather_body(x_refs, out_refs, send_sems, recv_sems, local_sems):
    n = len(x_refs)
    mx, my, mc = _position()
    me, sibling = (mx, my, mc), (mx, my, 1 - mc)
    chips = [(1 - mx, my), (mx, 1 - my), (1 - mx, 1 - my)]

    def copy(a, k, block, to, own=False):
        dst = out_refs[a].at[4 * block[0] + 2 * block[1] + block[2]]
        return pltpu.make_async_remote_copy(
            src_ref=x_refs[a] if own else dst, dst_ref=dst,
            send_sem=send_sems.at[a, k], recv_sem=recv_sems.at[a, k], device_id=to, device_id_type=MESH)

    mine = [pltpu.make_async_copy(x_refs[a], out_refs[a].at[4 * mx + 2 * my + mc], local_sems.at[a]) for a in range(n)]
    first = [copy(a, 1 + j, me, (*chip, mc), own=True) for j, chip in enumerate(chips) for a in range(n)]
    first += [copy(a, 0, me, sibling, own=True) for a in range(n)]
    for cp in first + mine:
        cp.start()
    passed = []
    for j, chip in enumerate(chips):
        for a in range(n):
            copy(a, 1 + j, (*chip, mc), me).wait_recv()
            passed.append(copy(a, 4 + j, (*chip, mc), sibling))
            passed[-1].start()
    for a in range(n):
        copy(a, 0, sibling, me).wait_recv()
    for j, chip in enumerate(chips):
        for a in range(n):
            copy(a, 4 + j, (*chip, 1 - mc), me).wait_recv()
    for cp in first + passed:
        cp.wait_send()
    for cp in mine:
        cp.wait()


def _exchange_sibling(gs, name, collective_id=None):
    n = len(gs)

    def exchange(g_refs, r_refs, send_sems, recv_sems):
        mx, my, mc = _position()
        cps = [pltpu.make_async_remote_copy(src_ref=g_refs[a].at[:, 1 - mc], dst_ref=r_refs[a],
                                            send_sem=send_sems.at[a], recv_sem=recv_sems.at[a],
                                            device_id=(mx, my, 1 - mc), device_id_type=MESH) for a in range(n)]
        for cp in cps:
            cp.start()
        for cp in cps:
            cp.wait()

    return _comm(exchange, lambda x, y, c: [(x, y, 1 - c)], gs,
                 [jax.ShapeDtypeStruct(g.shape[:1] + g.shape[2:], g.dtype) for g in gs],
                 [pltpu.SemaphoreType.DMA((n,)), pltpu.SemaphoreType.DMA((n,))], name, collective_id)


def _exchange_chips(ss, name, collective_id=None):
    n = len(ss)

    def exchange(s_refs, r_refs, send_sems, recv_sems, local_sems):
        mx, my, mc = _position()
        my_chip = 2 * mx + my
        chips = [(1 - mx, my), (mx, 1 - my), (1 - mx, 1 - my)]

        def copy(a, k, to_slot):
            px, py = chips[k]
            return pltpu.make_async_remote_copy(
                src_ref=s_refs[a].at[2 * px + py], dst_ref=r_refs[a].at[to_slot], send_sem=send_sems.at[a, k],
                recv_sem=recv_sems.at[a, k], device_id=(px, py, mc), device_id_type=MESH)

        sends = [copy(a, k, my_chip) for k in range(3) for a in range(n)]
        local = [pltpu.make_async_copy(s_refs[a].at[my_chip], r_refs[a].at[my_chip], local_sems.at[a])
                 for a in range(n)]
        for cp in sends + local:
            cp.start()
        for k in range(3):
            px, py = chips[k]
            for a in range(n):
                copy(a, k, 2 * px + py).wait_recv()
        for cp in sends:
            cp.wait_send()
        for cp in local:
            cp.wait()

    return _comm(exchange, lambda x, y, c: [(1 - x, y, c), (x, 1 - y, c), (1 - x, 1 - y, c)], ss,
                 [jax.ShapeDtypeStruct(s.shape, s.dtype) for s in ss],
                 [pltpu.SemaphoreType.DMA((n, 3)), pltpu.SemaphoreType.DMA((n, 3)), pltpu.SemaphoreType.DMA((n,))],
                 name, collective_id)


def _cols_concat(g, name, tb=128):
    _, k_dim, n = g.shape

    def body(g_ref, o_ref):
        o_ref[...] = jnp.concatenate([g_ref[d] for d in range(N_DEV)], axis=1)

    return pl.pallas_call(
        body, name=name, grid=(k_dim // tb,),
        in_specs=[pl.BlockSpec((N_DEV, tb, n), lambda i: (0, i, 0))],
        out_specs=pl.BlockSpec((tb, N_DEV * n), lambda i: (i, 0)),
        out_shape=jax.ShapeDtypeStruct((k_dim, N_DEV * n), g.dtype), compiler_params=_cp("parallel"),
    )(g)


def _cols_split(parts, name, tb=128):
    k_dim = parts[0].shape[0]
    n = sum(p.shape[1] for p in parts) // N_DEV

    def body(*refs):
        full = jnp.concatenate([r[...] for r in refs[:-1]], axis=1)
        for d in range(N_DEV):
            refs[-1][d] = full[:, d * n:(d + 1) * n]

    return pl.pallas_call(
        body, name=name, grid=(k_dim // tb,),
        in_specs=[pl.BlockSpec((tb, p.shape[1]), lambda i: (i, 0)) for p in parts],
        out_specs=pl.BlockSpec((N_DEV, tb, n), lambda i: (0, i, 0)),
        out_shape=jax.ShapeDtypeStruct((N_DEV, k_dim, n), parts[0].dtype), compiler_params=_cp("parallel"),
    )(*parts)


_Q0, _GL0 = 7200, 8224
N_SHARD_IN = N_IN // N_DEV


def _w_in_regions(g, name, tb=128):
    def body(g_ref, ssd_ref, lru_ref, q_ref, gl_ref):
        full = jnp.concatenate([g_ref[d] for d in range(N_DEV)], axis=1)
        lru_ref[...] = full[:, 0:2 * D]
        ssd_ref[:, :S_DT] = full[:, 2 * D:2 * D + S_DT]
        ssd_ref[:, S_DT:] = jnp.zeros((tb, W_SSD - S_DT), g.dtype)
        ssd_ref[:, S_DT:S_DT + DT_REAL] = full[:, 2 * D + S_DT:_Q0]
        q_ref[...] = full[:, _Q0:_GL0]
        gl_ref[...] = full[:, _GL0:N_IN]

    widths = (W_SSD, 2 * D, D, 3 * D)
    return pl.pallas_call(
        body, name=name, grid=(D // tb,),
        in_specs=[pl.BlockSpec((N_DEV, tb, N_SHARD_IN), lambda i: (0, i, 0))],
        out_specs=[pl.BlockSpec((tb, wd), lambda i: (i, 0)) for wd in widths],
        out_shape=[jax.ShapeDtypeStruct((D, wd), g.dtype) for wd in widths], compiler_params=_cp("parallel"),
    )(g)


def _w_in_shards(dssd, dlru, dq, dgl, name, tb=128):
    def body(ssd_ref, lru_ref, q_ref, gl_ref, o_ref):
        full = jnp.concatenate([lru_ref[...], ssd_ref[:, :S_DT + DT_REAL], q_ref[...], gl_ref[...]], axis=1)
        for d in range(N_DEV):
            o_ref[d] = full[:, d * N_SHARD_IN:(d + 1) * N_SHARD_IN]

    return pl.pallas_call(
        body, name=name, grid=(D // tb,),
        in_specs=[pl.BlockSpec((tb, a.shape[1]), lambda i: (i, 0)) for a in (dssd, dlru, dq, dgl)],
        out_specs=pl.BlockSpec((N_DEV, tb, N_SHARD_IN), lambda i: (0, i, 0)),
        out_shape=jax.ShapeDtypeStruct((N_DEV, D, N_SHARD_IN), F32), compiler_params=_cp("parallel"),
    )(dssd, dlru, dq, dgl)


_BIG = (("w_in", "col", (1024, 1412)), ("mem_w_kv", "col", (1024, 256)), ("w_br_lru", "row", (128, 1024)),
        ("w_br_ssd", "row", (256, 1024)), ("w_br_xa", "row", (128, 1024)), ("w_out", "row", (128, 1024)),
        ("ffn_w_in", "col", (1024, 704)), ("ffn_w_down", "row", (352, 1024)))
_SMALL = (("b_gate", (3, 128)), ("lru_conv_w", (4, 128)), ("ssd_conv_w", (4, 384)))
_REP = (("lru_conv_b", (1024,)), ("lru_w_a", (8, 128, 128)), ("lru_b_a", (1024,)), ("lru_w_i", (8, 128, 128)),
        ("lru_b_i", (1024,)), ("lru_lambda", (1024,)), ("ssd_conv_b", (3072,)), ("ssd_dt_bias", (32,)),
        ("ssd_a_log", (32,)), ("ssd_d", (32,)), ("ssd_norm_w", (2048,)), ("ln1_g", (1024,)), ("ln1_b", (1024,)),
        ("ln2_g", (1024,)), ("ln2_b", (1024,)))
_ORDER = ("w_in", "b_gate", "lru_conv_w", "lru_conv_b", "lru_w_a", "lru_b_a", "lru_w_i", "lru_b_i", "lru_lambda",
          "ssd_conv_w", "ssd_conv_b", "ssd_dt_bias", "ssd_a_log", "ssd_d", "ssd_norm_w", "mem_w_kv", "w_br_lru",
          "w_br_ssd", "w_br_xa", "w_out", "ln1_g", "ln1_b", "ffn_w_in", "ffn_w_down", "ln2_g", "ln2_b")

LANES = 1024
N_SMALL = sum(DEPTH * s[0] * s[1] for _, s in _SMALL)
R_SMALL = 8
N_REP = sum(DEPTH * math.prod(s) for _, s in _REP)
R_REP = 68
R_SM = R_SMALL + R_REP + 4
R_TAIL = R_SMALL + N_DEV * R_REP
TB_TAIL = 184
assert N_SMALL <= R_SMALL * LANES and N_REP <= N_DEV * R_REP * LANES


def _rows(flat, rows):
    return jnp.pad(flat, (0, rows * LANES - flat.shape[0])).reshape(rows, LANES)


def _rowblk(a, cap):
    return max(b for b in range(16, cap + 1, 16) if a % b == 0)


def _pack_tail(d):
    small = jnp.concatenate([d[n].reshape(-1) for n, _ in _SMALL])
    rep = jnp.concatenate([d[n].reshape(-1) for n, _ in _REP])
    return jnp.concatenate([_rows(small, R_SMALL), _rows(rep, N_DEV * R_REP)], axis=0)


def _unpack_tail(a):
    out, o = {}, 0
    flat = a[:R_SMALL].reshape(-1)
    for n, s in _SMALL:
        k = DEPTH * math.prod(s)
        out[n] = flat[o:o + k].reshape((DEPTH,) + s)
        o += k
    flat, o = a[R_SMALL:].reshape(-1), 0
    for n, s in _REP:
        k = DEPTH * math.prod(s)
        out[n] = flat[o:o + k].reshape((DEPTH,) + s)
        o += k
    return out


def _by_dest(g):
    g = g.reshape(g.shape[:-1] + (N_DEV, g.shape[-1] // N_DEV))
    return jnp.moveaxis(g, -2, 0).reshape(N_DEV, -1)


def _from_stack(st):
    st = jnp.moveaxis(st, 0, -2)
    return st.reshape(st.shape[:-2] + (st.shape[-2] * st.shape[-1],))


def _layer_fwd(x, mem, w, l):
    nm = lambda s: f"{s}_l{l}"
    wi = w["wi"]
    row = lambda v: v.reshape(1, -1)
    s = dict(x=x, wi=wi)
    s["p_ssd"] = _mm(x, wi["ssd"], name=nm("proj_ssd"))
    s["p_lru"] = _mm(x, wi["lru"], name=nm("proj_lru"))
    s["p_q"] = _mm(x, wi["q"], name=nm("proj_q"))
    s["p_gl"] = _mm(x, wi["gl"], name=nm("proj_gl"))
    s["lru_par"] = (w["lru_conv_w"], row(w["lru_conv_b"]), w["lru_w_a"], row(w["lru_b_a"]), w["lru_w_i"],
                    row(w["lru_b_i"]), row(w["lru_lambda"]))
    s["y_lru"], s["h"], s["xc"] = _lru_fwd(s["p_lru"], *s["lru_par"], name=nm("lru_fwd"))
    s["ssd_par"] = (w["ssd_conv_w"], row(w["ssd_conv_b"]), row(w["ssd_dt_bias"]), row(w["ssd_a_log"]),
                    row(w["ssd_d"]), row(w["ssd_norm_w"]))
    s["y_ssd"], s["yraw"], s["hs"], s["c_ssd"] = _ssd_fwd(s["p_ssd"], *s["ssd_par"], name=nm("ssd_fwd"))
    s["kv"] = _mm(mem, w["mem_w_kv"], name=nm("kv"))
    s["y_xa"] = _xa_fwd(s["p_q"], s["kv"], name=nm("xa_fwd"))
    s["b1"] = _mm(s["y_lru"], w["w_br_lru"], name=nm("br_lru"))
    s["b2"] = _mm(s["y_ssd"], w["w_br_ssd"], name=nm("br_ssd"))
    s["b3"] = _mm(s["y_xa"], w["w_br_xa"], name=nm("br_xa"))
    s["bg"] = row(w["b_gate"])
    s["merged"] = _merge_fwd(s["p_gl"], s["bg"], s["b1"], s["b2"], s["b3"], name=nm("merge_fwd"))
    s["mix"] = _mm(s["merged"], w["w_out"], name=nm("out_proj"))
    s["x1"] = _ln_fwd(x, s["mix"], row(w["ln1_g"]), row(w["ln1_b"]), name=nm("ln1_fwd"))
    s["gate"], s["up"], s["act"] = _ffn_in_swiglu(s["x1"], w["ffn_w_in"], name=nm("ffn_in"))
    s["f"] = _mm(s["act"], w["ffn_w_down"], name=nm("ffn_down"))
    s["x2"] = _ln_fwd(s["x1"], s["f"], row(w["ln2_g"]), row(w["ln2_b"]), name=nm("ln2_fwd"))
    return s


def _layer_bwd(s, mem, w, dxo, l, hooks=None):
    nm = lambda t: f"{t}_l{l}"
    g = {}
    hook = lambda stage, t: hooks[stage](t, g) if hooks and stage in hooks else t
    row = lambda v: v.reshape(1, -1)
    slabs = lambda a: a.reshape(N_DEV, a.shape[0] // N_DEV, a.shape[1])
    du2, dg, db = _ln_bwd(s["x1"], s["f"], dxo, row(w["ln2_g"]), name=nm("ln2_bwd"))
    g["ln2_g"], g["ln2_b"] = dg[0], db[0]
    dgate, dup = _d_swiglu(du2, w["ffn_w_down"], s["gate"], s["up"], name=nm("d_swiglu"))
    g["ffn_w_down"] = slabs(_mm(s["act"], du2, ta=True, name=nm("dw_ffn_down")))
    dx1 = _mm(dgate, w["ffn_w_in"][:, :D_FF], tb=True, add=du2, add_scale=ALPHA, name=nm("d_x1_gate"))
    dx1 = _mm(dup, w["ffn_w_in"][:, D_FF:], tb=True, add=dx1, name=nm("d_x1_up"))
    g["ffn_w_in"] = _cols_split([_mm(s["x1"], dgate, ta=True, name=nm("dw_ffn_gate")),
                                 _mm(s["x1"], dup, ta=True, name=nm("dw_ffn_up"))], name=nm("dw_ffn_in_shards"))
    du1, dg, db = _ln_bwd(s["x"], s["mix"], dx1, row(w["ln1_g"]), name=nm("ln1_bwd"))
    g["ln1_g"], g["ln1_b"] = dg[0], db[0]
    dmerged = hook("mid", _mm(du1, w["w_out"], tb=True, name=nm("d_merged")))
    g["w_out"] = slabs(_mm(s["merged"], du1, ta=True, name=nm("dw_out")))
    dp_gl, d1, d2, d3, dbg = _merge_bwd(s["p_gl"], s["bg"], s["b1"], s["b2"], s["b3"], dmerged, name=nm("merge_bwd"))
    g["b_gate"] = dbg.reshape(3, D)
    dy_lru = _mm(d1, w["w_br_lru"], tb=True, name=nm("d_y_lru"))
    g["w_br_lru"] = slabs(_mm(s["y_lru"], d1, ta=True, name=nm("dw_br_lru")))
    dy_ssd = _mm(d2, w["w_br_ssd"], tb=True, name=nm("d_y_ssd"))
    g["w_br_ssd"] = slabs(_mm(s["y_ssd"], d2, ta=True, name=nm("dw_br_ssd")))
    dy_xa = _mm(d3, w["w_br_xa"], tb=True, name=nm("d_y_xa"))
    g["w_br_xa"] = slabs(_mm(s["y_xa"], d3, ta=True, name=nm("dw_br_xa")))
    dp_q, dkv = _xa_bwd(s["p_q"], s["kv"], dy_xa, name=nm("xa_bwd"))
    g["mem_w_kv"] = _mm(mem, dkv, ta=True, split_n=2 * D // N_DEV, name=nm("dw_kv"))
    dy_ssd = hook("branches", dy_ssd)
    ssd_cw, _, *ssd_rest = s["ssd_par"]
    dp_ssd, dcw, dcb, ddtb, dalog, dd, dnw = _ssd_bwd(s["p_ssd"], s["c_ssd"], s["yraw"], s["hs"], dy_ssd, ssd_cw,
                                                      *ssd_rest, name=nm("ssd_bwd"))
    g["ssd_conv_w"], g["ssd_conv_b"], g["ssd_dt_bias"] = dcw, dcb[0], ddtb[0]
    g["ssd_a_log"], g["ssd_d"], g["ssd_norm_w"] = dalog[0], dd[0], dnw[0]
    dp_ssd = hook("ssd", dp_ssd)
    lru_cw, _, *lru_rest = s["lru_par"]
    dp_lru, dcw, dcb, dwa, dba, dwi, dbi, dlam = _lru_bwd(s["p_lru"], s["xc"], s["h"], dy_lru, lru_cw, *lru_rest,
                                                          name=nm("lru_bwd"))
    g["lru_conv_w"], g["lru_conv_b"], g["lru_w_a"], g["lru_b_a"] = dcw, dcb[0], dwa, dba[0]
    g["lru_w_i"], g["lru_b_i"], g["lru_lambda"] = dwi, dbi[0], dlam[0]
    wi, x = s["wi"], s["x"]
    g["w_in"] = _w_in_shards(_mm(x, dp_ssd, ta=True, name=nm("dw_in_ssd")), _mm(x, dp_lru, ta=True, name=nm("dw_in_lru")),
                             _mm(x, dp_q, ta=True, name=nm("dw_in_q")), _mm(x, dp_gl, ta=True, name=nm("dw_in_gl")),
                             name=nm("dw_in_shards"))
    dp_ssd = hook("weights", dp_ssd)
    dx = _mm(dp_ssd, wi["ssd"], tb=True, add=du1, add_scale=ALPHA, name=nm("dx_ssd"))
    dx = hook("dx", _mm(dp_lru, wi["lru"], tb=True, add=dx, name=nm("dx_lru")))
    dx = _mm(dp_q, wi["q"], tb=True, add=dx, name=nm("dx_q"))
    dx = _mm(dp_gl, wi["gl"], tb=True, add=dx, name=nm("dx_gl"))
    return dx, g


def _local_step(x, mem, target, layers, hooks=None):
    saved = []
    for l in range(DEPTH):
        saved.append(_layer_fwd(x, mem, layers[l], l))
        x = saved[-1]["x2"]
    dx, loss = _loss_head(x, target, name="loss_head")
    grads = [None] * DEPTH
    for l in reversed(range(DEPTH)):
        dx, grads[l] = _layer_bwd(saved[l], mem, layers[l], dx, l, hooks[l] if hooks else None)
    return loss, dx, grads


def kernel(x, mem, w_in, b_gate, lru_conv_w, lru_conv_b, lru_w_a, lru_b_a, lru_w_i, lru_b_i, lru_lambda, ssd_conv_w, ssd_conv_b, ssd_dt_bias, ssd_a_log, ssd_d, ssd_norm_w, mem_w_kv, w_br_lru, w_br_ssd, w_br_xa, w_out, ln1_g, ln1_b, ffn_w_in, ffn_w_down, ln2_g, ln2_b, loss_target, m_w_in, m_b_gate, m_lru_conv_w, m_lru_conv_b, m_lru_w_a, m_lru_b_a, m_lru_w_i, m_lru_b_i, m_lru_lambda, m_ssd_conv_w, m_ssd_conv_b, m_ssd_dt_bias, m_ssd_a_log, m_ssd_d, m_ssd_norm_w, m_mem_w_kv, m_w_br_lru, m_w_br_ssd, m_w_br_xa, m_w_out, m_ln1_g, m_ln1_b, m_ffn_w_in, m_ffn_w_down, m_ln2_g, m_ln2_b, v_w_in, v_b_gate, v_lru_conv_w, v_lru_conv_b, v_lru_w_a, v_lru_b_a, v_lru_w_i, v_lru_b_i, v_lru_lambda, v_ssd_conv_w, v_ssd_conv_b, v_ssd_dt_bias, v_ssd_a_log, v_ssd_d, v_ssd_norm_w, v_mem_w_kv, v_w_br_lru, v_w_br_ssd, v_w_br_xa, v_w_out, v_ln1_g, v_ln1_b, v_ffn_w_in, v_ffn_w_down, v_ln2_g, v_ln2_b):
    local = dict(locals())
    w = {n: local[n] for n in _ORDER}
    m = {n: local["m_" + n] for n in _ORDER}
    v = {n: local["v_" + n] for n in _ORDER}

    big = [n for n, _, _ in _BIG]
    kinds = {n: kind for n, kind, _ in _BIG}

    small = _rows(jnp.concatenate([w[n].reshape(-1) for n, _ in _SMALL]), R_SMALL)
    first = _all_gather([w["w_in"][0].astype(BF16), small], name="gather_w_in_l0")
    rest, later, _ = lax.optimization_barrier(([w[n][0].astype(BF16) for n in big[1:]],
                                               [w[n][1].astype(BF16) for n in big], first[-1]))
    rest = _all_gather(rest, "gather_weights_l0", collective_id=1)
    later = _all_gather(later, "gather_weights_l1", collective_id=4)
    stacks = [dict(zip(big, [first[0], *rest])), dict(zip(big, later))]
    small_all, o, small_full = first[-1].reshape(N_DEV, R_SMALL * LANES), 0, {}
    for n, s in _SMALL:
        k = DEPTH * s[0] * s[1]
        small_full[n] = _from_stack(small_all[:, o:o + k].reshape((N_DEV, DEPTH) + s))
        o += k
    layers = []
    for l in range(DEPTH):
        lw = {n: w[n][l] for n, _ in _REP}
        lw.update({n: small_full[n][l] for n, _ in _SMALL})
        lw["wi"] = dict(zip(("ssd", "lru", "q", "gl"), _w_in_regions(stacks[l]["w_in"], name=f"w_in_regions_l{l}")))
        for n in big[1:]:
            if kinds[n] == "col":
                lw[n] = _cols_concat(stacks[l][n], name=f"full_{n}_l{l}")
            else:
                lw[n] = stacks[l][n].reshape(-1, stacks[l][n].shape[-1])
        layers.append(lw)

    by_dest = lambda a: a.reshape((4, 2) + a.shape[1:])
    slots, pending, last_layer = {}, {}, {}
    queue = [stacks[1]["w_out"]]

    def after_last(operands):
        operands, _ = lax.optimization_barrier((list(operands), queue[-1]))
        return operands

    def start(tag, collective_id, names_and_grads):
        names, owns = zip(*names_and_grads)
        gots = _exchange_sibling(after_last(owns), name=f"reduce_cores_{tag}", collective_id=collective_id)
        queue.append(gots[0])
        pending[tag] = (names, owns, gots)

    def finish(tag, collective_id, t):
        names, owns, gots = pending.pop(tag)
        t, gots = lax.optimization_barrier((t, gots))
        sums = [_pair_sum(own, got, name=f"pair_sum_{tag}_{n}", out_dtype=F32 if n == "tail" else BF16,
                          tb=R_SM if n == "tail" else _rowblk(own.shape[2], 256))
                for n, own, got in zip(names, owns, gots)]
        t, sums = lax.optimization_barrier((t, sums))
        got = _exchange_chips(sums, name=f"reduce_chips_{tag}", collective_id=collective_id)
        queue.append(got[0])
        slots.update({(tag, n): s for n, s in zip(names, got)})
        return t

    def tail_of(g0):
        stacked = {n: jnp.stack([g0[n], last_layer[n]]) for n in [s[0] for s in _SMALL + _REP]}
        sm = jnp.concatenate([_by_dest(stacked[n]) for n, _ in _SMALL], axis=1)
        sm = jnp.pad(sm, ((0, 0), (0, R_SMALL * LANES - sm.shape[1])))
        rep = jnp.concatenate([stacked[n].reshape(-1) for n, _ in _REP])
        rep = jnp.pad(rep, (0, N_DEV * R_REP * LANES - rep.shape[0])).reshape(N_DEV, R_REP * LANES)
        tail = jnp.concatenate([sm, rep, jnp.zeros((N_DEV, (R_SM - R_SMALL - R_REP) * LANES), F32)], axis=1)
        return tail.reshape(4, 2, R_SM, LANES)

    def weights_l1(t, g):
        last_layer.update(g)
        start("l1", 2, [(n, by_dest(g[n])) for n in big])
        return t

    def branches_l0(t, g):
        start("l0a", 5, [(n, by_dest(g[n])) for n in big[1:]])
        return t

    def weights_l0(t, g):
        start("l0b", 7, [("w_in", by_dest(g["w_in"])), ("tail", tail_of(g))])
        return t

    hooks = [{"branches": branches_l0, "ssd": lambda t, g: finish("l0a", 6, t), "weights": weights_l0,
              "dx": lambda t, g: finish("l0b", 8, t)},
             {"weights": weights_l1, "dx": lambda t, g: finish("l1", 3, t)}]
    loss_tile, dx, grads = _local_step(x[0], mem[0], loss_target[0], layers, hooks)
    loss = lax.psum(loss_tile[0, 0], ("x", "y", "c"))

    res = {}
    for n in big:
        tb = _rowblk(w[n].shape[1], 128 if w[n].shape[2] > LANES else 256)
        res[n] = _adamw([slots["l0b" if n == "w_in" else "l0a", n], slots["l1", n]], w[n], m[n], v[n],
                        name=f"adamw_{n}", tb=tb)
    tail_sum = _slot_sum(slots["l0b", "tail"], name="sum_tail")
    rep_all = _all_gather([tail_sum[R_SMALL:R_SMALL + R_REP]], name="gather_replicated")[0]
    g_tail = jnp.concatenate([tail_sum[:R_SMALL], rep_all.reshape(N_DEV * R_REP, LANES)], axis=0)
    tails = _adamw([g_tail[None]], _pack_tail(w)[None], _pack_tail(m)[None], _pack_tail(v)[None],
                   name="adamw_tail", tb=TB_TAIL)

    outs = []
    for kind in range(4):
        d = {**{n: res[n][kind] for n in big}, **_unpack_tail(tails[kind][0])}
        outs += [d[n] for n in _ORDER]
    return (loss, dx[None], *outs)
```

```python
import math

import jax
import jax.numpy as jnp
from jax import lax
from jax.experimental import pallas as pl
from jax.experimental.pallas import tpu as pltpu
from jax.experimental.pallas import tpu_sc as plsc

F32 = jnp.float32
BF16 = jnp.bfloat16

D = 1024
DEPTH = 2
N_DEV = 8
CHUNK = 64
LRU_BLOCKS = 8
LRU_BLOCK = 128
LRU_C = 8.0
D_SSD = 2 * D
SSD_HEADS = 32
SSD_GROUPS = 4
GROUP_W = D_SSD // SSD_GROUPS
SSD_STATE = 128
D_XBC = D_SSD + 2 * SSD_GROUPS * SSD_STATE
XA_HEADS = 4
XA_HEAD_DIM = 256
D_FF = 2816
ALPHA = (2 * DEPTH) ** 0.25
EPS = 1e-5
N_IN = 11296

S_Z, S_XBC, S_DT, W_SSD = 0, 2048, 5120, 5632
DT_REAL = 32

ADAM_LR, ADAM_B1, ADAM_B2, ADAM_EPS, ADAM_WD, ADAM_STEP = 0.001, 0.9, 0.999, 1e-08, 0.01, 10

VMEM_LIMIT = 56 * 1024 * 1024
MESH = pl.DeviceIdType.MESH
ANY = pl.BlockSpec(memory_space=pl.ANY)


def _cp(*sem):
    return pltpu.CompilerParams(dimension_semantics=sem, vmem_limit_bytes=VMEM_LIMIT)


def _blk(n, target):
    if n % 128:
        return n
    best = 128
    for b in range(128, min(n, target) + 1, 128):
        if n % b == 0:
            best = b
    return best


def _iota(shape, dim):
    return lax.broadcasted_iota(jnp.int32, shape, dim)


def _sigmoid(x):
    return 0.5 + 0.5 * jnp.tanh(0.5 * x)


def _log1p(e):
    u = 1.0 + e
    return jnp.where(u == 1.0, e, jnp.log(u) * (e / (u - 1.0)))


def _softplus(x):
    return jnp.maximum(x, 0.0) + _log1p(jnp.exp(-jnp.abs(x)))


_G0 = math.sqrt(2.0 / math.pi)
_G1 = 0.044715


def _gelu_and_grad(x):
    x2 = x * x
    u = 0.5 + 0.5 * jnp.tanh(x * (_G0 + (_G0 * _G1) * x2))
    dg = u + (x * (u * (1.0 - u))) * ((2.0 * _G0) + (6.0 * _G0 * _G1) * x2)
    return x * u, dg


_NN = (((1,), (0,)), ((), ()))
_NT = (((1,), (1,)), ((), ()))
_TN = (((0,), (0,)), ((), ()))


def _dot(a, b, dims=_NN):
    return lax.dot_general(a.astype(BF16), b.astype(BF16), dims, preferred_element_type=F32)


def _dot_hi(a, b, dims=_NN):
    return lax.dot_general(a, b, dims, precision=lax.Precision.HIGHEST, preferred_element_type=F32)


def _split(v):
    hi = v.astype(BF16)
    return hi, (v - hi.astype(F32)).astype(BF16)


def _dot01(v, e, dims=_NN):
    hi, lo = _split(v)
    return (lax.dot_general(hi, e, dims, preferred_element_type=F32)
            + lax.dot_general(lo, e, dims, preferred_element_type=F32))


def _conv_taps(xe, n):
    return [xe[8:8 + n] if j == 3 else pltpu.roll(xe, 3 - j, 0)[8:8 + n] for j in range(4)]


def _conv_fwd(taps, cw, cb):
    return cb + cw[0:1] * taps[0] + cw[1:2] * taps[1] + cw[2:3] * taps[2] + cw[3:4] * taps[3]


def _conv_bwd(dc, dnext, x, cw, n):
    ext = jnp.concatenate([dc, dnext], axis=0)
    shifted = [pltpu.roll(ext, n + 8 - (3 - j), 0)[0:n] for j in range(3)] + [dc]
    dx = cw[0:1] * shifted[0] + cw[1:2] * shifted[1] + cw[2:3] * shifted[2] + cw[3:4] * dc
    dcw = jnp.concatenate([jnp.sum(x * shifted[j], axis=0, keepdims=True) for j in range(4)], axis=0)
    return dx, dcw, jnp.sum(dc, axis=0, keepdims=True)


MM_VMEM_BUDGET = 44 * 1024 * 1024
MM_MAX_TILE = 1408
MM_MAX_K = 5632


def _divisors(n, cap):
    return [n] if n % 128 else [b for b in range(128, min(n, cap) + 1, 128) if n % b == 0]


def _mm_tiles(m_dim, n_dim, k_dim, a_bytes, b_bytes, o_bytes, has_add, tn_fixed):
    best = None
    for tm in _divisors(m_dim, MM_MAX_TILE):
        for tn in ([tn_fixed] if tn_fixed else _divisors(n_dim, MM_MAX_TILE)):
            for tk in _divisors(k_dim, MM_MAX_K):
                vmem = 2 * (tm * tk * a_bytes + tk * tn * b_bytes + tm * tn * (o_bytes + (4 if has_add else 0)))
                vmem += tm * tn * 4 if tk < k_dim else 0
                if vmem <= MM_VMEM_BUDGET:
                    key = (tm * tn * tk, tk, tn)
                    if best is None or key > best[0]:
                        best = (key, (tm, tn, tk))
    assert best is not None, (m_dim, n_dim, k_dim)
    return best[1]


def _mm(a, b, *, ta=False, tb=False, out_dtype=F32, add=None, add_scale=1.0, name, split_n=None):
    if ta:
        k_dim, m_dim = a.shape
    else:
        m_dim, k_dim = a.shape
    if tb:
        n_dim, k2 = b.shape
    else:
        k2, n_dim = b.shape
    assert k_dim == k2, (a.shape, b.shape, ta, tb)
    tm, tn, tk = _mm_tiles(m_dim, n_dim, k_dim, a.dtype.itemsize, b.dtype.itemsize, jnp.dtype(out_dtype).itemsize,
                           add is not None, split_n)
    nk = k_dim // tk
    a_spec = pl.BlockSpec((tk, tm), lambda i, j, k: (k, i)) if ta else pl.BlockSpec((tm, tk), lambda i, j, k: (i, k))
    b_spec = pl.BlockSpec((tn, tk), lambda i, j, k: (j, k)) if tb else pl.BlockSpec((tk, tn), lambda i, j, k: (k, j))
    o_spec = pl.BlockSpec((tm, tn), lambda i, j, k: (i, j))
    out_shape = (m_dim, n_dim)
    if split_n is not None:
        assert add is None and tn == split_n, (tn, split_n)
        o_spec = pl.BlockSpec((None, tm, tn), lambda i, j, k: (j, i, 0))
        out_shape = (n_dim // tn, m_dim, tn)
    dims = (((0 if ta else 1,), (1 if tb else 0,)), ((), ()))
    has_add = add is not None

    def body(*refs):
        a_ref, b_ref = refs[:2]
        add_ref = refs[2] if has_add else None
        o_ref = refs[3] if has_add else refs[2]
        acc_ref = refs[-1] if nk > 1 else None
        k = pl.program_id(2)

        def product():
            return lax.dot_general(a_ref[...].astype(BF16), b_ref[...].astype(BF16), dims, preferred_element_type=F32)

        def finish(r):
            if has_add:
                r = r + add_scale * add_ref[...]
            o_ref[...] = r.astype(out_dtype)

        if nk == 1:
            finish(product())
            return

        @pl.when(k == 0)
        def _():
            acc_ref[...] = product()

        @pl.when((k > 0) & (k < nk - 1))
        def _():
            acc_ref[...] += product()

        @pl.when(k == nk - 1)
        def _():
            finish(acc_ref[...] + product())

    in_specs = [a_spec, b_spec] + ([o_spec] if has_add else [])
    args = (a, b) + ((add,) if has_add else ())
    return pl.pallas_call(
        body, name=name, grid=(m_dim // tm, n_dim // tn, nk),
        in_specs=in_specs, out_specs=o_spec,
        out_shape=jax.ShapeDtypeStruct(out_shape, out_dtype),
        scratch_shapes=[pltpu.VMEM((tm, tn), F32)] if nk > 1 else [],
        cost_estimate=pl.CostEstimate(
            flops=2 * m_dim * n_dim * k_dim, transcendentals=0,
            bytes_accessed=a.size * a.dtype.itemsize + b.size * b.dtype.itemsize
            + m_dim * n_dim * (jnp.dtype(out_dtype).itemsize + (4 if has_add else 0))),
        compiler_params=_cp("parallel", "parallel", "arbitrary"),
    )(*args)


def _ln_fwd(x, f, g, b, name, tb=512):
    t = x.shape[0]
    tb = min(tb, t)

    def body(x_ref, f_ref, g_ref, b_ref, o_ref):
        u = ALPHA * x_ref[...] + f_ref[...]
        mu = jnp.mean(u, axis=-1, keepdims=True)
        d = u - mu
        var = jnp.mean(d * d, axis=-1, keepdims=True)
        o_ref[...] = d * lax.rsqrt(var + EPS) * g_ref[...] + b_ref[...]

    row = pl.BlockSpec((tb, D), lambda i: (i, 0))
    par = pl.BlockSpec((1, D), lambda i: (0, 0))
    return pl.pallas_call(
        body, name=name, grid=(t // tb,), in_specs=[row, row, par, par], out_specs=row,
        out_shape=jax.ShapeDtypeStruct((t, D), F32), compiler_params=_cp("parallel"),
    )(x, f, g, b)


def _ln_bwd(x, f, dy, g, name, tb=512):
    t = x.shape[0]
    tb = min(tb, t)

    def body(x_ref, f_ref, dy_ref, g_ref, du_ref, dg_ref, db_ref):
        @pl.when(pl.program_id(0) == 0)
        def _():
            dg_ref[...] = jnp.zeros_like(dg_ref)
            db_ref[...] = jnp.zeros_like(db_ref)

        u = ALPHA * x_ref[...] + f_ref[...]
        mu = jnp.mean(u, axis=-1, keepdims=True)
        d = u - mu
        var = jnp.mean(d * d, axis=-1, keepdims=True)
        rstd = lax.rsqrt(var + EPS)
        xhat = d * rstd
        dy = dy_ref[...]
        dxh = dy * g_ref[...]
        m1 = jnp.mean(dxh, axis=-1, keepdims=True)
        m2 = jnp.mean(dxh * xhat, axis=-1, keepdims=True)
        du_ref[...] = rstd * (dxh - m1 - xhat * m2)
        dg_ref[...] += jnp.sum(dy * xhat, axis=0, keepdims=True)
        db_ref[...] += jnp.sum(dy, axis=0, keepdims=True)

    row = pl.BlockSpec((tb, D), lambda i: (i, 0))
    par = pl.BlockSpec((1, D), lambda i: (0, 0))
    return pl.pallas_call(
        body, name=name, grid=(t // tb,), in_specs=[row, row, row, par], out_specs=[row, par, par],
        out_shape=[jax.ShapeDtypeStruct((t, D), F32), jax.ShapeDtypeStruct((1, D), F32),
                   jax.ShapeDtypeStruct((1, D), F32)],
        compiler_params=_cp("arbitrary"),
    )(x, f, dy, g)


FFN_TM, FFN_TN = 512, D_FF // 2


def _ffn_in_swiglu(x, w, name):
    t = x.shape[0]
    tm = min(FFN_TM, t)
    nj = D_FF // FFN_TN

    def body(x_ref, wg_ref, wu_ref, g_ref, u_ref, a_ref):
        xb = x_ref[...].astype(BF16)
        g = lax.dot_general(xb, wg_ref[...], _NN, preferred_element_type=F32)
        u = lax.dot_general(xb, wu_ref[...], _NN, preferred_element_type=F32)
        g_ref[...] = g
        u_ref[...] = u
        a_ref[...] = (g * _sigmoid(g) * u).astype(BF16)

    tile = pl.BlockSpec((tm, FFN_TN), lambda i, j: (i, j))
    return pl.pallas_call(
        body, name=name, grid=(t // tm, nj),
        in_specs=[pl.BlockSpec((tm, D), lambda i, j: (i, 0)), pl.BlockSpec((D, FFN_TN), lambda i, j: (0, j)),
                  pl.BlockSpec((D, FFN_TN), lambda i, j: (0, nj + j))],
        out_specs=[tile, tile, tile],
        out_shape=[jax.ShapeDtypeStruct((t, D_FF), F32), jax.ShapeDtypeStruct((t, D_FF), F32),
                   jax.ShapeDtypeStruct((t, D_FF), BF16)],
        compiler_params=_cp("parallel", "parallel"),
    )(x, w, w)


def _d_swiglu(du, w_down, g, u, name):
    t = du.shape[0]
    tm = min(FFN_TM, t)

    def body(du_ref, w_ref, g_ref, u_ref, dg_ref, dup_ref):
        da = lax.dot_general(du_ref[...].astype(BF16), w_ref[...], _NT, preferred_element_type=F32)
        g_v = g_ref[...]
        s = _sigmoid(g_v)
        dg_ref[...] = (da * u_ref[...] * (s * (1.0 + g_v * (1.0 - s)))).astype(BF16)
        dup_ref[...] = (da * g_v * s).astype(BF16)

    tile = pl.BlockSpec((tm, FFN_TN), lambda i, j: (i, j))
    return pl.pallas_call(
        body, name=name, grid=(t // tm, D_FF // FFN_TN),
        in_specs=[pl.BlockSpec((tm, D), lambda i, j: (i, 0)), pl.BlockSpec((FFN_TN, D), lambda i, j: (j, 0)), tile, tile],
        out_specs=[tile, tile],
        out_shape=[jax.ShapeDtypeStruct((t, D_FF), BF16), jax.ShapeDtypeStruct((t, D_FF), BF16)],
        compiler_params=_cp("parallel", "parallel"),
    )(du, w_down, g, u)


def _merge_fwd(pgl, bg, b1, b2, b3, name, tb=512):
    t = pgl.shape[0]
    tb = min(tb, t)

    def body(gl_ref, bg_ref, b1_ref, b2_ref, b3_ref, o_ref):
        acc = None
        for j, b_ref in enumerate((b1_ref, b2_ref, b3_ref)):
            sl = slice(j * D, (j + 1) * D)
            term = _sigmoid(gl_ref[:, sl] + bg_ref[:, sl]) * b_ref[...]
            acc = term if acc is None else acc + term
        o_ref[...] = acc.astype(BF16)

    row = pl.BlockSpec((tb, D), lambda i: (i, 0))
    return pl.pallas_call(
        body, name=name, grid=(t // tb,),
        in_specs=[pl.BlockSpec((tb, 3 * D), lambda i: (i, 0)), pl.BlockSpec((1, 3 * D), lambda i: (0, 0)), row, row, row],
        out_specs=row, out_shape=jax.ShapeDtypeStruct((t, D), BF16), compiler_params=_cp("parallel"),
    )(pgl, bg, b1, b2, b3)


def _merge_bwd(pgl, bg, b1, b2, b3, dm, name, tb=512):
    t = pgl.shape[0]
    tb = min(tb, t)

    def body(gl_ref, bg_ref, b1_ref, b2_ref, b3_ref, dm_ref, dgl_ref, d1_ref, d2_ref, d3_ref, dbg_ref):
        @pl.when(pl.program_id(0) == 0)
        def _():
            dbg_ref[...] = jnp.zeros_like(dbg_ref)

        dm_v = dm_ref[...]
        for j, (b_ref, d_ref) in enumerate(((b1_ref, d1_ref), (b2_ref, d2_ref), (b3_ref, d3_ref))):
            sl = slice(j * D, (j + 1) * D)
            gate = _sigmoid(gl_ref[:, sl] + bg_ref[:, sl])
            d_ref[...] = (dm_v * gate).astype(BF16)
            dgl = dm_v * b_ref[...] * (gate * (1.0 - gate))
            dgl_ref[:, sl] = dgl.astype(BF16)
            dbg_ref[:, sl] += jnp.sum(dgl, axis=0, keepdims=True)

    row = pl.BlockSpec((tb, D), lambda i: (i, 0))
    wide = pl.BlockSpec((tb, 3 * D), lambda i: (i, 0))
    par = pl.BlockSpec((1, 3 * D), lambda i: (0, 0))
    return pl.pallas_call(
        body, name=name, grid=(t // tb,),
        in_specs=[wide, par, row, row, row, row], out_specs=[wide, row, row, row, par],
        out_shape=[jax.ShapeDtypeStruct((t, 3 * D), BF16)] + [jax.ShapeDtypeStruct((t, D), BF16)] * 3
                  + [jax.ShapeDtypeStruct((1, 3 * D), F32)],
        compiler_params=_cp("arbitrary"),
    )(pgl, bg, b1, b2, b3, dm)


def _xa_probs(q, kv_ref, hd):
    sl = slice(hd * XA_HEAD_DIM, (hd + 1) * XA_HEAD_DIM)
    k = kv_ref[:, sl]
    v = kv_ref[:, D + hd * XA_HEAD_DIM:D + (hd + 1) * XA_HEAD_DIM]
    s = _dot(q[:, sl], k, _NT) * (XA_HEAD_DIM ** -0.5)
    e = jnp.exp(s - jnp.max(s, axis=1, keepdims=True))
    return sl, k, v, e / jnp.sum(e, axis=1, keepdims=True)


def _xa_fwd(pq, kv, name, tb=512):
    t = pq.shape[0]
    tb = min(tb, t)

    def body(q_ref, kv_ref, o_ref):
        q = q_ref[...]
        for hd in range(XA_HEADS):
            sl, _, v, p = _xa_probs(q, kv_ref, hd)
            o_ref[:, sl] = _dot(p, v).astype(BF16)

    row = pl.BlockSpec((tb, D), lambda i: (i, 0))
    return pl.pallas_call(
        body, name=name, grid=(t // tb,),
        in_specs=[row, pl.BlockSpec(kv.shape, lambda i: (0, 0))], out_specs=row,
        out_shape=jax.ShapeDtypeStruct((t, D), BF16), compiler_params=_cp("parallel"),
    )(pq, kv)


def _xa_bwd(pq, kv, dy, name, tb=512):
    t = pq.shape[0]
    tb = min(tb, t)

    def body(q_ref, kv_ref, dy_ref, dq_ref, dkv_ref):
        @pl.when(pl.program_id(0) == 0)
        def _():
            dkv_ref[...] = jnp.zeros_like(dkv_ref)

        q = q_ref[...]
        for hd in range(XA_HEADS):
            sl, k, v, p = _xa_probs(q, kv_ref, hd)
            dyh = dy_ref[:, sl]
            vsl = slice(D + hd * XA_HEAD_DIM, D + (hd + 1) * XA_HEAD_DIM)
            dkv_ref[:, vsl] += _dot(p, dyh, _TN)
            dp = _dot(dyh, v, _NT)
            ds = p * (dp - jnp.sum(dp * p, axis=1, keepdims=True)) * (XA_HEAD_DIM ** -0.5)
            dq_ref[:, sl] = _dot(ds, k).astype(BF16)
            dkv_ref[:, sl] += _dot(ds, q[:, sl], _TN)

    row = pl.BlockSpec((tb, D), lambda i: (i, 0))
    kvs = pl.BlockSpec(kv.shape, lambda i: (0, 0))
    return pl.pallas_call(
        body, name=name, grid=(t // tb,), in_specs=[row, kvs, row], out_specs=[row, kvs],
        out_shape=[jax.ShapeDtypeStruct((t, D), BF16), jax.ShapeDtypeStruct(kv.shape, F32)],
        compiler_params=_cp("arbitrary"),
    )(pq, kv, dy)


SUBLANES = 8


def _scan(a, u, reverse):
    n, c = a.shape
    groups = n // SUBLANES
    a = a.reshape(groups, SUBLANES, c)
    u = u.reshape(groups, SUBLANES, c)
    sub = _iota((1, SUBLANES, 1), 1)
    d = 1
    while d < SUBLANES:
        keep = (sub < SUBLANES - d) if reverse else (sub >= d)
        shift = SUBLANES - d if reverse else d
        u = a * jnp.where(keep, pltpu.roll(u, shift, 1), 0.0) + u
        a = a * jnp.where(keep, pltpu.roll(a, shift, 1), 1.0)
        d *= 2
    edge = 0 if reverse else SUBLANES - 1
    out, carry = [None] * groups, None
    for j in (reversed(range(groups)) if reverse else range(groups)):
        out[j] = u[j] if carry is None else u[j] + a[j] * carry
        carry = out[j][edge:edge + 1]
    return jnp.concatenate(out, axis=0)


def _lru_gates(xc, wa_ref, ba, wi_ref, bi, lam):
    za = jnp.concatenate([_dot(xc[:, n * 128:(n + 1) * 128], wa_ref[n]) for n in range(LRU_BLOCKS)], axis=1) + ba
    zi = jnp.concatenate([_dot(xc[:, n * 128:(n + 1) * 128], wi_ref[n]) for n in range(LRU_BLOCKS)], axis=1) + bi
    r = 1.0 / (1.0 + jnp.exp(-za))
    ig = _sigmoid(zi)
    sp = _softplus(-lam)
    log_a = (-LRU_C) * r * sp
    a = jnp.exp(log_a)
    m = jnp.sqrt(-jnp.tanh(log_a) * (1.0 + a * a))
    u = m * (ig * xc)
    return a, u, r, ig, m, sp


def _lru_fwd(p, cw, cb, wa, ba, wi, bi, lam, name, tb=256):
    t = p.shape[0]
    tb = min(tb, t)
    nb = t // tb
    r8 = tb // 8

    def body(x_ref, xp_ref, g_ref, cw_ref, cb_ref, wa_ref, ba_ref, wi_ref, bi_ref, lam_ref, y_ref, h_ref, xc_ref,
             hc_ref):
        i = pl.program_id(0)

        @pl.when(i == 0)
        def _():
            hc_ref[...] = jnp.zeros_like(hc_ref)

        halo = jnp.where(i == 0, 0.0, xp_ref[...])
        taps = _conv_taps(jnp.concatenate([halo, x_ref[...]], axis=0), tb)
        xc = _conv_fwd(taps, cw_ref[...], cb_ref[...])
        xc_ref[...] = xc
        a, u, _, _, _, _ = _lru_gates(xc, wa_ref, ba_ref[...], wi_ref, bi_ref[...], lam_ref[...])
        row = _iota((tb, 1), 0)
        u = u + jnp.where(row == 0, a * hc_ref[...], 0.0)
        h = _scan(a, u, reverse=False)
        h_ref[...] = h
        hc_ref[...] = h[tb - 1:tb, :]
        gl, _ = _gelu_and_grad(g_ref[...])
        y_ref[...] = (gl * h).astype(BF16)

    par = pl.BlockSpec((1, D), lambda i: (0, 0))
    wsp = pl.BlockSpec((LRU_BLOCKS, LRU_BLOCK, LRU_BLOCK), lambda i: (0, 0, 0))
    row = pl.BlockSpec((tb, D), lambda i: (i, 0))
    return pl.pallas_call(
        body, name=name, grid=(nb,),
        in_specs=[row, pl.BlockSpec((8, D), lambda i: (jnp.maximum(i * r8 - 1, 0), 0)),
                  pl.BlockSpec((tb, D), lambda i: (i, 1)),
                  pl.BlockSpec((4, D), lambda i: (0, 0)), par, wsp, par, wsp, par, par],
        out_specs=[row, row, row],
        out_shape=[jax.ShapeDtypeStruct((t, D), BF16), jax.ShapeDtypeStruct((t, D), F32),
                   jax.ShapeDtypeStruct((t, D), F32)],
        scratch_shapes=[pltpu.VMEM((1, D), F32)],
        compiler_params=_cp("arbitrary"),
    )(p, p, p, cw, cb, wa, ba, wi, bi, lam)


def _lru_bwd(p, xc, h, dy, cw, wa, ba, wi, bi, lam, name, tb=256):
    t = p.shape[0]
    tb = min(tb, t)
    nb = t // tb
    r8 = tb // 8

    def body(x_ref, g_ref, xc_ref, h_ref, hp_ref, dy_ref, cw_ref, wa_ref, ba_ref, wi_ref, bi_ref, lam_ref,
             dp_ref, dcw_ref, dcb_ref, dwa_ref, dba_ref, dwi_ref, dbi_ref, dlam_ref, carry_ref, dnext_ref):
        i = pl.program_id(0)
        blk = nb - 1 - i

        @pl.when(i == 0)
        def _():
            for r in (dcw_ref, dcb_ref, dwa_ref, dba_ref, dwi_ref, dbi_ref, dlam_ref, carry_ref, dnext_ref):
                r[...] = jnp.zeros_like(r)

        xc = xc_ref[...]
        lam = lam_ref[...]
        a, _, r, ig, m, sp = _lru_gates(xc, wa_ref, ba_ref[...], wi_ref, bi_ref[...], lam)
        gl, dgl = _gelu_and_grad(g_ref[...])
        h = h_ref[...]
        dy = dy_ref[...]
        dp_ref[:, D:] = (dy * h * dgl).astype(BF16)
        row = _iota((tb, 1), 0)
        dh = dy * gl + jnp.where(row == tb - 1, carry_ref[...], 0.0)
        b = jnp.where(row < tb - 1, pltpu.roll(a, tb - 1, 0), 0.0)
        gs = _scan(b, dh, reverse=True)
        carry_ref[...] = a[0:1] * gs[0:1]
        h_last = jnp.where(blk == 0, 0.0, hp_ref[7:8, :])
        hprev = jnp.where(row == 0, h_last, pltpu.roll(h, 1, 0))
        da = gs * hprev
        dm = gs * ig * xc
        di = gs * m * xc
        dxc = gs * m * ig
        dlog = a * (da - a * (dm / m))
        dr = dlog * ((-LRU_C) * sp)
        dsp = jnp.sum(dlog * ((-LRU_C) * r), axis=0, keepdims=True)
        dlam_ref[...] += dsp * (-_sigmoid(-lam))
        dza = dr * r * (1.0 - r)
        dzi = di * ig * (1.0 - ig)
        dba_ref[...] += jnp.sum(dza, axis=0, keepdims=True)
        dbi_ref[...] += jnp.sum(dzi, axis=0, keepdims=True)
        parts = []
        for n in range(LRU_BLOCKS):
            sl = slice(n * 128, (n + 1) * 128)
            dwa_ref[n] += _dot(xc[:, sl], dza[:, sl], _TN)
            dwi_ref[n] += _dot(xc[:, sl], dzi[:, sl], _TN)
            parts.append(_dot(dza[:, sl], wa_ref[n], _NT) + _dot(dzi[:, sl], wi_ref[n], _NT))
        dxc = dxc + jnp.concatenate(parts, axis=1)
        dx, dcw, dcb = _conv_bwd(dxc, dnext_ref[...], x_ref[...], cw_ref[...], tb)
        dp_ref[:, :D] = dx.astype(BF16)
        dcw_ref[...] += dcw
        dcb_ref[...] += dcb
        dnext_ref[...] = dxc[0:8]

    par = pl.BlockSpec((1, D), lambda i: (0, 0))
    wsp = pl.BlockSpec((LRU_BLOCKS, LRU_BLOCK, LRU_BLOCK), lambda i: (0, 0, 0))
    cws = pl.BlockSpec((4, D), lambda i: (0, 0))
    rev = lambda i: nb - 1 - i
    blk0 = pl.BlockSpec((tb, D), lambda i: (rev(i), 0))
    w_shape = jax.ShapeDtypeStruct((LRU_BLOCKS, LRU_BLOCK, LRU_BLOCK), F32)
    v_shape = jax.ShapeDtypeStruct((1, D), F32)
    return pl.pallas_call(
        body, name=name, grid=(nb,),
        in_specs=[blk0, pl.BlockSpec((tb, D), lambda i: (rev(i), 1)), blk0, blk0,
                  pl.BlockSpec((8, D), lambda i: (jnp.maximum(rev(i) * r8 - 1, 0), 0)), blk0,
                  cws, wsp, par, wsp, par, par],
        out_specs=[pl.BlockSpec((tb, 2 * D), lambda i: (rev(i), 0)), cws, par, wsp, par, wsp, par, par],
        out_shape=[jax.ShapeDtypeStruct((t, 2 * D), BF16), jax.ShapeDtypeStruct((4, D), F32), v_shape,
                   w_shape, v_shape, w_shape, v_shape, v_shape],
        scratch_shapes=[pltpu.VMEM((1, D), F32), pltpu.VMEM((8, D), F32)],
        compiler_params=_cp("arbitrary"),
    )(p, p, xc, h, h, dy, cw, wa, ba, wi, bi, lam)


def _ssd_consts():
    m0 = _iota((1, 128), 1) < 64
    e = (jnp.right_shift(_iota((SSD_HEADS, D_SSD), 1), 6) == _iota((SSD_HEADS, D_SSD), 0)).astype(BF16)
    tril = (_iota((CHUNK, CHUNK), 0) >= _iota((CHUNK, CHUNK), 1)).astype(F32)
    eye = (_iota((SSD_HEADS, SSD_HEADS), 0) == _iota((SSD_HEADS, SSD_HEADS), 1)).astype(F32)
    r2 = _iota((CHUNK, 128), 0)
    c2 = jnp.bitwise_and(_iota((CHUNK, 128), 1), 63)
    return dict(m0=m0, e=e, tril=tril, eye=eye, causal2=r2 >= c2, fold=(c2 == r2).astype(BF16))


def _ssd_pre(c, p_ref, dtb_ref, alog_ref, dvec_ref, k):
    sg = _sigmoid(c)
    xbc = c * sg
    dtp = p_ref[:, S_DT:S_DT + DT_REAL] + dtb_ref[...]
    dt = _softplus(dtp)
    a = -jnp.exp(alog_ref[...])
    cs = _dot_hi(k["tril"], dt * a)
    cs_last = cs[CHUNK - 1:CHUNK]
    dend = jnp.exp(cs_last - cs)
    cdec = jnp.exp(cs_last)
    big = _dot01(jnp.concatenate([dt, jnp.exp(cs), dend], axis=0), k["e"])
    small = _dot01(jnp.concatenate([jnp.broadcast_to(cdec, (8, SSD_HEADS)),
                                    jnp.broadcast_to(dvec_ref[...], (8, SSD_HEADS))], axis=0), k["e"])
    cst2 = _dot_hi(k["eye"], jnp.concatenate([cs, cs], axis=0), _NT)
    return dict(c=c, sg=sg, xs=xbc[:, :D_SSD], bm=xbc[:, D_SSD:D_SSD + 512],
                cm=xbc[:, D_SSD + 512:], dtp=dtp, dt=dt, a=a, cs=cs, dend=dend, cdec=cdec,
                dtx=big[0:CHUNK], ecx=big[CHUNK:2 * CHUNK], dex=big[2 * CHUNK:3 * CHUNK],
                cdx=small[0:1], ddx=small[8:9], cst2=cst2)


def _pair_decay(p, cs, cst2, k):
    h0, h1 = 2 * p, 2 * p + 1
    colp = jnp.where(k["m0"], cs[:, h0:h0 + 1], cs[:, h1:h1 + 1])
    rowp = jnp.where(k["m0"], cst2[h0:h0 + 1, :], cst2[h1:h1 + 1, :])
    return jnp.where(k["causal2"], jnp.exp(colp - rowp), 0.0)


def _pair_stack(xp, k):
    return jnp.concatenate([jnp.where(k["m0"], xp, 0.0), jnp.where(k["m0"], 0.0, xp)], axis=0)


def _group_norm(yz, nw, with_stats=False):
    outs, stats = [], []
    for g in range(SSD_GROUPS):
        yzg = yz[:, g * GROUP_W:(g + 1) * GROUP_W]
        r = lax.rsqrt(jnp.mean(yzg * yzg, axis=1, keepdims=True) + EPS)
        outs.append(yzg * r)
        stats.append(r)
    y = jnp.concatenate(outs, axis=1) * nw
    return (y, stats) if with_stats else y


def _ssd_fwd(p, cw, cb, dtb, alog, dvec, nw, name):
    t = p.shape[0]
    nc = t // CHUNK

    def body(p_ref, pp_ref, cw_ref, cb_ref, dtb_ref, alog_ref, dvec_ref, nw_ref, y_ref, yraw_ref, hs_ref, c_ref,
             h_scr):
        i = pl.program_id(0)

        @pl.when(i == 0)
        def _():
            h_scr[...] = jnp.zeros_like(h_scr)

        k = _ssd_consts()
        halo = jnp.where(i == 0, 0.0, pp_ref[:, S_XBC:S_DT])
        taps = _conv_taps(jnp.concatenate([halo, p_ref[:, S_XBC:S_DT]], axis=0), CHUNK)
        c = _conv_fwd(taps, cw_ref[...], cb_ref[...])
        c_ref[...] = c
        s = _ssd_pre(c, p_ref, dtb_ref, alog_ref, dvec_ref, k)
        xs, bm, cm = s["xs"], s["bm"], s["cm"]
        xdt = xs * s["dtx"]
        hprev = h_scr[...]
        hs_ref[0] = hprev
        ys, hn = [], []
        for g in range(SSD_GROUPS):
            gs = slice(g * GROUP_W, (g + 1) * GROUP_W)
            bg = bm[:, g * 128:(g + 1) * 128]
            cg = cm[:, g * 128:(g + 1) * 128]
            cbdup = _dot(cg, jnp.concatenate([bg, bg], axis=0), _NT)
            hp_g = hprev[:, gs]
            yd = []
            for q in range(4):
                pr = g * 4 + q
                mp = cbdup * _pair_decay(pr, s["cs"], s["cst2"], k)
                yd.append(_dot(mp, _pair_stack(xdt[:, pr * 128:(pr + 1) * 128], k)))
            ys.append(jnp.concatenate(yd, axis=1) + _dot(cg, hp_g) * s["ecx"][:, gs])
            hn.append(hp_g * s["cdx"][:, gs] + _dot(bg, xdt[:, gs] * s["dex"][:, gs], _TN))
        h_scr[...] = jnp.concatenate(hn, axis=1)
        yraw = jnp.concatenate(ys, axis=1) + s["ddx"] * xs
        yraw_ref[...] = yraw
        z = p_ref[:, S_Z:S_Z + D_SSD]
        y_ref[...] = _group_norm(yraw * (z * _sigmoid(z)), nw_ref[...]).astype(BF16)

    hv = pl.BlockSpec((1, DT_REAL), lambda i: (0, 0))
    return pl.pallas_call(
        body, name=name, grid=(nc,),
        in_specs=[pl.BlockSpec((CHUNK, W_SSD), lambda i: (i, 0)),
                  pl.BlockSpec((8, W_SSD), lambda i: (jnp.maximum(i * (CHUNK // 8) - 1, 0), 0)),
                  pl.BlockSpec((4, D_XBC), lambda i: (0, 0)), pl.BlockSpec((1, D_XBC), lambda i: (0, 0)),
                  hv, hv, hv, pl.BlockSpec((1, D_SSD), lambda i: (0, 0))],
        out_specs=[pl.BlockSpec((CHUNK, D_SSD), lambda i: (i, 0)), pl.BlockSpec((CHUNK, D_SSD), lambda i: (i, 0)),
                   pl.BlockSpec((1, SSD_STATE, D_SSD), lambda i: (i, 0, 0)),
                   pl.BlockSpec((CHUNK, D_XBC), lambda i: (i, 0))],
        out_shape=[jax.ShapeDtypeStruct((t, D_SSD), BF16), jax.ShapeDtypeStruct((t, D_SSD), F32),
                   jax.ShapeDtypeStruct((nc, SSD_STATE, D_SSD), F32), jax.ShapeDtypeStruct((t, D_XBC), F32)],
        scratch_shapes=[pltpu.VMEM((SSD_STATE, D_SSD), F32)],
        compiler_params=_cp("arbitrary"),
    )(p, p, cw, cb, dtb, alog, dvec, nw)


def _ssd_bwd(p, c, yraw, hs, dy, cw, dtb, alog, dvec, nw, name):
    t = p.shape[0]
    nc = t // CHUNK

    def body(p_ref, c_ref, yraw_ref, hs_ref, dy_ref, cw_ref, dtb_ref, alog_ref, dvec_ref, nw_ref,
             dp_ref, dcw_ref, dcb_ref, ddtb_ref, dalog_ref, dd_ref, dnw_ref, dh_scr, dnext_scr):
        i = pl.program_id(0)

        @pl.when(i == 0)
        def _():
            for r in (dcw_ref, dcb_ref, ddtb_ref, dalog_ref, dd_ref, dnw_ref, dh_scr, dnext_scr):
                r[...] = jnp.zeros_like(r)

        k = _ssd_consts()
        s = _ssd_pre(c_ref[...], p_ref, dtb_ref, alog_ref, dvec_ref, k)
        xs, bm, cm, cs, dt, a = s["xs"], s["bm"], s["cm"], s["cs"], s["dt"], s["a"]
        m0 = k["m0"]
        xdt = xs * s["dtx"]
        hprev = hs_ref[0]
        dh = dh_scr[...]

        nw_v = nw_ref[...]
        yraw = yraw_ref[...]
        z = p_ref[:, S_Z:S_Z + D_SSD]
        sz = _sigmoid(z)
        siluz = z * sz
        yz = yraw * siluz
        dyo = dy_ref[...]
        dyn = dyo * nw_v
        dyz_parts, dnw_parts = [], []
        for g in range(SSD_GROUPS):
            gs = slice(g * GROUP_W, (g + 1) * GROUP_W)
            yzg = yz[:, gs]
            r = lax.rsqrt(jnp.mean(yzg * yzg, axis=1, keepdims=True) + EPS)
            dnw_parts.append(jnp.sum(dyo[:, gs] * yzg * r, axis=0, keepdims=True))
            dyz_parts.append(r * dyn[:, gs] - yzg * (r * r * r) * jnp.mean(dyn[:, gs] * yzg, axis=1, keepdims=True))
        dnw_ref[...] += jnp.concatenate(dnw_parts, axis=1)
        dyz = jnp.concatenate(dyz_parts, axis=1)
        d_y = dyz * siluz
        dp_ref[:, S_Z:S_Z + D_SSD] = (dyz * yraw * (sz * (1.0 + z * (1.0 - sz)))).astype(BF16)
        dd_row = jnp.sum(d_y * xs, axis=0, keepdims=True)
        dxs = d_y * s["ddx"]

        lane_h = _iota((1, SSD_HEADS), 1)
        sub_h = _iota((SSD_HEADS, 1), 0)
        dcs = jnp.zeros((CHUNK, SSD_HEADS), F32)
        dcst2 = jnp.zeros((SSD_HEADS, 128), F32)
        dxdt_parts, db_parts, dc_parts, dhp_parts, yoff_parts, dend_parts, dcd_parts = [], [], [], [], [], [], []
        for g in range(SSD_GROUPS):
            gs = slice(g * GROUP_W, (g + 1) * GROUP_W)
            bg = bm[:, g * 128:(g + 1) * 128]
            cg = cm[:, g * 128:(g + 1) * 128]
            bdup = jnp.concatenate([bg, bg], axis=0)
            cbdup = _dot(cg, bdup, _NT)
            dcb2 = jnp.zeros((CHUNK, 128), F32)
            dxp_parts = []
            for q in range(4):
                pr = g * 4 + q
                h0, h1 = 2 * pr, 2 * pr + 1
                lp = _pair_decay(pr, cs, s["cst2"], k)
                mp = cbdup * lp
                xst = _pair_stack(xdt[:, pr * 128:(pr + 1) * 128], k)
                dyp = d_y[:, pr * 128:(pr + 1) * 128]
                dmp = _dot(dyp, xst, _NT)
                dxst = _dot(mp, dyp, _TN)
                dxp_parts.append(jnp.where(m0, dxst[:CHUNK], dxst[CHUNK:]))
                dcb2 = dcb2 + dmp * lp
                dlm = dmp * mp
                rs0 = jnp.sum(jnp.where(m0, dlm, 0.0), axis=1, keepdims=True)
                rs1 = jnp.sum(jnp.where(m0, 0.0, dlm), axis=1, keepdims=True)
                dcs = dcs + jnp.where(lane_h == h0, rs0, 0.0) + jnp.where(lane_h == h1, rs1, 0.0)
                colsum = jnp.sum(dlm, axis=0, keepdims=True)
                sel = ((sub_h == h0) & m0) | ((sub_h == h1) & jnp.logical_not(m0))
                dcst2 = dcst2 - jnp.where(sel, colsum, 0.0)
            dcg = _dot(dcb2, bdup)
            dbdup = _dot(dcb2, cg, _TN)
            dbg = dbdup[:CHUNK] + dbdup[CHUNK:]
            hp_g = hprev[:, gs]
            zoff = _dot(cg, hp_g)
            dzo = d_y[:, gs] * s["ecx"][:, gs]
            dcg = dcg + _dot(dzo, hp_g, _NT)
            dh_g = dh[:, gs]
            dhp_parts.append(_dot(cg, dzo, _TN) + dh_g * s["cdx"][:, gs])
            dcd_parts.append(jnp.sum(dh_g * hp_g, axis=0, keepdims=True))
            wg = xdt[:, gs] * s["dex"][:, gs]
            dbg = dbg + _dot(wg, dh_g, _NT)
            dwg = _dot(bg, dh_g)
            dxdt_parts.append(jnp.concatenate(dxp_parts, axis=1) + dwg * s["dex"][:, gs])
            dend_g = dwg * wg
            dend_parts.append(jnp.sum(dend_g, axis=0, keepdims=True))
            yoff_parts.append(dzo * zoff - dend_g)
            db_parts.append(dbg)
            dc_parts.append(dcg)
        dh_scr[...] = jnp.concatenate(dhp_parts, axis=1)
        dxdt = jnp.concatenate(dxdt_parts, axis=1)
        sums = _dot01(jnp.concatenate([jnp.concatenate(yoff_parts, axis=1), dxdt * xs], axis=0), k["e"], _NT)
        rows8 = jnp.concatenate([jnp.broadcast_to(jnp.concatenate(r, axis=1), (8, D_SSD))
                                 for r in (dcd_parts, [dd_row], dend_parts)], axis=0)
        small = _dot01(rows8, k["e"], _NT)
        dd_ref[...] += small[8:9]
        dcs_last = small[0:1] * s["cdec"] + small[16:17]
        hi, lo = _split(dcst2)
        dcs = (dcs + sums[0:CHUNK]
               + lax.dot_general(k["fold"], hi, _NT, preferred_element_type=F32)
               + lax.dot_general(k["fold"], lo, _NT, preferred_element_type=F32)
               + jnp.where(_iota((CHUNK, 1), 0) == CHUNK - 1, dcs_last, 0.0))
        dda = _dot_hi(k["tril"], dcs, _TN)
        ddt = dda * a + sums[CHUNK:2 * CHUNK]
        dalog_ref[...] += jnp.sum(dda * dt, axis=0, keepdims=True) * a
        dxs = dxs + dxdt * s["dtx"]
        draw = ddt * _sigmoid(s["dtp"])
        ddtb_ref[...] += jnp.sum(draw, axis=0, keepdims=True)
        dp_ref[:, S_DT:] = jnp.zeros((CHUNK, W_SSD - S_DT), BF16)
        dp_ref[:, S_DT:S_DT + DT_REAL] = draw.astype(BF16)
        dxbc = jnp.concatenate([dxs] + db_parts + dc_parts, axis=1)
        sg, c = s["sg"], s["c"]
        dc = dxbc * (sg * (1.0 + c * (1.0 - sg)))
        dx, dcw, dcb = _conv_bwd(dc, dnext_scr[...], p_ref[:, S_XBC:S_DT], cw_ref[...], CHUNK)
        dp_ref[:, S_XBC:S_DT] = dx.astype(BF16)
        dcw_ref[...] += dcw
        dcb_ref[...] += dcb
        dnext_scr[...] = dc[0:8]

    rev = lambda i: nc - 1 - i
    hv = pl.BlockSpec((1, DT_REAL), lambda i: (0, 0))
    cws = pl.BlockSpec((4, D_XBC), lambda i: (0, 0))
    cbs = pl.BlockSpec((1, D_XBC), lambda i: (0, 0))
    nws = pl.BlockSpec((1, D_SSD), lambda i: (0, 0))
    wide = pl.BlockSpec((CHUNK, D_SSD), lambda i: (rev(i), 0))
    hshape = jax.ShapeDtypeStruct((1, DT_REAL), F32)
    return pl.pallas_call(
        body, name=name, grid=(nc,),
        in_specs=[pl.BlockSpec((CHUNK, W_SSD), lambda i: (rev(i), 0)),
                  pl.BlockSpec((CHUNK, D_XBC), lambda i: (rev(i), 0)),
                  wide, pl.BlockSpec((1, SSD_STATE, D_SSD), lambda i: (rev(i), 0, 0)), wide,
                  cws, hv, hv, hv, nws],
        out_specs=[pl.BlockSpec((CHUNK, W_SSD), lambda i: (rev(i), 0)), cws, cbs, hv, hv, hv, nws],
        out_shape=[jax.ShapeDtypeStruct((t, W_SSD), BF16), jax.ShapeDtypeStruct((4, D_XBC), F32),
                   jax.ShapeDtypeStruct((1, D_XBC), F32), hshape, hshape, hshape,
                   jax.ShapeDtypeStruct((1, D_SSD), F32)],
        scratch_shapes=[pltpu.VMEM((SSD_STATE, D_SSD), F32), pltpu.VMEM((8, D_XBC), F32)],
        compiler_params=_cp("arbitrary"),
    )(p, c, yraw, hs, dy, cw, dtb, alog, dvec, nw)


def _loss_head(y, target, name, tb=512):
    t = y.shape[0]
    tb = min(tb, t)

    def body(y_ref, t_ref, dy_ref, l_ref):
        @pl.when(pl.program_id(0) == 0)
        def _():
            l_ref[...] = jnp.zeros_like(l_ref)

        e = y_ref[...] - t_ref[...]
        dy_ref[...] = e * (1.0 / D)
        l_ref[...] += jnp.sum(jnp.sum(e * e, axis=1, keepdims=True), axis=0, keepdims=True) * (0.5 / D)

    row = pl.BlockSpec((tb, D), lambda i: (i, 0))
    return pl.pallas_call(
        body, name=name, grid=(t // tb,), in_specs=[row, row],
        out_specs=[row, pl.BlockSpec((8, 128), lambda i: (0, 0))],
        out_shape=[jax.ShapeDtypeStruct((t, D), F32), jax.ShapeDtypeStruct((8, 128), F32)],
        compiler_params=_cp("arbitrary"),
    )(y, target)


def _adamw(slots, w, m, v, name, tb):
    nl = len(slots)
    ns, r, c = slots[0].shape
    assert r % tb == 0 and w.shape == (nl, r, c), (r, tb, w.shape)

    def body(*refs):
        s_refs = refs[:nl]
        w_ref, m_ref, v_ref, g_ref, d_ref, m2_ref, v2_ref = refs[nl:]

        def total(ref):
            acc = ref[0].astype(F32)
            for j in range(1, ns):
                acc = acc + ref[j].astype(F32)
            return acc

        g = total(s_refs[0])
        for layer in range(1, nl):
            g = jnp.where(pl.program_id(0) == layer, total(s_refs[layer]), g)
        m2 = ADAM_B1 * m_ref[...] + (1.0 - ADAM_B1) * g
        v2 = ADAM_B2 * v_ref[...] + (1.0 - ADAM_B2) * (g * g)
        m_hat = m2 / (1.0 - ADAM_B1 ** ADAM_STEP)
        v_hat = v2 / (1.0 - ADAM_B2 ** ADAM_STEP)
        g_ref[...] = g
        d_ref[...] = -ADAM_LR * (m_hat / (jnp.sqrt(v_hat) + ADAM_EPS) + ADAM_WD * w_ref[...])
        m2_ref[...] = m2
        v2_ref[...] = v2

    def slot_spec(layer):
        return pl.BlockSpec((ns, tb, c), lambda l, i: (0, jnp.where(l == layer, i, 0), 0))

    row = pl.BlockSpec((None, tb, c), lambda l, i: (l, i, 0))
    shp = jax.ShapeDtypeStruct((nl, r, c), F32)
    return pl.pallas_call(
        body, name=name, grid=(nl, r // tb),
        in_specs=[slot_spec(layer) for layer in range(nl)] + [row, row, row],
        out_specs=[row, row, row, row], out_shape=[shp, shp, shp, shp], compiler_params=_cp("arbitrary", "arbitrary"),
    )(*slots, w, m, v)


def _pair_sum(own, got, name, out_dtype, tb):
    nj, _, r, c = own.shape
    mc = lax.axis_index("c")

    def body(mc_ref, a_ref, b_ref, o_ref):
        del mc_ref
        o_ref[...] = (a_ref[...] + b_ref[...]).astype(out_dtype)

    return pl.pallas_call(
        body, name=name,
        grid_spec=pltpu.PrefetchScalarGridSpec(
            num_scalar_prefetch=1, grid=(nj, r // tb),
            in_specs=[pl.BlockSpec((None, None, tb, c), lambda j, i, mc_ref: (j, mc_ref[0], i, 0)),
                      pl.BlockSpec((None, tb, c), lambda j, i, mc_ref: (j, i, 0))],
            out_specs=pl.BlockSpec((None, tb, c), lambda j, i, mc_ref: (j, i, 0))),
        out_shape=jax.ShapeDtypeStruct((nj, r, c), out_dtype), compiler_params=_cp("parallel", "parallel"),
    )(jnp.reshape(mc, (1,)).astype(jnp.int32), own, got)


def _slot_sum(slots, name):
    ns, r, c = slots.shape

    def body(s_ref, o_ref):
        g = s_ref[0]
        for j in range(1, ns):
            g = g + s_ref[j]
        o_ref[...] = g

    return pl.pallas_call(body, name=name, out_shape=jax.ShapeDtypeStruct((r, c), F32))(slots)


def _position():
    return lax.axis_index("x"), lax.axis_index("y"), lax.axis_index("c")


def _comm(exchange, peers, xs, out_shapes, sems, name, collective_id):
    n = len(xs)
    if collective_id is None:
        def body(*refs):
            exchange(refs[:n], refs[n:n + len(out_shapes)], *refs[n + len(out_shapes):])

        return pl.pallas_call(body, name=name, in_specs=[ANY] * n, out_specs=[ANY] * len(out_shapes),
                              out_shape=out_shapes, scratch_shapes=sems)(*xs)
    def launch(*refs):
        barrier = pltpu.get_barrier_semaphore()
        to = peers(*_position())
        for peer in to:
            pl.semaphore_signal(barrier, inc=1, device_id=peer, device_id_type=MESH)
        pl.semaphore_wait(barrier, len(to))
        exchange(refs[:n], refs[n:n + len(out_shapes)], *refs[n + len(out_shapes):])

    return pl.kernel(launch, out_type=out_shapes, mesh=plsc.ScalarSubcoreMesh(axis_name="seq", num_cores=1), name=name,
                     scratch_types=sems, compiler_params=pltpu.CompilerParams(collective_id=collective_id))(*xs)


def _all_gather(xs, name, collective_id=None):
    n = len(xs)
    return _comm(_gather_body, lambda x, y, c: [(x, y, 1 - c), (1 - x, y, c), (x, 1 - y, c), (1 - x, 1 - y, c)], xs,
                 [jax.ShapeDtypeStruct((N_DEV,) + x.shape, x.dtype) for x in xs],
                 [pltpu.SemaphoreType.DMA((n, 7)), pltpu.SemaphoreType.DMA((n, 7)), pltpu.SemaphoreType.DMA((n,))],
                 name, collective_id)


def _gather_body(x_refs, out_refs, send_sems, recv_sems, local_sems):
    n = len(x_refs)
    mx, my, mc = _position()
    me, sibling = (mx, my, mc), (mx, my, 1 - mc)
    chips = [(1 - mx, my), (mx, 1 - my), (1 - mx, 1 - my)]

    def copy(a, k, block, to, own=False):
        dst = out_refs[a].at[4 * block[0] + 2 * block[1] + block[2]]
        return pltpu.make_async_remote_copy(
            src_ref=x_refs[a] if own else dst, dst_ref=dst,
            send_sem=send_sems.at[a, k], recv_sem=recv_sems.at[a, k], device_id=to, device_id_type=MESH)

    mine = [pltpu.make_async_copy(x_refs[a], out_refs[a].at[4 * mx + 2 * my + mc], local_sems.at[a]) for a in range(n)]
    first = [copy(a, 1 + j, me, (*chip, mc), own=True) for j, chip in enumerate(chips) for a in range(n)]
    first += [copy(a, 0, me, sibling, own=True) for a in range(n)]
    for cp in first + mine:
        cp.start()
    passed = []
    for j, chip in enumerate(chips):
        for a in range(n):
            copy(a, 1 + j, (*chip, mc), me).wait_recv()
            passed.append(copy(a, 4 + j, (*chip, mc), sibling))
            passed[-1].start()
    for a in range(n):
        copy(a, 0, sibling, me).wait_recv()
    for j, chip in enumerate(chips):
        for a in range(n):
            copy(a, 4 + j, (*chip, 1 - mc), me).wait_recv()
    for cp in first + passed:
        cp.wait_send()
    for cp in mine:
        cp.wait()


def _exchange_sibling(gs, name, collective_id=None):
    n = len(gs)

    def exchange(g_refs, r_refs, send_sems, recv_sems):
        mx, my, mc = _position()
        cps = [pltpu.make_async_remote_copy(src_ref=g_refs[a].at[:, 1 - mc], dst_ref=r_refs[a],
                                            send_sem=send_sems.at[a], recv_sem=recv_sems.at[a],
                                            device_id=(mx, my, 1 - mc), device_id_type=MESH) for a in range(n)]
        for cp in cps:
            cp.start()
        for cp in cps:
            cp.wait()

    return _comm(exchange, lambda x, y, c: [(x, y, 1 - c)], gs,
                 [jax.ShapeDtypeStruct(g.shape[:1] + g.shape[2:], g.dtype) for g in gs],
                 [pltpu.SemaphoreType.DMA((n,)), pltpu.SemaphoreType.DMA((n,))], name, collective_id)


def _exchange_chips(ss, name, collective_id=None):
    n = len(ss)

    def exchange(s_refs, r_refs, send_sems, recv_sems, local_sems):
        mx, my, mc = _position()
        my_chip = 2 * mx + my
        chips = [(1 - mx, my), (mx, 1 - my), (1 - mx, 1 - my)]

        def copy(a, k, to_slot):
            px, py = chips[k]
            return pltpu.make_async_remote_copy(
                src_ref=s_refs[a].at[2 * px + py], dst_ref=r_refs[a].at[to_slot], send_sem=send_sems.at[a, k],
                recv_sem=recv_sems.at[a, k], device_id=(px, py, mc), device_id_type=MESH)

        sends = [copy(a, k, my_chip) for k in range(3) for a in range(n)]
        local = [pltpu.make_async_copy(s_refs[a].at[my_chip], r_refs[a].at[my_chip], local_sems.at[a])
                 for a in range(n)]
        for cp in sends + local:
            cp.start()
        for k in range(3):
            px, py = chips[k]
            for a in range(n):
                copy(a, k, 2 * px + py).wait_recv()
        for cp in sends:
            cp.wait_send()
        for cp in local:
            cp.wait()

    return _comm(exchange, lambda x, y, c: [(1 - x, y, c), (x, 1 - y, c), (1 - x, 1 - y, c)], ss,
                 [jax.ShapeDtypeStruct(s.shape, s.dtype) for s in ss],
                 [pltpu.SemaphoreType.DMA((n, 3)), pltpu.SemaphoreType.DMA((n, 3)), pltpu.SemaphoreType.DMA((n,))],
                 name, collective_id)


def _cols_concat(g, name, tb=128):
    _, k_dim, n = g.shape

    def body(g_ref, o_ref):
        o_ref[...] = jnp.concatenate([g_ref[d] for d in range(N_DEV)], axis=1)

    return pl.pallas_call(
        body, name=name, grid=(k_dim // tb,),
        in_specs=[pl.BlockSpec((N_DEV, tb, n), lambda i: (0, i, 0))],
        out_specs=pl.BlockSpec((tb, N_DEV * n), lambda i: (i, 0)),
        out_shape=jax.ShapeDtypeStruct((k_dim, N_DEV * n), g.dtype), compiler_params=_cp("parallel"),
    )(g)


def _cols_split(parts, name, tb=128):
    k_dim = parts[0].shape[0]
    n = sum(p.shape[1] for p in parts) // N_DEV

    def body(*refs):
        full = jnp.concatenate([r[...] for r in refs[:-1]], axis=1)
        for d in range(N_DEV):
            refs[-1][d] = full[:, d * n:(d + 1) * n]

    return pl.pallas_call(
        body, name=name, grid=(k_dim // tb,),
        in_specs=[pl.BlockSpec((tb, p.shape[1]), lambda i: (i, 0)) for p in parts],
        out_specs=pl.BlockSpec((N_DEV, tb, n), lambda i: (0, i, 0)),
        out_shape=jax.ShapeDtypeStruct((N_DEV, k_dim, n), parts[0].dtype), compiler_params=_cp("parallel"),
    )(*parts)


_Q0, _GL0 = 7200, 8224
N_SHARD_IN = N_IN // N_DEV


def _w_in_regions(g, name, tb=128):
    def body(g_ref, ssd_ref, lru_ref, q_ref, gl_ref):
        full = jnp.concatenate([g_ref[d] for d in range(N_DEV)], axis=1)
        lru_ref[...] = full[:, 0:2 * D]
        ssd_ref[:, :S_DT] = full[:, 2 * D:2 * D + S_DT]
        ssd_ref[:, S_DT:] = jnp.zeros((tb, W_SSD - S_DT), g.dtype)
        ssd_ref[:, S_DT:S_DT + DT_REAL] = full[:, 2 * D + S_DT:_Q0]
        q_ref[...] = full[:, _Q0:_GL0]
        gl_ref[...] = full[:, _GL0:N_IN]

    widths = (W_SSD, 2 * D, D, 3 * D)
    return pl.pallas_call(
        body, name=name, grid=(D // tb,),
        in_specs=[pl.BlockSpec((N_DEV, tb, N_SHARD_IN), lambda i: (0, i, 0))],
        out_specs=[pl.BlockSpec((tb, wd), lambda i: (i, 0)) for wd in widths],
        out_shape=[jax.ShapeDtypeStruct((D, wd), g.dtype) for wd in widths], compiler_params=_cp("parallel"),
    )(g)


def _w_in_shards(dssd, dlru, dq, dgl, name, tb=128):
    def body(ssd_ref, lru_ref, q_ref, gl_ref, o_ref):
        full = jnp.concatenate([lru_ref[...], ssd_ref[:, :S_DT + DT_REAL], q_ref[...], gl_ref[...]], axis=1)
        for d in range(N_DEV):
            o_ref[d] = full[:, d * N_SHARD_IN:(d + 1) * N_SHARD_IN]

    return pl.pallas_call(
        body, name=name, grid=(D // tb,),
        in_specs=[pl.BlockSpec((tb, a.shape[1]), lambda i: (i, 0)) for a in (dssd, dlru, dq, dgl)],
        out_specs=pl.BlockSpec((N_DEV, tb, N_SHARD_IN), lambda i: (0, i, 0)),
        out_shape=jax.ShapeDtypeStruct((N_DEV, D, N_SHARD_IN), F32), compiler_params=_cp("parallel"),
    )(dssd, dlru, dq, dgl)


_BIG = (("w_in", "col", (1024, 1412)), ("mem_w_kv", "col", (1024, 256)), ("w_br_lru", "row", (128, 1024)),
        ("w_br_ssd", "row", (256, 1024)), ("w_br_xa", "row", (128, 1024)), ("w_out", "row", (128, 1024)),
        ("ffn_w_in", "col", (1024, 704)), ("ffn_w_down", "row", (352, 1024)))
_SMALL = (("b_gate", (3, 128)), ("lru_conv_w", (4, 128)), ("ssd_conv_w", (4, 384)))
_REP = (("lru_conv_b", (1024,)), ("lru_w_a", (8, 128, 128)), ("lru_b_a", (1024,)), ("lru_w_i", (8, 128, 128)),
        ("lru_b_i", (1024,)), ("lru_lambda", (1024,)), ("ssd_conv_b", (3072,)), ("ssd_dt_bias", (32,)),
        ("ssd_a_log", (32,)), ("ssd_d", (32,)), ("ssd_norm_w", (2048,)), ("ln1_g", (1024,)), ("ln1_b", (1024,)),
        ("ln2_g", (1024,)), ("ln2_b", (1024,)))
_ORDER = ("w_in", "b_gate", "lru_conv_w", "lru_conv_b", "lru_w_a", "lru_b_a", "lru_w_i", "lru_b_i", "lru_lambda",
          "ssd_conv_w", "ssd_conv_b", "ssd_dt_bias", "ssd_a_log", "ssd_d", "ssd_norm_w", "mem_w_kv", "w_br_lru",
          "w_br_ssd", "w_br_xa", "w_out", "ln1_g", "ln1_b", "ffn_w_in", "ffn_w_down", "ln2_g", "ln2_b")

LANES = 1024
N_SMALL = sum(DEPTH * s[0] * s[1] for _, s in _SMALL)
R_SMALL = 8
N_REP = sum(DEPTH * math.prod(s) for _, s in _REP)
R_REP = 68
R_SM = R_SMALL + R_REP + 4
R_TAIL = R_SMALL + N_DEV * R_REP
TB_TAIL = 184
assert N_SMALL <= R_SMALL * LANES and N_REP <= N_DEV * R_REP * LANES


def _rows(flat, rows):
    return jnp.pad(flat, (0, rows * LANES - flat.shape[0])).reshape(rows, LANES)


def _rowblk(a, cap):
    return max(b for b in range(16, cap + 1, 16) if a % b == 0)


def _pack_tail(d):
    small = jnp.concatenate([d[n].reshape(-1) for n, _ in _SMALL])
    rep = jnp.concatenate([d[n].reshape(-1) for n, _ in _REP])
    return jnp.concatenate([_rows(small, R_SMALL), _rows(rep, N_DEV * R_REP)], axis=0)


def _unpack_tail(a):
    out, o = {}, 0
    flat = a[:R_SMALL].reshape(-1)
    for n, s in _SMALL:
        k = DEPTH * math.prod(s)
        out[n] = flat[o:o + k].reshape((DEPTH,) + s)
        o += k
    flat, o = a[R_SMALL:].reshape(-1), 0
    for n, s in _REP:
        k = DEPTH * math.prod(s)
        out[n] = flat[o:o + k].reshape((DEPTH,) + s)
        o += k
    return out


def _by_dest(g):
    g = g.reshape(g.shape[:-1] + (N_DEV, g.shape[-1] // N_DEV))
    return jnp.moveaxis(g, -2, 0).reshape(N_DEV, -1)


def _from_stack(st):
    st = jnp.moveaxis(st, 0, -2)
    return st.reshape(st.shape[:-2] + (st.shape[-2] * st.shape[-1],))


def _layer_fwd(x, mem, w, l):
    nm = lambda s: f"{s}_l{l}"
    wi = w["wi"]
    row = lambda v: v.reshape(1, -1)
    s = dict(x=x, wi=wi)
    s["p_ssd"] = _mm(x, wi["ssd"], name=nm("proj_ssd"))
    s["p_lru"] = _mm(x, wi["lru"], name=nm("proj_lru"))
    s["p_q"] = _mm(x, wi["q"], name=nm("proj_q"))
    s["p_gl"] = _mm(x, wi["gl"], name=nm("proj_gl"))
    s["lru_par"] = (w["lru_conv_w"], row(w["lru_conv_b"]), w["lru_w_a"], row(w["lru_b_a"]), w["lru_w_i"],
                    row(w["lru_b_i"]), row(w["lru_lambda"]))
    s["y_lru"], s["h"], s["xc"] = _lru_fwd(s["p_lru"], *s["lru_par"], name=nm("lru_fwd"))
    s["ssd_par"] = (w["ssd_conv_w"], row(w["ssd_conv_b"]), row(w["ssd_dt_bias"]), row(w["ssd_a_log"]),
                    row(w["ssd_d"]), row(w["ssd_norm_w"]))
    s["y_ssd"], s["yraw"], s["hs"], s["c_ssd"] = _ssd_fwd(s["p_ssd"], *s["ssd_par"], name=nm("ssd_fwd"))
    s["kv"] = _mm(mem, w["mem_w_kv"], name=nm("kv"))
    s["y_xa"] = _xa_fwd(s["p_q"], s["kv"], name=nm("xa_fwd"))
    s["b1"] = _mm(s["y_lru"], w["w_br_lru"], name=nm("br_lru"))
    s["b2"] = _mm(s["y_ssd"], w["w_br_ssd"], name=nm("br_ssd"))
    s["b3"] = _mm(s["y_xa"], w["w_br_xa"], name=nm("br_xa"))
    s["bg"] = row(w["b_gate"])
    s["merged"] = _merge_fwd(s["p_gl"], s["bg"], s["b1"], s["b2"], s["b3"], name=nm("merge_fwd"))
    s["mix"] = _mm(s["merged"], w["w_out"], name=nm("out_proj"))
    s["x1"] = _ln_fwd(x, s["mix"], row(w["ln1_g"]), row(w["ln1_b"]), name=nm("ln1_fwd"))
    s["gate"], s["up"], s["act"] = _ffn_in_swiglu(s["x1"], w["ffn_w_in"], name=nm("ffn_in"))
    s["f"] = _mm(s["act"], w["ffn_w_down"], name=nm("ffn_down"))
    s["x2"] = _ln_fwd(s["x1"], s["f"], row(w["ln2_g"]), row(w["ln2_b"]), name=nm("ln2_fwd"))
    return s


def _layer_bwd(s, mem, w, dxo, l, hooks=None):
    nm = lambda t: f"{t}_l{l}"
    g = {}
    hook = lambda stage, t: hooks[stage](t, g) if hooks and stage in hooks else t
    row = lambda v: v.reshape(1, -1)
    slabs = lambda a: a.reshape(N_DEV, a.shape[0] // N_DEV, a.shape[1])
    du2, dg, db = _ln_bwd(s["x1"], s["f"], dxo, row(w["ln2_g"]), name=nm("ln2_bwd"))
    g["ln2_g"], g["ln2_b"] = dg[0], db[0]
    dgate, dup = _d_swiglu(du2, w["ffn_w_down"], s["gate"], s["up"], name=nm("d_swiglu"))
    g["ffn_w_down"] = slabs(_mm(s["act"], du2, ta=True, name=nm("dw_ffn_down")))
    dx1 = _mm(dgate, w["ffn_w_in"][:, :D_FF], tb=True, add=du2, add_scale=ALPHA, name=nm("d_x1_gate"))
    dx1 = _mm(dup, w["ffn_w_in"][:, D_FF:], tb=True, add=dx1, name=nm("d_x1_up"))
    g["ffn_w_in"] = _cols_split([_mm(s["x1"], dgate, ta=True, name=nm("dw_ffn_gate")),
                                 _mm(s["x1"], dup, ta=True, name=nm("dw_ffn_up"))], name=nm("dw_ffn_in_shards"))
    du1, dg, db = _ln_bwd(s["x"], s["mix"], dx1, row(w["ln1_g"]), name=nm("ln1_bwd"))
    g["ln1_g"], g["ln1_b"] = dg[0], db[0]
    dmerged = hook("mid", _mm(du1, w["w_out"], tb=True, name=nm("d_merged")))
    g["w_out"] = slabs(_mm(s["merged"], du1, ta=True, name=nm("dw_out")))
    dp_gl, d1, d2, d3, dbg = _merge_bwd(s["p_gl"], s["bg"], s["b1"], s["b2"], s["b3"], dmerged, name=nm("merge_bwd"))
    g["b_gate"] = dbg.reshape(3, D)
    dy_lru = _mm(d1, w["w_br_lru"], tb=True, name=nm("d_y_lru"))
    g["w_br_lru"] = slabs(_mm(s["y_lru"], d1, ta=True, name=nm("dw_br_lru")))
    dy_ssd = _mm(d2, w["w_br_ssd"], tb=True, name=nm("d_y_ssd"))
    g["w_br_ssd"] = slabs(_mm(s["y_ssd"], d2, ta=True, name=nm("dw_br_ssd")))
    dy_xa = _mm(d3, w["w_br_xa"], tb=True, name=nm("d_y_xa"))
    g["w_br_xa"] = slabs(_mm(s["y_xa"], d3, ta=True, name=nm("dw_br_xa")))
    dp_q, dkv = _xa_bwd(s["p_q"], s["kv"], dy_xa, name=nm("xa_bwd"))
    g["mem_w_kv"] = _mm(mem, dkv, ta=True, split_n=2 * D // N_DEV, name=nm("dw_kv"))
    dy_ssd = hook("branches", dy_ssd)
    ssd_cw, _, *ssd_rest = s["ssd_par"]
    dp_ssd, dcw, dcb, ddtb, dalog, dd, dnw = _ssd_bwd(s["p_ssd"], s["c_ssd"], s["yraw"], s["hs"], dy_ssd, ssd_cw,
                                                      *ssd_rest, name=nm("ssd_bwd"))
    g["ssd_conv_w"], g["ssd_conv_b"], g["ssd_dt_bias"] = dcw, dcb[0], ddtb[0]
    g["ssd_a_log"], g["ssd_d"], g["ssd_norm_w"] = dalog[0], dd[0], dnw[0]
    dp_ssd = hook("ssd", dp_ssd)
    lru_cw, _, *lru_rest = s["lru_par"]
    dp_lru, dcw, dcb, dwa, dba, dwi, dbi, dlam = _lru_bwd(s["p_lru"], s["xc"], s["h"], dy_lru, lru_cw, *lru_rest,
                                                          name=nm("lru_bwd"))
    g["lru_conv_w"], g["lru_conv_b"], g["lru_w_a"], g["lru_b_a"] = dcw, dcb[0], dwa, dba[0]
    g["lru_w_i"], g["lru_b_i"], g["lru_lambda"] = dwi, dbi[0], dlam[0]
    wi, x = s["wi"], s["x"]
    g["w_in"] = _w_in_shards(_mm(x, dp_ssd, ta=True, name=nm("dw_in_ssd")), _mm(x, dp_lru, ta=True, name=nm("dw_in_lru")),
                             _mm(x, dp_q, ta=True, name=nm("dw_in_q")), _mm(x, dp_gl, ta=True, name=nm("dw_in_gl")),
                             name=nm("dw_in_shards"))
    dp_ssd = hook("weights", dp_ssd)
    dx = _mm(dp_ssd, wi["ssd"], tb=True, add=du1, add_scale=ALPHA, name=nm("dx_ssd"))
    dx = hook("dx", _mm(dp_lru, wi["lru"], tb=True, add=dx, name=nm("dx_lru")))
    dx = _mm(dp_q, wi["q"], tb=True, add=dx, name=nm("dx_q"))
    dx = _mm(dp_gl, wi["gl"], tb=True, add=dx, name=nm("dx_gl"))
    return dx, g


def _local_step(x, mem, target, layers, hooks=None):
    saved = []
    for l in range(DEPTH):
        saved.append(_layer_fwd(x, mem, layers[l], l))
        x = saved[-1]["x2"]
    dx, loss = _loss_head(x, target, name="loss_head")
    grads = [None] * DEPTH
    for l in reversed(range(DEPTH)):
        dx, grads[l] = _layer_bwd(saved[l], mem, layers[l], dx, l, hooks[l] if hooks else None)
    return loss, dx, grads


def kernel(x, mem, w_in, b_gate, lru_conv_w, lru_conv_b, lru_w_a, lru_b_a, lru_w_i, lru_b_i, lru_lambda, ssd_conv_w, ssd_conv_b, ssd_dt_bias, ssd_a_log, ssd_d, ssd_norm_w, mem_w_kv, w_br_lru, w_br_ssd, w_br_xa, w_out, ln1_g, ln1_b, ffn_w_in, ffn_w_down, ln2_g, ln2_b, loss_target, m_w_in, m_b_gate, m_lru_conv_w, m_lru_conv_b, m_lru_w_a, m_lru_b_a, m_lru_w_i, m_lru_b_i, m_lru_lambda, m_ssd_conv_w, m_ssd_conv_b, m_ssd_dt_bias, m_ssd_a_log, m_ssd_d, m_ssd_norm_w, m_mem_w_kv, m_w_br_lru, m_w_br_ssd, m_w_br_xa, m_w_out, m_ln1_g, m_ln1_b, m_ffn_w_in, m_ffn_w_down, m_ln2_g, m_ln2_b, v_w_in, v_b_gate, v_lru_conv_w, v_lru_conv_b, v_lru_w_a, v_lru_b_a, v_lru_w_i, v_lru_b_i, v_lru_lambda, v_ssd_conv_w, v_ssd_conv_b, v_ssd_dt_bias, v_ssd_a_log, v_ssd_d, v_ssd_norm_w, v_mem_w_kv, v_w_br_lru, v_w_br_ssd, v_w_br_xa, v_w_out, v_ln1_g, v_ln1_b, v_ffn_w_in, v_ffn_w_down, v_ln2_g, v_ln2_b):
    local = dict(locals())
    w = {n: local[n] for n in _ORDER}
    m = {n: local["m_" + n] for n in _ORDER}
    v = {n: local["v_" + n] for n in _ORDER}

    big = [n for n, _, _ in _BIG]
    kinds = {n: kind for n, kind, _ in _BIG}

    small = _rows(jnp.concatenate([w[n].reshape(-1) for n, _ in _SMALL]), R_SMALL)
    first = _all_gather([w["w_in"][0].astype(BF16), small], name="gather_w_in_l0")
    rest, later, _ = lax.optimization_barrier(([w[n][0].astype(BF16) for n in big[1:]],
                                               [w[n][1].astype(BF16) for n in big], first[-1]))
    rest = _all_gather(rest, "gather_weights_l0", collective_id=1)
    later = _all_gather(later, "gather_weights_l1", collective_id=4)
    stacks = [dict(zip(big, [first[0], *rest])), dict(zip(big, later))]
    small_all, o, small_full = first[-1].reshape(N_DEV, R_SMALL * LANES), 0, {}
    for n, s in _SMALL:
        k = DEPTH * s[0] * s[1]
        small_full[n] = _from_stack(small_all[:, o:o + k].reshape((N_DEV, DEPTH) + s))
        o += k
    layers = []
    for l in range(DEPTH):
        lw = {n: w[n][l] for n, _ in _REP}
        lw.update({n: small_full[n][l] for n, _ in _SMALL})
        lw["wi"] = dict(zip(("ssd", "lru", "q", "gl"), _w_in_regions(stacks[l]["w_in"], name=f"w_in_regions_l{l}")))
        for n in big[1:]:
            if kinds[n] == "col":
                lw[n] = _cols_concat(stacks[l][n], name=f"full_{n}_l{l}")
            else:
                lw[n] = stacks[l][n].reshape(-1, stacks[l][n].shape[-1])
        layers.append(lw)

    by_dest = lambda a: a.reshape((4, 2) + a.shape[1:])
    slots, pending, last_layer = {}, {}, {}
    queue = [stacks[1]["w_out"]]

    def after_last(operands):
        operands, _ = lax.optimization_barrier((list(operands), queue[-1]))
        return operands

    def start(tag, collective_id, names_and_grads):
        names, owns = zip(*names_and_grads)
        gots = _exchange_sibling(after_last(owns), name=f"reduce_cores_{tag}", collective_id=collective_id)
        queue.append(gots[0])
        pending[tag] = (names, owns, gots)

    def finish(tag, collective_id, t):
        names, owns, gots = pending.pop(tag)
        t, gots = lax.optimization_barrier((t, gots))
        sums = [_pair_sum(own, got, name=f"pair_sum_{tag}_{n}", out_dtype=F32 if n == "tail" else BF16,
                          tb=R_SM if n == "tail" else _rowblk(own.shape[2], 256))
                for n, own, got in zip(names, owns, gots)]
        t, sums = lax.optimization_barrier((t, sums))
        got = _exchange_chips(sums, name=f"reduce_chips_{tag}", collective_id=collective_id)
        queue.append(got[0])
        slots.update({(tag, n): s for n, s in zip(names, got)})
        return t

    def tail_of(g0):
        stacked = {n: jnp.stack([g0[n], last_layer[n]]) for n in [s[0] for s in _SMALL + _REP]}
        sm = jnp.concatenate([_by_dest(stacked[n]) for n, _ in _SMALL], axis=1)
        sm = jnp.pad(sm, ((0, 0), (0, R_SMALL * LANES - sm.shape[1])))
        rep = jnp.concatenate([stacked[n].reshape(-1) for n, _ in _REP])
        rep = jnp.pad(rep, (0, N_DEV * R_REP * LANES - rep.shape[0])).reshape(N_DEV, R_REP * LANES)
        tail = jnp.concatenate([sm, rep, jnp.zeros((N_DEV, (R_SM - R_SMALL - R_REP) * LANES), F32)], axis=1)
        return tail.reshape(4, 2, R_SM, LANES)

    def weights_l1(t, g):
        last_layer.update(g)
        start("l1", 2, [(n, by_dest(g[n])) for n in big])
        return t

    def branches_l0(t, g):
        start("l0a", 5, [(n, by_dest(g[n])) for n in big[1:]])
        return t

    def weights_l0(t, g):
        start("l0b", 7, [("w_in", by_dest(g["w_in"])), ("tail", tail_of(g))])
        return t

    hooks = [{"branches": branches_l0, "ssd": lambda t, g: finish("l0a", 6, t), "weights": weights_l0,
              "dx": lambda t, g: finish("l0b", 8, t)},
             {"weights": weights_l1, "dx": lambda t, g: finish("l1", 3, t)}]
    loss_tile, dx, grads = _local_step(x[0], mem[0], loss_target[0], layers, hooks)
    loss = lax.psum(loss_tile[0, 0], ("x", "y", "c"))

    res = {}
    for n in big:
        tb = _rowblk(w[n].shape[1], 128 if w[n].shape[2] > LANES else 256)
        res[n] = _adamw([slots["l0b" if n == "w_in" else "l0a", n], slots["l1", n]], w[n], m[n], v[n],
                        name=f"adamw_{n}", tb=tb)
    tail_sum = _slot_sum(slots["l0b", "tail"], name="sum_tail")
    rep_all = _all_gather([tail_sum[R_SMALL:R_SMALL + R_REP]], name="gather_replicated")[0]
    g_tail = jnp.concatenate([tail_sum[:R_SMALL], rep_all.reshape(N_DEV * R_REP, LANES)], axis=0)
    tails = _adamw([g_tail[None]], _pack_tail(w)[None], _pack_tail(m)[None], _pack_tail(v)[None],
                   name="adamw_tail", tb=TB_TAIL)

    outs = []
    for kind in range(4):
        d = {**{n: res[n][kind] for n in big}, **_unpack_tail(tails[kind][0])}
        outs += [d[n] for n in _ORDER]
    return (loss, dx[None], *outs)
```

```python
import math

import jax
import jax.numpy as jnp
from jax import lax
from jax.experimental import pallas as pl
from jax.experimental.pallas import tpu as pltpu
from jax.experimental.pallas import tpu_sc as plsc

F32 = jnp.float32
BF16 = jnp.bfloat16

D = 1024
DEPTH = 2
N_DEV = 8
CHUNK = 64
LRU_BLOCKS = 8
LRU_BLOCK = 128
LRU_C = 8.0
D_SSD = 2 * D
SSD_HEADS = 32
SSD_GROUPS = 4
GROUP_W = D_SSD // SSD_GROUPS
SSD_STATE = 128
D_XBC = D_SSD + 2 * SSD_GROUPS * SSD_STATE
XA_HEADS = 4
XA_HEAD_DIM = 256
D_FF = 2816
ALPHA = (2 * DEPTH) ** 0.25
EPS = 1e-5
N_IN = 11296

S_Z, S_XBC, S_DT, W_SSD = 0, 2048, 5120, 5632
DT_REAL = 32

ADAM_LR, ADAM_B1, ADAM_B2, ADAM_EPS, ADAM_WD, ADAM_STEP = 0.001, 0.9, 0.999, 1e-08, 0.01, 10

VMEM_LIMIT = 56 * 1024 * 1024
MESH = pl.DeviceIdType.MESH
ANY = pl.BlockSpec(memory_space=pl.ANY)


def _cp(*sem):
    return pltpu.CompilerParams(dimension_semantics=sem, vmem_limit_bytes=VMEM_LIMIT)


def _blk(n, target):
    if n % 128:
        return n
    best = 128
    for b in range(128, min(n, target) + 1, 128):
        if n % b == 0:
            best = b
    return best


def _iota(shape, dim):
    return lax.broadcasted_iota(jnp.int32, shape, dim)


def _sigmoid(x):
    return 0.5 + 0.5 * jnp.tanh(0.5 * x)


def _log1p(e):
    u = 1.0 + e
    return jnp.where(u == 1.0, e, jnp.log(u) * (e / (u - 1.0)))


def _softplus(x):
    return jnp.maximum(x, 0.0) + _log1p(jnp.exp(-jnp.abs(x)))


_G0 = math.sqrt(2.0 / math.pi)
_G1 = 0.044715


def _gelu_and_grad(x):
    x2 = x * x
    u = 0.5 + 0.5 * jnp.tanh(x * (_G0 + (_G0 * _G1) * x2))
    dg = u + (x * (u * (1.0 - u))) * ((2.0 * _G0) + (6.0 * _G0 * _G1) * x2)
    return x * u, dg


_NN = (((1,), (0,)), ((), ()))
_NT = (((1,), (1,)), ((), ()))
_TN = (((0,), (0,)), ((), ()))


def _dot(a, b, dims=_NN):
    return lax.dot_general(a.astype(BF16), b.astype(BF16), dims, preferred_element_type=F32)


def _dot_hi(a, b, dims=_NN):
    return lax.dot_general(a, b, dims, precision=lax.Precision.HIGHEST, preferred_element_type=F32)


def _split(v):
    hi = v.astype(BF16)
    return hi, (v - hi.astype(F32)).astype(BF16)


def _dot01(v, e, dims=_NN):
    hi, lo = _split(v)
    return (lax.dot_general(hi, e, dims, preferred_element_type=F32)
            + lax.dot_general(lo, e, dims, preferred_element_type=F32))


def _conv_taps(xe, n):
    return [xe[8:8 + n] if j == 3 else pltpu.roll(xe, 3 - j, 0)[8:8 + n] for j in range(4)]


def _conv_fwd(taps, cw, cb):
    return cb + cw[0:1] * taps[0] + cw[1:2] * taps[1] + cw[2:3] * taps[2] + cw[3:4] * taps[3]


def _conv_bwd(dc, dnext, x, cw, n):
    ext = jnp.concatenate([dc, dnext], axis=0)
    shifted = [pltpu.roll(ext, n + 8 - (3 - j), 0)[0:n] for j in range(3)] + [dc]
    dx = cw[0:1] * shifted[0] + cw[1:2] * shifted[1] + cw[2:3] * shifted[2] + cw[3:4] * dc
    dcw = jnp.concatenate([jnp.sum(x * shifted[j], axis=0, keepdims=True) for j in range(4)], axis=0)
    return dx, dcw, jnp.sum(dc, axis=0, keepdims=True)


MM_VMEM_BUDGET = 44 * 1024 * 1024
MM_MAX_TILE = 1408
MM_MAX_K = 5632


def _divisors(n, cap):
    return [n] if n % 128 else [b for b in range(128, min(n, cap) + 1, 128) if n % b == 0]


def _mm_tiles(m_dim, n_dim, k_dim, a_bytes, b_bytes, o_bytes, has_add, tn_fixed):
    best = None
    for tm in _divisors(m_dim, MM_MAX_TILE):
        for tn in ([tn_fixed] if tn_fixed else _divisors(n_dim, MM_MAX_TILE)):
            for tk in _divisors(k_dim, MM_MAX_K):
                vmem = 2 * (tm * tk * a_bytes + tk * tn * b_bytes + tm * tn * (o_bytes + (4 if has_add else 0)))
                vmem += tm * tn * 4 if tk < k_dim else 0
                if vmem <= MM_VMEM_BUDGET:
                    key = (tm * tn * tk, tk, tn)
                    if best is None or key > best[0]:
                        best = (key, (tm, tn, tk))
    assert best is not None, (m_dim, n_dim, k_dim)
    return best[1]


def _mm(a, b, *, ta=False, tb=False, out_dtype=F32, add=None, add_scale=1.0, name, split_n=None):
    if ta:
        k_dim, m_dim = a.shape
    else:
        m_dim, k_dim = a.shape
    if tb:
        n_dim, k2 = b.shape
    else:
        k2, n_dim = b.shape
    assert k_dim == k2, (a.shape, b.shape, ta, tb)
    tm, tn, tk = _mm_tiles(m_dim, n_dim, k_dim, a.dtype.itemsize, b.dtype.itemsize, jnp.dtype(out_dtype).itemsize,
                           add is not None, split_n)
    nk = k_dim // tk
    a_spec = pl.BlockSpec((tk, tm), lambda i, j, k: (k, i)) if ta else pl.BlockSpec((tm, tk), lambda i, j, k: (i, k))
    b_spec = pl.BlockSpec((tn, tk), lambda i, j, k: (j, k)) if tb else pl.BlockSpec((tk, tn), lambda i, j, k: (k, j))
    o_spec = pl.BlockSpec((tm, tn), lambda i, j, k: (i, j))
    out_shape = (m_dim, n_dim)
    if split_n is not None:
        assert add is None and tn == split_n, (tn, split_n)
        o_spec = pl.BlockSpec((None, tm, tn), lambda i, j, k: (j, i, 0))
        out_shape = (n_dim // tn, m_dim, tn)
    dims = (((0 if ta else 1,), (1 if tb else 0,)), ((), ()))
    has_add = add is not None

    def body(*refs):
        a_ref, b_ref = refs[:2]
        add_ref = refs[2] if has_add else None
        o_ref = refs[3] if has_add else refs[2]
        acc_ref = refs[-1] if nk > 1 else None
        k = pl.program_id(2)

        def product():
            return lax.dot_general(a_ref[...].astype(BF16), b_ref[...].astype(BF16), dims, preferred_element_type=F32)

        def finish(r):
            if has_add:
                r = r + add_scale * add_ref[...]
            o_ref[...] = r.astype(out_dtype)

        if nk == 1:
            finish(product())
            return

        @pl.when(k == 0)
        def _():
            acc_ref[...] = product()

        @pl.when((k > 0) & (k < nk - 1))
        def _():
            acc_ref[...] += product()

        @pl.when(k == nk - 1)
        def _():
            finish(acc_ref[...] + product())

    in_specs = [a_spec, b_spec] + ([o_spec] if has_add else [])
    args = (a, b) + ((add,) if has_add else ())
    return pl.pallas_call(
        body, name=name, grid=(m_dim // tm, n_dim // tn, nk),
        in_specs=in_specs, out_specs=o_spec,
        out_shape=jax.ShapeDtypeStruct(out_shape, out_dtype),
        scratch_shapes=[pltpu.VMEM((tm, tn), F32)] if nk > 1 else [],
        cost_estimate=pl.CostEstimate(
            flops=2 * m_dim * n_dim * k_dim, transcendentals=0,
            bytes_accessed=a.size * a.dtype.itemsize + b.size * b.dtype.itemsize
            + m_dim * n_dim * (jnp.dtype(out_dtype).itemsize + (4 if has_add else 0))),
        compiler_params=_cp("parallel", "parallel", "arbitrary"),
    )(*args)


def _ln_fwd(x, f, g, b, name, tb=512):
    t = x.shape[0]
    tb = min(tb, t)

    def body(x_ref, f_ref, g_ref, b_ref, o_ref, ob_ref):
        u = ALPHA * x_ref[...] + f_ref[...]
        mu = jnp.mean(u, axis=-1, keepdims=True)
        d = u - mu
        var = jnp.mean(d * d, axis=-1, keepdims=True)
        y = d * lax.rsqrt(var + EPS) * g_ref[...] + b_ref[...]
        o_ref[...] = y
        ob_ref[...] = y.astype(BF16)

    row = pl.BlockSpec((tb, D), lambda i: (i, 0))
    par = pl.BlockSpec((1, D), lambda i: (0, 0))
    return pl.pallas_call(
        body, name=name, grid=(t // tb,), in_specs=[row, row, par, par], out_specs=[row, row],
        out_shape=[jax.ShapeDtypeStruct((t, D), F32), jax.ShapeDtypeStruct((t, D), BF16)],
        compiler_params=_cp("parallel"),
    )(x, f, g, b)


def _ln_bwd(x, f, dy, g, name, tb=512):
    t = x.shape[0]
    tb = min(tb, t)

    def body(x_ref, f_ref, dy_ref, g_ref, du_ref, dg_ref, db_ref):
        @pl.when(pl.program_id(0) == 0)
        def _():
            dg_ref[...] = jnp.zeros_like(dg_ref)
            db_ref[...] = jnp.zeros_like(db_ref)

        u = ALPHA * x_ref[...] + f_ref[...]
        mu = jnp.mean(u, axis=-1, keepdims=True)
        d = u - mu
        var = jnp.mean(d * d, axis=-1, keepdims=True)
        rstd = lax.rsqrt(var + EPS)
        xhat = d * rstd
        dy = dy_ref[...]
        dxh = dy * g_ref[...]
        m1 = jnp.mean(dxh, axis=-1, keepdims=True)
        m2 = jnp.mean(dxh * xhat, axis=-1, keepdims=True)
        du_ref[...] = rstd * (dxh - m1 - xhat * m2)
        dg_ref[...] += jnp.sum(dy * xhat, axis=0, keepdims=True)
        db_ref[...] += jnp.sum(dy, axis=0, keepdims=True)

    row = pl.BlockSpec((tb, D), lambda i: (i, 0))
    par = pl.BlockSpec((1, D), lambda i: (0, 0))
    return pl.pallas_call(
        body, name=name, grid=(t // tb,), in_specs=[row, row, row, par], out_specs=[row, par, par],
        out_shape=[jax.ShapeDtypeStruct((t, D), F32), jax.ShapeDtypeStruct((1, D), F32),
                   jax.ShapeDtypeStruct((1, D), F32)],
        compiler_params=_cp("arbitrary"),
    )(x, f, dy, g)


FFN_TM, FFN_TN = 512, D_FF // 2


def _ffn_in_swiglu(x, w, name):
    t = x.shape[0]
    tm = min(FFN_TM, t)
    nj = D_FF // FFN_TN

    def body(x_ref, wg_ref, wu_ref, g_ref, u_ref, a_ref):
        xb = x_ref[...].astype(BF16)
        g = lax.dot_general(xb, wg_ref[...], _NN, preferred_element_type=F32)
        u = lax.dot_general(xb, wu_ref[...], _NN, preferred_element_type=F32)
        g_ref[...] = g.astype(BF16)
        u_ref[...] = u.astype(BF16)
        a_ref[...] = (g * _sigmoid(g) * u).astype(BF16)

    tile = pl.BlockSpec((tm, FFN_TN), lambda i, j: (i, j))
    return pl.pallas_call(
        body, name=name, grid=(t // tm, nj),
        in_specs=[pl.BlockSpec((tm, D), lambda i, j: (i, 0)), pl.BlockSpec((D, FFN_TN), lambda i, j: (0, j)),
                  pl.BlockSpec((D, FFN_TN), lambda i, j: (0, nj + j))],
        out_specs=[tile, tile, tile],
        out_shape=[jax.ShapeDtypeStruct((t, D_FF), BF16)] * 3,
        compiler_params=_cp("parallel", "parallel"),
    )(x, w, w)


def _d_swiglu(du, w_down, g, u, name):
    t = du.shape[0]
    tm = min(FFN_TM, t)

    def body(du_ref, w_ref, g_ref, u_ref, dg_ref, dup_ref):
        da = lax.dot_general(du_ref[...].astype(BF16), w_ref[...], _NT, preferred_element_type=F32)
        g_v = g_ref[...].astype(F32)
        s = _sigmoid(g_v)
        dg_ref[...] = (da * u_ref[...].astype(F32) * (s * (1.0 + g_v * (1.0 - s)))).astype(BF16)
        dup_ref[...] = (da * g_v * s).astype(BF16)

    tile = pl.BlockSpec((tm, FFN_TN), lambda i, j: (i, j))
    return pl.pallas_call(
        body, name=name, grid=(t // tm, D_FF // FFN_TN),
        in_specs=[pl.BlockSpec((tm, D), lambda i, j: (i, 0)), pl.BlockSpec((FFN_TN, D), lambda i, j: (j, 0)), tile, tile],
        out_specs=[tile, tile],
        out_shape=[jax.ShapeDtypeStruct((t, D_FF), BF16), jax.ShapeDtypeStruct((t, D_FF), BF16)],
        compiler_params=_cp("parallel", "parallel"),
    )(du, w_down, g, u)


def _merge_fwd(pgl, bg, b1, b2, b3, name, tb=512):
    t = pgl.shape[0]
    tb = min(tb, t)

    def body(gl_ref, bg_ref, b1_ref, b2_ref, b3_ref, o_ref):
        acc = None
        for j, b_ref in enumerate((b1_ref, b2_ref, b3_ref)):
            sl = slice(j * D, (j + 1) * D)
            term = _sigmoid(gl_ref[:, sl] + bg_ref[:, sl]) * b_ref[...].astype(F32)
            acc = term if acc is None else acc + term
        o_ref[...] = acc.astype(BF16)

    row = pl.BlockSpec((tb, D), lambda i: (i, 0))
    return pl.pallas_call(
        body, name=name, grid=(t // tb,),
        in_specs=[pl.BlockSpec((tb, 3 * D), lambda i: (i, 0)), pl.BlockSpec((1, 3 * D), lambda i: (0, 0)), row, row, row],
        out_specs=row, out_shape=jax.ShapeDtypeStruct((t, D), BF16), compiler_params=_cp("parallel"),
    )(pgl, bg, b1, b2, b3)


def _merge_bwd(pgl, bg, b1, b2, b3, dm, name, tb=512):
    t = pgl.shape[0]
    tb = min(tb, t)

    def body(gl_ref, bg_ref, b1_ref, b2_ref, b3_ref, dm_ref, dgl_ref, d1_ref, d2_ref, d3_ref, dbg_ref):
        @pl.when(pl.program_id(0) == 0)
        def _():
            dbg_ref[...] = jnp.zeros_like(dbg_ref)

        dm_v = dm_ref[...]
        for j, (b_ref, d_ref) in enumerate(((b1_ref, d1_ref), (b2_ref, d2_ref), (b3_ref, d3_ref))):
            sl = slice(j * D, (j + 1) * D)
            gate = _sigmoid(gl_ref[:, sl] + bg_ref[:, sl])
            d_ref[...] = (dm_v * gate).astype(BF16)
            dgl = dm_v * b_ref[...].astype(F32) * (gate * (1.0 - gate))
            dgl_ref[:, sl] = dgl.astype(BF16)
            dbg_ref[:, sl] += jnp.sum(dgl, axis=0, keepdims=True)

    row = pl.BlockSpec((tb, D), lambda i: (i, 0))
    wide = pl.BlockSpec((tb, 3 * D), lambda i: (i, 0))
    par = pl.BlockSpec((1, 3 * D), lambda i: (0, 0))
    return pl.pallas_call(
        body, name=name, grid=(t // tb,),
        in_specs=[wide, par, row, row, row, row], out_specs=[wide, row, row, row, par],
        out_shape=[jax.ShapeDtypeStruct((t, 3 * D), BF16)] + [jax.ShapeDtypeStruct((t, D), BF16)] * 3
                  + [jax.ShapeDtypeStruct((1, 3 * D), F32)],
        compiler_params=_cp("arbitrary"),
    )(pgl, bg, b1, b2, b3, dm)


def _xa_probs(q, kv_ref, hd):
    sl = slice(hd * XA_HEAD_DIM, (hd + 1) * XA_HEAD_DIM)
    k = kv_ref[:, sl]
    v = kv_ref[:, D + hd * XA_HEAD_DIM:D + (hd + 1) * XA_HEAD_DIM]
    s = _dot(q[:, sl], k, _NT) * (XA_HEAD_DIM ** -0.5)
    e = jnp.exp(s - jnp.max(s, axis=1, keepdims=True))
    return sl, k, v, e / jnp.sum(e, axis=1, keepdims=True)


def _xa_fwd(pq, kv, name, tb=512):
    t = pq.shape[0]
    tb = min(tb, t)

    def body(q_ref, kv_ref, o_ref):
        q = q_ref[...]
        for hd in range(XA_HEADS):
            sl, _, v, p = _xa_probs(q, kv_ref, hd)
            o_ref[:, sl] = _dot(p, v).astype(BF16)

    row = pl.BlockSpec((tb, D), lambda i: (i, 0))
    return pl.pallas_call(
        body, name=name, grid=(t // tb,),
        in_specs=[row, pl.BlockSpec(kv.shape, lambda i: (0, 0))], out_specs=row,
        out_shape=jax.ShapeDtypeStruct((t, D), BF16), compiler_params=_cp("parallel"),
    )(pq, kv)


def _xa_bwd(pq, kv, dy, name, tb=512):
    t = pq.shape[0]
    tb = min(tb, t)

    def body(q_ref, kv_ref, dy_ref, dq_ref, dkv_ref):
        @pl.when(pl.program_id(0) == 0)
        def _():
            dkv_ref[...] = jnp.zeros_like(dkv_ref)

        q = q_ref[...]
        for hd in range(XA_HEADS):
            sl, k, v, p = _xa_probs(q, kv_ref, hd)
            dyh = dy_ref[:, sl]
            vsl = slice(D + hd * XA_HEAD_DIM, D + (hd + 1) * XA_HEAD_DIM)
            dkv_ref[:, vsl] += _dot(p, dyh, _TN)
            dp = _dot(dyh, v, _NT)
            ds = p * (dp - jnp.sum(dp * p, axis=1, keepdims=True)) * (XA_HEAD_DIM ** -0.5)
            dq_ref[:, sl] = _dot(ds, k).astype(BF16)
            dkv_ref[:, sl] += _dot(ds, q[:, sl], _TN)

    row = pl.BlockSpec((tb, D), lambda i: (i, 0))
    kvs = pl.BlockSpec(kv.shape, lambda i: (0, 0))
    return pl.pallas_call(
        body, name=name, grid=(t // tb,), in_specs=[row, kvs, row], out_specs=[row, kvs],
        out_shape=[jax.ShapeDtypeStruct((t, D), BF16), jax.ShapeDtypeStruct(kv.shape, F32)],
        compiler_params=_cp("arbitrary"),
    )(pq, kv, dy)


SUBLANES = 8


def _scan(a, u, reverse):
    n, c = a.shape
    groups = n // SUBLANES
    a = a.reshape(groups, SUBLANES, c)
    u = u.reshape(groups, SUBLANES, c)
    sub = _iota((1, SUBLANES, 1), 1)
    d = 1
    while d < SUBLANES:
        keep = (sub < SUBLANES - d) if reverse else (sub >= d)
        shift = SUBLANES - d if reverse else d
        u = a * jnp.where(keep, pltpu.roll(u, shift, 1), 0.0) + u
        a = a * jnp.where(keep, pltpu.roll(a, shift, 1), 1.0)
        d *= 2
    edge = 0 if reverse else SUBLANES - 1
    out, carry = [None] * groups, None
    for j in (reversed(range(groups)) if reverse else range(groups)):
        out[j] = u[j] if carry is None else u[j] + a[j] * carry
        carry = out[j][edge:edge + 1]
    return jnp.concatenate(out, axis=0)


def _lru_gates(xc, wa_ref, ba, wi_ref, bi, lam):
    za = jnp.concatenate([_dot(xc[:, n * 128:(n + 1) * 128], wa_ref[n]) for n in range(LRU_BLOCKS)], axis=1) + ba
    zi = jnp.concatenate([_dot(xc[:, n * 128:(n + 1) * 128], wi_ref[n]) for n in range(LRU_BLOCKS)], axis=1) + bi
    r = 1.0 / (1.0 + jnp.exp(-za))
    ig = _sigmoid(zi)
    sp = _softplus(-lam)
    log_a = (-LRU_C) * r * sp
    a = jnp.exp(log_a)
    m = jnp.sqrt(-jnp.tanh(log_a) * (1.0 + a * a))
    u = m * (ig * xc)
    return a, u, r, ig, m, sp


def _lru_fwd(p, cw, cb, wa, ba, wi, bi, lam, name, tb=256):
    t = p.shape[0]
    tb = min(tb, t)
    nb = t // tb
    r8 = tb // 8

    def body(x_ref, xp_ref, g_ref, cw_ref, cb_ref, wa_ref, ba_ref, wi_ref, bi_ref, lam_ref, y_ref, h_ref, xc_ref,
             hc_ref):
        i = pl.program_id(0)

        @pl.when(i == 0)
        def _():
            hc_ref[...] = jnp.zeros_like(hc_ref)

        halo = jnp.where(i == 0, 0.0, xp_ref[...])
        taps = _conv_taps(jnp.concatenate([halo, x_ref[...]], axis=0), tb)
        xc = _conv_fwd(taps, cw_ref[...], cb_ref[...])
        xc_ref[...] = xc
        a, u, _, _, _, _ = _lru_gates(xc, wa_ref, ba_ref[...], wi_ref, bi_ref[...], lam_ref[...])
        row = _iota((tb, 1), 0)
        u = u + jnp.where(row == 0, a * hc_ref[...], 0.0)
        h = _scan(a, u, reverse=False)
        h_ref[...] = h
        hc_ref[...] = h[tb - 1:tb, :]
        gl, _ = _gelu_and_grad(g_ref[...])
        y_ref[...] = (gl * h).astype(BF16)

    par = pl.BlockSpec((1, D), lambda i: (0, 0))
    wsp = pl.BlockSpec((LRU_BLOCKS, LRU_BLOCK, LRU_BLOCK), lambda i: (0, 0, 0))
    row = pl.BlockSpec((tb, D), lambda i: (i, 0))
    return pl.pallas_call(
        body, name=name, grid=(nb,),
        in_specs=[row, pl.BlockSpec((8, D), lambda i: (jnp.maximum(i * r8 - 1, 0), 0)),
                  pl.BlockSpec((tb, D), lambda i: (i, 1)),
                  pl.BlockSpec((4, D), lambda i: (0, 0)), par, wsp, par, wsp, par, par],
        out_specs=[row, row, row],
        out_shape=[jax.ShapeDtypeStruct((t, D), BF16), jax.ShapeDtypeStruct((t, D), F32),
                   jax.ShapeDtypeStruct((t, D), F32)],
        scratch_shapes=[pltpu.VMEM((1, D), F32)],
        compiler_params=_cp("arbitrary"),
    )(p, p, p, cw, cb, wa, ba, wi, bi, lam)


def _lru_bwd(p, xc, h, dy, cw, wa, ba, wi, bi, lam, name, tb=256):
    t = p.shape[0]
    tb = min(tb, t)
    nb = t // tb
    r8 = tb // 8

    def body(x_ref, g_ref, xc_ref, h_ref, hp_ref, dy_ref, cw_ref, wa_ref, ba_ref, wi_ref, bi_ref, lam_ref,
             dp_ref, dcw_ref, dcb_ref, dwa_ref, dba_ref, dwi_ref, dbi_ref, dlam_ref, carry_ref, dnext_ref):
        i = pl.program_id(0)
        blk = nb - 1 - i

        @pl.when(i == 0)
        def _():
            for r in (dcw_ref, dcb_ref, dwa_ref, dba_ref, dwi_ref, dbi_ref, dlam_ref, carry_ref, dnext_ref):
                r[...] = jnp.zeros_like(r)

        xc = xc_ref[...]
        lam = lam_ref[...]
        a, _, r, ig, m, sp = _lru_gates(xc, wa_ref, ba_ref[...], wi_ref, bi_ref[...], lam)
        gl, dgl = _gelu_and_grad(g_ref[...])
        h = h_ref[...]
        dy = dy_ref[...]
        dp_ref[:, D:] = (dy * h * dgl).astype(BF16)
        row = _iota((tb, 1), 0)
        dh = dy * gl + jnp.where(row == tb - 1, carry_ref[...], 0.0)
        b = jnp.where(row < tb - 1, pltpu.roll(a, tb - 1, 0), 0.0)
        gs = _scan(b, dh, reverse=True)
        carry_ref[...] = a[0:1] * gs[0:1]
        h_last = jnp.where(blk == 0, 0.0, hp_ref[7:8, :])
        hprev = jnp.where(row == 0, h_last, pltpu.roll(h, 1, 0))
        da = gs * hprev
        dm = gs * ig * xc
        di = gs * m * xc
        dxc = gs * m * ig
        dlog = a * (da - a * (dm / m))
        dr = dlog * ((-LRU_C) * sp)
        dsp = jnp.sum(dlog * ((-LRU_C) * r), axis=0, keepdims=True)
        dlam_ref[...] += dsp * (-_sigmoid(-lam))
        dza = dr * r * (1.0 - r)
        dzi = di * ig * (1.0 - ig)
        dba_ref[...] += jnp.sum(dza, axis=0, keepdims=True)
        dbi_ref[...] += jnp.sum(dzi, axis=0, keepdims=True)
        parts = []
        for n in range(LRU_BLOCKS):
            sl = slice(n * 128, (n + 1) * 128)
            dwa_ref[n] += _dot(xc[:, sl], dza[:, sl], _TN)
            dwi_ref[n] += _dot(xc[:, sl], dzi[:, sl], _TN)
            parts.append(_dot(dza[:, sl], wa_ref[n], _NT) + _dot(dzi[:, sl], wi_ref[n], _NT))
        dxc = dxc + jnp.concatenate(parts, axis=1)
        dx, dcw, dcb = _conv_bwd(dxc, dnext_ref[...], x_ref[...], cw_ref[...], tb)
        dp_ref[:, :D] = dx.astype(BF16)
        dcw_ref[...] += dcw
        dcb_ref[...] += dcb
        dnext_ref[...] = dxc[0:8]

    par = pl.BlockSpec((1, D), lambda i: (0, 0))
    wsp = pl.BlockSpec((LRU_BLOCKS, LRU_BLOCK, LRU_BLOCK), lambda i: (0, 0, 0))
    cws = pl.BlockSpec((4, D), lambda i: (0, 0))
    rev = lambda i: nb - 1 - i
    blk0 = pl.BlockSpec((tb, D), lambda i: (rev(i), 0))
    w_shape = jax.ShapeDtypeStruct((LRU_BLOCKS, LRU_BLOCK, LRU_BLOCK), F32)
    v_shape = jax.ShapeDtypeStruct((1, D), F32)
    return pl.pallas_call(
        body, name=name, grid=(nb,),
        in_specs=[blk0, pl.BlockSpec((tb, D), lambda i: (rev(i), 1)), blk0, blk0,
                  pl.BlockSpec((8, D), lambda i: (jnp.maximum(rev(i) * r8 - 1, 0), 0)), blk0,
                  cws, wsp, par, wsp, par, par],
        out_specs=[pl.BlockSpec((tb, 2 * D), lambda i: (rev(i), 0)), cws, par, wsp, par, wsp, par, par],
        out_shape=[jax.ShapeDtypeStruct((t, 2 * D), BF16), jax.ShapeDtypeStruct((4, D), F32), v_shape,
                   w_shape, v_shape, w_shape, v_shape, v_shape],
        scratch_shapes=[pltpu.VMEM((1, D), F32), pltpu.VMEM((8, D), F32)],
        compiler_params=_cp("arbitrary"),
    )(p, p, xc, h, h, dy, cw, wa, ba, wi, bi, lam)


def _ssd_consts():
    m0 = _iota((1, 128), 1) < 64
    e = (jnp.right_shift(_iota((SSD_HEADS, D_SSD), 1), 6) == _iota((SSD_HEADS, D_SSD), 0)).astype(BF16)
    tril = (_iota((CHUNK, CHUNK), 0) >= _iota((CHUNK, CHUNK), 1)).astype(F32)
    eye = (_iota((SSD_HEADS, SSD_HEADS), 0) == _iota((SSD_HEADS, SSD_HEADS), 1)).astype(F32)
    r2 = _iota((CHUNK, 128), 0)
    c2 = jnp.bitwise_and(_iota((CHUNK, 128), 1), 63)
    return dict(m0=m0, e=e, tril=tril, eye=eye, causal2=r2 >= c2, fold=(c2 == r2).astype(BF16))


def _ssd_pre(c, p_ref, dtb_ref, alog_ref, dvec_ref, k):
    sg = _sigmoid(c)
    xbc = c * sg
    dtp = p_ref[:, S_DT:S_DT + DT_REAL] + dtb_ref[...]
    dt = _softplus(dtp)
    a = -jnp.exp(alog_ref[...])
    cs = _dot_hi(k["tril"], dt * a)
    cs_last = cs[CHUNK - 1:CHUNK]
    dend = jnp.exp(cs_last - cs)
    cdec = jnp.exp(cs_last)
    big = _dot01(jnp.concatenate([dt, jnp.exp(cs), dend], axis=0), k["e"])
    small = _dot01(jnp.concatenate([jnp.broadcast_to(cdec, (8, SSD_HEADS)),
                                    jnp.broadcast_to(dvec_ref[...], (8, SSD_HEADS))], axis=0), k["e"])
    cst2 = _dot_hi(k["eye"], jnp.concatenate([cs, cs], axis=0), _NT)
    return dict(c=c, sg=sg, xs=xbc[:, :D_SSD], bm=xbc[:, D_SSD:D_SSD + 512],
                cm=xbc[:, D_SSD + 512:], dtp=dtp, dt=dt, a=a, cs=cs, dend=dend, cdec=cdec,
                dtx=big[0:CHUNK], ecx=big[CHUNK:2 * CHUNK], dex=big[2 * CHUNK:3 * CHUNK],
                cdx=small[0:1], ddx=small[8:9], cst2=cst2)


def _pair_decay(p, cs, cst2, k):
    h0, h1 = 2 * p, 2 * p + 1
    colp = jnp.where(k["m0"], cs[:, h0:h0 + 1], cs[:, h1:h1 + 1])
    rowp = jnp.where(k["m0"], cst2[h0:h0 + 1, :], cst2[h1:h1 + 1, :])
    return jnp.where(k["causal2"], jnp.exp(colp - rowp), 0.0)


def _pair_stack(xp, k):
    return jnp.concatenate([jnp.where(k["m0"], xp, 0.0), jnp.where(k["m0"], 0.0, xp)], axis=0)


def _group_norm(yz, nw, with_stats=False):
    outs, stats = [], []
    for g in range(SSD_GROUPS):
        yzg = yz[:, g * GROUP_W:(g + 1) * GROUP_W]
        r = lax.rsqrt(jnp.mean(yzg * yzg, axis=1, keepdims=True) + EPS)
        outs.append(yzg * r)
        stats.append(r)
    y = jnp.concatenate(outs, axis=1) * nw
    return (y, stats) if with_stats else y


def _ssd_fwd(p, cw, cb, dtb, alog, dvec, nw, name):
    t = p.shape[0]
    nc = t // CHUNK

    def body(p_ref, pp_ref, cw_ref, cb_ref, dtb_ref, alog_ref, dvec_ref, nw_ref, y_ref, yraw_ref, hs_ref, c_ref,
             h_scr):
        i = pl.program_id(0)

        @pl.when(i == 0)
        def _():
            h_scr[...] = jnp.zeros_like(h_scr)

        k = _ssd_consts()
        halo = jnp.where(i == 0, 0.0, pp_ref[:, S_XBC:S_DT])
        taps = _conv_taps(jnp.concatenate([halo, p_ref[:, S_XBC:S_DT]], axis=0), CHUNK)
        c = _conv_fwd(taps, cw_ref[...], cb_ref[...])
        c_ref[...] = c
        s = _ssd_pre(c, p_ref, dtb_ref, alog_ref, dvec_ref, k)
        xs, bm, cm = s["xs"], s["bm"], s["cm"]
        xdt = xs * s["dtx"]
        hprev = h_scr[...]
        hs_ref[0] = hprev
        ys, hn = [], []
        for g in range(SSD_GROUPS):
            gs = slice(g * GROUP_W, (g + 1) * GROUP_W)
            bg = bm[:, g * 128:(g + 1) * 128]
            cg = cm[:, g * 128:(g + 1) * 128]
            cbdup = _dot(cg, jnp.concatenate([bg, bg], axis=0), _NT)
            hp_g = hprev[:, gs]
            yd = []
            for q in range(4):
                pr = g * 4 + q
                mp = cbdup * _pair_decay(pr, s["cs"], s["cst2"], k)
                yd.append(_dot(mp, _pair_stack(xdt[:, pr * 128:(pr + 1) * 128], k)))
            ys.append(jnp.concatenate(yd, axis=1) + _dot(cg, hp_g) * s["ecx"][:, gs])
            hn.append(hp_g * s["cdx"][:, gs] + _dot(bg, xdt[:, gs] * s["dex"][:, gs], _TN))
        h_scr[...] = jnp.concatenate(hn, axis=1)
        yraw = jnp.concatenate(ys, axis=1) + s["ddx"] * xs
        yraw_ref[...] = yraw
        z = p_ref[:, S_Z:S_Z + D_SSD]
        y_ref[...] = _group_norm(yraw * (z * _sigmoid(z)), nw_ref[...]).astype(BF16)

    hv = pl.BlockSpec((1, DT_REAL), lambda i: (0, 0))
    return pl.pallas_call(
        body, name=name, grid=(nc,),
        in_specs=[pl.BlockSpec((CHUNK, W_SSD), lambda i: (i, 0)),
                  pl.BlockSpec((8, W_SSD), lambda i: (jnp.maximum(i * (CHUNK // 8) - 1, 0), 0)),
                  pl.BlockSpec((4, D_XBC), lambda i: (0, 0)), pl.BlockSpec((1, D_XBC), lambda i: (0, 0)),
                  hv, hv, hv, pl.BlockSpec((1, D_SSD), lambda i: (0, 0))],
        out_specs=[pl.BlockSpec((CHUNK, D_SSD), lambda i: (i, 0)), pl.BlockSpec((CHUNK, D_SSD), lambda i: (i, 0)),
                   pl.BlockSpec((1, SSD_STATE, D_SSD), lambda i: (i, 0, 0)),
                   pl.BlockSpec((CHUNK, D_XBC), lambda i: (i, 0))],
        out_shape=[jax.ShapeDtypeStruct((t, D_SSD), BF16), jax.ShapeDtypeStruct((t, D_SSD), F32),
                   jax.ShapeDtypeStruct((nc, SSD_STATE, D_SSD), F32), jax.ShapeDtypeStruct((t, D_XBC), F32)],
        scratch_shapes=[pltpu.VMEM((SSD_STATE, D_SSD), F32)],
        compiler_params=_cp("arbitrary"),
    )(p, p, cw, cb, dtb, alog, dvec, nw)


def _ssd_bwd(p, c, yraw, hs, dy, cw, dtb, alog, dvec, nw, name):
    t = p.shape[0]
    nc = t // CHUNK

    def body(p_ref, c_ref, yraw_ref, hs_ref, dy_ref, cw_ref, dtb_ref, alog_ref, dvec_ref, nw_ref,
             dp_ref, dcw_ref, dcb_ref, ddtb_ref, dalog_ref, dd_ref, dnw_ref, dh_scr, dnext_scr):
        i = pl.program_id(0)

        @pl.when(i == 0)
        def _():
            for r in (dcw_ref, dcb_ref, ddtb_ref, dalog_ref, dd_ref, dnw_ref, dh_scr, dnext_scr):
                r[...] = jnp.zeros_like(r)

        k = _ssd_consts()
        s = _ssd_pre(c_ref[...], p_ref, dtb_ref, alog_ref, dvec_ref, k)
        xs, bm, cm, cs, dt, a = s["xs"], s["bm"], s["cm"], s["cs"], s["dt"], s["a"]
        m0 = k["m0"]
        xdt = xs * s["dtx"]
        hprev = hs_ref[0]
        dh = dh_scr[...]

        nw_v = nw_ref[...]
        yraw = yraw_ref[...]
        z = p_ref[:, S_Z:S_Z + D_SSD]
        sz = _sigmoid(z)
        siluz = z * sz
        yz = yraw * siluz
        dyo = dy_ref[...]
        dyn = dyo * nw_v
        dyz_parts, dnw_parts = [], []
        for g in range(SSD_GROUPS):
            gs = slice(g * GROUP_W, (g + 1) * GROUP_W)
            yzg = yz[:, gs]
            r = lax.rsqrt(jnp.mean(yzg * yzg, axis=1, keepdims=True) + EPS)
            dnw_parts.append(jnp.sum(dyo[:, gs] * yzg * r, axis=0, keepdims=True))
            dyz_parts.append(r * dyn[:, gs] - yzg * (r * r * r) * jnp.mean(dyn[:, gs] * yzg, axis=1, keepdims=True))
        dnw_ref[...] += jnp.concatenate(dnw_parts, axis=1)
        dyz = jnp.concatenate(dyz_parts, axis=1)
        d_y = dyz * siluz
        dp_ref[:, S_Z:S_Z + D_SSD] = (dyz * yraw * (sz * (1.0 + z * (1.0 - sz)))).astype(BF16)
        dd_row = jnp.sum(d_y * xs, axis=0, keepdims=True)
        dxs = d_y * s["ddx"]

        lane_h = _iota((1, SSD_HEADS), 1)
        sub_h = _iota((SSD_HEADS, 1), 0)
        dcs = jnp.zeros((CHUNK, SSD_HEADS), F32)
        dcst2 = jnp.zeros((SSD_HEADS, 128), F32)
        dxdt_parts, db_parts, dc_parts, dhp_parts, yoff_parts, dend_parts, dcd_parts = [], [], [], [], [], [], []
        for g in range(SSD_GROUPS):
            gs = slice(g * GROUP_W, (g + 1) * GROUP_W)
            bg = bm[:, g * 128:(g + 1) * 128]
            cg = cm[:, g * 128:(g + 1) * 128]
            bdup = jnp.concatenate([bg, bg], axis=0)
            cbdup = _dot(cg, bdup, _NT)
            dcb2 = jnp.zeros((CHUNK, 128), F32)
            dxp_parts = []
            for q in range(4):
                pr = g * 4 + q
                h0, h1 = 2 * pr, 2 * pr + 1
                lp = _pair_decay(pr, cs, s["cst2"], k)
                mp = cbdup * lp
                xst = _pair_stack(xdt[:, pr * 128:(pr + 1) * 128], k)
                dyp = d_y[:, pr * 128:(pr + 1) * 128]
                dmp = _dot(dyp, xst, _NT)
                dxst = _dot(mp, dyp, _TN)
                dxp_parts.append(jnp.where(m0, dxst[:CHUNK], dxst[CHUNK:]))
                dcb2 = dcb2 + dmp * lp
                dlm = dmp * mp
                rs0 = jnp.sum(jnp.where(m0, dlm, 0.0), axis=1, keepdims=True)
                rs1 = jnp.sum(jnp.where(m0, 0.0, dlm), axis=1, keepdims=True)
                dcs = dcs + jnp.where(lane_h == h0, rs0, 0.0) + jnp.where(lane_h == h1, rs1, 0.0)
                colsum = jnp.sum(dlm, axis=0, keepdims=True)
                sel = ((sub_h == h0) & m0) | ((sub_h == h1) & jnp.logical_not(m0))
                dcst2 = dcst2 - jnp.where(sel, colsum, 0.0)
            dcg = _dot(dcb2, bdup)
            dbdup = _dot(dcb2, cg, _TN)
            dbg = dbdup[:CHUNK] + dbdup[CHUNK:]
            hp_g = hprev[:, gs]
            zoff = _dot(cg, hp_g)
            dzo = d_y[:, gs] * s["ecx"][:, gs]
            dcg = dcg + _dot(dzo, hp_g, _NT)
            dh_g = dh[:, gs]
            dhp_parts.append(_dot(cg, dzo, _TN) + dh_g * s["cdx"][:, gs])
            dcd_parts.append(jnp.sum(dh_g * hp_g, axis=0, keepdims=True))
            wg = xdt[:, gs] * s["dex"][:, gs]
            dbg = dbg + _dot(wg, dh_g, _NT)
            dwg = _dot(bg, dh_g)
            dxdt_parts.append(jnp.concatenate(dxp_parts, axis=1) + dwg * s["dex"][:, gs])
            dend_g = dwg * wg
            dend_parts.append(jnp.sum(dend_g, axis=0, keepdims=True))
            yoff_parts.append(dzo * zoff - dend_g)
            db_parts.append(dbg)
            dc_parts.append(dcg)
        dh_scr[...] = jnp.concatenate(dhp_parts, axis=1)
        dxdt = jnp.concatenate(dxdt_parts, axis=1)
        sums = _dot01(jnp.concatenate([jnp.concatenate(yoff_parts, axis=1), dxdt * xs], axis=0), k["e"], _NT)
        rows8 = jnp.concatenate([jnp.broadcast_to(jnp.concatenate(r, axis=1), (8, D_SSD))
                                 for r in (dcd_parts, [dd_row], dend_parts)], axis=0)
        small = _dot01(rows8, k["e"], _NT)
        dd_ref[...] += small[8:9]
        dcs_last = small[0:1] * s["cdec"] + small[16:17]
        hi, lo = _split(dcst2)
        dcs = (dcs + sums[0:CHUNK]
               + lax.dot_general(k["fold"], hi, _NT, preferred_element_type=F32)
               + lax.dot_general(k["fold"], lo, _NT, preferred_element_type=F32)
               + jnp.where(_iota((CHUNK, 1), 0) == CHUNK - 1, dcs_last, 0.0))
        dda = _dot_hi(k["tril"], dcs, _TN)
        ddt = dda * a + sums[CHUNK:2 * CHUNK]
        dalog_ref[...] += jnp.sum(dda * dt, axis=0, keepdims=True) * a
        dxs = dxs + dxdt * s["dtx"]
        draw = ddt * _sigmoid(s["dtp"])
        ddtb_ref[...] += jnp.sum(draw, axis=0, keepdims=True)
        dp_ref[:, S_DT:] = jnp.zeros((CHUNK, W_SSD - S_DT), BF16)
        dp_ref[:, S_DT:S_DT + DT_REAL] = draw.astype(BF16)
        dxbc = jnp.concatenate([dxs] + db_parts + dc_parts, axis=1)
        sg, c = s["sg"], s["c"]
        dc = dxbc * (sg * (1.0 + c * (1.0 - sg)))
        dx, dcw, dcb = _conv_bwd(dc, dnext_scr[...], p_ref[:, S_XBC:S_DT], cw_ref[...], CHUNK)
        dp_ref[:, S_XBC:S_DT] = dx.astype(BF16)
        dcw_ref[...] += dcw
        dcb_ref[...] += dcb
        dnext_scr[...] = dc[0:8]

    rev = lambda i: nc - 1 - i
    hv = pl.BlockSpec((1, DT_REAL), lambda i: (0, 0))
    cws = pl.BlockSpec((4, D_XBC), lambda i: (0, 0))
    cbs = pl.BlockSpec((1, D_XBC), lambda i: (0, 0))
    nws = pl.BlockSpec((1, D_SSD), lambda i: (0, 0))
    wide = pl.BlockSpec((CHUNK, D_SSD), lambda i: (rev(i), 0))
    hshape = jax.ShapeDtypeStruct((1, DT_REAL), F32)
    return pl.pallas_call(
        body, name=name, grid=(nc,),
        in_specs=[pl.BlockSpec((CHUNK, W_SSD), lambda i: (rev(i), 0)),
                  pl.BlockSpec((CHUNK, D_XBC), lambda i: (rev(i), 0)),
                  wide, pl.BlockSpec((1, SSD_STATE, D_SSD), lambda i: (rev(i), 0, 0)), wide,
                  cws, hv, hv, hv, nws],
        out_specs=[pl.BlockSpec((CHUNK, W_SSD), lambda i: (rev(i), 0)), cws, cbs, hv, hv, hv, nws],
        out_shape=[jax.ShapeDtypeStruct((t, W_SSD), BF16), jax.ShapeDtypeStruct((4, D_XBC), F32),
                   jax.ShapeDtypeStruct((1, D_XBC), F32), hshape, hshape, hshape,
                   jax.ShapeDtypeStruct((1, D_SSD), F32)],
        scratch_shapes=[pltpu.VMEM((SSD_STATE, D_SSD), F32), pltpu.VMEM((8, D_XBC), F32)],
        compiler_params=_cp("arbitrary"),
    )(p, c, yraw, hs, dy, cw, dtb, alog, dvec, nw)


def _loss_head(y, target, name, tb=512):
    t = y.shape[0]
    tb = min(tb, t)

    def body(y_ref, t_ref, dy_ref, l_ref):
        @pl.when(pl.program_id(0) == 0)
        def _():
            l_ref[...] = jnp.zeros_like(l_ref)

        e = y_ref[...] - t_ref[...]
        dy_ref[...] = e * (1.0 / D)
        l_ref[...] += jnp.sum(jnp.sum(e * e, axis=1, keepdims=True), axis=0, keepdims=True) * (0.5 / D)

    row = pl.BlockSpec((tb, D), lambda i: (i, 0))
    return pl.pallas_call(
        body, name=name, grid=(t // tb,), in_specs=[row, row],
        out_specs=[row, pl.BlockSpec((8, 128), lambda i: (0, 0))],
        out_shape=[jax.ShapeDtypeStruct((t, D), F32), jax.ShapeDtypeStruct((8, 128), F32)],
        compiler_params=_cp("arbitrary"),
    )(y, target)


def _adamw(slots, w, m, v, name, tb):
    nl = len(slots)
    ns, r, c = slots[0].shape
    assert r % tb == 0 and w.shape == (nl, r, c), (r, tb, w.shape)

    def body(*refs):
        s_refs = refs[:nl]
        w_ref, m_ref, v_ref, g_ref, d_ref, m2_ref, v2_ref = refs[nl:]

        def total(ref):
            acc = ref[0].astype(F32)
            for j in range(1, ns):
                acc = acc + ref[j].astype(F32)
            return acc

        g = total(s_refs[0])
        for layer in range(1, nl):
            g = jnp.where(pl.program_id(0) == layer, total(s_refs[layer]), g)
        m2 = ADAM_B1 * m_ref[...] + (1.0 - ADAM_B1) * g
        v2 = ADAM_B2 * v_ref[...] + (1.0 - ADAM_B2) * (g * g)
        m_hat = m2 / (1.0 - ADAM_B1 ** ADAM_STEP)
        v_hat = v2 / (1.0 - ADAM_B2 ** ADAM_STEP)
        g_ref[...] = g
        d_ref[...] = -ADAM_LR * (m_hat / (jnp.sqrt(v_hat) + ADAM_EPS) + ADAM_WD * w_ref[...])
        m2_ref[...] = m2
        v2_ref[...] = v2

    def slot_spec(layer):
        return pl.BlockSpec((ns, tb, c), lambda l, i: (0, jnp.where(l == layer, i, 0), 0))

    row = pl.BlockSpec((None, tb, c), lambda l, i: (l, i, 0))
    shp = jax.ShapeDtypeStruct((nl, r, c), F32)
    return pl.pallas_call(
        body, name=name, grid=(nl, r // tb),
        in_specs=[slot_spec(layer) for layer in range(nl)] + [row, row, row],
        out_specs=[row, row, row, row], out_shape=[shp, shp, shp, shp], compiler_params=_cp("arbitrary", "arbitrary"),
    )(*slots, w, m, v)


def _pair_sum(own, got, name, out_dtype, tb):
    nj, _, r, c = own.shape
    mc = lax.axis_index("c")

    def body(mc_ref, a_ref, b_ref, o_ref):
        del mc_ref
        o_ref[...] = (a_ref[...] + b_ref[...]).astype(out_dtype)

    return pl.pallas_call(
        body, name=name,
        grid_spec=pltpu.PrefetchScalarGridSpec(
            num_scalar_prefetch=1, grid=(nj, r // tb),
            in_specs=[pl.BlockSpec((None, None, tb, c), lambda j, i, mc_ref: (j, mc_ref[0], i, 0)),
                      pl.BlockSpec((None, tb, c), lambda j, i, mc_ref: (j, i, 0))],
            out_specs=pl.BlockSpec((None, tb, c), lambda j, i, mc_ref: (j, i, 0))),
        out_shape=jax.ShapeDtypeStruct((nj, r, c), out_dtype), compiler_params=_cp("parallel", "parallel"),
    )(jnp.reshape(mc, (1,)).astype(jnp.int32), own, got)


def _slot_sum(slots, name):
    ns, r, c = slots.shape

    def body(s_ref, o_ref):
        g = s_ref[0]
        for j in range(1, ns):
            g = g + s_ref[j]
        o_ref[...] = g

    return pl.pallas_call(body, name=name, out_shape=jax.ShapeDtypeStruct((r, c), F32))(slots)


def _position():
    return lax.axis_index("x"), lax.axis_index("y"), lax.axis_index("c")


def _comm(exchange, peers, xs, out_shapes, sems, name, collective_id):
    n = len(xs)
    if collective_id is None:
        def body(*refs):
            exchange(refs[:n], refs[n:n + len(out_shapes)], *refs[n + len(out_shapes):])

        return pl.pallas_call(body, name=name, in_specs=[ANY] * n, out_specs=[ANY] * len(out_shapes),
                              out_shape=out_shapes, scratch_shapes=sems)(*xs)
    def launch(*refs):
        barrier = pltpu.get_barrier_semaphore()
        to = peers(*_position())
        for peer in to:
            pl.semaphore_signal(barrier, inc=1, device_id=peer, device_id_type=MESH)
        pl.semaphore_wait(barrier, len(to))
        exchange(refs[:n], refs[n:n + len(out_shapes)], *refs[n + len(out_shapes):])

    return pl.kernel(launch, out_type=out_shapes, mesh=plsc.ScalarSubcoreMesh(axis_name="seq", num_cores=1), name=name,
                     scratch_types=sems, compiler_params=pltpu.CompilerParams(collective_id=collective_id))(*xs)


def _all_gather(xs, name, collective_id=None):
    n = len(xs)
    return _comm(_gather_body, lambda x, y, c: [(x, y, 1 - c), (1 - x, y, c), (x, 1 - y, c), (1 - x, 1 - y, c)], xs,
                 [jax.ShapeDtypeStruct((N_DEV,) + x.shape, x.dtype) for x in xs],
                 [pltpu.SemaphoreType.DMA((n, 7)), pltpu.SemaphoreType.DMA((n, 7)), pltpu.SemaphoreType.DMA((n,))],
                 name, collective_id)


def _gather_body(x_refs, out_refs, send_sems, recv_sems, local_sems):
    n = len(x_refs)
    mx, my, mc = _position()
    me, sibling = (mx, my, mc), (mx, my, 1 - mc)
    chips = [(1 - mx, my), (mx, 1 - my), (1 - mx, 1 - my)]

    def copy(a, k, block, to, own=False):
        dst = out_refs[a].at[4 * block[0] + 2 * block[1] + block[2]]
        return pltpu.make_async_remote_copy(
            src_ref=x_refs[a] if own else dst, dst_ref=dst,
            send_sem=send_sems.at[a, k], recv_sem=recv_sems.at[a, k], device_id=to, device_id_type=MESH)

    mine = [pltpu.make_async_copy(x_refs[a], out_refs[a].at[4 * mx + 2 * my + mc], local_sems.at[a]) for a in range(n)]
    first = [copy(a, 1 + j, me, (*chip, mc), own=True) for j, chip in enumerate(chips) for a in range(n)]
    first += [copy(a, 0, me, sibling, own=True) for a in range(n)]
    for cp in first + mine:
        cp.start()
    passed = []
    for j, chip in enumerate(chips):
        for a in range(n):
            copy(a, 1 + j, (*chip, mc), me).wait_recv()
            passed.append(copy(a, 4 + j, (*chip, mc), sibling))
            passed[-1].start()
    for a in range(n):
        copy(a, 0, sibling, me).wait_recv()
    for j, chip in enumerate(chips):
        for a in range(n):
            copy(a, 4 + j, (*chip, 1 - mc), me).wait_recv()
    for cp in first + passed:
        cp.wait_send()
    for cp in mine:
        cp.wait()


def _exchange_sibling(gs, name, collective_id=None):
    n = len(gs)

    def exchange(g_refs, r_refs, send_sems, recv_sems):
        mx, my, mc = _position()
        cps = [pltpu.make_async_remote_copy(src_ref=g_refs[a].at[:, 1 - mc], dst_ref=r_refs[a],
                                            send_sem=send_sems.at[a], recv_sem=recv_sems.at[a],
                                            device_id=(mx, my, 1 - mc), device_id_type=MESH) for a in range(n)]
        for cp in cps:
            cp.start()
        for cp in cps:
            cp.wait()

    return _comm(exchange, lambda x, y, c: [(x, y, 1 - c)], gs,
                 [jax.ShapeDtypeStruct(g.shape[:1] + g.shape[2:], g.dtype) for g in gs],
                 [pltpu.SemaphoreType.DMA((n,)), pltpu.SemaphoreType.DMA((n,))], name, collective_id)


def _exchange_chips(ss, name, collective_id=None):
    n = len(ss)

    def exchange(s_refs, r_refs, send_sems, recv_sems, local_sems):
        mx, my, mc = _position()
        my_chip = 2 * mx + my
        chips = [(1 - mx, my), (mx, 1 - my), (1 - mx, 1 - my)]

        def copy(a, k, to_slot):
            px, py = chips[k]
            return pltpu.make_async_remote_copy(
                src_ref=s_refs[a].at[2 * px + py], dst_ref=r_refs[a].at[to_slot], send_sem=send_sems.at[a, k],
                recv_sem=recv_sems.at[a, k], device_id=(px, py, mc), device_id_type=MESH)

        sends = [copy(a, k, my_chip) for k in range(3) for a in range(n)]
        local = [pltpu.make_async_copy(s_refs[a].at[my_chip], r_refs[a].at[my_chip], local_sems.at[a])
                 for a in range(n)]
        for cp in sends + local:
            cp.start()
        for k in range(3):
            px, py = chips[k]
            for a in range(n):
                copy(a, k, 2 * px + py).wait_recv()
        for cp in sends:
            cp.wait_send()
        for cp in local:
            cp.wait()

    return _comm(exchange, lambda x, y, c: [(1 - x, y, c), (x, 1 - y, c), (1 - x, 1 - y, c)], ss,
                 [jax.ShapeDtypeStruct(s.shape, s.dtype) for s in ss],
                 [pltpu.SemaphoreType.DMA((n, 3)), pltpu.SemaphoreType.DMA((n, 3)), pltpu.SemaphoreType.DMA((n,))],
                 name, collective_id)


def _cols_concat(g, name, tb=128):
    _, k_dim, n = g.shape

    def body(g_ref, o_ref):
        o_ref[...] = jnp.concatenate([g_ref[d] for d in range(N_DEV)], axis=1)

    return pl.pallas_call(
        body, name=name, grid=(k_dim // tb,),
        in_specs=[pl.BlockSpec((N_DEV, tb, n), lambda i: (0, i, 0))],
        out_specs=pl.BlockSpec((tb, N_DEV * n), lambda i: (i, 0)),
        out_shape=jax.ShapeDtypeStruct((k_dim, N_DEV * n), g.dtype), compiler_params=_cp("parallel"),
    )(g)


def _cols_split(parts, name, tb=128):
    k_dim = parts[0].shape[0]
    n = sum(p.shape[1] for p in parts) // N_DEV

    def body(*refs):
        full = jnp.concatenate([r[...] for r in refs[:-1]], axis=1)
        for d in range(N_DEV):
            refs[-1][d] = full[:, d * n:(d + 1) * n]

    return pl.pallas_call(
        body, name=name, grid=(k_dim // tb,),
        in_specs=[pl.BlockSpec((tb, p.shape[1]), lambda i: (i, 0)) for p in parts],
        out_specs=pl.BlockSpec((N_DEV, tb, n), lambda i: (0, i, 0)),
        out_shape=jax.ShapeDtypeStruct((N_DEV, k_dim, n), parts[0].dtype), compiler_params=_cp("parallel"),
    )(*parts)


_Q0, _GL0 = 7200, 8224
N_SHARD_IN = N_IN // N_DEV


def _w_in_regions(g, name, tb=128):
    def body(g_ref, ssd_ref, lru_ref, q_ref, gl_ref):
        full = jnp.concatenate([g_ref[d] for d in range(N_DEV)], axis=1)
        lru_ref[...] = full[:, 0:2 * D]
        ssd_ref[:, :S_DT] = full[:, 2 * D:2 * D + S_DT]
        ssd_ref[:, S_DT:] = jnp.zeros((tb, W_SSD - S_DT), g.dtype)
        ssd_ref[:, S_DT:S_DT + DT_REAL] = full[:, 2 * D + S_DT:_Q0]
        q_ref[...] = full[:, _Q0:_GL0]
        gl_ref[...] = full[:, _GL0:N_IN]

    widths = (W_SSD, 2 * D, D, 3 * D)
    return pl.pallas_call(
        body, name=name, grid=(D // tb,),
        in_specs=[pl.BlockSpec((N_DEV, tb, N_SHARD_IN), lambda i: (0, i, 0))],
        out_specs=[pl.BlockSpec((tb, wd), lambda i: (i, 0)) for wd in widths],
        out_shape=[jax.ShapeDtypeStruct((D, wd), g.dtype) for wd in widths], compiler_params=_cp("parallel"),
    )(g)


def _w_in_shards(dssd, dlru, dq, dgl, name, tb=128):
    def body(ssd_ref, lru_ref, q_ref, gl_ref, o_ref):
        full = jnp.concatenate([lru_ref[...], ssd_ref[:, :S_DT + DT_REAL], q_ref[...], gl_ref[...]], axis=1)
        for d in range(N_DEV):
            o_ref[d] = full[:, d * N_SHARD_IN:(d + 1) * N_SHARD_IN]

    return pl.pallas_call(
        body, name=name, grid=(D // tb,),
        in_specs=[pl.BlockSpec((tb, a.shape[1]), lambda i: (i, 0)) for a in (dssd, dlru, dq, dgl)],
        out_specs=pl.BlockSpec((N_DEV, tb, N_SHARD_IN), lambda i: (0, i, 0)),
        out_shape=jax.ShapeDtypeStruct((N_DEV, D, N_SHARD_IN), F32), compiler_params=_cp("parallel"),
    )(dssd, dlru, dq, dgl)


_BIG = (("w_in", "col", (1024, 1412)), ("mem_w_kv", "col", (1024, 256)), ("w_br_lru", "row", (128, 1024)),
        ("w_br_ssd", "row", (256, 1024)), ("w_br_xa", "row", (128, 1024)), ("w_out", "row", (128, 1024)),
        ("ffn_w_in", "col", (1024, 704)), ("ffn_w_down", "row", (352, 1024)))
_SMALL = (("b_gate", (3, 128)), ("lru_conv_w", (4, 128)), ("ssd_conv_w", (4, 384)))
_REP = (("lru_conv_b", (1024,)), ("lru_w_a", (8, 128, 128)), ("lru_b_a", (1024,)), ("lru_w_i", (8, 128, 128)),
        ("lru_b_i", (1024,)), ("lru_lambda", (1024,)), ("ssd_conv_b", (3072,)), ("ssd_dt_bias", (32,)),
        ("ssd_a_log", (32,)), ("ssd_d", (32,)), ("ssd_norm_w", (2048,)), ("ln1_g", (1024,)), ("ln1_b", (1024,)),
        ("ln2_g", (1024,)), ("ln2_b", (1024,)))
_ORDER = ("w_in", "b_gate", "lru_conv_w", "lru_conv_b", "lru_w_a", "lru_b_a", "lru_w_i", "lru_b_i", "lru_lambda",
          "ssd_conv_w", "ssd_conv_b", "ssd_dt_bias", "ssd_a_log", "ssd_d", "ssd_norm_w", "mem_w_kv", "w_br_lru",
          "w_br_ssd", "w_br_xa", "w_out", "ln1_g", "ln1_b", "ffn_w_in", "ffn_w_down", "ln2_g", "ln2_b")

LANES = 1024
N_SMALL = sum(DEPTH * s[0] * s[1] for _, s in _SMALL)
R_SMALL = 8
N_REP = sum(DEPTH * math.prod(s) for _, s in _REP)
R_REP = 68
R_SM = R_SMALL + R_REP + 4
R_TAIL = R_SMALL + N_DEV * R_REP
TB_TAIL = 184
assert N_SMALL <= R_SMALL * LANES and N_REP <= N_DEV * R_REP * LANES


def _rows(flat, rows):
    return jnp.pad(flat, (0, rows * LANES - flat.shape[0])).reshape(rows, LANES)


def _rowblk(a, cap):
    return max(b for b in range(16, cap + 1, 16) if a % b == 0)


def _pack_tail(d):
    small = jnp.concatenate([d[n].reshape(-1) for n, _ in _SMALL])
    rep = jnp.concatenate([d[n].reshape(-1) for n, _ in _REP])
    return jnp.concatenate([_rows(small, R_SMALL), _rows(rep, N_DEV * R_REP)], axis=0)


def _unpack_tail(a):
    out, o = {}, 0
    flat = a[:R_SMALL].reshape(-1)
    for n, s in _SMALL:
        k = DEPTH * math.prod(s)
        out[n] = flat[o:o + k].reshape((DEPTH,) + s)
        o += k
    flat, o = a[R_SMALL:].reshape(-1), 0
    for n, s in _REP:
        k = DEPTH * math.prod(s)
        out[n] = flat[o:o + k].reshape((DEPTH,) + s)
        o += k
    return out


def _by_dest(g):
    g = g.reshape(g.shape[:-1] + (N_DEV, g.shape[-1] // N_DEV))
    return jnp.moveaxis(g, -2, 0).reshape(N_DEV, -1)


def _from_stack(st):
    st = jnp.moveaxis(st, 0, -2)
    return st.reshape(st.shape[:-2] + (st.shape[-2] * st.shape[-1],))


def _layer_fwd(x, xb, mem, w, l):
    nm = lambda s: f"{s}_l{l}"
    wi = w["wi"]
    row = lambda v: v.reshape(1, -1)
    s = dict(x=x, xb=xb, wi=wi)
    s["p_ssd"] = _mm(xb, wi["ssd"], name=nm("proj_ssd"))
    s["p_lru"] = _mm(xb, wi["lru"], name=nm("proj_lru"))
    s["p_q"] = _mm(xb, wi["q"], name=nm("proj_q"))
    s["p_gl"] = _mm(xb, wi["gl"], name=nm("proj_gl"))
    s["lru_par"] = (w["lru_conv_w"], row(w["lru_conv_b"]), w["lru_w_a"], row(w["lru_b_a"]), w["lru_w_i"],
                    row(w["lru_b_i"]), row(w["lru_lambda"]))
    s["y_lru"], s["h"], s["xc"] = _lru_fwd(s["p_lru"], *s["lru_par"], name=nm("lru_fwd"))
    s["ssd_par"] = (w["ssd_conv_w"], row(w["ssd_conv_b"]), row(w["ssd_dt_bias"]), row(w["ssd_a_log"]),
                    row(w["ssd_d"]), row(w["ssd_norm_w"]))
    s["y_ssd"], s["yraw"], s["hs"], s["c_ssd"] = _ssd_fwd(s["p_ssd"], *s["ssd_par"], name=nm("ssd_fwd"))
    s["kv"] = _mm(mem, w["mem_w_kv"], name=nm("kv"))
    s["y_xa"] = _xa_fwd(s["p_q"], s["kv"], name=nm("xa_fwd"))
    s["b1"] = _mm(s["y_lru"], w["w_br_lru"], out_dtype=BF16, name=nm("br_lru"))
    s["b2"] = _mm(s["y_ssd"], w["w_br_ssd"], out_dtype=BF16, name=nm("br_ssd"))
    s["b3"] = _mm(s["y_xa"], w["w_br_xa"], out_dtype=BF16, name=nm("br_xa"))
    s["bg"] = row(w["b_gate"])
    s["merged"] = _merge_fwd(s["p_gl"], s["bg"], s["b1"], s["b2"], s["b3"], name=nm("merge_fwd"))
    s["mix"] = _mm(s["merged"], w["w_out"], name=nm("out_proj"))
    s["x1"], s["x1b"] = _ln_fwd(x, s["mix"], row(w["ln1_g"]), row(w["ln1_b"]), name=nm("ln1_fwd"))
    s["gate"], s["up"], s["act"] = _ffn_in_swiglu(s["x1b"], w["ffn_w_in"], name=nm("ffn_in"))
    s["f"] = _mm(s["act"], w["ffn_w_down"], name=nm("ffn_down"))
    s["x2"], s["x2b"] = _ln_fwd(s["x1"], s["f"], row(w["ln2_g"]), row(w["ln2_b"]), name=nm("ln2_fwd"))
    return s


def _layer_bwd(s, mem, w, dxo, l, hooks=None):
    nm = lambda t: f"{t}_l{l}"
    g = {}
    hook = lambda stage, t: hooks[stage](t, g) if hooks and stage in hooks else t
    row = lambda v: v.reshape(1, -1)
    slabs = lambda a: a.reshape(N_DEV, a.shape[0] // N_DEV, a.shape[1])
    du2, dg, db = _ln_bwd(s["x1"], s["f"], dxo, row(w["ln2_g"]), name=nm("ln2_bwd"))
    g["ln2_g"], g["ln2_b"] = dg[0], db[0]
    dgate, dup = _d_swiglu(du2, w["ffn_w_down"], s["gate"], s["up"], name=nm("d_swiglu"))
    g["ffn_w_down"] = slabs(_mm(s["act"], du2, ta=True, name=nm("dw_ffn_down")))
    dx1 = _mm(dgate, w["ffn_w_in"][:, :D_FF], tb=True, add=du2, add_scale=ALPHA, name=nm("d_x1_gate"))
    dx1 = _mm(dup, w["ffn_w_in"][:, D_FF:], tb=True, add=dx1, name=nm("d_x1_up"))
    g["ffn_w_in"] = _cols_split([_mm(s["x1b"], dgate, ta=True, name=nm("dw_ffn_gate")),
                                 _mm(s["x1b"], dup, ta=True, name=nm("dw_ffn_up"))], name=nm("dw_ffn_in_shards"))
    du1, dg, db = _ln_bwd(s["x"], s["mix"], dx1, row(w["ln1_g"]), name=nm("ln1_bwd"))
    g["ln1_g"], g["ln1_b"] = dg[0], db[0]
    dmerged = hook("mid", _mm(du1, w["w_out"], tb=True, name=nm("d_merged")))
    g["w_out"] = slabs(_mm(s["merged"], du1, ta=True, name=nm("dw_out")))
    dp_gl, d1, d2, d3, dbg = _merge_bwd(s["p_gl"], s["bg"], s["b1"], s["b2"], s["b3"], dmerged, name=nm("merge_bwd"))
    g["b_gate"] = dbg.reshape(3, D)
    dy_lru = _mm(d1, w["w_br_lru"], tb=True, name=nm("d_y_lru"))
    g["w_br_lru"] = slabs(_mm(s["y_lru"], d1, ta=True, name=nm("dw_br_lru")))
    dy_ssd = _mm(d2, w["w_br_ssd"], tb=True, name=nm("d_y_ssd"))
    g["w_br_ssd"] = slabs(_mm(s["y_ssd"], d2, ta=True, name=nm("dw_br_ssd")))
    dy_xa = _mm(d3, w["w_br_xa"], tb=True, name=nm("d_y_xa"))
    g["w_br_xa"] = slabs(_mm(s["y_xa"], d3, ta=True, name=nm("dw_br_xa")))
    dp_q, dkv = _xa_bwd(s["p_q"], s["kv"], dy_xa, name=nm("xa_bwd"))
    g["mem_w_kv"] = _mm(mem, dkv, ta=True, split_n=2 * D // N_DEV, name=nm("dw_kv"))
    dy_ssd = hook("branches", dy_ssd)
    ssd_cw, _, *ssd_rest = s["ssd_par"]
    dp_ssd, dcw, dcb, ddtb, dalog, dd, dnw = _ssd_bwd(s["p_ssd"], s["c_ssd"], s["yraw"], s["hs"], dy_ssd, ssd_cw,
                                                      *ssd_rest, name=nm("ssd_bwd"))
    g["ssd_conv_w"], g["ssd_conv_b"], g["ssd_dt_bias"] = dcw, dcb[0], ddtb[0]
    g["ssd_a_log"], g["ssd_d"], g["ssd_norm_w"] = dalog[0], dd[0], dnw[0]
    dp_ssd = hook("ssd", dp_ssd)
    lru_cw, _, *lru_rest = s["lru_par"]
    dp_lru, dcw, dcb, dwa, dba, dwi, dbi, dlam = _lru_bwd(s["p_lru"], s["xc"], s["h"], dy_lru, lru_cw, *lru_rest,
                                                          name=nm("lru_bwd"))
    g["lru_conv_w"], g["lru_conv_b"], g["lru_w_a"], g["lru_b_a"] = dcw, dcb[0], dwa, dba[0]
    g["lru_w_i"], g["lru_b_i"], g["lru_lambda"] = dwi, dbi[0], dlam[0]
    wi, x = s["wi"], s["xb"]
    g["w_in"] = _w_in_shards(_mm(x, dp_ssd, ta=True, name=nm("dw_in_ssd")), _mm(x, dp_lru, ta=True, name=nm("dw_in_lru")),
                             _mm(x, dp_q, ta=True, name=nm("dw_in_q")), _mm(x, dp_gl, ta=True, name=nm("dw_in_gl")),
                             name=nm("dw_in_shards"))
    dp_ssd = hook("weights", dp_ssd)
    dx = _mm(dp_ssd, wi["ssd"], tb=True, add=du1, add_scale=ALPHA, name=nm("dx_ssd"))
    dx = hook("dx", _mm(dp_lru, wi["lru"], tb=True, add=dx, name=nm("dx_lru")))
    dx = _mm(dp_q, wi["q"], tb=True, add=dx, name=nm("dx_q"))
    dx = _mm(dp_gl, wi["gl"], tb=True, add=dx, name=nm("dx_gl"))
    return dx, g


def _local_step(x, mem, target, layers, hooks=None):
    saved, xb = [], x.astype(BF16)
    for l in range(DEPTH):
        saved.append(_layer_fwd(x, xb, mem, layers[l], l))
        x, xb = saved[-1]["x2"], saved[-1]["x2b"]
    dx, loss = _loss_head(x, target, name="loss_head")
    grads = [None] * DEPTH
    for l in reversed(range(DEPTH)):
        dx, grads[l] = _layer_bwd(saved[l], mem, layers[l], dx, l, hooks[l] if hooks else None)
    return loss, dx, grads


def kernel(x, mem, w_in, b_gate, lru_conv_w, lru_conv_b, lru_w_a, lru_b_a, lru_w_i, lru_b_i, lru_lambda, ssd_conv_w, ssd_conv_b, ssd_dt_bias, ssd_a_log, ssd_d, ssd_norm_w, mem_w_kv, w_br_lru, w_br_ssd, w_br_xa, w_out, ln1_g, ln1_b, ffn_w_in, ffn_w_down, ln2_g, ln2_b, loss_target, m_w_in, m_b_gate, m_lru_conv_w, m_lru_conv_b, m_lru_w_a, m_lru_b_a, m_lru_w_i, m_lru_b_i, m_lru_lambda, m_ssd_conv_w, m_ssd_conv_b, m_ssd_dt_bias, m_ssd_a_log, m_ssd_d, m_ssd_norm_w, m_mem_w_kv, m_w_br_lru, m_w_br_ssd, m_w_br_xa, m_w_out, m_ln1_g, m_ln1_b, m_ffn_w_in, m_ffn_w_down, m_ln2_g, m_ln2_b, v_w_in, v_b_gate, v_lru_conv_w, v_lru_conv_b, v_lru_w_a, v_lru_b_a, v_lru_w_i, v_lru_b_i, v_lru_lambda, v_ssd_conv_w, v_ssd_conv_b, v_ssd_dt_bias, v_ssd_a_log, v_ssd_d, v_ssd_norm_w, v_mem_w_kv, v_w_br_lru, v_w_br_ssd, v_w_br_xa, v_w_out, v_ln1_g, v_ln1_b, v_ffn_w_in, v_ffn_w_down, v_ln2_g, v_ln2_b):
    local = dict(locals())
    w = {n: local[n] for n in _ORDER}
    m = {n: local["m_" + n] for n in _ORDER}
    v = {n: local["v_" + n] for n in _ORDER}

    big = [n for n, _, _ in _BIG]
    kinds = {n: kind for n, kind, _ in _BIG}

    small = _rows(jnp.concatenate([w[n].reshape(-1) for n, _ in _SMALL]), R_SMALL)
    first = _all_gather([w["w_in"][0].astype(BF16), small], name="gather_w_in_l0")
    rest, later, _ = lax.optimization_barrier(([w[n][0].astype(BF16) for n in big[1:]],
                                               [w[n][1].astype(BF16) for n in big], first[-1]))
    rest = _all_gather(rest, "gather_weights_l0", collective_id=1)
    later = _all_gather(later, "gather_weights_l1", collective_id=4)
    stacks = [dict(zip(big, [first[0], *rest])), dict(zip(big, later))]
    small_all, o, small_full = first[-1].reshape(N_DEV, R_SMALL * LANES), 0, {}
    for n, s in _SMALL:
        k = DEPTH * s[0] * s[1]
        small_full[n] = _from_stack(small_all[:, o:o + k].reshape((N_DEV, DEPTH) + s))
        o += k
    layers = []
    for l in range(DEPTH):
        lw = {n: w[n][l] for n, _ in _REP}
        lw.update({n: small_full[n][l] for n, _ in _SMALL})
        lw["wi"] = dict(zip(("ssd", "lru", "q", "gl"), _w_in_regions(stacks[l]["w_in"], name=f"w_in_regions_l{l}")))
        for n in big[1:]:
            if kinds[n] == "col":
                lw[n] = _cols_concat(stacks[l][n], name=f"full_{n}_l{l}")
            else:
                lw[n] = stacks[l][n].reshape(-1, stacks[l][n].shape[-1])
        layers.append(lw)

    by_dest = lambda a: a.reshape((4, 2) + a.shape[1:])
    slots, pending, last_layer = {}, {}, {}
    queue = [stacks[1]["w_out"]]

    def after_last(operands):
        operands, _ = lax.optimization_barrier((list(operands), queue[-1]))
        return operands

    def start(tag, collective_id, names_and_grads):
        names, owns = zip(*names_and_grads)
        gots = _exchange_sibling(after_last(owns), name=f"reduce_cores_{tag}", collective_id=collective_id)
        queue.append(gots[0])
        pending[tag] = (names, owns, gots)

    def finish(tag, collective_id, t):
        names, owns, gots = pending.pop(tag)
        t, gots = lax.optimization_barrier((t, gots))
        sums = [_pair_sum(own, got, name=f"pair_sum_{tag}_{n}", out_dtype=F32 if n == "tail" else BF16,
                          tb=R_SM if n == "tail" else _rowblk(own.shape[2], 256))
                for n, own, got in zip(names, owns, gots)]
        t, sums = lax.optimization_barrier((t, sums))
        got = _exchange_chips(sums, name=f"reduce_chips_{tag}", collective_id=collective_id)
        queue.append(got[0])
        slots.update({(tag, n): s for n, s in zip(names, got)})
        return t

    def tail_of(g0):
        stacked = {n: jnp.stack([g0[n], last_layer[n]]) for n in [s[0] for s in _SMALL + _REP]}
        sm = jnp.concatenate([_by_dest(stacked[n]) for n, _ in _SMALL], axis=1)
        sm = jnp.pad(sm, ((0, 0), (0, R_SMALL * LANES - sm.shape[1])))
        rep = jnp.concatenate([stacked[n].reshape(-1) for n, _ in _REP])
        rep = jnp.pad(rep, (0, N_DEV * R_REP * LANES - rep.shape[0])).reshape(N_DEV, R_REP * LANES)
        tail = jnp.concatenate([sm, rep, jnp.zeros((N_DEV, (R_SM - R_SMALL - R_REP) * LANES), F32)], axis=1)
        return tail.reshape(4, 2, R_SM, LANES)

    def weights_l1(t, g):
        last_layer.update(g)
        start("l1", 2, [(n, by_dest(g[n])) for n in big])
        return t

    def branches_l0(t, g):
        start("l0a", 5, [(n, by_dest(g[n])) for n in big[1:]])
        return t

    def weights_l0(t, g):
        start("l0b", 7, [("w_in", by_dest(g["w_in"])), ("tail", tail_of(g))])
        return t

    hooks = [{"branches": branches_l0, "ssd": lambda t, g: finish("l0a", 6, t), "weights": weights_l0,
              "dx": lambda t, g: finish("l0b", 8, t)},
             {"weights": weights_l1, "dx": lambda t, g: finish("l1", 3, t)}]
    loss_tile, dx, grads = _local_step(x[0], mem[0], loss_target[0], layers, hooks)
    loss = lax.psum(loss_tile[0, 0], ("x", "y", "c"))

    res = {}
    for n in big:
        tb = _rowblk(w[n].shape[1], 128 if w[n].shape[2] > LANES else 256)
        res[n] = _adamw([slots["l0b" if n == "w_in" else "l0a", n], slots["l1", n]], w[n], m[n], v[n],
                        name=f"adamw_{n}", tb=tb)
    tail_sum = _slot_sum(slots["l0b", "tail"], name="sum_tail")
    rep_all = _all_gather([tail_sum[R_SMALL:R_SMALL + R_REP]], name="gather_replicated")[0]
    g_tail = jnp.concatenate([tail_sum[:R_SMALL], rep_all.reshape(N_DEV * R_REP, LANES)], axis=0)
    tails = _adamw([g_tail[None]], _pack_tail(w)[None], _pack_tail(m)[None], _pack_tail(v)[None],
                   name="adamw_tail", tb=TB_TAIL)

    outs = []
    for kind in range(4):
        d = {**{n: res[n][kind] for n in big}, **_unpack_tail(tails[kind][0])}
        outs += [d[n] for n in _ORDER]
    return (loss, dx[None], *outs)
```

```python
import math

import jax
import jax.numpy as jnp
from jax import lax
from jax.experimental import pallas as pl
from jax.experimental.pallas import tpu as pltpu
from jax.experimental.pallas import tpu_sc as plsc

F32 = jnp.float32
BF16 = jnp.bfloat16

D = 1024
DEPTH = 2
N_DEV = 8
CHUNK = 64
LRU_BLOCKS = 8
LRU_BLOCK = 128
LRU_C = 8.0
D_SSD = 2 * D
SSD_HEADS = 32
SSD_GROUPS = 4
GROUP_W = D_SSD // SSD_GROUPS
SSD_STATE = 128
D_XBC = D_SSD + 2 * SSD_GROUPS * SSD_STATE
XA_HEADS = 4
XA_HEAD_DIM = 256
D_FF = 2816
ALPHA = (2 * DEPTH) ** 0.25
EPS = 1e-5
N_IN = 11296

S_Z, S_XBC, S_DT, W_SSD = 0, 2048, 5120, 5632
DT_REAL = 32

ADAM_LR, ADAM_B1, ADAM_B2, ADAM_EPS, ADAM_WD, ADAM_STEP = 0.001, 0.9, 0.999, 1e-08, 0.01, 10

VMEM_LIMIT = 56 * 1024 * 1024
MESH = pl.DeviceIdType.MESH
ANY = pl.BlockSpec(memory_space=pl.ANY)


def _cp(*sem):
    return pltpu.CompilerParams(dimension_semantics=sem, vmem_limit_bytes=VMEM_LIMIT)


def _blk(n, target):
    if n % 128:
        return n
    best = 128
    for b in range(128, min(n, target) + 1, 128):
        if n % b == 0:
            best = b
    return best


def _iota(shape, dim):
    return lax.broadcasted_iota(jnp.int32, shape, dim)


def _sigmoid(x):
    return 0.5 + 0.5 * jnp.tanh(0.5 * x)


def _log1p(e):
    u = 1.0 + e
    return jnp.where(u == 1.0, e, jnp.log(u) * (e / (u - 1.0)))


def _softplus(x):
    return jnp.maximum(x, 0.0) + _log1p(jnp.exp(-jnp.abs(x)))


_G0 = math.sqrt(2.0 / math.pi)
_G1 = 0.044715


def _gelu_and_grad(x):
    x2 = x * x
    u = 0.5 + 0.5 * jnp.tanh(x * (_G0 + (_G0 * _G1) * x2))
    dg = u + (x * (u * (1.0 - u))) * ((2.0 * _G0) + (6.0 * _G0 * _G1) * x2)
    return x * u, dg


_NN = (((1,), (0,)), ((), ()))
_NT = (((1,), (1,)), ((), ()))
_TN = (((0,), (0,)), ((), ()))


def _dot(a, b, dims=_NN):
    return lax.dot_general(a.astype(BF16), b.astype(BF16), dims, preferred_element_type=F32)


def _dot_hi(a, b, dims=_NN):
    return lax.dot_general(a, b, dims, precision=lax.Precision.HIGHEST, preferred_element_type=F32)


def _split(v):
    hi = v.astype(BF16)
    return hi, (v - hi.astype(F32)).astype(BF16)


def _dot01(v, e, dims=_NN):
    hi, lo = _split(v)
    return (lax.dot_general(hi, e, dims, preferred_element_type=F32)
            + lax.dot_general(lo, e, dims, preferred_element_type=F32))


def _conv_taps(xe, n):
    return [xe[8:8 + n] if j == 3 else pltpu.roll(xe, 3 - j, 0)[8:8 + n] for j in range(4)]


def _conv_fwd(taps, cw, cb):
    return cb + cw[0:1] * taps[0] + cw[1:2] * taps[1] + cw[2:3] * taps[2] + cw[3:4] * taps[3]


def _conv_bwd(dc, dnext, x, cw, n):
    ext = jnp.concatenate([dc, dnext], axis=0)
    shifted = [pltpu.roll(ext, n + 8 - (3 - j), 0)[0:n] for j in range(3)] + [dc]
    dx = cw[0:1] * shifted[0] + cw[1:2] * shifted[1] + cw[2:3] * shifted[2] + cw[3:4] * dc
    dcw = jnp.concatenate([jnp.sum(x * shifted[j], axis=0, keepdims=True) for j in range(4)], axis=0)
    return dx, dcw, jnp.sum(dc, axis=0, keepdims=True)


MM_VMEM_BUDGET = 44 * 1024 * 1024
MM_MAX_TILE = 1408
MM_MAX_K = 5632


def _divisors(n, cap):
    return [n] if n % 128 else [b for b in range(128, min(n, cap) + 1, 128) if n % b == 0]


def _mm_tiles(m_dim, n_dim, k_dim, a_bytes, b_bytes, o_bytes, has_add, tn_fixed):
    best = None
    for tm in _divisors(m_dim, MM_MAX_TILE):
        for tn in ([tn_fixed] if tn_fixed else _divisors(n_dim, MM_MAX_TILE)):
            for tk in _divisors(k_dim, MM_MAX_K):
                vmem = 2 * (tm * tk * a_bytes + tk * tn * b_bytes + tm * tn * (o_bytes + (4 if has_add else 0)))
                vmem += tm * tn * 4 if tk < k_dim else 0
                if vmem <= MM_VMEM_BUDGET:
                    key = (tm * tn * tk, tk, tn)
                    if best is None or key > best[0]:
                        best = (key, (tm, tn, tk))
    assert best is not None, (m_dim, n_dim, k_dim)
    return best[1]


def _mm(a, b, *, ta=False, tb=False, out_dtype=F32, add=None, add_scale=1.0, name, split_n=None):
    if ta:
        k_dim, m_dim = a.shape
    else:
        m_dim, k_dim = a.shape
    if tb:
        n_dim, k2 = b.shape
    else:
        k2, n_dim = b.shape
    assert k_dim == k2, (a.shape, b.shape, ta, tb)
    tm, tn, tk = _mm_tiles(m_dim, n_dim, k_dim, a.dtype.itemsize, b.dtype.itemsize, jnp.dtype(out_dtype).itemsize,
                           add is not None, split_n)
    nk = k_dim // tk
    a_spec = pl.BlockSpec((tk, tm), lambda i, j, k: (k, i)) if ta else pl.BlockSpec((tm, tk), lambda i, j, k: (i, k))
    b_spec = pl.BlockSpec((tn, tk), lambda i, j, k: (j, k)) if tb else pl.BlockSpec((tk, tn), lambda i, j, k: (k, j))
    o_spec = pl.BlockSpec((tm, tn), lambda i, j, k: (i, j))
    out_shape = (m_dim, n_dim)
    if split_n is not None:
        assert add is None and tn == split_n, (tn, split_n)
        o_spec = pl.BlockSpec((None, tm, tn), lambda i, j, k: (j, i, 0))
        out_shape = (n_dim // tn, m_dim, tn)
    dims = (((0 if ta else 1,), (1 if tb else 0,)), ((), ()))
    has_add = add is not None

    def body(*refs):
        a_ref, b_ref = refs[:2]
        add_ref = refs[2] if has_add else None
        o_ref = refs[3] if has_add else refs[2]
        acc_ref = refs[-1] if nk > 1 else None
        k = pl.program_id(2)

        def product():
            return lax.dot_general(a_ref[...].astype(BF16), b_ref[...].astype(BF16), dims, preferred_element_type=F32)

        def finish(r):
            if has_add:
                r = r + add_scale * add_ref[...]
            o_ref[...] = r.astype(out_dtype)

        if nk == 1:
            finish(product())
            return

        @pl.when(k == 0)
        def _():
            acc_ref[...] = product()

        @pl.when((k > 0) & (k < nk - 1))
        def _():
            acc_ref[...] += product()

        @pl.when(k == nk - 1)
        def _():
            finish(acc_ref[...] + product())

    in_specs = [a_spec, b_spec] + ([o_spec] if has_add else [])
    args = (a, b) + ((add,) if has_add else ())
    return pl.pallas_call(
        body, name=name, grid=(m_dim // tm, n_dim // tn, nk),
        in_specs=in_specs, out_specs=o_spec,
        out_shape=jax.ShapeDtypeStruct(out_shape, out_dtype),
        scratch_shapes=[pltpu.VMEM((tm, tn), F32)] if nk > 1 else [],
        cost_estimate=pl.CostEstimate(
            flops=2 * m_dim * n_dim * k_dim, transcendentals=0,
            bytes_accessed=a.size * a.dtype.itemsize + b.size * b.dtype.itemsize
            + m_dim * n_dim * (jnp.dtype(out_dtype).itemsize + (4 if has_add else 0))),
        compiler_params=_cp("parallel", "parallel", "arbitrary"),
    )(*args)


def _ln_fwd(x, f, g, b, name, tb=512):
    t = x.shape[0]
    tb = min(tb, t)

    def body(x_ref, f_ref, g_ref, b_ref, o_ref, ob_ref):
        u = ALPHA * x_ref[...] + f_ref[...]
        mu = jnp.mean(u, axis=-1, keepdims=True)
        d = u - mu
        var = jnp.mean(d * d, axis=-1, keepdims=True)
        y = d * lax.rsqrt(var + EPS) * g_ref[...] + b_ref[...]
        o_ref[...] = y
        ob_ref[...] = y.astype(BF16)

    row = pl.BlockSpec((tb, D), lambda i: (i, 0))
    par = pl.BlockSpec((1, D), lambda i: (0, 0))
    return pl.pallas_call(
        body, name=name, grid=(t // tb,), in_specs=[row, row, par, par], out_specs=[row, row],
        out_shape=[jax.ShapeDtypeStruct((t, D), F32), jax.ShapeDtypeStruct((t, D), BF16)],
        compiler_params=_cp("parallel"),
    )(x, f, g, b)


def _ln_bwd(x, f, dy, g, name, tb=512):
    t = x.shape[0]
    tb = min(tb, t)

    def body(x_ref, f_ref, dy_ref, g_ref, du_ref, dg_ref, db_ref):
        @pl.when(pl.program_id(0) == 0)
        def _():
            dg_ref[...] = jnp.zeros_like(dg_ref)
            db_ref[...] = jnp.zeros_like(db_ref)

        u = ALPHA * x_ref[...] + f_ref[...]
        mu = jnp.mean(u, axis=-1, keepdims=True)
        d = u - mu
        var = jnp.mean(d * d, axis=-1, keepdims=True)
        rstd = lax.rsqrt(var + EPS)
        xhat = d * rstd
        dy = dy_ref[...]
        dxh = dy * g_ref[...]
        m1 = jnp.mean(dxh, axis=-1, keepdims=True)
        m2 = jnp.mean(dxh * xhat, axis=-1, keepdims=True)
        du_ref[...] = rstd * (dxh - m1 - xhat * m2)
        dg_ref[...] += jnp.sum(dy * xhat, axis=0, keepdims=True)
        db_ref[...] += jnp.sum(dy, axis=0, keepdims=True)

    row = pl.BlockSpec((tb, D), lambda i: (i, 0))
    par = pl.BlockSpec((1, D), lambda i: (0, 0))
    return pl.pallas_call(
        body, name=name, grid=(t // tb,), in_specs=[row, row, row, par], out_specs=[row, par, par],
        out_shape=[jax.ShapeDtypeStruct((t, D), F32), jax.ShapeDtypeStruct((1, D), F32),
                   jax.ShapeDtypeStruct((1, D), F32)],
        compiler_params=_cp("arbitrary"),
    )(x, f, dy, g)


FFN_TM, FFN_TN = 512, D_FF // 2


def _ffn_in_swiglu(x, w, name):
    t = x.shape[0]
    tm = min(FFN_TM, t)
    nj = D_FF // FFN_TN

    def body(x_ref, wg_ref, wu_ref, g_ref, u_ref, a_ref):
        xb = x_ref[...].astype(BF16)
        g = lax.dot_general(xb, wg_ref[...], _NN, preferred_element_type=F32)
        u = lax.dot_general(xb, wu_ref[...], _NN, preferred_element_type=F32)
        g_ref[...] = g.astype(BF16)
        u_ref[...] = u.astype(BF16)
        a_ref[...] = (g * _sigmoid(g) * u).astype(BF16)

    tile = pl.BlockSpec((tm, FFN_TN), lambda i, j: (i, j))
    return pl.pallas_call(
        body, name=name, grid=(t // tm, nj),
        in_specs=[pl.BlockSpec((tm, D), lambda i, j: (i, 0)), pl.BlockSpec((D, FFN_TN), lambda i, j: (0, j)),
                  pl.BlockSpec((D, FFN_TN), lambda i, j: (0, nj + j))],
        out_specs=[tile, tile, tile],
        out_shape=[jax.ShapeDtypeStruct((t, D_FF), BF16)] * 3,
        compiler_params=_cp("parallel", "parallel"),
    )(x, w, w)


def _d_swiglu(du, w_down, g, u, name):
    t = du.shape[0]
    tm = min(FFN_TM, t)

    def body(du_ref, w_ref, g_ref, u_ref, dg_ref, dup_ref):
        da = lax.dot_general(du_ref[...].astype(BF16), w_ref[...], _NT, preferred_element_type=F32)
        g_v = g_ref[...].astype(F32)
        s = _sigmoid(g_v)
        dg_ref[...] = (da * u_ref[...].astype(F32) * (s * (1.0 + g_v * (1.0 - s)))).astype(BF16)
        dup_ref[...] = (da * g_v * s).astype(BF16)

    tile = pl.BlockSpec((tm, FFN_TN), lambda i, j: (i, j))
    return pl.pallas_call(
        body, name=name, grid=(t // tm, D_FF // FFN_TN),
        in_specs=[pl.BlockSpec((tm, D), lambda i, j: (i, 0)), pl.BlockSpec((FFN_TN, D), lambda i, j: (j, 0)), tile, tile],
        out_specs=[tile, tile],
        out_shape=[jax.ShapeDtypeStruct((t, D_FF), BF16), jax.ShapeDtypeStruct((t, D_FF), BF16)],
        compiler_params=_cp("parallel", "parallel"),
    )(du, w_down, g, u)


def _merge_fwd(pgl, bg, b1, b2, b3, name, tb=512):
    t = pgl.shape[0]
    tb = min(tb, t)

    def body(gl_ref, bg_ref, b1_ref, b2_ref, b3_ref, o_ref):
        acc = None
        for j, b_ref in enumerate((b1_ref, b2_ref, b3_ref)):
            sl = slice(j * D, (j + 1) * D)
            term = _sigmoid(gl_ref[:, sl].astype(F32) + bg_ref[:, sl]) * b_ref[...].astype(F32)
            acc = term if acc is None else acc + term
        o_ref[...] = acc.astype(BF16)

    row = pl.BlockSpec((tb, D), lambda i: (i, 0))
    return pl.pallas_call(
        body, name=name, grid=(t // tb,),
        in_specs=[pl.BlockSpec((tb, 3 * D), lambda i: (i, 0)), pl.BlockSpec((1, 3 * D), lambda i: (0, 0)), row, row, row],
        out_specs=row, out_shape=jax.ShapeDtypeStruct((t, D), BF16), compiler_params=_cp("parallel"),
    )(pgl, bg, b1, b2, b3)


def _merge_bwd(pgl, bg, b1, b2, b3, dm, name, tb=512):
    t = pgl.shape[0]
    tb = min(tb, t)

    def body(gl_ref, bg_ref, b1_ref, b2_ref, b3_ref, dm_ref, dgl_ref, d1_ref, d2_ref, d3_ref, dbg_ref):
        @pl.when(pl.program_id(0) == 0)
        def _():
            dbg_ref[...] = jnp.zeros_like(dbg_ref)

        dm_v = dm_ref[...]
        for j, (b_ref, d_ref) in enumerate(((b1_ref, d1_ref), (b2_ref, d2_ref), (b3_ref, d3_ref))):
            sl = slice(j * D, (j + 1) * D)
            gate = _sigmoid(gl_ref[:, sl].astype(F32) + bg_ref[:, sl])
            d_ref[...] = (dm_v * gate).astype(BF16)
            dgl = dm_v * b_ref[...].astype(F32) * (gate * (1.0 - gate))
            dgl_ref[:, sl] = dgl.astype(BF16)
            dbg_ref[:, sl] += jnp.sum(dgl, axis=0, keepdims=True)

    row = pl.BlockSpec((tb, D), lambda i: (i, 0))
    wide = pl.BlockSpec((tb, 3 * D), lambda i: (i, 0))
    par = pl.BlockSpec((1, 3 * D), lambda i: (0, 0))
    return pl.pallas_call(
        body, name=name, grid=(t // tb,),
        in_specs=[wide, par, row, row, row, row], out_specs=[wide, row, row, row, par],
        out_shape=[jax.ShapeDtypeStruct((t, 3 * D), BF16)] + [jax.ShapeDtypeStruct((t, D), BF16)] * 3
                  + [jax.ShapeDtypeStruct((1, 3 * D), F32)],
        compiler_params=_cp("arbitrary"),
    )(pgl, bg, b1, b2, b3, dm)


def _xa_probs(q, kv_ref, hd):
    sl = slice(hd * XA_HEAD_DIM, (hd + 1) * XA_HEAD_DIM)
    k = kv_ref[:, sl]
    v = kv_ref[:, D + hd * XA_HEAD_DIM:D + (hd + 1) * XA_HEAD_DIM]
    s = _dot(q[:, sl], k, _NT) * (XA_HEAD_DIM ** -0.5)
    e = jnp.exp(s - jnp.max(s, axis=1, keepdims=True))
    return sl, k, v, e / jnp.sum(e, axis=1, keepdims=True)


def _xa_fwd(pq, kv, name, tb=512):
    t = pq.shape[0]
    tb = min(tb, t)

    def body(q_ref, kv_ref, o_ref):
        q = q_ref[...]
        for hd in range(XA_HEADS):
            sl, _, v, p = _xa_probs(q, kv_ref, hd)
            o_ref[:, sl] = _dot(p, v).astype(BF16)

    row = pl.BlockSpec((tb, D), lambda i: (i, 0))
    return pl.pallas_call(
        body, name=name, grid=(t // tb,),
        in_specs=[row, pl.BlockSpec(kv.shape, lambda i: (0, 0))], out_specs=row,
        out_shape=jax.ShapeDtypeStruct((t, D), BF16), compiler_params=_cp("parallel"),
    )(pq, kv)


def _xa_bwd(pq, kv, dy, name, tb=512):
    t = pq.shape[0]
    tb = min(tb, t)

    def body(q_ref, kv_ref, dy_ref, dq_ref, dkv_ref):
        @pl.when(pl.program_id(0) == 0)
        def _():
            dkv_ref[...] = jnp.zeros_like(dkv_ref)

        q = q_ref[...]
        for hd in range(XA_HEADS):
            sl, k, v, p = _xa_probs(q, kv_ref, hd)
            dyh = dy_ref[:, sl]
            vsl = slice(D + hd * XA_HEAD_DIM, D + (hd + 1) * XA_HEAD_DIM)
            dkv_ref[:, vsl] += _dot(p, dyh, _TN)
            dp = _dot(dyh, v, _NT)
            ds = p * (dp - jnp.sum(dp * p, axis=1, keepdims=True)) * (XA_HEAD_DIM ** -0.5)
            dq_ref[:, sl] = _dot(ds, k).astype(BF16)
            dkv_ref[:, sl] += _dot(ds, q[:, sl], _TN)

    row = pl.BlockSpec((tb, D), lambda i: (i, 0))
    kvs = pl.BlockSpec(kv.shape, lambda i: (0, 0))
    return pl.pallas_call(
        body, name=name, grid=(t // tb,), in_specs=[row, kvs, row], out_specs=[row, kvs],
        out_shape=[jax.ShapeDtypeStruct((t, D), BF16), jax.ShapeDtypeStruct(kv.shape, F32)],
        compiler_params=_cp("arbitrary"),
    )(pq, kv, dy)


SUBLANES = 8


def _scan(a, u, reverse):
    n, c = a.shape
    groups = n // SUBLANES
    a = a.reshape(groups, SUBLANES, c)
    u = u.reshape(groups, SUBLANES, c)
    sub = _iota((1, SUBLANES, 1), 1)
    d = 1
    while d < SUBLANES:
        keep = (sub < SUBLANES - d) if reverse else (sub >= d)
        shift = SUBLANES - d if reverse else d
        u = a * jnp.where(keep, pltpu.roll(u, shift, 1), 0.0) + u
        a = a * jnp.where(keep, pltpu.roll(a, shift, 1), 1.0)
        d *= 2
    edge = 0 if reverse else SUBLANES - 1
    out, carry = [None] * groups, None
    for j in (reversed(range(groups)) if reverse else range(groups)):
        out[j] = u[j] if carry is None else u[j] + a[j] * carry
        carry = out[j][edge:edge + 1]
    return jnp.concatenate(out, axis=0)


def _lru_gates(xc, wa_ref, ba, wi_ref, bi, lam):
    za = jnp.concatenate([_dot(xc[:, n * 128:(n + 1) * 128], wa_ref[n]) for n in range(LRU_BLOCKS)], axis=1) + ba
    zi = jnp.concatenate([_dot(xc[:, n * 128:(n + 1) * 128], wi_ref[n]) for n in range(LRU_BLOCKS)], axis=1) + bi
    r = 1.0 / (1.0 + jnp.exp(-za))
    ig = _sigmoid(zi)
    sp = _softplus(-lam)
    log_a = (-LRU_C) * r * sp
    a = jnp.exp(log_a)
    m = jnp.sqrt(-jnp.tanh(log_a) * (1.0 + a * a))
    u = m * (ig * xc)
    return a, u, r, ig, m, sp


def _lru_fwd(p, cw, cb, wa, ba, wi, bi, lam, name, tb=256):
    t = p.shape[0]
    tb = min(tb, t)
    nb = t // tb
    r8 = tb // 8

    def body(x_ref, xp_ref, g_ref, cw_ref, cb_ref, wa_ref, ba_ref, wi_ref, bi_ref, lam_ref, y_ref, h_ref, xc_ref,
             hc_ref):
        i = pl.program_id(0)

        @pl.when(i == 0)
        def _():
            hc_ref[...] = jnp.zeros_like(hc_ref)

        halo = jnp.where(i == 0, 0.0, xp_ref[...])
        taps = _conv_taps(jnp.concatenate([halo, x_ref[...]], axis=0), tb)
        xc = _conv_fwd(taps, cw_ref[...], cb_ref[...])
        xc_ref[...] = xc
        a, u, _, _, _, _ = _lru_gates(xc, wa_ref, ba_ref[...], wi_ref, bi_ref[...], lam_ref[...])
        row = _iota((tb, 1), 0)
        u = u + jnp.where(row == 0, a * hc_ref[...], 0.0)
        h = _scan(a, u, reverse=False)
        h_ref[...] = h
        hc_ref[...] = h[tb - 1:tb, :]
        gl, _ = _gelu_and_grad(g_ref[...])
        y_ref[...] = (gl * h).astype(BF16)

    par = pl.BlockSpec((1, D), lambda i: (0, 0))
    wsp = pl.BlockSpec((LRU_BLOCKS, LRU_BLOCK, LRU_BLOCK), lambda i: (0, 0, 0))
    row = pl.BlockSpec((tb, D), lambda i: (i, 0))
    return pl.pallas_call(
        body, name=name, grid=(nb,),
        in_specs=[row, pl.BlockSpec((8, D), lambda i: (jnp.maximum(i * r8 - 1, 0), 0)),
                  pl.BlockSpec((tb, D), lambda i: (i, 1)),
                  pl.BlockSpec((4, D), lambda i: (0, 0)), par, wsp, par, wsp, par, par],
        out_specs=[row, row, row],
        out_shape=[jax.ShapeDtypeStruct((t, D), BF16), jax.ShapeDtypeStruct((t, D), F32),
                   jax.ShapeDtypeStruct((t, D), F32)],
        scratch_shapes=[pltpu.VMEM((1, D), F32)],
        compiler_params=_cp("arbitrary"),
    )(p, p, p, cw, cb, wa, ba, wi, bi, lam)


def _lru_bwd(p, xc, h, dy, cw, wa, ba, wi, bi, lam, name, tb=256):
    t = p.shape[0]
    tb = min(tb, t)
    nb = t // tb
    r8 = tb // 8

    def body(x_ref, g_ref, xc_ref, h_ref, hp_ref, dy_ref, cw_ref, wa_ref, ba_ref, wi_ref, bi_ref, lam_ref,
             dp_ref, dcw_ref, dcb_ref, dwa_ref, dba_ref, dwi_ref, dbi_ref, dlam_ref, carry_ref, dnext_ref):
        i = pl.program_id(0)
        blk = nb - 1 - i

        @pl.when(i == 0)
        def _():
            for r in (dcw_ref, dcb_ref, dwa_ref, dba_ref, dwi_ref, dbi_ref, dlam_ref, carry_ref, dnext_ref):
                r[...] = jnp.zeros_like(r)

        xc = xc_ref[...]
        lam = lam_ref[...]
        a, _, r, ig, m, sp = _lru_gates(xc, wa_ref, ba_ref[...], wi_ref, bi_ref[...], lam)
        gl, dgl = _gelu_and_grad(g_ref[...])
        h = h_ref[...]
        dy = dy_ref[...]
        dp_ref[:, D:] = (dy * h * dgl).astype(BF16)
        row = _iota((tb, 1), 0)
        dh = dy * gl + jnp.where(row == tb - 1, carry_ref[...], 0.0)
        b = jnp.where(row < tb - 1, pltpu.roll(a, tb - 1, 0), 0.0)
        gs = _scan(b, dh, reverse=True)
        carry_ref[...] = a[0:1] * gs[0:1]
        h_last = jnp.where(blk == 0, 0.0, hp_ref[7:8, :])
        hprev = jnp.where(row == 0, h_last, pltpu.roll(h, 1, 0))
        da = gs * hprev
        dm = gs * ig * xc
        di = gs * m * xc
        dxc = gs * m * ig
        dlog = a * (da - a * (dm / m))
        dr = dlog * ((-LRU_C) * sp)
        dsp = jnp.sum(dlog * ((-LRU_C) * r), axis=0, keepdims=True)
        dlam_ref[...] += dsp * (-_sigmoid(-lam))
        dza = dr * r * (1.0 - r)
        dzi = di * ig * (1.0 - ig)
        dba_ref[...] += jnp.sum(dza, axis=0, keepdims=True)
        dbi_ref[...] += jnp.sum(dzi, axis=0, keepdims=True)
        parts = []
        for n in range(LRU_BLOCKS):
            sl = slice(n * 128, (n + 1) * 128)
            dwa_ref[n] += _dot(xc[:, sl], dza[:, sl], _TN)
            dwi_ref[n] += _dot(xc[:, sl], dzi[:, sl], _TN)
            parts.append(_dot(dza[:, sl], wa_ref[n], _NT) + _dot(dzi[:, sl], wi_ref[n], _NT))
        dxc = dxc + jnp.concatenate(parts, axis=1)
        dx, dcw, dcb = _conv_bwd(dxc, dnext_ref[...], x_ref[...], cw_ref[...], tb)
        dp_ref[:, :D] = dx.astype(BF16)
        dcw_ref[...] += dcw
        dcb_ref[...] += dcb
        dnext_ref[...] = dxc[0:8]

    par = pl.BlockSpec((1, D), lambda i: (0, 0))
    wsp = pl.BlockSpec((LRU_BLOCKS, LRU_BLOCK, LRU_BLOCK), lambda i: (0, 0, 0))
    cws = pl.BlockSpec((4, D), lambda i: (0, 0))
    rev = lambda i: nb - 1 - i
    blk0 = pl.BlockSpec((tb, D), lambda i: (rev(i), 0))
    w_shape = jax.ShapeDtypeStruct((LRU_BLOCKS, LRU_BLOCK, LRU_BLOCK), F32)
    v_shape = jax.ShapeDtypeStruct((1, D), F32)
    return pl.pallas_call(
        body, name=name, grid=(nb,),
        in_specs=[blk0, pl.BlockSpec((tb, D), lambda i: (rev(i), 1)), blk0, blk0,
                  pl.BlockSpec((8, D), lambda i: (jnp.maximum(rev(i) * r8 - 1, 0), 0)), blk0,
                  cws, wsp, par, wsp, par, par],
        out_specs=[pl.BlockSpec((tb, 2 * D), lambda i: (rev(i), 0)), cws, par, wsp, par, wsp, par, par],
        out_shape=[jax.ShapeDtypeStruct((t, 2 * D), BF16), jax.ShapeDtypeStruct((4, D), F32), v_shape,
                   w_shape, v_shape, w_shape, v_shape, v_shape],
        scratch_shapes=[pltpu.VMEM((1, D), F32), pltpu.VMEM((8, D), F32)],
        compiler_params=_cp("arbitrary"),
    )(p, p, xc, h, h, dy, cw, wa, ba, wi, bi, lam)


def _ssd_consts():
    m0 = _iota((1, 128), 1) < 64
    e = (jnp.right_shift(_iota((SSD_HEADS, D_SSD), 1), 6) == _iota((SSD_HEADS, D_SSD), 0)).astype(BF16)
    tril = (_iota((CHUNK, CHUNK), 0) >= _iota((CHUNK, CHUNK), 1)).astype(F32)
    eye = (_iota((SSD_HEADS, SSD_HEADS), 0) == _iota((SSD_HEADS, SSD_HEADS), 1)).astype(F32)
    r2 = _iota((CHUNK, 128), 0)
    c2 = jnp.bitwise_and(_iota((CHUNK, 128), 1), 63)
    return dict(m0=m0, e=e, tril=tril, eye=eye, causal2=r2 >= c2, fold=(c2 == r2).astype(BF16))


SSD_STEP = 4


def _ssd_pre(c, dt_raw, dtb_ref, alog_ref, dvec_ref, k):
    sg = _sigmoid(c)
    xbc = c * sg
    dtp = dt_raw + dtb_ref[...]
    dt = _softplus(dtp)
    a = -jnp.exp(alog_ref[...])
    cs = _dot_hi(k["tril"], dt * a)
    cs_last = cs[CHUNK - 1:CHUNK]
    dend = jnp.exp(cs_last - cs)
    cdec = jnp.exp(cs_last)
    big = _dot01(jnp.concatenate([dt, jnp.exp(cs), dend], axis=0), k["e"])
    small = _dot01(jnp.concatenate([jnp.broadcast_to(cdec, (8, SSD_HEADS)),
                                    jnp.broadcast_to(dvec_ref[...], (8, SSD_HEADS))], axis=0), k["e"])
    cst2 = _dot_hi(k["eye"], jnp.concatenate([cs, cs], axis=0), _NT)
    return dict(c=c, sg=sg, xs=xbc[:, :D_SSD], bm=xbc[:, D_SSD:D_SSD + 512],
                cm=xbc[:, D_SSD + 512:], dtp=dtp, dt=dt, a=a, cs=cs, dend=dend, cdec=cdec,
                dtx=big[0:CHUNK], ecx=big[CHUNK:2 * CHUNK], dex=big[2 * CHUNK:3 * CHUNK],
                cdx=small[0:1], ddx=small[8:9], cst2=cst2)


def _pair_decay(p, cs, cst2, k):
    h0, h1 = 2 * p, 2 * p + 1
    colp = jnp.where(k["m0"], cs[:, h0:h0 + 1], cs[:, h1:h1 + 1])
    rowp = jnp.where(k["m0"], cst2[h0:h0 + 1, :], cst2[h1:h1 + 1, :])
    return jnp.where(k["causal2"], jnp.exp(colp - rowp), 0.0)


def _pair_stack(xp, k):
    return jnp.concatenate([jnp.where(k["m0"], xp, 0.0), jnp.where(k["m0"], 0.0, xp)], axis=0)


def _group_norm(yz, nw, with_stats=False):
    outs, stats = [], []
    for g in range(SSD_GROUPS):
        yzg = yz[:, g * GROUP_W:(g + 1) * GROUP_W]
        r = lax.rsqrt(jnp.mean(yzg * yzg, axis=1, keepdims=True) + EPS)
        outs.append(yzg * r)
        stats.append(r)
    y = jnp.concatenate(outs, axis=1) * nw
    return (y, stats) if with_stats else y


def _ssd_fwd(p, cw, cb, dtb, alog, dvec, nw, name):
    t = p.shape[0]
    step = SSD_STEP if t % (SSD_STEP * CHUNK) == 0 else 1
    rows_blk, nb, nc = step * CHUNK, t // (step * CHUNK), t // CHUNK

    def body(p_blk, cw_ref, cb_ref, dtb_ref, alog_ref, dvec_ref, nw_ref, y_blk, yraw_blk, hs_blk, c_blk,
             h_scr, tail_scr):
        @pl.when(pl.program_id(0) == 0)
        def _():
            h_scr[...] = jnp.zeros_like(h_scr)
            tail_scr[...] = jnp.zeros_like(tail_scr)

        k = _ssd_consts()

        def one_chunk(j, carry):
            rows = pl.ds(pl.multiple_of(j * CHUNK, CHUNK), CHUNK)
            chunk(p_blk.at[rows], y_blk.at[rows], yraw_blk.at[rows], hs_blk.at[j], c_blk.at[rows], k,
                  cw_ref, cb_ref, dtb_ref, alog_ref, dvec_ref, nw_ref, h_scr, tail_scr)
            return carry

        lax.fori_loop(0, step, one_chunk, 0)

    def chunk(p_ref, y_ref, yraw_ref, hs_ref, c_ref, k, cw_ref, cb_ref, dtb_ref, alog_ref, dvec_ref, nw_ref,
              h_scr, tail_scr):
        x_in = p_ref[:, S_XBC:S_DT]
        taps = _conv_taps(jnp.concatenate([tail_scr[...], x_in], axis=0), CHUNK)
        tail_scr[...] = x_in[CHUNK - 8:]
        c = _conv_fwd(taps, cw_ref[...], cb_ref[...])
        c_ref[...] = c
        s = _ssd_pre(c, p_ref[:, S_DT:S_DT + DT_REAL], dtb_ref, alog_ref, dvec_ref, k)
        xs, bm, cm = s["xs"], s["bm"], s["cm"]
        xdt = xs * s["dtx"]
        hprev = h_scr[...]
        hs_ref[...] = hprev
        ys, hn = [], []
        for g in range(SSD_GROUPS):
            gs = slice(g * GROUP_W, (g + 1) * GROUP_W)
            bg = bm[:, g * 128:(g + 1) * 128]
            cg = cm[:, g * 128:(g + 1) * 128]
            cbdup = _dot(cg, jnp.concatenate([bg, bg], axis=0), _NT)
            hp_g = hprev[:, gs]
            yd = []
            for q in range(4):
                pr = g * 4 + q
                mp = cbdup * _pair_decay(pr, s["cs"], s["cst2"], k)
                yd.append(_dot(mp, _pair_stack(xdt[:, pr * 128:(pr + 1) * 128], k)))
            ys.append(jnp.concatenate(yd, axis=1) + _dot(cg, hp_g) * s["ecx"][:, gs])
            hn.append(hp_g * s["cdx"][:, gs] + _dot(bg, xdt[:, gs] * s["dex"][:, gs], _TN))
        h_scr[...] = jnp.concatenate(hn, axis=1)
        yraw = jnp.concatenate(ys, axis=1) + s["ddx"] * xs
        yraw_ref[...] = yraw
        z = p_ref[:, S_Z:S_Z + D_SSD]
        y_ref[...] = _group_norm(yraw * (z * _sigmoid(z)), nw_ref[...]).astype(BF16)

    hv = pl.BlockSpec((1, DT_REAL), lambda i: (0, 0))
    return pl.pallas_call(
        body, name=name, grid=(nb,),
        in_specs=[pl.BlockSpec((rows_blk, W_SSD), lambda i: (i, 0)),
                  pl.BlockSpec((4, D_XBC), lambda i: (0, 0)), pl.BlockSpec((1, D_XBC), lambda i: (0, 0)),
                  hv, hv, hv, pl.BlockSpec((1, D_SSD), lambda i: (0, 0))],
        out_specs=[pl.BlockSpec((rows_blk, D_SSD), lambda i: (i, 0)), pl.BlockSpec((rows_blk, D_SSD), lambda i: (i, 0)),
                   pl.BlockSpec((step, SSD_STATE, D_SSD), lambda i: (i, 0, 0)),
                   pl.BlockSpec((rows_blk, D_XBC), lambda i: (i, 0))],
        out_shape=[jax.ShapeDtypeStruct((t, D_SSD), BF16), jax.ShapeDtypeStruct((t, D_SSD), F32),
                   jax.ShapeDtypeStruct((nc, SSD_STATE, D_SSD), F32), jax.ShapeDtypeStruct((t, D_XBC), F32)],
        scratch_shapes=[pltpu.VMEM((SSD_STATE, D_SSD), F32), pltpu.VMEM((8, D_XBC), F32)],
        compiler_params=_cp("arbitrary"),
    )(p, cw, cb, dtb, alog, dvec, nw)


def _ssd_bwd(p, c, yraw, hs, dy, cw, dtb, alog, dvec, nw, name):
    t = p.shape[0]
    step = SSD_STEP if t % (SSD_STEP * CHUNK) == 0 else 1
    rows_blk, nb = step * CHUNK, t // (step * CHUNK)

    def body(p_blk, c_blk, yraw_blk, hs_blk, dy_blk, cw_ref, dtb_ref, alog_ref, dvec_ref, nw_ref,
             dp_blk, dcw_ref, dcb_ref, ddtb_ref, dalog_ref, dd_ref, dnw_ref, dh_scr, dnext_scr):
        @pl.when(pl.program_id(0) == 0)
        def _():
            for r in (dcw_ref, dcb_ref, ddtb_ref, dalog_ref, dd_ref, dnw_ref, dh_scr, dnext_scr):
                r[...] = jnp.zeros_like(r)

        k = _ssd_consts()

        def one_chunk(jj, carry):
            j = step - 1 - jj
            rows = pl.ds(pl.multiple_of(j * CHUNK, CHUNK), CHUNK)
            chunk(p_blk.at[rows], c_blk.at[rows], yraw_blk.at[rows], hs_blk.at[j], dy_blk.at[rows], dp_blk.at[rows], k,
                  cw_ref, dtb_ref, alog_ref, dvec_ref, nw_ref, dcw_ref, dcb_ref, ddtb_ref, dalog_ref, dd_ref, dnw_ref,
                  dh_scr, dnext_scr)
            return carry

        lax.fori_loop(0, step, one_chunk, 0)

    def chunk(p_ref, c_ref, yraw_ref, hs_ref, dy_ref, dp_ref, k, cw_ref, dtb_ref, alog_ref, dvec_ref, nw_ref,
              dcw_ref, dcb_ref, ddtb_ref, dalog_ref, dd_ref, dnw_ref, dh_scr, dnext_scr):
        s = _ssd_pre(c_ref[...], p_ref[:, S_DT:S_DT + DT_REAL], dtb_ref, alog_ref, dvec_ref, k)
        xs, bm, cm, cs, dt, a = s["xs"], s["bm"], s["cm"], s["cs"], s["dt"], s["a"]
        m0 = k["m0"]
        xdt = xs * s["dtx"]
        hprev = hs_ref[...]
        dh = dh_scr[...]

        nw_v = nw_ref[...]
        yraw = yraw_ref[...]
        z = p_ref[:, S_Z:S_Z + D_SSD]
        sz = _sigmoid(z)
        siluz = z * sz
        yz = yraw * siluz
        dyo = dy_ref[...]
        dyn = dyo * nw_v
        dyz_parts, dnw_parts = [], []
        for g in range(SSD_GROUPS):
            gs = slice(g * GROUP_W, (g + 1) * GROUP_W)
            yzg = yz[:, gs]
            r = lax.rsqrt(jnp.mean(yzg * yzg, axis=1, keepdims=True) + EPS)
            dnw_parts.append(jnp.sum(dyo[:, gs] * yzg * r, axis=0, keepdims=True))
            dyz_parts.append(r * dyn[:, gs] - yzg * (r * r * r) * jnp.mean(dyn[:, gs] * yzg, axis=1, keepdims=True))
        dnw_ref[...] += jnp.concatenate(dnw_parts, axis=1)
        dyz = jnp.concatenate(dyz_parts, axis=1)
        d_y = dyz * siluz
        dp_ref[:, S_Z:S_Z + D_SSD] = (dyz * yraw * (sz * (1.0 + z * (1.0 - sz)))).astype(BF16)
        dd_row = jnp.sum(d_y * xs, axis=0, keepdims=True)
        dxs = d_y * s["ddx"]

        lane_h = _iota((1, SSD_HEADS), 1)
        sub_h = _iota((SSD_HEADS, 1), 0)
        dcs = jnp.zeros((CHUNK, SSD_HEADS), F32)
        dcst2 = jnp.zeros((SSD_HEADS, 128), F32)
        dxdt_parts, db_parts, dc_parts, dhp_parts, yoff_parts, dend_parts, dcd_parts = [], [], [], [], [], [], []
        for g in range(SSD_GROUPS):
            gs = slice(g * GROUP_W, (g + 1) * GROUP_W)
            bg = bm[:, g * 128:(g + 1) * 128]
            cg = cm[:, g * 128:(g + 1) * 128]
            bdup = jnp.concatenate([bg, bg], axis=0)
            cbdup = _dot(cg, bdup, _NT)
            dcb2 = jnp.zeros((CHUNK, 128), F32)
            dxp_parts = []
            for q in range(4):
                pr = g * 4 + q
                h0, h1 = 2 * pr, 2 * pr + 1
                lp = _pair_decay(pr, cs, s["cst2"], k)
                mp = cbdup * lp
                xst = _pair_stack(xdt[:, pr * 128:(pr + 1) * 128], k)
                dyp = d_y[:, pr * 128:(pr + 1) * 128]
                dmp = _dot(dyp, xst, _NT)
                dxst = _dot(mp, dyp, _TN)
                dxp_parts.append(jnp.where(m0, dxst[:CHUNK], dxst[CHUNK:]))
                dcb2 = dcb2 + dmp * lp
                dlm = dmp * mp
                rs0 = jnp.sum(jnp.where(m0, dlm, 0.0), axis=1, keepdims=True)
                rs1 = jnp.sum(jnp.where(m0, 0.0, dlm), axis=1, keepdims=True)
                dcs = dcs + jnp.where(lane_h == h0, rs0, 0.0) + jnp.where(lane_h == h1, rs1, 0.0)
                colsum = jnp.sum(dlm, axis=0, keepdims=True)
                sel = ((sub_h == h0) & m0) | ((sub_h == h1) & jnp.logical_not(m0))
                dcst2 = dcst2 - jnp.where(sel, colsum, 0.0)
            dcg = _dot(dcb2, bdup)
            dbdup = _dot(dcb2, cg, _TN)
            dbg = dbdup[:CHUNK] + dbdup[CHUNK:]
            hp_g = hprev[:, gs]
            zoff = _dot(cg, hp_g)
            dzo = d_y[:, gs] * s["ecx"][:, gs]
            dcg = dcg + _dot(dzo, hp_g, _NT)
            dh_g = dh[:, gs]
            dhp_parts.append(_dot(cg, dzo, _TN) + dh_g * s["cdx"][:, gs])
            dcd_parts.append(jnp.sum(dh_g * hp_g, axis=0, keepdims=True))
            wg = xdt[:, gs] * s["dex"][:, gs]
            dbg = dbg + _dot(wg, dh_g, _NT)
            dwg = _dot(bg, dh_g)
            dxdt_parts.append(jnp.concatenate(dxp_parts, axis=1) + dwg * s["dex"][:, gs])
            dend_g = dwg * wg
            dend_parts.append(jnp.sum(dend_g, axis=0, keepdims=True))
            yoff_parts.append(dzo * zoff - dend_g)
            db_parts.append(dbg)
            dc_parts.append(dcg)
        dh_scr[...] = jnp.concatenate(dhp_parts, axis=1)
        dxdt = jnp.concatenate(dxdt_parts, axis=1)
        sums = _dot01(jnp.concatenate([jnp.concatenate(yoff_parts, axis=1), dxdt * xs], axis=0), k["e"], _NT)
        rows8 = jnp.concatenate([jnp.broadcast_to(jnp.concatenate(r, axis=1), (8, D_SSD))
                                 for r in (dcd_parts, [dd_row], dend_parts)], axis=0)
        small = _dot01(rows8, k["e"], _NT)
        dd_ref[...] += small[8:9]
        dcs_last = small[0:1] * s["cdec"] + small[16:17]
        hi, lo = _split(dcst2)
        dcs = (dcs + sums[0:CHUNK]
               + lax.dot_general(k["fold"], hi, _NT, preferred_element_type=F32)
               + lax.dot_general(k["fold"], lo, _NT, preferred_element_type=F32)
               + jnp.where(_iota((CHUNK, 1), 0) == CHUNK - 1, dcs_last, 0.0))
        dda = _dot_hi(k["tril"], dcs, _TN)
        ddt = dda * a + sums[CHUNK:2 * CHUNK]
        dalog_ref[...] += jnp.sum(dda * dt, axis=0, keepdims=True) * a
        dxs = dxs + dxdt * s["dtx"]
        draw = ddt * _sigmoid(s["dtp"])
        ddtb_ref[...] += jnp.sum(draw, axis=0, keepdims=True)
        dp_ref[:, S_DT:] = jnp.zeros((CHUNK, W_SSD - S_DT), BF16)
        dp_ref[:, S_DT:S_DT + DT_REAL] = draw.astype(BF16)
        dxbc = jnp.concatenate([dxs] + db_parts + dc_parts, axis=1)
        sg, c = s["sg"], s["c"]
        dc = dxbc * (sg * (1.0 + c * (1.0 - sg)))
        dx, dcw, dcb = _conv_bwd(dc, dnext_scr[...], p_ref[:, S_XBC:S_DT], cw_ref[...], CHUNK)
        dp_ref[:, S_XBC:S_DT] = dx.astype(BF16)
        dcw_ref[...] += dcw
        dcb_ref[...] += dcb
        dnext_scr[...] = dc[0:8]

    rev = lambda i: nb - 1 - i
    hv = pl.BlockSpec((1, DT_REAL), lambda i: (0, 0))
    cws = pl.BlockSpec((4, D_XBC), lambda i: (0, 0))
    cbs = pl.BlockSpec((1, D_XBC), lambda i: (0, 0))
    nws = pl.BlockSpec((1, D_SSD), lambda i: (0, 0))
    wide = pl.BlockSpec((rows_blk, D_SSD), lambda i: (rev(i), 0))
    hshape = jax.ShapeDtypeStruct((1, DT_REAL), F32)
    return pl.pallas_call(
        body, name=name, grid=(nb,),
        in_specs=[pl.BlockSpec((rows_blk, W_SSD), lambda i: (rev(i), 0)),
                  pl.BlockSpec((rows_blk, D_XBC), lambda i: (rev(i), 0)),
                  wide, pl.BlockSpec((step, SSD_STATE, D_SSD), lambda i: (rev(i), 0, 0)), wide,
                  cws, hv, hv, hv, nws],
        out_specs=[pl.BlockSpec((rows_blk, W_SSD), lambda i: (rev(i), 0)), cws, cbs, hv, hv, hv, nws],
        out_shape=[jax.ShapeDtypeStruct((t, W_SSD), BF16), jax.ShapeDtypeStruct((4, D_XBC), F32),
                   jax.ShapeDtypeStruct((1, D_XBC), F32), hshape, hshape, hshape,
                   jax.ShapeDtypeStruct((1, D_SSD), F32)],
        scratch_shapes=[pltpu.VMEM((SSD_STATE, D_SSD), F32), pltpu.VMEM((8, D_XBC), F32)],
        compiler_params=_cp("arbitrary"),
    )(p, c, yraw, hs, dy, cw, dtb, alog, dvec, nw)


def _loss_head(y, target, name, tb=512):
    t = y.shape[0]
    tb = min(tb, t)

    def body(y_ref, t_ref, dy_ref, l_ref):
        @pl.when(pl.program_id(0) == 0)
        def _():
            l_ref[...] = jnp.zeros_like(l_ref)

        e = y_ref[...] - t_ref[...]
        dy_ref[...] = e * (1.0 / D)
        l_ref[...] += jnp.sum(jnp.sum(e * e, axis=1, keepdims=True), axis=0, keepdims=True) * (0.5 / D)

    row = pl.BlockSpec((tb, D), lambda i: (i, 0))
    return pl.pallas_call(
        body, name=name, grid=(t // tb,), in_specs=[row, row],
        out_specs=[row, pl.BlockSpec((8, 128), lambda i: (0, 0))],
        out_shape=[jax.ShapeDtypeStruct((t, D), F32), jax.ShapeDtypeStruct((8, 128), F32)],
        compiler_params=_cp("arbitrary"),
    )(y, target)


def _adamw(slots, w, m, v, name, tb):
    nl = len(slots)
    ns, r, c = slots[0].shape
    assert r % tb == 0 and w.shape == (nl, r, c), (r, tb, w.shape)

    def body(*refs):
        s_refs = refs[:nl]
        w_ref, m_ref, v_ref, g_ref, d_ref, m2_ref, v2_ref = refs[nl:]

        def total(ref):
            acc = ref[0].astype(F32)
            for j in range(1, ns):
                acc = acc + ref[j].astype(F32)
            return acc

        g = total(s_refs[0])
        for layer in range(1, nl):
            g = jnp.where(pl.program_id(0) == layer, total(s_refs[layer]), g)
        m2 = ADAM_B1 * m_ref[...] + (1.0 - ADAM_B1) * g
        v2 = ADAM_B2 * v_ref[...] + (1.0 - ADAM_B2) * (g * g)
        m_hat = m2 / (1.0 - ADAM_B1 ** ADAM_STEP)
        v_hat = v2 / (1.0 - ADAM_B2 ** ADAM_STEP)
        g_ref[...] = g
        d_ref[...] = -ADAM_LR * (m_hat / (jnp.sqrt(v_hat) + ADAM_EPS) + ADAM_WD * w_ref[...])
        m2_ref[...] = m2
        v2_ref[...] = v2

    def slot_spec(layer):
        return pl.BlockSpec((ns, tb, c), lambda l, i: (0, jnp.where(l == layer, i, 0), 0))

    row = pl.BlockSpec((None, tb, c), lambda l, i: (l, i, 0))
    shp = jax.ShapeDtypeStruct((nl, r, c), F32)
    return pl.pallas_call(
        body, name=name, grid=(nl, r // tb),
        in_specs=[slot_spec(layer) for layer in range(nl)] + [row, row, row],
        out_specs=[row, row, row, row], out_shape=[shp, shp, shp, shp], compiler_params=_cp("arbitrary", "arbitrary"),
    )(*slots, w, m, v)


def _pair_sum(own, got, name, out_dtype, tb):
    nj, _, r, c = own.shape
    mc = lax.axis_index("c")

    def body(mc_ref, a_ref, b_ref, o_ref):
        del mc_ref
        o_ref[...] = (a_ref[...] + b_ref[...]).astype(out_dtype)

    return pl.pallas_call(
        body, name=name,
        grid_spec=pltpu.PrefetchScalarGridSpec(
            num_scalar_prefetch=1, grid=(nj, r // tb),
            in_specs=[pl.BlockSpec((None, None, tb, c), lambda j, i, mc_ref: (j, mc_ref[0], i, 0)),
                      pl.BlockSpec((None, tb, c), lambda j, i, mc_ref: (j, i, 0))],
            out_specs=pl.BlockSpec((None, tb, c), lambda j, i, mc_ref: (j, i, 0))),
        out_shape=jax.ShapeDtypeStruct((nj, r, c), out_dtype), compiler_params=_cp("parallel", "parallel"),
    )(jnp.reshape(mc, (1,)).astype(jnp.int32), own, got)


def _slot_sum(slots, name):
    ns, r, c = slots.shape

    def body(s_ref, o_ref):
        g = s_ref[0]
        for j in range(1, ns):
            g = g + s_ref[j]
        o_ref[...] = g

    return pl.pallas_call(body, name=name, out_shape=jax.ShapeDtypeStruct((r, c), F32))(slots)


def _position():
    return lax.axis_index("x"), lax.axis_index("y"), lax.axis_index("c")


def _comm(exchange, peers, xs, out_shapes, sems, name, collective_id):
    n = len(xs)
    if collective_id is None:
        def body(*refs):
            exchange(refs[:n], refs[n:n + len(out_shapes)], *refs[n + len(out_shapes):])

        return pl.pallas_call(body, name=name, in_specs=[ANY] * n, out_specs=[ANY] * len(out_shapes),
                              out_shape=out_shapes, scratch_shapes=sems)(*xs)
    def launch(*refs):
        barrier = pltpu.get_barrier_semaphore()
        to = peers(*_position())
        for peer in to:
            pl.semaphore_signal(barrier, inc=1, device_id=peer, device_id_type=MESH)
        pl.semaphore_wait(barrier, len(to))
        exchange(refs[:n], refs[n:n + len(out_shapes)], *refs[n + len(out_shapes):])

    return pl.kernel(launch, out_type=out_shapes, mesh=plsc.ScalarSubcoreMesh(axis_name="seq", num_cores=1), name=name,
                     scratch_types=sems, compiler_params=pltpu.CompilerParams(collective_id=collective_id))(*xs)


def _all_gather(xs, name, collective_id=None):
    n = len(xs)
    return _comm(_gather_body, lambda x, y, c: [(x, y, 1 - c), (1 - x, y, c), (x, 1 - y, c), (1 - x, 1 - y, c)], xs,
                 [jax.ShapeDtypeStruct((N_DEV,) + x.shape, x.dtype) for x in xs],
                 [pltpu.SemaphoreType.DMA((n, 7)), pltpu.SemaphoreType.DMA((n, 7)), pltpu.SemaphoreType.DMA((n,))],
                 name, collective_id)


def _gather_body(x_refs, out_refs, send_sems, recv_sems, local_sems):
    n = len(x_refs)
    mx, my, mc = _position()
    me, sibling = (mx, my, mc), (mx, my, 1 - mc)
    chips = [(1 - mx, my), (mx, 1 - my), (1 - mx, 1 - my)]

    def copy(a, k, block, to, own=False):
        dst = out_refs[a].at[4 * block[0] + 2 * block[1] + block[2]]
        return pltpu.make_async_remote_copy(
            src_ref=x_refs[a] if own else dst, dst_ref=dst,
            send_sem=send_sems.at[a, k], recv_sem=recv_sems.at[a, k], device_id=to, device_id_type=MESH)

    mine = [pltpu.make_async_copy(x_refs[a], out_refs[a].at[4 * mx + 2 * my + mc], local_sems.at[a]) for a in range(n)]
    first = [copy(a, 1 + j, me, (*chip, mc), own=True) for j, chip in enumerate(chips) for a in range(n)]
    first += [copy(a, 0, me, sibling, own=True) for a in range(n)]
    for cp in first + mine:
        cp.start()
    passed = []
    for j, chip in enumerate(chips):
        for a in range(n):
            copy(a, 1 + j, (*chip, mc), me).wait_recv()
            passed.append(copy(a, 4 + j, (*chip, mc), sibling))
            passed[-1].start()
    for a in range(n):
        copy(a, 0, sibling, me).wait_recv()
    for j, chip in enumerate(chips):
        for a in range(n):
            copy(a, 4 + j, (*chip, 1 - mc), me).wait_recv()
    for cp in first + passed:
        cp.wait_send()
    for cp in mine:
        cp.wait()


def _exchange_sibling(gs, name, collective_id=None):
    n = len(gs)

    def exchange(g_refs, r_refs, send_sems, recv_sems):
        mx, my, mc = _position()
        cps = [pltpu.make_async_remote_copy(src_ref=g_refs[a].at[:, 1 - mc], dst_ref=r_refs[a],
                                            send_sem=send_sems.at[a], recv_sem=recv_sems.at[a],
                                            device_id=(mx, my, 1 - mc), device_id_type=MESH) for a in range(n)]
        for cp in cps:
            cp.start()
        for cp in cps:
            cp.wait()

    return _comm(exchange, lambda x, y, c: [(x, y, 1 - c)], gs,
                 [jax.ShapeDtypeStruct(g.shape[:1] + g.shape[2:], g.dtype) for g in gs],
                 [pltpu.SemaphoreType.DMA((n,)), pltpu.SemaphoreType.DMA((n,))], name, collective_id)


def _exchange_chips(ss, name, collective_id=None):
    n = len(ss)

    def exchange(s_refs, r_refs, send_sems, recv_sems, local_sems):
        mx, my, mc = _position()
        my_chip = 2 * mx + my
        chips = [(1 - mx, my), (mx, 1 - my), (1 - mx, 1 - my)]

        def copy(a, k, to_slot):
            px, py = chips[k]
            return pltpu.make_async_remote_copy(
                src_ref=s_refs[a].at[2 * px + py], dst_ref=r_refs[a].at[to_slot], send_sem=send_sems.at[a, k],
                recv_sem=recv_sems.at[a, k], device_id=(px, py, mc), device_id_type=MESH)

        sends = [copy(a, k, my_chip) for k in range(3) for a in range(n)]
        local = [pltpu.make_async_copy(s_refs[a].at[my_chip], r_refs[a].at[my_chip], local_sems.at[a])
                 for a in range(n)]
        for cp in sends + local:
            cp.start()
        for k in range(3):
            px, py = chips[k]
            for a in range(n):
                copy(a, k, 2 * px + py).wait_recv()
        for cp in sends:
            cp.wait_send()
        for cp in local:
            cp.wait()

    return _comm(exchange, lambda x, y, c: [(1 - x, y, c), (x, 1 - y, c), (1 - x, 1 - y, c)], ss,
                 [jax.ShapeDtypeStruct(s.shape, s.dtype) for s in ss],
                 [pltpu.SemaphoreType.DMA((n, 3)), pltpu.SemaphoreType.DMA((n, 3)), pltpu.SemaphoreType.DMA((n,))],
                 name, collective_id)


def _cols_concat(g, name, tb=128):
    _, k_dim, n = g.shape

    def body(g_ref, o_ref):
        o_ref[...] = jnp.concatenate([g_ref[d] for d in range(N_DEV)], axis=1)

    return pl.pallas_call(
        body, name=name, grid=(k_dim // tb,),
        in_specs=[pl.BlockSpec((N_DEV, tb, n), lambda i: (0, i, 0))],
        out_specs=pl.BlockSpec((tb, N_DEV * n), lambda i: (i, 0)),
        out_shape=jax.ShapeDtypeStruct((k_dim, N_DEV * n), g.dtype), compiler_params=_cp("parallel"),
    )(g)


def _cols_split(parts, name, tb=128):
    k_dim = parts[0].shape[0]
    n = sum(p.shape[1] for p in parts) // N_DEV

    def body(*refs):
        full = jnp.concatenate([r[...] for r in refs[:-1]], axis=1)
        for d in range(N_DEV):
            refs[-1][d] = full[:, d * n:(d + 1) * n]

    return pl.pallas_call(
        body, name=name, grid=(k_dim // tb,),
        in_specs=[pl.BlockSpec((tb, p.shape[1]), lambda i: (i, 0)) for p in parts],
        out_specs=pl.BlockSpec((N_DEV, tb, n), lambda i: (0, i, 0)),
        out_shape=jax.ShapeDtypeStruct((N_DEV, k_dim, n), parts[0].dtype), compiler_params=_cp("parallel"),
    )(*parts)


_Q0, _GL0 = 7200, 8224
N_SHARD_IN = N_IN // N_DEV


def _w_in_regions(g, name, tb=128):
    def body(g_ref, ssd_ref, lru_ref, q_ref, gl_ref):
        full = jnp.concatenate([g_ref[d] for d in range(N_DEV)], axis=1)
        lru_ref[...] = full[:, 0:2 * D]
        ssd_ref[:, :S_DT] = full[:, 2 * D:2 * D + S_DT]
        ssd_ref[:, S_DT:] = jnp.zeros((tb, W_SSD - S_DT), g.dtype)
        ssd_ref[:, S_DT:S_DT + DT_REAL] = full[:, 2 * D + S_DT:_Q0]
        q_ref[...] = full[:, _Q0:_GL0]
        gl_ref[...] = full[:, _GL0:N_IN]

    widths = (W_SSD, 2 * D, D, 3 * D)
    return pl.pallas_call(
        body, name=name, grid=(D // tb,),
        in_specs=[pl.BlockSpec((N_DEV, tb, N_SHARD_IN), lambda i: (0, i, 0))],
        out_specs=[pl.BlockSpec((tb, wd), lambda i: (i, 0)) for wd in widths],
        out_shape=[jax.ShapeDtypeStruct((D, wd), g.dtype) for wd in widths], compiler_params=_cp("parallel"),
    )(g)


def _w_in_shards(dssd, dlru, dq, dgl, name, tb=128):
    def body(ssd_ref, lru_ref, q_ref, gl_ref, o_ref):
        full = jnp.concatenate([lru_ref[...], ssd_ref[:, :S_DT + DT_REAL], q_ref[...], gl_ref[...]], axis=1)
        for d in range(N_DEV):
            o_ref[d] = full[:, d * N_SHARD_IN:(d + 1) * N_SHARD_IN]

    return pl.pallas_call(
        body, name=name, grid=(D // tb,),
        in_specs=[pl.BlockSpec((tb, a.shape[1]), lambda i: (i, 0)) for a in (dssd, dlru, dq, dgl)],
        out_specs=pl.BlockSpec((N_DEV, tb, N_SHARD_IN), lambda i: (0, i, 0)),
        out_shape=jax.ShapeDtypeStruct((N_DEV, D, N_SHARD_IN), F32), compiler_params=_cp("parallel"),
    )(dssd, dlru, dq, dgl)


_BIG = (("w_in", "col", (1024, 1412)), ("mem_w_kv", "col", (1024, 256)), ("w_br_lru", "row", (128, 1024)),
        ("w_br_ssd", "row", (256, 1024)), ("w_br_xa", "row", (128, 1024)), ("w_out", "row", (128, 1024)),
        ("ffn_w_in", "col", (1024, 704)), ("ffn_w_down", "row", (352, 1024)))
_SMALL = (("b_gate", (3, 128)), ("lru_conv_w", (4, 128)), ("ssd_conv_w", (4, 384)))
_REP = (("lru_conv_b", (1024,)), ("lru_w_a", (8, 128, 128)), ("lru_b_a", (1024,)), ("lru_w_i", (8, 128, 128)),
        ("lru_b_i", (1024,)), ("lru_lambda", (1024,)), ("ssd_conv_b", (3072,)), ("ssd_dt_bias", (32,)),
        ("ssd_a_log", (32,)), ("ssd_d", (32,)), ("ssd_norm_w", (2048,)), ("ln1_g", (1024,)), ("ln1_b", (1024,)),
        ("ln2_g", (1024,)), ("ln2_b", (1024,)))
_ORDER = ("w_in", "b_gate", "lru_conv_w", "lru_conv_b", "lru_w_a", "lru_b_a", "lru_w_i", "lru_b_i", "lru_lambda",
          "ssd_conv_w", "ssd_conv_b", "ssd_dt_bias", "ssd_a_log", "ssd_d", "ssd_norm_w", "mem_w_kv", "w_br_lru",
          "w_br_ssd", "w_br_xa", "w_out", "ln1_g", "ln1_b", "ffn_w_in", "ffn_w_down", "ln2_g", "ln2_b")

LANES = 1024
N_SMALL = sum(DEPTH * s[0] * s[1] for _, s in _SMALL)
R_SMALL = 8
N_REP = sum(DEPTH * math.prod(s) for _, s in _REP)
R_REP = 68
R_SM = R_SMALL + R_REP + 4
R_TAIL = R_SMALL + N_DEV * R_REP
TB_TAIL = 184
assert N_SMALL <= R_SMALL * LANES and N_REP <= N_DEV * R_REP * LANES


def _rows(flat, rows):
    return jnp.pad(flat, (0, rows * LANES - flat.shape[0])).reshape(rows, LANES)


def _rowblk(a, cap):
    return max(b for b in range(16, cap + 1, 16) if a % b == 0)


def _pack_tail(d):
    small = jnp.concatenate([d[n].reshape(-1) for n, _ in _SMALL])
    rep = jnp.concatenate([d[n].reshape(-1) for n, _ in _REP])
    return jnp.concatenate([_rows(small, R_SMALL), _rows(rep, N_DEV * R_REP)], axis=0)


def _unpack_tail(a):
    out, o = {}, 0
    flat = a[:R_SMALL].reshape(-1)
    for n, s in _SMALL:
        k = DEPTH * math.prod(s)
        out[n] = flat[o:o + k].reshape((DEPTH,) + s)
        o += k
    flat, o = a[R_SMALL:].reshape(-1), 0
    for n, s in _REP:
        k = DEPTH * math.prod(s)
        out[n] = flat[o:o + k].reshape((DEPTH,) + s)
        o += k
    return out


def _by_dest(g):
    g = g.reshape(g.shape[:-1] + (N_DEV, g.shape[-1] // N_DEV))
    return jnp.moveaxis(g, -2, 0).reshape(N_DEV, -1)


def _from_stack(st):
    st = jnp.moveaxis(st, 0, -2)
    return st.reshape(st.shape[:-2] + (st.shape[-2] * st.shape[-1],))


def _layer_fwd(x, xb, mem, w, l):
    nm = lambda s: f"{s}_l{l}"
    wi = w["wi"]
    row = lambda v: v.reshape(1, -1)
    s = dict(x=x, xb=xb, wi=wi)
    s["p_ssd"] = _mm(xb, wi["ssd"], name=nm("proj_ssd"))
    s["p_lru"] = _mm(xb, wi["lru"], name=nm("proj_lru"))
    s["p_q"] = _mm(xb, wi["q"], out_dtype=BF16, name=nm("proj_q"))
    s["p_gl"] = _mm(xb, wi["gl"], out_dtype=BF16, name=nm("proj_gl"))
    s["lru_par"] = (w["lru_conv_w"], row(w["lru_conv_b"]), w["lru_w_a"], row(w["lru_b_a"]), w["lru_w_i"],
                    row(w["lru_b_i"]), row(w["lru_lambda"]))
    s["y_lru"], s["h"], s["xc"] = _lru_fwd(s["p_lru"], *s["lru_par"], name=nm("lru_fwd"))
    s["ssd_par"] = (w["ssd_conv_w"], row(w["ssd_conv_b"]), row(w["ssd_dt_bias"]), row(w["ssd_a_log"]),
                    row(w["ssd_d"]), row(w["ssd_norm_w"]))
    s["y_ssd"], s["yraw"], s["hs"], s["c_ssd"] = _ssd_fwd(s["p_ssd"], *s["ssd_par"], name=nm("ssd_fwd"))
    s["kv"] = _mm(mem, w["mem_w_kv"], name=nm("kv"))
    s["y_xa"] = _xa_fwd(s["p_q"], s["kv"], name=nm("xa_fwd"))
    s["b1"] = _mm(s["y_lru"], w["w_br_lru"], out_dtype=BF16, name=nm("br_lru"))
    s["b2"] = _mm(s["y_ssd"], w["w_br_ssd"], out_dtype=BF16, name=nm("br_ssd"))
    s["b3"] = _mm(s["y_xa"], w["w_br_xa"], out_dtype=BF16, name=nm("br_xa"))
    s["bg"] = row(w["b_gate"])
    s["merged"] = _merge_fwd(s["p_gl"], s["bg"], s["b1"], s["b2"], s["b3"], name=nm("merge_fwd"))
    s["mix"] = _mm(s["merged"], w["w_out"], name=nm("out_proj"))
    s["x1"], s["x1b"] = _ln_fwd(x, s["mix"], row(w["ln1_g"]), row(w["ln1_b"]), name=nm("ln1_fwd"))
    s["gate"], s["up"], s["act"] = _ffn_in_swiglu(s["x1b"], w["ffn_w_in"], name=nm("ffn_in"))
    s["f"] = _mm(s["act"], w["ffn_w_down"], name=nm("ffn_down"))
    s["x2"], s["x2b"] = _ln_fwd(s["x1"], s["f"], row(w["ln2_g"]), row(w["ln2_b"]), name=nm("ln2_fwd"))
    return s


def _layer_bwd(s, mem, w, dxo, l, hooks=None):
    nm = lambda t: f"{t}_l{l}"
    g = {}
    hook = lambda stage, t: hooks[stage](t, g) if hooks and stage in hooks else t
    row = lambda v: v.reshape(1, -1)
    slabs = lambda a: a.reshape(N_DEV, a.shape[0] // N_DEV, a.shape[1])
    du2, dg, db = _ln_bwd(s["x1"], s["f"], dxo, row(w["ln2_g"]), name=nm("ln2_bwd"))
    g["ln2_g"], g["ln2_b"] = dg[0], db[0]
    dgate, dup = _d_swiglu(du2, w["ffn_w_down"], s["gate"], s["up"], name=nm("d_swiglu"))
    g["ffn_w_down"] = slabs(_mm(s["act"], du2, ta=True, name=nm("dw_ffn_down")))
    dx1 = _mm(dgate, w["ffn_w_in"][:, :D_FF], tb=True, add=du2, add_scale=ALPHA, name=nm("d_x1_gate"))
    dx1 = _mm(dup, w["ffn_w_in"][:, D_FF:], tb=True, add=dx1, name=nm("d_x1_up"))
    g["ffn_w_in"] = _cols_split([_mm(s["x1b"], dgate, ta=True, name=nm("dw_ffn_gate")),
                                 _mm(s["x1b"], dup, ta=True, name=nm("dw_ffn_up"))], name=nm("dw_ffn_in_shards"))
    du1, dg, db = _ln_bwd(s["x"], s["mix"], dx1, row(w["ln1_g"]), name=nm("ln1_bwd"))
    g["ln1_g"], g["ln1_b"] = dg[0], db[0]
    dmerged = hook("mid", _mm(du1, w["w_out"], tb=True, name=nm("d_merged")))
    g["w_out"] = slabs(_mm(s["merged"], du1, ta=True, name=nm("dw_out")))
    dp_gl, d1, d2, d3, dbg = _merge_bwd(s["p_gl"], s["bg"], s["b1"], s["b2"], s["b3"], dmerged, name=nm("merge_bwd"))
    g["b_gate"] = dbg.reshape(3, D)
    dy_lru = _mm(d1, w["w_br_lru"], tb=True, name=nm("d_y_lru"))
    g["w_br_lru"] = slabs(_mm(s["y_lru"], d1, ta=True, name=nm("dw_br_lru")))
    dy_ssd = _mm(d2, w["w_br_ssd"], tb=True, name=nm("d_y_ssd"))
    g["w_br_ssd"] = slabs(_mm(s["y_ssd"], d2, ta=True, name=nm("dw_br_ssd")))
    dy_xa = _mm(d3, w["w_br_xa"], tb=True, out_dtype=BF16, name=nm("d_y_xa"))
    g["w_br_xa"] = slabs(_mm(s["y_xa"], d3, ta=True, name=nm("dw_br_xa")))
    dp_q, dkv = _xa_bwd(s["p_q"], s["kv"], dy_xa, name=nm("xa_bwd"))
    g["mem_w_kv"] = _mm(mem, dkv, ta=True, split_n=2 * D // N_DEV, name=nm("dw_kv"))
    dy_ssd = hook("branches", dy_ssd)
    ssd_cw, _, *ssd_rest = s["ssd_par"]
    dp_ssd, dcw, dcb, ddtb, dalog, dd, dnw = _ssd_bwd(s["p_ssd"], s["c_ssd"], s["yraw"], s["hs"], dy_ssd, ssd_cw,
                                                      *ssd_rest, name=nm("ssd_bwd"))
    g["ssd_conv_w"], g["ssd_conv_b"], g["ssd_dt_bias"] = dcw, dcb[0], ddtb[0]
    g["ssd_a_log"], g["ssd_d"], g["ssd_norm_w"] = dalog[0], dd[0], dnw[0]
    dp_ssd = hook("ssd", dp_ssd)
    lru_cw, _, *lru_rest = s["lru_par"]
    dp_lru, dcw, dcb, dwa, dba, dwi, dbi, dlam = _lru_bwd(s["p_lru"], s["xc"], s["h"], dy_lru, lru_cw, *lru_rest,
                                                          name=nm("lru_bwd"))
    g["lru_conv_w"], g["lru_conv_b"], g["lru_w_a"], g["lru_b_a"] = dcw, dcb[0], dwa, dba[0]
    g["lru_w_i"], g["lru_b_i"], g["lru_lambda"] = dwi, dbi[0], dlam[0]
    wi, x = s["wi"], s["xb"]
    g["w_in"] = _w_in_shards(_mm(x, dp_ssd, ta=True, name=nm("dw_in_ssd")), _mm(x, dp_lru, ta=True, name=nm("dw_in_lru")),
                             _mm(x, dp_q, ta=True, name=nm("dw_in_q")), _mm(x, dp_gl, ta=True, name=nm("dw_in_gl")),
                             name=nm("dw_in_shards"))
    dp_ssd = hook("weights", dp_ssd)
    dx = _mm(dp_ssd, wi["ssd"], tb=True, add=du1, add_scale=ALPHA, name=nm("dx_ssd"))
    dx = hook("dx", _mm(dp_lru, wi["lru"], tb=True, add=dx, name=nm("dx_lru")))
    dx = _mm(dp_q, wi["q"], tb=True, add=dx, name=nm("dx_q"))
    dx = _mm(dp_gl, wi["gl"], tb=True, add=dx, name=nm("dx_gl"))
    return dx, g


def _local_step(x, mem, target, layers, hooks=None):
    saved, xb = [], x.astype(BF16)
    for l in range(DEPTH):
        saved.append(_layer_fwd(x, xb, mem, layers[l], l))
        x, xb = saved[-1]["x2"], saved[-1]["x2b"]
    dx, loss = _loss_head(x, target, name="loss_head")
    grads = [None] * DEPTH
    for l in reversed(range(DEPTH)):
        dx, grads[l] = _layer_bwd(saved[l], mem, layers[l], dx, l, hooks[l] if hooks else None)
    return loss, dx, grads


def kernel(x, mem, w_in, b_gate, lru_conv_w, lru_conv_b, lru_w_a, lru_b_a, lru_w_i, lru_b_i, lru_lambda, ssd_conv_w, ssd_conv_b, ssd_dt_bias, ssd_a_log, ssd_d, ssd_norm_w, mem_w_kv, w_br_lru, w_br_ssd, w_br_xa, w_out, ln1_g, ln1_b, ffn_w_in, ffn_w_down, ln2_g, ln2_b, loss_target, m_w_in, m_b_gate, m_lru_conv_w, m_lru_conv_b, m_lru_w_a, m_lru_b_a, m_lru_w_i, m_lru_b_i, m_lru_lambda, m_ssd_conv_w, m_ssd_conv_b, m_ssd_dt_bias, m_ssd_a_log, m_ssd_d, m_ssd_norm_w, m_mem_w_kv, m_w_br_lru, m_w_br_ssd, m_w_br_xa, m_w_out, m_ln1_g, m_ln1_b, m_ffn_w_in, m_ffn_w_down, m_ln2_g, m_ln2_b, v_w_in, v_b_gate, v_lru_conv_w, v_lru_conv_b, v_lru_w_a, v_lru_b_a, v_lru_w_i, v_lru_b_i, v_lru_lambda, v_ssd_conv_w, v_ssd_conv_b, v_ssd_dt_bias, v_ssd_a_log, v_ssd_d, v_ssd_norm_w, v_mem_w_kv, v_w_br_lru, v_w_br_ssd, v_w_br_xa, v_w_out, v_ln1_g, v_ln1_b, v_ffn_w_in, v_ffn_w_down, v_ln2_g, v_ln2_b):
    local = dict(locals())
    w = {n: local[n] for n in _ORDER}
    m = {n: local["m_" + n] for n in _ORDER}
    v = {n: local["v_" + n] for n in _ORDER}

    big = [n for n, _, _ in _BIG]
    kinds = {n: kind for n, kind, _ in _BIG}

    small = _rows(jnp.concatenate([w[n].reshape(-1) for n, _ in _SMALL]), R_SMALL)
    first = _all_gather([w["w_in"][0].astype(BF16), small], name="gather_w_in_l0")
    rest, later, _ = lax.optimization_barrier(([w[n][0].astype(BF16) for n in big[1:]],
                                               [w[n][1].astype(BF16) for n in big], first[-1]))
    rest = _all_gather(rest, "gather_weights_l0", collective_id=1)
    later = _all_gather(later, "gather_weights_l1", collective_id=4)
    stacks = [dict(zip(big, [first[0], *rest])), dict(zip(big, later))]
    small_all, o, small_full = first[-1].reshape(N_DEV, R_SMALL * LANES), 0, {}
    for n, s in _SMALL:
        k = DEPTH * s[0] * s[1]
        small_full[n] = _from_stack(small_all[:, o:o + k].reshape((N_DEV, DEPTH) + s))
        o += k
    layers = []
    for l in range(DEPTH):
        lw = {n: w[n][l] for n, _ in _REP}
        lw.update({n: small_full[n][l] for n, _ in _SMALL})
        lw["wi"] = dict(zip(("ssd", "lru", "q", "gl"), _w_in_regions(stacks[l]["w_in"], name=f"w_in_regions_l{l}")))
        for n in big[1:]:
            if kinds[n] == "col":
                lw[n] = _cols_concat(stacks[l][n], name=f"full_{n}_l{l}")
            else:
                lw[n] = stacks[l][n].reshape(-1, stacks[l][n].shape[-1])
        layers.append(lw)

    by_dest = lambda a: a.reshape((4, 2) + a.shape[1:])
    slots, pending, last_layer = {}, {}, {}
    queue = [stacks[1]["w_out"]]

    def after_last(operands):
        operands, _ = lax.optimization_barrier((list(operands), queue[-1]))
        return operands

    def start(tag, collective_id, names_and_grads):
        names, owns = zip(*names_and_grads)
        gots = _exchange_sibling(after_last(owns), name=f"reduce_cores_{tag}", collective_id=collective_id)
        queue.append(gots[0])
        pending[tag] = (names, owns, gots)

    def finish(tag, collective_id, t):
        names, owns, gots = pending.pop(tag)
        t, gots = lax.optimization_barrier((t, gots))
        sums = [_pair_sum(own, got, name=f"pair_sum_{tag}_{n}", out_dtype=F32 if n == "tail" else BF16,
                          tb=R_SM if n == "tail" else _rowblk(own.shape[2], 256))
                for n, own, got in zip(names, owns, gots)]
        t, sums = lax.optimization_barrier((t, sums))
        got = _exchange_chips(sums, name=f"reduce_chips_{tag}", collective_id=collective_id)
        queue.append(got[0])
        slots.update({(tag, n): s for n, s in zip(names, got)})
        return t

    def tail_of(g0):
        stacked = {n: jnp.stack([g0[n], last_layer[n]]) for n in [s[0] for s in _SMALL + _REP]}
        sm = jnp.concatenate([_by_dest(stacked[n]) for n, _ in _SMALL], axis=1)
        sm = jnp.pad(sm, ((0, 0), (0, R_SMALL * LANES - sm.shape[1])))
        rep = jnp.concatenate([stacked[n].reshape(-1) for n, _ in _REP])
        rep = jnp.pad(rep, (0, N_DEV * R_REP * LANES - rep.shape[0])).reshape(N_DEV, R_REP * LANES)
        tail = jnp.concatenate([sm, rep, jnp.zeros((N_DEV, (R_SM - R_SMALL - R_REP) * LANES), F32)], axis=1)
        return tail.reshape(4, 2, R_SM, LANES)

    def weights_l1(t, g):
        last_layer.update(g)
        start("l1", 2, [(n, by_dest(g[n])) for n in big])
        return t

    def branches_l0(t, g):
        start("l0a", 5, [(n, by_dest(g[n])) for n in big[1:]])
        return t

    def weights_l0(t, g):
        start("l0b", 7, [("w_in", by_dest(g["w_in"])), ("tail", tail_of(g))])
        return t

    hooks = [{"branches": branches_l0, "ssd": lambda t, g: finish("l0a", 6, t), "weights": weights_l0,
              "dx": lambda t, g: finish("l0b", 8, t)},
             {"weights": weights_l1, "dx": lambda t, g: finish("l1", 3, t)}]
    loss_tile, dx, grads = _local_step(x[0], mem[0], loss_target[0], layers, hooks)
    loss = lax.psum(loss_tile[0, 0], ("x", "y", "c"))

    res = {}
    for n in big:
        tb = _rowblk(w[n].shape[1], 128 if w[n].shape[2] > LANES else 256)
        res[n] = _adamw([slots["l0b" if n == "w_in" else "l0a", n], slots["l1", n]], w[n], m[n], v[n],
                        name=f"adamw_{n}", tb=tb)
    tail_sum = _slot_sum(slots["l0b", "tail"], name="sum_tail")
    rep_all = _all_gather([tail_sum[R_SMALL:R_SMALL + R_REP]], name="gather_replicated")[0]
    g_tail = jnp.concatenate([tail_sum[:R_SMALL], rep_all.reshape(N_DEV * R_REP, LANES)], axis=0)
    tails = _adamw([g_tail[None]], _pack_tail(w)[None], _pack_tail(m)[None], _pack_tail(v)[None],
                   name="adamw_tail", tb=TB_TAIL)

    outs = []
    for kind in range(4):
        d = {**{n: res[n][kind] for n in big}, **_unpack_tail(tails[kind][0])}
        outs += [d[n] for n in _ORDER]
    return (loss, dx[None], *outs)
```

```python
import math

import jax
import jax.numpy as jnp
from jax import lax
from jax.experimental import pallas as pl
from jax.experimental.pallas import tpu as pltpu
from jax.experimental.pallas import tpu_sc as plsc

F32 = jnp.float32
BF16 = jnp.bfloat16

D = 1024
DEPTH = 2
N_DEV = 8
CHUNK = 64
LRU_BLOCKS = 8
LRU_BLOCK = 128
LRU_C = 8.0
D_SSD = 2 * D
SSD_HEADS = 32
SSD_GROUPS = 4
GROUP_W = D_SSD // SSD_GROUPS
SSD_STATE = 128
D_XBC = D_SSD + 2 * SSD_GROUPS * SSD_STATE
XA_HEADS = 4
XA_HEAD_DIM = 256
D_FF = 2816
ALPHA = (2 * DEPTH) ** 0.25
EPS = 1e-5
N_IN = 11296

S_Z, S_XBC, S_DT, W_SSD = 0, 2048, 5120, 5632
DT_REAL = 32

ADAM_LR, ADAM_B1, ADAM_B2, ADAM_EPS, ADAM_WD, ADAM_STEP = 0.001, 0.9, 0.999, 1e-08, 0.01, 10

VMEM_LIMIT = 56 * 1024 * 1024
MESH = pl.DeviceIdType.MESH
ANY = pl.BlockSpec(memory_space=pl.ANY)


def _cp(*sem):
    return pltpu.CompilerParams(dimension_semantics=sem, vmem_limit_bytes=VMEM_LIMIT)


def _blk(n, target):
    if n % 128:
        return n
    best = 128
    for b in range(128, min(n, target) + 1, 128):
        if n % b == 0:
            best = b
    return best


def _iota(shape, dim):
    return lax.broadcasted_iota(jnp.int32, shape, dim)


def _sigmoid(x):
    return 0.5 + 0.5 * jnp.tanh(0.5 * x)


def _log1p(e):
    u = 1.0 + e
    return jnp.where(u == 1.0, e, jnp.log(u) * (e / (u - 1.0)))


def _softplus(x):
    return jnp.maximum(x, 0.0) + _log1p(jnp.exp(-jnp.abs(x)))


_G0 = math.sqrt(2.0 / math.pi)
_G1 = 0.044715


def _gelu_and_grad(x):
    x2 = x * x
    u = 0.5 + 0.5 * jnp.tanh(x * (_G0 + (_G0 * _G1) * x2))
    dg = u + (x * (u * (1.0 - u))) * ((2.0 * _G0) + (6.0 * _G0 * _G1) * x2)
    return x * u, dg


_NN = (((1,), (0,)), ((), ()))
_NT = (((1,), (1,)), ((), ()))
_TN = (((0,), (0,)), ((), ()))


def _dot(a, b, dims=_NN):
    return lax.dot_general(a.astype(BF16), b.astype(BF16), dims, preferred_element_type=F32)


def _dot_hi(a, b, dims=_NN):
    return lax.dot_general(a, b, dims, precision=lax.Precision.HIGHEST, preferred_element_type=F32)


def _split(v):
    hi = v.astype(BF16)
    return hi, (v - hi.astype(F32)).astype(BF16)


def _dot01(v, e, dims=_NN):
    hi, lo = _split(v)
    return (lax.dot_general(hi, e, dims, preferred_element_type=F32)
            + lax.dot_general(lo, e, dims, preferred_element_type=F32))


def _conv_taps(xe, n):
    return [xe[8:8 + n] if j == 3 else pltpu.roll(xe, 3 - j, 0)[8:8 + n] for j in range(4)]


def _conv_fwd(taps, cw, cb):
    return cb + cw[0:1] * taps[0] + cw[1:2] * taps[1] + cw[2:3] * taps[2] + cw[3:4] * taps[3]


def _conv_bwd(dc, dnext, x, cw, n):
    ext = jnp.concatenate([dc, dnext], axis=0)
    shifted = [pltpu.roll(ext, n + 8 - (3 - j), 0)[0:n] for j in range(3)] + [dc]
    dx = cw[0:1] * shifted[0] + cw[1:2] * shifted[1] + cw[2:3] * shifted[2] + cw[3:4] * dc
    dcw = jnp.concatenate([jnp.sum(x * shifted[j], axis=0, keepdims=True) for j in range(4)], axis=0)
    return dx, dcw, jnp.sum(dc, axis=0, keepdims=True)


MM_VMEM_BUDGET = 44 * 1024 * 1024
MM_MAX_TILE = 1408
MM_MAX_K = 5632


def _divisors(n, cap):
    return [n] if n % 128 else [b for b in range(128, min(n, cap) + 1, 128) if n % b == 0]


def _mm_tiles(m_dim, n_dim, k_dim, a_bytes, b_bytes, o_bytes, has_add, tn_fixed):
    best = None
    for tm in _divisors(m_dim, MM_MAX_TILE):
        for tn in ([tn_fixed] if tn_fixed else _divisors(n_dim, MM_MAX_TILE)):
            for tk in _divisors(k_dim, MM_MAX_K):
                vmem = 2 * (tm * tk * a_bytes + tk * tn * b_bytes + tm * tn * (o_bytes + (4 if has_add else 0)))
                vmem += tm * tn * 4 if tk < k_dim else 0
                if vmem <= MM_VMEM_BUDGET:
                    key = (tm * tn * tk, tk, tn)
                    if best is None or key > best[0]:
                        best = (key, (tm, tn, tk))
    assert best is not None, (m_dim, n_dim, k_dim)
    return best[1]


def _mm(a, b, *, ta=False, tb=False, out_dtype=F32, add=None, add_scale=1.0, name, split_n=None):
    if ta:
        k_dim, m_dim = a.shape
    else:
        m_dim, k_dim = a.shape
    if tb:
        n_dim, k2 = b.shape
    else:
        k2, n_dim = b.shape
    assert k_dim == k2, (a.shape, b.shape, ta, tb)
    tm, tn, tk = _mm_tiles(m_dim, n_dim, k_dim, a.dtype.itemsize, b.dtype.itemsize, jnp.dtype(out_dtype).itemsize,
                           add is not None, split_n)
    nk = k_dim // tk
    a_spec = pl.BlockSpec((tk, tm), lambda i, j, k: (k, i)) if ta else pl.BlockSpec((tm, tk), lambda i, j, k: (i, k))
    b_spec = pl.BlockSpec((tn, tk), lambda i, j, k: (j, k)) if tb else pl.BlockSpec((tk, tn), lambda i, j, k: (k, j))
    o_spec = pl.BlockSpec((tm, tn), lambda i, j, k: (i, j))
    out_shape = (m_dim, n_dim)
    if split_n is not None:
        assert add is None and tn == split_n, (tn, split_n)
        o_spec = pl.BlockSpec((None, tm, tn), lambda i, j, k: (j, i, 0))
        out_shape = (n_dim // tn, m_dim, tn)
    dims = (((0 if ta else 1,), (1 if tb else 0,)), ((), ()))
    has_add = add is not None

    def body(*refs):
        a_ref, b_ref = refs[:2]
        add_ref = refs[2] if has_add else None
        o_ref = refs[3] if has_add else refs[2]
        acc_ref = refs[-1] if nk > 1 else None
        k = pl.program_id(2)

        def product():
            return lax.dot_general(a_ref[...].astype(BF16), b_ref[...].astype(BF16), dims, preferred_element_type=F32)

        def finish(r):
            if has_add:
                r = r + add_scale * add_ref[...]
            o_ref[...] = r.astype(out_dtype)

        if nk == 1:
            finish(product())
            return

        @pl.when(k == 0)
        def _():
            acc_ref[...] = product()

        @pl.when((k > 0) & (k < nk - 1))
        def _():
            acc_ref[...] += product()

        @pl.when(k == nk - 1)
        def _():
            finish(acc_ref[...] + product())

    in_specs = [a_spec, b_spec] + ([o_spec] if has_add else [])
    args = (a, b) + ((add,) if has_add else ())
    return pl.pallas_call(
        body, name=name, grid=(m_dim // tm, n_dim // tn, nk),
        in_specs=in_specs, out_specs=o_spec,
        out_shape=jax.ShapeDtypeStruct(out_shape, out_dtype),
        scratch_shapes=[pltpu.VMEM((tm, tn), F32)] if nk > 1 else [],
        cost_estimate=pl.CostEstimate(
            flops=2 * m_dim * n_dim * k_dim, transcendentals=0,
            bytes_accessed=a.size * a.dtype.itemsize + b.size * b.dtype.itemsize
            + m_dim * n_dim * (jnp.dtype(out_dtype).itemsize + (4 if has_add else 0))),
        compiler_params=_cp("parallel", "parallel", "arbitrary"),
    )(*args)


def _ln_fwd(x, f, g, b, name, tb=512):
    t = x.shape[0]
    tb = min(tb, t)

    def body(x_ref, f_ref, g_ref, b_ref, o_ref, ob_ref):
        u = ALPHA * x_ref[...] + f_ref[...]
        mu = jnp.mean(u, axis=-1, keepdims=True)
        d = u - mu
        var = jnp.mean(d * d, axis=-1, keepdims=True)
        y = d * lax.rsqrt(var + EPS) * g_ref[...] + b_ref[...]
        o_ref[...] = y
        ob_ref[...] = y.astype(BF16)

    row = pl.BlockSpec((tb, D), lambda i: (i, 0))
    par = pl.BlockSpec((1, D), lambda i: (0, 0))
    return pl.pallas_call(
        body, name=name, grid=(t // tb,), in_specs=[row, row, par, par], out_specs=[row, row],
        out_shape=[jax.ShapeDtypeStruct((t, D), F32), jax.ShapeDtypeStruct((t, D), BF16)],
        compiler_params=_cp("parallel"),
    )(x, f, g, b)


def _ln_bwd(x, f, dy, g, name, tb=512):
    t = x.shape[0]
    tb = min(tb, t)

    def body(x_ref, f_ref, dy_ref, g_ref, du_ref, dg_ref, db_ref):
        @pl.when(pl.program_id(0) == 0)
        def _():
            dg_ref[...] = jnp.zeros_like(dg_ref)
            db_ref[...] = jnp.zeros_like(db_ref)

        u = ALPHA * x_ref[...] + f_ref[...]
        mu = jnp.mean(u, axis=-1, keepdims=True)
        d = u - mu
        var = jnp.mean(d * d, axis=-1, keepdims=True)
        rstd = lax.rsqrt(var + EPS)
        xhat = d * rstd
        dy = dy_ref[...]
        dxh = dy * g_ref[...]
        m1 = jnp.mean(dxh, axis=-1, keepdims=True)
        m2 = jnp.mean(dxh * xhat, axis=-1, keepdims=True)
        du_ref[...] = rstd * (dxh - m1 - xhat * m2)
        dg_ref[...] += jnp.sum(dy * xhat, axis=0, keepdims=True)
        db_ref[...] += jnp.sum(dy, axis=0, keepdims=True)

    row = pl.BlockSpec((tb, D), lambda i: (i, 0))
    par = pl.BlockSpec((1, D), lambda i: (0, 0))
    return pl.pallas_call(
        body, name=name, grid=(t // tb,), in_specs=[row, row, row, par], out_specs=[row, par, par],
        out_shape=[jax.ShapeDtypeStruct((t, D), F32), jax.ShapeDtypeStruct((1, D), F32),
                   jax.ShapeDtypeStruct((1, D), F32)],
        compiler_params=_cp("arbitrary"),
    )(x, f, dy, g)


FFN_TM, FFN_TN = 512, D_FF // 2


def _ffn_in_swiglu(x, w, name):
    t = x.shape[0]
    tm = min(FFN_TM, t)
    nj = D_FF // FFN_TN

    def body(x_ref, wg_ref, wu_ref, g_ref, u_ref, a_ref):
        xb = x_ref[...].astype(BF16)
        g = lax.dot_general(xb, wg_ref[...], _NT, preferred_element_type=F32)
        u = lax.dot_general(xb, wu_ref[...], _NT, preferred_element_type=F32)
        g_ref[...] = g.astype(BF16)
        u_ref[...] = u.astype(BF16)
        a_ref[...] = (g * _sigmoid(g) * u).astype(BF16)

    tile = pl.BlockSpec((tm, FFN_TN), lambda i, j: (i, j))
    return pl.pallas_call(
        body, name=name, grid=(t // tm, nj),
        in_specs=[pl.BlockSpec((tm, D), lambda i, j: (i, 0)), pl.BlockSpec((FFN_TN, D), lambda i, j: (j, 0)),
                  pl.BlockSpec((FFN_TN, D), lambda i, j: (nj + j, 0))],
        out_specs=[tile, tile, tile],
        out_shape=[jax.ShapeDtypeStruct((t, D_FF), BF16)] * 3,
        compiler_params=_cp("parallel", "parallel"),
    )(x, w, w)


def _d_swiglu(du, w_down, g, u, name):
    t = du.shape[0]
    tm = min(FFN_TM, t)

    def body(du_ref, w_ref, g_ref, u_ref, dg_ref, dup_ref):
        da = lax.dot_general(du_ref[...].astype(BF16), w_ref[...], _NT, preferred_element_type=F32)
        g_v = g_ref[...].astype(F32)
        s = _sigmoid(g_v)
        dg_ref[...] = (da * u_ref[...].astype(F32) * (s * (1.0 + g_v * (1.0 - s)))).astype(BF16)
        dup_ref[...] = (da * g_v * s).astype(BF16)

    tile = pl.BlockSpec((tm, FFN_TN), lambda i, j: (i, j))
    return pl.pallas_call(
        body, name=name, grid=(t // tm, D_FF // FFN_TN),
        in_specs=[pl.BlockSpec((tm, D), lambda i, j: (i, 0)), pl.BlockSpec((FFN_TN, D), lambda i, j: (j, 0)), tile, tile],
        out_specs=[tile, tile],
        out_shape=[jax.ShapeDtypeStruct((t, D_FF), BF16), jax.ShapeDtypeStruct((t, D_FF), BF16)],
        compiler_params=_cp("parallel", "parallel"),
    )(du, w_down, g, u)


def _merge_fwd(pgl, bg, b1, b2, b3, name, tb=512):
    t = pgl.shape[0]
    tb = min(tb, t)

    def body(gl_ref, bg_ref, b1_ref, b2_ref, b3_ref, o_ref):
        acc = None
        for j, b_ref in enumerate((b1_ref, b2_ref, b3_ref)):
            sl = slice(j * D, (j + 1) * D)
            term = _sigmoid(gl_ref[:, sl].astype(F32) + bg_ref[:, sl]) * b_ref[...].astype(F32)
            acc = term if acc is None else acc + term
        o_ref[...] = acc.astype(BF16)

    row = pl.BlockSpec((tb, D), lambda i: (i, 0))
    return pl.pallas_call(
        body, name=name, grid=(t // tb,),
        in_specs=[pl.BlockSpec((tb, 3 * D), lambda i: (i, 0)), pl.BlockSpec((1, 3 * D), lambda i: (0, 0)), row, row, row],
        out_specs=row, out_shape=jax.ShapeDtypeStruct((t, D), BF16), compiler_params=_cp("parallel"),
    )(pgl, bg, b1, b2, b3)


def _merge_bwd(pgl, bg, b1, b2, b3, dm, name, tb=512):
    t = pgl.shape[0]
    tb = min(tb, t)

    def body(gl_ref, bg_ref, b1_ref, b2_ref, b3_ref, dm_ref, dgl_ref, d1_ref, d2_ref, d3_ref, dbg_ref):
        @pl.when(pl.program_id(0) == 0)
        def _():
            dbg_ref[...] = jnp.zeros_like(dbg_ref)

        dm_v = dm_ref[...]
        for j, (b_ref, d_ref) in enumerate(((b1_ref, d1_ref), (b2_ref, d2_ref), (b3_ref, d3_ref))):
            sl = slice(j * D, (j + 1) * D)
            gate = _sigmoid(gl_ref[:, sl].astype(F32) + bg_ref[:, sl])
            d_ref[...] = (dm_v * gate).astype(BF16)
            dgl = dm_v * b_ref[...].astype(F32) * (gate * (1.0 - gate))
            dgl_ref[:, sl] = dgl.astype(BF16)
            dbg_ref[:, sl] += jnp.sum(dgl, axis=0, keepdims=True)

    row = pl.BlockSpec((tb, D), lambda i: (i, 0))
    wide = pl.BlockSpec((tb, 3 * D), lambda i: (i, 0))
    par = pl.BlockSpec((1, 3 * D), lambda i: (0, 0))
    return pl.pallas_call(
        body, name=name, grid=(t // tb,),
        in_specs=[wide, par, row, row, row, row], out_specs=[wide, row, row, row, par],
        out_shape=[jax.ShapeDtypeStruct((t, 3 * D), BF16)] + [jax.ShapeDtypeStruct((t, D), BF16)] * 3
                  + [jax.ShapeDtypeStruct((1, 3 * D), F32)],
        compiler_params=_cp("arbitrary"),
    )(pgl, bg, b1, b2, b3, dm)


def _xa_probs(q, kv_ref, hd):
    sl = slice(hd * XA_HEAD_DIM, (hd + 1) * XA_HEAD_DIM)
    k = kv_ref[:, sl]
    v = kv_ref[:, D + hd * XA_HEAD_DIM:D + (hd + 1) * XA_HEAD_DIM]
    s = _dot(q[:, sl], k, _NT) * (XA_HEAD_DIM ** -0.5)
    e = jnp.exp(s - jnp.max(s, axis=1, keepdims=True))
    return sl, k, v, e / jnp.sum(e, axis=1, keepdims=True)


def _xa_fwd(pq, kv, name, tb=512):
    t = pq.shape[0]
    tb = min(tb, t)

    def body(q_ref, kv_ref, o_ref):
        q = q_ref[...]
        for hd in range(XA_HEADS):
            sl, _, v, p = _xa_probs(q, kv_ref, hd)
            o_ref[:, sl] = _dot(p, v).astype(BF16)

    row = pl.BlockSpec((tb, D), lambda i: (i, 0))
    return pl.pallas_call(
        body, name=name, grid=(t // tb,),
        in_specs=[row, pl.BlockSpec(kv.shape, lambda i: (0, 0))], out_specs=row,
        out_shape=jax.ShapeDtypeStruct((t, D), BF16), compiler_params=_cp("parallel"),
    )(pq, kv)


def _xa_bwd(pq, kv, dy, name, tb=512):
    t = pq.shape[0]
    tb = min(tb, t)

    def body(q_ref, kv_ref, dy_ref, dq_ref, dkv_ref):
        @pl.when(pl.program_id(0) == 0)
        def _():
            dkv_ref[...] = jnp.zeros_like(dkv_ref)

        q = q_ref[...]
        for hd in range(XA_HEADS):
            sl, k, v, p = _xa_probs(q, kv_ref, hd)
            dyh = dy_ref[:, sl]
            vsl = slice(D + hd * XA_HEAD_DIM, D + (hd + 1) * XA_HEAD_DIM)
            dkv_ref[:, vsl] += _dot(p, dyh, _TN)
            dp = _dot(dyh, v, _NT)
            ds = p * (dp - jnp.sum(dp * p, axis=1, keepdims=True)) * (XA_HEAD_DIM ** -0.5)
            dq_ref[:, sl] = _dot(ds, k).astype(BF16)
            dkv_ref[:, sl] += _dot(ds, q[:, sl], _TN)

    row = pl.BlockSpec((tb, D), lambda i: (i, 0))
    kvs = pl.BlockSpec(kv.shape, lambda i: (0, 0))
    return pl.pallas_call(
        body, name=name, grid=(t // tb,), in_specs=[row, kvs, row], out_specs=[row, kvs],
        out_shape=[jax.ShapeDtypeStruct((t, D), BF16), jax.ShapeDtypeStruct(kv.shape, F32)],
        compiler_params=_cp("arbitrary"),
    )(pq, kv, dy)


SUBLANES = 8


def _scan(a, u, reverse):
    n, c = a.shape
    groups = n // SUBLANES
    a = a.reshape(groups, SUBLANES, c)
    u = u.reshape(groups, SUBLANES, c)
    sub = _iota((1, SUBLANES, 1), 1)
    d = 1
    while d < SUBLANES:
        keep = (sub < SUBLANES - d) if reverse else (sub >= d)
        shift = SUBLANES - d if reverse else d
        u = a * jnp.where(keep, pltpu.roll(u, shift, 1), 0.0) + u
        a = a * jnp.where(keep, pltpu.roll(a, shift, 1), 1.0)
        d *= 2
    edge = 0 if reverse else SUBLANES - 1
    out, carry = [None] * groups, None
    for j in (reversed(range(groups)) if reverse else range(groups)):
        out[j] = u[j] if carry is None else u[j] + a[j] * carry
        carry = out[j][edge:edge + 1]
    return jnp.concatenate(out, axis=0)


def _lru_gates(xc, wa_ref, ba, wi_ref, bi, lam):
    za = jnp.concatenate([_dot(xc[:, n * 128:(n + 1) * 128], wa_ref[n]) for n in range(LRU_BLOCKS)], axis=1) + ba
    zi = jnp.concatenate([_dot(xc[:, n * 128:(n + 1) * 128], wi_ref[n]) for n in range(LRU_BLOCKS)], axis=1) + bi
    r = 1.0 / (1.0 + jnp.exp(-za))
    ig = _sigmoid(zi)
    sp = _softplus(-lam)
    log_a = (-LRU_C) * r * sp
    a = jnp.exp(log_a)
    m = jnp.sqrt(-jnp.tanh(log_a) * (1.0 + a * a))
    u = m * (ig * xc)
    return a, u, r, ig, m, sp


def _lru_fwd(p, cw, cb, wa, ba, wi, bi, lam, name, tb=256):
    t = p.shape[0]
    tb = min(tb, t)
    nb = t // tb
    r8 = tb // 8

    def body(x_ref, xp_ref, g_ref, cw_ref, cb_ref, wa_ref, ba_ref, wi_ref, bi_ref, lam_ref, y_ref, h_ref, xc_ref,
             hc_ref):
        i = pl.program_id(0)

        @pl.when(i == 0)
        def _():
            hc_ref[...] = jnp.zeros_like(hc_ref)

        halo = jnp.where(i == 0, 0.0, xp_ref[...])
        taps = _conv_taps(jnp.concatenate([halo, x_ref[...]], axis=0), tb)
        xc = _conv_fwd(taps, cw_ref[...], cb_ref[...])
        xc_ref[...] = xc
        a, u, _, _, _, _ = _lru_gates(xc, wa_ref, ba_ref[...], wi_ref, bi_ref[...], lam_ref[...])
        row = _iota((tb, 1), 0)
        u = u + jnp.where(row == 0, a * hc_ref[...], 0.0)
        h = _scan(a, u, reverse=False)
        h_ref[...] = h
        hc_ref[...] = h[tb - 1:tb, :]
        gl, _ = _gelu_and_grad(g_ref[...])
        y_ref[...] = (gl * h).astype(BF16)

    par = pl.BlockSpec((1, D), lambda i: (0, 0))
    wsp = pl.BlockSpec((LRU_BLOCKS, LRU_BLOCK, LRU_BLOCK), lambda i: (0, 0, 0))
    row = pl.BlockSpec((tb, D), lambda i: (i, 0))
    return pl.pallas_call(
        body, name=name, grid=(nb,),
        in_specs=[row, pl.BlockSpec((8, D), lambda i: (jnp.maximum(i * r8 - 1, 0), 0)),
                  pl.BlockSpec((tb, D), lambda i: (i, 1)),
                  pl.BlockSpec((4, D), lambda i: (0, 0)), par, wsp, par, wsp, par, par],
        out_specs=[row, row, row],
        out_shape=[jax.ShapeDtypeStruct((t, D), BF16), jax.ShapeDtypeStruct((t, D), F32),
                   jax.ShapeDtypeStruct((t, D), F32)],
        scratch_shapes=[pltpu.VMEM((1, D), F32)],
        compiler_params=_cp("arbitrary"),
    )(p, p, p, cw, cb, wa, ba, wi, bi, lam)


def _lru_bwd(p, xc, h, dy, cw, wa, ba, wi, bi, lam, name, tb=256):
    t = p.shape[0]
    tb = min(tb, t)
    nb = t // tb
    r8 = tb // 8

    def body(x_ref, g_ref, xc_ref, h_ref, hp_ref, dy_ref, cw_ref, wa_ref, ba_ref, wi_ref, bi_ref, lam_ref,
             dp_ref, dcw_ref, dcb_ref, dwa_ref, dba_ref, dwi_ref, dbi_ref, dlam_ref, carry_ref, dnext_ref):
        i = pl.program_id(0)
        blk = nb - 1 - i

        @pl.when(i == 0)
        def _():
            for r in (dcw_ref, dcb_ref, dwa_ref, dba_ref, dwi_ref, dbi_ref, dlam_ref, carry_ref, dnext_ref):
                r[...] = jnp.zeros_like(r)

        xc = xc_ref[...]
        lam = lam_ref[...]
        a, _, r, ig, m, sp = _lru_gates(xc, wa_ref, ba_ref[...], wi_ref, bi_ref[...], lam)
        gl, dgl = _gelu_and_grad(g_ref[...])
        h = h_ref[...]
        dy = dy_ref[...]
        dp_ref[:, D:] = (dy * h * dgl).astype(BF16)
        row = _iota((tb, 1), 0)
        dh = dy * gl + jnp.where(row == tb - 1, carry_ref[...], 0.0)
        b = jnp.where(row < tb - 1, pltpu.roll(a, tb - 1, 0), 0.0)
        gs = _scan(b, dh, reverse=True)
        carry_ref[...] = a[0:1] * gs[0:1]
        h_last = jnp.where(blk == 0, 0.0, hp_ref[7:8, :])
        hprev = jnp.where(row == 0, h_last, pltpu.roll(h, 1, 0))
        da = gs * hprev
        dm = gs * ig * xc
        di = gs * m * xc
        dxc = gs * m * ig
        dlog = a * (da - a * (dm / m))
        dr = dlog * ((-LRU_C) * sp)
        dsp = jnp.sum(dlog * ((-LRU_C) * r), axis=0, keepdims=True)
        dlam_ref[...] += dsp * (-_sigmoid(-lam))
        dza = dr * r * (1.0 - r)
        dzi = di * ig * (1.0 - ig)
        dba_ref[...] += jnp.sum(dza, axis=0, keepdims=True)
        dbi_ref[...] += jnp.sum(dzi, axis=0, keepdims=True)
        parts = []
        for n in range(LRU_BLOCKS):
            sl = slice(n * 128, (n + 1) * 128)
            dwa_ref[n] += _dot(xc[:, sl], dza[:, sl], _TN)
            dwi_ref[n] += _dot(xc[:, sl], dzi[:, sl], _TN)
            parts.append(_dot(dza[:, sl], wa_ref[n], _NT) + _dot(dzi[:, sl], wi_ref[n], _NT))
        dxc = dxc + jnp.concatenate(parts, axis=1)
        dx, dcw, dcb = _conv_bwd(dxc, dnext_ref[...], x_ref[...], cw_ref[...], tb)
        dp_ref[:, :D] = dx.astype(BF16)
        dcw_ref[...] += dcw
        dcb_ref[...] += dcb
        dnext_ref[...] = dxc[0:8]

    par = pl.BlockSpec((1, D), lambda i: (0, 0))
    wsp = pl.BlockSpec((LRU_BLOCKS, LRU_BLOCK, LRU_BLOCK), lambda i: (0, 0, 0))
    cws = pl.BlockSpec((4, D), lambda i: (0, 0))
    rev = lambda i: nb - 1 - i
    blk0 = pl.BlockSpec((tb, D), lambda i: (rev(i), 0))
    w_shape = jax.ShapeDtypeStruct((LRU_BLOCKS, LRU_BLOCK, LRU_BLOCK), F32)
    v_shape = jax.ShapeDtypeStruct((1, D), F32)
    return pl.pallas_call(
        body, name=name, grid=(nb,),
        in_specs=[blk0, pl.BlockSpec((tb, D), lambda i: (rev(i), 1)), blk0, blk0,
                  pl.BlockSpec((8, D), lambda i: (jnp.maximum(rev(i) * r8 - 1, 0), 0)), blk0,
                  cws, wsp, par, wsp, par, par],
        out_specs=[pl.BlockSpec((tb, 2 * D), lambda i: (rev(i), 0)), cws, par, wsp, par, wsp, par, par],
        out_shape=[jax.ShapeDtypeStruct((t, 2 * D), BF16), jax.ShapeDtypeStruct((4, D), F32), v_shape,
                   w_shape, v_shape, w_shape, v_shape, v_shape],
        scratch_shapes=[pltpu.VMEM((1, D), F32), pltpu.VMEM((8, D), F32)],
        compiler_params=_cp("arbitrary"),
    )(p, p, xc, h, h, dy, cw, wa, ba, wi, bi, lam)


def _ssd_consts():
    m0 = _iota((1, 128), 1) < 64
    e = (jnp.right_shift(_iota((SSD_HEADS, D_SSD), 1), 6) == _iota((SSD_HEADS, D_SSD), 0)).astype(BF16)
    tril = (_iota((CHUNK, CHUNK), 0) >= _iota((CHUNK, CHUNK), 1)).astype(F32)
    eye = (_iota((SSD_HEADS, SSD_HEADS), 0) == _iota((SSD_HEADS, SSD_HEADS), 1)).astype(F32)
    r2 = _iota((CHUNK, 128), 0)
    c2 = jnp.bitwise_and(_iota((CHUNK, 128), 1), 63)
    return dict(m0=m0, e=e, tril=tril, eye=eye, causal2=r2 >= c2, fold=(c2 == r2).astype(BF16))


SSD_STEP = 4


def _ssd_pre(c, dt_raw, dtb_ref, alog_ref, dvec_ref, k):
    sg = _sigmoid(c)
    xbc = c * sg
    dtp = dt_raw + dtb_ref[...]
    dt = _softplus(dtp)
    a = -jnp.exp(alog_ref[...])
    cs = _dot_hi(k["tril"], dt * a)
    cs_last = cs[CHUNK - 1:CHUNK]
    dend = jnp.exp(cs_last - cs)
    cdec = jnp.exp(cs_last)
    big = _dot01(jnp.concatenate([dt, jnp.exp(cs), dend], axis=0), k["e"])
    small = _dot01(jnp.concatenate([jnp.broadcast_to(cdec, (8, SSD_HEADS)),
                                    jnp.broadcast_to(dvec_ref[...], (8, SSD_HEADS))], axis=0), k["e"])
    cst2 = _dot_hi(k["eye"], jnp.concatenate([cs, cs], axis=0), _NT)
    return dict(c=c, sg=sg, xs=xbc[:, :D_SSD], bm=xbc[:, D_SSD:D_SSD + 512],
                cm=xbc[:, D_SSD + 512:], dtp=dtp, dt=dt, a=a, cs=cs, dend=dend, cdec=cdec,
                dtx=big[0:CHUNK], ecx=big[CHUNK:2 * CHUNK], dex=big[2 * CHUNK:3 * CHUNK],
                cdx=small[0:1], ddx=small[8:9], cst2=cst2)


def _pair_decay(p, cs, cst2, k):
    h0, h1 = 2 * p, 2 * p + 1
    colp = jnp.where(k["m0"], cs[:, h0:h0 + 1], cs[:, h1:h1 + 1])
    rowp = jnp.where(k["m0"], cst2[h0:h0 + 1, :], cst2[h1:h1 + 1, :])
    return jnp.where(k["causal2"], jnp.exp(colp - rowp), 0.0)


def _pair_stack(xp, k):
    return jnp.concatenate([jnp.where(k["m0"], xp, 0.0), jnp.where(k["m0"], 0.0, xp)], axis=0)


def _group_norm(yz, nw, with_stats=False):
    outs, stats = [], []
    for g in range(SSD_GROUPS):
        yzg = yz[:, g * GROUP_W:(g + 1) * GROUP_W]
        r = lax.rsqrt(jnp.mean(yzg * yzg, axis=1, keepdims=True) + EPS)
        outs.append(yzg * r)
        stats.append(r)
    y = jnp.concatenate(outs, axis=1) * nw
    return (y, stats) if with_stats else y


def _ssd_fwd(p, cw, cb, dtb, alog, dvec, nw, name):
    t = p.shape[0]
    step = SSD_STEP if t % (SSD_STEP * CHUNK) == 0 else 1
    rows_blk, nb, nc = step * CHUNK, t // (step * CHUNK), t // CHUNK

    def body(p_blk, cw_ref, cb_ref, dtb_ref, alog_ref, dvec_ref, nw_ref, y_blk, yraw_blk, hs_blk, c_blk,
             h_scr, tail_scr):
        @pl.when(pl.program_id(0) == 0)
        def _():
            h_scr[...] = jnp.zeros_like(h_scr)
            tail_scr[...] = jnp.zeros_like(tail_scr)

        k = _ssd_consts()

        def one_chunk(j, carry):
            rows = pl.ds(pl.multiple_of(j * CHUNK, CHUNK), CHUNK)
            chunk(p_blk.at[rows], y_blk.at[rows], yraw_blk.at[rows], hs_blk.at[j], c_blk.at[rows], k,
                  cw_ref, cb_ref, dtb_ref, alog_ref, dvec_ref, nw_ref, h_scr, tail_scr)
            return carry

        lax.fori_loop(0, step, one_chunk, 0)

    def chunk(p_ref, y_ref, yraw_ref, hs_ref, c_ref, k, cw_ref, cb_ref, dtb_ref, alog_ref, dvec_ref, nw_ref,
              h_scr, tail_scr):
        x_in = p_ref[:, S_XBC:S_DT]
        taps = _conv_taps(jnp.concatenate([tail_scr[...], x_in], axis=0), CHUNK)
        tail_scr[...] = x_in[CHUNK - 8:]
        c = _conv_fwd(taps, cw_ref[...], cb_ref[...])
        c_ref[...] = c
        s = _ssd_pre(c, p_ref[:, S_DT:S_DT + DT_REAL], dtb_ref, alog_ref, dvec_ref, k)
        xs, bm, cm = s["xs"], s["bm"], s["cm"]
        xdt = xs * s["dtx"]
        hprev = h_scr[...]
        hs_ref[...] = hprev
        ys, hn = [], []
        for g in range(SSD_GROUPS):
            gs = slice(g * GROUP_W, (g + 1) * GROUP_W)
            bg = bm[:, g * 128:(g + 1) * 128]
            cg = cm[:, g * 128:(g + 1) * 128]
            cbdup = _dot(cg, jnp.concatenate([bg, bg], axis=0), _NT)
            hp_g = hprev[:, gs]
            yd = []
            for q in range(4):
                pr = g * 4 + q
                mp = cbdup * _pair_decay(pr, s["cs"], s["cst2"], k)
                yd.append(_dot(mp, _pair_stack(xdt[:, pr * 128:(pr + 1) * 128], k)))
            ys.append(jnp.concatenate(yd, axis=1) + _dot(cg, hp_g) * s["ecx"][:, gs])
            hn.append(hp_g * s["cdx"][:, gs] + _dot(bg, xdt[:, gs] * s["dex"][:, gs], _TN))
        h_scr[...] = jnp.concatenate(hn, axis=1)
        yraw = jnp.concatenate(ys, axis=1) + s["ddx"] * xs
        yraw_ref[...] = yraw
        z = p_ref[:, S_Z:S_Z + D_SSD]
        y_ref[...] = _group_norm(yraw * (z * _sigmoid(z)), nw_ref[...]).astype(BF16)

    hv = pl.BlockSpec((1, DT_REAL), lambda i: (0, 0))
    return pl.pallas_call(
        body, name=name, grid=(nb,),
        in_specs=[pl.BlockSpec((rows_blk, W_SSD), lambda i: (i, 0)),
                  pl.BlockSpec((4, D_XBC), lambda i: (0, 0)), pl.BlockSpec((1, D_XBC), lambda i: (0, 0)),
                  hv, hv, hv, pl.BlockSpec((1, D_SSD), lambda i: (0, 0))],
        out_specs=[pl.BlockSpec((rows_blk, D_SSD), lambda i: (i, 0)), pl.BlockSpec((rows_blk, D_SSD), lambda i: (i, 0)),
                   pl.BlockSpec((step, SSD_STATE, D_SSD), lambda i: (i, 0, 0)),
                   pl.BlockSpec((rows_blk, D_XBC), lambda i: (i, 0))],
        out_shape=[jax.ShapeDtypeStruct((t, D_SSD), BF16), jax.ShapeDtypeStruct((t, D_SSD), F32),
                   jax.ShapeDtypeStruct((nc, SSD_STATE, D_SSD), F32), jax.ShapeDtypeStruct((t, D_XBC), F32)],
        scratch_shapes=[pltpu.VMEM((SSD_STATE, D_SSD), F32), pltpu.VMEM((8, D_XBC), F32)],
        compiler_params=_cp("arbitrary"),
    )(p, cw, cb, dtb, alog, dvec, nw)


def _ssd_bwd(p, c, yraw, hs, dy, cw, dtb, alog, dvec, nw, name):
    t = p.shape[0]
    step = 1
    rows_blk, nb = step * CHUNK, t // (step * CHUNK)

    def body(p_blk, c_blk, yraw_blk, hs_blk, dy_blk, cw_ref, dtb_ref, alog_ref, dvec_ref, nw_ref,
             dp_blk, dcw_ref, dcb_ref, ddtb_ref, dalog_ref, dd_ref, dnw_ref, dh_scr, dnext_scr):
        @pl.when(pl.program_id(0) == 0)
        def _():
            for r in (dcw_ref, dcb_ref, ddtb_ref, dalog_ref, dd_ref, dnw_ref, dh_scr, dnext_scr):
                r[...] = jnp.zeros_like(r)

        k = _ssd_consts()

        def one_chunk(jj, carry):
            j = step - 1 - jj
            rows = pl.ds(pl.multiple_of(j * CHUNK, CHUNK), CHUNK)
            chunk(p_blk.at[rows], c_blk.at[rows], yraw_blk.at[rows], hs_blk.at[j], dy_blk.at[rows], dp_blk.at[rows], k,
                  cw_ref, dtb_ref, alog_ref, dvec_ref, nw_ref, dcw_ref, dcb_ref, ddtb_ref, dalog_ref, dd_ref, dnw_ref,
                  dh_scr, dnext_scr)
            return carry

        lax.fori_loop(0, step, one_chunk, 0)

    def chunk(p_ref, c_ref, yraw_ref, hs_ref, dy_ref, dp_ref, k, cw_ref, dtb_ref, alog_ref, dvec_ref, nw_ref,
              dcw_ref, dcb_ref, ddtb_ref, dalog_ref, dd_ref, dnw_ref, dh_scr, dnext_scr):
        s = _ssd_pre(c_ref[...], p_ref[:, S_DT:S_DT + DT_REAL], dtb_ref, alog_ref, dvec_ref, k)
        xs, bm, cm, cs, dt, a = s["xs"], s["bm"], s["cm"], s["cs"], s["dt"], s["a"]
        m0 = k["m0"]
        xdt = xs * s["dtx"]
        hprev = hs_ref[...]
        dh = dh_scr[...]

        nw_v = nw_ref[...]
        yraw = yraw_ref[...]
        z = p_ref[:, S_Z:S_Z + D_SSD]
        sz = _sigmoid(z)
        siluz = z * sz
        yz = yraw * siluz
        dyo = dy_ref[...]
        dyn = dyo * nw_v
        dyz_parts, dnw_parts = [], []
        for g in range(SSD_GROUPS):
            gs = slice(g * GROUP_W, (g + 1) * GROUP_W)
            yzg = yz[:, gs]
            r = lax.rsqrt(jnp.mean(yzg * yzg, axis=1, keepdims=True) + EPS)
            dnw_parts.append(jnp.sum(dyo[:, gs] * yzg * r, axis=0, keepdims=True))
            dyz_parts.append(r * dyn[:, gs] - yzg * (r * r * r) * jnp.mean(dyn[:, gs] * yzg, axis=1, keepdims=True))
        dnw_ref[...] += jnp.concatenate(dnw_parts, axis=1)
        dyz = jnp.concatenate(dyz_parts, axis=1)
        d_y = dyz * siluz
        dp_ref[:, S_Z:S_Z + D_SSD] = (dyz * yraw * (sz * (1.0 + z * (1.0 - sz)))).astype(BF16)
        dd_row = jnp.sum(d_y * xs, axis=0, keepdims=True)
        dxs = d_y * s["ddx"]

        lane_h = _iota((1, SSD_HEADS), 1)
        sub_h = _iota((SSD_HEADS, 1), 0)
        dcs = jnp.zeros((CHUNK, SSD_HEADS), F32)
        dcst2 = jnp.zeros((SSD_HEADS, 128), F32)
        dxdt_parts, db_parts, dc_parts, dhp_parts, yoff_parts, dend_parts, dcd_parts = [], [], [], [], [], [], []
        for g in range(SSD_GROUPS):
            gs = slice(g * GROUP_W, (g + 1) * GROUP_W)
            bg = bm[:, g * 128:(g + 1) * 128]
            cg = cm[:, g * 128:(g + 1) * 128]
            bdup = jnp.concatenate([bg, bg], axis=0)
            cbdup = _dot(cg, bdup, _NT)
            dcb2 = jnp.zeros((CHUNK, 128), F32)
            dxp_parts = []
            for q in range(4):
                pr = g * 4 + q
                h0, h1 = 2 * pr, 2 * pr + 1
                lp = _pair_decay(pr, cs, s["cst2"], k)
                mp = cbdup * lp
                xst = _pair_stack(xdt[:, pr * 128:(pr + 1) * 128], k)
                dyp = d_y[:, pr * 128:(pr + 1) * 128]
                dmp = _dot(dyp, xst, _NT)
                dxst = _dot(mp, dyp, _TN)
                dxp_parts.append(jnp.where(m0, dxst[:CHUNK], dxst[CHUNK:]))
                dcb2 = dcb2 + dmp * lp
                dlm = dmp * mp
                rs0 = jnp.sum(jnp.where(m0, dlm, 0.0), axis=1, keepdims=True)
                rs1 = jnp.sum(jnp.where(m0, 0.0, dlm), axis=1, keepdims=True)
                dcs = dcs + jnp.where(lane_h == h0, rs0, 0.0) + jnp.where(lane_h == h1, rs1, 0.0)
                colsum = jnp.sum(dlm, axis=0, keepdims=True)
                sel = ((sub_h == h0) & m0) | ((sub_h == h1) & jnp.logical_not(m0))
                dcst2 = dcst2 - jnp.where(sel, colsum, 0.0)
            dcg = _dot(dcb2, bdup)
            dbdup = _dot(dcb2, cg, _TN)
            dbg = dbdup[:CHUNK] + dbdup[CHUNK:]
            hp_g = hprev[:, gs]
            zoff = _dot(cg, hp_g)
            dzo = d_y[:, gs] * s["ecx"][:, gs]
            dcg = dcg + _dot(dzo, hp_g, _NT)
            dh_g = dh[:, gs]
            dhp_parts.append(_dot(cg, dzo, _TN) + dh_g * s["cdx"][:, gs])
            dcd_parts.append(jnp.sum(dh_g * hp_g, axis=0, keepdims=True))
            wg = xdt[:, gs] * s["dex"][:, gs]
            dbg = dbg + _dot(wg, dh_g, _NT)
            dwg = _dot(bg, dh_g)
            dxdt_parts.append(jnp.concatenate(dxp_parts, axis=1) + dwg * s["dex"][:, gs])
            dend_g = dwg * wg
            dend_parts.append(jnp.sum(dend_g, axis=0, keepdims=True))
            yoff_parts.append(dzo * zoff - dend_g)
            db_parts.append(dbg)
            dc_parts.append(dcg)
        dh_scr[...] = jnp.concatenate(dhp_parts, axis=1)
        dxdt = jnp.concatenate(dxdt_parts, axis=1)
        sums = _dot01(jnp.concatenate([jnp.concatenate(yoff_parts, axis=1), dxdt * xs], axis=0), k["e"], _NT)
        rows8 = jnp.concatenate([jnp.broadcast_to(jnp.concatenate(r, axis=1), (8, D_SSD))
                                 for r in (dcd_parts, [dd_row], dend_parts)], axis=0)
        small = _dot01(rows8, k["e"], _NT)
        dd_ref[...] += small[8:9]
        dcs_last = small[0:1] * s["cdec"] + small[16:17]
        hi, lo = _split(dcst2)
        dcs = (dcs + sums[0:CHUNK]
               + lax.dot_general(k["fold"], hi, _NT, preferred_element_type=F32)
               + lax.dot_general(k["fold"], lo, _NT, preferred_element_type=F32)
               + jnp.where(_iota((CHUNK, 1), 0) == CHUNK - 1, dcs_last, 0.0))
        dda = _dot_hi(k["tril"], dcs, _TN)
        ddt = dda * a + sums[CHUNK:2 * CHUNK]
        dalog_ref[...] += jnp.sum(dda * dt, axis=0, keepdims=True) * a
        dxs = dxs + dxdt * s["dtx"]
        draw = ddt * _sigmoid(s["dtp"])
        ddtb_ref[...] += jnp.sum(draw, axis=0, keepdims=True)
        dp_ref[:, S_DT:] = jnp.zeros((CHUNK, W_SSD - S_DT), BF16)
        dp_ref[:, S_DT:S_DT + DT_REAL] = draw.astype(BF16)
        dxbc = jnp.concatenate([dxs] + db_parts + dc_parts, axis=1)
        sg, c = s["sg"], s["c"]
        dc = dxbc * (sg * (1.0 + c * (1.0 - sg)))
        dx, dcw, dcb = _conv_bwd(dc, dnext_scr[...], p_ref[:, S_XBC:S_DT], cw_ref[...], CHUNK)
        dp_ref[:, S_XBC:S_DT] = dx.astype(BF16)
        dcw_ref[...] += dcw
        dcb_ref[...] += dcb
        dnext_scr[...] = dc[0:8]

    rev = lambda i: nb - 1 - i
    hv = pl.BlockSpec((1, DT_REAL), lambda i: (0, 0))
    cws = pl.BlockSpec((4, D_XBC), lambda i: (0, 0))
    cbs = pl.BlockSpec((1, D_XBC), lambda i: (0, 0))
    nws = pl.BlockSpec((1, D_SSD), lambda i: (0, 0))
    wide = pl.BlockSpec((rows_blk, D_SSD), lambda i: (rev(i), 0))
    hshape = jax.ShapeDtypeStruct((1, DT_REAL), F32)
    return pl.pallas_call(
        body, name=name, grid=(nb,),
        in_specs=[pl.BlockSpec((rows_blk, W_SSD), lambda i: (rev(i), 0)),
                  pl.BlockSpec((rows_blk, D_XBC), lambda i: (rev(i), 0)),
                  wide, pl.BlockSpec((step, SSD_STATE, D_SSD), lambda i: (rev(i), 0, 0)), wide,
                  cws, hv, hv, hv, nws],
        out_specs=[pl.BlockSpec((rows_blk, W_SSD), lambda i: (rev(i), 0)), cws, cbs, hv, hv, hv, nws],
        out_shape=[jax.ShapeDtypeStruct((t, W_SSD), BF16), jax.ShapeDtypeStruct((4, D_XBC), F32),
                   jax.ShapeDtypeStruct((1, D_XBC), F32), hshape, hshape, hshape,
                   jax.ShapeDtypeStruct((1, D_SSD), F32)],
        scratch_shapes=[pltpu.VMEM((SSD_STATE, D_SSD), F32), pltpu.VMEM((8, D_XBC), F32)],
        compiler_params=_cp("arbitrary"),
    )(p, c, yraw, hs, dy, cw, dtb, alog, dvec, nw)


def _loss_head(y, target, name, tb=512):
    t = y.shape[0]
    tb = min(tb, t)

    def body(y_ref, t_ref, dy_ref, l_ref):
        @pl.when(pl.program_id(0) == 0)
        def _():
            l_ref[...] = jnp.zeros_like(l_ref)

        e = y_ref[...] - t_ref[...]
        dy_ref[...] = e * (1.0 / D)
        l_ref[...] += jnp.sum(jnp.sum(e * e, axis=1, keepdims=True), axis=0, keepdims=True) * (0.5 / D)

    row = pl.BlockSpec((tb, D), lambda i: (i, 0))
    return pl.pallas_call(
        body, name=name, grid=(t // tb,), in_specs=[row, row],
        out_specs=[row, pl.BlockSpec((8, 128), lambda i: (0, 0))],
        out_shape=[jax.ShapeDtypeStruct((t, D), F32), jax.ShapeDtypeStruct((8, 128), F32)],
        compiler_params=_cp("arbitrary"),
    )(y, target)


def _adamw(slots, w, m, v, name, tb):
    nl = len(slots)
    ns, r, c = slots[0].shape
    assert r % tb == 0 and w.shape == (nl, r, c), (r, tb, w.shape)

    def body(*refs):
        s_refs = refs[:nl]
        w_ref, m_ref, v_ref, g_ref, d_ref, m2_ref, v2_ref = refs[nl:]

        def total(ref):
            acc = ref[0].astype(F32)
            for j in range(1, ns):
                acc = acc + ref[j].astype(F32)
            return acc

        g = total(s_refs[0])
        for layer in range(1, nl):
            g = jnp.where(pl.program_id(0) == layer, total(s_refs[layer]), g)
        m2 = ADAM_B1 * m_ref[...] + (1.0 - ADAM_B1) * g
        v2 = ADAM_B2 * v_ref[...] + (1.0 - ADAM_B2) * (g * g)
        m_hat = m2 / (1.0 - ADAM_B1 ** ADAM_STEP)
        v_hat = v2 / (1.0 - ADAM_B2 ** ADAM_STEP)
        g_ref[...] = g
        d_ref[...] = -ADAM_LR * (m_hat / (jnp.sqrt(v_hat) + ADAM_EPS) + ADAM_WD * w_ref[...])
        m2_ref[...] = m2
        v2_ref[...] = v2

    def slot_spec(layer):
        return pl.BlockSpec((ns, tb, c), lambda l, i: (0, jnp.where(l == layer, i, 0), 0))

    row = pl.BlockSpec((None, tb, c), lambda l, i: (l, i, 0))
    shp = jax.ShapeDtypeStruct((nl, r, c), F32)
    return pl.pallas_call(
        body, name=name, grid=(nl, r // tb),
        in_specs=[slot_spec(layer) for layer in range(nl)] + [row, row, row],
        out_specs=[row, row, row, row], out_shape=[shp, shp, shp, shp], compiler_params=_cp("arbitrary", "arbitrary"),
    )(*slots, w, m, v)


def _pair_sum(own, got, name, out_dtype, tb):
    nj, _, r, c = own.shape
    mc = lax.axis_index("c")

    def body(mc_ref, a_ref, b_ref, o_ref):
        del mc_ref
        o_ref[...] = (a_ref[...] + b_ref[...]).astype(out_dtype)

    return pl.pallas_call(
        body, name=name,
        grid_spec=pltpu.PrefetchScalarGridSpec(
            num_scalar_prefetch=1, grid=(nj, r // tb),
            in_specs=[pl.BlockSpec((None, None, tb, c), lambda j, i, mc_ref: (j, mc_ref[0], i, 0)),
                      pl.BlockSpec((None, tb, c), lambda j, i, mc_ref: (j, i, 0))],
            out_specs=pl.BlockSpec((None, tb, c), lambda j, i, mc_ref: (j, i, 0))),
        out_shape=jax.ShapeDtypeStruct((nj, r, c), out_dtype), compiler_params=_cp("parallel", "parallel"),
    )(jnp.reshape(mc, (1,)).astype(jnp.int32), own, got)


def _slot_sum(slots, name):
    ns, r, c = slots.shape

    def body(s_ref, o_ref):
        g = s_ref[0]
        for j in range(1, ns):
            g = g + s_ref[j]
        o_ref[...] = g

    return pl.pallas_call(body, name=name, out_shape=jax.ShapeDtypeStruct((r, c), F32))(slots)


def _position():
    return lax.axis_index("x"), lax.axis_index("y"), lax.axis_index("c")


def _comm(exchange, peers, xs, out_shapes, sems, name, collective_id):
    n = len(xs)
    if collective_id is None:
        def body(*refs):
            exchange(refs[:n], refs[n:n + len(out_shapes)], *refs[n + len(out_shapes):])

        return pl.pallas_call(body, name=name, in_specs=[ANY] * n, out_specs=[ANY] * len(out_shapes),
                              out_shape=out_shapes, scratch_shapes=sems)(*xs)
    def launch(*refs):
        barrier = pltpu.get_barrier_semaphore()
        to = peers(*_position())
        for peer in to:
            pl.semaphore_signal(barrier, inc=1, device_id=peer, device_id_type=MESH)
        pl.semaphore_wait(barrier, len(to))
        exchange(refs[:n], refs[n:n + len(out_shapes)], *refs[n + len(out_shapes):])

    return pl.kernel(launch, out_type=out_shapes, mesh=plsc.ScalarSubcoreMesh(axis_name="seq", num_cores=1), name=name,
                     scratch_types=sems, compiler_params=pltpu.CompilerParams(collective_id=collective_id))(*xs)


def _all_gather(xs, name, collective_id=None):
    n = len(xs)
    return _comm(_gather_body, lambda x, y, c: [(x, y, 1 - c), (1 - x, y, c), (x, 1 - y, c), (1 - x, 1 - y, c)], xs,
                 [jax.ShapeDtypeStruct((N_DEV,) + x.shape, x.dtype) for x in xs],
                 [pltpu.SemaphoreType.DMA((n, 7)), pltpu.SemaphoreType.DMA((n, 7)), pltpu.SemaphoreType.DMA((n,))],
                 name, collective_id)


def _gather_body(x_refs, out_refs, send_sems, recv_sems, local_sems):
    n = len(x_refs)
    mx, my, mc = _position()
    me, sibling = (mx, my, mc), (mx, my, 1 - mc)
    chips = [(1 - mx, my), (mx, 1 - my), (1 - mx, 1 - my)]

    def copy(a, k, block, to, own=False):
        dst = out_refs[a].at[4 * block[0] + 2 * block[1] + block[2]]
        return pltpu.make_async_remote_copy(
            src_ref=x_refs[a] if own else dst, dst_ref=dst,
            send_sem=send_sems.at[a, k], recv_sem=recv_sems.at[a, k], device_id=to, device_id_type=MESH)

    mine = [pltpu.make_async_copy(x_refs[a], out_refs[a].at[4 * mx + 2 * my + mc], local_sems.at[a]) for a in range(n)]
    first = [copy(a, 1 + j, me, (*chip, mc), own=True) for j, chip in enumerate(chips) for a in range(n)]
    first += [copy(a, 0, me, sibling, own=True) for a in range(n)]
    for cp in first + mine:
        cp.start()
    passed = []
    for j, chip in enumerate(chips):
        for a in range(n):
            copy(a, 1 + j, (*chip, mc), me).wait_recv()
            passed.append(copy(a, 4 + j, (*chip, mc), sibling))
            passed[-1].start()
    for a in range(n):
        copy(a, 0, sibling, me).wait_recv()
    for j, chip in enumerate(chips):
        for a in range(n):
            copy(a, 4 + j, (*chip, 1 - mc), me).wait_recv()
    for cp in first + passed:
        cp.wait_send()
    for cp in mine:
        cp.wait()


def _exchange_sibling(gs, name, collective_id=None):
    n = len(gs)

    def exchange(g_refs, r_refs, send_sems, recv_sems):
        mx, my, mc = _position()
        cps = [pltpu.make_async_remote_copy(src_ref=g_refs[a].at[:, 1 - mc], dst_ref=r_refs[a],
                                            send_sem=send_sems.at[a], recv_sem=recv_sems.at[a],
                                            device_id=(mx, my, 1 - mc), device_id_type=MESH) for a in range(n)]
        for cp in cps:
            cp.start()
        for cp in cps:
            cp.wait()

    return _comm(exchange, lambda x, y, c: [(x, y, 1 - c)], gs,
                 [jax.ShapeDtypeStruct(g.shape[:1] + g.shape[2:], g.dtype) for g in gs],
                 [pltpu.SemaphoreType.DMA((n,)), pltpu.SemaphoreType.DMA((n,))], name, collective_id)


def _exchange_chips(ss, name, collective_id=None):
    n = len(ss)

    def exchange(s_refs, r_refs, send_sems, recv_sems, local_sems):
        mx, my, mc = _position()
        my_chip = 2 * mx + my
        chips = [(1 - mx, my), (mx, 1 - my), (1 - mx, 1 - my)]

        def copy(a, k, to_slot):
            px, py = chips[k]
            return pltpu.make_async_remote_copy(
                src_ref=s_refs[a].at[2 * px + py], dst_ref=r_refs[a].at[to_slot], send_sem=send_sems.at[a, k],
                recv_sem=recv_sems.at[a, k], device_id=(px, py, mc), device_id_type=MESH)

        sends = [copy(a, k, my_chip) for k in range(3) for a in range(n)]
        local = [pltpu.make_async_copy(s_refs[a].at[my_chip], r_refs[a].at[my_chip], local_sems.at[a])
                 for a in range(n)]
        for cp in sends + local:
            cp.start()
        for k in range(3):
            px, py = chips[k]
            for a in range(n):
                copy(a, k, 2 * px + py).wait_recv()
        for cp in sends:
            cp.wait_send()
        for cp in local:
            cp.wait()

    return _comm(exchange, lambda x, y, c: [(1 - x, y, c), (x, 1 - y, c), (1 - x, 1 - y, c)], ss,
                 [jax.ShapeDtypeStruct(s.shape, s.dtype) for s in ss],
                 [pltpu.SemaphoreType.DMA((n, 3)), pltpu.SemaphoreType.DMA((n, 3)), pltpu.SemaphoreType.DMA((n,))],
                 name, collective_id)


def _cols_concat(g, name, tb=128):
    _, k_dim, n = g.shape

    def body(g_ref, o_ref):
        o_ref[...] = jnp.concatenate([g_ref[d] for d in range(N_DEV)], axis=1)

    return pl.pallas_call(
        body, name=name, grid=(k_dim // tb,),
        in_specs=[pl.BlockSpec((N_DEV, tb, n), lambda i: (0, i, 0))],
        out_specs=pl.BlockSpec((tb, N_DEV * n), lambda i: (i, 0)),
        out_shape=jax.ShapeDtypeStruct((k_dim, N_DEV * n), g.dtype), compiler_params=_cp("parallel"),
    )(g)


def _cols_split(parts, name, tb=128):
    k_dim = parts[0].shape[0]
    n = sum(p.shape[1] for p in parts) // N_DEV

    def body(*refs):
        full = jnp.concatenate([r[...] for r in refs[:-1]], axis=1)
        for d in range(N_DEV):
            refs[-1][d] = full[:, d * n:(d + 1) * n]

    return pl.pallas_call(
        body, name=name, grid=(k_dim // tb,),
        in_specs=[pl.BlockSpec((tb, p.shape[1]), lambda i: (i, 0)) for p in parts],
        out_specs=pl.BlockSpec((N_DEV, tb, n), lambda i: (0, i, 0)),
        out_shape=jax.ShapeDtypeStruct((N_DEV, k_dim, n), parts[0].dtype), compiler_params=_cp("parallel"),
    )(*parts)


_Q0, _GL0 = 7200, 8224
N_SHARD_IN = N_IN // N_DEV


def _w_in_regions(g, name, tb=128):
    def body(g_ref, ssd_ref, lru_ref, q_ref, gl_ref):
        full = jnp.concatenate([g_ref[d] for d in range(N_DEV)], axis=1)
        lru_ref[...] = full[:, 0:2 * D]
        ssd_ref[:, :S_DT] = full[:, 2 * D:2 * D + S_DT]
        ssd_ref[:, S_DT:] = jnp.zeros((tb, W_SSD - S_DT), g.dtype)
        ssd_ref[:, S_DT:S_DT + DT_REAL] = full[:, 2 * D + S_DT:_Q0]
        q_ref[...] = full[:, _Q0:_GL0]
        gl_ref[...] = full[:, _GL0:N_IN]

    widths = (W_SSD, 2 * D, D, 3 * D)
    return pl.pallas_call(
        body, name=name, grid=(D // tb,),
        in_specs=[pl.BlockSpec((N_DEV, tb, N_SHARD_IN), lambda i: (0, i, 0))],
        out_specs=[pl.BlockSpec((tb, wd), lambda i: (i, 0)) for wd in widths],
        out_shape=[jax.ShapeDtypeStruct((D, wd), g.dtype) for wd in widths], compiler_params=_cp("parallel"),
    )(g)


def _w_in_shards(dssd, dlru, dq, dgl, name, tb=128):
    def body(ssd_ref, lru_ref, q_ref, gl_ref, o_ref):
        full = jnp.concatenate([lru_ref[...], ssd_ref[:, :S_DT + DT_REAL], q_ref[...], gl_ref[...]], axis=1)
        for d in range(N_DEV):
            o_ref[d] = full[:, d * N_SHARD_IN:(d + 1) * N_SHARD_IN]

    return pl.pallas_call(
        body, name=name, grid=(D // tb,),
        in_specs=[pl.BlockSpec((tb, a.shape[1]), lambda i: (i, 0)) for a in (dssd, dlru, dq, dgl)],
        out_specs=pl.BlockSpec((N_DEV, tb, N_SHARD_IN), lambda i: (0, i, 0)),
        out_shape=jax.ShapeDtypeStruct((N_DEV, D, N_SHARD_IN), F32), compiler_params=_cp("parallel"),
    )(dssd, dlru, dq, dgl)


_BIG = (("w_in", "col", (1024, 1412)), ("mem_w_kv", "col", (1024, 256)), ("w_br_lru", "row", (128, 1024)),
        ("w_br_ssd", "row", (256, 1024)), ("w_br_xa", "row", (128, 1024)), ("w_out", "row", (128, 1024)),
        ("ffn_w_in", "row", (704, 1024)), ("ffn_w_down", "row", (352, 1024)))
_TRANSPOSED = ("ffn_w_in",)
_SMALL = (("b_gate", (3, 128)), ("lru_conv_w", (4, 128)), ("ssd_conv_w", (4, 384)))
_REP = (("lru_conv_b", (1024,)), ("lru_w_a", (8, 128, 128)), ("lru_b_a", (1024,)), ("lru_w_i", (8, 128, 128)),
        ("lru_b_i", (1024,)), ("lru_lambda", (1024,)), ("ssd_conv_b", (3072,)), ("ssd_dt_bias", (32,)),
        ("ssd_a_log", (32,)), ("ssd_d", (32,)), ("ssd_norm_w", (2048,)), ("ln1_g", (1024,)), ("ln1_b", (1024,)),
        ("ln2_g", (1024,)), ("ln2_b", (1024,)))
_ORDER = ("w_in", "b_gate", "lru_conv_w", "lru_conv_b", "lru_w_a", "lru_b_a", "lru_w_i", "lru_b_i", "lru_lambda",
          "ssd_conv_w", "ssd_conv_b", "ssd_dt_bias", "ssd_a_log", "ssd_d", "ssd_norm_w", "mem_w_kv", "w_br_lru",
          "w_br_ssd", "w_br_xa", "w_out", "ln1_g", "ln1_b", "ffn_w_in", "ffn_w_down", "ln2_g", "ln2_b")

LANES = 1024
N_SMALL = sum(DEPTH * s[0] * s[1] for _, s in _SMALL)
R_SMALL = 8
N_REP = sum(DEPTH * math.prod(s) for _, s in _REP)
R_REP = 68
R_SM = R_SMALL + R_REP + 4
R_TAIL = R_SMALL + N_DEV * R_REP
TB_TAIL = 184
assert N_SMALL <= R_SMALL * LANES and N_REP <= N_DEV * R_REP * LANES


def _rows(flat, rows):
    return jnp.pad(flat, (0, rows * LANES - flat.shape[0])).reshape(rows, LANES)


def _rowblk(a, cap):
    return max(b for b in range(16, cap + 1, 16) if a % b == 0)


def _pack_tail(d):
    small = jnp.concatenate([d[n].reshape(-1) for n, _ in _SMALL])
    rep = jnp.concatenate([d[n].reshape(-1) for n, _ in _REP])
    return jnp.concatenate([_rows(small, R_SMALL), _rows(rep, N_DEV * R_REP)], axis=0)


def _unpack_tail(a):
    out, o = {}, 0
    flat = a[:R_SMALL].reshape(-1)
    for n, s in _SMALL:
        k = DEPTH * math.prod(s)
        out[n] = flat[o:o + k].reshape((DEPTH,) + s)
        o += k
    flat, o = a[R_SMALL:].reshape(-1), 0
    for n, s in _REP:
        k = DEPTH * math.prod(s)
        out[n] = flat[o:o + k].reshape((DEPTH,) + s)
        o += k
    return out


def _by_dest(g):
    g = g.reshape(g.shape[:-1] + (N_DEV, g.shape[-1] // N_DEV))
    return jnp.moveaxis(g, -2, 0).reshape(N_DEV, -1)


def _from_stack(st):
    st = jnp.moveaxis(st, 0, -2)
    return st.reshape(st.shape[:-2] + (st.shape[-2] * st.shape[-1],))


def _layer_fwd(x, xb, mem, w, l):
    nm = lambda s: f"{s}_l{l}"
    wi = w["wi"]
    row = lambda v: v.reshape(1, -1)
    s = dict(x=x, xb=xb, wi=wi)
    s["p_ssd"] = _mm(xb, wi["ssd"], name=nm("proj_ssd"))
    s["p_lru"] = _mm(xb, wi["lru"], name=nm("proj_lru"))
    s["p_q"] = _mm(xb, wi["q"], out_dtype=BF16, name=nm("proj_q"))
    s["p_gl"] = _mm(xb, wi["gl"], out_dtype=BF16, name=nm("proj_gl"))
    s["lru_par"] = (w["lru_conv_w"], row(w["lru_conv_b"]), w["lru_w_a"], row(w["lru_b_a"]), w["lru_w_i"],
                    row(w["lru_b_i"]), row(w["lru_lambda"]))
    s["y_lru"], s["h"], s["xc"] = _lru_fwd(s["p_lru"], *s["lru_par"], name=nm("lru_fwd"))
    s["ssd_par"] = (w["ssd_conv_w"], row(w["ssd_conv_b"]), row(w["ssd_dt_bias"]), row(w["ssd_a_log"]),
                    row(w["ssd_d"]), row(w["ssd_norm_w"]))
    s["y_ssd"], s["yraw"], s["hs"], s["c_ssd"] = _ssd_fwd(s["p_ssd"], *s["ssd_par"], name=nm("ssd_fwd"))
    s["kv"] = _mm(mem, w["mem_w_kv"], name=nm("kv"))
    s["y_xa"] = _xa_fwd(s["p_q"], s["kv"], name=nm("xa_fwd"))
    s["b1"] = _mm(s["y_lru"], w["w_br_lru"], out_dtype=BF16, name=nm("br_lru"))
    s["b2"] = _mm(s["y_ssd"], w["w_br_ssd"], out_dtype=BF16, name=nm("br_ssd"))
    s["b3"] = _mm(s["y_xa"], w["w_br_xa"], out_dtype=BF16, name=nm("br_xa"))
    s["bg"] = row(w["b_gate"])
    s["merged"] = _merge_fwd(s["p_gl"], s["bg"], s["b1"], s["b2"], s["b3"], name=nm("merge_fwd"))
    s["mix"] = _mm(s["merged"], w["w_out"], name=nm("out_proj"))
    s["x1"], s["x1b"] = _ln_fwd(x, s["mix"], row(w["ln1_g"]), row(w["ln1_b"]), name=nm("ln1_fwd"))
    s["gate"], s["up"], s["act"] = _ffn_in_swiglu(s["x1b"], w["ffn_w_in"], name=nm("ffn_in"))
    s["f"] = _mm(s["act"], w["ffn_w_down"], name=nm("ffn_down"))
    s["x2"], s["x2b"] = _ln_fwd(s["x1"], s["f"], row(w["ln2_g"]), row(w["ln2_b"]), name=nm("ln2_fwd"))
    return s


def _layer_bwd(s, mem, w, dxo, l, hooks=None):
    nm = lambda t: f"{t}_l{l}"
    g = {}
    hook = lambda stage, t: hooks[stage](t, g) if hooks and stage in hooks else t
    row = lambda v: v.reshape(1, -1)
    slabs = lambda a: a.reshape(N_DEV, a.shape[0] // N_DEV, a.shape[1])
    du2, dg, db = _ln_bwd(s["x1"], s["f"], dxo, row(w["ln2_g"]), name=nm("ln2_bwd"))
    g["ln2_g"], g["ln2_b"] = dg[0], db[0]
    dgate, dup = _d_swiglu(du2, w["ffn_w_down"], s["gate"], s["up"], name=nm("d_swiglu"))
    g["ffn_w_down"] = slabs(_mm(s["act"], du2, ta=True, name=nm("dw_ffn_down")))
    dx1 = _mm(dgate, w["ffn_w_in"][:D_FF], add=du2, add_scale=ALPHA, name=nm("d_x1_gate"))
    dx1 = _mm(dup, w["ffn_w_in"][D_FF:], add=dx1, name=nm("d_x1_up"))
    g["ffn_w_in"] = slabs(jnp.concatenate([_mm(dgate, s["x1b"], ta=True, name=nm("dw_ffn_gate")),
                                           _mm(dup, s["x1b"], ta=True, name=nm("dw_ffn_up"))], axis=0))
    du1, dg, db = _ln_bwd(s["x"], s["mix"], dx1, row(w["ln1_g"]), name=nm("ln1_bwd"))
    g["ln1_g"], g["ln1_b"] = dg[0], db[0]
    dmerged = hook("mid", _mm(du1, w["w_out"], tb=True, name=nm("d_merged")))
    g["w_out"] = slabs(_mm(s["merged"], du1, ta=True, name=nm("dw_out")))
    dp_gl, d1, d2, d3, dbg = _merge_bwd(s["p_gl"], s["bg"], s["b1"], s["b2"], s["b3"], dmerged, name=nm("merge_bwd"))
    g["b_gate"] = dbg.reshape(3, D)
    dy_lru = _mm(d1, w["w_br_lru"], tb=True, name=nm("d_y_lru"))
    g["w_br_lru"] = slabs(_mm(s["y_lru"], d1, ta=True, name=nm("dw_br_lru")))
    dy_ssd = _mm(d2, w["w_br_ssd"], tb=True, name=nm("d_y_ssd"))
    g["w_br_ssd"] = slabs(_mm(s["y_ssd"], d2, ta=True, name=nm("dw_br_ssd")))
    dy_xa = _mm(d3, w["w_br_xa"], tb=True, out_dtype=BF16, name=nm("d_y_xa"))
    g["w_br_xa"] = slabs(_mm(s["y_xa"], d3, ta=True, name=nm("dw_br_xa")))
    dp_q, dkv = _xa_bwd(s["p_q"], s["kv"], dy_xa, name=nm("xa_bwd"))
    g["mem_w_kv"] = _mm(mem, dkv, ta=True, split_n=2 * D // N_DEV, name=nm("dw_kv"))
    dy_ssd = hook("branches", dy_ssd)
    ssd_cw, _, *ssd_rest = s["ssd_par"]
    dp_ssd, dcw, dcb, ddtb, dalog, dd, dnw = _ssd_bwd(s["p_ssd"], s["c_ssd"], s["yraw"], s["hs"], dy_ssd, ssd_cw,
                                                      *ssd_rest, name=nm("ssd_bwd"))
    g["ssd_conv_w"], g["ssd_conv_b"], g["ssd_dt_bias"] = dcw, dcb[0], ddtb[0]
    g["ssd_a_log"], g["ssd_d"], g["ssd_norm_w"] = dalog[0], dd[0], dnw[0]
    dp_ssd = hook("ssd", dp_ssd)
    lru_cw, _, *lru_rest = s["lru_par"]
    dp_lru, dcw, dcb, dwa, dba, dwi, dbi, dlam = _lru_bwd(s["p_lru"], s["xc"], s["h"], dy_lru, lru_cw, *lru_rest,
                                                          name=nm("lru_bwd"))
    g["lru_conv_w"], g["lru_conv_b"], g["lru_w_a"], g["lru_b_a"] = dcw, dcb[0], dwa, dba[0]
    g["lru_w_i"], g["lru_b_i"], g["lru_lambda"] = dwi, dbi[0], dlam[0]
    wi, x = s["wi"], s["xb"]
    g["w_in"] = _w_in_shards(_mm(x, dp_ssd, ta=True, name=nm("dw_in_ssd")), _mm(x, dp_lru, ta=True, name=nm("dw_in_lru")),
                             _mm(x, dp_q, ta=True, name=nm("dw_in_q")), _mm(x, dp_gl, ta=True, name=nm("dw_in_gl")),
                             name=nm("dw_in_shards"))
    dp_ssd = hook("weights", dp_ssd)
    dx = _mm(dp_ssd, wi["ssd"], tb=True, add=du1, add_scale=ALPHA, name=nm("dx_ssd"))
    dx = hook("dx", _mm(dp_lru, wi["lru"], tb=True, add=dx, name=nm("dx_lru")))
    dx = _mm(dp_q, wi["q"], tb=True, add=dx, name=nm("dx_q"))
    dx = _mm(dp_gl, wi["gl"], tb=True, add=dx, name=nm("dx_gl"))
    return dx, g


def _local_step(x, mem, target, layers, hooks=None):
    saved, xb = [], x.astype(BF16)
    for l in range(DEPTH):
        saved.append(_layer_fwd(x, xb, mem, layers[l], l))
        x, xb = saved[-1]["x2"], saved[-1]["x2b"]
    dx, loss = _loss_head(x, target, name="loss_head")
    grads = [None] * DEPTH
    for l in reversed(range(DEPTH)):
        dx, grads[l] = _layer_bwd(saved[l], mem, layers[l], dx, l, hooks[l] if hooks else None)
    return loss, dx, grads


def kernel(x, mem, w_in, b_gate, lru_conv_w, lru_conv_b, lru_w_a, lru_b_a, lru_w_i, lru_b_i, lru_lambda, ssd_conv_w, ssd_conv_b, ssd_dt_bias, ssd_a_log, ssd_d, ssd_norm_w, mem_w_kv, w_br_lru, w_br_ssd, w_br_xa, w_out, ln1_g, ln1_b, ffn_w_in, ffn_w_down, ln2_g, ln2_b, loss_target, m_w_in, m_b_gate, m_lru_conv_w, m_lru_conv_b, m_lru_w_a, m_lru_b_a, m_lru_w_i, m_lru_b_i, m_lru_lambda, m_ssd_conv_w, m_ssd_conv_b, m_ssd_dt_bias, m_ssd_a_log, m_ssd_d, m_ssd_norm_w, m_mem_w_kv, m_w_br_lru, m_w_br_ssd, m_w_br_xa, m_w_out, m_ln1_g, m_ln1_b, m_ffn_w_in, m_ffn_w_down, m_ln2_g, m_ln2_b, v_w_in, v_b_gate, v_lru_conv_w, v_lru_conv_b, v_lru_w_a, v_lru_b_a, v_lru_w_i, v_lru_b_i, v_lru_lambda, v_ssd_conv_w, v_ssd_conv_b, v_ssd_dt_bias, v_ssd_a_log, v_ssd_d, v_ssd_norm_w, v_mem_w_kv, v_w_br_lru, v_w_br_ssd, v_w_br_xa, v_w_out, v_ln1_g, v_ln1_b, v_ffn_w_in, v_ffn_w_down, v_ln2_g, v_ln2_b):
    local = dict(locals())
    w = {n: local[n] for n in _ORDER}
    m = {n: local["m_" + n] for n in _ORDER}
    v = {n: local["v_" + n] for n in _ORDER}
    for n in _TRANSPOSED:
        w[n], m[n], v[n] = (jnp.swapaxes(a, 1, 2) for a in (w[n], m[n], v[n]))

    big = [n for n, _, _ in _BIG]
    kinds = {n: kind for n, kind, _ in _BIG}

    small = _rows(jnp.concatenate([w[n].reshape(-1) for n, _ in _SMALL]), R_SMALL)
    first = _all_gather([w["w_in"][0].astype(BF16), small], name="gather_w_in_l0")
    rest, later, _ = lax.optimization_barrier(([w[n][0].astype(BF16) for n in big[1:]],
                                               [w[n][1].astype(BF16) for n in big], first[-1]))
    rest = _all_gather(rest, "gather_weights_l0", collective_id=1)
    later = _all_gather(later, "gather_weights_l1", collective_id=4)
    stacks = [dict(zip(big, [first[0], *rest])), dict(zip(big, later))]
    small_all, o, small_full = first[-1].reshape(N_DEV, R_SMALL * LANES), 0, {}
    for n, s in _SMALL:
        k = DEPTH * s[0] * s[1]
        small_full[n] = _from_stack(small_all[:, o:o + k].reshape((N_DEV, DEPTH) + s))
        o += k
    layers = []
    for l in range(DEPTH):
        lw = {n: w[n][l] for n, _ in _REP}
        lw.update({n: small_full[n][l] for n, _ in _SMALL})
        lw["wi"] = dict(zip(("ssd", "lru", "q", "gl"), _w_in_regions(stacks[l]["w_in"], name=f"w_in_regions_l{l}")))
        for n in big[1:]:
            if kinds[n] == "col":
                lw[n] = _cols_concat(stacks[l][n], name=f"full_{n}_l{l}")
            else:
                lw[n] = stacks[l][n].reshape(-1, stacks[l][n].shape[-1])
        layers.append(lw)

    by_dest = lambda a: a.reshape((4, 2) + a.shape[1:])
    slots, pending, last_layer = {}, {}, {}
    queue = [stacks[1]["w_out"]]

    def after_last(operands):
        operands, _ = lax.optimization_barrier((list(operands), queue[-1]))
        return operands

    def start(tag, collective_id, names_and_grads):
        names, owns = zip(*names_and_grads)
        gots = _exchange_sibling(after_last(owns), name=f"reduce_cores_{tag}", collective_id=collective_id)
        queue.append(gots[0])
        pending[tag] = (names, owns, gots)

    def finish(tag, collective_id, t):
        names, owns, gots = pending.pop(tag)
        t, gots = lax.optimization_barrier((t, gots))
        sums = [_pair_sum(own, got, name=f"pair_sum_{tag}_{n}", out_dtype=F32 if n == "tail" else BF16,
                          tb=R_SM if n == "tail" else _rowblk(own.shape[2], 256))
                for n, own, got in zip(names, owns, gots)]
        t, sums = lax.optimization_barrier((t, sums))
        got = _exchange_chips(sums, name=f"reduce_chips_{tag}", collective_id=collective_id)
        queue.append(got[0])
        slots.update({(tag, n): s for n, s in zip(names, got)})
        return t

    def tail_of(g0):
        stacked = {n: jnp.stack([g0[n], last_layer[n]]) for n in [s[0] for s in _SMALL + _REP]}
        sm = jnp.concatenate([_by_dest(stacked[n]) for n, _ in _SMALL], axis=1)
        sm = jnp.pad(sm, ((0, 0), (0, R_SMALL * LANES - sm.shape[1])))
        rep = jnp.concatenate([stacked[n].reshape(-1) for n, _ in _REP])
        rep = jnp.pad(rep, (0, N_DEV * R_REP * LANES - rep.shape[0])).reshape(N_DEV, R_REP * LANES)
        tail = jnp.concatenate([sm, rep, jnp.zeros((N_DEV, (R_SM - R_SMALL - R_REP) * LANES), F32)], axis=1)
        return tail.reshape(4, 2, R_SM, LANES)

    def weights_l1(t, g):
        last_layer.update(g)
        start("l1", 2, [(n, by_dest(g[n])) for n in big])
        return t

    def branches_l0(t, g):
        start("l0a", 5, [(n, by_dest(g[n])) for n in big[1:]])
        return t

    def weights_l0(t, g):
        start("l0b", 7, [("w_in", by_dest(g["w_in"])), ("tail", tail_of(g))])
        return t

    hooks = [{"branches": branches_l0, "ssd": lambda t, g: finish("l0a", 6, t), "weights": weights_l0,
              "dx": lambda t, g: finish("l0b", 8, t)},
             {"weights": weights_l1, "dx": lambda t, g: finish("l1", 3, t)}]
    loss_tile, dx, grads = _local_step(x[0], mem[0], loss_target[0], layers, hooks)
    loss = lax.psum(loss_tile[0, 0], ("x", "y", "c"))

    res = {}
    for n in big:
        tb = _rowblk(w[n].shape[1], 128 if w[n].shape[2] > LANES else 256)
        res[n] = _adamw([slots["l0b" if n == "w_in" else "l0a", n], slots["l1", n]], w[n], m[n], v[n],
                        name=f"adamw_{n}", tb=tb)
    tail_sum = _slot_sum(slots["l0b", "tail"], name="sum_tail")
    rep_all = _all_gather([tail_sum[R_SMALL:R_SMALL + R_REP]], name="gather_replicated")[0]
    g_tail = jnp.concatenate([tail_sum[:R_SMALL], rep_all.reshape(N_DEV * R_REP, LANES)], axis=0)
    tails = _adamw([g_tail[None]], _pack_tail(w)[None], _pack_tail(m)[None], _pack_tail(v)[None],
                   name="adamw_tail", tb=TB_TAIL)

    outs = []
    for kind in range(4):
        d = {**{n: res[n][kind] for n in big}, **_unpack_tail(tails[kind][0])}
        d.update({n: jnp.swapaxes(d[n], 1, 2) for n in _TRANSPOSED})
        outs += [d[n] for n in _ORDER]
    return (loss, dx[None], *outs)
```

```python
import math

import jax
import jax.numpy as jnp
from jax import lax
from jax.experimental import pallas as pl
from jax.experimental.pallas import tpu as pltpu
from jax.experimental.pallas import tpu_sc as plsc

F32 = jnp.float32
BF16 = jnp.bfloat16

D = 1024
DEPTH = 2
N_DEV = 8
CHUNK = 64
LRU_BLOCKS = 8
LRU_BLOCK = 128
LRU_C = 8.0
D_SSD = 2 * D
SSD_HEADS = 32
SSD_GROUPS = 4
GROUP_W = D_SSD // SSD_GROUPS
SSD_STATE = 128
D_XBC = D_SSD + 2 * SSD_GROUPS * SSD_STATE
XA_HEADS = 4
XA_HEAD_DIM = 256
D_FF = 2816
ALPHA = (2 * DEPTH) ** 0.25
EPS = 1e-5
N_IN = 11296

S_Z, S_XBC, S_DT, W_SSD = 0, 2048, 5120, 5632
DT_REAL = 32

ADAM_LR, ADAM_B1, ADAM_B2, ADAM_EPS, ADAM_WD, ADAM_STEP = 0.001, 0.9, 0.999, 1e-08, 0.01, 10

VMEM_LIMIT = 56 * 1024 * 1024
MESH = pl.DeviceIdType.MESH
ANY = pl.BlockSpec(memory_space=pl.ANY)


def _cp(*sem):
    return pltpu.CompilerParams(dimension_semantics=sem, vmem_limit_bytes=VMEM_LIMIT)


def _blk(n, target):
    if n % 128:
        return n
    best = 128
    for b in range(128, min(n, target) + 1, 128):
        if n % b == 0:
            best = b
    return best


def _iota(shape, dim):
    return lax.broadcasted_iota(jnp.int32, shape, dim)


def _sigmoid(x):
    return 0.5 + 0.5 * jnp.tanh(0.5 * x)


def _log1p(e):
    u = 1.0 + e
    return jnp.where(u == 1.0, e, jnp.log(u) * (e / (u - 1.0)))


def _softplus(x):
    return jnp.maximum(x, 0.0) + _log1p(jnp.exp(-jnp.abs(x)))


_G0 = math.sqrt(2.0 / math.pi)
_G1 = 0.044715


def _gelu_and_grad(x):
    x2 = x * x
    u = 0.5 + 0.5 * jnp.tanh(x * (_G0 + (_G0 * _G1) * x2))
    dg = u + (x * (u * (1.0 - u))) * ((2.0 * _G0) + (6.0 * _G0 * _G1) * x2)
    return x * u, dg


_NN = (((1,), (0,)), ((), ()))
_NT = (((1,), (1,)), ((), ()))
_TN = (((0,), (0,)), ((), ()))


def _dot(a, b, dims=_NN):
    return lax.dot_general(a.astype(BF16), b.astype(BF16), dims, preferred_element_type=F32)


def _dot_hi(a, b, dims=_NN):
    return lax.dot_general(a, b, dims, precision=lax.Precision.HIGHEST, preferred_element_type=F32)


def _split(v):
    hi = v.astype(BF16)
    return hi, (v - hi.astype(F32)).astype(BF16)


def _dot01(v, e, dims=_NN):
    hi, lo = _split(v)
    return (lax.dot_general(hi, e, dims, preferred_element_type=F32)
            + lax.dot_general(lo, e, dims, preferred_element_type=F32))


def _conv_taps(xe, n):
    return [xe[8:8 + n] if j == 3 else pltpu.roll(xe, 3 - j, 0)[8:8 + n] for j in range(4)]


def _conv_fwd(taps, cw, cb):
    return cb + cw[0:1] * taps[0] + cw[1:2] * taps[1] + cw[2:3] * taps[2] + cw[3:4] * taps[3]


def _conv_bwd(dc, dnext, x, cw, n):
    ext = jnp.concatenate([dc, dnext], axis=0)
    shifted = [pltpu.roll(ext, n + 8 - (3 - j), 0)[0:n] for j in range(3)] + [dc]
    dx = cw[0:1] * shifted[0] + cw[1:2] * shifted[1] + cw[2:3] * shifted[2] + cw[3:4] * dc
    dcw = jnp.concatenate([jnp.sum(x * shifted[j], axis=0, keepdims=True) for j in range(4)], axis=0)
    return dx, dcw, jnp.sum(dc, axis=0, keepdims=True)


MM_VMEM_BUDGET = 44 * 1024 * 1024
MM_MAX_TILE = 1408
MM_MAX_K = 5632


def _divisors(n, cap):
    return [n] if n % 128 else [b for b in range(128, min(n, cap) + 1, 128) if n % b == 0]


def _mm_tiles(m_dim, n_dim, k_dim, a_bytes, b_bytes, o_bytes, has_add, tn_fixed):
    best = None
    for tm in _divisors(m_dim, MM_MAX_TILE):
        for tn in ([tn_fixed] if tn_fixed else _divisors(n_dim, MM_MAX_TILE)):
            for tk in _divisors(k_dim, MM_MAX_K):
                vmem = 2 * (tm * tk * a_bytes + tk * tn * b_bytes + tm * tn * (o_bytes + (4 if has_add else 0)))
                vmem += tm * tn * 4 if tk < k_dim else 0
                if vmem <= MM_VMEM_BUDGET:
                    key = (tm * tn * tk, tk, tn)
                    if best is None or key > best[0]:
                        best = (key, (tm, tn, tk))
    assert best is not None, (m_dim, n_dim, k_dim)
    return best[1]


def _mm(a, b, *, ta=False, tb=False, out_dtype=F32, add=None, add_scale=1.0, name, split_n=None):
    if ta:
        k_dim, m_dim = a.shape
    else:
        m_dim, k_dim = a.shape
    if tb:
        n_dim, k2 = b.shape
    else:
        k2, n_dim = b.shape
    assert k_dim == k2, (a.shape, b.shape, ta, tb)
    tm, tn, tk = _mm_tiles(m_dim, n_dim, k_dim, a.dtype.itemsize, b.dtype.itemsize, jnp.dtype(out_dtype).itemsize,
                           add is not None, split_n)
    nk = k_dim // tk
    a_spec = pl.BlockSpec((tk, tm), lambda i, j, k: (k, i)) if ta else pl.BlockSpec((tm, tk), lambda i, j, k: (i, k))
    b_spec = pl.BlockSpec((tn, tk), lambda i, j, k: (j, k)) if tb else pl.BlockSpec((tk, tn), lambda i, j, k: (k, j))
    o_spec = pl.BlockSpec((tm, tn), lambda i, j, k: (i, j))
    out_shape = (m_dim, n_dim)
    if split_n is not None:
        assert add is None and tn == split_n, (tn, split_n)
        o_spec = pl.BlockSpec((None, tm, tn), lambda i, j, k: (j, i, 0))
        out_shape = (n_dim // tn, m_dim, tn)
    dims = (((0 if ta else 1,), (1 if tb else 0,)), ((), ()))
    has_add = add is not None

    def body(*refs):
        a_ref, b_ref = refs[:2]
        add_ref = refs[2] if has_add else None
        o_ref = refs[3] if has_add else refs[2]
        acc_ref = refs[-1] if nk > 1 else None
        k = pl.program_id(2)

        def product():
            return lax.dot_general(a_ref[...].astype(BF16), b_ref[...].astype(BF16), dims, preferred_element_type=F32)

        def finish(r):
            if has_add:
                r = r + add_scale * add_ref[...]
            o_ref[...] = r.astype(out_dtype)

        if nk == 1:
            finish(product())
            return

        @pl.when(k == 0)
        def _():
            acc_ref[...] = product()

        @pl.when((k > 0) & (k < nk - 1))
        def _():
            acc_ref[...] += product()

        @pl.when(k == nk - 1)
        def _():
            finish(acc_ref[...] + product())

    in_specs = [a_spec, b_spec] + ([o_spec] if has_add else [])
    args = (a, b) + ((add,) if has_add else ())
    return pl.pallas_call(
        body, name=name, grid=(m_dim // tm, n_dim // tn, nk),
        in_specs=in_specs, out_specs=o_spec,
        out_shape=jax.ShapeDtypeStruct(out_shape, out_dtype),
        scratch_shapes=[pltpu.VMEM((tm, tn), F32)] if nk > 1 else [],
        cost_estimate=pl.CostEstimate(
            flops=2 * m_dim * n_dim * k_dim, transcendentals=0,
            bytes_accessed=a.size * a.dtype.itemsize + b.size * b.dtype.itemsize
            + m_dim * n_dim * (jnp.dtype(out_dtype).itemsize + (4 if has_add else 0))),
        compiler_params=_cp("parallel", "parallel", "arbitrary"),
    )(*args)


def _ln_fwd(x, f, g, b, name, tb=512):
    t = x.shape[0]
    tb = min(tb, t)

    def body(x_ref, f_ref, g_ref, b_ref, o_ref, ob_ref):
        u = ALPHA * x_ref[...] + f_ref[...]
        mu = jnp.mean(u, axis=-1, keepdims=True)
        d = u - mu
        var = jnp.mean(d * d, axis=-1, keepdims=True)
        y = d * lax.rsqrt(var + EPS) * g_ref[...] + b_ref[...]
        o_ref[...] = y
        ob_ref[...] = y.astype(BF16)

    row = pl.BlockSpec((tb, D), lambda i: (i, 0))
    par = pl.BlockSpec((1, D), lambda i: (0, 0))
    return pl.pallas_call(
        body, name=name, grid=(t // tb,), in_specs=[row, row, par, par], out_specs=[row, row],
        out_shape=[jax.ShapeDtypeStruct((t, D), F32), jax.ShapeDtypeStruct((t, D), BF16)],
        compiler_params=_cp("parallel"),
    )(x, f, g, b)


def _ln_bwd(x, f, dy, g, name, tb=512):
    t = x.shape[0]
    tb = min(tb, t)

    def body(x_ref, f_ref, dy_ref, g_ref, du_ref, dg_ref, db_ref):
        @pl.when(pl.program_id(0) == 0)
        def _():
            dg_ref[...] = jnp.zeros_like(dg_ref)
            db_ref[...] = jnp.zeros_like(db_ref)

        u = ALPHA * x_ref[...] + f_ref[...]
        mu = jnp.mean(u, axis=-1, keepdims=True)
        d = u - mu
        var = jnp.mean(d * d, axis=-1, keepdims=True)
        rstd = lax.rsqrt(var + EPS)
        xhat = d * rstd
        dy = dy_ref[...]
        dxh = dy * g_ref[...]
        m1 = jnp.mean(dxh, axis=-1, keepdims=True)
        m2 = jnp.mean(dxh * xhat, axis=-1, keepdims=True)
        du_ref[...] = rstd * (dxh - m1 - xhat * m2)
        dg_ref[...] += jnp.sum(dy * xhat, axis=0, keepdims=True)
        db_ref[...] += jnp.sum(dy, axis=0, keepdims=True)

    row = pl.BlockSpec((tb, D), lambda i: (i, 0))
    par = pl.BlockSpec((1, D), lambda i: (0, 0))
    return pl.pallas_call(
        body, name=name, grid=(t // tb,), in_specs=[row, row, row, par], out_specs=[row, par, par],
        out_shape=[jax.ShapeDtypeStruct((t, D), F32), jax.ShapeDtypeStruct((1, D), F32),
                   jax.ShapeDtypeStruct((1, D), F32)],
        compiler_params=_cp("arbitrary"),
    )(x, f, dy, g)


FFN_TM, FFN_TN = 512, D_FF // 2


def _ffn_in_swiglu(x, w, name):
    t = x.shape[0]
    tm = min(FFN_TM, t)
    nj = D_FF // FFN_TN

    def body(x_ref, wg_ref, wu_ref, g_ref, u_ref, a_ref):
        xb = x_ref[...].astype(BF16)
        g = lax.dot_general(xb, wg_ref[...], _NT, preferred_element_type=F32)
        u = lax.dot_general(xb, wu_ref[...], _NT, preferred_element_type=F32)
        g_ref[...] = g.astype(BF16)
        u_ref[...] = u.astype(BF16)
        a_ref[...] = (g * _sigmoid(g) * u).astype(BF16)

    tile = pl.BlockSpec((tm, FFN_TN), lambda i, j: (i, j))
    return pl.pallas_call(
        body, name=name, grid=(t // tm, nj),
        in_specs=[pl.BlockSpec((tm, D), lambda i, j: (i, 0)), pl.BlockSpec((FFN_TN, D), lambda i, j: (j, 0)),
                  pl.BlockSpec((FFN_TN, D), lambda i, j: (nj + j, 0))],
        out_specs=[tile, tile, tile],
        out_shape=[jax.ShapeDtypeStruct((t, D_FF), BF16)] * 3,
        compiler_params=_cp("parallel", "parallel"),
    )(x, w, w)


def _d_swiglu(du, w_down, g, u, name):
    t = du.shape[0]
    tm = min(FFN_TM, t)

    def body(du_ref, w_ref, g_ref, u_ref, dg_ref, dup_ref):
        da = lax.dot_general(du_ref[...].astype(BF16), w_ref[...], _NT, preferred_element_type=F32)
        g_v = g_ref[...].astype(F32)
        s = _sigmoid(g_v)
        dg_ref[...] = (da * u_ref[...].astype(F32) * (s * (1.0 + g_v * (1.0 - s)))).astype(BF16)
        dup_ref[...] = (da * g_v * s).astype(BF16)

    tile = pl.BlockSpec((tm, FFN_TN), lambda i, j: (i, j))
    return pl.pallas_call(
        body, name=name, grid=(t // tm, D_FF // FFN_TN),
        in_specs=[pl.BlockSpec((tm, D), lambda i, j: (i, 0)), pl.BlockSpec((FFN_TN, D), lambda i, j: (j, 0)), tile, tile],
        out_specs=[tile, tile],
        out_shape=[jax.ShapeDtypeStruct((t, D_FF), BF16), jax.ShapeDtypeStruct((t, D_FF), BF16)],
        compiler_params=_cp("parallel", "parallel"),
    )(du, w_down, g, u)


def _merge_fwd(pgl, bg, b1, b2, b3, name, tb=512):
    t = pgl.shape[0]
    tb = min(tb, t)

    def body(gl_ref, bg_ref, b1_ref, b2_ref, b3_ref, o_ref):
        acc = None
        for j, b_ref in enumerate((b1_ref, b2_ref, b3_ref)):
            sl = slice(j * D, (j + 1) * D)
            term = _sigmoid(gl_ref[:, sl].astype(F32) + bg_ref[:, sl]) * b_ref[...].astype(F32)
            acc = term if acc is None else acc + term
        o_ref[...] = acc.astype(BF16)

    row = pl.BlockSpec((tb, D), lambda i: (i, 0))
    return pl.pallas_call(
        body, name=name, grid=(t // tb,),
        in_specs=[pl.BlockSpec((tb, 3 * D), lambda i: (i, 0)), pl.BlockSpec((1, 3 * D), lambda i: (0, 0)), row, row, row],
        out_specs=row, out_shape=jax.ShapeDtypeStruct((t, D), BF16), compiler_params=_cp("parallel"),
    )(pgl, bg, b1, b2, b3)


def _merge_bwd(pgl, bg, b1, b2, b3, dm, name, tb=512):
    t = pgl.shape[0]
    tb = min(tb, t)

    def body(gl_ref, bg_ref, b1_ref, b2_ref, b3_ref, dm_ref, dgl_ref, d1_ref, d2_ref, d3_ref, dbg_ref):
        @pl.when(pl.program_id(0) == 0)
        def _():
            dbg_ref[...] = jnp.zeros_like(dbg_ref)

        dm_v = dm_ref[...]
        for j, (b_ref, d_ref) in enumerate(((b1_ref, d1_ref), (b2_ref, d2_ref), (b3_ref, d3_ref))):
            sl = slice(j * D, (j + 1) * D)
            gate = _sigmoid(gl_ref[:, sl].astype(F32) + bg_ref[:, sl])
            d_ref[...] = (dm_v * gate).astype(BF16)
            dgl = dm_v * b_ref[...].astype(F32) * (gate * (1.0 - gate))
            dgl_ref[:, sl] = dgl.astype(BF16)
            dbg_ref[:, sl] += jnp.sum(dgl, axis=0, keepdims=True)

    row = pl.BlockSpec((tb, D), lambda i: (i, 0))
    wide = pl.BlockSpec((tb, 3 * D), lambda i: (i, 0))
    par = pl.BlockSpec((1, 3 * D), lambda i: (0, 0))
    return pl.pallas_call(
        body, name=name, grid=(t // tb,),
        in_specs=[wide, par, row, row, row, row], out_specs=[wide, row, row, row, par],
        out_shape=[jax.ShapeDtypeStruct((t, 3 * D), BF16)] + [jax.ShapeDtypeStruct((t, D), BF16)] * 3
                  + [jax.ShapeDtypeStruct((1, 3 * D), F32)],
        compiler_params=_cp("arbitrary"),
    )(pgl, bg, b1, b2, b3, dm)


def _xa_probs(q, kv_ref, hd):
    sl = slice(hd * XA_HEAD_DIM, (hd + 1) * XA_HEAD_DIM)
    k = kv_ref[:, sl]
    v = kv_ref[:, D + hd * XA_HEAD_DIM:D + (hd + 1) * XA_HEAD_DIM]
    s = _dot(q[:, sl], k, _NT) * (XA_HEAD_DIM ** -0.5)
    e = jnp.exp(s - jnp.max(s, axis=1, keepdims=True))
    return sl, k, v, e / jnp.sum(e, axis=1, keepdims=True)


def _xa_fwd(pq, kv, name, tb=512):
    t = pq.shape[0]
    tb = min(tb, t)

    def body(q_ref, kv_ref, o_ref):
        q = q_ref[...]
        for hd in range(XA_HEADS):
            sl, _, v, p = _xa_probs(q, kv_ref, hd)
            o_ref[:, sl] = _dot(p, v).astype(BF16)

    row = pl.BlockSpec((tb, D), lambda i: (i, 0))
    return pl.pallas_call(
        body, name=name, grid=(t // tb,),
        in_specs=[row, pl.BlockSpec(kv.shape, lambda i: (0, 0))], out_specs=row,
        out_shape=jax.ShapeDtypeStruct((t, D), BF16), compiler_params=_cp("parallel"),
    )(pq, kv)


def _xa_bwd(pq, kv, dy, name, tb=512):
    t = pq.shape[0]
    tb = min(tb, t)

    def body(q_ref, kv_ref, dy_ref, dq_ref, dkv_ref):
        @pl.when(pl.program_id(0) == 0)
        def _():
            dkv_ref[...] = jnp.zeros_like(dkv_ref)

        q = q_ref[...]
        for hd in range(XA_HEADS):
            sl, k, v, p = _xa_probs(q, kv_ref, hd)
            dyh = dy_ref[:, sl]
            vsl = slice(D + hd * XA_HEAD_DIM, D + (hd + 1) * XA_HEAD_DIM)
            dkv_ref[:, vsl] += _dot(p, dyh, _TN)
            dp = _dot(dyh, v, _NT)
            ds = p * (dp - jnp.sum(dp * p, axis=1, keepdims=True)) * (XA_HEAD_DIM ** -0.5)
            dq_ref[:, sl] = _dot(ds, k).astype(BF16)
            dkv_ref[:, sl] += _dot(ds, q[:, sl], _TN)

    row = pl.BlockSpec((tb, D), lambda i: (i, 0))
    kvs = pl.BlockSpec(kv.shape, lambda i: (0, 0))
    return pl.pallas_call(
        body, name=name, grid=(t // tb,), in_specs=[row, kvs, row], out_specs=[row, kvs],
        out_shape=[jax.ShapeDtypeStruct((t, D), BF16), jax.ShapeDtypeStruct(kv.shape, F32)],
        compiler_params=_cp("arbitrary"),
    )(pq, kv, dy)


SUBLANES = 8


def _scan(a, u, reverse):
    n, c = a.shape
    groups = n // SUBLANES
    a = a.reshape(groups, SUBLANES, c)
    u = u.reshape(groups, SUBLANES, c)
    sub = _iota((1, SUBLANES, 1), 1)
    d = 1
    while d < SUBLANES:
        keep = (sub < SUBLANES - d) if reverse else (sub >= d)
        shift = SUBLANES - d if reverse else d
        u = a * jnp.where(keep, pltpu.roll(u, shift, 1), 0.0) + u
        a = a * jnp.where(keep, pltpu.roll(a, shift, 1), 1.0)
        d *= 2
    edge = 0 if reverse else SUBLANES - 1
    out, carry = [None] * groups, None
    for j in (reversed(range(groups)) if reverse else range(groups)):
        out[j] = u[j] if carry is None else u[j] + a[j] * carry
        carry = out[j][edge:edge + 1]
    return jnp.concatenate(out, axis=0)


def _lru_gates(xc, wa_ref, ba, wi_ref, bi, lam):
    za = jnp.concatenate([_dot(xc[:, n * 128:(n + 1) * 128], wa_ref[n]) for n in range(LRU_BLOCKS)], axis=1) + ba
    zi = jnp.concatenate([_dot(xc[:, n * 128:(n + 1) * 128], wi_ref[n]) for n in range(LRU_BLOCKS)], axis=1) + bi
    r = 1.0 / (1.0 + jnp.exp(-za))
    ig = _sigmoid(zi)
    sp = _softplus(-lam)
    log_a = (-LRU_C) * r * sp
    a = jnp.exp(log_a)
    m = jnp.sqrt(-jnp.tanh(log_a) * (1.0 + a * a))
    u = m * (ig * xc)
    return a, u, r, ig, m, sp


def _lru_fwd(p, cw, cb, wa, ba, wi, bi, lam, name, tb=256):
    t = p.shape[0]
    tb = min(tb, t)
    nb = t // tb
    r8 = tb // 8

    def body(x_ref, xp_ref, g_ref, cw_ref, cb_ref, wa_ref, ba_ref, wi_ref, bi_ref, lam_ref, y_ref, h_ref, xc_ref,
             hc_ref):
        i = pl.program_id(0)

        @pl.when(i == 0)
        def _():
            hc_ref[...] = jnp.zeros_like(hc_ref)

        halo = jnp.where(i == 0, 0.0, xp_ref[...])
        taps = _conv_taps(jnp.concatenate([halo, x_ref[...]], axis=0), tb)
        xc = _conv_fwd(taps, cw_ref[...], cb_ref[...])
        xc_ref[...] = xc
        a, u, _, _, _, _ = _lru_gates(xc, wa_ref, ba_ref[...], wi_ref, bi_ref[...], lam_ref[...])
        row = _iota((tb, 1), 0)
        u = u + jnp.where(row == 0, a * hc_ref[...], 0.0)
        h = _scan(a, u, reverse=False)
        h_ref[...] = h
        hc_ref[...] = h[tb - 1:tb, :]
        gl, _ = _gelu_and_grad(g_ref[...])
        y_ref[...] = (gl * h).astype(BF16)

    par = pl.BlockSpec((1, D), lambda i: (0, 0))
    wsp = pl.BlockSpec((LRU_BLOCKS, LRU_BLOCK, LRU_BLOCK), lambda i: (0, 0, 0))
    row = pl.BlockSpec((tb, D), lambda i: (i, 0))
    return pl.pallas_call(
        body, name=name, grid=(nb,),
        in_specs=[row, pl.BlockSpec((8, D), lambda i: (jnp.maximum(i * r8 - 1, 0), 0)),
                  pl.BlockSpec((tb, D), lambda i: (i, 1)),
                  pl.BlockSpec((4, D), lambda i: (0, 0)), par, wsp, par, wsp, par, par],
        out_specs=[row, row, row],
        out_shape=[jax.ShapeDtypeStruct((t, D), BF16), jax.ShapeDtypeStruct((t, D), F32),
                   jax.ShapeDtypeStruct((t, D), F32)],
        scratch_shapes=[pltpu.VMEM((1, D), F32)],
        compiler_params=_cp("arbitrary"),
    )(p, p, p, cw, cb, wa, ba, wi, bi, lam)


def _lru_bwd(p, xc, h, dy, cw, wa, ba, wi, bi, lam, name, tb=256):
    t = p.shape[0]
    tb = min(tb, t)
    nb = t // tb
    r8 = tb // 8

    def body(x_ref, g_ref, xc_ref, h_ref, hp_ref, dy_ref, cw_ref, wa_ref, ba_ref, wi_ref, bi_ref, lam_ref,
             dp_ref, dcw_ref, dcb_ref, dwa_ref, dba_ref, dwi_ref, dbi_ref, dlam_ref, carry_ref, dnext_ref):
        i = pl.program_id(0)
        blk = nb - 1 - i

        @pl.when(i == 0)
        def _():
            for r in (dcw_ref, dcb_ref, dwa_ref, dba_ref, dwi_ref, dbi_ref, dlam_ref, carry_ref, dnext_ref):
                r[...] = jnp.zeros_like(r)

        xc = xc_ref[...]
        lam = lam_ref[...]
        a, _, r, ig, m, sp = _lru_gates(xc, wa_ref, ba_ref[...], wi_ref, bi_ref[...], lam)
        gl, dgl = _gelu_and_grad(g_ref[...])
        h = h_ref[...]
        dy = dy_ref[...]
        dp_ref[:, D:] = (dy * h * dgl).astype(BF16)
        row = _iota((tb, 1), 0)
        dh = dy * gl + jnp.where(row == tb - 1, carry_ref[...], 0.0)
        b = jnp.where(row < tb - 1, pltpu.roll(a, tb - 1, 0), 0.0)
        gs = _scan(b, dh, reverse=True)
        carry_ref[...] = a[0:1] * gs[0:1]
        h_last = jnp.where(blk == 0, 0.0, hp_ref[7:8, :])
        hprev = jnp.where(row == 0, h_last, pltpu.roll(h, 1, 0))
        da = gs * hprev
        dm = gs * ig * xc
        di = gs * m * xc
        dxc = gs * m * ig
        dlog = a * (da - a * (dm / m))
        dr = dlog * ((-LRU_C) * sp)
        dsp = jnp.sum(dlog * ((-LRU_C) * r), axis=0, keepdims=True)
        dlam_ref[...] += dsp * (-_sigmoid(-lam))
        dza = dr * r * (1.0 - r)
        dzi = di * ig * (1.0 - ig)
        dba_ref[...] += jnp.sum(dza, axis=0, keepdims=True)
        dbi_ref[...] += jnp.sum(dzi, axis=0, keepdims=True)
        parts = []
        for n in range(LRU_BLOCKS):
            sl = slice(n * 128, (n + 1) * 128)
            dwa_ref[n] += _dot(xc[:, sl], dza[:, sl], _TN)
            dwi_ref[n] += _dot(xc[:, sl], dzi[:, sl], _TN)
            parts.append(_dot(dza[:, sl], wa_ref[n], _NT) + _dot(dzi[:, sl], wi_ref[n], _NT))
        dxc = dxc + jnp.concatenate(parts, axis=1)
        dx, dcw, dcb = _conv_bwd(dxc, dnext_ref[...], x_ref[...], cw_ref[...], tb)
        dp_ref[:, :D] = dx.astype(BF16)
        dcw_ref[...] += dcw
        dcb_ref[...] += dcb
        dnext_ref[...] = dxc[0:8]

    par = pl.BlockSpec((1, D), lambda i: (0, 0))
    wsp = pl.BlockSpec((LRU_BLOCKS, LRU_BLOCK, LRU_BLOCK), lambda i: (0, 0, 0))
    cws = pl.BlockSpec((4, D), lambda i: (0, 0))
    rev = lambda i: nb - 1 - i
    blk0 = pl.BlockSpec((tb, D), lambda i: (rev(i), 0))
    w_shape = jax.ShapeDtypeStruct((LRU_BLOCKS, LRU_BLOCK, LRU_BLOCK), F32)
    v_shape = jax.ShapeDtypeStruct((1, D), F32)
    return pl.pallas_call(
        body, name=name, grid=(nb,),
        in_specs=[blk0, pl.BlockSpec((tb, D), lambda i: (rev(i), 1)), blk0, blk0,
                  pl.BlockSpec((8, D), lambda i: (jnp.maximum(rev(i) * r8 - 1, 0), 0)), blk0,
                  cws, wsp, par, wsp, par, par],
        out_specs=[pl.BlockSpec((tb, 2 * D), lambda i: (rev(i), 0)), cws, par, wsp, par, wsp, par, par],
        out_shape=[jax.ShapeDtypeStruct((t, 2 * D), BF16), jax.ShapeDtypeStruct((4, D), F32), v_shape,
                   w_shape, v_shape, w_shape, v_shape, v_shape],
        scratch_shapes=[pltpu.VMEM((1, D), F32), pltpu.VMEM((8, D), F32)],
        compiler_params=_cp("arbitrary"),
    )(p, p, xc, h, h, dy, cw, wa, ba, wi, bi, lam)


def _ssd_consts():
    m0 = _iota((1, 128), 1) < 64
    e = (jnp.right_shift(_iota((SSD_HEADS, D_SSD), 1), 6) == _iota((SSD_HEADS, D_SSD), 0)).astype(BF16)
    tril = (_iota((CHUNK, CHUNK), 0) >= _iota((CHUNK, CHUNK), 1)).astype(F32)
    eye = (_iota((SSD_HEADS, SSD_HEADS), 0) == _iota((SSD_HEADS, SSD_HEADS), 1)).astype(F32)
    r2 = _iota((CHUNK, 128), 0)
    c2 = jnp.bitwise_and(_iota((CHUNK, 128), 1), 63)
    return dict(m0=m0, e=e, tril=tril, eye=eye, causal2=r2 >= c2, fold=(c2 == r2).astype(BF16))


SSD_STEP = 4


def _ssd_pre(c, dt_raw, dtb_ref, alog_ref, dvec_ref, k):
    sg = _sigmoid(c)
    xbc = c * sg
    dtp = dt_raw + dtb_ref[...]
    dt = _softplus(dtp)
    a = -jnp.exp(alog_ref[...])
    cs = _dot_hi(k["tril"], dt * a)
    cs_last = cs[CHUNK - 1:CHUNK]
    dend = jnp.exp(cs_last - cs)
    cdec = jnp.exp(cs_last)
    big = _dot01(jnp.concatenate([dt, jnp.exp(cs), dend], axis=0), k["e"])
    small = _dot01(jnp.concatenate([jnp.broadcast_to(cdec, (8, SSD_HEADS)),
                                    jnp.broadcast_to(dvec_ref[...], (8, SSD_HEADS))], axis=0), k["e"])
    cst2 = _dot_hi(k["eye"], jnp.concatenate([cs, cs], axis=0), _NT)
    return dict(c=c, sg=sg, xs=xbc[:, :D_SSD], bm=xbc[:, D_SSD:D_SSD + 512],
                cm=xbc[:, D_SSD + 512:], dtp=dtp, dt=dt, a=a, cs=cs, dend=dend, cdec=cdec,
                dtx=big[0:CHUNK], ecx=big[CHUNK:2 * CHUNK], dex=big[2 * CHUNK:3 * CHUNK],
                cdx=small[0:1], ddx=small[8:9], cst2=cst2)


def _pair_decay(p, cs, cst2, k):
    h0, h1 = 2 * p, 2 * p + 1
    colp = jnp.where(k["m0"], cs[:, h0:h0 + 1], cs[:, h1:h1 + 1])
    rowp = jnp.where(k["m0"], cst2[h0:h0 + 1, :], cst2[h1:h1 + 1, :])
    return jnp.where(k["causal2"], jnp.exp(colp - rowp), 0.0)


def _pair_stack(xp, k):
    return jnp.concatenate([jnp.where(k["m0"], xp, 0.0), jnp.where(k["m0"], 0.0, xp)], axis=0)


def _group_norm(yz, nw, with_stats=False):
    outs, stats = [], []
    for g in range(SSD_GROUPS):
        yzg = yz[:, g * GROUP_W:(g + 1) * GROUP_W]
        r = lax.rsqrt(jnp.mean(yzg * yzg, axis=1, keepdims=True) + EPS)
        outs.append(yzg * r)
        stats.append(r)
    y = jnp.concatenate(outs, axis=1) * nw
    return (y, stats) if with_stats else y


def _ssd_fwd(p, cw, cb, dtb, alog, dvec, nw, name):
    t = p.shape[0]
    step = SSD_STEP if t % (SSD_STEP * CHUNK) == 0 else 1
    rows_blk, nb, nc = step * CHUNK, t // (step * CHUNK), t // CHUNK

    def body(p_blk, cw_ref, cb_ref, dtb_ref, alog_ref, dvec_ref, nw_ref, y_blk, yraw_blk, hs_blk, c_blk,
             h_scr, tail_scr):
        @pl.when(pl.program_id(0) == 0)
        def _():
            h_scr[...] = jnp.zeros_like(h_scr)
            tail_scr[...] = jnp.zeros_like(tail_scr)

        k = _ssd_consts()

        def one_chunk(j, carry):
            rows = pl.ds(pl.multiple_of(j * CHUNK, CHUNK), CHUNK)
            chunk(p_blk.at[rows], y_blk.at[rows], yraw_blk.at[rows], hs_blk.at[j], c_blk.at[rows], k,
                  cw_ref, cb_ref, dtb_ref, alog_ref, dvec_ref, nw_ref, h_scr, tail_scr)
            return carry

        lax.fori_loop(0, step, one_chunk, 0)

    def chunk(p_ref, y_ref, yraw_ref, hs_ref, c_ref, k, cw_ref, cb_ref, dtb_ref, alog_ref, dvec_ref, nw_ref,
              h_scr, tail_scr):
        x_in = p_ref[:, S_XBC:S_DT]
        taps = _conv_taps(jnp.concatenate([tail_scr[...], x_in], axis=0), CHUNK)
        tail_scr[...] = x_in[CHUNK - 8:]
        c = _conv_fwd(taps, cw_ref[...], cb_ref[...])
        c_ref[...] = c
        s = _ssd_pre(c, p_ref[:, S_DT:S_DT + DT_REAL], dtb_ref, alog_ref, dvec_ref, k)
        xs, bm, cm = s["xs"], s["bm"], s["cm"]
        xdt = xs * s["dtx"]
        hprev = h_scr[...]
        hs_ref[...] = hprev
        ys, hn = [], []
        for g in range(SSD_GROUPS):
            gs = slice(g * GROUP_W, (g + 1) * GROUP_W)
            bg = bm[:, g * 128:(g + 1) * 128]
            cg = cm[:, g * 128:(g + 1) * 128]
            cbdup = _dot(cg, jnp.concatenate([bg, bg], axis=0), _NT)
            hp_g = hprev[:, gs]
            yd = []
            for q in range(4):
                pr = g * 4 + q
                mp = cbdup * _pair_decay(pr, s["cs"], s["cst2"], k)
                yd.append(_dot(mp, _pair_stack(xdt[:, pr * 128:(pr + 1) * 128], k)))
            ys.append(jnp.concatenate(yd, axis=1) + _dot(cg, hp_g) * s["ecx"][:, gs])
            hn.append(hp_g * s["cdx"][:, gs] + _dot(bg, xdt[:, gs] * s["dex"][:, gs], _TN))
        h_scr[...] = jnp.concatenate(hn, axis=1)
        yraw = jnp.concatenate(ys, axis=1) + s["ddx"] * xs
        yraw_ref[...] = yraw
        z = p_ref[:, S_Z:S_Z + D_SSD]
        y_ref[...] = _group_norm(yraw * (z * _sigmoid(z)), nw_ref[...]).astype(BF16)

    hv = pl.BlockSpec((1, DT_REAL), lambda i: (0, 0))
    return pl.pallas_call(
        body, name=name, grid=(nb,),
        in_specs=[pl.BlockSpec((rows_blk, W_SSD), lambda i: (i, 0)),
                  pl.BlockSpec((4, D_XBC), lambda i: (0, 0)), pl.BlockSpec((1, D_XBC), lambda i: (0, 0)),
                  hv, hv, hv, pl.BlockSpec((1, D_SSD), lambda i: (0, 0))],
        out_specs=[pl.BlockSpec((rows_blk, D_SSD), lambda i: (i, 0)), pl.BlockSpec((rows_blk, D_SSD), lambda i: (i, 0)),
                   pl.BlockSpec((step, SSD_STATE, D_SSD), lambda i: (i, 0, 0)),
                   pl.BlockSpec((rows_blk, D_XBC), lambda i: (i, 0))],
        out_shape=[jax.ShapeDtypeStruct((t, D_SSD), BF16), jax.ShapeDtypeStruct((t, D_SSD), F32),
                   jax.ShapeDtypeStruct((nc, SSD_STATE, D_SSD), F32), jax.ShapeDtypeStruct((t, D_XBC), F32)],
        scratch_shapes=[pltpu.VMEM((SSD_STATE, D_SSD), F32), pltpu.VMEM((8, D_XBC), F32)],
        compiler_params=_cp("arbitrary"),
    )(p, cw, cb, dtb, alog, dvec, nw)


def _ssd_bwd(p, c, yraw, hs, dy, cw, dtb, alog, dvec, nw, name):
    t = p.shape[0]
    step = 1
    rows_blk, nb = step * CHUNK, t // (step * CHUNK)

    def body(p_blk, c_blk, yraw_blk, hs_blk, dy_blk, cw_ref, dtb_ref, alog_ref, dvec_ref, nw_ref,
             dp_blk, dcw_ref, dcb_ref, ddtb_ref, dalog_ref, dd_ref, dnw_ref, dh_scr, dnext_scr):
        @pl.when(pl.program_id(0) == 0)
        def _():
            for r in (dcw_ref, dcb_ref, ddtb_ref, dalog_ref, dd_ref, dnw_ref, dh_scr, dnext_scr):
                r[...] = jnp.zeros_like(r)

        k = _ssd_consts()

        def one_chunk(jj, carry):
            j = step - 1 - jj
            rows = pl.ds(pl.multiple_of(j * CHUNK, CHUNK), CHUNK)
            chunk(p_blk.at[rows], c_blk.at[rows], yraw_blk.at[rows], hs_blk.at[j], dy_blk.at[rows], dp_blk.at[rows], k,
                  cw_ref, dtb_ref, alog_ref, dvec_ref, nw_ref, dcw_ref, dcb_ref, ddtb_ref, dalog_ref, dd_ref, dnw_ref,
                  dh_scr, dnext_scr)
            return carry

        lax.fori_loop(0, step, one_chunk, 0)

    def chunk(p_ref, c_ref, yraw_ref, hs_ref, dy_ref, dp_ref, k, cw_ref, dtb_ref, alog_ref, dvec_ref, nw_ref,
              dcw_ref, dcb_ref, ddtb_ref, dalog_ref, dd_ref, dnw_ref, dh_scr, dnext_scr):
        s = _ssd_pre(c_ref[...], p_ref[:, S_DT:S_DT + DT_REAL], dtb_ref, alog_ref, dvec_ref, k)
        xs, bm, cm, cs, dt, a = s["xs"], s["bm"], s["cm"], s["cs"], s["dt"], s["a"]
        m0 = k["m0"]
        xdt = xs * s["dtx"]
        hprev = hs_ref[...]
        dh = dh_scr[...]

        nw_v = nw_ref[...]
        yraw = yraw_ref[...]
        z = p_ref[:, S_Z:S_Z + D_SSD]
        sz = _sigmoid(z)
        siluz = z * sz
        yz = yraw * siluz
        dyo = dy_ref[...]
        dyn = dyo * nw_v
        dyz_parts, dnw_parts = [], []
        for g in range(SSD_GROUPS):
            gs = slice(g * GROUP_W, (g + 1) * GROUP_W)
            yzg = yz[:, gs]
            r = lax.rsqrt(jnp.mean(yzg * yzg, axis=1, keepdims=True) + EPS)
            dnw_parts.append(jnp.sum(dyo[:, gs] * yzg * r, axis=0, keepdims=True))
            dyz_parts.append(r * dyn[:, gs] - yzg * (r * r * r) * jnp.mean(dyn[:, gs] * yzg, axis=1, keepdims=True))
        dnw_ref[...] += jnp.concatenate(dnw_parts, axis=1)
        dyz = jnp.concatenate(dyz_parts, axis=1)
        d_y = dyz * siluz
        dp_ref[:, S_Z:S_Z + D_SSD] = (dyz * yraw * (sz * (1.0 + z * (1.0 - sz)))).astype(BF16)
        dd_row = jnp.sum(d_y * xs, axis=0, keepdims=True)
        dxs = d_y * s["ddx"]

        lane_h = _iota((1, SSD_HEADS), 1)
        sub_h = _iota((SSD_HEADS, 1), 0)
        dcs = jnp.zeros((CHUNK, SSD_HEADS), F32)
        dcst2 = jnp.zeros((SSD_HEADS, 128), F32)
        dxdt_parts, db_parts, dc_parts, dhp_parts, yoff_parts, dend_parts, dcd_parts = [], [], [], [], [], [], []
        for g in range(SSD_GROUPS):
            gs = slice(g * GROUP_W, (g + 1) * GROUP_W)
            bg = bm[:, g * 128:(g + 1) * 128]
            cg = cm[:, g * 128:(g + 1) * 128]
            bdup = jnp.concatenate([bg, bg], axis=0)
            cbdup = _dot(cg, bdup, _NT)
            dcb2 = jnp.zeros((CHUNK, 128), F32)
            dxp_parts = []
            for q in range(4):
                pr = g * 4 + q
                h0, h1 = 2 * pr, 2 * pr + 1
                lp = _pair_decay(pr, cs, s["cst2"], k)
                mp = cbdup * lp
                xst = _pair_stack(xdt[:, pr * 128:(pr + 1) * 128], k)
                dyp = d_y[:, pr * 128:(pr + 1) * 128]
                dmp = _dot(dyp, xst, _NT)
                dxst = _dot(mp, dyp, _TN)
                dxp_parts.append(jnp.where(m0, dxst[:CHUNK], dxst[CHUNK:]))
                dcb2 = dcb2 + dmp * lp
                dlm = dmp * mp
                rs0 = jnp.sum(jnp.where(m0, dlm, 0.0), axis=1, keepdims=True)
                rs1 = jnp.sum(jnp.where(m0, 0.0, dlm), axis=1, keepdims=True)
                dcs = dcs + jnp.where(lane_h == h0, rs0, 0.0) + jnp.where(lane_h == h1, rs1, 0.0)
                colsum = jnp.sum(dlm, axis=0, keepdims=True)
                sel = ((sub_h == h0) & m0) | ((sub_h == h1) & jnp.logical_not(m0))
                dcst2 = dcst2 - jnp.where(sel, colsum, 0.0)
            dcg = _dot(dcb2, bdup)
            dbdup = _dot(dcb2, cg, _TN)
            dbg = dbdup[:CHUNK] + dbdup[CHUNK:]
            hp_g = hprev[:, gs]
            zoff = _dot(cg, hp_g)
            dzo = d_y[:, gs] * s["ecx"][:, gs]
            dcg = dcg + _dot(dzo, hp_g, _NT)
            dh_g = dh[:, gs]
            dhp_parts.append(_dot(cg, dzo, _TN) + dh_g * s["cdx"][:, gs])
            dcd_parts.append(jnp.sum(dh_g * hp_g, axis=0, keepdims=True))
            wg = xdt[:, gs] * s["dex"][:, gs]
            dbg = dbg + _dot(wg, dh_g, _NT)
            dwg = _dot(bg, dh_g)
            dxdt_parts.append(jnp.concatenate(dxp_parts, axis=1) + dwg * s["dex"][:, gs])
            dend_g = dwg * wg
            dend_parts.append(jnp.sum(dend_g, axis=0, keepdims=True))
            yoff_parts.append(dzo * zoff - dend_g)
            db_parts.append(dbg)
            dc_parts.append(dcg)
        dh_scr[...] = jnp.concatenate(dhp_parts, axis=1)
        dxdt = jnp.concatenate(dxdt_parts, axis=1)
        sums = _dot01(jnp.concatenate([jnp.concatenate(yoff_parts, axis=1), dxdt * xs], axis=0), k["e"], _NT)
        rows8 = jnp.concatenate([jnp.broadcast_to(jnp.concatenate(r, axis=1), (8, D_SSD))
                                 for r in (dcd_parts, [dd_row], dend_parts)], axis=0)
        small = _dot01(rows8, k["e"], _NT)
        dd_ref[...] += small[8:9]
        dcs_last = small[0:1] * s["cdec"] + small[16:17]
        hi, lo = _split(dcst2)
        dcs = (dcs + sums[0:CHUNK]
               + lax.dot_general(k["fold"], hi, _NT, preferred_element_type=F32)
               + lax.dot_general(k["fold"], lo, _NT, preferred_element_type=F32)
               + jnp.where(_iota((CHUNK, 1), 0) == CHUNK - 1, dcs_last, 0.0))
        dda = _dot_hi(k["tril"], dcs, _TN)
        ddt = dda * a + sums[CHUNK:2 * CHUNK]
        dalog_ref[...] += jnp.sum(dda * dt, axis=0, keepdims=True) * a
        dxs = dxs + dxdt * s["dtx"]
        draw = ddt * _sigmoid(s["dtp"])
        ddtb_ref[...] += jnp.sum(draw, axis=0, keepdims=True)
        dp_ref[:, S_DT:] = jnp.zeros((CHUNK, W_SSD - S_DT), BF16)
        dp_ref[:, S_DT:S_DT + DT_REAL] = draw.astype(BF16)
        dxbc = jnp.concatenate([dxs] + db_parts + dc_parts, axis=1)
        sg, c = s["sg"], s["c"]
        dc = dxbc * (sg * (1.0 + c * (1.0 - sg)))
        dx, dcw, dcb = _conv_bwd(dc, dnext_scr[...], p_ref[:, S_XBC:S_DT], cw_ref[...], CHUNK)
        dp_ref[:, S_XBC:S_DT] = dx.astype(BF16)
        dcw_ref[...] += dcw
        dcb_ref[...] += dcb
        dnext_scr[...] = dc[0:8]

    rev = lambda i: nb - 1 - i
    hv = pl.BlockSpec((1, DT_REAL), lambda i: (0, 0))
    cws = pl.BlockSpec((4, D_XBC), lambda i: (0, 0))
    cbs = pl.BlockSpec((1, D_XBC), lambda i: (0, 0))
    nws = pl.BlockSpec((1, D_SSD), lambda i: (0, 0))
    wide = pl.BlockSpec((rows_blk, D_SSD), lambda i: (rev(i), 0))
    hshape = jax.ShapeDtypeStruct((1, DT_REAL), F32)
    return pl.pallas_call(
        body, name=name, grid=(nb,),
        in_specs=[pl.BlockSpec((rows_blk, W_SSD), lambda i: (rev(i), 0)),
                  pl.BlockSpec((rows_blk, D_XBC), lambda i: (rev(i), 0)),
                  wide, pl.BlockSpec((step, SSD_STATE, D_SSD), lambda i: (rev(i), 0, 0)), wide,
                  cws, hv, hv, hv, nws],
        out_specs=[pl.BlockSpec((rows_blk, W_SSD), lambda i: (rev(i), 0)), cws, cbs, hv, hv, hv, nws],
        out_shape=[jax.ShapeDtypeStruct((t, W_SSD), BF16), jax.ShapeDtypeStruct((4, D_XBC), F32),
                   jax.ShapeDtypeStruct((1, D_XBC), F32), hshape, hshape, hshape,
                   jax.ShapeDtypeStruct((1, D_SSD), F32)],
        scratch_shapes=[pltpu.VMEM((SSD_STATE, D_SSD), F32), pltpu.VMEM((8, D_XBC), F32)],
        compiler_params=_cp("arbitrary"),
    )(p, c, yraw, hs, dy, cw, dtb, alog, dvec, nw)


def _loss_head(y, target, name, tb=512):
    t = y.shape[0]
    tb = min(tb, t)

    def body(y_ref, t_ref, dy_ref, l_ref):
        @pl.when(pl.program_id(0) == 0)
        def _():
            l_ref[...] = jnp.zeros_like(l_ref)

        e = y_ref[...] - t_ref[...]
        dy_ref[...] = e * (1.0 / D)
        l_ref[...] += jnp.sum(jnp.sum(e * e, axis=1, keepdims=True), axis=0, keepdims=True) * (0.5 / D)

    row = pl.BlockSpec((tb, D), lambda i: (i, 0))
    return pl.pallas_call(
        body, name=name, grid=(t // tb,), in_specs=[row, row],
        out_specs=[row, pl.BlockSpec((8, 128), lambda i: (0, 0))],
        out_shape=[jax.ShapeDtypeStruct((t, D), F32), jax.ShapeDtypeStruct((8, 128), F32)],
        compiler_params=_cp("arbitrary"),
    )(y, target)


def _adamw(slots, w, m, v, name, tb):
    nl = len(slots)
    ns, r, c = slots[0].shape
    assert r % tb == 0 and w.shape == (nl, r, c), (r, tb, w.shape)

    def body(*refs):
        s_refs = refs[:nl]
        w_ref, m_ref, v_ref, g_ref, d_ref, m2_ref, v2_ref = refs[nl:]

        def total(ref):
            acc = ref[0].astype(F32)
            for j in range(1, ns):
                acc = acc + ref[j].astype(F32)
            return acc

        g = total(s_refs[0])
        for layer in range(1, nl):
            g = jnp.where(pl.program_id(0) == layer, total(s_refs[layer]), g)
        m2 = ADAM_B1 * m_ref[...] + (1.0 - ADAM_B1) * g
        v2 = ADAM_B2 * v_ref[...] + (1.0 - ADAM_B2) * (g * g)
        m_hat = m2 / (1.0 - ADAM_B1 ** ADAM_STEP)
        v_hat = v2 / (1.0 - ADAM_B2 ** ADAM_STEP)
        g_ref[...] = g
        d_ref[...] = -ADAM_LR * (m_hat / (jnp.sqrt(v_hat) + ADAM_EPS) + ADAM_WD * w_ref[...])
        m2_ref[...] = m2
        v2_ref[...] = v2

    def slot_spec(layer):
        return pl.BlockSpec((ns, tb, c), lambda l, i: (0, jnp.where(l == layer, i, 0), 0))

    row = pl.BlockSpec((None, tb, c), lambda l, i: (l, i, 0))
    shp = jax.ShapeDtypeStruct((nl, r, c), F32)
    return pl.pallas_call(
        body, name=name, grid=(nl, r // tb),
        in_specs=[slot_spec(layer) for layer in range(nl)] + [row, row, row],
        out_specs=[row, row, row, row], out_shape=[shp, shp, shp, shp], compiler_params=_cp("arbitrary", "arbitrary"),
    )(*slots, w, m, v)


def _pair_sum(own, got, name, out_dtype, tb):
    nj, _, r, c = own.shape
    mc = lax.axis_index("c")

    def body(mc_ref, a_ref, b_ref, o_ref):
        del mc_ref
        o_ref[...] = (a_ref[...] + b_ref[...]).astype(out_dtype)

    return pl.pallas_call(
        body, name=name,
        grid_spec=pltpu.PrefetchScalarGridSpec(
            num_scalar_prefetch=1, grid=(nj, r // tb),
            in_specs=[pl.BlockSpec((None, None, tb, c), lambda j, i, mc_ref: (j, mc_ref[0], i, 0)),
                      pl.BlockSpec((None, tb, c), lambda j, i, mc_ref: (j, i, 0))],
            out_specs=pl.BlockSpec((None, tb, c), lambda j, i, mc_ref: (j, i, 0))),
        out_shape=jax.ShapeDtypeStruct((nj, r, c), out_dtype), compiler_params=_cp("parallel", "parallel"),
    )(jnp.reshape(mc, (1,)).astype(jnp.int32), own, got)


def _slot_sum(slots, name):
    ns, r, c = slots.shape

    def body(s_ref, o_ref):
        g = s_ref[0]
        for j in range(1, ns):
            g = g + s_ref[j]
        o_ref[...] = g

    return pl.pallas_call(body, name=name, out_shape=jax.ShapeDtypeStruct((r, c), F32))(slots)


def _position():
    return lax.axis_index("x"), lax.axis_index("y"), lax.axis_index("c")


def _comm(exchange, peers, xs, out_shapes, sems, name, collective_id):
    n = len(xs)
    if collective_id is None:
        def body(*refs):
            exchange(refs[:n], refs[n:n + len(out_shapes)], *refs[n + len(out_shapes):])

        return pl.pallas_call(body, name=name, in_specs=[ANY] * n, out_specs=[ANY] * len(out_shapes),
                              out_shape=out_shapes, scratch_shapes=sems)(*xs)
    def launch(*refs):
        barrier = pltpu.get_barrier_semaphore()
        to = peers(*_position())
        for peer in to:
            pl.semaphore_signal(barrier, inc=1, device_id=peer, device_id_type=MESH)
        pl.semaphore_wait(barrier, len(to))
        exchange(refs[:n], refs[n:n + len(out_shapes)], *refs[n + len(out_shapes):])

    return pl.kernel(launch, out_type=out_shapes, mesh=plsc.ScalarSubcoreMesh(axis_name="seq", num_cores=1), name=name,
                     scratch_types=sems, compiler_params=pltpu.CompilerParams(collective_id=collective_id))(*xs)


def _all_gather(xs, name, collective_id=None):
    n = len(xs)
    return _comm(_gather_body, lambda x, y, c: [(x, y, 1 - c), (1 - x, y, c), (x, 1 - y, c), (1 - x, 1 - y, c)], xs,
                 [jax.ShapeDtypeStruct((N_DEV,) + x.shape, x.dtype) for x in xs],
                 [pltpu.SemaphoreType.DMA((n, 7)), pltpu.SemaphoreType.DMA((n, 7)), pltpu.SemaphoreType.DMA((n,))],
                 name, collective_id)


def _gather_body(x_refs, out_refs, send_sems, recv_sems, local_sems):
    n = len(x_refs)
    mx, my, mc = _position()
    me, sibling = (mx, my, mc), (mx, my, 1 - mc)
    chips = [(1 - mx, my), (mx, 1 - my), (1 - mx, 1 - my)]

    def copy(a, k, block, to, own=False):
        dst = out_refs[a].at[4 * block[0] + 2 * block[1] + block[2]]
        return pltpu.make_async_remote_copy(
            src_ref=x_refs[a] if own else dst, dst_ref=dst,
            send_sem=send_sems.at[a, k], recv_sem=recv_sems.at[a, k], device_id=to, device_id_type=MESH)

    mine = [pltpu.make_async_copy(x_refs[a], out_refs[a].at[4 * mx + 2 * my + mc], local_sems.at[a]) for a in range(n)]
    first = [copy(a, 1 + j, me, (*chip, mc), own=True) for j, chip in enumerate(chips) for a in range(n)]
    first += [copy(a, 0, me, sibling, own=True) for a in range(n)]
    for cp in first + mine:
        cp.start()
    passed = []
    for j, chip in enumerate(chips):
        for a in range(n):
            copy(a, 1 + j, (*chip, mc), me).wait_recv()
            passed.append(copy(a, 4 + j, (*chip, mc), sibling))
            passed[-1].start()
    for a in range(n):
        copy(a, 0, sibling, me).wait_recv()
    for j, chip in enumerate(chips):
        for a in range(n):
            copy(a, 4 + j, (*chip, 1 - mc), me).wait_recv()
    for cp in first + passed:
        cp.wait_send()
    for cp in mine:
        cp.wait()


def _exchange_sibling(gs, name, collective_id=None):
    n = len(gs)

    def exchange(g_refs, r_refs, send_sems, recv_sems):
        mx, my, mc = _position()
        cps = [pltpu.make_async_remote_copy(src_ref=g_refs[a].at[:, 1 - mc], dst_ref=r_refs[a],
                                            send_sem=send_sems.at[a], recv_sem=recv_sems.at[a],
                                            device_id=(mx, my, 1 - mc), device_id_type=MESH) for a in range(n)]
        for cp in cps:
            cp.start()
        for cp in cps:
            cp.wait()

    return _comm(exchange, lambda x, y, c: [(x, y, 1 - c)], gs,
                 [jax.ShapeDtypeStruct(g.shape[:1] + g.shape[2:], g.dtype) for g in gs],
                 [pltpu.SemaphoreType.DMA((n,)), pltpu.SemaphoreType.DMA((n,))], name, collective_id)


def _exchange_chips(ss, name, collective_id=None):
    n = len(ss)

    def exchange(s_refs, r_refs, send_sems, recv_sems, local_sems):
        mx, my, mc = _position()
        my_chip = 2 * mx + my
        chips = [(1 - mx, my), (mx, 1 - my), (1 - mx, 1 - my)]

        def copy(a, k, to_slot):
            px, py = chips[k]
            return pltpu.make_async_remote_copy(
                src_ref=s_refs[a].at[2 * px + py], dst_ref=r_refs[a].at[to_slot], send_sem=send_sems.at[a, k],
                recv_sem=recv_sems.at[a, k], device_id=(px, py, mc), device_id_type=MESH)

        sends = [copy(a, k, my_chip) for k in range(3) for a in range(n)]
        local = [pltpu.make_async_copy(s_refs[a].at[my_chip], r_refs[a].at[my_chip], local_sems.at[a])
                 for a in range(n)]
        for cp in sends + local:
            cp.start()
        for k in range(3):
            px, py = chips[k]
            for a in range(n):
                copy(a, k, 2 * px + py).wait_recv()
        for cp in sends:
            cp.wait_send()
        for cp in local:
            cp.wait()

    return _comm(exchange, lambda x, y, c: [(1 - x, y, c), (x, 1 - y, c), (1 - x, 1 - y, c)], ss,
                 [jax.ShapeDtypeStruct(s.shape, s.dtype) for s in ss],
                 [pltpu.SemaphoreType.DMA((n, 3)), pltpu.SemaphoreType.DMA((n, 3)), pltpu.SemaphoreType.DMA((n,))],
                 name, collective_id)


def _cols_concat(g, name, tb=128):
    _, k_dim, n = g.shape

    def body(g_ref, o_ref):
        o_ref[...] = jnp.concatenate([g_ref[d] for d in range(N_DEV)], axis=1)

    return pl.pallas_call(
        body, name=name, grid=(k_dim // tb,),
        in_specs=[pl.BlockSpec((N_DEV, tb, n), lambda i: (0, i, 0))],
        out_specs=pl.BlockSpec((tb, N_DEV * n), lambda i: (i, 0)),
        out_shape=jax.ShapeDtypeStruct((k_dim, N_DEV * n), g.dtype), compiler_params=_cp("parallel"),
    )(g)


def _cols_split(parts, name, tb=128):
    k_dim = parts[0].shape[0]
    n = sum(p.shape[1] for p in parts) // N_DEV

    def body(*refs):
        full = jnp.concatenate([r[...] for r in refs[:-1]], axis=1)
        for d in range(N_DEV):
            refs[-1][d] = full[:, d * n:(d + 1) * n]

    return pl.pallas_call(
        body, name=name, grid=(k_dim // tb,),
        in_specs=[pl.BlockSpec((tb, p.shape[1]), lambda i: (i, 0)) for p in parts],
        out_specs=pl.BlockSpec((N_DEV, tb, n), lambda i: (0, i, 0)),
        out_shape=jax.ShapeDtypeStruct((N_DEV, k_dim, n), parts[0].dtype), compiler_params=_cp("parallel"),
    )(*parts)


_Q0, _GL0 = 7200, 8224
N_SHARD_IN = N_IN // N_DEV


def _w_in_regions(g, name, tb=128):
    def body(g_ref, ssd_ref, lru_ref, q_ref, gl_ref):
        full = jnp.concatenate([g_ref[d] for d in range(N_DEV)], axis=1)
        lru_ref[...] = full[:, 0:2 * D]
        ssd_ref[:, :S_DT] = full[:, 2 * D:2 * D + S_DT]
        ssd_ref[:, S_DT:] = jnp.zeros((tb, W_SSD - S_DT), g.dtype)
        ssd_ref[:, S_DT:S_DT + DT_REAL] = full[:, 2 * D + S_DT:_Q0]
        q_ref[...] = full[:, _Q0:_GL0]
        gl_ref[...] = full[:, _GL0:N_IN]

    widths = (W_SSD, 2 * D, D, 3 * D)
    return pl.pallas_call(
        body, name=name, grid=(D // tb,),
        in_specs=[pl.BlockSpec((N_DEV, tb, N_SHARD_IN), lambda i: (0, i, 0))],
        out_specs=[pl.BlockSpec((tb, wd), lambda i: (i, 0)) for wd in widths],
        out_shape=[jax.ShapeDtypeStruct((D, wd), g.dtype) for wd in widths], compiler_params=_cp("parallel"),
    )(g)


def _w_in_shards(dssd, dlru, dq, dgl, name, tb=128):
    def body(ssd_ref, lru_ref, q_ref, gl_ref, o_ref):
        full = jnp.concatenate([lru_ref[...], ssd_ref[:, :S_DT + DT_REAL], q_ref[...], gl_ref[...]], axis=1)
        for d in range(N_DEV):
            o_ref[d] = full[:, d * N_SHARD_IN:(d + 1) * N_SHARD_IN]

    return pl.pallas_call(
        body, name=name, grid=(D // tb,),
        in_specs=[pl.BlockSpec((tb, a.shape[1]), lambda i: (i, 0)) for a in (dssd, dlru, dq, dgl)],
        out_specs=pl.BlockSpec((N_DEV, tb, N_SHARD_IN), lambda i: (0, i, 0)),
        out_shape=jax.ShapeDtypeStruct((N_DEV, D, N_SHARD_IN), F32), compiler_params=_cp("parallel"),
    )(dssd, dlru, dq, dgl)


_BIG = (("w_in", "col", (1024, 1412)), ("mem_w_kv", "col", (1024, 256)), ("w_br_lru", "row", (128, 1024)),
        ("w_br_ssd", "row", (256, 1024)), ("w_br_xa", "row", (128, 1024)), ("w_out", "row", (128, 1024)),
        ("ffn_w_in", "row", (704, 1024)), ("ffn_w_down", "row", (352, 1024)))
_TRANSPOSED = ("ffn_w_in",)
_SMALL = (("b_gate", (3, 128)), ("lru_conv_w", (4, 128)), ("ssd_conv_w", (4, 384)))
_REP = (("lru_conv_b", (1024,)), ("lru_w_a", (8, 128, 128)), ("lru_b_a", (1024,)), ("lru_w_i", (8, 128, 128)),
        ("lru_b_i", (1024,)), ("lru_lambda", (1024,)), ("ssd_conv_b", (3072,)), ("ssd_dt_bias", (32,)),
        ("ssd_a_log", (32,)), ("ssd_d", (32,)), ("ssd_norm_w", (2048,)), ("ln1_g", (1024,)), ("ln1_b", (1024,)),
        ("ln2_g", (1024,)), ("ln2_b", (1024,)))
_ORDER = ("w_in", "b_gate", "lru_conv_w", "lru_conv_b", "lru_w_a", "lru_b_a", "lru_w_i", "lru_b_i", "lru_lambda",
          "ssd_conv_w", "ssd_conv_b", "ssd_dt_bias", "ssd_a_log", "ssd_d", "ssd_norm_w", "mem_w_kv", "w_br_lru",
          "w_br_ssd", "w_br_xa", "w_out", "ln1_g", "ln1_b", "ffn_w_in", "ffn_w_down", "ln2_g", "ln2_b")

LANES = 1024
N_SMALL = sum(DEPTH * s[0] * s[1] for _, s in _SMALL)
R_SMALL = 8
N_REP = sum(DEPTH * math.prod(s) for _, s in _REP)
R_REP = 68
R_SM = R_SMALL + R_REP + 4
R_TAIL = R_SMALL + N_DEV * R_REP
TB_TAIL = 184
assert N_SMALL <= R_SMALL * LANES and N_REP + 1 <= N_DEV * R_REP * LANES


def _rows(flat, rows):
    return jnp.pad(flat, (0, rows * LANES - flat.shape[0])).reshape(rows, LANES)


def _rowblk(a, cap):
    return max(b for b in range(16, cap + 1, 16) if a % b == 0)


def _pack_tail(d):
    small = jnp.concatenate([d[n].reshape(-1) for n, _ in _SMALL])
    rep = jnp.concatenate([d[n].reshape(-1) for n, _ in _REP])
    return jnp.concatenate([_rows(small, R_SMALL), _rows(rep, N_DEV * R_REP)], axis=0)


def _unpack_tail(a):
    out, o = {}, 0
    flat = a[:R_SMALL].reshape(-1)
    for n, s in _SMALL:
        k = DEPTH * math.prod(s)
        out[n] = flat[o:o + k].reshape((DEPTH,) + s)
        o += k
    flat, o = a[R_SMALL:].reshape(-1), 0
    for n, s in _REP:
        k = DEPTH * math.prod(s)
        out[n] = flat[o:o + k].reshape((DEPTH,) + s)
        o += k
    return out


def _by_dest(g):
    g = g.reshape(g.shape[:-1] + (N_DEV, g.shape[-1] // N_DEV))
    return jnp.moveaxis(g, -2, 0).reshape(N_DEV, -1)


def _from_stack(st):
    st = jnp.moveaxis(st, 0, -2)
    return st.reshape(st.shape[:-2] + (st.shape[-2] * st.shape[-1],))


def _layer_fwd(x, xb, mem, w, l):
    nm = lambda s: f"{s}_l{l}"
    wi = w["wi"]
    row = lambda v: v.reshape(1, -1)
    s = dict(x=x, xb=xb, wi=wi)
    s["p_ssd"] = _mm(xb, wi["ssd"], name=nm("proj_ssd"))
    s["p_lru"] = _mm(xb, wi["lru"], name=nm("proj_lru"))
    s["p_q"] = _mm(xb, wi["q"], out_dtype=BF16, name=nm("proj_q"))
    s["p_gl"] = _mm(xb, wi["gl"], out_dtype=BF16, name=nm("proj_gl"))
    s["lru_par"] = (w["lru_conv_w"], row(w["lru_conv_b"]), w["lru_w_a"], row(w["lru_b_a"]), w["lru_w_i"],
                    row(w["lru_b_i"]), row(w["lru_lambda"]))
    s["y_lru"], s["h"], s["xc"] = _lru_fwd(s["p_lru"], *s["lru_par"], name=nm("lru_fwd"))
    s["ssd_par"] = (w["ssd_conv_w"], row(w["ssd_conv_b"]), row(w["ssd_dt_bias"]), row(w["ssd_a_log"]),
                    row(w["ssd_d"]), row(w["ssd_norm_w"]))
    s["y_ssd"], s["yraw"], s["hs"], s["c_ssd"] = _ssd_fwd(s["p_ssd"], *s["ssd_par"], name=nm("ssd_fwd"))
    s["kv"] = _mm(mem, w["mem_w_kv"], name=nm("kv"))
    s["y_xa"] = _xa_fwd(s["p_q"], s["kv"], name=nm("xa_fwd"))
    s["b1"] = _mm(s["y_lru"], w["w_br_lru"], out_dtype=BF16, name=nm("br_lru"))
    s["b2"] = _mm(s["y_ssd"], w["w_br_ssd"], out_dtype=BF16, name=nm("br_ssd"))
    s["b3"] = _mm(s["y_xa"], w["w_br_xa"], out_dtype=BF16, name=nm("br_xa"))
    s["bg"] = row(w["b_gate"])
    s["merged"] = _merge_fwd(s["p_gl"], s["bg"], s["b1"], s["b2"], s["b3"], name=nm("merge_fwd"))
    s["mix"] = _mm(s["merged"], w["w_out"], name=nm("out_proj"))
    s["x1"], s["x1b"] = _ln_fwd(x, s["mix"], row(w["ln1_g"]), row(w["ln1_b"]), name=nm("ln1_fwd"))
    s["gate"], s["up"], s["act"] = _ffn_in_swiglu(s["x1b"], w["ffn_w_in"], name=nm("ffn_in"))
    s["f"] = _mm(s["act"], w["ffn_w_down"], name=nm("ffn_down"))
    s["x2"], s["x2b"] = _ln_fwd(s["x1"], s["f"], row(w["ln2_g"]), row(w["ln2_b"]), name=nm("ln2_fwd"))
    return s


def _layer_bwd(s, mem, w, dxo, l, hooks=None):
    nm = lambda t: f"{t}_l{l}"
    g = {}
    hook = lambda stage, t: hooks[stage](t, g) if hooks and stage in hooks else t
    row = lambda v: v.reshape(1, -1)
    slabs = lambda a: a.reshape(N_DEV, a.shape[0] // N_DEV, a.shape[1])
    du2, dg, db = _ln_bwd(s["x1"], s["f"], dxo, row(w["ln2_g"]), name=nm("ln2_bwd"))
    g["ln2_g"], g["ln2_b"] = dg[0], db[0]
    dgate, dup = _d_swiglu(du2, w["ffn_w_down"], s["gate"], s["up"], name=nm("d_swiglu"))
    g["ffn_w_down"] = slabs(_mm(s["act"], du2, ta=True, name=nm("dw_ffn_down")))
    dx1 = _mm(dgate, w["ffn_w_in"][:D_FF], add=du2, add_scale=ALPHA, name=nm("d_x1_gate"))
    dx1 = _mm(dup, w["ffn_w_in"][D_FF:], add=dx1, name=nm("d_x1_up"))
    g["ffn_w_in"] = slabs(jnp.concatenate([_mm(dgate, s["x1b"], ta=True, name=nm("dw_ffn_gate")),
                                           _mm(dup, s["x1b"], ta=True, name=nm("dw_ffn_up"))], axis=0))
    du1, dg, db = _ln_bwd(s["x"], s["mix"], dx1, row(w["ln1_g"]), name=nm("ln1_bwd"))
    g["ln1_g"], g["ln1_b"] = dg[0], db[0]
    dmerged = hook("mid", _mm(du1, w["w_out"], tb=True, name=nm("d_merged")))
    g["w_out"] = slabs(_mm(s["merged"], du1, ta=True, name=nm("dw_out")))
    dp_gl, d1, d2, d3, dbg = _merge_bwd(s["p_gl"], s["bg"], s["b1"], s["b2"], s["b3"], dmerged, name=nm("merge_bwd"))
    g["b_gate"] = dbg.reshape(3, D)
    dy_lru = _mm(d1, w["w_br_lru"], tb=True, name=nm("d_y_lru"))
    g["w_br_lru"] = slabs(_mm(s["y_lru"], d1, ta=True, name=nm("dw_br_lru")))
    dy_ssd = _mm(d2, w["w_br_ssd"], tb=True, name=nm("d_y_ssd"))
    g["w_br_ssd"] = slabs(_mm(s["y_ssd"], d2, ta=True, name=nm("dw_br_ssd")))
    dy_xa = _mm(d3, w["w_br_xa"], tb=True, out_dtype=BF16, name=nm("d_y_xa"))
    g["w_br_xa"] = slabs(_mm(s["y_xa"], d3, ta=True, name=nm("dw_br_xa")))
    dp_q, dkv = _xa_bwd(s["p_q"], s["kv"], dy_xa, name=nm("xa_bwd"))
    g["mem_w_kv"] = _mm(mem, dkv, ta=True, split_n=2 * D // N_DEV, name=nm("dw_kv"))
    dy_ssd = hook("branches", dy_ssd)
    ssd_cw, _, *ssd_rest = s["ssd_par"]
    dp_ssd, dcw, dcb, ddtb, dalog, dd, dnw = _ssd_bwd(s["p_ssd"], s["c_ssd"], s["yraw"], s["hs"], dy_ssd, ssd_cw,
                                                      *ssd_rest, name=nm("ssd_bwd"))
    g["ssd_conv_w"], g["ssd_conv_b"], g["ssd_dt_bias"] = dcw, dcb[0], ddtb[0]
    g["ssd_a_log"], g["ssd_d"], g["ssd_norm_w"] = dalog[0], dd[0], dnw[0]
    dp_ssd = hook("ssd", dp_ssd)
    lru_cw, _, *lru_rest = s["lru_par"]
    dp_lru, dcw, dcb, dwa, dba, dwi, dbi, dlam = _lru_bwd(s["p_lru"], s["xc"], s["h"], dy_lru, lru_cw, *lru_rest,
                                                          name=nm("lru_bwd"))
    g["lru_conv_w"], g["lru_conv_b"], g["lru_w_a"], g["lru_b_a"] = dcw, dcb[0], dwa, dba[0]
    g["lru_w_i"], g["lru_b_i"], g["lru_lambda"] = dwi, dbi[0], dlam[0]
    wi, x = s["wi"], s["xb"]
    g["w_in"] = _w_in_shards(_mm(x, dp_ssd, ta=True, name=nm("dw_in_ssd")), _mm(x, dp_lru, ta=True, name=nm("dw_in_lru")),
                             _mm(x, dp_q, ta=True, name=nm("dw_in_q")), _mm(x, dp_gl, ta=True, name=nm("dw_in_gl")),
                             name=nm("dw_in_shards"))
    dp_ssd = hook("weights", dp_ssd)
    dx = _mm(dp_ssd, wi["ssd"], tb=True, add=du1, add_scale=ALPHA, name=nm("dx_ssd"))
    dx = hook("dx", _mm(dp_lru, wi["lru"], tb=True, add=dx, name=nm("dx_lru")))
    dx = _mm(dp_q, wi["q"], tb=True, add=dx, name=nm("dx_q"))
    dx = _mm(dp_gl, wi["gl"], tb=True, add=dx, name=nm("dx_gl"))
    return dx, g


def _local_step(x, mem, target, layers, hooks=None):
    saved, xb = [], x.astype(BF16)
    for l in range(DEPTH):
        saved.append(_layer_fwd(x, xb, mem, layers[l], l))
        x, xb = saved[-1]["x2"], saved[-1]["x2b"]
    dx, loss = _loss_head(x, target, name="loss_head")
    if hooks and "loss" in hooks[-1]:
        loss = hooks[-1]["loss"](loss, None)
    grads = [None] * DEPTH
    for l in reversed(range(DEPTH)):
        dx, grads[l] = _layer_bwd(saved[l], mem, layers[l], dx, l, hooks[l] if hooks else None)
    return loss, dx, grads


def kernel(x, mem, w_in, b_gate, lru_conv_w, lru_conv_b, lru_w_a, lru_b_a, lru_w_i, lru_b_i, lru_lambda, ssd_conv_w, ssd_conv_b, ssd_dt_bias, ssd_a_log, ssd_d, ssd_norm_w, mem_w_kv, w_br_lru, w_br_ssd, w_br_xa, w_out, ln1_g, ln1_b, ffn_w_in, ffn_w_down, ln2_g, ln2_b, loss_target, m_w_in, m_b_gate, m_lru_conv_w, m_lru_conv_b, m_lru_w_a, m_lru_b_a, m_lru_w_i, m_lru_b_i, m_lru_lambda, m_ssd_conv_w, m_ssd_conv_b, m_ssd_dt_bias, m_ssd_a_log, m_ssd_d, m_ssd_norm_w, m_mem_w_kv, m_w_br_lru, m_w_br_ssd, m_w_br_xa, m_w_out, m_ln1_g, m_ln1_b, m_ffn_w_in, m_ffn_w_down, m_ln2_g, m_ln2_b, v_w_in, v_b_gate, v_lru_conv_w, v_lru_conv_b, v_lru_w_a, v_lru_b_a, v_lru_w_i, v_lru_b_i, v_lru_lambda, v_ssd_conv_w, v_ssd_conv_b, v_ssd_dt_bias, v_ssd_a_log, v_ssd_d, v_ssd_norm_w, v_mem_w_kv, v_w_br_lru, v_w_br_ssd, v_w_br_xa, v_w_out, v_ln1_g, v_ln1_b, v_ffn_w_in, v_ffn_w_down, v_ln2_g, v_ln2_b):
    local = dict(locals())
    w = {n: local[n] for n in _ORDER}
    m = {n: local["m_" + n] for n in _ORDER}
    v = {n: local["v_" + n] for n in _ORDER}
    for n in _TRANSPOSED:
        w[n], m[n], v[n] = (jnp.swapaxes(a, 1, 2) for a in (w[n], m[n], v[n]))

    big = [n for n, _, _ in _BIG]
    kinds = {n: kind for n, kind, _ in _BIG}

    small = _rows(jnp.concatenate([w[n].reshape(-1) for n, _ in _SMALL]), R_SMALL)
    first = _all_gather([w["w_in"][0].astype(BF16), small], name="gather_w_in_l0")
    rest, later, _ = lax.optimization_barrier(([w[n][0].astype(BF16) for n in big[1:]],
                                               [w[n][1].astype(BF16) for n in big], first[-1]))
    rest = _all_gather(rest, "gather_weights_l0", collective_id=1)
    later = _all_gather(later, "gather_weights_l1", collective_id=4)
    stacks = [dict(zip(big, [first[0], *rest])), dict(zip(big, later))]
    small_all, o, small_full = first[-1].reshape(N_DEV, R_SMALL * LANES), 0, {}
    for n, s in _SMALL:
        k = DEPTH * s[0] * s[1]
        small_full[n] = _from_stack(small_all[:, o:o + k].reshape((N_DEV, DEPTH) + s))
        o += k
    layers = []
    for l in range(DEPTH):
        lw = {n: w[n][l] for n, _ in _REP}
        lw.update({n: small_full[n][l] for n, _ in _SMALL})
        lw["wi"] = dict(zip(("ssd", "lru", "q", "gl"), _w_in_regions(stacks[l]["w_in"], name=f"w_in_regions_l{l}")))
        for n in big[1:]:
            if kinds[n] == "col":
                lw[n] = _cols_concat(stacks[l][n], name=f"full_{n}_l{l}")
            else:
                lw[n] = stacks[l][n].reshape(-1, stacks[l][n].shape[-1])
        layers.append(lw)

    by_dest = lambda a: a.reshape((4, 2) + a.shape[1:])
    slots, pending, last_layer = {}, {}, {}
    queue = [stacks[1]["w_out"]]

    def after_last(operands):
        operands, _ = lax.optimization_barrier((list(operands), queue[-1]))
        return operands

    def start(tag, collective_id, names_and_grads):
        names, owns = zip(*names_and_grads)
        gots = _exchange_sibling(after_last(owns), name=f"reduce_cores_{tag}", collective_id=collective_id)
        queue.append(gots[0])
        pending[tag] = (names, owns, gots)

    def finish(tag, collective_id, t):
        names, owns, gots = pending.pop(tag)
        t, gots = lax.optimization_barrier((t, gots))
        sums = [_pair_sum(own, got, name=f"pair_sum_{tag}_{n}", out_dtype=F32 if n == "tail" else BF16,
                          tb=R_SM if n == "tail" else _rowblk(own.shape[2], 256))
                for n, own, got in zip(names, owns, gots)]
        t, sums = lax.optimization_barrier((t, sums))
        got = _exchange_chips(sums, name=f"reduce_chips_{tag}", collective_id=collective_id)
        queue.append(got[0])
        slots.update({(tag, n): s for n, s in zip(names, got)})
        return t

    def tail_of(g0):
        stacked = {n: jnp.stack([g0[n], last_layer[n]]) for n in [s[0] for s in _SMALL + _REP]}
        sm = jnp.concatenate([_by_dest(stacked[n]) for n, _ in _SMALL], axis=1)
        sm = jnp.pad(sm, ((0, 0), (0, R_SMALL * LANES - sm.shape[1])))
        rep = jnp.concatenate([stacked[n].reshape(-1) for n, _ in _REP] + [last_layer["loss"][0, :1]])
        rep = jnp.pad(rep, (0, N_DEV * R_REP * LANES - rep.shape[0])).reshape(N_DEV, R_REP * LANES)
        tail = jnp.concatenate([sm, rep, jnp.zeros((N_DEV, (R_SM - R_SMALL - R_REP) * LANES), F32)], axis=1)
        return tail.reshape(4, 2, R_SM, LANES)

    def weights_l1(t, g):
        last_layer.update(g)
        start("l1", 2, [(n, by_dest(g[n])) for n in big])
        return t

    def branches_l0(t, g):
        start("l0a", 5, [(n, by_dest(g[n])) for n in big[1:]])
        return t

    def weights_l0(t, g):
        start("l0b", 7, [("w_in", by_dest(g["w_in"])), ("tail", tail_of(g))])
        return t

    hooks = [{"branches": branches_l0, "ssd": lambda t, g: finish("l0a", 6, t), "weights": weights_l0,
              "dx": lambda t, g: finish("l0b", 8, t)},
             {"weights": weights_l1, "dx": lambda t, g: finish("l1", 3, t),
              "loss": lambda t, g: last_layer.setdefault("loss", t)}]
    _, dx, grads = _local_step(x[0], mem[0], loss_target[0], layers, hooks)

    res = {}
    for n in big:
        tb = _rowblk(w[n].shape[1], 128 if w[n].shape[2] > LANES else 256)
        res[n] = _adamw([slots["l0b" if n == "w_in" else "l0a", n], slots["l1", n]], w[n], m[n], v[n],
                        name=f"adamw_{n}", tb=tb)
    tail_sum = _slot_sum(slots["l0b", "tail"], name="sum_tail")
    rep_all = _all_gather([tail_sum[R_SMALL:R_SMALL + R_REP]], name="gather_replicated")[0]
    g_tail = jnp.concatenate([tail_sum[:R_SMALL], rep_all.reshape(N_DEV * R_REP, LANES)], axis=0)
    loss = rep_all.reshape(-1)[N_REP]
    tails = _adamw([g_tail[None]], _pack_tail(w)[None], _pack_tail(m)[None], _pack_tail(v)[None],
                   name="adamw_tail", tb=TB_TAIL)

    outs = []
    for kind in range(4):
        d = {**{n: res[n][kind] for n in big}, **_unpack_tail(tails[kind][0])}
        d.update({n: jnp.swapaxes(d[n], 1, 2) for n in _TRANSPOSED})
        outs += [d[n] for n in _ORDER]
    return (loss, dx[None], *outs)
```

```python
import math

import jax
import jax.numpy as jnp
from jax import lax
from jax.experimental import pallas as pl
from jax.experimental.pallas import tpu as pltpu
from jax.experimental.pallas import tpu_sc as plsc

F32 = jnp.float32
BF16 = jnp.bfloat16

D = 1024
DEPTH = 2
N_DEV = 8
CHUNK = 64
LRU_BLOCKS = 8
LRU_BLOCK = 128
LRU_C = 8.0
D_SSD = 2 * D
SSD_HEADS = 32
SSD_GROUPS = 4
GROUP_W = D_SSD // SSD_GROUPS
SSD_STATE = 128
D_XBC = D_SSD + 2 * SSD_GROUPS * SSD_STATE
XA_HEADS = 4
XA_HEAD_DIM = 256
D_FF = 2816
ALPHA = (2 * DEPTH) ** 0.25
EPS = 1e-5
N_IN = 11296

S_Z, S_XBC, S_DT, W_SSD = 0, 2048, 5120, 5632
DT_REAL = 32

ADAM_LR, ADAM_B1, ADAM_B2, ADAM_EPS, ADAM_WD, ADAM_STEP = 0.001, 0.9, 0.999, 1e-08, 0.01, 10

VMEM_LIMIT = 56 * 1024 * 1024
MESH = pl.DeviceIdType.MESH
ANY = pl.BlockSpec(memory_space=pl.ANY)


def _cp(*sem):
    return pltpu.CompilerParams(dimension_semantics=sem, vmem_limit_bytes=VMEM_LIMIT)


def _blk(n, target):
    if n % 128:
        return n
    best = 128
    for b in range(128, min(n, target) + 1, 128):
        if n % b == 0:
            best = b
    return best


def _iota(shape, dim):
    return lax.broadcasted_iota(jnp.int32, shape, dim)


def _sigmoid(x):
    return 0.5 + 0.5 * jnp.tanh(0.5 * x)


def _log1p(e):
    u = 1.0 + e
    return jnp.where(u == 1.0, e, jnp.log(u) * (e / (u - 1.0)))


def _softplus(x):
    return jnp.maximum(x, 0.0) + _log1p(jnp.exp(-jnp.abs(x)))


_G0 = math.sqrt(2.0 / math.pi)
_G1 = 0.044715


def _gelu_and_grad(x):
    x2 = x * x
    u = 0.5 + 0.5 * jnp.tanh(x * (_G0 + (_G0 * _G1) * x2))
    dg = u + (x * (u * (1.0 - u))) * ((2.0 * _G0) + (6.0 * _G0 * _G1) * x2)
    return x * u, dg


_NN = (((1,), (0,)), ((), ()))
_NT = (((1,), (1,)), ((), ()))
_TN = (((0,), (0,)), ((), ()))


def _dot(a, b, dims=_NN):
    return lax.dot_general(a.astype(BF16), b.astype(BF16), dims, preferred_element_type=F32)


def _dot_hi(a, b, dims=_NN):
    return lax.dot_general(a, b, dims, precision=lax.Precision.HIGHEST, preferred_element_type=F32)


def _split(v):
    hi = v.astype(BF16)
    return hi, (v - hi.astype(F32)).astype(BF16)


def _dot01(v, e, dims=_NN):
    hi, lo = _split(v)
    return (lax.dot_general(hi, e, dims, preferred_element_type=F32)
            + lax.dot_general(lo, e, dims, preferred_element_type=F32))


def _conv_taps(xe, n):
    return [xe[8:8 + n] if j == 3 else pltpu.roll(xe, 3 - j, 0)[8:8 + n] for j in range(4)]


def _conv_fwd(taps, cw, cb):
    return cb + cw[0:1] * taps[0] + cw[1:2] * taps[1] + cw[2:3] * taps[2] + cw[3:4] * taps[3]


def _conv_bwd(dc, dnext, x, cw, n):
    ext = jnp.concatenate([dc, dnext], axis=0)
    shifted = [pltpu.roll(ext, n + 8 - (3 - j), 0)[0:n] for j in range(3)] + [dc]
    dx = cw[0:1] * shifted[0] + cw[1:2] * shifted[1] + cw[2:3] * shifted[2] + cw[3:4] * dc
    dcw = jnp.concatenate([jnp.sum(x * shifted[j], axis=0, keepdims=True) for j in range(4)], axis=0)
    return dx, dcw, jnp.sum(dc, axis=0, keepdims=True)


MM_VMEM_BUDGET = 44 * 1024 * 1024
MM_MAX_TILE = 1408
MM_MAX_K = 5632


def _divisors(n, cap):
    return [n] if n % 128 else [b for b in range(128, min(n, cap) + 1, 128) if n % b == 0]


def _mm_tiles(m_dim, n_dim, k_dim, a_bytes, b_bytes, o_bytes, has_add, tn_fixed):
    best = None
    for tm in _divisors(m_dim, MM_MAX_TILE):
        for tn in ([tn_fixed] if tn_fixed else _divisors(n_dim, MM_MAX_TILE)):
            for tk in _divisors(k_dim, MM_MAX_K):
                vmem = 2 * (tm * tk * a_bytes + tk * tn * b_bytes + tm * tn * (o_bytes + (4 if has_add else 0)))
                vmem += tm * tn * 4 if tk < k_dim else 0
                if vmem <= MM_VMEM_BUDGET:
                    key = (tm * tn * tk, tk, tn)
                    if best is None or key > best[0]:
                        best = (key, (tm, tn, tk))
    assert best is not None, (m_dim, n_dim, k_dim)
    return best[1]


def _mm(a, b, *, ta=False, tb=False, out_dtype=F32, add=None, add_scale=1.0, name, split_n=None):
    if ta:
        k_dim, m_dim = a.shape
    else:
        m_dim, k_dim = a.shape
    if tb:
        n_dim, k2 = b.shape
    else:
        k2, n_dim = b.shape
    assert k_dim == k2, (a.shape, b.shape, ta, tb)
    tm, tn, tk = _mm_tiles(m_dim, n_dim, k_dim, a.dtype.itemsize, b.dtype.itemsize, jnp.dtype(out_dtype).itemsize,
                           add is not None, split_n)
    nk = k_dim // tk
    a_spec = pl.BlockSpec((tk, tm), lambda i, j, k: (k, i)) if ta else pl.BlockSpec((tm, tk), lambda i, j, k: (i, k))
    b_spec = pl.BlockSpec((tn, tk), lambda i, j, k: (j, k)) if tb else pl.BlockSpec((tk, tn), lambda i, j, k: (k, j))
    o_spec = pl.BlockSpec((tm, tn), lambda i, j, k: (i, j))
    out_shape = (m_dim, n_dim)
    if split_n is not None:
        assert add is None and tn == split_n, (tn, split_n)
        o_spec = pl.BlockSpec((None, tm, tn), lambda i, j, k: (j, i, 0))
        out_shape = (n_dim // tn, m_dim, tn)
    dims = (((0 if ta else 1,), (1 if tb else 0,)), ((), ()))
    has_add = add is not None

    def body(*refs):
        a_ref, b_ref = refs[:2]
        add_ref = refs[2] if has_add else None
        o_ref = refs[3] if has_add else refs[2]
        acc_ref = refs[-1] if nk > 1 else None
        k = pl.program_id(2)

        def product():
            return lax.dot_general(a_ref[...].astype(BF16), b_ref[...].astype(BF16), dims, preferred_element_type=F32)

        def finish(r):
            if has_add:
                r = r + add_scale * add_ref[...]
            o_ref[...] = r.astype(out_dtype)

        if nk == 1:
            finish(product())
            return

        @pl.when(k == 0)
        def _():
            acc_ref[...] = product()

        @pl.when((k > 0) & (k < nk - 1))
        def _():
            acc_ref[...] += product()

        @pl.when(k == nk - 1)
        def _():
            finish(acc_ref[...] + product())

    in_specs = [a_spec, b_spec] + ([o_spec] if has_add else [])
    args = (a, b) + ((add,) if has_add else ())
    return pl.pallas_call(
        body, name=name, grid=(m_dim // tm, n_dim // tn, nk),
        in_specs=in_specs, out_specs=o_spec,
        out_shape=jax.ShapeDtypeStruct(out_shape, out_dtype),
        scratch_shapes=[pltpu.VMEM((tm, tn), F32)] if nk > 1 else [],
        cost_estimate=pl.CostEstimate(
            flops=2 * m_dim * n_dim * k_dim, transcendentals=0,
            bytes_accessed=a.size * a.dtype.itemsize + b.size * b.dtype.itemsize
            + m_dim * n_dim * (jnp.dtype(out_dtype).itemsize + (4 if has_add else 0))),
        compiler_params=_cp("parallel", "parallel", "arbitrary"),
    )(*args)


def _ln_fwd(x, f, g, b, name, tb=1024):
    t = x.shape[0]
    tb = min(tb, t)

    def body(x_ref, f_ref, g_ref, b_ref, o_ref, ob_ref):
        u = ALPHA * x_ref[...] + f_ref[...]
        mu = jnp.mean(u, axis=-1, keepdims=True)
        d = u - mu
        var = jnp.mean(d * d, axis=-1, keepdims=True)
        y = d * lax.rsqrt(var + EPS) * g_ref[...] + b_ref[...]
        o_ref[...] = y
        ob_ref[...] = y.astype(BF16)

    row = pl.BlockSpec((tb, D), lambda i: (i, 0))
    par = pl.BlockSpec((1, D), lambda i: (0, 0))
    return pl.pallas_call(
        body, name=name, grid=(t // tb,), in_specs=[row, row, par, par], out_specs=[row, row],
        out_shape=[jax.ShapeDtypeStruct((t, D), F32), jax.ShapeDtypeStruct((t, D), BF16)],
        compiler_params=_cp("parallel"),
    )(x, f, g, b)


def _ln_bwd(x, f, dy, g, name, tb=1024):
    t = x.shape[0]
    tb = min(tb, t)

    def body(x_ref, f_ref, dy_ref, g_ref, du_ref, dg_ref, db_ref):
        @pl.when(pl.program_id(0) == 0)
        def _():
            dg_ref[...] = jnp.zeros_like(dg_ref)
            db_ref[...] = jnp.zeros_like(db_ref)

        u = ALPHA * x_ref[...] + f_ref[...]
        mu = jnp.mean(u, axis=-1, keepdims=True)
        d = u - mu
        var = jnp.mean(d * d, axis=-1, keepdims=True)
        rstd = lax.rsqrt(var + EPS)
        xhat = d * rstd
        dy = dy_ref[...]
        dxh = dy * g_ref[...]
        m1 = jnp.mean(dxh, axis=-1, keepdims=True)
        m2 = jnp.mean(dxh * xhat, axis=-1, keepdims=True)
        du_ref[...] = rstd * (dxh - m1 - xhat * m2)
        dg_ref[...] += jnp.sum(dy * xhat, axis=0, keepdims=True)
        db_ref[...] += jnp.sum(dy, axis=0, keepdims=True)

    row = pl.BlockSpec((tb, D), lambda i: (i, 0))
    par = pl.BlockSpec((1, D), lambda i: (0, 0))
    return pl.pallas_call(
        body, name=name, grid=(t // tb,), in_specs=[row, row, row, par], out_specs=[row, par, par],
        out_shape=[jax.ShapeDtypeStruct((t, D), F32), jax.ShapeDtypeStruct((1, D), F32),
                   jax.ShapeDtypeStruct((1, D), F32)],
        compiler_params=_cp("arbitrary"),
    )(x, f, dy, g)


FFN_TM, FFN_TN = 512, D_FF // 2


def _ffn_in_swiglu(x, w, name):
    t = x.shape[0]
    tm = min(FFN_TM, t)
    nj = D_FF // FFN_TN

    def body(x_ref, wg_ref, wu_ref, g_ref, u_ref, a_ref):
        xb = x_ref[...].astype(BF16)
        g = lax.dot_general(xb, wg_ref[...], _NT, preferred_element_type=F32)
        u = lax.dot_general(xb, wu_ref[...], _NT, preferred_element_type=F32)
        g_ref[...] = g.astype(BF16)
        u_ref[...] = u.astype(BF16)
        a_ref[...] = (g * _sigmoid(g) * u).astype(BF16)

    tile = pl.BlockSpec((tm, FFN_TN), lambda i, j: (i, j))
    return pl.pallas_call(
        body, name=name, grid=(t // tm, nj),
        in_specs=[pl.BlockSpec((tm, D), lambda i, j: (i, 0)), pl.BlockSpec((FFN_TN, D), lambda i, j: (j, 0)),
                  pl.BlockSpec((FFN_TN, D), lambda i, j: (nj + j, 0))],
        out_specs=[tile, tile, tile],
        out_shape=[jax.ShapeDtypeStruct((t, D_FF), BF16)] * 3,
        compiler_params=_cp("parallel", "parallel"),
    )(x, w, w)


def _d_swiglu(du, w_down, g, u, name):
    t = du.shape[0]
    tm = min(FFN_TM, t)

    def body(du_ref, w_ref, g_ref, u_ref, dg_ref, dup_ref):
        da = lax.dot_general(du_ref[...].astype(BF16), w_ref[...], _NT, preferred_element_type=F32)
        g_v = g_ref[...].astype(F32)
        s = _sigmoid(g_v)
        dg_ref[...] = (da * u_ref[...].astype(F32) * (s * (1.0 + g_v * (1.0 - s)))).astype(BF16)
        dup_ref[...] = (da * g_v * s).astype(BF16)

    tile = pl.BlockSpec((tm, FFN_TN), lambda i, j: (i, j))
    return pl.pallas_call(
        body, name=name, grid=(t // tm, D_FF // FFN_TN),
        in_specs=[pl.BlockSpec((tm, D), lambda i, j: (i, 0)), pl.BlockSpec((FFN_TN, D), lambda i, j: (j, 0)), tile, tile],
        out_specs=[tile, tile],
        out_shape=[jax.ShapeDtypeStruct((t, D_FF), BF16), jax.ShapeDtypeStruct((t, D_FF), BF16)],
        compiler_params=_cp("parallel", "parallel"),
    )(du, w_down, g, u)


def _merge_fwd(pgl, bg, b1, b2, b3, name, tb=1024):
    t = pgl.shape[0]
    tb = min(tb, t)

    def body(gl_ref, bg_ref, b1_ref, b2_ref, b3_ref, o_ref):
        acc = None
        for j, b_ref in enumerate((b1_ref, b2_ref, b3_ref)):
            sl = slice(j * D, (j + 1) * D)
            term = _sigmoid(gl_ref[:, sl].astype(F32) + bg_ref[:, sl]) * b_ref[...].astype(F32)
            acc = term if acc is None else acc + term
        o_ref[...] = acc.astype(BF16)

    row = pl.BlockSpec((tb, D), lambda i: (i, 0))
    return pl.pallas_call(
        body, name=name, grid=(t // tb,),
        in_specs=[pl.BlockSpec((tb, 3 * D), lambda i: (i, 0)), pl.BlockSpec((1, 3 * D), lambda i: (0, 0)), row, row, row],
        out_specs=row, out_shape=jax.ShapeDtypeStruct((t, D), BF16), compiler_params=_cp("parallel"),
    )(pgl, bg, b1, b2, b3)


def _merge_bwd(pgl, bg, b1, b2, b3, dm, name, tb=512):
    t = pgl.shape[0]
    tb = min(tb, t)

    def body(gl_ref, bg_ref, b1_ref, b2_ref, b3_ref, dm_ref, dgl_ref, d1_ref, d2_ref, d3_ref, dbg_ref):
        @pl.when(pl.program_id(0) == 0)
        def _():
            dbg_ref[...] = jnp.zeros_like(dbg_ref)

        dm_v = dm_ref[...]
        for j, (b_ref, d_ref) in enumerate(((b1_ref, d1_ref), (b2_ref, d2_ref), (b3_ref, d3_ref))):
            sl = slice(j * D, (j + 1) * D)
            gate = _sigmoid(gl_ref[:, sl].astype(F32) + bg_ref[:, sl])
            d_ref[...] = (dm_v * gate).astype(BF16)
            dgl = dm_v * b_ref[...].astype(F32) * (gate * (1.0 - gate))
            dgl_ref[:, sl] = dgl.astype(BF16)
            dbg_ref[:, sl] += jnp.sum(dgl, axis=0, keepdims=True)

    row = pl.BlockSpec((tb, D), lambda i: (i, 0))
    wide = pl.BlockSpec((tb, 3 * D), lambda i: (i, 0))
    par = pl.BlockSpec((1, 3 * D), lambda i: (0, 0))
    return pl.pallas_call(
        body, name=name, grid=(t // tb,),
        in_specs=[wide, par, row, row, row, row], out_specs=[wide, row, row, row, par],
        out_shape=[jax.ShapeDtypeStruct((t, 3 * D), BF16)] + [jax.ShapeDtypeStruct((t, D), BF16)] * 3
                  + [jax.ShapeDtypeStruct((1, 3 * D), F32)],
        compiler_params=_cp("arbitrary"),
    )(pgl, bg, b1, b2, b3, dm)


def _xa_probs(q, kv_ref, hd):
    sl = slice(hd * XA_HEAD_DIM, (hd + 1) * XA_HEAD_DIM)
    k = kv_ref[:, sl]
    v = kv_ref[:, D + hd * XA_HEAD_DIM:D + (hd + 1) * XA_HEAD_DIM]
    s = _dot(q[:, sl], k, _NT) * (XA_HEAD_DIM ** -0.5)
    e = jnp.exp(s - jnp.max(s, axis=1, keepdims=True))
    return sl, k, v, e / jnp.sum(e, axis=1, keepdims=True)


def _xa_fwd(pq, kv, name, tb=512):
    t = pq.shape[0]
    tb = min(tb, t)

    def body(q_ref, kv_ref, o_ref):
        q = q_ref[...]
        for hd in range(XA_HEADS):
            sl, _, v, p = _xa_probs(q, kv_ref, hd)
            o_ref[:, sl] = _dot(p, v).astype(BF16)

    row = pl.BlockSpec((tb, D), lambda i: (i, 0))
    return pl.pallas_call(
        body, name=name, grid=(t // tb,),
        in_specs=[row, pl.BlockSpec(kv.shape, lambda i: (0, 0))], out_specs=row,
        out_shape=jax.ShapeDtypeStruct((t, D), BF16), compiler_params=_cp("parallel"),
    )(pq, kv)


def _xa_bwd(pq, kv, dy, name, tb=512):
    t = pq.shape[0]
    tb = min(tb, t)

    def body(q_ref, kv_ref, dy_ref, dq_ref, dkv_ref):
        @pl.when(pl.program_id(0) == 0)
        def _():
            dkv_ref[...] = jnp.zeros_like(dkv_ref)

        q = q_ref[...]
        for hd in range(XA_HEADS):
            sl, k, v, p = _xa_probs(q, kv_ref, hd)
            dyh = dy_ref[:, sl]
            vsl = slice(D + hd * XA_HEAD_DIM, D + (hd + 1) * XA_HEAD_DIM)
            dkv_ref[:, vsl] += _dot(p, dyh, _TN)
            dp = _dot(dyh, v, _NT)
            ds = p * (dp - jnp.sum(dp * p, axis=1, keepdims=True)) * (XA_HEAD_DIM ** -0.5)
            dq_ref[:, sl] = _dot(ds, k).astype(BF16)
            dkv_ref[:, sl] += _dot(ds, q[:, sl], _TN)

    row = pl.BlockSpec((tb, D), lambda i: (i, 0))
    kvs = pl.BlockSpec(kv.shape, lambda i: (0, 0))
    return pl.pallas_call(
        body, name=name, grid=(t // tb,), in_specs=[row, kvs, row], out_specs=[row, kvs],
        out_shape=[jax.ShapeDtypeStruct((t, D), BF16), jax.ShapeDtypeStruct(kv.shape, F32)],
        compiler_params=_cp("arbitrary"),
    )(pq, kv, dy)


SUBLANES = 8


def _scan(a, u, reverse):
    n, c = a.shape
    groups = n // SUBLANES
    a = a.reshape(groups, SUBLANES, c)
    u = u.reshape(groups, SUBLANES, c)
    sub = _iota((1, SUBLANES, 1), 1)
    d = 1
    while d < SUBLANES:
        keep = (sub < SUBLANES - d) if reverse else (sub >= d)
        shift = SUBLANES - d if reverse else d
        u = a * jnp.where(keep, pltpu.roll(u, shift, 1), 0.0) + u
        a = a * jnp.where(keep, pltpu.roll(a, shift, 1), 1.0)
        d *= 2
    edge = 0 if reverse else SUBLANES - 1
    out, carry = [None] * groups, None
    for j in (reversed(range(groups)) if reverse else range(groups)):
        out[j] = u[j] if carry is None else u[j] + a[j] * carry
        carry = out[j][edge:edge + 1]
    return jnp.concatenate(out, axis=0)


def _lru_gates(xc, wa_ref, ba, wi_ref, bi, lam):
    za = jnp.concatenate([_dot(xc[:, n * 128:(n + 1) * 128], wa_ref[n]) for n in range(LRU_BLOCKS)], axis=1) + ba
    zi = jnp.concatenate([_dot(xc[:, n * 128:(n + 1) * 128], wi_ref[n]) for n in range(LRU_BLOCKS)], axis=1) + bi
    r = 1.0 / (1.0 + jnp.exp(-za))
    ig = _sigmoid(zi)
    sp = _softplus(-lam)
    log_a = (-LRU_C) * r * sp
    a = jnp.exp(log_a)
    m = jnp.sqrt(-jnp.tanh(log_a) * (1.0 + a * a))
    u = m * (ig * xc)
    return a, u, r, ig, m, sp


def _lru_fwd(p, cw, cb, wa, ba, wi, bi, lam, name, tb=256):
    t = p.shape[0]
    tb = min(tb, t)
    nb = t // tb
    r8 = tb // 8

    def body(x_ref, xp_ref, g_ref, cw_ref, cb_ref, wa_ref, ba_ref, wi_ref, bi_ref, lam_ref, y_ref, h_ref, xc_ref,
             hc_ref):
        i = pl.program_id(0)

        @pl.when(i == 0)
        def _():
            hc_ref[...] = jnp.zeros_like(hc_ref)

        halo = jnp.where(i == 0, 0.0, xp_ref[...])
        taps = _conv_taps(jnp.concatenate([halo, x_ref[...]], axis=0), tb)
        xc = _conv_fwd(taps, cw_ref[...], cb_ref[...])
        xc_ref[...] = xc
        a, u, _, _, _, _ = _lru_gates(xc, wa_ref, ba_ref[...], wi_ref, bi_ref[...], lam_ref[...])
        row = _iota((tb, 1), 0)
        u = u + jnp.where(row == 0, a * hc_ref[...], 0.0)
        h = _scan(a, u, reverse=False)
        h_ref[...] = h
        hc_ref[...] = h[tb - 1:tb, :]
        gl, _ = _gelu_and_grad(g_ref[...])
        y_ref[...] = (gl * h).astype(BF16)

    par = pl.BlockSpec((1, D), lambda i: (0, 0))
    wsp = pl.BlockSpec((LRU_BLOCKS, LRU_BLOCK, LRU_BLOCK), lambda i: (0, 0, 0))
    row = pl.BlockSpec((tb, D), lambda i: (i, 0))
    return pl.pallas_call(
        body, name=name, grid=(nb,),
        in_specs=[row, pl.BlockSpec((8, D), lambda i: (jnp.maximum(i * r8 - 1, 0), 0)),
                  pl.BlockSpec((tb, D), lambda i: (i, 1)),
                  pl.BlockSpec((4, D), lambda i: (0, 0)), par, wsp, par, wsp, par, par],
        out_specs=[row, row, row],
        out_shape=[jax.ShapeDtypeStruct((t, D), BF16), jax.ShapeDtypeStruct((t, D), F32),
                   jax.ShapeDtypeStruct((t, D), F32)],
        scratch_shapes=[pltpu.VMEM((1, D), F32)],
        compiler_params=_cp("arbitrary"),
    )(p, p, p, cw, cb, wa, ba, wi, bi, lam)


def _lru_bwd(p, xc, h, dy, cw, wa, ba, wi, bi, lam, name, tb=256):
    t = p.shape[0]
    tb = min(tb, t)
    nb = t // tb
    r8 = tb // 8

    def body(x_ref, g_ref, xc_ref, h_ref, hp_ref, dy_ref, cw_ref, wa_ref, ba_ref, wi_ref, bi_ref, lam_ref,
             dp_ref, dcw_ref, dcb_ref, dwa_ref, dba_ref, dwi_ref, dbi_ref, dlam_ref, carry_ref, dnext_ref):
        i = pl.program_id(0)
        blk = nb - 1 - i

        @pl.when(i == 0)
        def _():
            for r in (dcw_ref, dcb_ref, dwa_ref, dba_ref, dwi_ref, dbi_ref, dlam_ref, carry_ref, dnext_ref):
                r[...] = jnp.zeros_like(r)

        xc = xc_ref[...]
        lam = lam_ref[...]
        a, _, r, ig, m, sp = _lru_gates(xc, wa_ref, ba_ref[...], wi_ref, bi_ref[...], lam)
        gl, dgl = _gelu_and_grad(g_ref[...])
        h = h_ref[...]
        dy = dy_ref[...]
        dp_ref[:, D:] = (dy * h * dgl).astype(BF16)
        row = _iota((tb, 1), 0)
        dh = dy * gl + jnp.where(row == tb - 1, carry_ref[...], 0.0)
        b = jnp.where(row < tb - 1, pltpu.roll(a, tb - 1, 0), 0.0)
        gs = _scan(b, dh, reverse=True)
        carry_ref[...] = a[0:1] * gs[0:1]
        h_last = jnp.where(blk == 0, 0.0, hp_ref[7:8, :])
        hprev = jnp.where(row == 0, h_last, pltpu.roll(h, 1, 0))
        da = gs * hprev
        dm = gs * ig * xc
        di = gs * m * xc
        dxc = gs * m * ig
        dlog = a * (da - a * (dm / m))
        dr = dlog * ((-LRU_C) * sp)
        dsp = jnp.sum(dlog * ((-LRU_C) * r), axis=0, keepdims=True)
        dlam_ref[...] += dsp * (-_sigmoid(-lam))
        dza = dr * r * (1.0 - r)
        dzi = di * ig * (1.0 - ig)
        dba_ref[...] += jnp.sum(dza, axis=0, keepdims=True)
        dbi_ref[...] += jnp.sum(dzi, axis=0, keepdims=True)
        parts = []
        for n in range(LRU_BLOCKS):
            sl = slice(n * 128, (n + 1) * 128)
            dwa_ref[n] += _dot(xc[:, sl], dza[:, sl], _TN)
            dwi_ref[n] += _dot(xc[:, sl], dzi[:, sl], _TN)
            parts.append(_dot(dza[:, sl], wa_ref[n], _NT) + _dot(dzi[:, sl], wi_ref[n], _NT))
        dxc = dxc + jnp.concatenate(parts, axis=1)
        dx, dcw, dcb = _conv_bwd(dxc, dnext_ref[...], x_ref[...], cw_ref[...], tb)
        dp_ref[:, :D] = dx.astype(BF16)
        dcw_ref[...] += dcw
        dcb_ref[...] += dcb
        dnext_ref[...] = dxc[0:8]

    par = pl.BlockSpec((1, D), lambda i: (0, 0))
    wsp = pl.BlockSpec((LRU_BLOCKS, LRU_BLOCK, LRU_BLOCK), lambda i: (0, 0, 0))
    cws = pl.BlockSpec((4, D), lambda i: (0, 0))
    rev = lambda i: nb - 1 - i
    blk0 = pl.BlockSpec((tb, D), lambda i: (rev(i), 0))
    w_shape = jax.ShapeDtypeStruct((LRU_BLOCKS, LRU_BLOCK, LRU_BLOCK), F32)
    v_shape = jax.ShapeDtypeStruct((1, D), F32)
    return pl.pallas_call(
        body, name=name, grid=(nb,),
        in_specs=[blk0, pl.BlockSpec((tb, D), lambda i: (rev(i), 1)), blk0, blk0,
                  pl.BlockSpec((8, D), lambda i: (jnp.maximum(rev(i) * r8 - 1, 0), 0)), blk0,
                  cws, wsp, par, wsp, par, par],
        out_specs=[pl.BlockSpec((tb, 2 * D), lambda i: (rev(i), 0)), cws, par, wsp, par, wsp, par, par],
        out_shape=[jax.ShapeDtypeStruct((t, 2 * D), BF16), jax.ShapeDtypeStruct((4, D), F32), v_shape,
                   w_shape, v_shape, w_shape, v_shape, v_shape],
        scratch_shapes=[pltpu.VMEM((1, D), F32), pltpu.VMEM((8, D), F32)],
        compiler_params=_cp("arbitrary"),
    )(p, p, xc, h, h, dy, cw, wa, ba, wi, bi, lam)


def _ssd_consts():
    m0 = _iota((1, 128), 1) < 64
    e = (jnp.right_shift(_iota((SSD_HEADS, D_SSD), 1), 6) == _iota((SSD_HEADS, D_SSD), 0)).astype(BF16)
    tril = (_iota((CHUNK, CHUNK), 0) >= _iota((CHUNK, CHUNK), 1)).astype(F32)
    eye = (_iota((SSD_HEADS, SSD_HEADS), 0) == _iota((SSD_HEADS, SSD_HEADS), 1)).astype(F32)
    r2 = _iota((CHUNK, 128), 0)
    c2 = jnp.bitwise_and(_iota((CHUNK, 128), 1), 63)
    return dict(m0=m0, e=e, tril=tril, eye=eye, causal2=r2 >= c2, fold=(c2 == r2).astype(BF16))


SSD_STEP = 4


def _ssd_pre(c, dt_raw, dtb_ref, alog_ref, dvec_ref, k):
    sg = _sigmoid(c)
    xbc = c * sg
    dtp = dt_raw + dtb_ref[...]
    dt = _softplus(dtp)
    a = -jnp.exp(alog_ref[...])
    cs = _dot_hi(k["tril"], dt * a)
    cs_last = cs[CHUNK - 1:CHUNK]
    dend = jnp.exp(cs_last - cs)
    cdec = jnp.exp(cs_last)
    big = _dot01(jnp.concatenate([dt, jnp.exp(cs), dend], axis=0), k["e"])
    small = _dot01(jnp.concatenate([jnp.broadcast_to(cdec, (8, SSD_HEADS)),
                                    jnp.broadcast_to(dvec_ref[...], (8, SSD_HEADS))], axis=0), k["e"])
    cst2 = _dot_hi(k["eye"], jnp.concatenate([cs, cs], axis=0), _NT)
    return dict(c=c, sg=sg, xs=xbc[:, :D_SSD], bm=xbc[:, D_SSD:D_SSD + 512],
                cm=xbc[:, D_SSD + 512:], dtp=dtp, dt=dt, a=a, cs=cs, dend=dend, cdec=cdec,
                dtx=big[0:CHUNK], ecx=big[CHUNK:2 * CHUNK], dex=big[2 * CHUNK:3 * CHUNK],
                cdx=small[0:1], ddx=small[8:9], cst2=cst2)


def _pair_decay(p, cs, cst2, k):
    h0, h1 = 2 * p, 2 * p + 1
    colp = jnp.where(k["m0"], cs[:, h0:h0 + 1], cs[:, h1:h1 + 1])
    rowp = jnp.where(k["m0"], cst2[h0:h0 + 1, :], cst2[h1:h1 + 1, :])
    return jnp.where(k["causal2"], jnp.exp(colp - rowp), 0.0)


def _pair_stack(xp, k):
    return jnp.concatenate([jnp.where(k["m0"], xp, 0.0), jnp.where(k["m0"], 0.0, xp)], axis=0)


def _group_norm(yz, nw, with_stats=False):
    outs, stats = [], []
    for g in range(SSD_GROUPS):
        yzg = yz[:, g * GROUP_W:(g + 1) * GROUP_W]
        r = lax.rsqrt(jnp.mean(yzg * yzg, axis=1, keepdims=True) + EPS)
        outs.append(yzg * r)
        stats.append(r)
    y = jnp.concatenate(outs, axis=1) * nw
    return (y, stats) if with_stats else y


def _ssd_fwd(p, cw, cb, dtb, alog, dvec, nw, name):
    t = p.shape[0]
    step = SSD_STEP if t % (SSD_STEP * CHUNK) == 0 else 1
    rows_blk, nb, nc = step * CHUNK, t // (step * CHUNK), t // CHUNK

    def body(p_blk, cw_ref, cb_ref, dtb_ref, alog_ref, dvec_ref, nw_ref, y_blk, yraw_blk, hs_blk, c_blk,
             h_scr, tail_scr):
        @pl.when(pl.program_id(0) == 0)
        def _():
            h_scr[...] = jnp.zeros_like(h_scr)
            tail_scr[...] = jnp.zeros_like(tail_scr)

        k = _ssd_consts()

        def one_chunk(j, carry):
            rows = pl.ds(pl.multiple_of(j * CHUNK, CHUNK), CHUNK)
            chunk(p_blk.at[rows], y_blk.at[rows], yraw_blk.at[rows], hs_blk.at[j], c_blk.at[rows], k,
                  cw_ref, cb_ref, dtb_ref, alog_ref, dvec_ref, nw_ref, h_scr, tail_scr)
            return carry

        lax.fori_loop(0, step, one_chunk, 0)

    def chunk(p_ref, y_ref, yraw_ref, hs_ref, c_ref, k, cw_ref, cb_ref, dtb_ref, alog_ref, dvec_ref, nw_ref,
              h_scr, tail_scr):
        x_in = p_ref[:, S_XBC:S_DT]
        taps = _conv_taps(jnp.concatenate([tail_scr[...], x_in], axis=0), CHUNK)
        tail_scr[...] = x_in[CHUNK - 8:]
        c = _conv_fwd(taps, cw_ref[...], cb_ref[...])
        c_ref[...] = c
        s = _ssd_pre(c, p_ref[:, S_DT:S_DT + DT_REAL], dtb_ref, alog_ref, dvec_ref, k)
        xs, bm, cm = s["xs"], s["bm"], s["cm"]
        xdt = xs * s["dtx"]
        hprev = h_scr[...]
        hs_ref[...] = hprev
        ys, hn = [], []
        for g in range(SSD_GROUPS):
            gs = slice(g * GROUP_W, (g + 1) * GROUP_W)
            bg = bm[:, g * 128:(g + 1) * 128]
            cg = cm[:, g * 128:(g + 1) * 128]
            cbdup = _dot(cg, jnp.concatenate([bg, bg], axis=0), _NT)
            hp_g = hprev[:, gs]
            yd = []
            for q in range(4):
                pr = g * 4 + q
                mp = cbdup * _pair_decay(pr, s["cs"], s["cst2"], k)
                yd.append(_dot(mp, _pair_stack(xdt[:, pr * 128:(pr + 1) * 128], k)))
            ys.append(jnp.concatenate(yd, axis=1) + _dot(cg, hp_g) * s["ecx"][:, gs])
            hn.append(hp_g * s["cdx"][:, gs] + _dot(bg, xdt[:, gs] * s["dex"][:, gs], _TN))
        h_scr[...] = jnp.concatenate(hn, axis=1)
        yraw = jnp.concatenate(ys, axis=1) + s["ddx"] * xs
        yraw_ref[...] = yraw
        z = p_ref[:, S_Z:S_Z + D_SSD]
        y_ref[...] = _group_norm(yraw * (z * _sigmoid(z)), nw_ref[...]).astype(BF16)

    hv = pl.BlockSpec((1, DT_REAL), lambda i: (0, 0))
    return pl.pallas_call(
        body, name=name, grid=(nb,),
        in_specs=[pl.BlockSpec((rows_blk, W_SSD), lambda i: (i, 0)),
                  pl.BlockSpec((4, D_XBC), lambda i: (0, 0)), pl.BlockSpec((1, D_XBC), lambda i: (0, 0)),
                  hv, hv, hv, pl.BlockSpec((1, D_SSD), lambda i: (0, 0))],
        out_specs=[pl.BlockSpec((rows_blk, D_SSD), lambda i: (i, 0)), pl.BlockSpec((rows_blk, D_SSD), lambda i: (i, 0)),
                   pl.BlockSpec((step, SSD_STATE, D_SSD), lambda i: (i, 0, 0)),
                   pl.BlockSpec((rows_blk, D_XBC), lambda i: (i, 0))],
        out_shape=[jax.ShapeDtypeStruct((t, D_SSD), BF16), jax.ShapeDtypeStruct((t, D_SSD), F32),
                   jax.ShapeDtypeStruct((nc, SSD_STATE, D_SSD), F32), jax.ShapeDtypeStruct((t, D_XBC), F32)],
        scratch_shapes=[pltpu.VMEM((SSD_STATE, D_SSD), F32), pltpu.VMEM((8, D_XBC), F32)],
        compiler_params=_cp("arbitrary"),
    )(p, cw, cb, dtb, alog, dvec, nw)


def _ssd_bwd(p, c, yraw, hs, dy, cw, dtb, alog, dvec, nw, name):
    t = p.shape[0]
    step = 1
    rows_blk, nb = step * CHUNK, t // (step * CHUNK)

    def body(p_blk, c_blk, yraw_blk, hs_blk, dy_blk, cw_ref, dtb_ref, alog_ref, dvec_ref, nw_ref,
             dp_blk, dcw_ref, dcb_ref, ddtb_ref, dalog_ref, dd_ref, dnw_ref, dh_scr, dnext_scr):
        @pl.when(pl.program_id(0) == 0)
        def _():
            for r in (dcw_ref, dcb_ref, ddtb_ref, dalog_ref, dd_ref, dnw_ref, dh_scr, dnext_scr):
                r[...] = jnp.zeros_like(r)

        k = _ssd_consts()

        def one_chunk(jj, carry):
            j = step - 1 - jj
            rows = pl.ds(pl.multiple_of(j * CHUNK, CHUNK), CHUNK)
            chunk(p_blk.at[rows], c_blk.at[rows], yraw_blk.at[rows], hs_blk.at[j], dy_blk.at[rows], dp_blk.at[rows], k,
                  cw_ref, dtb_ref, alog_ref, dvec_ref, nw_ref, dcw_ref, dcb_ref, ddtb_ref, dalog_ref, dd_ref, dnw_ref,
                  dh_scr, dnext_scr)
            return carry

        lax.fori_loop(0, step, one_chunk, 0)

    def chunk(p_ref, c_ref, yraw_ref, hs_ref, dy_ref, dp_ref, k, cw_ref, dtb_ref, alog_ref, dvec_ref, nw_ref,
              dcw_ref, dcb_ref, ddtb_ref, dalog_ref, dd_ref, dnw_ref, dh_scr, dnext_scr):
        s = _ssd_pre(c_ref[...], p_ref[:, S_DT:S_DT + DT_REAL], dtb_ref, alog_ref, dvec_ref, k)
        xs, bm, cm, cs, dt, a = s["xs"], s["bm"], s["cm"], s["cs"], s["dt"], s["a"]
        m0 = k["m0"]
        xdt = xs * s["dtx"]
        hprev = hs_ref[...]
        dh = dh_scr[...]

        nw_v = nw_ref[...]
        yraw = yraw_ref[...]
        z = p_ref[:, S_Z:S_Z + D_SSD]
        sz = _sigmoid(z)
        siluz = z * sz
        yz = yraw * siluz
        dyo = dy_ref[...]
        dyn = dyo * nw_v
        dyz_parts, dnw_parts = [], []
        for g in range(SSD_GROUPS):
            gs = slice(g * GROUP_W, (g + 1) * GROUP_W)
            yzg = yz[:, gs]
            r = lax.rsqrt(jnp.mean(yzg * yzg, axis=1, keepdims=True) + EPS)
            dnw_parts.append(jnp.sum(dyo[:, gs] * yzg * r, axis=0, keepdims=True))
            dyz_parts.append(r * dyn[:, gs] - yzg * (r * r * r) * jnp.mean(dyn[:, gs] * yzg, axis=1, keepdims=True))
        dnw_ref[...] += jnp.concatenate(dnw_parts, axis=1)
        dyz = jnp.concatenate(dyz_parts, axis=1)
        d_y = dyz * siluz
        dp_ref[:, S_Z:S_Z + D_SSD] = (dyz * yraw * (sz * (1.0 + z * (1.0 - sz)))).astype(BF16)
        dd_row = jnp.sum(d_y * xs, axis=0, keepdims=True)
        dxs = d_y * s["ddx"]

        lane_h = _iota((1, SSD_HEADS), 1)
        sub_h = _iota((SSD_HEADS, 1), 0)
        dcs = jnp.zeros((CHUNK, SSD_HEADS), F32)
        dcst2 = jnp.zeros((SSD_HEADS, 128), F32)
        dxdt_parts, db_parts, dc_parts, dhp_parts, yoff_parts, dend_parts, dcd_parts = [], [], [], [], [], [], []
        for g in range(SSD_GROUPS):
            gs = slice(g * GROUP_W, (g + 1) * GROUP_W)
            bg = bm[:, g * 128:(g + 1) * 128]
            cg = cm[:, g * 128:(g + 1) * 128]
            bdup = jnp.concatenate([bg, bg], axis=0)
            cbdup = _dot(cg, bdup, _NT)
            dcb2 = jnp.zeros((CHUNK, 128), F32)
            dxp_parts = []
            for q in range(4):
                pr = g * 4 + q
                h0, h1 = 2 * pr, 2 * pr + 1
                lp = _pair_decay(pr, cs, s["cst2"], k)
                mp = cbdup * lp
                xst = _pair_stack(xdt[:, pr * 128:(pr + 1) * 128], k)
                dyp = d_y[:, pr * 128:(pr + 1) * 128]
                dmp = _dot(dyp, xst, _NT)
                dxst = _dot(mp, dyp, _TN)
                dxp_parts.append(jnp.where(m0, dxst[:CHUNK], dxst[CHUNK:]))
                dcb2 = dcb2 + dmp * lp
                dlm = dmp * mp
                rs0 = jnp.sum(jnp.where(m0, dlm, 0.0), axis=1, keepdims=True)
                rs1 = jnp.sum(jnp.where(m0, 0.0, dlm), axis=1, keepdims=True)
                dcs = dcs + jnp.where(lane_h == h0, rs0, 0.0) + jnp.where(lane_h == h1, rs1, 0.0)
                colsum = jnp.sum(dlm, axis=0, keepdims=True)
                sel = ((sub_h == h0) & m0) | ((sub_h == h1) & jnp.logical_not(m0))
                dcst2 = dcst2 - jnp.where(sel, colsum, 0.0)
            dcg = _dot(dcb2, bdup)
            dbdup = _dot(dcb2, cg, _TN)
            dbg = dbdup[:CHUNK] + dbdup[CHUNK:]
            hp_g = hprev[:, gs]
            zoff = _dot(cg, hp_g)
            dzo = d_y[:, gs] * s["ecx"][:, gs]
            dcg = dcg + _dot(dzo, hp_g, _NT)
            dh_g = dh[:, gs]
            dhp_parts.append(_dot(cg, dzo, _TN) + dh_g * s["cdx"][:, gs])
            dcd_parts.append(jnp.sum(dh_g * hp_g, axis=0, keepdims=True))
            wg = xdt[:, gs] * s["dex"][:, gs]
            dbg = dbg + _dot(wg, dh_g, _NT)
            dwg = _dot(bg, dh_g)
            dxdt_parts.append(jnp.concatenate(dxp_parts, axis=1) + dwg * s["dex"][:, gs])
            dend_g = dwg * wg
            dend_parts.append(jnp.sum(dend_g, axis=0, keepdims=True))
            yoff_parts.append(dzo * zoff - dend_g)
            db_parts.append(dbg)
            dc_parts.append(dcg)
        dh_scr[...] = jnp.concatenate(dhp_parts, axis=1)
        dxdt = jnp.concatenate(dxdt_parts, axis=1)
        sums = _dot01(jnp.concatenate([jnp.concatenate(yoff_parts, axis=1), dxdt * xs], axis=0), k["e"], _NT)
        rows8 = jnp.concatenate([jnp.broadcast_to(jnp.concatenate(r, axis=1), (8, D_SSD))
                                 for r in (dcd_parts, [dd_row], dend_parts)], axis=0)
        small = _dot01(rows8, k["e"], _NT)
        dd_ref[...] += small[8:9]
        dcs_last = small[0:1] * s["cdec"] + small[16:17]
        hi, lo = _split(dcst2)
        dcs = (dcs + sums[0:CHUNK]
               + lax.dot_general(k["fold"], hi, _NT, preferred_element_type=F32)
               + lax.dot_general(k["fold"], lo, _NT, preferred_element_type=F32)
               + jnp.where(_iota((CHUNK, 1), 0) == CHUNK - 1, dcs_last, 0.0))
        dda = _dot_hi(k["tril"], dcs, _TN)
        ddt = dda * a + sums[CHUNK:2 * CHUNK]
        dalog_ref[...] += jnp.sum(dda * dt, axis=0, keepdims=True) * a
        dxs = dxs + dxdt * s["dtx"]
        draw = ddt * _sigmoid(s["dtp"])
        ddtb_ref[...] += jnp.sum(draw, axis=0, keepdims=True)
        dp_ref[:, S_DT:] = jnp.zeros((CHUNK, W_SSD - S_DT), BF16)
        dp_ref[:, S_DT:S_DT + DT_REAL] = draw.astype(BF16)
        dxbc = jnp.concatenate([dxs] + db_parts + dc_parts, axis=1)
        sg, c = s["sg"], s["c"]
        dc = dxbc * (sg * (1.0 + c * (1.0 - sg)))
        dx, dcw, dcb = _conv_bwd(dc, dnext_scr[...], p_ref[:, S_XBC:S_DT], cw_ref[...], CHUNK)
        dp_ref[:, S_XBC:S_DT] = dx.astype(BF16)
        dcw_ref[...] += dcw
        dcb_ref[...] += dcb
        dnext_scr[...] = dc[0:8]

    rev = lambda i: nb - 1 - i
    hv = pl.BlockSpec((1, DT_REAL), lambda i: (0, 0))
    cws = pl.BlockSpec((4, D_XBC), lambda i: (0, 0))
    cbs = pl.BlockSpec((1, D_XBC), lambda i: (0, 0))
    nws = pl.BlockSpec((1, D_SSD), lambda i: (0, 0))
    wide = pl.BlockSpec((rows_blk, D_SSD), lambda i: (rev(i), 0))
    hshape = jax.ShapeDtypeStruct((1, DT_REAL), F32)
    return pl.pallas_call(
        body, name=name, grid=(nb,),
        in_specs=[pl.BlockSpec((rows_blk, W_SSD), lambda i: (rev(i), 0)),
                  pl.BlockSpec((rows_blk, D_XBC), lambda i: (rev(i), 0)),
                  wide, pl.BlockSpec((step, SSD_STATE, D_SSD), lambda i: (rev(i), 0, 0)), wide,
                  cws, hv, hv, hv, nws],
        out_specs=[pl.BlockSpec((rows_blk, W_SSD), lambda i: (rev(i), 0)), cws, cbs, hv, hv, hv, nws],
        out_shape=[jax.ShapeDtypeStruct((t, W_SSD), BF16), jax.ShapeDtypeStruct((4, D_XBC), F32),
                   jax.ShapeDtypeStruct((1, D_XBC), F32), hshape, hshape, hshape,
                   jax.ShapeDtypeStruct((1, D_SSD), F32)],
        scratch_shapes=[pltpu.VMEM((SSD_STATE, D_SSD), F32), pltpu.VMEM((8, D_XBC), F32)],
        compiler_params=_cp("arbitrary"),
    )(p, c, yraw, hs, dy, cw, dtb, alog, dvec, nw)


def _loss_head(y, target, name, tb=1024):
    t = y.shape[0]
    tb = min(tb, t)

    def body(y_ref, t_ref, dy_ref, l_ref):
        @pl.when(pl.program_id(0) == 0)
        def _():
            l_ref[...] = jnp.zeros_like(l_ref)

        e = y_ref[...] - t_ref[...]
        dy_ref[...] = e * (1.0 / D)
        l_ref[...] += jnp.sum(jnp.sum(e * e, axis=1, keepdims=True), axis=0, keepdims=True) * (0.5 / D)

    row = pl.BlockSpec((tb, D), lambda i: (i, 0))
    return pl.pallas_call(
        body, name=name, grid=(t // tb,), in_specs=[row, row],
        out_specs=[row, pl.BlockSpec((8, 128), lambda i: (0, 0))],
        out_shape=[jax.ShapeDtypeStruct((t, D), F32), jax.ShapeDtypeStruct((8, 128), F32)],
        compiler_params=_cp("arbitrary"),
    )(y, target)


def _adamw(slots, w, m, v, name, tb):
    nl = len(slots)
    ns, r, c = slots[0].shape
    assert r % tb == 0 and w.shape == (nl, r, c), (r, tb, w.shape)

    def body(*refs):
        s_refs = refs[:nl]
        w_ref, m_ref, v_ref, g_ref, d_ref, m2_ref, v2_ref = refs[nl:]

        def total(ref):
            acc = ref[0].astype(F32)
            for j in range(1, ns):
                acc = acc + ref[j].astype(F32)
            return acc

        g = total(s_refs[0])
        for layer in range(1, nl):
            g = jnp.where(pl.program_id(0) == layer, total(s_refs[layer]), g)
        m2 = ADAM_B1 * m_ref[...] + (1.0 - ADAM_B1) * g
        v2 = ADAM_B2 * v_ref[...] + (1.0 - ADAM_B2) * (g * g)
        m_hat = m2 / (1.0 - ADAM_B1 ** ADAM_STEP)
        v_hat = v2 / (1.0 - ADAM_B2 ** ADAM_STEP)
        g_ref[...] = g
        d_ref[...] = -ADAM_LR * (m_hat / (jnp.sqrt(v_hat) + ADAM_EPS) + ADAM_WD * w_ref[...])
        m2_ref[...] = m2
        v2_ref[...] = v2

    def slot_spec(layer):
        return pl.BlockSpec((ns, tb, c), lambda l, i: (0, jnp.where(l == layer, i, 0), 0))

    row = pl.BlockSpec((None, tb, c), lambda l, i: (l, i, 0))
    shp = jax.ShapeDtypeStruct((nl, r, c), F32)
    return pl.pallas_call(
        body, name=name, grid=(nl, r // tb),
        in_specs=[slot_spec(layer) for layer in range(nl)] + [row, row, row],
        out_specs=[row, row, row, row], out_shape=[shp, shp, shp, shp], compiler_params=_cp("arbitrary", "arbitrary"),
    )(*slots, w, m, v)


def _pair_sum(own, got, name, out_dtype, tb):
    nj, _, r, c = own.shape
    mc = lax.axis_index("c")

    def body(mc_ref, a_ref, b_ref, o_ref):
        del mc_ref
        o_ref[...] = (a_ref[...] + b_ref[...]).astype(out_dtype)

    return pl.pallas_call(
        body, name=name,
        grid_spec=pltpu.PrefetchScalarGridSpec(
            num_scalar_prefetch=1, grid=(nj, r // tb),
            in_specs=[pl.BlockSpec((None, None, tb, c), lambda j, i, mc_ref: (j, mc_ref[0], i, 0)),
                      pl.BlockSpec((None, tb, c), lambda j, i, mc_ref: (j, i, 0))],
            out_specs=pl.BlockSpec((None, tb, c), lambda j, i, mc_ref: (j, i, 0))),
        out_shape=jax.ShapeDtypeStruct((nj, r, c), out_dtype), compiler_params=_cp("parallel", "parallel"),
    )(jnp.reshape(mc, (1,)).astype(jnp.int32), own, got)


def _slot_sum(slots, name):
    ns, r, c = slots.shape

    def body(s_ref, o_ref):
        g = s_ref[0]
        for j in range(1, ns):
            g = g + s_ref[j]
        o_ref[...] = g

    return pl.pallas_call(body, name=name, out_shape=jax.ShapeDtypeStruct((r, c), F32))(slots)


def _position():
    return lax.axis_index("x"), lax.axis_index("y"), lax.axis_index("c")


def _comm(exchange, peers, xs, out_shapes, sems, name, collective_id):
    n = len(xs)
    if collective_id is None:
        def body(*refs):
            exchange(refs[:n], refs[n:n + len(out_shapes)], *refs[n + len(out_shapes):])

        return pl.pallas_call(body, name=name, in_specs=[ANY] * n, out_specs=[ANY] * len(out_shapes),
                              out_shape=out_shapes, scratch_shapes=sems)(*xs)
    def launch(*refs):
        barrier = pltpu.get_barrier_semaphore()
        to = peers(*_position())
        for peer in to:
            pl.semaphore_signal(barrier, inc=1, device_id=peer, device_id_type=MESH)
        pl.semaphore_wait(barrier, len(to))
        exchange(refs[:n], refs[n:n + len(out_shapes)], *refs[n + len(out_shapes):])

    return pl.kernel(launch, out_type=out_shapes, mesh=plsc.ScalarSubcoreMesh(axis_name="seq", num_cores=1), name=name,
                     scratch_types=sems, compiler_params=pltpu.CompilerParams(collective_id=collective_id))(*xs)


def _all_gather(xs, name, collective_id=None):
    n = len(xs)
    return _comm(_gather_body, lambda x, y, c: [(x, y, 1 - c), (1 - x, y, c), (x, 1 - y, c), (1 - x, 1 - y, c)], xs,
                 [jax.ShapeDtypeStruct((N_DEV,) + x.shape, x.dtype) for x in xs],
                 [pltpu.SemaphoreType.DMA((n, 7)), pltpu.SemaphoreType.DMA((n, 7)), pltpu.SemaphoreType.DMA((n,))],
                 name, collective_id)


def _gather_body(x_refs, out_refs, send_sems, recv_sems, local_sems):
    n = len(x_refs)
    mx, my, mc = _position()
    me, sibling = (mx, my, mc), (mx, my, 1 - mc)
    chips = [(1 - mx, my), (mx, 1 - my), (1 - mx, 1 - my)]

    def copy(a, k, block, to, own=False):
        dst = out_refs[a].at[4 * block[0] + 2 * block[1] + block[2]]
        return pltpu.make_async_remote_copy(
            src_ref=x_refs[a] if own else dst, dst_ref=dst,
            send_sem=send_sems.at[a, k], recv_sem=recv_sems.at[a, k], device_id=to, device_id_type=MESH)

    mine = [pltpu.make_async_copy(x_refs[a], out_refs[a].at[4 * mx + 2 * my + mc], local_sems.at[a]) for a in range(n)]
    first = [copy(a, 1 + j, me, (*chip, mc), own=True) for j, chip in enumerate(chips) for a in range(n)]
    first += [copy(a, 0, me, sibling, own=True) for a in range(n)]
    for cp in first + mine:
        cp.start()
    passed = []
    for j, chip in enumerate(chips):
        for a in range(n):
            copy(a, 1 + j, (*chip, mc), me).wait_recv()
            passed.append(copy(a, 4 + j, (*chip, mc), sibling))
            passed[-1].start()
    for a in range(n):
        copy(a, 0, sibling, me).wait_recv()
    for j, chip in enumerate(chips):
        for a in range(n):
            copy(a, 4 + j, (*chip, 1 - mc), me).wait_recv()
    for cp in first + passed:
        cp.wait_send()
    for cp in mine:
        cp.wait()


def _exchange_sibling(gs, name, collective_id=None):
    n = len(gs)

    def exchange(g_refs, r_refs, send_sems, recv_sems):
        mx, my, mc = _position()
        cps = [pltpu.make_async_remote_copy(src_ref=g_refs[a].at[:, 1 - mc], dst_ref=r_refs[a],
                                            send_sem=send_sems.at[a], recv_sem=recv_sems.at[a],
                                            device_id=(mx, my, 1 - mc), device_id_type=MESH) for a in range(n)]
        for cp in cps:
            cp.start()
        for cp in cps:
            cp.wait()

    return _comm(exchange, lambda x, y, c: [(x, y, 1 - c)], gs,
                 [jax.ShapeDtypeStruct(g.shape[:1] + g.shape[2:], g.dtype) for g in gs],
                 [pltpu.SemaphoreType.DMA((n,)), pltpu.SemaphoreType.DMA((n,))], name, collective_id)


def _exchange_chips(ss, name, collective_id=None):
    n = len(ss)

    def exchange(s_refs, r_refs, send_sems, recv_sems, local_sems):
        mx, my, mc = _position()
        my_chip = 2 * mx + my
        chips = [(1 - mx, my), (mx, 1 - my), (1 - mx, 1 - my)]

        def copy(a, k, to_slot):
            px, py = chips[k]
            return pltpu.make_async_remote_copy(
                src_ref=s_refs[a].at[2 * px + py], dst_ref=r_refs[a].at[to_slot], send_sem=send_sems.at[a, k],
                recv_sem=recv_sems.at[a, k], device_id=(px, py, mc), device_id_type=MESH)

        sends = [copy(a, k, my_chip) for k in range(3) for a in range(n)]
        local = [pltpu.make_async_copy(s_refs[a].at[my_chip], r_refs[a].at[my_chip], local_sems.at[a])
                 for a in range(n)]
        for cp in sends + local:
            cp.start()
        for k in range(3):
            px, py = chips[k]
            for a in range(n):
                copy(a, k, 2 * px + py).wait_recv()
        for cp in sends:
            cp.wait_send()
        for cp in local:
            cp.wait()

    return _comm(exchange, lambda x, y, c: [(1 - x, y, c), (x, 1 - y, c), (1 - x, 1 - y, c)], ss,
                 [jax.ShapeDtypeStruct(s.shape, s.dtype) for s in ss],
                 [pltpu.SemaphoreType.DMA((n, 3)), pltpu.SemaphoreType.DMA((n, 3)), pltpu.SemaphoreType.DMA((n,))],
                 name, collective_id)


def _cols_concat(g, name, tb=128):
    _, k_dim, n = g.shape

    def body(g_ref, o_ref):
        o_ref[...] = jnp.concatenate([g_ref[d] for d in range(N_DEV)], axis=1)

    return pl.pallas_call(
        body, name=name, grid=(k_dim // tb,),
        in_specs=[pl.BlockSpec((N_DEV, tb, n), lambda i: (0, i, 0))],
        out_specs=pl.BlockSpec((tb, N_DEV * n), lambda i: (i, 0)),
        out_shape=jax.ShapeDtypeStruct((k_dim, N_DEV * n), g.dtype), compiler_params=_cp("parallel"),
    )(g)


def _cols_split(parts, name, tb=128):
    k_dim = parts[0].shape[0]
    n = sum(p.shape[1] for p in parts) // N_DEV

    def body(*refs):
        full = jnp.concatenate([r[...] for r in refs[:-1]], axis=1)
        for d in range(N_DEV):
            refs[-1][d] = full[:, d * n:(d + 1) * n]

    return pl.pallas_call(
        body, name=name, grid=(k_dim // tb,),
        in_specs=[pl.BlockSpec((tb, p.shape[1]), lambda i: (i, 0)) for p in parts],
        out_specs=pl.BlockSpec((N_DEV, tb, n), lambda i: (0, i, 0)),
        out_shape=jax.ShapeDtypeStruct((N_DEV, k_dim, n), parts[0].dtype), compiler_params=_cp("parallel"),
    )(*parts)


_Q0, _GL0 = 7200, 8224
N_SHARD_IN = N_IN // N_DEV


def _w_in_regions(g, name, tb=256):
    def body(g_ref, ssd_ref, lru_ref, q_ref, gl_ref):
        full = jnp.concatenate([g_ref[d] for d in range(N_DEV)], axis=1)
        lru_ref[...] = full[:, 0:2 * D]
        ssd_ref[:, :S_DT] = full[:, 2 * D:2 * D + S_DT]
        ssd_ref[:, S_DT:] = jnp.zeros((tb, W_SSD - S_DT), g.dtype)
        ssd_ref[:, S_DT:S_DT + DT_REAL] = full[:, 2 * D + S_DT:_Q0]
        q_ref[...] = full[:, _Q0:_GL0]
        gl_ref[...] = full[:, _GL0:N_IN]

    widths = (W_SSD, 2 * D, D, 3 * D)
    return pl.pallas_call(
        body, name=name, grid=(D // tb,),
        in_specs=[pl.BlockSpec((N_DEV, tb, N_SHARD_IN), lambda i: (0, i, 0))],
        out_specs=[pl.BlockSpec((tb, wd), lambda i: (i, 0)) for wd in widths],
        out_shape=[jax.ShapeDtypeStruct((D, wd), g.dtype) for wd in widths], compiler_params=_cp("parallel"),
    )(g)


def _w_in_shards(dssd, dlru, dq, dgl, name, tb=128):
    def body(ssd_ref, lru_ref, q_ref, gl_ref, o_ref):
        full = jnp.concatenate([lru_ref[...], ssd_ref[:, :S_DT + DT_REAL], q_ref[...], gl_ref[...]], axis=1)
        for d in range(N_DEV):
            o_ref[d] = full[:, d * N_SHARD_IN:(d + 1) * N_SHARD_IN]

    return pl.pallas_call(
        body, name=name, grid=(D // tb,),
        in_specs=[pl.BlockSpec((tb, a.shape[1]), lambda i: (i, 0)) for a in (dssd, dlru, dq, dgl)],
        out_specs=pl.BlockSpec((N_DEV, tb, N_SHARD_IN), lambda i: (0, i, 0)),
        out_shape=jax.ShapeDtypeStruct((N_DEV, D, N_SHARD_IN), F32), compiler_params=_cp("parallel"),
    )(dssd, dlru, dq, dgl)


_BIG = (("w_in", "col", (1024, 1412)), ("mem_w_kv", "col", (1024, 256)), ("w_br_lru", "row", (128, 1024)),
        ("w_br_ssd", "row", (256, 1024)), ("w_br_xa", "row", (128, 1024)), ("w_out", "row", (128, 1024)),
        ("ffn_w_in", "row", (704, 1024)), ("ffn_w_down", "row", (352, 1024)))
_TRANSPOSED = ("ffn_w_in",)
_SMALL = (("b_gate", (3, 128)), ("lru_conv_w", (4, 128)), ("ssd_conv_w", (4, 384)))
_REP = (("lru_conv_b", (1024,)), ("lru_w_a", (8, 128, 128)), ("lru_b_a", (1024,)), ("lru_w_i", (8, 128, 128)),
        ("lru_b_i", (1024,)), ("lru_lambda", (1024,)), ("ssd_conv_b", (3072,)), ("ssd_dt_bias", (32,)),
        ("ssd_a_log", (32,)), ("ssd_d", (32,)), ("ssd_norm_w", (2048,)), ("ln1_g", (1024,)), ("ln1_b", (1024,)),
        ("ln2_g", (1024,)), ("ln2_b", (1024,)))
_ORDER = ("w_in", "b_gate", "lru_conv_w", "lru_conv_b", "lru_w_a", "lru_b_a", "lru_w_i", "lru_b_i", "lru_lambda",
          "ssd_conv_w", "ssd_conv_b", "ssd_dt_bias", "ssd_a_log", "ssd_d", "ssd_norm_w", "mem_w_kv", "w_br_lru",
          "w_br_ssd", "w_br_xa", "w_out", "ln1_g", "ln1_b", "ffn_w_in", "ffn_w_down", "ln2_g", "ln2_b")

LANES = 1024
N_SMALL = sum(DEPTH * s[0] * s[1] for _, s in _SMALL)
R_SMALL = 8
N_REP = sum(DEPTH * math.prod(s) for _, s in _REP)
R_REP = 68
R_SM = R_SMALL + R_REP + 4
R_TAIL = R_SMALL + N_DEV * R_REP
TB_TAIL = 184
assert N_SMALL <= R_SMALL * LANES and N_REP + 1 <= N_DEV * R_REP * LANES


def _rows(flat, rows):
    return jnp.pad(flat, (0, rows * LANES - flat.shape[0])).reshape(rows, LANES)


def _rowblk(a, cap):
    return max(b for b in range(16, cap + 1, 16) if a % b == 0)


def _pack_tail(d):
    small = jnp.concatenate([d[n].reshape(-1) for n, _ in _SMALL])
    rep = jnp.concatenate([d[n].reshape(-1) for n, _ in _REP])
    return jnp.concatenate([_rows(small, R_SMALL), _rows(rep, N_DEV * R_REP)], axis=0)


def _unpack_tail(a):
    out, o = {}, 0
    flat = a[:R_SMALL].reshape(-1)
    for n, s in _SMALL:
        k = DEPTH * math.prod(s)
        out[n] = flat[o:o + k].reshape((DEPTH,) + s)
        o += k
    flat, o = a[R_SMALL:].reshape(-1), 0
    for n, s in _REP:
        k = DEPTH * math.prod(s)
        out[n] = flat[o:o + k].reshape((DEPTH,) + s)
        o += k
    return out


def _by_dest(g):
    g = g.reshape(g.shape[:-1] + (N_DEV, g.shape[-1] // N_DEV))
    return jnp.moveaxis(g, -2, 0).reshape(N_DEV, -1)


def _from_stack(st):
    st = jnp.moveaxis(st, 0, -2)
    return st.reshape(st.shape[:-2] + (st.shape[-2] * st.shape[-1],))


def _layer_fwd(x, xb, mem, w, l):
    nm = lambda s: f"{s}_l{l}"
    wi = w["wi"]
    row = lambda v: v.reshape(1, -1)
    s = dict(x=x, xb=xb, wi=wi)
    s["p_ssd"] = _mm(xb, wi["ssd"], name=nm("proj_ssd"))
    s["p_lru"] = _mm(xb, wi["lru"], name=nm("proj_lru"))
    s["p_q"] = _mm(xb, wi["q"], out_dtype=BF16, name=nm("proj_q"))
    s["p_gl"] = _mm(xb, wi["gl"], out_dtype=BF16, name=nm("proj_gl"))
    s["lru_par"] = (w["lru_conv_w"], row(w["lru_conv_b"]), w["lru_w_a"], row(w["lru_b_a"]), w["lru_w_i"],
                    row(w["lru_b_i"]), row(w["lru_lambda"]))
    s["y_lru"], s["h"], s["xc"] = _lru_fwd(s["p_lru"], *s["lru_par"], name=nm("lru_fwd"))
    s["ssd_par"] = (w["ssd_conv_w"], row(w["ssd_conv_b"]), row(w["ssd_dt_bias"]), row(w["ssd_a_log"]),
                    row(w["ssd_d"]), row(w["ssd_norm_w"]))
    s["y_ssd"], s["yraw"], s["hs"], s["c_ssd"] = _ssd_fwd(s["p_ssd"], *s["ssd_par"], name=nm("ssd_fwd"))
    s["kv"] = _mm(mem, w["mem_w_kv"], name=nm("kv"))
    s["y_xa"] = _xa_fwd(s["p_q"], s["kv"], name=nm("xa_fwd"))
    s["b1"] = _mm(s["y_lru"], w["w_br_lru"], out_dtype=BF16, name=nm("br_lru"))
    s["b2"] = _mm(s["y_ssd"], w["w_br_ssd"], out_dtype=BF16, name=nm("br_ssd"))
    s["b3"] = _mm(s["y_xa"], w["w_br_xa"], out_dtype=BF16, name=nm("br_xa"))
    s["bg"] = row(w["b_gate"])
    s["merged"] = _merge_fwd(s["p_gl"], s["bg"], s["b1"], s["b2"], s["b3"], name=nm("merge_fwd"))
    s["mix"] = _mm(s["merged"], w["w_out"], name=nm("out_proj"))
    s["x1"], s["x1b"] = _ln_fwd(x, s["mix"], row(w["ln1_g"]), row(w["ln1_b"]), name=nm("ln1_fwd"))
    s["gate"], s["up"], s["act"] = _ffn_in_swiglu(s["x1b"], w["ffn_w_in"], name=nm("ffn_in"))
    s["f"] = _mm(s["act"], w["ffn_w_down"], name=nm("ffn_down"))
    s["x2"], s["x2b"] = _ln_fwd(s["x1"], s["f"], row(w["ln2_g"]), row(w["ln2_b"]), name=nm("ln2_fwd"))
    return s


def _layer_bwd(s, mem, w, dxo, l, hooks=None):
    nm = lambda t: f"{t}_l{l}"
    g = {}
    hook = lambda stage, t: hooks[stage](t, g) if hooks and stage in hooks else t
    row = lambda v: v.reshape(1, -1)
    slabs = lambda a: a.reshape(N_DEV, a.shape[0] // N_DEV, a.shape[1])
    du2, dg, db = _ln_bwd(s["x1"], s["f"], dxo, row(w["ln2_g"]), name=nm("ln2_bwd"))
    g["ln2_g"], g["ln2_b"] = dg[0], db[0]
    dgate, dup = _d_swiglu(du2, w["ffn_w_down"], s["gate"], s["up"], name=nm("d_swiglu"))
    g["ffn_w_down"] = slabs(_mm(s["act"], du2, ta=True, name=nm("dw_ffn_down")))
    dx1 = _mm(dgate, w["ffn_w_in"][:D_FF], add=du2, add_scale=ALPHA, name=nm("d_x1_gate"))
    dx1 = _mm(dup, w["ffn_w_in"][D_FF:], add=dx1, name=nm("d_x1_up"))
    g["ffn_w_in"] = slabs(jnp.concatenate([_mm(dgate, s["x1b"], ta=True, name=nm("dw_ffn_gate")),
                                           _mm(dup, s["x1b"], ta=True, name=nm("dw_ffn_up"))], axis=0))
    du1, dg, db = _ln_bwd(s["x"], s["mix"], dx1, row(w["ln1_g"]), name=nm("ln1_bwd"))
    g["ln1_g"], g["ln1_b"] = dg[0], db[0]
    dmerged = hook("mid", _mm(du1, w["w_out"], tb=True, name=nm("d_merged")))
    g["w_out"] = slabs(_mm(s["merged"], du1, ta=True, name=nm("dw_out")))
    dp_gl, d1, d2, d3, dbg = _merge_bwd(s["p_gl"], s["bg"], s["b1"], s["b2"], s["b3"], dmerged, name=nm("merge_bwd"))
    g["b_gate"] = dbg.reshape(3, D)
    dy_lru = _mm(d1, w["w_br_lru"], tb=True, name=nm("d_y_lru"))
    g["w_br_lru"] = slabs(_mm(s["y_lru"], d1, ta=True, name=nm("dw_br_lru")))
    dy_ssd = _mm(d2, w["w_br_ssd"], tb=True, name=nm("d_y_ssd"))
    g["w_br_ssd"] = slabs(_mm(s["y_ssd"], d2, ta=True, name=nm("dw_br_ssd")))
    dy_xa = _mm(d3, w["w_br_xa"], tb=True, out_dtype=BF16, name=nm("d_y_xa"))
    g["w_br_xa"] = slabs(_mm(s["y_xa"], d3, ta=True, name=nm("dw_br_xa")))
    dp_q, dkv = _xa_bwd(s["p_q"], s["kv"], dy_xa, name=nm("xa_bwd"))
    g["mem_w_kv"] = _mm(mem, dkv, ta=True, split_n=2 * D // N_DEV, name=nm("dw_kv"))
    dy_ssd = hook("branches", dy_ssd)
    ssd_cw, _, *ssd_rest = s["ssd_par"]
    dp_ssd, dcw, dcb, ddtb, dalog, dd, dnw = _ssd_bwd(s["p_ssd"], s["c_ssd"], s["yraw"], s["hs"], dy_ssd, ssd_cw,
                                                      *ssd_rest, name=nm("ssd_bwd"))
    g["ssd_conv_w"], g["ssd_conv_b"], g["ssd_dt_bias"] = dcw, dcb[0], ddtb[0]
    g["ssd_a_log"], g["ssd_d"], g["ssd_norm_w"] = dalog[0], dd[0], dnw[0]
    dp_ssd = hook("ssd", dp_ssd)
    lru_cw, _, *lru_rest = s["lru_par"]
    dp_lru, dcw, dcb, dwa, dba, dwi, dbi, dlam = _lru_bwd(s["p_lru"], s["xc"], s["h"], dy_lru, lru_cw, *lru_rest,
                                                          name=nm("lru_bwd"))
    g["lru_conv_w"], g["lru_conv_b"], g["lru_w_a"], g["lru_b_a"] = dcw, dcb[0], dwa, dba[0]
    g["lru_w_i"], g["lru_b_i"], g["lru_lambda"] = dwi, dbi[0], dlam[0]
    wi, x = s["wi"], s["xb"]
    g["w_in"] = _w_in_shards(_mm(x, dp_ssd, ta=True, name=nm("dw_in_ssd")), _mm(x, dp_lru, ta=True, name=nm("dw_in_lru")),
                             _mm(x, dp_q, ta=True, name=nm("dw_in_q")), _mm(x, dp_gl, ta=True, name=nm("dw_in_gl")),
                             name=nm("dw_in_shards"))
    dp_ssd = hook("weights", dp_ssd)
    dx = _mm(dp_ssd, wi["ssd"], tb=True, add=du1, add_scale=ALPHA, name=nm("dx_ssd"))
    dx = hook("dx", _mm(dp_lru, wi["lru"], tb=True, add=dx, name=nm("dx_lru")))
    dx = _mm(dp_q, wi["q"], tb=True, add=dx, name=nm("dx_q"))
    dx = _mm(dp_gl, wi["gl"], tb=True, add=dx, name=nm("dx_gl"))
    return dx, g


def _local_step(x, mem, target, layers, hooks=None):
    saved, xb = [], x.astype(BF16)
    for l in range(DEPTH):
        saved.append(_layer_fwd(x, xb, mem, layers[l], l))
        x, xb = saved[-1]["x2"], saved[-1]["x2b"]
    dx, loss = _loss_head(x, target, name="loss_head")
    if hooks and "loss" in hooks[-1]:
        loss = hooks[-1]["loss"](loss, None)
    grads = [None] * DEPTH
    for l in reversed(range(DEPTH)):
        dx, grads[l] = _layer_bwd(saved[l], mem, layers[l], dx, l, hooks[l] if hooks else None)
    return loss, dx, grads


def kernel(x, mem, w_in, b_gate, lru_conv_w, lru_conv_b, lru_w_a, lru_b_a, lru_w_i, lru_b_i, lru_lambda, ssd_conv_w, ssd_conv_b, ssd_dt_bias, ssd_a_log, ssd_d, ssd_norm_w, mem_w_kv, w_br_lru, w_br_ssd, w_br_xa, w_out, ln1_g, ln1_b, ffn_w_in, ffn_w_down, ln2_g, ln2_b, loss_target, m_w_in, m_b_gate, m_lru_conv_w, m_lru_conv_b, m_lru_w_a, m_lru_b_a, m_lru_w_i, m_lru_b_i, m_lru_lambda, m_ssd_conv_w, m_ssd_conv_b, m_ssd_dt_bias, m_ssd_a_log, m_ssd_d, m_ssd_norm_w, m_mem_w_kv, m_w_br_lru, m_w_br_ssd, m_w_br_xa, m_w_out, m_ln1_g, m_ln1_b, m_ffn_w_in, m_ffn_w_down, m_ln2_g, m_ln2_b, v_w_in, v_b_gate, v_lru_conv_w, v_lru_conv_b, v_lru_w_a, v_lru_b_a, v_lru_w_i, v_lru_b_i, v_lru_lambda, v_ssd_conv_w, v_ssd_conv_b, v_ssd_dt_bias, v_ssd_a_log, v_ssd_d, v_ssd_norm_w, v_mem_w_kv, v_w_br_lru, v_w_br_ssd, v_w_br_xa, v_w_out, v_ln1_g, v_ln1_b, v_ffn_w_in, v_ffn_w_down, v_ln2_g, v_ln2_b):
    local = dict(locals())
    w = {n: local[n] for n in _ORDER}
    m = {n: local["m_" + n] for n in _ORDER}
    v = {n: local["v_" + n] for n in _ORDER}
    for n in _TRANSPOSED:
        w[n], m[n], v[n] = (jnp.swapaxes(a, 1, 2) for a in (w[n], m[n], v[n]))

    big = [n for n, _, _ in _BIG]
    kinds = {n: kind for n, kind, _ in _BIG}

    small = _rows(jnp.concatenate([w[n].reshape(-1) for n, _ in _SMALL]), R_SMALL)
    first = _all_gather([w["w_in"][0].astype(BF16), small], name="gather_w_in_l0")
    rest, later, _ = lax.optimization_barrier(([w[n][0].astype(BF16) for n in big[1:]],
                                               [w[n][1].astype(BF16) for n in big], first[-1]))
    rest = _all_gather(rest, "gather_weights_l0", collective_id=1)
    later = _all_gather(later, "gather_weights_l1", collective_id=4)
    stacks = [dict(zip(big, [first[0], *rest])), dict(zip(big, later))]
    small_all, o, small_full = first[-1].reshape(N_DEV, R_SMALL * LANES), 0, {}
    for n, s in _SMALL:
        k = DEPTH * s[0] * s[1]
        small_full[n] = _from_stack(small_all[:, o:o + k].reshape((N_DEV, DEPTH) + s))
        o += k
    layers = []
    for l in range(DEPTH):
        lw = {n: w[n][l] for n, _ in _REP}
        lw.update({n: small_full[n][l] for n, _ in _SMALL})
        lw["wi"] = dict(zip(("ssd", "lru", "q", "gl"), _w_in_regions(stacks[l]["w_in"], name=f"w_in_regions_l{l}")))
        for n in big[1:]:
            if kinds[n] == "col":
                lw[n] = _cols_concat(stacks[l][n], name=f"full_{n}_l{l}")
            else:
                lw[n] = stacks[l][n].reshape(-1, stacks[l][n].shape[-1])
        layers.append(lw)

    by_dest = lambda a: a.reshape((4, 2) + a.shape[1:])
    slots, pending, last_layer = {}, {}, {}
    queue = [stacks[1]["w_out"]]

    def after_last(operands):
        operands, _ = lax.optimization_barrier((list(operands), queue[-1]))
        return operands

    def start(tag, collective_id, names_and_grads):
        names, owns = zip(*names_and_grads)
        gots = _exchange_sibling(after_last(owns), name=f"reduce_cores_{tag}", collective_id=collective_id)
        queue.append(gots[0])
        pending[tag] = (names, owns, gots)

    def finish(tag, collective_id, t):
        names, owns, gots = pending.pop(tag)
        t, gots = lax.optimization_barrier((t, gots))
        sums = [_pair_sum(own, got, name=f"pair_sum_{tag}_{n}", out_dtype=F32 if n == "tail" else BF16,
                          tb=R_SM if n == "tail" else _rowblk(own.shape[2], 256))
                for n, own, got in zip(names, owns, gots)]
        t, sums = lax.optimization_barrier((t, sums))
        got = _exchange_chips(sums, name=f"reduce_chips_{tag}", collective_id=collective_id)
        queue.append(got[0])
        slots.update({(tag, n): s for n, s in zip(names, got)})
        return t

    def tail_of(g0):
        stacked = {n: jnp.stack([g0[n], last_layer[n]]) for n in [s[0] for s in _SMALL + _REP]}
        sm = jnp.concatenate([_by_dest(stacked[n]) for n, _ in _SMALL], axis=1)
        sm = jnp.pad(sm, ((0, 0), (0, R_SMALL * LANES - sm.shape[1])))
        rep = jnp.concatenate([stacked[n].reshape(-1) for n, _ in _REP] + [last_layer["loss"][0, :1]])
        rep = jnp.pad(rep, (0, N_DEV * R_REP * LANES - rep.shape[0])).reshape(N_DEV, R_REP * LANES)
        tail = jnp.concatenate([sm, rep, jnp.zeros((N_DEV, (R_SM - R_SMALL - R_REP) * LANES), F32)], axis=1)
        return tail.reshape(4, 2, R_SM, LANES)

    def weights_l1(t, g):
        last_layer.update(g)
        start("l1", 2, [(n, by_dest(g[n])) for n in big])
        return t

    def branches_l0(t, g):
        start("l0a", 5, [(n, by_dest(g[n])) for n in big[1:]])
        return t

    def weights_l0(t, g):
        start("l0b", 7, [("w_in", by_dest(g["w_in"])), ("tail", tail_of(g))])
        return t

    hooks = [{"branches": branches_l0, "ssd": lambda t, g: finish("l0a", 6, t), "weights": weights_l0,
              "dx": lambda t, g: finish("l0b", 8, t)},
             {"weights": weights_l1, "dx": lambda t, g: finish("l1", 3, t),
              "loss": lambda t, g: last_layer.setdefault("loss", t)}]
    _, dx, grads = _local_step(x[0], mem[0], loss_target[0], layers, hooks)

    res = {}
    for n in big:
        tb = _rowblk(w[n].shape[1], 128 if w[n].shape[2] > LANES else 256)
        res[n] = _adamw([slots["l0b" if n == "w_in" else "l0a", n], slots["l1", n]], w[n], m[n], v[n],
                        name=f"adamw_{n}", tb=tb)
    tail_sum = _slot_sum(slots["l0b", "tail"], name="sum_tail")
    rep_all = _all_gather([tail_sum[R_SMALL:R_SMALL + R_REP]], name="gather_replicated")[0]
    g_tail = jnp.concatenate([tail_sum[:R_SMALL], rep_all.reshape(N_DEV * R_REP, LANES)], axis=0)
    loss = rep_all.reshape(-1)[N_REP]
    tails = _adamw([g_tail[None]], _pack_tail(w)[None], _pack_tail(m)[None], _pack_tail(v)[None],
                   name="adamw_tail", tb=TB_TAIL)

    outs = []
    for kind in range(4):
        d = {**{n: res[n][kind] for n in big}, **_unpack_tail(tails[kind][0])}
        d.update({n: jnp.swapaxes(d[n], 1, 2) for n in _TRANSPOSED})
        outs += [d[n] for n in _ORDER]
    return (loss, dx[None], *outs)
```

```python
import math

import jax
import jax.numpy as jnp
from jax import lax
from jax.experimental import pallas as pl
from jax.experimental.pallas import tpu as pltpu
from jax.experimental.pallas import tpu_sc as plsc

F32 = jnp.float32
BF16 = jnp.bfloat16

D = 1024
DEPTH = 2
N_DEV = 8
CHUNK = 64
LRU_BLOCKS = 8
LRU_BLOCK = 128
LRU_C = 8.0
D_SSD = 2 * D
SSD_HEADS = 32
SSD_GROUPS = 4
GROUP_W = D_SSD // SSD_GROUPS
SSD_STATE = 128
D_XBC = D_SSD + 2 * SSD_GROUPS * SSD_STATE
XA_HEADS = 4
XA_HEAD_DIM = 256
D_FF = 2816
ALPHA = (2 * DEPTH) ** 0.25
EPS = 1e-5
N_IN = 11296

S_Z, S_XBC, S_DT, W_SSD = 0, 2048, 5120, 5632
DT_REAL = 32

ADAM_LR, ADAM_B1, ADAM_B2, ADAM_EPS, ADAM_WD, ADAM_STEP = 0.001, 0.9, 0.999, 1e-08, 0.01, 10

VMEM_LIMIT = 56 * 1024 * 1024
MESH = pl.DeviceIdType.MESH
ANY = pl.BlockSpec(memory_space=pl.ANY)


def _cp(*sem):
    return pltpu.CompilerParams(dimension_semantics=sem, vmem_limit_bytes=VMEM_LIMIT)


def _blk(n, target):
    if n % 128:
        return n
    best = 128
    for b in range(128, min(n, target) + 1, 128):
        if n % b == 0:
            best = b
    return best


def _iota(shape, dim):
    return lax.broadcasted_iota(jnp.int32, shape, dim)


def _sigmoid(x):
    return 0.5 + 0.5 * jnp.tanh(0.5 * x)


def _log1p(e):
    u = 1.0 + e
    return jnp.where(u == 1.0, e, jnp.log(u) * (e / (u - 1.0)))


def _softplus(x):
    return jnp.maximum(x, 0.0) + _log1p(jnp.exp(-jnp.abs(x)))


_G0 = math.sqrt(2.0 / math.pi)
_G1 = 0.044715


def _gelu_and_grad(x):
    x2 = x * x
    u = 0.5 + 0.5 * jnp.tanh(x * (_G0 + (_G0 * _G1) * x2))
    dg = u + (x * (u * (1.0 - u))) * ((2.0 * _G0) + (6.0 * _G0 * _G1) * x2)
    return x * u, dg


_NN = (((1,), (0,)), ((), ()))
_NT = (((1,), (1,)), ((), ()))
_TN = (((0,), (0,)), ((), ()))


def _dot(a, b, dims=_NN):
    return lax.dot_general(a.astype(BF16), b.astype(BF16), dims, preferred_element_type=F32)


def _dot_hi(a, b, dims=_NN):
    return lax.dot_general(a, b, dims, precision=lax.Precision.HIGHEST, preferred_element_type=F32)


def _split(v):
    hi = v.astype(BF16)
    return hi, (v - hi.astype(F32)).astype(BF16)


def _dot01(v, e, dims=_NN):
    hi, lo = _split(v)
    return (lax.dot_general(hi, e, dims, preferred_element_type=F32)
            + lax.dot_general(lo, e, dims, preferred_element_type=F32))


def _conv_taps(xe, n):
    return [xe[8:8 + n] if j == 3 else pltpu.roll(xe, 3 - j, 0)[8:8 + n] for j in range(4)]


def _conv_fwd(taps, cw, cb):
    return cb + cw[0:1] * taps[0] + cw[1:2] * taps[1] + cw[2:3] * taps[2] + cw[3:4] * taps[3]


def _conv_bwd(dc, dnext, x, cw, n):
    ext = jnp.concatenate([dc, dnext], axis=0)
    shifted = [pltpu.roll(ext, n + 8 - (3 - j), 0)[0:n] for j in range(3)] + [dc]
    dx = cw[0:1] * shifted[0] + cw[1:2] * shifted[1] + cw[2:3] * shifted[2] + cw[3:4] * dc
    dcw = jnp.concatenate([jnp.sum(x * shifted[j], axis=0, keepdims=True) for j in range(4)], axis=0)
    return dx, dcw, jnp.sum(dc, axis=0, keepdims=True)


MM_VMEM_BUDGET = 44 * 1024 * 1024
MM_MAX_TILE = 1408
MM_MAX_K = 5632


def _divisors(n, cap):
    return [n] if n % 128 else [b for b in range(128, min(n, cap) + 1, 128) if n % b == 0]


def _mm_tiles(m_dim, n_dim, k_dim, a_bytes, b_bytes, o_bytes, has_add, tn_fixed):
    best = None
    for tm in _divisors(m_dim, MM_MAX_TILE):
        for tn in ([tn_fixed] if tn_fixed else _divisors(n_dim, MM_MAX_TILE)):
            for tk in _divisors(k_dim, MM_MAX_K):
                vmem = 2 * (tm * tk * a_bytes + tk * tn * b_bytes + tm * tn * (o_bytes + (4 if has_add else 0)))
                vmem += tm * tn * 4 if tk < k_dim else 0
                if vmem <= MM_VMEM_BUDGET:
                    key = (tm * tn * tk, tk, tn)
                    if best is None or key > best[0]:
                        best = (key, (tm, tn, tk))
    assert best is not None, (m_dim, n_dim, k_dim)
    return best[1]


def _mm(a, b, *, ta=False, tb=False, out_dtype=F32, add=None, add_scale=1.0, name, split_n=None):
    if ta:
        k_dim, m_dim = a.shape
    else:
        m_dim, k_dim = a.shape
    if tb:
        n_dim, k2 = b.shape
    else:
        k2, n_dim = b.shape
    assert k_dim == k2, (a.shape, b.shape, ta, tb)
    tm, tn, tk = _mm_tiles(m_dim, n_dim, k_dim, a.dtype.itemsize, b.dtype.itemsize, jnp.dtype(out_dtype).itemsize,
                           add is not None, split_n)
    nk = k_dim // tk
    a_spec = pl.BlockSpec((tk, tm), lambda i, j, k: (k, i)) if ta else pl.BlockSpec((tm, tk), lambda i, j, k: (i, k))
    b_spec = pl.BlockSpec((tn, tk), lambda i, j, k: (j, k)) if tb else pl.BlockSpec((tk, tn), lambda i, j, k: (k, j))
    o_spec = pl.BlockSpec((tm, tn), lambda i, j, k: (i, j))
    out_shape = (m_dim, n_dim)
    if split_n is not None:
        assert add is None and tn == split_n, (tn, split_n)
        o_spec = pl.BlockSpec((None, tm, tn), lambda i, j, k: (j, i, 0))
        out_shape = (n_dim // tn, m_dim, tn)
    dims = (((0 if ta else 1,), (1 if tb else 0,)), ((), ()))
    has_add = add is not None

    def body(*refs):
        a_ref, b_ref = refs[:2]
        add_ref = refs[2] if has_add else None
        o_ref = refs[3] if has_add else refs[2]
        acc_ref = refs[-1] if nk > 1 else None
        k = pl.program_id(2)

        def product():
            return lax.dot_general(a_ref[...].astype(BF16), b_ref[...].astype(BF16), dims, preferred_element_type=F32)

        def finish(r):
            if has_add:
                r = r + add_scale * add_ref[...]
            o_ref[...] = r.astype(out_dtype)

        if nk == 1:
            finish(product())
            return

        @pl.when(k == 0)
        def _():
            acc_ref[...] = product()

        @pl.when((k > 0) & (k < nk - 1))
        def _():
            acc_ref[...] += product()

        @pl.when(k == nk - 1)
        def _():
            finish(acc_ref[...] + product())

    in_specs = [a_spec, b_spec] + ([o_spec] if has_add else [])
    args = (a, b) + ((add,) if has_add else ())
    return pl.pallas_call(
        body, name=name, grid=(m_dim // tm, n_dim // tn, nk),
        in_specs=in_specs, out_specs=o_spec,
        out_shape=jax.ShapeDtypeStruct(out_shape, out_dtype),
        scratch_shapes=[pltpu.VMEM((tm, tn), F32)] if nk > 1 else [],
        cost_estimate=pl.CostEstimate(
            flops=2 * m_dim * n_dim * k_dim, transcendentals=0,
            bytes_accessed=a.size * a.dtype.itemsize + b.size * b.dtype.itemsize
            + m_dim * n_dim * (jnp.dtype(out_dtype).itemsize + (4 if has_add else 0))),
        compiler_params=_cp("parallel", "parallel", "arbitrary"),
    )(*args)


def _ln_fwd(x, f, g, b, name, tb=512):
    t = x.shape[0]
    tb = min(tb, t)

    def body(x_ref, f_ref, g_ref, b_ref, o_ref, ob_ref):
        u = ALPHA * x_ref[...] + f_ref[...]
        mu = jnp.mean(u, axis=-1, keepdims=True)
        d = u - mu
        var = jnp.mean(d * d, axis=-1, keepdims=True)
        y = d * lax.rsqrt(var + EPS) * g_ref[...] + b_ref[...]
        o_ref[...] = y
        ob_ref[...] = y.astype(BF16)

    row = pl.BlockSpec((tb, D), lambda i: (i, 0))
    par = pl.BlockSpec((1, D), lambda i: (0, 0))
    return pl.pallas_call(
        body, name=name, grid=(t // tb,), in_specs=[row, row, par, par], out_specs=[row, row],
        out_shape=[jax.ShapeDtypeStruct((t, D), F32), jax.ShapeDtypeStruct((t, D), BF16)],
        compiler_params=_cp("parallel"),
    )(x, f, g, b)


def _ln_bwd(x, f, dy, g, name, tb=512):
    t = x.shape[0]
    tb = min(tb, t)

    def body(x_ref, f_ref, dy_ref, g_ref, du_ref, dg_ref, db_ref):
        @pl.when(pl.program_id(0) == 0)
        def _():
            dg_ref[...] = jnp.zeros_like(dg_ref)
            db_ref[...] = jnp.zeros_like(db_ref)

        u = ALPHA * x_ref[...] + f_ref[...]
        mu = jnp.mean(u, axis=-1, keepdims=True)
        d = u - mu
        var = jnp.mean(d * d, axis=-1, keepdims=True)
        rstd = lax.rsqrt(var + EPS)
        xhat = d * rstd
        dy = dy_ref[...]
        dxh = dy * g_ref[...]
        m1 = jnp.mean(dxh, axis=-1, keepdims=True)
        m2 = jnp.mean(dxh * xhat, axis=-1, keepdims=True)
        du_ref[...] = rstd * (dxh - m1 - xhat * m2)
        dg_ref[...] += jnp.sum(dy * xhat, axis=0, keepdims=True)
        db_ref[...] += jnp.sum(dy, axis=0, keepdims=True)

    row = pl.BlockSpec((tb, D), lambda i: (i, 0))
    par = pl.BlockSpec((1, D), lambda i: (0, 0))
    return pl.pallas_call(
        body, name=name, grid=(t // tb,), in_specs=[row, row, row, par], out_specs=[row, par, par],
        out_shape=[jax.ShapeDtypeStruct((t, D), F32), jax.ShapeDtypeStruct((1, D), F32),
                   jax.ShapeDtypeStruct((1, D), F32)],
        compiler_params=_cp("arbitrary"),
    )(x, f, dy, g)


FFN_TM, FFN_TN = 1024, D_FF // 2


def _ffn_in_swiglu(x, w, name):
    t = x.shape[0]
    tm = min(FFN_TM, t)
    nj = D_FF // FFN_TN

    def body(x_ref, wg_ref, wu_ref, g_ref, u_ref, a_ref):
        xb = x_ref[...].astype(BF16)
        g = lax.dot_general(xb, wg_ref[...], _NT, preferred_element_type=F32)
        u = lax.dot_general(xb, wu_ref[...], _NT, preferred_element_type=F32)
        g_ref[...] = g.astype(BF16)
        u_ref[...] = u.astype(BF16)
        a_ref[...] = (g * _sigmoid(g) * u).astype(BF16)

    tile = pl.BlockSpec((tm, FFN_TN), lambda i, j: (i, j))
    return pl.pallas_call(
        body, name=name, grid=(t // tm, nj),
        in_specs=[pl.BlockSpec((tm, D), lambda i, j: (i, 0)), pl.BlockSpec((FFN_TN, D), lambda i, j: (j, 0)),
                  pl.BlockSpec((FFN_TN, D), lambda i, j: (nj + j, 0))],
        out_specs=[tile, tile, tile],
        out_shape=[jax.ShapeDtypeStruct((t, D_FF), BF16)] * 3,
        compiler_params=_cp("parallel", "parallel"),
    )(x, w, w)


def _d_swiglu(du, w_down, g, u, name):
    t = du.shape[0]
    tm = min(FFN_TM, t)

    def body(du_ref, w_ref, g_ref, u_ref, dg_ref, dup_ref):
        da = lax.dot_general(du_ref[...].astype(BF16), w_ref[...], _NT, preferred_element_type=F32)
        g_v = g_ref[...].astype(F32)
        s = _sigmoid(g_v)
        dg_ref[...] = (da * u_ref[...].astype(F32) * (s * (1.0 + g_v * (1.0 - s)))).astype(BF16)
        dup_ref[...] = (da * g_v * s).astype(BF16)

    tile = pl.BlockSpec((tm, FFN_TN), lambda i, j: (i, j))
    return pl.pallas_call(
        body, name=name, grid=(t // tm, D_FF // FFN_TN),
        in_specs=[pl.BlockSpec((tm, D), lambda i, j: (i, 0)), pl.BlockSpec((FFN_TN, D), lambda i, j: (j, 0)), tile, tile],
        out_specs=[tile, tile],
        out_shape=[jax.ShapeDtypeStruct((t, D_FF), BF16), jax.ShapeDtypeStruct((t, D_FF), BF16)],
        compiler_params=_cp("parallel", "parallel"),
    )(du, w_down, g, u)


def _merge_fwd(pgl, bg, b1, b2, b3, name, tb=512):
    t = pgl.shape[0]
    tb = min(tb, t)

    def body(gl_ref, bg_ref, b1_ref, b2_ref, b3_ref, o_ref):
        acc = None
        for j, b_ref in enumerate((b1_ref, b2_ref, b3_ref)):
            sl = slice(j * D, (j + 1) * D)
            term = _sigmoid(gl_ref[:, sl].astype(F32) + bg_ref[:, sl]) * b_ref[...].astype(F32)
            acc = term if acc is None else acc + term
        o_ref[...] = acc.astype(BF16)

    row = pl.BlockSpec((tb, D), lambda i: (i, 0))
    return pl.pallas_call(
        body, name=name, grid=(t // tb,),
        in_specs=[pl.BlockSpec((tb, 3 * D), lambda i: (i, 0)), pl.BlockSpec((1, 3 * D), lambda i: (0, 0)), row, row, row],
        out_specs=row, out_shape=jax.ShapeDtypeStruct((t, D), BF16), compiler_params=_cp("parallel"),
    )(pgl, bg, b1, b2, b3)


def _merge_bwd(pgl, bg, b1, b2, b3, dm, name, tb=512):
    t = pgl.shape[0]
    tb = min(tb, t)

    def body(gl_ref, bg_ref, b1_ref, b2_ref, b3_ref, dm_ref, dgl_ref, d1_ref, d2_ref, d3_ref, dbg_ref):
        @pl.when(pl.program_id(0) == 0)
        def _():
            dbg_ref[...] = jnp.zeros_like(dbg_ref)

        dm_v = dm_ref[...]
        for j, (b_ref, d_ref) in enumerate(((b1_ref, d1_ref), (b2_ref, d2_ref), (b3_ref, d3_ref))):
            sl = slice(j * D, (j + 1) * D)
            gate = _sigmoid(gl_ref[:, sl].astype(F32) + bg_ref[:, sl])
            d_ref[...] = (dm_v * gate).astype(BF16)
            dgl = dm_v * b_ref[...].astype(F32) * (gate * (1.0 - gate))
            dgl_ref[:, sl] = dgl.astype(BF16)
            dbg_ref[:, sl] += jnp.sum(dgl, axis=0, keepdims=True)

    row = pl.BlockSpec((tb, D), lambda i: (i, 0))
    wide = pl.BlockSpec((tb, 3 * D), lambda i: (i, 0))
    par = pl.BlockSpec((1, 3 * D), lambda i: (0, 0))
    return pl.pallas_call(
        body, name=name, grid=(t // tb,),
        in_specs=[wide, par, row, row, row, row], out_specs=[wide, row, row, row, par],
        out_shape=[jax.ShapeDtypeStruct((t, 3 * D), BF16)] + [jax.ShapeDtypeStruct((t, D), BF16)] * 3
                  + [jax.ShapeDtypeStruct((1, 3 * D), F32)],
        compiler_params=_cp("arbitrary"),
    )(pgl, bg, b1, b2, b3, dm)


def _xa_probs(q, kv_ref, hd):
    sl = slice(hd * XA_HEAD_DIM, (hd + 1) * XA_HEAD_DIM)
    k = kv_ref[:, sl]
    v = kv_ref[:, D + hd * XA_HEAD_DIM:D + (hd + 1) * XA_HEAD_DIM]
    s = _dot(q[:, sl], k, _NT) * (XA_HEAD_DIM ** -0.5)
    e = jnp.exp(s - jnp.max(s, axis=1, keepdims=True))
    return sl, k, v, e / jnp.sum(e, axis=1, keepdims=True)


def _xa_fwd(pq, kv, name, tb=512):
    t = pq.shape[0]
    tb = min(tb, t)

    def body(q_ref, kv_ref, o_ref):
        q = q_ref[...]
        for hd in range(XA_HEADS):
            sl, _, v, p = _xa_probs(q, kv_ref, hd)
            o_ref[:, sl] = _dot(p, v).astype(BF16)

    row = pl.BlockSpec((tb, D), lambda i: (i, 0))
    return pl.pallas_call(
        body, name=name, grid=(t // tb,),
        in_specs=[row, pl.BlockSpec(kv.shape, lambda i: (0, 0))], out_specs=row,
        out_shape=jax.ShapeDtypeStruct((t, D), BF16), compiler_params=_cp("parallel"),
    )(pq, kv)


def _xa_bwd(pq, kv, dy, name, tb=512):
    t = pq.shape[0]
    tb = min(tb, t)

    def body(q_ref, kv_ref, dy_ref, dq_ref, dkv_ref):
        @pl.when(pl.program_id(0) == 0)
        def _():
            dkv_ref[...] = jnp.zeros_like(dkv_ref)

        q = q_ref[...]
        for hd in range(XA_HEADS):
            sl, k, v, p = _xa_probs(q, kv_ref, hd)
            dyh = dy_ref[:, sl]
            vsl = slice(D + hd * XA_HEAD_DIM, D + (hd + 1) * XA_HEAD_DIM)
            dkv_ref[:, vsl] += _dot(p, dyh, _TN)
            dp = _dot(dyh, v, _NT)
            ds = p * (dp - jnp.sum(dp * p, axis=1, keepdims=True)) * (XA_HEAD_DIM ** -0.5)
            dq_ref[:, sl] = _dot(ds, k).astype(BF16)
            dkv_ref[:, sl] += _dot(ds, q[:, sl], _TN)

    row = pl.BlockSpec((tb, D), lambda i: (i, 0))
    kvs = pl.BlockSpec(kv.shape, lambda i: (0, 0))
    return pl.pallas_call(
        body, name=name, grid=(t // tb,), in_specs=[row, kvs, row], out_specs=[row, kvs],
        out_shape=[jax.ShapeDtypeStruct((t, D), BF16), jax.ShapeDtypeStruct(kv.shape, F32)],
        compiler_params=_cp("arbitrary"),
    )(pq, kv, dy)


SUBLANES = 8


def _scan(a, u, reverse):
    n, c = a.shape
    groups = n // SUBLANES
    a = a.reshape(groups, SUBLANES, c)
    u = u.reshape(groups, SUBLANES, c)
    sub = _iota((1, SUBLANES, 1), 1)
    d = 1
    while d < SUBLANES:
        keep = (sub < SUBLANES - d) if reverse else (sub >= d)
        shift = SUBLANES - d if reverse else d
        u = a * jnp.where(keep, pltpu.roll(u, shift, 1), 0.0) + u
        a = a * jnp.where(keep, pltpu.roll(a, shift, 1), 1.0)
        d *= 2
    edge = 0 if reverse else SUBLANES - 1
    out, carry = [None] * groups, None
    for j in (reversed(range(groups)) if reverse else range(groups)):
        out[j] = u[j] if carry is None else u[j] + a[j] * carry
        carry = out[j][edge:edge + 1]
    return jnp.concatenate(out, axis=0)


def _lru_gates(xc, wa_ref, ba, wi_ref, bi, lam):
    za = jnp.concatenate([_dot(xc[:, n * 128:(n + 1) * 128], wa_ref[n]) for n in range(LRU_BLOCKS)], axis=1) + ba
    zi = jnp.concatenate([_dot(xc[:, n * 128:(n + 1) * 128], wi_ref[n]) for n in range(LRU_BLOCKS)], axis=1) + bi
    r = 1.0 / (1.0 + jnp.exp(-za))
    ig = _sigmoid(zi)
    sp = _softplus(-lam)
    log_a = (-LRU_C) * r * sp
    a = jnp.exp(log_a)
    m = jnp.sqrt(-jnp.tanh(log_a) * (1.0 + a * a))
    u = m * (ig * xc)
    return a, u, r, ig, m, sp


def _lru_fwd(p, cw, cb, wa, ba, wi, bi, lam, name, tb=256):
    t = p.shape[0]
    tb = min(tb, t)
    nb = t // tb
    r8 = tb // 8

    def body(x_ref, xp_ref, g_ref, cw_ref, cb_ref, wa_ref, ba_ref, wi_ref, bi_ref, lam_ref, y_ref, h_ref, xc_ref,
             hc_ref):
        i = pl.program_id(0)

        @pl.when(i == 0)
        def _():
            hc_ref[...] = jnp.zeros_like(hc_ref)

        halo = jnp.where(i == 0, 0.0, xp_ref[...])
        taps = _conv_taps(jnp.concatenate([halo, x_ref[...]], axis=0), tb)
        xc = _conv_fwd(taps, cw_ref[...], cb_ref[...])
        xc_ref[...] = xc
        a, u, _, _, _, _ = _lru_gates(xc, wa_ref, ba_ref[...], wi_ref, bi_ref[...], lam_ref[...])
        row = _iota((tb, 1), 0)
        u = u + jnp.where(row == 0, a * hc_ref[...], 0.0)
        h = _scan(a, u, reverse=False)
        h_ref[...] = h
        hc_ref[...] = h[tb - 1:tb, :]
        gl, _ = _gelu_and_grad(g_ref[...])
        y_ref[...] = (gl * h).astype(BF16)

    par = pl.BlockSpec((1, D), lambda i: (0, 0))
    wsp = pl.BlockSpec((LRU_BLOCKS, LRU_BLOCK, LRU_BLOCK), lambda i: (0, 0, 0))
    row = pl.BlockSpec((tb, D), lambda i: (i, 0))
    return pl.pallas_call(
        body, name=name, grid=(nb,),
        in_specs=[row, pl.BlockSpec((8, D), lambda i: (jnp.maximum(i * r8 - 1, 0), 0)),
                  pl.BlockSpec((tb, D), lambda i: (i, 1)),
                  pl.BlockSpec((4, D), lambda i: (0, 0)), par, wsp, par, wsp, par, par],
        out_specs=[row, row, row],
        out_shape=[jax.ShapeDtypeStruct((t, D), BF16), jax.ShapeDtypeStruct((t, D), F32),
                   jax.ShapeDtypeStruct((t, D), F32)],
        scratch_shapes=[pltpu.VMEM((1, D), F32)],
        compiler_params=_cp("arbitrary"),
    )(p, p, p, cw, cb, wa, ba, wi, bi, lam)


def _lru_bwd(p, xc, h, dy, cw, wa, ba, wi, bi, lam, name, tb=256):
    t = p.shape[0]
    tb = min(tb, t)
    nb = t // tb
    r8 = tb // 8

    def body(x_ref, g_ref, xc_ref, h_ref, hp_ref, dy_ref, cw_ref, wa_ref, ba_ref, wi_ref, bi_ref, lam_ref,
             dp_ref, dcw_ref, dcb_ref, dwa_ref, dba_ref, dwi_ref, dbi_ref, dlam_ref, carry_ref, dnext_ref):
        i = pl.program_id(0)
        blk = nb - 1 - i

        @pl.when(i == 0)
        def _():
            for r in (dcw_ref, dcb_ref, dwa_ref, dba_ref, dwi_ref, dbi_ref, dlam_ref, carry_ref, dnext_ref):
                r[...] = jnp.zeros_like(r)

        xc = xc_ref[...]
        lam = lam_ref[...]
        a, _, r, ig, m, sp = _lru_gates(xc, wa_ref, ba_ref[...], wi_ref, bi_ref[...], lam)
        gl, dgl = _gelu_and_grad(g_ref[...])
        h = h_ref[...]
        dy = dy_ref[...]
        dp_ref[:, D:] = (dy * h * dgl).astype(BF16)
        row = _iota((tb, 1), 0)
        dh = dy * gl + jnp.where(row == tb - 1, carry_ref[...], 0.0)
        b = jnp.where(row < tb - 1, pltpu.roll(a, tb - 1, 0), 0.0)
        gs = _scan(b, dh, reverse=True)
        carry_ref[...] = a[0:1] * gs[0:1]
        h_last = jnp.where(blk == 0, 0.0, hp_ref[7:8, :])
        hprev = jnp.where(row == 0, h_last, pltpu.roll(h, 1, 0))
        da = gs * hprev
        dm = gs * ig * xc
        di = gs * m * xc
        dxc = gs * m * ig
        dlog = a * (da - a * (dm / m))
        dr = dlog * ((-LRU_C) * sp)
        dsp = jnp.sum(dlog * ((-LRU_C) * r), axis=0, keepdims=True)
        dlam_ref[...] += dsp * (-_sigmoid(-lam))
        dza = dr * r * (1.0 - r)
        dzi = di * ig * (1.0 - ig)
        dba_ref[...] += jnp.sum(dza, axis=0, keepdims=True)
        dbi_ref[...] += jnp.sum(dzi, axis=0, keepdims=True)
        parts = []
        for n in range(LRU_BLOCKS):
            sl = slice(n * 128, (n + 1) * 128)
            dwa_ref[n] += _dot(xc[:, sl], dza[:, sl], _TN)
            dwi_ref[n] += _dot(xc[:, sl], dzi[:, sl], _TN)
            parts.append(_dot(dza[:, sl], wa_ref[n], _NT) + _dot(dzi[:, sl], wi_ref[n], _NT))
        dxc = dxc + jnp.concatenate(parts, axis=1)
        dx, dcw, dcb = _conv_bwd(dxc, dnext_ref[...], x_ref[...], cw_ref[...], tb)
        dp_ref[:, :D] = dx.astype(BF16)
        dcw_ref[...] += dcw
        dcb_ref[...] += dcb
        dnext_ref[...] = dxc[0:8]

    par = pl.BlockSpec((1, D), lambda i: (0, 0))
    wsp = pl.BlockSpec((LRU_BLOCKS, LRU_BLOCK, LRU_BLOCK), lambda i: (0, 0, 0))
    cws = pl.BlockSpec((4, D), lambda i: (0, 0))
    rev = lambda i: nb - 1 - i
    blk0 = pl.BlockSpec((tb, D), lambda i: (rev(i), 0))
    w_shape = jax.ShapeDtypeStruct((LRU_BLOCKS, LRU_BLOCK, LRU_BLOCK), F32)
    v_shape = jax.ShapeDtypeStruct((1, D), F32)
    return pl.pallas_call(
        body, name=name, grid=(nb,),
        in_specs=[blk0, pl.BlockSpec((tb, D), lambda i: (rev(i), 1)), blk0, blk0,
                  pl.BlockSpec((8, D), lambda i: (jnp.maximum(rev(i) * r8 - 1, 0), 0)), blk0,
                  cws, wsp, par, wsp, par, par],
        out_specs=[pl.BlockSpec((tb, 2 * D), lambda i: (rev(i), 0)), cws, par, wsp, par, wsp, par, par],
        out_shape=[jax.ShapeDtypeStruct((t, 2 * D), BF16), jax.ShapeDtypeStruct((4, D), F32), v_shape,
                   w_shape, v_shape, w_shape, v_shape, v_shape],
        scratch_shapes=[pltpu.VMEM((1, D), F32), pltpu.VMEM((8, D), F32)],
        compiler_params=_cp("arbitrary"),
    )(p, p, xc, h, h, dy, cw, wa, ba, wi, bi, lam)


def _ssd_consts():
    m0 = _iota((1, 128), 1) < 64
    e = (jnp.right_shift(_iota((SSD_HEADS, D_SSD), 1), 6) == _iota((SSD_HEADS, D_SSD), 0)).astype(BF16)
    tril = (_iota((CHUNK, CHUNK), 0) >= _iota((CHUNK, CHUNK), 1)).astype(F32)
    eye = (_iota((SSD_HEADS, SSD_HEADS), 0) == _iota((SSD_HEADS, SSD_HEADS), 1)).astype(F32)
    r2 = _iota((CHUNK, 128), 0)
    c2 = jnp.bitwise_and(_iota((CHUNK, 128), 1), 63)
    return dict(m0=m0, e=e, tril=tril, eye=eye, causal2=r2 >= c2, fold=(c2 == r2).astype(BF16))


SSD_STEP = 4


def _ssd_pre(c, dt_raw, dtb_ref, alog_ref, dvec_ref, k):
    sg = _sigmoid(c)
    xbc = c * sg
    dtp = dt_raw + dtb_ref[...]
    dt = _softplus(dtp)
    a = -jnp.exp(alog_ref[...])
    cs = _dot_hi(k["tril"], dt * a)
    cs_last = cs[CHUNK - 1:CHUNK]
    dend = jnp.exp(cs_last - cs)
    cdec = jnp.exp(cs_last)
    big = _dot01(jnp.concatenate([dt, jnp.exp(cs), dend], axis=0), k["e"])
    small = _dot01(jnp.concatenate([jnp.broadcast_to(cdec, (8, SSD_HEADS)),
                                    jnp.broadcast_to(dvec_ref[...], (8, SSD_HEADS))], axis=0), k["e"])
    cst2 = _dot_hi(k["eye"], jnp.concatenate([cs, cs], axis=0), _NT)
    return dict(c=c, sg=sg, xs=xbc[:, :D_SSD], bm=xbc[:, D_SSD:D_SSD + 512],
                cm=xbc[:, D_SSD + 512:], dtp=dtp, dt=dt, a=a, cs=cs, dend=dend, cdec=cdec,
                dtx=big[0:CHUNK], ecx=big[CHUNK:2 * CHUNK], dex=big[2 * CHUNK:3 * CHUNK],
                cdx=small[0:1], ddx=small[8:9], cst2=cst2)


def _pair_decay(p, cs, cst2, k):
    h0, h1 = 2 * p, 2 * p + 1
    colp = jnp.where(k["m0"], cs[:, h0:h0 + 1], cs[:, h1:h1 + 1])
    rowp = jnp.where(k["m0"], cst2[h0:h0 + 1, :], cst2[h1:h1 + 1, :])
    return jnp.where(k["causal2"], jnp.exp(colp - rowp), 0.0)


def _pair_stack(xp, k):
    return jnp.concatenate([jnp.where(k["m0"], xp, 0.0), jnp.where(k["m0"], 0.0, xp)], axis=0)


def _group_norm(yz, nw, with_stats=False):
    outs, stats = [], []
    for g in range(SSD_GROUPS):
        yzg = yz[:, g * GROUP_W:(g + 1) * GROUP_W]
        r = lax.rsqrt(jnp.mean(yzg * yzg, axis=1, keepdims=True) + EPS)
        outs.append(yzg * r)
        stats.append(r)
    y = jnp.concatenate(outs, axis=1) * nw
    return (y, stats) if with_stats else y


def _ssd_fwd(p, cw, cb, dtb, alog, dvec, nw, name):
    t = p.shape[0]
    step = SSD_STEP if t % (SSD_STEP * CHUNK) == 0 else 1
    rows_blk, nb, nc = step * CHUNK, t // (step * CHUNK), t // CHUNK

    def body(p_blk, cw_ref, cb_ref, dtb_ref, alog_ref, dvec_ref, nw_ref, y_blk, yraw_blk, hs_blk, c_blk,
             h_scr, tail_scr):
        @pl.when(pl.program_id(0) == 0)
        def _():
            h_scr[...] = jnp.zeros_like(h_scr)
            tail_scr[...] = jnp.zeros_like(tail_scr)

        k = _ssd_consts()

        def one_chunk(j, carry):
            rows = pl.ds(pl.multiple_of(j * CHUNK, CHUNK), CHUNK)
            chunk(p_blk.at[rows], y_blk.at[rows], yraw_blk.at[rows], hs_blk.at[j], c_blk.at[rows], k,
                  cw_ref, cb_ref, dtb_ref, alog_ref, dvec_ref, nw_ref, h_scr, tail_scr)
            return carry

        lax.fori_loop(0, step, one_chunk, 0)

    def chunk(p_ref, y_ref, yraw_ref, hs_ref, c_ref, k, cw_ref, cb_ref, dtb_ref, alog_ref, dvec_ref, nw_ref,
              h_scr, tail_scr):
        x_in = p_ref[:, S_XBC:S_DT]
        taps = _conv_taps(jnp.concatenate([tail_scr[...], x_in], axis=0), CHUNK)
        tail_scr[...] = x_in[CHUNK - 8:]
        c = _conv_fwd(taps, cw_ref[...], cb_ref[...])
        c_ref[...] = c
        s = _ssd_pre(c, p_ref[:, S_DT:S_DT + DT_REAL], dtb_ref, alog_ref, dvec_ref, k)
        xs, bm, cm = s["xs"], s["bm"], s["cm"]
        xdt = xs * s["dtx"]
        hprev = h_scr[...]
        hs_ref[...] = hprev
        ys, hn = [], []
        for g in range(SSD_GROUPS):
            gs = slice(g * GROUP_W, (g + 1) * GROUP_W)
            bg = bm[:, g * 128:(g + 1) * 128]
            cg = cm[:, g * 128:(g + 1) * 128]
            cbdup = _dot(cg, jnp.concatenate([bg, bg], axis=0), _NT)
            hp_g = hprev[:, gs]
            yd = []
            for q in range(4):
                pr = g * 4 + q
                mp = cbdup * _pair_decay(pr, s["cs"], s["cst2"], k)
                yd.append(_dot(mp, _pair_stack(xdt[:, pr * 128:(pr + 1) * 128], k)))
            ys.append(jnp.concatenate(yd, axis=1) + _dot(cg, hp_g) * s["ecx"][:, gs])
            hn.append(hp_g * s["cdx"][:, gs] + _dot(bg, xdt[:, gs] * s["dex"][:, gs], _TN))
        h_scr[...] = jnp.concatenate(hn, axis=1)
        yraw = jnp.concatenate(ys, axis=1) + s["ddx"] * xs
        yraw_ref[...] = yraw
        z = p_ref[:, S_Z:S_Z + D_SSD]
        y_ref[...] = _group_norm(yraw * (z * _sigmoid(z)), nw_ref[...]).astype(BF16)

    hv = pl.BlockSpec((1, DT_REAL), lambda i: (0, 0))
    return pl.pallas_call(
        body, name=name, grid=(nb,),
        in_specs=[pl.BlockSpec((rows_blk, W_SSD), lambda i: (i, 0)),
                  pl.BlockSpec((4, D_XBC), lambda i: (0, 0)), pl.BlockSpec((1, D_XBC), lambda i: (0, 0)),
                  hv, hv, hv, pl.BlockSpec((1, D_SSD), lambda i: (0, 0))],
        out_specs=[pl.BlockSpec((rows_blk, D_SSD), lambda i: (i, 0)), pl.BlockSpec((rows_blk, D_SSD), lambda i: (i, 0)),
                   pl.BlockSpec((step, SSD_STATE, D_SSD), lambda i: (i, 0, 0)),
                   pl.BlockSpec((rows_blk, D_XBC), lambda i: (i, 0))],
        out_shape=[jax.ShapeDtypeStruct((t, D_SSD), BF16), jax.ShapeDtypeStruct((t, D_SSD), F32),
                   jax.ShapeDtypeStruct((nc, SSD_STATE, D_SSD), F32), jax.ShapeDtypeStruct((t, D_XBC), F32)],
        scratch_shapes=[pltpu.VMEM((SSD_STATE, D_SSD), F32), pltpu.VMEM((8, D_XBC), F32)],
        compiler_params=_cp("arbitrary"),
    )(p, cw, cb, dtb, alog, dvec, nw)


def _ssd_bwd(p, c, yraw, hs, dy, cw, dtb, alog, dvec, nw, name):
    t = p.shape[0]
    step = 1
    rows_blk, nb = step * CHUNK, t // (step * CHUNK)

    def body(p_blk, c_blk, yraw_blk, hs_blk, dy_blk, cw_ref, dtb_ref, alog_ref, dvec_ref, nw_ref,
             dp_blk, dcw_ref, dcb_ref, ddtb_ref, dalog_ref, dd_ref, dnw_ref, dh_scr, dnext_scr):
        @pl.when(pl.program_id(0) == 0)
        def _():
            for r in (dcw_ref, dcb_ref, ddtb_ref, dalog_ref, dd_ref, dnw_ref, dh_scr, dnext_scr):
                r[...] = jnp.zeros_like(r)

        k = _ssd_consts()

        def one_chunk(jj, carry):
            j = step - 1 - jj
            rows = pl.ds(pl.multiple_of(j * CHUNK, CHUNK), CHUNK)
            chunk(p_blk.at[rows], c_blk.at[rows], yraw_blk.at[rows], hs_blk.at[j], dy_blk.at[rows], dp_blk.at[rows], k,
                  cw_ref, dtb_ref, alog_ref, dvec_ref, nw_ref, dcw_ref, dcb_ref, ddtb_ref, dalog_ref, dd_ref, dnw_ref,
                  dh_scr, dnext_scr)
            return carry

        lax.fori_loop(0, step, one_chunk, 0)

    def chunk(p_ref, c_ref, yraw_ref, hs_ref, dy_ref, dp_ref, k, cw_ref, dtb_ref, alog_ref, dvec_ref, nw_ref,
              dcw_ref, dcb_ref, ddtb_ref, dalog_ref, dd_ref, dnw_ref, dh_scr, dnext_scr):
        s = _ssd_pre(c_ref[...], p_ref[:, S_DT:S_DT + DT_REAL], dtb_ref, alog_ref, dvec_ref, k)
        xs, bm, cm, cs, dt, a = s["xs"], s["bm"], s["cm"], s["cs"], s["dt"], s["a"]
        m0 = k["m0"]
        xdt = xs * s["dtx"]
        hprev = hs_ref[...]
        dh = dh_scr[...]

        nw_v = nw_ref[...]
        yraw = yraw_ref[...]
        z = p_ref[:, S_Z:S_Z + D_SSD]
        sz = _sigmoid(z)
        siluz = z * sz
        yz = yraw * siluz
        dyo = dy_ref[...]
        dyn = dyo * nw_v
        dyz_parts, dnw_parts = [], []
        for g in range(SSD_GROUPS):
            gs = slice(g * GROUP_W, (g + 1) * GROUP_W)
            yzg = yz[:, gs]
            r = lax.rsqrt(jnp.mean(yzg * yzg, axis=1, keepdims=True) + EPS)
            dnw_parts.append(jnp.sum(dyo[:, gs] * yzg * r, axis=0, keepdims=True))
            dyz_parts.append(r * dyn[:, gs] - yzg * (r * r * r) * jnp.mean(dyn[:, gs] * yzg, axis=1, keepdims=True))
        dnw_ref[...] += jnp.concatenate(dnw_parts, axis=1)
        dyz = jnp.concatenate(dyz_parts, axis=1)
        d_y = dyz * siluz
        dp_ref[:, S_Z:S_Z + D_SSD] = (dyz * yraw * (sz * (1.0 + z * (1.0 - sz)))).astype(BF16)
        dd_row = jnp.sum(d_y * xs, axis=0, keepdims=True)
        dxs = d_y * s["ddx"]

        lane_h = _iota((1, SSD_HEADS), 1)
        sub_h = _iota((SSD_HEADS, 1), 0)
        dcs = jnp.zeros((CHUNK, SSD_HEADS), F32)
        dcst2 = jnp.zeros((SSD_HEADS, 128), F32)
        dxdt_parts, db_parts, dc_parts, dhp_parts, yoff_parts, dend_parts, dcd_parts = [], [], [], [], [], [], []
        for g in range(SSD_GROUPS):
            gs = slice(g * GROUP_W, (g + 1) * GROUP_W)
            bg = bm[:, g * 128:(g + 1) * 128]
            cg = cm[:, g * 128:(g + 1) * 128]
            bdup = jnp.concatenate([bg, bg], axis=0)
            cbdup = _dot(cg, bdup, _NT)
            dcb2 = jnp.zeros((CHUNK, 128), F32)
            dxp_parts = []
            for q in range(4):
                pr = g * 4 + q
                h0, h1 = 2 * pr, 2 * pr + 1
                lp = _pair_decay(pr, cs, s["cst2"], k)
                mp = cbdup * lp
                xst = _pair_stack(xdt[:, pr * 128:(pr + 1) * 128], k)
                dyp = d_y[:, pr * 128:(pr + 1) * 128]
                dmp = _dot(dyp, xst, _NT)
                dxst = _dot(mp, dyp, _TN)
                dxp_parts.append(jnp.where(m0, dxst[:CHUNK], dxst[CHUNK:]))
                dcb2 = dcb2 + dmp * lp
                dlm = dmp * mp
                rs0 = jnp.sum(jnp.where(m0, dlm, 0.0), axis=1, keepdims=True)
                rs1 = jnp.sum(jnp.where(m0, 0.0, dlm), axis=1, keepdims=True)
                dcs = dcs + jnp.where(lane_h == h0, rs0, 0.0) + jnp.where(lane_h == h1, rs1, 0.0)
                colsum = jnp.sum(dlm, axis=0, keepdims=True)
                sel = ((sub_h == h0) & m0) | ((sub_h == h1) & jnp.logical_not(m0))
                dcst2 = dcst2 - jnp.where(sel, colsum, 0.0)
            dcg = _dot(dcb2, bdup)
            dbdup = _dot(dcb2, cg, _TN)
            dbg = dbdup[:CHUNK] + dbdup[CHUNK:]
            hp_g = hprev[:, gs]
            zoff = _dot(cg, hp_g)
            dzo = d_y[:, gs] * s["ecx"][:, gs]
            dcg = dcg + _dot(dzo, hp_g, _NT)
            dh_g = dh[:, gs]
            dhp_parts.append(_dot(cg, dzo, _TN) + dh_g * s["cdx"][:, gs])
            dcd_parts.append(jnp.sum(dh_g * hp_g, axis=0, keepdims=True))
            wg = xdt[:, gs] * s["dex"][:, gs]
            dbg = dbg + _dot(wg, dh_g, _NT)
            dwg = _dot(bg, dh_g)
            dxdt_parts.append(jnp.concatenate(dxp_parts, axis=1) + dwg * s["dex"][:, gs])
            dend_g = dwg * wg
            dend_parts.append(jnp.sum(dend_g, axis=0, keepdims=True))
            yoff_parts.append(dzo * zoff - dend_g)
            db_parts.append(dbg)
            dc_parts.append(dcg)
        dh_scr[...] = jnp.concatenate(dhp_parts, axis=1)
        dxdt = jnp.concatenate(dxdt_parts, axis=1)
        sums = _dot01(jnp.concatenate([jnp.concatenate(yoff_parts, axis=1), dxdt * xs], axis=0), k["e"], _NT)
        rows8 = jnp.concatenate([jnp.broadcast_to(jnp.concatenate(r, axis=1), (8, D_SSD))
                                 for r in (dcd_parts, [dd_row], dend_parts)], axis=0)
        small = _dot01(rows8, k["e"], _NT)
        dd_ref[...] += small[8:9]
        dcs_last = small[0:1] * s["cdec"] + small[16:17]
        hi, lo = _split(dcst2)
        dcs = (dcs + sums[0:CHUNK]
               + lax.dot_general(k["fold"], hi, _NT, preferred_element_type=F32)
               + lax.dot_general(k["fold"], lo, _NT, preferred_element_type=F32)
               + jnp.where(_iota((CHUNK, 1), 0) == CHUNK - 1, dcs_last, 0.0))
        dda = _dot_hi(k["tril"], dcs, _TN)
        ddt = dda * a + sums[CHUNK:2 * CHUNK]
        dalog_ref[...] += jnp.sum(dda * dt, axis=0, keepdims=True) * a
        dxs = dxs + dxdt * s["dtx"]
        draw = ddt * _sigmoid(s["dtp"])
        ddtb_ref[...] += jnp.sum(draw, axis=0, keepdims=True)
        dp_ref[:, S_DT:] = jnp.zeros((CHUNK, W_SSD - S_DT), BF16)
        dp_ref[:, S_DT:S_DT + DT_REAL] = draw.astype(BF16)
        dxbc = jnp.concatenate([dxs] + db_parts + dc_parts, axis=1)
        sg, c = s["sg"], s["c"]
        dc = dxbc * (sg * (1.0 + c * (1.0 - sg)))
        dx, dcw, dcb = _conv_bwd(dc, dnext_scr[...], p_ref[:, S_XBC:S_DT], cw_ref[...], CHUNK)
        dp_ref[:, S_XBC:S_DT] = dx.astype(BF16)
        dcw_ref[...] += dcw
        dcb_ref[...] += dcb
        dnext_scr[...] = dc[0:8]

    rev = lambda i: nb - 1 - i
    hv = pl.BlockSpec((1, DT_REAL), lambda i: (0, 0))
    cws = pl.BlockSpec((4, D_XBC), lambda i: (0, 0))
    cbs = pl.BlockSpec((1, D_XBC), lambda i: (0, 0))
    nws = pl.BlockSpec((1, D_SSD), lambda i: (0, 0))
    wide = pl.BlockSpec((rows_blk, D_SSD), lambda i: (rev(i), 0))
    hshape = jax.ShapeDtypeStruct((1, DT_REAL), F32)
    return pl.pallas_call(
        body, name=name, grid=(nb,),
        in_specs=[pl.BlockSpec((rows_blk, W_SSD), lambda i: (rev(i), 0)),
                  pl.BlockSpec((rows_blk, D_XBC), lambda i: (rev(i), 0)),
                  wide, pl.BlockSpec((step, SSD_STATE, D_SSD), lambda i: (rev(i), 0, 0)), wide,
                  cws, hv, hv, hv, nws],
        out_specs=[pl.BlockSpec((rows_blk, W_SSD), lambda i: (rev(i), 0)), cws, cbs, hv, hv, hv, nws],
        out_shape=[jax.ShapeDtypeStruct((t, W_SSD), BF16), jax.ShapeDtypeStruct((4, D_XBC), F32),
                   jax.ShapeDtypeStruct((1, D_XBC), F32), hshape, hshape, hshape,
                   jax.ShapeDtypeStruct((1, D_SSD), F32)],
        scratch_shapes=[pltpu.VMEM((SSD_STATE, D_SSD), F32), pltpu.VMEM((8, D_XBC), F32)],
        compiler_params=_cp("arbitrary"),
    )(p, c, yraw, hs, dy, cw, dtb, alog, dvec, nw)


def _loss_head(y, target, name, tb=512):
    t = y.shape[0]
    tb = min(tb, t)

    def body(y_ref, t_ref, dy_ref, l_ref):
        @pl.when(pl.program_id(0) == 0)
        def _():
            l_ref[...] = jnp.zeros_like(l_ref)

        e = y_ref[...] - t_ref[...]
        dy_ref[...] = e * (1.0 / D)
        l_ref[...] += jnp.sum(jnp.sum(e * e, axis=1, keepdims=True), axis=0, keepdims=True) * (0.5 / D)

    row = pl.BlockSpec((tb, D), lambda i: (i, 0))
    return pl.pallas_call(
        body, name=name, grid=(t // tb,), in_specs=[row, row],
        out_specs=[row, pl.BlockSpec((8, 128), lambda i: (0, 0))],
        out_shape=[jax.ShapeDtypeStruct((t, D), F32), jax.ShapeDtypeStruct((8, 128), F32)],
        compiler_params=_cp("arbitrary"),
    )(y, target)


def _adamw(slots, w, m, v, name, tb):
    nl = len(slots)
    ns, r, c = slots[0].shape
    assert r % tb == 0 and w.shape == (nl, r, c), (r, tb, w.shape)

    def body(*refs):
        s_refs = refs[:nl]
        w_ref, m_ref, v_ref, g_ref, d_ref, m2_ref, v2_ref = refs[nl:]

        def total(ref):
            acc = ref[0].astype(F32)
            for j in range(1, ns):
                acc = acc + ref[j].astype(F32)
            return acc

        g = total(s_refs[0])
        for layer in range(1, nl):
            g = jnp.where(pl.program_id(0) == layer, total(s_refs[layer]), g)
        m2 = ADAM_B1 * m_ref[...] + (1.0 - ADAM_B1) * g
        v2 = ADAM_B2 * v_ref[...] + (1.0 - ADAM_B2) * (g * g)
        m_hat = m2 / (1.0 - ADAM_B1 ** ADAM_STEP)
        v_hat = v2 / (1.0 - ADAM_B2 ** ADAM_STEP)
        g_ref[...] = g
        d_ref[...] = -ADAM_LR * (m_hat / (jnp.sqrt(v_hat) + ADAM_EPS) + ADAM_WD * w_ref[...])
        m2_ref[...] = m2
        v2_ref[...] = v2

    def slot_spec(layer):
        return pl.BlockSpec((ns, tb, c), lambda l, i: (0, jnp.where(l == layer, i, 0), 0))

    row = pl.BlockSpec((None, tb, c), lambda l, i: (l, i, 0))
    shp = jax.ShapeDtypeStruct((nl, r, c), F32)
    return pl.pallas_call(
        body, name=name, grid=(nl, r // tb),
        in_specs=[slot_spec(layer) for layer in range(nl)] + [row, row, row],
        out_specs=[row, row, row, row], out_shape=[shp, shp, shp, shp], compiler_params=_cp("arbitrary", "arbitrary"),
    )(*slots, w, m, v)


def _pair_sum(own, got, name, out_dtype, tb):
    nj, _, r, c = own.shape
    mc = lax.axis_index("c")

    def body(mc_ref, a_ref, b_ref, o_ref):
        del mc_ref
        o_ref[...] = (a_ref[...] + b_ref[...]).astype(out_dtype)

    return pl.pallas_call(
        body, name=name,
        grid_spec=pltpu.PrefetchScalarGridSpec(
            num_scalar_prefetch=1, grid=(nj, r // tb),
            in_specs=[pl.BlockSpec((None, None, tb, c), lambda j, i, mc_ref: (j, mc_ref[0], i, 0)),
                      pl.BlockSpec((None, tb, c), lambda j, i, mc_ref: (j, i, 0))],
            out_specs=pl.BlockSpec((None, tb, c), lambda j, i, mc_ref: (j, i, 0))),
        out_shape=jax.ShapeDtypeStruct((nj, r, c), out_dtype), compiler_params=_cp("parallel", "parallel"),
    )(jnp.reshape(mc, (1,)).astype(jnp.int32), own, got)


def _slot_sum(slots, name):
    ns, r, c = slots.shape

    def body(s_ref, o_ref):
        g = s_ref[0]
        for j in range(1, ns):
            g = g + s_ref[j]
        o_ref[...] = g

    return pl.pallas_call(body, name=name, out_shape=jax.ShapeDtypeStruct((r, c), F32))(slots)


def _position():
    return lax.axis_index("x"), lax.axis_index("y"), lax.axis_index("c")


def _comm(exchange, peers, xs, out_shapes, sems, name, collective_id):
    n = len(xs)
    if collective_id is None:
        def body(*refs):
            exchange(refs[:n], refs[n:n + len(out_shapes)], *refs[n + len(out_shapes):])

        return pl.pallas_call(body, name=name, in_specs=[ANY] * n, out_specs=[ANY] * len(out_shapes),
                              out_shape=out_shapes, scratch_shapes=sems)(*xs)
    def launch(*refs):
        barrier = pltpu.get_barrier_semaphore()
        to = peers(*_position())
        for peer in to:
            pl.semaphore_signal(barrier, inc=1, device_id=peer, device_id_type=MESH)
        pl.semaphore_wait(barrier, len(to))
        exchange(refs[:n], refs[n:n + len(out_shapes)], *refs[n + len(out_shapes):])

    return pl.kernel(launch, out_type=out_shapes, mesh=plsc.ScalarSubcoreMesh(axis_name="seq", num_cores=1), name=name,
                     scratch_types=sems, compiler_params=pltpu.CompilerParams(collective_id=collective_id))(*xs)


def _all_gather(xs, name, collective_id=None):
    n = len(xs)
    return _comm(_gather_body, lambda x, y, c: [(x, y, 1 - c), (1 - x, y, c), (x, 1 - y, c), (1 - x, 1 - y, c)], xs,
                 [jax.ShapeDtypeStruct((N_DEV,) + x.shape, x.dtype) for x in xs],
                 [pltpu.SemaphoreType.DMA((n, 7)), pltpu.SemaphoreType.DMA((n, 7)), pltpu.SemaphoreType.DMA((n,))],
                 name, collective_id)


def _gather_body(x_refs, out_refs, send_sems, recv_sems, local_sems):
    n = len(x_refs)
    mx, my, mc = _position()
    me, sibling = (mx, my, mc), (mx, my, 1 - mc)
    chips = [(1 - mx, my), (mx, 1 - my), (1 - mx, 1 - my)]

    def copy(a, k, block, to, own=False):
        dst = out_refs[a].at[4 * block[0] + 2 * block[1] + block[2]]
        return pltpu.make_async_remote_copy(
            src_ref=x_refs[a] if own else dst, dst_ref=dst,
            send_sem=send_sems.at[a, k], recv_sem=recv_sems.at[a, k], device_id=to, device_id_type=MESH)

    mine = [pltpu.make_async_copy(x_refs[a], out_refs[a].at[4 * mx + 2 * my + mc], local_sems.at[a]) for a in range(n)]
    first = [copy(a, 1 + j, me, (*chip, mc), own=True) for j, chip in enumerate(chips) for a in range(n)]
    first += [copy(a, 0, me, sibling, own=True) for a in range(n)]
    for cp in first + mine:
        cp.start()
    passed = []
    for j, chip in enumerate(chips):
        for a in range(n):
            copy(a, 1 + j, (*chip, mc), me).wait_recv()
            passed.append(copy(a, 4 + j, (*chip, mc), sibling))
            passed[-1].start()
    for a in range(n):
        copy(a, 0, sibling, me).wait_recv()
    for j, chip in enumerate(chips):
        for a in range(n):
            copy(a, 4 + j, (*chip, 1 - mc), me).wait_recv()
    for cp in first + passed:
        cp.wait_send()
    for cp in mine:
        cp.wait()


def _exchange_sibling(gs, name, collective_id=None):
    n = len(gs)

    def exchange(g_refs, r_refs, send_sems, recv_sems):
        mx, my, mc = _position()
        cps = [pltpu.make_async_remote_copy(src_ref=g_refs[a].at[:, 1 - mc], dst_ref=r_refs[a],
                                            send_sem=send_sems.at[a], recv_sem=recv_sems.at[a],
                                            device_id=(mx, my, 1 - mc), device_id_type=MESH) for a in range(n)]
        for cp in cps:
            cp.start()
        for cp in cps:
            cp.wait()

    return _comm(exchange, lambda x, y, c: [(x, y, 1 - c)], gs,
                 [jax.ShapeDtypeStruct(g.shape[:1] + g.shape[2:], g.dtype) for g in gs],
                 [pltpu.SemaphoreType.DMA((n,)), pltpu.SemaphoreType.DMA((n,))], name, collective_id)


def _exchange_chips(ss, name, collective_id=None):
    n = len(ss)

    def exchange(s_refs, r_refs, send_sems, recv_sems, local_sems):
        mx, my, mc = _position()
        my_chip = 2 * mx + my
        chips = [(1 - mx, my), (mx, 1 - my), (1 - mx, 1 - my)]

        def copy(a, k, to_slot):
            px, py = chips[k]
            return pltpu.make_async_remote_copy(
                src_ref=s_refs[a].at[2 * px + py], dst_ref=r_refs[a].at[to_slot], send_sem=send_sems.at[a, k],
                recv_sem=recv_sems.at[a, k], device_id=(px, py, mc), device_id_type=MESH)

        sends = [copy(a, k, my_chip) for k in range(3) for a in range(n)]
        local = [pltpu.make_async_copy(s_refs[a].at[my_chip], r_refs[a].at[my_chip], local_sems.at[a])
                 for a in range(n)]
        for cp in sends + local:
            cp.start()
        for k in range(3):
            px, py = chips[k]
            for a in range(n):
                copy(a, k, 2 * px + py).wait_recv()
        for cp in sends:
            cp.wait_send()
        for cp in local:
            cp.wait()

    return _comm(exchange, lambda x, y, c: [(1 - x, y, c), (x, 1 - y, c), (1 - x, 1 - y, c)], ss,
                 [jax.ShapeDtypeStruct(s.shape, s.dtype) for s in ss],
                 [pltpu.SemaphoreType.DMA((n, 3)), pltpu.SemaphoreType.DMA((n, 3)), pltpu.SemaphoreType.DMA((n,))],
                 name, collective_id)


def _cols_concat(g, name, tb=128):
    _, k_dim, n = g.shape

    def body(g_ref, o_ref):
        o_ref[...] = jnp.concatenate([g_ref[d] for d in range(N_DEV)], axis=1)

    return pl.pallas_call(
        body, name=name, grid=(k_dim // tb,),
        in_specs=[pl.BlockSpec((N_DEV, tb, n), lambda i: (0, i, 0))],
        out_specs=pl.BlockSpec((tb, N_DEV * n), lambda i: (i, 0)),
        out_shape=jax.ShapeDtypeStruct((k_dim, N_DEV * n), g.dtype), compiler_params=_cp("parallel"),
    )(g)


def _cols_split(parts, name, tb=128):
    k_dim = parts[0].shape[0]
    n = sum(p.shape[1] for p in parts) // N_DEV

    def body(*refs):
        full = jnp.concatenate([r[...] for r in refs[:-1]], axis=1)
        for d in range(N_DEV):
            refs[-1][d] = full[:, d * n:(d + 1) * n]

    return pl.pallas_call(
        body, name=name, grid=(k_dim // tb,),
        in_specs=[pl.BlockSpec((tb, p.shape[1]), lambda i: (i, 0)) for p in parts],
        out_specs=pl.BlockSpec((N_DEV, tb, n), lambda i: (0, i, 0)),
        out_shape=jax.ShapeDtypeStruct((N_DEV, k_dim, n), parts[0].dtype), compiler_params=_cp("parallel"),
    )(*parts)


_Q0, _GL0 = 7200, 8224
N_SHARD_IN = N_IN // N_DEV


def _w_in_regions(g, name, tb=128):
    def body(g_ref, ssd_ref, lru_ref, q_ref, gl_ref):
        full = jnp.concatenate([g_ref[d] for d in range(N_DEV)], axis=1)
        lru_ref[...] = full[:, 0:2 * D]
        ssd_ref[:, :S_DT] = full[:, 2 * D:2 * D + S_DT]
        ssd_ref[:, S_DT:] = jnp.zeros((tb, W_SSD - S_DT), g.dtype)
        ssd_ref[:, S_DT:S_DT + DT_REAL] = full[:, 2 * D + S_DT:_Q0]
        q_ref[...] = full[:, _Q0:_GL0]
        gl_ref[...] = full[:, _GL0:N_IN]

    widths = (W_SSD, 2 * D, D, 3 * D)
    return pl.pallas_call(
        body, name=name, grid=(D // tb,),
        in_specs=[pl.BlockSpec((N_DEV, tb, N_SHARD_IN), lambda i: (0, i, 0))],
        out_specs=[pl.BlockSpec((tb, wd), lambda i: (i, 0)) for wd in widths],
        out_shape=[jax.ShapeDtypeStruct((D, wd), g.dtype) for wd in widths], compiler_params=_cp("parallel"),
    )(g)


def _w_in_shards(dssd, dlru, dq, dgl, name, tb=128):
    def body(ssd_ref, lru_ref, q_ref, gl_ref, o_ref):
        full = jnp.concatenate([lru_ref[...], ssd_ref[:, :S_DT + DT_REAL], q_ref[...], gl_ref[...]], axis=1)
        for d in range(N_DEV):
            o_ref[d] = full[:, d * N_SHARD_IN:(d + 1) * N_SHARD_IN]

    return pl.pallas_call(
        body, name=name, grid=(D // tb,),
        in_specs=[pl.BlockSpec((tb, a.shape[1]), lambda i: (i, 0)) for a in (dssd, dlru, dq, dgl)],
        out_specs=pl.BlockSpec((N_DEV, tb, N_SHARD_IN), lambda i: (0, i, 0)),
        out_shape=jax.ShapeDtypeStruct((N_DEV, D, N_SHARD_IN), F32), compiler_params=_cp("parallel"),
    )(dssd, dlru, dq, dgl)


_BIG = (("w_in", "col", (1024, 1412)), ("mem_w_kv", "col", (1024, 256)), ("w_br_lru", "row", (128, 1024)),
        ("w_br_ssd", "row", (256, 1024)), ("w_br_xa", "row", (128, 1024)), ("w_out", "row", (128, 1024)),
        ("ffn_w_in", "row", (704, 1024)), ("ffn_w_down", "row", (352, 1024)))
_TRANSPOSED = ("ffn_w_in",)
_SMALL = (("b_gate", (3, 128)), ("lru_conv_w", (4, 128)), ("ssd_conv_w", (4, 384)))
_REP = (("lru_conv_b", (1024,)), ("lru_w_a", (8, 128, 128)), ("lru_b_a", (1024,)), ("lru_w_i", (8, 128, 128)),
        ("lru_b_i", (1024,)), ("lru_lambda", (1024,)), ("ssd_conv_b", (3072,)), ("ssd_dt_bias", (32,)),
        ("ssd_a_log", (32,)), ("ssd_d", (32,)), ("ssd_norm_w", (2048,)), ("ln1_g", (1024,)), ("ln1_b", (1024,)),
        ("ln2_g", (1024,)), ("ln2_b", (1024,)))
_ORDER = ("w_in", "b_gate", "lru_conv_w", "lru_conv_b", "lru_w_a", "lru_b_a", "lru_w_i", "lru_b_i", "lru_lambda",
          "ssd_conv_w", "ssd_conv_b", "ssd_dt_bias", "ssd_a_log", "ssd_d", "ssd_norm_w", "mem_w_kv", "w_br_lru",
          "w_br_ssd", "w_br_xa", "w_out", "ln1_g", "ln1_b", "ffn_w_in", "ffn_w_down", "ln2_g", "ln2_b")

LANES = 1024
N_SMALL = sum(DEPTH * s[0] * s[1] for _, s in _SMALL)
R_SMALL = 8
N_REP = sum(DEPTH * math.prod(s) for _, s in _REP)
R_REP = 68
R_SM = R_SMALL + R_REP + 4
R_TAIL = R_SMALL + N_DEV * R_REP
TB_TAIL = 184
assert N_SMALL <= R_SMALL * LANES and N_REP + 1 <= N_DEV * R_REP * LANES


def _rows(flat, rows):
    return jnp.pad(flat, (0, rows * LANES - flat.shape[0])).reshape(rows, LANES)


def _rowblk(a, cap):
    return max(b for b in range(16, cap + 1, 16) if a % b == 0)


def _pack_tail(d):
    small = jnp.concatenate([d[n].reshape(-1) for n, _ in _SMALL])
    rep = jnp.concatenate([d[n].reshape(-1) for n, _ in _REP])
    return jnp.concatenate([_rows(small, R_SMALL), _rows(rep, N_DEV * R_REP)], axis=0)


def _unpack_tail(a):
    out, o = {}, 0
    flat = a[:R_SMALL].reshape(-1)
    for n, s in _SMALL:
        k = DEPTH * math.prod(s)
        out[n] = flat[o:o + k].reshape((DEPTH,) + s)
        o += k
    flat, o = a[R_SMALL:].reshape(-1), 0
    for n, s in _REP:
        k = DEPTH * math.prod(s)
        out[n] = flat[o:o + k].reshape((DEPTH,) + s)
        o += k
    return out


def _by_dest(g):
    g = g.reshape(g.shape[:-1] + (N_DEV, g.shape[-1] // N_DEV))
    return jnp.moveaxis(g, -2, 0).reshape(N_DEV, -1)


def _from_stack(st):
    st = jnp.moveaxis(st, 0, -2)
    return st.reshape(st.shape[:-2] + (st.shape[-2] * st.shape[-1],))


def _layer_fwd(x, xb, mem, w, l):
    nm = lambda s: f"{s}_l{l}"
    wi = w["wi"]
    row = lambda v: v.reshape(1, -1)
    s = dict(x=x, xb=xb, wi=wi)
    s["p_ssd"] = _mm(xb, wi["ssd"], name=nm("proj_ssd"))
    s["p_lru"] = _mm(xb, wi["lru"], name=nm("proj_lru"))
    s["p_q"] = _mm(xb, wi["q"], out_dtype=BF16, name=nm("proj_q"))
    s["p_gl"] = _mm(xb, wi["gl"], out_dtype=BF16, name=nm("proj_gl"))
    s["lru_par"] = (w["lru_conv_w"], row(w["lru_conv_b"]), w["lru_w_a"], row(w["lru_b_a"]), w["lru_w_i"],
                    row(w["lru_b_i"]), row(w["lru_lambda"]))
    s["y_lru"], s["h"], s["xc"] = _lru_fwd(s["p_lru"], *s["lru_par"], name=nm("lru_fwd"))
    s["ssd_par"] = (w["ssd_conv_w"], row(w["ssd_conv_b"]), row(w["ssd_dt_bias"]), row(w["ssd_a_log"]),
                    row(w["ssd_d"]), row(w["ssd_norm_w"]))
    s["y_ssd"], s["yraw"], s["hs"], s["c_ssd"] = _ssd_fwd(s["p_ssd"], *s["ssd_par"], name=nm("ssd_fwd"))
    s["kv"] = _mm(mem, w["mem_w_kv"], name=nm("kv"))
    s["y_xa"] = _xa_fwd(s["p_q"], s["kv"], name=nm("xa_fwd"))
    s["b1"] = _mm(s["y_lru"], w["w_br_lru"], out_dtype=BF16, name=nm("br_lru"))
    s["b2"] = _mm(s["y_ssd"], w["w_br_ssd"], out_dtype=BF16, name=nm("br_ssd"))
    s["b3"] = _mm(s["y_xa"], w["w_br_xa"], out_dtype=BF16, name=nm("br_xa"))
    s["bg"] = row(w["b_gate"])
    s["merged"] = _merge_fwd(s["p_gl"], s["bg"], s["b1"], s["b2"], s["b3"], name=nm("merge_fwd"))
    s["mix"] = _mm(s["merged"], w["w_out"], name=nm("out_proj"))
    s["x1"], s["x1b"] = _ln_fwd(x, s["mix"], row(w["ln1_g"]), row(w["ln1_b"]), name=nm("ln1_fwd"))
    s["gate"], s["up"], s["act"] = _ffn_in_swiglu(s["x1b"], w["ffn_w_in"], name=nm("ffn_in"))
    s["f"] = _mm(s["act"], w["ffn_w_down"], name=nm("ffn_down"))
    s["x2"], s["x2b"] = _ln_fwd(s["x1"], s["f"], row(w["ln2_g"]), row(w["ln2_b"]), name=nm("ln2_fwd"))
    return s


def _layer_bwd(s, mem, w, dxo, l, hooks=None):
    nm = lambda t: f"{t}_l{l}"
    g = {}
    hook = lambda stage, t: hooks[stage](t, g) if hooks and stage in hooks else t
    row = lambda v: v.reshape(1, -1)
    slabs = lambda a: a.reshape(N_DEV, a.shape[0] // N_DEV, a.shape[1])
    du2, dg, db = _ln_bwd(s["x1"], s["f"], dxo, row(w["ln2_g"]), name=nm("ln2_bwd"))
    g["ln2_g"], g["ln2_b"] = dg[0], db[0]
    dgate, dup = _d_swiglu(du2, w["ffn_w_down"], s["gate"], s["up"], name=nm("d_swiglu"))
    g["ffn_w_down"] = slabs(_mm(s["act"], du2, ta=True, name=nm("dw_ffn_down")))
    dx1 = _mm(dgate, w["ffn_w_in"][:D_FF], add=du2, add_scale=ALPHA, name=nm("d_x1_gate"))
    dx1 = _mm(dup, w["ffn_w_in"][D_FF:], add=dx1, name=nm("d_x1_up"))
    g["ffn_w_in"] = slabs(jnp.concatenate([_mm(dgate, s["x1b"], ta=True, name=nm("dw_ffn_gate")),
                                           _mm(dup, s["x1b"], ta=True, name=nm("dw_ffn_up"))], axis=0))
    du1, dg, db = _ln_bwd(s["x"], s["mix"], dx1, row(w["ln1_g"]), name=nm("ln1_bwd"))
    g["ln1_g"], g["ln1_b"] = dg[0], db[0]
    dmerged = hook("mid", _mm(du1, w["w_out"], tb=True, name=nm("d_merged")))
    g["w_out"] = slabs(_mm(s["merged"], du1, ta=True, name=nm("dw_out")))
    dp_gl, d1, d2, d3, dbg = _merge_bwd(s["p_gl"], s["bg"], s["b1"], s["b2"], s["b3"], dmerged, name=nm("merge_bwd"))
    g["b_gate"] = dbg.reshape(3, D)
    dy_lru = _mm(d1, w["w_br_lru"], tb=True, name=nm("d_y_lru"))
    g["w_br_lru"] = slabs(_mm(s["y_lru"], d1, ta=True, name=nm("dw_br_lru")))
    dy_ssd = _mm(d2, w["w_br_ssd"], tb=True, name=nm("d_y_ssd"))
    g["w_br_ssd"] = slabs(_mm(s["y_ssd"], d2, ta=True, name=nm("dw_br_ssd")))
    dy_xa = _mm(d3, w["w_br_xa"], tb=True, out_dtype=BF16, name=nm("d_y_xa"))
    g["w_br_xa"] = slabs(_mm(s["y_xa"], d3, ta=True, name=nm("dw_br_xa")))
    dp_q, dkv = _xa_bwd(s["p_q"], s["kv"], dy_xa, name=nm("xa_bwd"))
    g["mem_w_kv"] = _mm(mem, dkv, ta=True, split_n=2 * D // N_DEV, name=nm("dw_kv"))
    dy_ssd = hook("branches", dy_ssd)
    ssd_cw, _, *ssd_rest = s["ssd_par"]
    dp_ssd, dcw, dcb, ddtb, dalog, dd, dnw = _ssd_bwd(s["p_ssd"], s["c_ssd"], s["yraw"], s["hs"], dy_ssd, ssd_cw,
                                                      *ssd_rest, name=nm("ssd_bwd"))
    g["ssd_conv_w"], g["ssd_conv_b"], g["ssd_dt_bias"] = dcw, dcb[0], ddtb[0]
    g["ssd_a_log"], g["ssd_d"], g["ssd_norm_w"] = dalog[0], dd[0], dnw[0]
    dp_ssd = hook("ssd", dp_ssd)
    lru_cw, _, *lru_rest = s["lru_par"]
    dp_lru, dcw, dcb, dwa, dba, dwi, dbi, dlam = _lru_bwd(s["p_lru"], s["xc"], s["h"], dy_lru, lru_cw, *lru_rest,
                                                          name=nm("lru_bwd"))
    g["lru_conv_w"], g["lru_conv_b"], g["lru_w_a"], g["lru_b_a"] = dcw, dcb[0], dwa, dba[0]
    g["lru_w_i"], g["lru_b_i"], g["lru_lambda"] = dwi, dbi[0], dlam[0]
    wi, x = s["wi"], s["xb"]
    g["w_in"] = _w_in_shards(_mm(x, dp_ssd, ta=True, name=nm("dw_in_ssd")), _mm(x, dp_lru, ta=True, name=nm("dw_in_lru")),
                             _mm(x, dp_q, ta=True, name=nm("dw_in_q")), _mm(x, dp_gl, ta=True, name=nm("dw_in_gl")),
                             name=nm("dw_in_shards"))
    dp_ssd = hook("weights", dp_ssd)
    dx = _mm(dp_ssd, wi["ssd"], tb=True, add=du1, add_scale=ALPHA, name=nm("dx_ssd"))
    dx = hook("dx", _mm(dp_lru, wi["lru"], tb=True, add=dx, name=nm("dx_lru")))
    dx = _mm(dp_q, wi["q"], tb=True, add=dx, name=nm("dx_q"))
    dx = _mm(dp_gl, wi["gl"], tb=True, add=dx, name=nm("dx_gl"))
    return dx, g


def _local_step(x, mem, target, layers, hooks=None):
    saved, xb = [], x.astype(BF16)
    for l in range(DEPTH):
        saved.append(_layer_fwd(x, xb, mem, layers[l], l))
        x, xb = saved[-1]["x2"], saved[-1]["x2b"]
    dx, loss = _loss_head(x, target, name="loss_head")
    if hooks and "loss" in hooks[-1]:
        loss = hooks[-1]["loss"](loss, None)
    grads = [None] * DEPTH
    for l in reversed(range(DEPTH)):
        dx, grads[l] = _layer_bwd(saved[l], mem, layers[l], dx, l, hooks[l] if hooks else None)
    return loss, dx, grads


def kernel(x, mem, w_in, b_gate, lru_conv_w, lru_conv_b, lru_w_a, lru_b_a, lru_w_i, lru_b_i, lru_lambda, ssd_conv_w, ssd_conv_b, ssd_dt_bias, ssd_a_log, ssd_d, ssd_norm_w, mem_w_kv, w_br_lru, w_br_ssd, w_br_xa, w_out, ln1_g, ln1_b, ffn_w_in, ffn_w_down, ln2_g, ln2_b, loss_target, m_w_in, m_b_gate, m_lru_conv_w, m_lru_conv_b, m_lru_w_a, m_lru_b_a, m_lru_w_i, m_lru_b_i, m_lru_lambda, m_ssd_conv_w, m_ssd_conv_b, m_ssd_dt_bias, m_ssd_a_log, m_ssd_d, m_ssd_norm_w, m_mem_w_kv, m_w_br_lru, m_w_br_ssd, m_w_br_xa, m_w_out, m_ln1_g, m_ln1_b, m_ffn_w_in, m_ffn_w_down, m_ln2_g, m_ln2_b, v_w_in, v_b_gate, v_lru_conv_w, v_lru_conv_b, v_lru_w_a, v_lru_b_a, v_lru_w_i, v_lru_b_i, v_lru_lambda, v_ssd_conv_w, v_ssd_conv_b, v_ssd_dt_bias, v_ssd_a_log, v_ssd_d, v_ssd_norm_w, v_mem_w_kv, v_w_br_lru, v_w_br_ssd, v_w_br_xa, v_w_out, v_ln1_g, v_ln1_b, v_ffn_w_in, v_ffn_w_down, v_ln2_g, v_ln2_b):
    local = dict(locals())
    w = {n: local[n] for n in _ORDER}
    m = {n: local["m_" + n] for n in _ORDER}
    v = {n: local["v_" + n] for n in _ORDER}
    for n in _TRANSPOSED:
        w[n], m[n], v[n] = (jnp.swapaxes(a, 1, 2) for a in (w[n], m[n], v[n]))

    big = [n for n, _, _ in _BIG]
    kinds = {n: kind for n, kind, _ in _BIG}

    small = _rows(jnp.concatenate([w[n].reshape(-1) for n, _ in _SMALL]), R_SMALL)
    first = _all_gather([w["w_in"][0].astype(BF16), small], name="gather_w_in_l0")
    rest, later, _ = lax.optimization_barrier(([w[n][0].astype(BF16) for n in big[1:]],
                                               [w[n][1].astype(BF16) for n in big], first[-1]))
    rest = _all_gather(rest, "gather_weights_l0", collective_id=1)
    later = _all_gather(later, "gather_weights_l1", collective_id=4)
    stacks = [dict(zip(big, [first[0], *rest])), dict(zip(big, later))]
    small_all, o, small_full = first[-1].reshape(N_DEV, R_SMALL * LANES), 0, {}
    for n, s in _SMALL:
        k = DEPTH * s[0] * s[1]
        small_full[n] = _from_stack(small_all[:, o:o + k].reshape((N_DEV, DEPTH) + s))
        o += k
    layers = []
    for l in range(DEPTH):
        lw = {n: w[n][l] for n, _ in _REP}
        lw.update({n: small_full[n][l] for n, _ in _SMALL})
        lw["wi"] = dict(zip(("ssd", "lru", "q", "gl"), _w_in_regions(stacks[l]["w_in"], name=f"w_in_regions_l{l}")))
        for n in big[1:]:
            if kinds[n] == "col":
                lw[n] = _cols_concat(stacks[l][n], name=f"full_{n}_l{l}")
            else:
                lw[n] = stacks[l][n].reshape(-1, stacks[l][n].shape[-1])
        layers.append(lw)

    by_dest = lambda a: a.reshape((4, 2) + a.shape[1:])
    slots, pending, last_layer = {}, {}, {}
    queue = [stacks[1]["w_out"]]

    def after_last(operands):
        operands, _ = lax.optimization_barrier((list(operands), queue[-1]))
        return operands

    def start(tag, collective_id, names_and_grads):
        names, owns = zip(*names_and_grads)
        gots = _exchange_sibling(after_last(owns), name=f"reduce_cores_{tag}", collective_id=collective_id)
        queue.append(gots[0])
        pending[tag] = (names, owns, gots)

    def finish(tag, collective_id, t):
        names, owns, gots = pending.pop(tag)
        t, gots = lax.optimization_barrier((t, gots))
        sums = [_pair_sum(own, got, name=f"pair_sum_{tag}_{n}", out_dtype=F32 if n == "tail" else BF16,
                          tb=R_SM if n == "tail" else _rowblk(own.shape[2], 256))
                for n, own, got in zip(names, owns, gots)]
        t, sums = lax.optimization_barrier((t, sums))
        got = _exchange_chips(sums, name=f"reduce_chips_{tag}", collective_id=collective_id)
        queue.append(got[0])
        slots.update({(tag, n): s for n, s in zip(names, got)})
        return t

    def tail_of(g0):
        stacked = {n: jnp.stack([g0[n], last_layer[n]]) for n in [s[0] for s in _SMALL + _REP]}
        sm = jnp.concatenate([_by_dest(stacked[n]) for n, _ in _SMALL], axis=1)
        sm = jnp.pad(sm, ((0, 0), (0, R_SMALL * LANES - sm.shape[1])))
        rep = jnp.concatenate([stacked[n].reshape(-1) for n, _ in _REP] + [last_layer["loss"][0, :1]])
        rep = jnp.pad(rep, (0, N_DEV * R_REP * LANES - rep.shape[0])).reshape(N_DEV, R_REP * LANES)
        tail = jnp.concatenate([sm, rep, jnp.zeros((N_DEV, (R_SM - R_SMALL - R_REP) * LANES), F32)], axis=1)
        return tail.reshape(4, 2, R_SM, LANES)

    def weights_l1(t, g):
        last_layer.update(g)
        start("l1", 2, [(n, by_dest(g[n])) for n in big])
        return t

    def branches_l0(t, g):
        start("l0a", 5, [(n, by_dest(g[n])) for n in big[1:]])
        return t

    def weights_l0(t, g):
        start("l0b", 7, [("w_in", by_dest(g["w_in"])), ("tail", tail_of(g))])
        return t

    hooks = [{"branches": branches_l0, "ssd": lambda t, g: finish("l0a", 6, t), "weights": weights_l0,
              "dx": lambda t, g: finish("l0b", 8, t)},
             {"weights": weights_l1, "dx": lambda t, g: finish("l1", 3, t),
              "loss": lambda t, g: last_layer.setdefault("loss", t)}]
    _, dx, grads = _local_step(x[0], mem[0], loss_target[0], layers, hooks)

    res = {}
    for n in big:
        tb = _rowblk(w[n].shape[1], 128 if w[n].shape[2] > LANES else 256)
        res[n] = _adamw([slots["l0b" if n == "w_in" else "l0a", n], slots["l1", n]], w[n], m[n], v[n],
                        name=f"adamw_{n}", tb=tb)
    tail_sum = _slot_sum(slots["l0b", "tail"], name="sum_tail")
    rep_all = _all_gather([tail_sum[R_SMALL:R_SMALL + R_REP]], name="gather_replicated")[0]
    g_tail = jnp.concatenate([tail_sum[:R_SMALL], rep_all.reshape(N_DEV * R_REP, LANES)], axis=0)
    loss = rep_all.reshape(-1)[N_REP]
    tails = _adamw([g_tail[None]], _pack_tail(w)[None], _pack_tail(m)[None], _pack_tail(v)[None],
                   name="adamw_tail", tb=TB_TAIL)

    outs = []
    for kind in range(4):
        d = {**{n: res[n][kind] for n in big}, **_unpack_tail(tails[kind][0])}
        d.update({n: jnp.swapaxes(d[n], 1, 2) for n in _TRANSPOSED})
        outs += [d[n] for n in _ORDER]
    return (loss, dx[None], *outs)
```

```python
import math

import jax
import jax.numpy as jnp
from jax import lax
from jax.experimental import pallas as pl
from jax.experimental.pallas import tpu as pltpu
from jax.experimental.pallas import tpu_sc as plsc

F32 = jnp.float32
BF16 = jnp.bfloat16

D = 1024
DEPTH = 2
N_DEV = 8
CHUNK = 64
LRU_BLOCKS = 8
LRU_BLOCK = 128
LRU_C = 8.0
D_SSD = 2 * D
SSD_HEADS = 32
SSD_GROUPS = 4
GROUP_W = D_SSD // SSD_GROUPS
SSD_STATE = 128
D_XBC = D_SSD + 2 * SSD_GROUPS * SSD_STATE
XA_HEADS = 4
XA_HEAD_DIM = 256
D_FF = 2816
ALPHA = (2 * DEPTH) ** 0.25
EPS = 1e-5
N_IN = 11296

S_Z, S_XBC, S_DT, W_SSD = 0, 2048, 5120, 5632
DT_REAL = 32

ADAM_LR, ADAM_B1, ADAM_B2, ADAM_EPS, ADAM_WD, ADAM_STEP = 0.001, 0.9, 0.999, 1e-08, 0.01, 10

VMEM_LIMIT = 56 * 1024 * 1024
MESH = pl.DeviceIdType.MESH
ANY = pl.BlockSpec(memory_space=pl.ANY)


def _cp(*sem):
    return pltpu.CompilerParams(dimension_semantics=sem, vmem_limit_bytes=VMEM_LIMIT)


def _blk(n, target):
    if n % 128:
        return n
    best = 128
    for b in range(128, min(n, target) + 1, 128):
        if n % b == 0:
            best = b
    return best


def _iota(shape, dim):
    return lax.broadcasted_iota(jnp.int32, shape, dim)


def _sigmoid(x):
    return 0.5 + 0.5 * jnp.tanh(0.5 * x)


def _log1p(e):
    u = 1.0 + e
    return jnp.where(u == 1.0, e, jnp.log(u) * (e / (u - 1.0)))


def _softplus(x):
    return jnp.maximum(x, 0.0) + _log1p(jnp.exp(-jnp.abs(x)))


_G0 = math.sqrt(2.0 / math.pi)
_G1 = 0.044715


def _gelu_and_grad(x):
    x2 = x * x
    u = 0.5 + 0.5 * jnp.tanh(x * (_G0 + (_G0 * _G1) * x2))
    dg = u + (x * (u * (1.0 - u))) * ((2.0 * _G0) + (6.0 * _G0 * _G1) * x2)
    return x * u, dg


_NN = (((1,), (0,)), ((), ()))
_NT = (((1,), (1,)), ((), ()))
_TN = (((0,), (0,)), ((), ()))


def _dot(a, b, dims=_NN):
    return lax.dot_general(a.astype(BF16), b.astype(BF16), dims, preferred_element_type=F32)


def _dot_hi(a, b, dims=_NN):
    return lax.dot_general(a, b, dims, precision=lax.Precision.HIGHEST, preferred_element_type=F32)


def _split(v):
    hi = v.astype(BF16)
    return hi, (v - hi.astype(F32)).astype(BF16)


def _dot01(v, e, dims=_NN):
    hi, lo = _split(v)
    return (lax.dot_general(hi, e, dims, preferred_element_type=F32)
            + lax.dot_general(lo, e, dims, preferred_element_type=F32))


def _conv_taps(xe, n):
    return [xe[8:8 + n] if j == 3 else pltpu.roll(xe, 3 - j, 0)[8:8 + n] for j in range(4)]


def _conv_fwd(taps, cw, cb):
    return cb + cw[0:1] * taps[0] + cw[1:2] * taps[1] + cw[2:3] * taps[2] + cw[3:4] * taps[3]


def _conv_bwd(dc, dnext, x, cw, n):
    ext = jnp.concatenate([dc, dnext], axis=0)
    shifted = [pltpu.roll(ext, n + 8 - (3 - j), 0)[0:n] for j in range(3)] + [dc]
    dx = cw[0:1] * shifted[0] + cw[1:2] * shifted[1] + cw[2:3] * shifted[2] + cw[3:4] * dc
    dcw = jnp.concatenate([jnp.sum(x * shifted[j], axis=0, keepdims=True) for j in range(4)], axis=0)
    return dx, dcw, jnp.sum(dc, axis=0, keepdims=True)


MM_VMEM_BUDGET = 44 * 1024 * 1024
MM_MAX_TILE = 1408
MM_MAX_K = 5632


def _divisors(n, cap):
    return [n] if n % 128 else [b for b in range(128, min(n, cap) + 1, 128) if n % b == 0]


def _mm_tiles(m_dim, n_dim, k_dim, a_bytes, b_bytes, o_bytes, has_add, tn_fixed):
    best = None
    for tm in _divisors(m_dim, MM_MAX_TILE):
        for tn in ([tn_fixed] if tn_fixed else _divisors(n_dim, MM_MAX_TILE)):
            for tk in _divisors(k_dim, MM_MAX_K):
                vmem = 2 * (tm * tk * a_bytes + tm * tn * (o_bytes + (4 if has_add else 0))) + 3 * tk * tn * b_bytes
                vmem += tm * tn * 4 if tk < k_dim else 0
                if vmem <= MM_VMEM_BUDGET:
                    key = (tm * tn * tk, tk, tn)
                    if best is None or key > best[0]:
                        best = (key, (tm, tn, tk))
    assert best is not None, (m_dim, n_dim, k_dim)
    return best[1]


def _mm(a, b, *, ta=False, tb=False, out_dtype=F32, add=None, add_scale=1.0, name, split_n=None):
    if ta:
        k_dim, m_dim = a.shape
    else:
        m_dim, k_dim = a.shape
    if tb:
        n_dim, k2 = b.shape
    else:
        k2, n_dim = b.shape
    assert k_dim == k2, (a.shape, b.shape, ta, tb)
    tm, tn, tk = _mm_tiles(m_dim, n_dim, k_dim, a.dtype.itemsize, b.dtype.itemsize, jnp.dtype(out_dtype).itemsize,
                           add is not None, split_n)
    nk = k_dim // tk
    a_spec = pl.BlockSpec((tk, tm), lambda i, j, k: (k, i)) if ta else pl.BlockSpec((tm, tk), lambda i, j, k: (i, k))
    b_spec = pl.BlockSpec((tn, tk), lambda i, j, k: (j, k)) if tb else pl.BlockSpec((tk, tn), lambda i, j, k: (k, j))
    o_spec = pl.BlockSpec((tm, tn), lambda i, j, k: (i, j))
    out_shape = (m_dim, n_dim)
    if split_n is not None:
        assert add is None and tn == split_n, (tn, split_n)
        o_spec = pl.BlockSpec((None, tm, tn), lambda i, j, k: (j, i, 0))
        out_shape = (n_dim // tn, m_dim, tn)
    dims = (((0 if ta else 1,), (1 if tb else 0,)), ((), ()))
    has_add = add is not None

    nj = n_dim // tn
    steps = (m_dim // tm) * nj * nk

    def body(*refs):
        a_ref, b_hbm = refs[:2]
        add_ref = refs[2] if has_add else None
        o_ref = refs[3] if has_add else refs[2]
        acc_ref = refs[-3] if nk > 1 else None
        b_buf, b_sem = refs[-2:]
        k = pl.program_id(2)
        s = (pl.program_id(0) * nj + pl.program_id(1)) * nk + k

        def fetch(step):
            jb = pl.multiple_of(((step // nk) % nj) * tn, tn)
            kb = pl.multiple_of((step % nk) * tk, tk)
            src = b_hbm.at[pl.ds(jb, tn), pl.ds(kb, tk)] if tb else b_hbm.at[pl.ds(kb, tk), pl.ds(jb, tn)]
            return pltpu.make_async_copy(src, b_buf.at[step % 3], b_sem.at[step % 3])

        @pl.when(s == 0)
        def _():
            fetch(s).start()
            if steps > 1:
                fetch(s + 1).start()

        @pl.when(s + 2 < steps)
        def _():
            fetch(s + 2).start()

        fetch(s).wait()

        def product():
            return lax.dot_general(a_ref[...].astype(BF16), b_buf[s % 3].astype(BF16), dims, preferred_element_type=F32)

        def finish(r):
            if has_add:
                r = r + add_scale * add_ref[...]
            o_ref[...] = r.astype(out_dtype)

        if nk == 1:
            finish(product())
            return

        @pl.when(k == 0)
        def _():
            acc_ref[...] = product()

        @pl.when((k > 0) & (k < nk - 1))
        def _():
            acc_ref[...] += product()

        @pl.when(k == nk - 1)
        def _():
            finish(acc_ref[...] + product())

    in_specs = [a_spec, ANY] + ([o_spec] if has_add else [])
    args = (a, b) + ((add,) if has_add else ())
    return pl.pallas_call(
        body, name=name, grid=(m_dim // tm, n_dim // tn, nk),
        in_specs=in_specs, out_specs=o_spec,
        out_shape=jax.ShapeDtypeStruct(out_shape, out_dtype),
        scratch_shapes=([pltpu.VMEM((tm, tn), F32)] if nk > 1 else [])
        + [pltpu.VMEM((3,) + b_spec.block_shape, b.dtype), pltpu.SemaphoreType.DMA((3,))],
        cost_estimate=pl.CostEstimate(
            flops=2 * m_dim * n_dim * k_dim, transcendentals=0,
            bytes_accessed=a.size * a.dtype.itemsize + b.size * b.dtype.itemsize
            + m_dim * n_dim * (jnp.dtype(out_dtype).itemsize + (4 if has_add else 0))),
        compiler_params=_cp("arbitrary", "arbitrary", "arbitrary"),
    )(*args)


def _ln_fwd(x, f, g, b, name, tb=512):
    t = x.shape[0]
    tb = min(tb, t)

    def body(x_ref, f_ref, g_ref, b_ref, o_ref, ob_ref):
        u = ALPHA * x_ref[...] + f_ref[...]
        mu = jnp.mean(u, axis=-1, keepdims=True)
        d = u - mu
        var = jnp.mean(d * d, axis=-1, keepdims=True)
        y = d * lax.rsqrt(var + EPS) * g_ref[...] + b_ref[...]
        o_ref[...] = y
        ob_ref[...] = y.astype(BF16)

    row = pl.BlockSpec((tb, D), lambda i: (i, 0))
    par = pl.BlockSpec((1, D), lambda i: (0, 0))
    return pl.pallas_call(
        body, name=name, grid=(t // tb,), in_specs=[row, row, par, par], out_specs=[row, row],
        out_shape=[jax.ShapeDtypeStruct((t, D), F32), jax.ShapeDtypeStruct((t, D), BF16)],
        compiler_params=_cp("parallel"),
    )(x, f, g, b)


def _ln_bwd(x, f, dy, g, name, tb=512):
    t = x.shape[0]
    tb = min(tb, t)

    def body(x_ref, f_ref, dy_ref, g_ref, du_ref, dg_ref, db_ref):
        @pl.when(pl.program_id(0) == 0)
        def _():
            dg_ref[...] = jnp.zeros_like(dg_ref)
            db_ref[...] = jnp.zeros_like(db_ref)

        u = ALPHA * x_ref[...] + f_ref[...]
        mu = jnp.mean(u, axis=-1, keepdims=True)
        d = u - mu
        var = jnp.mean(d * d, axis=-1, keepdims=True)
        rstd = lax.rsqrt(var + EPS)
        xhat = d * rstd
        dy = dy_ref[...]
        dxh = dy * g_ref[...]
        m1 = jnp.mean(dxh, axis=-1, keepdims=True)
        m2 = jnp.mean(dxh * xhat, axis=-1, keepdims=True)
        du_ref[...] = rstd * (dxh - m1 - xhat * m2)
        dg_ref[...] += jnp.sum(dy * xhat, axis=0, keepdims=True)
        db_ref[...] += jnp.sum(dy, axis=0, keepdims=True)

    row = pl.BlockSpec((tb, D), lambda i: (i, 0))
    par = pl.BlockSpec((1, D), lambda i: (0, 0))
    return pl.pallas_call(
        body, name=name, grid=(t // tb,), in_specs=[row, row, row, par], out_specs=[row, par, par],
        out_shape=[jax.ShapeDtypeStruct((t, D), F32), jax.ShapeDtypeStruct((1, D), F32),
                   jax.ShapeDtypeStruct((1, D), F32)],
        compiler_params=_cp("arbitrary"),
    )(x, f, dy, g)


FFN_TM, FFN_TN = 1024, D_FF // 2


def _ffn_in_swiglu(x, w, name):
    t = x.shape[0]
    tm = min(FFN_TM, t)
    nj = D_FF // FFN_TN

    def body(x_ref, wg_ref, wu_ref, g_ref, u_ref, a_ref):
        xb = x_ref[...].astype(BF16)
        g = lax.dot_general(xb, wg_ref[...], _NT, preferred_element_type=F32)
        u = lax.dot_general(xb, wu_ref[...], _NT, preferred_element_type=F32)
        g_ref[...] = g.astype(BF16)
        u_ref[...] = u.astype(BF16)
        a_ref[...] = (g * _sigmoid(g) * u).astype(BF16)

    tile = pl.BlockSpec((tm, FFN_TN), lambda i, j: (i, j))
    return pl.pallas_call(
        body, name=name, grid=(t // tm, nj),
        in_specs=[pl.BlockSpec((tm, D), lambda i, j: (i, 0)), pl.BlockSpec((FFN_TN, D), lambda i, j: (j, 0)),
                  pl.BlockSpec((FFN_TN, D), lambda i, j: (nj + j, 0))],
        out_specs=[tile, tile, tile],
        out_shape=[jax.ShapeDtypeStruct((t, D_FF), BF16)] * 3,
        compiler_params=_cp("parallel", "parallel"),
    )(x, w, w)


def _d_swiglu(du, w_down, g, u, name):
    t = du.shape[0]
    tm = min(FFN_TM, t)

    def body(du_ref, w_ref, g_ref, u_ref, dg_ref, dup_ref):
        da = lax.dot_general(du_ref[...].astype(BF16), w_ref[...], _NT, preferred_element_type=F32)
        g_v = g_ref[...].astype(F32)
        s = _sigmoid(g_v)
        dg_ref[...] = (da * u_ref[...].astype(F32) * (s * (1.0 + g_v * (1.0 - s)))).astype(BF16)
        dup_ref[...] = (da * g_v * s).astype(BF16)

    tile = pl.BlockSpec((tm, FFN_TN), lambda i, j: (i, j))
    return pl.pallas_call(
        body, name=name, grid=(t // tm, D_FF // FFN_TN),
        in_specs=[pl.BlockSpec((tm, D), lambda i, j: (i, 0)), pl.BlockSpec((FFN_TN, D), lambda i, j: (j, 0)), tile, tile],
        out_specs=[tile, tile],
        out_shape=[jax.ShapeDtypeStruct((t, D_FF), BF16), jax.ShapeDtypeStruct((t, D_FF), BF16)],
        compiler_params=_cp("parallel", "parallel"),
    )(du, w_down, g, u)


def _merge_fwd(pgl, bg, b1, b2, b3, name, tb=512):
    t = pgl.shape[0]
    tb = min(tb, t)

    def body(gl_ref, bg_ref, b1_ref, b2_ref, b3_ref, o_ref):
        acc = None
        for j, b_ref in enumerate((b1_ref, b2_ref, b3_ref)):
            sl = slice(j * D, (j + 1) * D)
            term = _sigmoid(gl_ref[:, sl].astype(F32) + bg_ref[:, sl]) * b_ref[...].astype(F32)
            acc = term if acc is None else acc + term
        o_ref[...] = acc.astype(BF16)

    row = pl.BlockSpec((tb, D), lambda i: (i, 0))
    return pl.pallas_call(
        body, name=name, grid=(t // tb,),
        in_specs=[pl.BlockSpec((tb, 3 * D), lambda i: (i, 0)), pl.BlockSpec((1, 3 * D), lambda i: (0, 0)), row, row, row],
        out_specs=row, out_shape=jax.ShapeDtypeStruct((t, D), BF16), compiler_params=_cp("parallel"),
    )(pgl, bg, b1, b2, b3)


def _merge_bwd(pgl, bg, b1, b2, b3, dm, name, tb=512):
    t = pgl.shape[0]
    tb = min(tb, t)

    def body(gl_ref, bg_ref, b1_ref, b2_ref, b3_ref, dm_ref, dgl_ref, d1_ref, d2_ref, d3_ref, dbg_ref):
        @pl.when(pl.program_id(0) == 0)
        def _():
            dbg_ref[...] = jnp.zeros_like(dbg_ref)

        dm_v = dm_ref[...]
        for j, (b_ref, d_ref) in enumerate(((b1_ref, d1_ref), (b2_ref, d2_ref), (b3_ref, d3_ref))):
            sl = slice(j * D, (j + 1) * D)
            gate = _sigmoid(gl_ref[:, sl].astype(F32) + bg_ref[:, sl])
            d_ref[...] = (dm_v * gate).astype(BF16)
            dgl = dm_v * b_ref[...].astype(F32) * (gate * (1.0 - gate))
            dgl_ref[:, sl] = dgl.astype(BF16)
            dbg_ref[:, sl] += jnp.sum(dgl, axis=0, keepdims=True)

    row = pl.BlockSpec((tb, D), lambda i: (i, 0))
    wide = pl.BlockSpec((tb, 3 * D), lambda i: (i, 0))
    par = pl.BlockSpec((1, 3 * D), lambda i: (0, 0))
    return pl.pallas_call(
        body, name=name, grid=(t // tb,),
        in_specs=[wide, par, row, row, row, row], out_specs=[wide, row, row, row, par],
        out_shape=[jax.ShapeDtypeStruct((t, 3 * D), BF16)] + [jax.ShapeDtypeStruct((t, D), BF16)] * 3
                  + [jax.ShapeDtypeStruct((1, 3 * D), F32)],
        compiler_params=_cp("arbitrary"),
    )(pgl, bg, b1, b2, b3, dm)


def _xa_probs(q, kv_ref, hd):
    sl = slice(hd * XA_HEAD_DIM, (hd + 1) * XA_HEAD_DIM)
    k = kv_ref[:, sl]
    v = kv_ref[:, D + hd * XA_HEAD_DIM:D + (hd + 1) * XA_HEAD_DIM]
    s = _dot(q[:, sl], k, _NT) * (XA_HEAD_DIM ** -0.5)
    e = jnp.exp(s - jnp.max(s, axis=1, keepdims=True))
    return sl, k, v, e / jnp.sum(e, axis=1, keepdims=True)


def _xa_fwd(pq, kv, name, tb=512):
    t = pq.shape[0]
    tb = min(tb, t)

    def body(q_ref, kv_ref, o_ref):
        q = q_ref[...]
        for hd in range(XA_HEADS):
            sl, _, v, p = _xa_probs(q, kv_ref, hd)
            o_ref[:, sl] = _dot(p, v).astype(BF16)

    row = pl.BlockSpec((tb, D), lambda i: (i, 0))
    return pl.pallas_call(
        body, name=name, grid=(t // tb,),
        in_specs=[row, pl.BlockSpec(kv.shape, lambda i: (0, 0))], out_specs=row,
        out_shape=jax.ShapeDtypeStruct((t, D), BF16), compiler_params=_cp("parallel"),
    )(pq, kv)


def _xa_bwd(pq, kv, dy, name, tb=512):
    t = pq.shape[0]
    tb = min(tb, t)

    def body(q_ref, kv_ref, dy_ref, dq_ref, dkv_ref):
        @pl.when(pl.program_id(0) == 0)
        def _():
            dkv_ref[...] = jnp.zeros_like(dkv_ref)

        q = q_ref[...]
        for hd in range(XA_HEADS):
            sl, k, v, p = _xa_probs(q, kv_ref, hd)
            dyh = dy_ref[:, sl]
            vsl = slice(D + hd * XA_HEAD_DIM, D + (hd + 1) * XA_HEAD_DIM)
            dkv_ref[:, vsl] += _dot(p, dyh, _TN)
            dp = _dot(dyh, v, _NT)
            ds = p * (dp - jnp.sum(dp * p, axis=1, keepdims=True)) * (XA_HEAD_DIM ** -0.5)
            dq_ref[:, sl] = _dot(ds, k).astype(BF16)
            dkv_ref[:, sl] += _dot(ds, q[:, sl], _TN)

    row = pl.BlockSpec((tb, D), lambda i: (i, 0))
    kvs = pl.BlockSpec(kv.shape, lambda i: (0, 0))
    return pl.pallas_call(
        body, name=name, grid=(t // tb,), in_specs=[row, kvs, row], out_specs=[row, kvs],
        out_shape=[jax.ShapeDtypeStruct((t, D), BF16), jax.ShapeDtypeStruct(kv.shape, F32)],
        compiler_params=_cp("arbitrary"),
    )(pq, kv, dy)


SUBLANES = 8


def _scan(a, u, reverse):
    n, c = a.shape
    groups = n // SUBLANES
    a = a.reshape(groups, SUBLANES, c)
    u = u.reshape(groups, SUBLANES, c)
    sub = _iota((1, SUBLANES, 1), 1)
    d = 1
    while d < SUBLANES:
        keep = (sub < SUBLANES - d) if reverse else (sub >= d)
        shift = SUBLANES - d if reverse else d
        u = a * jnp.where(keep, pltpu.roll(u, shift, 1), 0.0) + u
        a = a * jnp.where(keep, pltpu.roll(a, shift, 1), 1.0)
        d *= 2
    edge = 0 if reverse else SUBLANES - 1
    out, carry = [None] * groups, None
    for j in (reversed(range(groups)) if reverse else range(groups)):
        out[j] = u[j] if carry is None else u[j] + a[j] * carry
        carry = out[j][edge:edge + 1]
    return jnp.concatenate(out, axis=0)


def _lru_gates(xc, wa_ref, ba, wi_ref, bi, lam):
    za = jnp.concatenate([_dot(xc[:, n * 128:(n + 1) * 128], wa_ref[n]) for n in range(LRU_BLOCKS)], axis=1) + ba
    zi = jnp.concatenate([_dot(xc[:, n * 128:(n + 1) * 128], wi_ref[n]) for n in range(LRU_BLOCKS)], axis=1) + bi
    r = 1.0 / (1.0 + jnp.exp(-za))
    ig = _sigmoid(zi)
    sp = _softplus(-lam)
    log_a = (-LRU_C) * r * sp
    a = jnp.exp(log_a)
    m = jnp.sqrt(-jnp.tanh(log_a) * (1.0 + a * a))
    u = m * (ig * xc)
    return a, u, r, ig, m, sp


def _lru_fwd(p, cw, cb, wa, ba, wi, bi, lam, name, tb=256):
    t = p.shape[0]
    tb = min(tb, t)
    nb = t // tb
    r8 = tb // 8

    def body(x_ref, xp_ref, g_ref, cw_ref, cb_ref, wa_ref, ba_ref, wi_ref, bi_ref, lam_ref, y_ref, h_ref, xc_ref,
             hc_ref):
        i = pl.program_id(0)

        @pl.when(i == 0)
        def _():
            hc_ref[...] = jnp.zeros_like(hc_ref)

        halo = jnp.where(i == 0, 0.0, xp_ref[...])
        taps = _conv_taps(jnp.concatenate([halo, x_ref[...]], axis=0), tb)
        xc = _conv_fwd(taps, cw_ref[...], cb_ref[...])
        xc_ref[...] = xc
        a, u, _, _, _, _ = _lru_gates(xc, wa_ref, ba_ref[...], wi_ref, bi_ref[...], lam_ref[...])
        row = _iota((tb, 1), 0)
        u = u + jnp.where(row == 0, a * hc_ref[...], 0.0)
        h = _scan(a, u, reverse=False)
        h_ref[...] = h
        hc_ref[...] = h[tb - 1:tb, :]
        gl, _ = _gelu_and_grad(g_ref[...])
        y_ref[...] = (gl * h).astype(BF16)

    par = pl.BlockSpec((1, D), lambda i: (0, 0))
    wsp = pl.BlockSpec((LRU_BLOCKS, LRU_BLOCK, LRU_BLOCK), lambda i: (0, 0, 0))
    row = pl.BlockSpec((tb, D), lambda i: (i, 0))
    return pl.pallas_call(
        body, name=name, grid=(nb,),
        in_specs=[row, pl.BlockSpec((8, D), lambda i: (jnp.maximum(i * r8 - 1, 0), 0)),
                  pl.BlockSpec((tb, D), lambda i: (i, 1)),
                  pl.BlockSpec((4, D), lambda i: (0, 0)), par, wsp, par, wsp, par, par],
        out_specs=[row, row, row],
        out_shape=[jax.ShapeDtypeStruct((t, D), BF16), jax.ShapeDtypeStruct((t, D), F32),
                   jax.ShapeDtypeStruct((t, D), F32)],
        scratch_shapes=[pltpu.VMEM((1, D), F32)],
        compiler_params=_cp("arbitrary"),
    )(p, p, p, cw, cb, wa, ba, wi, bi, lam)


def _lru_bwd(p, xc, h, dy, cw, wa, ba, wi, bi, lam, name, tb=256):
    t = p.shape[0]
    tb = min(tb, t)
    nb = t // tb
    r8 = tb // 8

    def body(x_ref, g_ref, xc_ref, h_ref, hp_ref, dy_ref, cw_ref, wa_ref, ba_ref, wi_ref, bi_ref, lam_ref,
             dp_ref, dcw_ref, dcb_ref, dwa_ref, dba_ref, dwi_ref, dbi_ref, dlam_ref, carry_ref, dnext_ref):
        i = pl.program_id(0)
        blk = nb - 1 - i

        @pl.when(i == 0)
        def _():
            for r in (dcw_ref, dcb_ref, dwa_ref, dba_ref, dwi_ref, dbi_ref, dlam_ref, carry_ref, dnext_ref):
                r[...] = jnp.zeros_like(r)

        xc = xc_ref[...]
        lam = lam_ref[...]
        a, _, r, ig, m, sp = _lru_gates(xc, wa_ref, ba_ref[...], wi_ref, bi_ref[...], lam)
        gl, dgl = _gelu_and_grad(g_ref[...])
        h = h_ref[...]
        dy = dy_ref[...]
        dp_ref[:, D:] = (dy * h * dgl).astype(BF16)
        row = _iota((tb, 1), 0)
        dh = dy * gl + jnp.where(row == tb - 1, carry_ref[...], 0.0)
        b = jnp.where(row < tb - 1, pltpu.roll(a, tb - 1, 0), 0.0)
        gs = _scan(b, dh, reverse=True)
        carry_ref[...] = a[0:1] * gs[0:1]
        h_last = jnp.where(blk == 0, 0.0, hp_ref[7:8, :])
        hprev = jnp.where(row == 0, h_last, pltpu.roll(h, 1, 0))
        da = gs * hprev
        dm = gs * ig * xc
        di = gs * m * xc
        dxc = gs * m * ig
        dlog = a * (da - a * (dm / m))
        dr = dlog * ((-LRU_C) * sp)
        dsp = jnp.sum(dlog * ((-LRU_C) * r), axis=0, keepdims=True)
        dlam_ref[...] += dsp * (-_sigmoid(-lam))
        dza = dr * r * (1.0 - r)
        dzi = di * ig * (1.0 - ig)
        dba_ref[...] += jnp.sum(dza, axis=0, keepdims=True)
        dbi_ref[...] += jnp.sum(dzi, axis=0, keepdims=True)
        parts = []
        for n in range(LRU_BLOCKS):
            sl = slice(n * 128, (n + 1) * 128)
            dwa_ref[n] += _dot(xc[:, sl], dza[:, sl], _TN)
            dwi_ref[n] += _dot(xc[:, sl], dzi[:, sl], _TN)
            parts.append(_dot(dza[:, sl], wa_ref[n], _NT) + _dot(dzi[:, sl], wi_ref[n], _NT))
        dxc = dxc + jnp.concatenate(parts, axis=1)
        dx, dcw, dcb = _conv_bwd(dxc, dnext_ref[...], x_ref[...], cw_ref[...], tb)
        dp_ref[:, :D] = dx.astype(BF16)
        dcw_ref[...] += dcw
        dcb_ref[...] += dcb
        dnext_ref[...] = dxc[0:8]

    par = pl.BlockSpec((1, D), lambda i: (0, 0))
    wsp = pl.BlockSpec((LRU_BLOCKS, LRU_BLOCK, LRU_BLOCK), lambda i: (0, 0, 0))
    cws = pl.BlockSpec((4, D), lambda i: (0, 0))
    rev = lambda i: nb - 1 - i
    blk0 = pl.BlockSpec((tb, D), lambda i: (rev(i), 0))
    w_shape = jax.ShapeDtypeStruct((LRU_BLOCKS, LRU_BLOCK, LRU_BLOCK), F32)
    v_shape = jax.ShapeDtypeStruct((1, D), F32)
    return pl.pallas_call(
        body, name=name, grid=(nb,),
        in_specs=[blk0, pl.BlockSpec((tb, D), lambda i: (rev(i), 1)), blk0, blk0,
                  pl.BlockSpec((8, D), lambda i: (jnp.maximum(rev(i) * r8 - 1, 0), 0)), blk0,
                  cws, wsp, par, wsp, par, par],
        out_specs=[pl.BlockSpec((tb, 2 * D), lambda i: (rev(i), 0)), cws, par, wsp, par, wsp, par, par],
        out_shape=[jax.ShapeDtypeStruct((t, 2 * D), BF16), jax.ShapeDtypeStruct((4, D), F32), v_shape,
                   w_shape, v_shape, w_shape, v_shape, v_shape],
        scratch_shapes=[pltpu.VMEM((1, D), F32), pltpu.VMEM((8, D), F32)],
        compiler_params=_cp("arbitrary"),
    )(p, p, xc, h, h, dy, cw, wa, ba, wi, bi, lam)


def _ssd_consts():
    m0 = _iota((1, 128), 1) < 64
    e = (jnp.right_shift(_iota((SSD_HEADS, D_SSD), 1), 6) == _iota((SSD_HEADS, D_SSD), 0)).astype(BF16)
    tril = (_iota((CHUNK, CHUNK), 0) >= _iota((CHUNK, CHUNK), 1)).astype(F32)
    eye = (_iota((SSD_HEADS, SSD_HEADS), 0) == _iota((SSD_HEADS, SSD_HEADS), 1)).astype(F32)
    r2 = _iota((CHUNK, 128), 0)
    c2 = jnp.bitwise_and(_iota((CHUNK, 128), 1), 63)
    return dict(m0=m0, e=e, tril=tril, eye=eye, causal2=r2 >= c2, fold=(c2 == r2).astype(BF16))


SSD_STEP = 4


def _ssd_pre(c, dt_raw, dtb_ref, alog_ref, dvec_ref, k):
    sg = _sigmoid(c)
    xbc = c * sg
    dtp = dt_raw + dtb_ref[...]
    dt = _softplus(dtp)
    a = -jnp.exp(alog_ref[...])
    cs = _dot_hi(k["tril"], dt * a)
    cs_last = cs[CHUNK - 1:CHUNK]
    dend = jnp.exp(cs_last - cs)
    cdec = jnp.exp(cs_last)
    big = _dot01(jnp.concatenate([dt, jnp.exp(cs), dend], axis=0), k["e"])
    small = _dot01(jnp.concatenate([jnp.broadcast_to(cdec, (8, SSD_HEADS)),
                                    jnp.broadcast_to(dvec_ref[...], (8, SSD_HEADS))], axis=0), k["e"])
    cst2 = _dot_hi(k["eye"], jnp.concatenate([cs, cs], axis=0), _NT)
    return dict(c=c, sg=sg, xs=xbc[:, :D_SSD], bm=xbc[:, D_SSD:D_SSD + 512],
                cm=xbc[:, D_SSD + 512:], dtp=dtp, dt=dt, a=a, cs=cs, dend=dend, cdec=cdec,
                dtx=big[0:CHUNK], ecx=big[CHUNK:2 * CHUNK], dex=big[2 * CHUNK:3 * CHUNK],
                cdx=small[0:1], ddx=small[8:9], cst2=cst2)


def _pair_decay(p, cs, cst2, k):
    h0, h1 = 2 * p, 2 * p + 1
    colp = jnp.where(k["m0"], cs[:, h0:h0 + 1], cs[:, h1:h1 + 1])
    rowp = jnp.where(k["m0"], cst2[h0:h0 + 1, :], cst2[h1:h1 + 1, :])
    return jnp.where(k["causal2"], jnp.exp(colp - rowp), 0.0)


def _pair_stack(xp, k):
    return jnp.concatenate([jnp.where(k["m0"], xp, 0.0), jnp.where(k["m0"], 0.0, xp)], axis=0)


def _group_norm(yz, nw, with_stats=False):
    outs, stats = [], []
    for g in range(SSD_GROUPS):
        yzg = yz[:, g * GROUP_W:(g + 1) * GROUP_W]
        r = lax.rsqrt(jnp.mean(yzg * yzg, axis=1, keepdims=True) + EPS)
        outs.append(yzg * r)
        stats.append(r)
    y = jnp.concatenate(outs, axis=1) * nw
    return (y, stats) if with_stats else y


def _ssd_fwd(p, cw, cb, dtb, alog, dvec, nw, name):
    t = p.shape[0]
    step = SSD_STEP if t % (SSD_STEP * CHUNK) == 0 else 1
    rows_blk, nb, nc = step * CHUNK, t // (step * CHUNK), t // CHUNK

    def body(p_blk, cw_ref, cb_ref, dtb_ref, alog_ref, dvec_ref, nw_ref, y_blk, yraw_blk, hs_blk, c_blk,
             h_scr, tail_scr):
        @pl.when(pl.program_id(0) == 0)
        def _():
            h_scr[...] = jnp.zeros_like(h_scr)
            tail_scr[...] = jnp.zeros_like(tail_scr)

        k = _ssd_consts()

        def one_chunk(j, carry):
            rows = pl.ds(pl.multiple_of(j * CHUNK, CHUNK), CHUNK)
            chunk(p_blk.at[rows], y_blk.at[rows], yraw_blk.at[rows], hs_blk.at[j], c_blk.at[rows], k,
                  cw_ref, cb_ref, dtb_ref, alog_ref, dvec_ref, nw_ref, h_scr, tail_scr)
            return carry

        lax.fori_loop(0, step, one_chunk, 0)

    def chunk(p_ref, y_ref, yraw_ref, hs_ref, c_ref, k, cw_ref, cb_ref, dtb_ref, alog_ref, dvec_ref, nw_ref,
              h_scr, tail_scr):
        x_in = p_ref[:, S_XBC:S_DT]
        taps = _conv_taps(jnp.concatenate([tail_scr[...], x_in], axis=0), CHUNK)
        tail_scr[...] = x_in[CHUNK - 8:]
        c = _conv_fwd(taps, cw_ref[...], cb_ref[...])
        c_ref[...] = c
        s = _ssd_pre(c, p_ref[:, S_DT:S_DT + DT_REAL], dtb_ref, alog_ref, dvec_ref, k)
        xs, bm, cm = s["xs"], s["bm"], s["cm"]
        xdt = xs * s["dtx"]
        hprev = h_scr[...]
        hs_ref[...] = hprev
        ys, hn = [], []
        for g in range(SSD_GROUPS):
            gs = slice(g * GROUP_W, (g + 1) * GROUP_W)
            bg = bm[:, g * 128:(g + 1) * 128]
            cg = cm[:, g * 128:(g + 1) * 128]
            cbdup = _dot(cg, jnp.concatenate([bg, bg], axis=0), _NT)
            hp_g = hprev[:, gs]
            yd = []
            for q in range(4):
                pr = g * 4 + q
                mp = cbdup * _pair_decay(pr, s["cs"], s["cst2"], k)
                yd.append(_dot(mp, _pair_stack(xdt[:, pr * 128:(pr + 1) * 128], k)))
            ys.append(jnp.concatenate(yd, axis=1) + _dot(cg, hp_g) * s["ecx"][:, gs])
            hn.append(hp_g * s["cdx"][:, gs] + _dot(bg, xdt[:, gs] * s["dex"][:, gs], _TN))
        h_scr[...] = jnp.concatenate(hn, axis=1)
        yraw = jnp.concatenate(ys, axis=1) + s["ddx"] * xs
        yraw_ref[...] = yraw
        z = p_ref[:, S_Z:S_Z + D_SSD]
        y_ref[...] = _group_norm(yraw * (z * _sigmoid(z)), nw_ref[...]).astype(BF16)

    hv = pl.BlockSpec((1, DT_REAL), lambda i: (0, 0))
    return pl.pallas_call(
        body, name=name, grid=(nb,),
        in_specs=[pl.BlockSpec((rows_blk, W_SSD), lambda i: (i, 0)),
                  pl.BlockSpec((4, D_XBC), lambda i: (0, 0)), pl.BlockSpec((1, D_XBC), lambda i: (0, 0)),
                  hv, hv, hv, pl.BlockSpec((1, D_SSD), lambda i: (0, 0))],
        out_specs=[pl.BlockSpec((rows_blk, D_SSD), lambda i: (i, 0)), pl.BlockSpec((rows_blk, D_SSD), lambda i: (i, 0)),
                   pl.BlockSpec((step, SSD_STATE, D_SSD), lambda i: (i, 0, 0)),
                   pl.BlockSpec((rows_blk, D_XBC), lambda i: (i, 0))],
        out_shape=[jax.ShapeDtypeStruct((t, D_SSD), BF16), jax.ShapeDtypeStruct((t, D_SSD), F32),
                   jax.ShapeDtypeStruct((nc, SSD_STATE, D_SSD), F32), jax.ShapeDtypeStruct((t, D_XBC), F32)],
        scratch_shapes=[pltpu.VMEM((SSD_STATE, D_SSD), F32), pltpu.VMEM((8, D_XBC), F32)],
        compiler_params=_cp("arbitrary"),
    )(p, cw, cb, dtb, alog, dvec, nw)


def _ssd_bwd(p, c, yraw, hs, dy, cw, dtb, alog, dvec, nw, name):
    t = p.shape[0]
    step = 1
    rows_blk, nb = step * CHUNK, t // (step * CHUNK)

    def body(p_blk, c_blk, yraw_blk, hs_blk, dy_blk, cw_ref, dtb_ref, alog_ref, dvec_ref, nw_ref,
             dp_blk, dcw_ref, dcb_ref, ddtb_ref, dalog_ref, dd_ref, dnw_ref, dh_scr, dnext_scr):
        @pl.when(pl.program_id(0) == 0)
        def _():
            for r in (dcw_ref, dcb_ref, ddtb_ref, dalog_ref, dd_ref, dnw_ref, dh_scr, dnext_scr):
                r[...] = jnp.zeros_like(r)

        k = _ssd_consts()

        def one_chunk(jj, carry):
            j = step - 1 - jj
            rows = pl.ds(pl.multiple_of(j * CHUNK, CHUNK), CHUNK)
            chunk(p_blk.at[rows], c_blk.at[rows], yraw_blk.at[rows], hs_blk.at[j], dy_blk.at[rows], dp_blk.at[rows], k,
                  cw_ref, dtb_ref, alog_ref, dvec_ref, nw_ref, dcw_ref, dcb_ref, ddtb_ref, dalog_ref, dd_ref, dnw_ref,
                  dh_scr, dnext_scr)
            return carry

        lax.fori_loop(0, step, one_chunk, 0)

    def chunk(p_ref, c_ref, yraw_ref, hs_ref, dy_ref, dp_ref, k, cw_ref, dtb_ref, alog_ref, dvec_ref, nw_ref,
              dcw_ref, dcb_ref, ddtb_ref, dalog_ref, dd_ref, dnw_ref, dh_scr, dnext_scr):
        s = _ssd_pre(c_ref[...], p_ref[:, S_DT:S_DT + DT_REAL], dtb_ref, alog_ref, dvec_ref, k)
        xs, bm, cm, cs, dt, a = s["xs"], s["bm"], s["cm"], s["cs"], s["dt"], s["a"]
        m0 = k["m0"]
        xdt = xs * s["dtx"]
        hprev = hs_ref[...]
        dh = dh_scr[...]

        nw_v = nw_ref[...]
        yraw = yraw_ref[...]
        z = p_ref[:, S_Z:S_Z + D_SSD]
        sz = _sigmoid(z)
        siluz = z * sz
        yz = yraw * siluz
        dyo = dy_ref[...]
        dyn = dyo * nw_v
        dyz_parts, dnw_parts = [], []
        for g in range(SSD_GROUPS):
            gs = slice(g * GROUP_W, (g + 1) * GROUP_W)
            yzg = yz[:, gs]
            r = lax.rsqrt(jnp.mean(yzg * yzg, axis=1, keepdims=True) + EPS)
            dnw_parts.append(jnp.sum(dyo[:, gs] * yzg * r, axis=0, keepdims=True))
            dyz_parts.append(r * dyn[:, gs] - yzg * (r * r * r) * jnp.mean(dyn[:, gs] * yzg, axis=1, keepdims=True))
        dnw_ref[...] += jnp.concatenate(dnw_parts, axis=1)
        dyz = jnp.concatenate(dyz_parts, axis=1)
        d_y = dyz * siluz
        dp_ref[:, S_Z:S_Z + D_SSD] = (dyz * yraw * (sz * (1.0 + z * (1.0 - sz)))).astype(BF16)
        dd_row = jnp.sum(d_y * xs, axis=0, keepdims=True)
        dxs = d_y * s["ddx"]

        lane_h = _iota((1, SSD_HEADS), 1)
        sub_h = _iota((SSD_HEADS, 1), 0)
        dcs = jnp.zeros((CHUNK, SSD_HEADS), F32)
        dcst2 = jnp.zeros((SSD_HEADS, 128), F32)
        dxdt_parts, db_parts, dc_parts, dhp_parts, yoff_parts, dend_parts, dcd_parts = [], [], [], [], [], [], []
        for g in range(SSD_GROUPS):
            gs = slice(g * GROUP_W, (g + 1) * GROUP_W)
            bg = bm[:, g * 128:(g + 1) * 128]
            cg = cm[:, g * 128:(g + 1) * 128]
            bdup = jnp.concatenate([bg, bg], axis=0)
            cbdup = _dot(cg, bdup, _NT)
            dcb2 = jnp.zeros((CHUNK, 128), F32)
            dxp_parts = []
            for q in range(4):
                pr = g * 4 + q
                h0, h1 = 2 * pr, 2 * pr + 1
                lp = _pair_decay(pr, cs, s["cst2"], k)
                mp = cbdup * lp
                xst = _pair_stack(xdt[:, pr * 128:(pr + 1) * 128], k)
                dyp = d_y[:, pr * 128:(pr + 1) * 128]
                dmp = _dot(dyp, xst, _NT)
                dxst = _dot(mp, dyp, _TN)
                dxp_parts.append(jnp.where(m0, dxst[:CHUNK], dxst[CHUNK:]))
                dcb2 = dcb2 + dmp * lp
                dlm = dmp * mp
                rs0 = jnp.sum(jnp.where(m0, dlm, 0.0), axis=1, keepdims=True)
                rs1 = jnp.sum(jnp.where(m0, 0.0, dlm), axis=1, keepdims=True)
                dcs = dcs + jnp.where(lane_h == h0, rs0, 0.0) + jnp.where(lane_h == h1, rs1, 0.0)
                colsum = jnp.sum(dlm, axis=0, keepdims=True)
                sel = ((sub_h == h0) & m0) | ((sub_h == h1) & jnp.logical_not(m0))
                dcst2 = dcst2 - jnp.where(sel, colsum, 0.0)
            dcg = _dot(dcb2, bdup)
            dbdup = _dot(dcb2, cg, _TN)
            dbg = dbdup[:CHUNK] + dbdup[CHUNK:]
            hp_g = hprev[:, gs]
            zoff = _dot(cg, hp_g)
            dzo = d_y[:, gs] * s["ecx"][:, gs]
            dcg = dcg + _dot(dzo, hp_g, _NT)
            dh_g = dh[:, gs]
            dhp_parts.append(_dot(cg, dzo, _TN) + dh_g * s["cdx"][:, gs])
            dcd_parts.append(jnp.sum(dh_g * hp_g, axis=0, keepdims=True))
            wg = xdt[:, gs] * s["dex"][:, gs]
            dbg = dbg + _dot(wg, dh_g, _NT)
            dwg = _dot(bg, dh_g)
            dxdt_parts.append(jnp.concatenate(dxp_parts, axis=1) + dwg * s["dex"][:, gs])
            dend_g = dwg * wg
            dend_parts.append(jnp.sum(dend_g, axis=0, keepdims=True))
            yoff_parts.append(dzo * zoff - dend_g)
            db_parts.append(dbg)
            dc_parts.append(dcg)
        dh_scr[...] = jnp.concatenate(dhp_parts, axis=1)
        dxdt = jnp.concatenate(dxdt_parts, axis=1)
        sums = _dot01(jnp.concatenate([jnp.concatenate(yoff_parts, axis=1), dxdt * xs], axis=0), k["e"], _NT)
        rows8 = jnp.concatenate([jnp.broadcast_to(jnp.concatenate(r, axis=1), (8, D_SSD))
                                 for r in (dcd_parts, [dd_row], dend_parts)], axis=0)
        small = _dot01(rows8, k["e"], _NT)
        dd_ref[...] += small[8:9]
        dcs_last = small[0:1] * s["cdec"] + small[16:17]
        hi, lo = _split(dcst2)
        dcs = (dcs + sums[0:CHUNK]
               + lax.dot_general(k["fold"], hi, _NT, preferred_element_type=F32)
               + lax.dot_general(k["fold"], lo, _NT, preferred_element_type=F32)
               + jnp.where(_iota((CHUNK, 1), 0) == CHUNK - 1, dcs_last, 0.0))
        dda = _dot_hi(k["tril"], dcs, _TN)
        ddt = dda * a + sums[CHUNK:2 * CHUNK]
        dalog_ref[...] += jnp.sum(dda * dt, axis=0, keepdims=True) * a
        dxs = dxs + dxdt * s["dtx"]
        draw = ddt * _sigmoid(s["dtp"])
        ddtb_ref[...] += jnp.sum(draw, axis=0, keepdims=True)
        dp_ref[:, S_DT:] = jnp.zeros((CHUNK, W_SSD - S_DT), BF16)
        dp_ref[:, S_DT:S_DT + DT_REAL] = draw.astype(BF16)
        dxbc = jnp.concatenate([dxs] + db_parts + dc_parts, axis=1)
        sg, c = s["sg"], s["c"]
        dc = dxbc * (sg * (1.0 + c * (1.0 - sg)))
        dx, dcw, dcb = _conv_bwd(dc, dnext_scr[...], p_ref[:, S_XBC:S_DT], cw_ref[...], CHUNK)
        dp_ref[:, S_XBC:S_DT] = dx.astype(BF16)
        dcw_ref[...] += dcw
        dcb_ref[...] += dcb
        dnext_scr[...] = dc[0:8]

    rev = lambda i: nb - 1 - i
    hv = pl.BlockSpec((1, DT_REAL), lambda i: (0, 0))
    cws = pl.BlockSpec((4, D_XBC), lambda i: (0, 0))
    cbs = pl.BlockSpec((1, D_XBC), lambda i: (0, 0))
    nws = pl.BlockSpec((1, D_SSD), lambda i: (0, 0))
    wide = pl.BlockSpec((rows_blk, D_SSD), lambda i: (rev(i), 0))
    hshape = jax.ShapeDtypeStruct((1, DT_REAL), F32)
    return pl.pallas_call(
        body, name=name, grid=(nb,),
        in_specs=[pl.BlockSpec((rows_blk, W_SSD), lambda i: (rev(i), 0)),
                  pl.BlockSpec((rows_blk, D_XBC), lambda i: (rev(i), 0)),
                  wide, pl.BlockSpec((step, SSD_STATE, D_SSD), lambda i: (rev(i), 0, 0)), wide,
                  cws, hv, hv, hv, nws],
        out_specs=[pl.BlockSpec((rows_blk, W_SSD), lambda i: (rev(i), 0)), cws, cbs, hv, hv, hv, nws],
        out_shape=[jax.ShapeDtypeStruct((t, W_SSD), BF16), jax.ShapeDtypeStruct((4, D_XBC), F32),
                   jax.ShapeDtypeStruct((1, D_XBC), F32), hshape, hshape, hshape,
                   jax.ShapeDtypeStruct((1, D_SSD), F32)],
        scratch_shapes=[pltpu.VMEM((SSD_STATE, D_SSD), F32), pltpu.VMEM((8, D_XBC), F32)],
        compiler_params=_cp("arbitrary"),
    )(p, c, yraw, hs, dy, cw, dtb, alog, dvec, nw)


def _loss_head(y, target, name, tb=512):
    t = y.shape[0]
    tb = min(tb, t)

    def body(y_ref, t_ref, dy_ref, l_ref):
        @pl.when(pl.program_id(0) == 0)
        def _():
            l_ref[...] = jnp.zeros_like(l_ref)

        e = y_ref[...] - t_ref[...]
        dy_ref[...] = e * (1.0 / D)
        l_ref[...] += jnp.sum(jnp.sum(e * e, axis=1, keepdims=True), axis=0, keepdims=True) * (0.5 / D)

    row = pl.BlockSpec((tb, D), lambda i: (i, 0))
    return pl.pallas_call(
        body, name=name, grid=(t // tb,), in_specs=[row, row],
        out_specs=[row, pl.BlockSpec((8, 128), lambda i: (0, 0))],
        out_shape=[jax.ShapeDtypeStruct((t, D), F32), jax.ShapeDtypeStruct((8, 128), F32)],
        compiler_params=_cp("arbitrary"),
    )(y, target)


def _adamw(slots, w, m, v, name, tb):
    nl = len(slots)
    ns, r, c = slots[0].shape
    assert r % tb == 0 and w.shape == (nl, r, c), (r, tb, w.shape)

    def body(*refs):
        s_refs = refs[:nl]
        w_ref, m_ref, v_ref, g_ref, d_ref, m2_ref, v2_ref = refs[nl:]

        def total(ref):
            acc = ref[0].astype(F32)
            for j in range(1, ns):
                acc = acc + ref[j].astype(F32)
            return acc

        g = total(s_refs[0])
        for layer in range(1, nl):
            g = jnp.where(pl.program_id(0) == layer, total(s_refs[layer]), g)
        m2 = ADAM_B1 * m_ref[...] + (1.0 - ADAM_B1) * g
        v2 = ADAM_B2 * v_ref[...] + (1.0 - ADAM_B2) * (g * g)
        m_hat = m2 / (1.0 - ADAM_B1 ** ADAM_STEP)
        v_hat = v2 / (1.0 - ADAM_B2 ** ADAM_STEP)
        g_ref[...] = g
        d_ref[...] = -ADAM_LR * (m_hat / (jnp.sqrt(v_hat) + ADAM_EPS) + ADAM_WD * w_ref[...])
        m2_ref[...] = m2
        v2_ref[...] = v2

    def slot_spec(layer):
        return pl.BlockSpec((ns, tb, c), lambda l, i: (0, jnp.where(l == layer, i, 0), 0))

    row = pl.BlockSpec((None, tb, c), lambda l, i: (l, i, 0))
    shp = jax.ShapeDtypeStruct((nl, r, c), F32)
    return pl.pallas_call(
        body, name=name, grid=(nl, r // tb),
        in_specs=[slot_spec(layer) for layer in range(nl)] + [row, row, row],
        out_specs=[row, row, row, row], out_shape=[shp, shp, shp, shp], compiler_params=_cp("arbitrary", "arbitrary"),
    )(*slots, w, m, v)


def _pair_sum(own, got, name, out_dtype, tb):
    nj, _, r, c = own.shape
    mc = lax.axis_index("c")

    def body(mc_ref, a_ref, b_ref, o_ref):
        del mc_ref
        o_ref[...] = (a_ref[...] + b_ref[...]).astype(out_dtype)

    return pl.pallas_call(
        body, name=name,
        grid_spec=pltpu.PrefetchScalarGridSpec(
            num_scalar_prefetch=1, grid=(nj, r // tb),
            in_specs=[pl.BlockSpec((None, None, tb, c), lambda j, i, mc_ref: (j, mc_ref[0], i, 0)),
                      pl.BlockSpec((None, tb, c), lambda j, i, mc_ref: (j, i, 0))],
            out_specs=pl.BlockSpec((None, tb, c), lambda j, i, mc_ref: (j, i, 0))),
        out_shape=jax.ShapeDtypeStruct((nj, r, c), out_dtype), compiler_params=_cp("parallel", "parallel"),
    )(jnp.reshape(mc, (1,)).astype(jnp.int32), own, got)


def _slot_sum(slots, name):
    ns, r, c = slots.shape

    def body(s_ref, o_ref):
        g = s_ref[0]
        for j in range(1, ns):
            g = g + s_ref[j]
        o_ref[...] = g

    return pl.pallas_call(body, name=name, out_shape=jax.ShapeDtypeStruct((r, c), F32))(slots)


def _position():
    return lax.axis_index("x"), lax.axis_index("y"), lax.axis_index("c")


def _comm(exchange, peers, xs, out_shapes, sems, name, collective_id):
    n = len(xs)
    if collective_id is None:
        def body(*refs):
            exchange(refs[:n], refs[n:n + len(out_shapes)], *refs[n + len(out_shapes):])

        return pl.pallas_call(body, name=name, in_specs=[ANY] * n, out_specs=[ANY] * len(out_shapes),
                              out_shape=out_shapes, scratch_shapes=sems)(*xs)
    def launch(*refs):
        barrier = pltpu.get_barrier_semaphore()
        to = peers(*_position())
        for peer in to:
            pl.semaphore_signal(barrier, inc=1, device_id=peer, device_id_type=MESH)
        pl.semaphore_wait(barrier, len(to))
        exchange(refs[:n], refs[n:n + len(out_shapes)], *refs[n + len(out_shapes):])

    return pl.kernel(launch, out_type=out_shapes, mesh=plsc.ScalarSubcoreMesh(axis_name="seq", num_cores=1), name=name,
                     scratch_types=sems, compiler_params=pltpu.CompilerParams(collective_id=collective_id))(*xs)


def _all_gather(xs, name, collective_id=None):
    n = len(xs)
    return _comm(_gather_body, lambda x, y, c: [(x, y, 1 - c), (1 - x, y, c), (x, 1 - y, c), (1 - x, 1 - y, c)], xs,
                 [jax.ShapeDtypeStruct((N_DEV,) + x.shape, x.dtype) for x in xs],
                 [pltpu.SemaphoreType.DMA((n, 7)), pltpu.SemaphoreType.DMA((n, 7)), pltpu.SemaphoreType.DMA((n,))],
                 name, collective_id)


def _gather_body(x_refs, out_refs, send_sems, recv_sems, local_sems):
    n = len(x_refs)
    mx, my, mc = _position()
    me, sibling = (mx, my, mc), (mx, my, 1 - mc)
    chips = [(1 - mx, my), (mx, 1 - my), (1 - mx, 1 - my)]

    def copy(a, k, block, to, own=False):
        dst = out_refs[a].at[4 * block[0] + 2 * block[1] + block[2]]
        return pltpu.make_async_remote_copy(
            src_ref=x_refs[a] if own else dst, dst_ref=dst,
            send_sem=send_sems.at[a, k], recv_sem=recv_sems.at[a, k], device_id=to, device_id_type=MESH)

    mine = [pltpu.make_async_copy(x_refs[a], out_refs[a].at[4 * mx + 2 * my + mc], local_sems.at[a]) for a in range(n)]
    first = [copy(a, 1 + j, me, (*chip, mc), own=True) for j, chip in enumerate(chips) for a in range(n)]
    first += [copy(a, 0, me, sibling, own=True) for a in range(n)]
    for cp in first + mine:
        cp.start()
    passed = []
    for j, chip in enumerate(chips):
        for a in range(n):
            copy(a, 1 + j, (*chip, mc), me).wait_recv()
            passed.append(copy(a, 4 + j, (*chip, mc), sibling))
            passed[-1].start()
    for a in range(n):
        copy(a, 0, sibling, me).wait_recv()
    for j, chip in enumerate(chips):
        for a in range(n):
            copy(a, 4 + j, (*chip, 1 - mc), me).wait_recv()
    for cp in first + passed:
        cp.wait_send()
    for cp in mine:
        cp.wait()


def _exchange_sibling(gs, name, collective_id=None):
    n = len(gs)

    def exchange(g_refs, r_refs, send_sems, recv_sems):
        mx, my, mc = _position()
        cps = [pltpu.make_async_remote_copy(src_ref=g_refs[a].at[:, 1 - mc], dst_ref=r_refs[a],
                                            send_sem=send_sems.at[a], recv_sem=recv_sems.at[a],
                                            device_id=(mx, my, 1 - mc), device_id_type=MESH) for a in range(n)]
        for cp in cps:
            cp.start()
        for cp in cps:
            cp.wait()

    return _comm(exchange, lambda x, y, c: [(x, y, 1 - c)], gs,
                 [jax.ShapeDtypeStruct(g.shape[:1] + g.shape[2:], g.dtype) for g in gs],
                 [pltpu.SemaphoreType.DMA((n,)), pltpu.SemaphoreType.DMA((n,))], name, collective_id)


def _exchange_chips(ss, name, collective_id=None):
    n = len(ss)

    def exchange(s_refs, r_refs, send_sems, recv_sems, local_sems):
        mx, my, mc = _position()
        my_chip = 2 * mx + my
        chips = [(1 - mx, my), (mx, 1 - my), (1 - mx, 1 - my)]

        def copy(a, k, to_slot):
            px, py = chips[k]
            return pltpu.make_async_remote_copy(
                src_ref=s_refs[a].at[2 * px + py], dst_ref=r_refs[a].at[to_slot], send_sem=send_sems.at[a, k],
                recv_sem=recv_sems.at[a, k], device_id=(px, py, mc), device_id_type=MESH)

        sends = [copy(a, k, my_chip) for k in range(3) for a in range(n)]
        local = [pltpu.make_async_copy(s_refs[a].at[my_chip], r_refs[a].at[my_chip], local_sems.at[a])
                 for a in range(n)]
        for cp in sends + local:
            cp.start()
        for k in range(3):
            px, py = chips[k]
            for a in range(n):
                copy(a, k, 2 * px + py).wait_recv()
        for cp in sends:
            cp.wait_send()
        for cp in local:
            cp.wait()

    return _comm(exchange, lambda x, y, c: [(1 - x, y, c), (x, 1 - y, c), (1 - x, 1 - y, c)], ss,
                 [jax.ShapeDtypeStruct(s.shape, s.dtype) for s in ss],
                 [pltpu.SemaphoreType.DMA((n, 3)), pltpu.SemaphoreType.DMA((n, 3)), pltpu.SemaphoreType.DMA((n,))],
                 name, collective_id)


def _cols_concat(g, name, tb=128):
    _, k_dim, n = g.shape

    def body(g_ref, o_ref):
        o_ref[...] = jnp.concatenate([g_ref[d] for d in range(N_DEV)], axis=1)

    return pl.pallas_call(
        body, name=name, grid=(k_dim // tb,),
        in_specs=[pl.BlockSpec((N_DEV, tb, n), lambda i: (0, i, 0))],
        out_specs=pl.BlockSpec((tb, N_DEV * n), lambda i: (i, 0)),
        out_shape=jax.ShapeDtypeStruct((k_dim, N_DEV * n), g.dtype), compiler_params=_cp("parallel"),
    )(g)


def _cols_split(parts, name, tb=128):
    k_dim = parts[0].shape[0]
    n = sum(p.shape[1] for p in parts) // N_DEV

    def body(*refs):
        full = jnp.concatenate([r[...] for r in refs[:-1]], axis=1)
        for d in range(N_DEV):
            refs[-1][d] = full[:, d * n:(d + 1) * n]

    return pl.pallas_call(
        body, name=name, grid=(k_dim // tb,),
        in_specs=[pl.BlockSpec((tb, p.shape[1]), lambda i: (i, 0)) for p in parts],
        out_specs=pl.BlockSpec((N_DEV, tb, n), lambda i: (0, i, 0)),
        out_shape=jax.ShapeDtypeStruct((N_DEV, k_dim, n), parts[0].dtype), compiler_params=_cp("parallel"),
    )(*parts)


_Q0, _GL0 = 7200, 8224
N_SHARD_IN = N_IN // N_DEV


def _w_in_regions(g, name, tb=128):
    def body(g_ref, ssd_ref, lru_ref, q_ref, gl_ref):
        full = jnp.concatenate([g_ref[d] for d in range(N_DEV)], axis=1)
        lru_ref[...] = full[:, 0:2 * D]
        ssd_ref[:, :S_DT] = full[:, 2 * D:2 * D + S_DT]
        ssd_ref[:, S_DT:] = jnp.zeros((tb, W_SSD - S_DT), g.dtype)
        ssd_ref[:, S_DT:S_DT + DT_REAL] = full[:, 2 * D + S_DT:_Q0]
        q_ref[...] = full[:, _Q0:_GL0]
        gl_ref[...] = full[:, _GL0:N_IN]

    widths = (W_SSD, 2 * D, D, 3 * D)
    return pl.pallas_call(
        body, name=name, grid=(D // tb,),
        in_specs=[pl.BlockSpec((N_DEV, tb, N_SHARD_IN), lambda i: (0, i, 0))],
        out_specs=[pl.BlockSpec((tb, wd), lambda i: (i, 0)) for wd in widths],
        out_shape=[jax.ShapeDtypeStruct((D, wd), g.dtype) for wd in widths], compiler_params=_cp("parallel"),
    )(g)


def _w_in_shards(dssd, dlru, dq, dgl, name, tb=128):
    def body(ssd_ref, lru_ref, q_ref, gl_ref, o_ref):
        full = jnp.concatenate([lru_ref[...], ssd_ref[:, :S_DT + DT_REAL], q_ref[...], gl_ref[...]], axis=1)
        for d in range(N_DEV):
            o_ref[d] = full[:, d * N_SHARD_IN:(d + 1) * N_SHARD_IN]

    return pl.pallas_call(
        body, name=name, grid=(D // tb,),
        in_specs=[pl.BlockSpec((tb, a.shape[1]), lambda i: (i, 0)) for a in (dssd, dlru, dq, dgl)],
        out_specs=pl.BlockSpec((N_DEV, tb, N_SHARD_IN), lambda i: (0, i, 0)),
        out_shape=jax.ShapeDtypeStruct((N_DEV, D, N_SHARD_IN), F32), compiler_params=_cp("parallel"),
    )(dssd, dlru, dq, dgl)


_BIG = (("w_in", "col", (1024, 1412)), ("mem_w_kv", "col", (1024, 256)), ("w_br_lru", "row", (128, 1024)),
        ("w_br_ssd", "row", (256, 1024)), ("w_br_xa", "row", (128, 1024)), ("w_out", "row", (128, 1024)),
        ("ffn_w_in", "row", (704, 1024)), ("ffn_w_down", "row", (352, 1024)))
_TRANSPOSED = ("ffn_w_in",)
_SMALL = (("b_gate", (3, 128)), ("lru_conv_w", (4, 128)), ("ssd_conv_w", (4, 384)))
_REP = (("lru_conv_b", (1024,)), ("lru_w_a", (8, 128, 128)), ("lru_b_a", (1024,)), ("lru_w_i", (8, 128, 128)),
        ("lru_b_i", (1024,)), ("lru_lambda", (1024,)), ("ssd_conv_b", (3072,)), ("ssd_dt_bias", (32,)),
        ("ssd_a_log", (32,)), ("ssd_d", (32,)), ("ssd_norm_w", (2048,)), ("ln1_g", (1024,)), ("ln1_b", (1024,)),
        ("ln2_g", (1024,)), ("ln2_b", (1024,)))
_ORDER = ("w_in", "b_gate", "lru_conv_w", "lru_conv_b", "lru_w_a", "lru_b_a", "lru_w_i", "lru_b_i", "lru_lambda",
          "ssd_conv_w", "ssd_conv_b", "ssd_dt_bias", "ssd_a_log", "ssd_d", "ssd_norm_w", "mem_w_kv", "w_br_lru",
          "w_br_ssd", "w_br_xa", "w_out", "ln1_g", "ln1_b", "ffn_w_in", "ffn_w_down", "ln2_g", "ln2_b")

LANES = 1024
N_SMALL = sum(DEPTH * s[0] * s[1] for _, s in _SMALL)
R_SMALL = 8
N_REP = sum(DEPTH * math.prod(s) for _, s in _REP)
R_REP = 68
R_SM = R_SMALL + R_REP + 4
R_TAIL = R_SMALL + N_DEV * R_REP
TB_TAIL = 184
assert N_SMALL <= R_SMALL * LANES and N_REP + 1 <= N_DEV * R_REP * LANES


def _rows(flat, rows):
    return jnp.pad(flat, (0, rows * LANES - flat.shape[0])).reshape(rows, LANES)


def _rowblk(a, cap):
    return max(b for b in range(16, cap + 1, 16) if a % b == 0)


def _pack_tail(d):
    small = jnp.concatenate([d[n].reshape(-1) for n, _ in _SMALL])
    rep = jnp.concatenate([d[n].reshape(-1) for n, _ in _REP])
    return jnp.concatenate([_rows(small, R_SMALL), _rows(rep, N_DEV * R_REP)], axis=0)


def _unpack_tail(a):
    out, o = {}, 0
    flat = a[:R_SMALL].reshape(-1)
    for n, s in _SMALL:
        k = DEPTH * math.prod(s)
        out[n] = flat[o:o + k].reshape((DEPTH,) + s)
        o += k
    flat, o = a[R_SMALL:].reshape(-1), 0
    for n, s in _REP:
        k = DEPTH * math.prod(s)
        out[n] = flat[o:o + k].reshape((DEPTH,) + s)
        o += k
    return out


def _by_dest(g):
    g = g.reshape(g.shape[:-1] + (N_DEV, g.shape[-1] // N_DEV))
    return jnp.moveaxis(g, -2, 0).reshape(N_DEV, -1)


def _from_stack(st):
    st = jnp.moveaxis(st, 0, -2)
    return st.reshape(st.shape[:-2] + (st.shape[-2] * st.shape[-1],))


def _layer_fwd(x, xb, mem, w, l):
    nm = lambda s: f"{s}_l{l}"
    wi = w["wi"]
    row = lambda v: v.reshape(1, -1)
    s = dict(x=x, xb=xb, wi=wi)
    s["p_ssd"] = _mm(xb, wi["ssd"], name=nm("proj_ssd"))
    s["p_lru"] = _mm(xb, wi["lru"], name=nm("proj_lru"))
    s["p_q"] = _mm(xb, wi["q"], out_dtype=BF16, name=nm("proj_q"))
    s["p_gl"] = _mm(xb, wi["gl"], out_dtype=BF16, name=nm("proj_gl"))
    s["lru_par"] = (w["lru_conv_w"], row(w["lru_conv_b"]), w["lru_w_a"], row(w["lru_b_a"]), w["lru_w_i"],
                    row(w["lru_b_i"]), row(w["lru_lambda"]))
    s["y_lru"], s["h"], s["xc"] = _lru_fwd(s["p_lru"], *s["lru_par"], name=nm("lru_fwd"))
    s["ssd_par"] = (w["ssd_conv_w"], row(w["ssd_conv_b"]), row(w["ssd_dt_bias"]), row(w["ssd_a_log"]),
                    row(w["ssd_d"]), row(w["ssd_norm_w"]))
    s["y_ssd"], s["yraw"], s["hs"], s["c_ssd"] = _ssd_fwd(s["p_ssd"], *s["ssd_par"], name=nm("ssd_fwd"))
    s["kv"] = _mm(mem, w["mem_w_kv"], name=nm("kv"))
    s["y_xa"] = _xa_fwd(s["p_q"], s["kv"], name=nm("xa_fwd"))
    s["b1"] = _mm(s["y_lru"], w["w_br_lru"], out_dtype=BF16, name=nm("br_lru"))
    s["b2"] = _mm(s["y_ssd"], w["w_br_ssd"], out_dtype=BF16, name=nm("br_ssd"))
    s["b3"] = _mm(s["y_xa"], w["w_br_xa"], out_dtype=BF16, name=nm("br_xa"))
    s["bg"] = row(w["b_gate"])
    s["merged"] = _merge_fwd(s["p_gl"], s["bg"], s["b1"], s["b2"], s["b3"], name=nm("merge_fwd"))
    s["mix"] = _mm(s["merged"], w["w_out"], name=nm("out_proj"))
    s["x1"], s["x1b"] = _ln_fwd(x, s["mix"], row(w["ln1_g"]), row(w["ln1_b"]), name=nm("ln1_fwd"))
    s["gate"], s["up"], s["act"] = _ffn_in_swiglu(s["x1b"], w["ffn_w_in"], name=nm("ffn_in"))
    s["f"] = _mm(s["act"], w["ffn_w_down"], name=nm("ffn_down"))
    s["x2"], s["x2b"] = _ln_fwd(s["x1"], s["f"], row(w["ln2_g"]), row(w["ln2_b"]), name=nm("ln2_fwd"))
    return s


def _layer_bwd(s, mem, w, dxo, l, hooks=None):
    nm = lambda t: f"{t}_l{l}"
    g = {}
    hook = lambda stage, t: hooks[stage](t, g) if hooks and stage in hooks else t
    row = lambda v: v.reshape(1, -1)
    slabs = lambda a: a.reshape(N_DEV, a.shape[0] // N_DEV, a.shape[1])
    du2, dg, db = _ln_bwd(s["x1"], s["f"], dxo, row(w["ln2_g"]), name=nm("ln2_bwd"))
    g["ln2_g"], g["ln2_b"] = dg[0], db[0]
    dgate, dup = _d_swiglu(du2, w["ffn_w_down"], s["gate"], s["up"], name=nm("d_swiglu"))
    g["ffn_w_down"] = slabs(_mm(s["act"], du2, ta=True, name=nm("dw_ffn_down")))
    dx1 = _mm(dgate, w["ffn_w_in"][:D_FF], add=du2, add_scale=ALPHA, name=nm("d_x1_gate"))
    dx1 = _mm(dup, w["ffn_w_in"][D_FF:], add=dx1, name=nm("d_x1_up"))
    g["ffn_w_in"] = slabs(jnp.concatenate([_mm(dgate, s["x1b"], ta=True, name=nm("dw_ffn_gate")),
                                           _mm(dup, s["x1b"], ta=True, name=nm("dw_ffn_up"))], axis=0))
    du1, dg, db = _ln_bwd(s["x"], s["mix"], dx1, row(w["ln1_g"]), name=nm("ln1_bwd"))
    g["ln1_g"], g["ln1_b"] = dg[0], db[0]
    dmerged = hook("mid", _mm(du1, w["w_out"], tb=True, name=nm("d_merged")))
    g["w_out"] = slabs(_mm(s["merged"], du1, ta=True, name=nm("dw_out")))
    dp_gl, d1, d2, d3, dbg = _merge_bwd(s["p_gl"], s["bg"], s["b1"], s["b2"], s["b3"], dmerged, name=nm("merge_bwd"))
    g["b_gate"] = dbg.reshape(3, D)
    dy_lru = _mm(d1, w["w_br_lru"], tb=True, name=nm("d_y_lru"))
    g["w_br_lru"] = slabs(_mm(s["y_lru"], d1, ta=True, name=nm("dw_br_lru")))
    dy_ssd = _mm(d2, w["w_br_ssd"], tb=True, name=nm("d_y_ssd"))
    g["w_br_ssd"] = slabs(_mm(s["y_ssd"], d2, ta=True, name=nm("dw_br_ssd")))
    dy_xa = _mm(d3, w["w_br_xa"], tb=True, out_dtype=BF16, name=nm("d_y_xa"))
    g["w_br_xa"] = slabs(_mm(s["y_xa"], d3, ta=True, name=nm("dw_br_xa")))
    dp_q, dkv = _xa_bwd(s["p_q"], s["kv"], dy_xa, name=nm("xa_bwd"))
    g["mem_w_kv"] = _mm(mem, dkv, ta=True, split_n=2 * D // N_DEV, name=nm("dw_kv"))
    dy_ssd = hook("branches", dy_ssd)
    ssd_cw, _, *ssd_rest = s["ssd_par"]
    dp_ssd, dcw, dcb, ddtb, dalog, dd, dnw = _ssd_bwd(s["p_ssd"], s["c_ssd"], s["yraw"], s["hs"], dy_ssd, ssd_cw,
                                                      *ssd_rest, name=nm("ssd_bwd"))
    g["ssd_conv_w"], g["ssd_conv_b"], g["ssd_dt_bias"] = dcw, dcb[0], ddtb[0]
    g["ssd_a_log"], g["ssd_d"], g["ssd_norm_w"] = dalog[0], dd[0], dnw[0]
    dp_ssd = hook("ssd", dp_ssd)
    lru_cw, _, *lru_rest = s["lru_par"]
    dp_lru, dcw, dcb, dwa, dba, dwi, dbi, dlam = _lru_bwd(s["p_lru"], s["xc"], s["h"], dy_lru, lru_cw, *lru_rest,
                                                          name=nm("lru_bwd"))
    g["lru_conv_w"], g["lru_conv_b"], g["lru_w_a"], g["lru_b_a"] = dcw, dcb[0], dwa, dba[0]
    g["lru_w_i"], g["lru_b_i"], g["lru_lambda"] = dwi, dbi[0], dlam[0]
    wi, x = s["wi"], s["xb"]
    g["w_in"] = _w_in_shards(_mm(x, dp_ssd, ta=True, name=nm("dw_in_ssd")), _mm(x, dp_lru, ta=True, name=nm("dw_in_lru")),
                             _mm(x, dp_q, ta=True, name=nm("dw_in_q")), _mm(x, dp_gl, ta=True, name=nm("dw_in_gl")),
                             name=nm("dw_in_shards"))
    dp_ssd = hook("weights", dp_ssd)
    dx = _mm(dp_ssd, wi["ssd"], tb=True, add=du1, add_scale=ALPHA, name=nm("dx_ssd"))
    dx = hook("dx", _mm(dp_lru, wi["lru"], tb=True, add=dx, name=nm("dx_lru")))
    dx = _mm(dp_q, wi["q"], tb=True, add=dx, name=nm("dx_q"))
    dx = _mm(dp_gl, wi["gl"], tb=True, add=dx, name=nm("dx_gl"))
    return dx, g


def _local_step(x, mem, target, layers, hooks=None):
    saved, xb = [], x.astype(BF16)
    for l in range(DEPTH):
        saved.append(_layer_fwd(x, xb, mem, layers[l], l))
        x, xb = saved[-1]["x2"], saved[-1]["x2b"]
    dx, loss = _loss_head(x, target, name="loss_head")
    if hooks and "loss" in hooks[-1]:
        loss = hooks[-1]["loss"](loss, None)
    grads = [None] * DEPTH
    for l in reversed(range(DEPTH)):
        dx, grads[l] = _layer_bwd(saved[l], mem, layers[l], dx, l, hooks[l] if hooks else None)
    return loss, dx, grads


def kernel(x, mem, w_in, b_gate, lru_conv_w, lru_conv_b, lru_w_a, lru_b_a, lru_w_i, lru_b_i, lru_lambda, ssd_conv_w, ssd_conv_b, ssd_dt_bias, ssd_a_log, ssd_d, ssd_norm_w, mem_w_kv, w_br_lru, w_br_ssd, w_br_xa, w_out, ln1_g, ln1_b, ffn_w_in, ffn_w_down, ln2_g, ln2_b, loss_target, m_w_in, m_b_gate, m_lru_conv_w, m_lru_conv_b, m_lru_w_a, m_lru_b_a, m_lru_w_i, m_lru_b_i, m_lru_lambda, m_ssd_conv_w, m_ssd_conv_b, m_ssd_dt_bias, m_ssd_a_log, m_ssd_d, m_ssd_norm_w, m_mem_w_kv, m_w_br_lru, m_w_br_ssd, m_w_br_xa, m_w_out, m_ln1_g, m_ln1_b, m_ffn_w_in, m_ffn_w_down, m_ln2_g, m_ln2_b, v_w_in, v_b_gate, v_lru_conv_w, v_lru_conv_b, v_lru_w_a, v_lru_b_a, v_lru_w_i, v_lru_b_i, v_lru_lambda, v_ssd_conv_w, v_ssd_conv_b, v_ssd_dt_bias, v_ssd_a_log, v_ssd_d, v_ssd_norm_w, v_mem_w_kv, v_w_br_lru, v_w_br_ssd, v_w_br_xa, v_w_out, v_ln1_g, v_ln1_b, v_ffn_w_in, v_ffn_w_down, v_ln2_g, v_ln2_b):
    local = dict(locals())
    w = {n: local[n] for n in _ORDER}
    m = {n: local["m_" + n] for n in _ORDER}
    v = {n: local["v_" + n] for n in _ORDER}
    for n in _TRANSPOSED:
        w[n], m[n], v[n] = (jnp.swapaxes(a, 1, 2) for a in (w[n], m[n], v[n]))

    big = [n for n, _, _ in _BIG]
    kinds = {n: kind for n, kind, _ in _BIG}

    small = _rows(jnp.concatenate([w[n].reshape(-1) for n, _ in _SMALL]), R_SMALL)
    first = _all_gather([w["w_in"][0].astype(BF16), small], name="gather_w_in_l0")
    rest, later, _ = lax.optimization_barrier(([w[n][0].astype(BF16) for n in big[1:]],
                                               [w[n][1].astype(BF16) for n in big], first[-1]))
    rest = _all_gather(rest, "gather_weights_l0", collective_id=1)
    later = _all_gather(later, "gather_weights_l1", collective_id=4)
    stacks = [dict(zip(big, [first[0], *rest])), dict(zip(big, later))]
    small_all, o, small_full = first[-1].reshape(N_DEV, R_SMALL * LANES), 0, {}
    for n, s in _SMALL:
        k = DEPTH * s[0] * s[1]
        small_full[n] = _from_stack(small_all[:, o:o + k].reshape((N_DEV, DEPTH) + s))
        o += k
    layers = []
    for l in range(DEPTH):
        lw = {n: w[n][l] for n, _ in _REP}
        lw.update({n: small_full[n][l] for n, _ in _SMALL})
        lw["wi"] = dict(zip(("ssd", "lru", "q", "gl"), _w_in_regions(stacks[l]["w_in"], name=f"w_in_regions_l{l}")))
        for n in big[1:]:
            if kinds[n] == "col":
                lw[n] = _cols_concat(stacks[l][n], name=f"full_{n}_l{l}")
            else:
                lw[n] = stacks[l][n].reshape(-1, stacks[l][n].shape[-1])
        layers.append(lw)

    by_dest = lambda a: a.reshape((4, 2) + a.shape[1:])
    slots, pending, last_layer = {}, {}, {}
    queue = [stacks[1]["w_out"]]

    def after_last(operands):
        operands, _ = lax.optimization_barrier((list(operands), queue[-1]))
        return operands

    def start(tag, collective_id, names_and_grads):
        names, owns = zip(*names_and_grads)
        gots = _exchange_sibling(after_last(owns), name=f"reduce_cores_{tag}", collective_id=collective_id)
        queue.append(gots[0])
        pending[tag] = (names, owns, gots)

    def finish(tag, collective_id, t):
        names, owns, gots = pending.pop(tag)
        t, gots = lax.optimization_barrier((t, gots))
        sums = [_pair_sum(own, got, name=f"pair_sum_{tag}_{n}", out_dtype=F32 if n == "tail" else BF16,
                          tb=R_SM if n == "tail" else _rowblk(own.shape[2], 256))
                for n, own, got in zip(names, owns, gots)]
        t, sums = lax.optimization_barrier((t, sums))
        got = _exchange_chips(sums, name=f"reduce_chips_{tag}", collective_id=collective_id)
        queue.append(got[0])
        slots.update({(tag, n): s for n, s in zip(names, got)})
        return t

    def tail_of(g0):
        stacked = {n: jnp.stack([g0[n], last_layer[n]]) for n in [s[0] for s in _SMALL + _REP]}
        sm = jnp.concatenate([_by_dest(stacked[n]) for n, _ in _SMALL], axis=1)
        sm = jnp.pad(sm, ((0, 0), (0, R_SMALL * LANES - sm.shape[1])))
        rep = jnp.concatenate([stacked[n].reshape(-1) for n, _ in _REP] + [last_layer["loss"][0, :1]])
        rep = jnp.pad(rep, (0, N_DEV * R_REP * LANES - rep.shape[0])).reshape(N_DEV, R_REP * LANES)
        tail = jnp.concatenate([sm, rep, jnp.zeros((N_DEV, (R_SM - R_SMALL - R_REP) * LANES), F32)], axis=1)
        return tail.reshape(4, 2, R_SM, LANES)

    def weights_l1(t, g):
        last_layer.update(g)
        start("l1", 2, [(n, by_dest(g[n])) for n in big])
        return t

    def branches_l0(t, g):
        start("l0a", 5, [(n, by_dest(g[n])) for n in big[1:]])
        return t

    def weights_l0(t, g):
        start("l0b", 7, [("w_in", by_dest(g["w_in"])), ("tail", tail_of(g))])
        return t

    hooks = [{"branches": branches_l0, "ssd": lambda t, g: finish("l0a", 6, t), "weights": weights_l0,
              "dx": lambda t, g: finish("l0b", 8, t)},
             {"weights": weights_l1, "dx": lambda t, g: finish("l1", 3, t),
              "loss": lambda t, g: last_layer.setdefault("loss", t)}]
    _, dx, grads = _local_step(x[0], mem[0], loss_target[0], layers, hooks)

    res = {}
    for n in big:
        tb = _rowblk(w[n].shape[1], 128 if w[n].shape[2] > LANES else 256)
        res[n] = _adamw([slots["l0b" if n == "w_in" else "l0a", n], slots["l1", n]], w[n], m[n], v[n],
                        name=f"adamw_{n}", tb=tb)
    tail_sum = _slot_sum(slots["l0b", "tail"], name="sum_tail")
    rep_all = _all_gather([tail_sum[R_SMALL:R_SMALL + R_REP]], name="gather_replicated")[0]
    g_tail = jnp.concatenate([tail_sum[:R_SMALL], rep_all.reshape(N_DEV * R_REP, LANES)], axis=0)
    loss = rep_all.reshape(-1)[N_REP]
    tails = _adamw([g_tail[None]], _pack_tail(w)[None], _pack_tail(m)[None], _pack_tail(v)[None],
                   name="adamw_tail", tb=TB_TAIL)

    outs = []
    for kind in range(4):
        d = {**{n: res[n][kind] for n in big}, **_unpack_tail(tails[kind][0])}
        d.update({n: jnp.swapaxes(d[n], 1, 2) for n in _TRANSPOSED})
        outs += [d[n] for n in _ORDER]
    return (loss, dx[None], *outs)
```
